```python
import jax, jax.numpy as jnp
from jax import lax
import numpy as np

D_MODEL = 1024
BATCH = 2
SEQ = 8192
DEPTH = 2

GLA_HEADS = 4
GLA_DK = 64
GLA_DV = 128
GLA_GATE_RANK = 16
GLA_TAU = 16.0
GLA_CHUNK = 64
NSA_HEADS = 8
NSA_KV_GROUPS = 2
NSA_HPG = NSA_HEADS // NSA_KV_GROUPS
NSA_DH = 64
CMP_LEN = 32
CMP_STRIDE = 16
CMP_HIDDEN = 256
SEL_BLOCK = 64
SEL_TOPK = 16
WINDOW = 512
Q_BLOCK = 128
ROPE_THETA = 500000.0
ROT_DIM = NSA_DH // 4
SGU_CHUNK = 128
SGU_GROUPS = 8
SGU_WIDTH = 2048
SGU_GROUP_DIM = SGU_WIDTH // SGU_GROUPS
N_EXPERTS = 16
N_EXPERT_GROUPS = 4
EXPERTS_PER_GROUP = N_EXPERTS // N_EXPERT_GROUPS
MOE_TOPK = 2
EXPERT_HIDDEN = 512

GLA_QK_W = GLA_HEADS * GLA_DK
GLA_V_W = GLA_HEADS * GLA_DV
NSA_Q_W = NSA_HEADS * NSA_DH
NSA_KV_W = NSA_KV_GROUPS * NSA_DH
AB_SPLITS = (GLA_QK_W, GLA_QK_W, GLA_V_W, GLA_GATE_RANK, GLA_V_W,
             NSA_Q_W, NSA_KV_W, NSA_KV_W, NSA_KV_W, NSA_KV_W, NSA_KV_W, NSA_KV_W,
             NSA_HEADS * 3)
AB_IN_WIDTH = sum(AB_SPLITS)
AB_MIX_WIDTH = GLA_V_W + NSA_Q_W

NORM_EPS = 1e-6
NEG_INF = -1e30
FORCE_BONUS = 1e4

kernel_name = "hybrid_gla_nsa_gmlp_moe_trunk"


def rms_norm(x, g):
    xf = x.astype(jnp.float32)
    y = xf * lax.rsqrt(jnp.mean(xf * xf, axis=-1, keepdims=True) + NORM_EPS)
    return (y * g.astype(jnp.float32)).astype(x.dtype)


def modulate(h, shift, scale):
    return h * (1 + scale[:, None, :]) + shift[:, None, :]


def rope(x, pos):
    half = ROT_DIM // 2
    inv_freq = jnp.float32(ROPE_THETA) ** (-jnp.arange(half, dtype=jnp.float32) / half)
    ang = pos.astype(jnp.float32)[..., None] * inv_freq
    cos = jnp.cos(ang)[..., None, :]
    sin = jnp.sin(ang)[..., None, :]
    xr = x[..., :ROT_DIM].astype(jnp.float32)
    x1, x2 = xr[..., :half], xr[..., half:]
    rot = jnp.concatenate([x1 * cos - x2 * sin, x2 * cos + x1 * sin], axis=-1).astype(x.dtype)
    return jnp.concatenate([rot, x[..., ROT_DIM:]], axis=-1)


def masked_softmax(s, mask):
    s = jnp.where(mask, s.astype(jnp.float32), NEG_INF)
    return jax.nn.softmax(s, axis=-1) * mask


def gla_mixer(q, k, v, g_lr, r, w_gate2, b_gate, out_g):
    B, S, _ = q.shape
    n = S // GLA_CHUNK
    f32 = jnp.float32
    shp = (B, n, GLA_CHUNK, GLA_HEADS, -1)
    qf = q.astype(f32).reshape(shp) * (GLA_DK ** -0.5)
    kf = k.astype(f32).reshape(shp)
    vf = v.astype(f32).reshape(shp)
    log_a = jax.nn.log_sigmoid((g_lr @ w_gate2 + b_gate).astype(f32)) / GLA_TAU
    b = jnp.cumsum(log_a.reshape(shp), axis=2)
    b_last = b[:, :, -1:]
    b_mid = b[:, :, GLA_CHUNK // 2:GLA_CHUNK // 2 + 1]
    scores = jnp.einsum("bnihd,bnjhd->bnhij", qf * jnp.exp(b - b_mid), kf * jnp.exp(b_mid - b))
    causal = jnp.tril(jnp.ones((GLA_CHUNK, GLA_CHUNK), dtype=bool))
    scores = jnp.where(causal, scores, 0.0)
    o_intra = jnp.einsum("bnhij,bnjhe->bnihe", scores, vf)
    kv = jnp.einsum("bnchd,bnche->bnhde", kf * jnp.exp(b_last - b), vf)
    decay = jnp.exp(b_last[:, :, 0])

    def step(state, inp):
        dec, kv_n = inp
        return dec[..., None] * state + kv_n, state

    init = jnp.zeros((B, GLA_HEADS, GLA_DK, GLA_DV), f32)
    _, states = lax.scan(step, init, (jnp.moveaxis(decay, 1, 0), jnp.moveaxis(kv, 1, 0)))
    states = jnp.moveaxis(states, 0, 1)
    o_inter = jnp.einsum("bnchd,bnhde->bnche", qf * jnp.exp(b), states)
    o = (o_intra + o_inter).reshape(B, S, GLA_HEADS, GLA_DV)
    o = rms_norm(o, out_g) * jax.nn.silu(r.astype(f32)).reshape(B, S, GLA_HEADS, GLA_DV)
    return o.reshape(B, S, GLA_V_W)


def nsa_mixer(q, kc, vc, ks, vs, kw, vw, g_logits, positions,
              q_gain, k_gain, cmp_pe, cmp_w1, cmp_w2):
    B, S, _ = q.shape
    G, HPG, DH = NSA_KV_GROUPS, NSA_HPG, NSA_DH
    scale = DH ** -0.5
    heads_kv = lambda t: t.reshape(B, S, G, DH)
    qh = rope(rms_norm(q.reshape(B, S, NSA_HEADS, DH), q_gain), positions)
    k_win = rope(rms_norm(heads_kv(kw), k_gain[2]), positions)
    v_win = heads_kv(vw)
    k_sel = rope(rms_norm(heads_kv(ks), k_gain[1]), positions)
    v_sel = heads_kv(vs)
    n_cmp = (S - CMP_LEN) // CMP_STRIDE + 1
    starts = jnp.arange(n_cmp) * CMP_STRIDE
    tok = starts[:, None] + jnp.arange(CMP_LEN)[None, :]

    def compress(t, pe, w1, w2):
        blk = heads_kv(t)[:, tok] + pe[None, None, :, None, :]
        blk = jnp.moveaxis(blk, 3, 2).reshape(B, n_cmp, G, CMP_LEN * DH)
        return jax.nn.gelu(blk @ w1) @ w2

    k_cmp = compress(kc, cmp_pe[0], cmp_w1[0], cmp_w2[0])
    k_cmp = rope(rms_norm(k_cmp, k_gain[0]), positions[:, tok[:, -1]])
    v_cmp = compress(vc, cmp_pe[1], cmp_w1[1], cmp_w2[1])

    q_g = qh.reshape(B, S, G, HPG, DH).transpose(0, 2, 3, 1, 4)
    gm = lambda t: t.transpose(0, 2, 1, 3)
    kcg, vcg = gm(k_cmp), gm(v_cmp)
    n_sel = S // SEL_BLOCK
    topk = min(SEL_TOPK, n_sel)
    ksb = gm(k_sel).reshape(B, G, n_sel, SEL_BLOCK, DH)
    vsb = gm(v_sel).reshape(B, G, n_sel, SEL_BLOCK, DH)
    pad = ((0, 0), (0, 0), (WINDOW, 0), (0, 0))
    kwp = jnp.pad(gm(k_win), pad)
    vwp = jnp.pad(gm(v_win), pad)
    gates = jax.nn.sigmoid(g_logits.astype(jnp.float32)).reshape(B, S, G, HPG, 3).transpose(0, 2, 3, 1, 4)
    ci = starts[:, None]
    sj = jnp.arange(n_sel)[None, :] * SEL_BLOCK
    cover = ((ci < sj + SEL_BLOCK) & (ci + CMP_LEN > sj)).astype(jnp.float32)
    cmp_end = starts + CMP_LEN - 1
    jj = jnp.arange(n_sel)
    bi = jnp.arange(B)[:, None, None, None]
    gi = jnp.arange(G)[None, :, None, None]

    def block(i):
        qs = i * Q_BLOCK
        qb = lax.dynamic_slice_in_dim(q_g, qs, Q_BLOCK, axis=3)
        t = qs + jnp.arange(Q_BLOCK)
        s_c = jnp.einsum("bghqd,bgkd->bghqk", qb, kcg).astype(jnp.float32) * scale
        p_c = masked_softmax(s_c, cmp_end[None, :] <= t[:, None])
        o_c = jnp.einsum("bghqk,bgkd->bghqd", p_c, vcg)
        imp = jnp.einsum("bghqk,kj->bgqj", p_c, cover)
        cur = t // SEL_BLOCK
        forced = (jj[None] == 0) | (jj[None] == cur[:, None]) | (jj[None] == cur[:, None] - 1)
        valid = jj[None] * SEL_BLOCK <= t[:, None]
        sel_score = jnp.where(valid, imp + jnp.where(forced, FORCE_BONUS, 0.0), NEG_INF)
        _, idx = lax.top_k(sel_score, topk)
        kb = ksb[bi, gi, idx]
        vb = vsb[bi, gi, idx]
        kpos = idx[..., None] * SEL_BLOCK + jnp.arange(SEL_BLOCK)
        smask = (kpos <= t[:, None, None]).reshape(B, G, 1, Q_BLOCK, topk * SEL_BLOCK)
        s_s = jnp.einsum("bghqd,bgqnkd->bghqnk", qb, kb).astype(jnp.float32) * scale
        p_s = masked_softmax(s_s.reshape(B, G, HPG, Q_BLOCK, topk * SEL_BLOCK), smask)
        o_s = jnp.einsum("bghqm,bgqmd->bghqd", p_s,
                         vb.reshape(B, G, Q_BLOCK, topk * SEL_BLOCK, DH))
        kwb = lax.dynamic_slice_in_dim(kwp, qs, WINDOW + Q_BLOCK, axis=2)
        vwb = lax.dynamic_slice_in_dim(vwp, qs, WINDOW + Q_BLOCK, axis=2)
        kpw = qs - WINDOW + jnp.arange(WINDOW + Q_BLOCK)
        diff = t[:, None] - kpw[None, :]
        wmask = (diff >= 0) & (diff < WINDOW) & (kpw[None, :] >= 0)
        s_w = jnp.einsum("bghqd,bgkd->bghqk", qb, kwb).astype(jnp.float32) * scale
        p_w = masked_softmax(s_w, wmask)
        o_w = jnp.einsum("bghqk,bgkd->bghqd", p_w, vwb)
        gb = lax.dynamic_slice_in_dim(gates, qs, Q_BLOCK, axis=3)
        return gb[..., 0:1] * o_c + gb[..., 1:2] * o_s + gb[..., 2:3] * o_w

    outs = lax.map(block, jnp.arange(S // Q_BLOCK))
    return outs.transpose(1, 0, 4, 2, 3, 5).reshape(B, S, NSA_Q_W)


def mixer_ab(h, positions, w_in, w_out, gla_w_gate2, gla_b_gate, gla_norm_g,
             q_gain, k_gain, cmp_pe, cmp_w1, cmp_w2):
    offs = np.cumsum(AB_SPLITS)[:-1].tolist()
    (gq, gk, gv, g_lr, g_r, nq, kc, vc, ks, vs, kw, vw, ng) = jnp.split(h @ w_in, offs, axis=-1)
    o_a = gla_mixer(gq, gk, gv, g_lr, g_r, gla_w_gate2, gla_b_gate, gla_norm_g)
    o_b = nsa_mixer(nq, kc, vc, ks, vs, kw, vw, ng, positions, q_gain, k_gain, cmp_pe, cmp_w1, cmp_w2)
    return jnp.concatenate([o_a.astype(h.dtype), o_b.astype(h.dtype)], axis=-1) @ w_out


def mixer_c(h, w_in, norm_g, w_s, b_s, w_out):
    B, S, _ = h.shape
    z = jax.nn.gelu(h @ w_in)
    u, v = jnp.split(z, 2, axis=-1)
    v = rms_norm(v, norm_g)
    n = S // SGU_CHUNK
    v = v.reshape(B, n, SGU_CHUNK, SGU_GROUPS, SGU_GROUP_DIM)
    w = w_s * jnp.tril(jnp.ones((SGU_CHUNK, SGU_CHUNK), dtype=w_s.dtype))
    mix = jnp.einsum("gts,bnsgc->bntgc", w, v) + b_s.T[None, None, :, :, None]
    return (u * mix.reshape(B, S, SGU_WIDTH)) @ w_out


def moe(h, w_router, router_bias, w_gate, w_up, w_down):
    B, S, D = h.shape
    ht = h.reshape(B * S, D)
    scores = jax.nn.sigmoid((ht @ w_router).astype(jnp.float32))
    sel = scores + router_bias.astype(jnp.float32)
    grp = sel.reshape(-1, N_EXPERT_GROUPS, EXPERTS_PER_GROUP)
    group_score = lax.top_k(grp, MOE_TOPK)[0].sum(-1)
    g_idx = jnp.argmax(group_score, axis=-1)
    in_grp = (jnp.arange(N_EXPERTS) // EXPERTS_PER_GROUP)[None, :] == g_idx[:, None]
    _, e_idx = lax.top_k(jnp.where(in_grp, sel, NEG_INF), MOE_TOPK)
    wts = jnp.take_along_axis(scores, e_idx, axis=-1)
    wts = wts / jnp.sum(wts, axis=-1, keepdims=True)
    combine = jnp.sum(jax.nn.one_hot(e_idx, N_EXPERTS, dtype=jnp.float32) * wts[..., None], axis=1)
    out = jnp.zeros((B * S, D), jnp.float32)
    for e in range(N_EXPERTS):
        hid = jax.nn.silu(ht @ w_gate[e]) * (ht @ w_up[e])
        out = out + combine[:, e:e + 1] * (hid @ w_down[e])
    return out.reshape(B, S, D).astype(h.dtype)


def setup_inputs(seed: int = 0) -> dict:
    key = jax.random.key(seed)
    ks = jax.random.split(key, 32)
    f32 = jnp.float32
    n_even = (DEPTH + 1) // 2
    n_odd = DEPTH // 2
    nrm = lambda k, shape, s: jax.random.normal(k, shape, f32) * s
    x = nrm(ks[0], (BATCH, SEQ, D_MODEL), 1.0)
    c = nrm(ks[1], (BATCH, D_MODEL), 1.0)
    positions = (jax.random.randint(ks[2], (BATCH, 1), 0, 4096)
                 + jnp.arange(SEQ, dtype=jnp.int32)[None, :]).astype(jnp.int32)
    return {
        "x": x,
        "c": c,
        "positions": positions,
        "w_ada": nrm(ks[3], (DEPTH, D_MODEL, 6 * D_MODEL), 0.5 * D_MODEL ** -0.5),
        "b_ada": nrm(ks[4], (DEPTH, 6 * D_MODEL), 0.02),
        "norm_g": 1.0 + nrm(ks[5], (DEPTH, 2, D_MODEL), 0.01),
        "w_in_ab": nrm(ks[6], (n_even, D_MODEL, AB_IN_WIDTH), D_MODEL ** -0.5),
        "w_out_ab": nrm(ks[7], (n_even, AB_MIX_WIDTH, D_MODEL), AB_MIX_WIDTH ** -0.5),
        "gla_w_gate2": nrm(ks[8], (n_even, GLA_GATE_RANK, GLA_QK_W), GLA_GATE_RANK ** -0.5),
        "gla_b_gate": nrm(ks[9], (n_even, GLA_QK_W), 0.1),
        "gla_norm_g": 1.0 + nrm(ks[10], (n_even, GLA_DV), 0.01),
        "nsa_q_gain": 1.0 + nrm(ks[11], (n_even, NSA_DH), 0.01),
        "nsa_k_gain": 1.0 + nrm(ks[12], (n_even, 3, NSA_DH), 0.01),
        "nsa_cmp_pe": nrm(ks[13], (n_even, 2, CMP_LEN, NSA_DH), 0.1),
        "nsa_cmp_w1": nrm(ks[14], (n_even, 2, CMP_LEN * NSA_DH, CMP_HIDDEN), (CMP_LEN * NSA_DH) ** -0.5),
        "nsa_cmp_w2": nrm(ks[15], (n_even, 2, CMP_HIDDEN, NSA_DH), CMP_HIDDEN ** -0.5),
        "w_in_c": nrm(ks[16], (n_odd, D_MODEL, 2 * SGU_WIDTH), D_MODEL ** -0.5),
        "sgu_norm_g": 1.0 + nrm(ks[17], (n_odd, SGU_WIDTH), 0.01),
        "sgu_w_s": nrm(ks[18], (n_odd, SGU_GROUPS, SGU_CHUNK, SGU_CHUNK), SGU_CHUNK ** -0.5),
        "sgu_b_s": 1.0 + nrm(ks[19], (n_odd, SGU_GROUPS, SGU_CHUNK), 0.02),
        "w_out_c": nrm(ks[20], (n_odd, SGU_WIDTH, D_MODEL), SGU_WIDTH ** -0.5),
        "w_router": nrm(ks[21], (D_MODEL, N_EXPERTS), D_MODEL ** -0.5),
        "router_bias": nrm(ks[22], (N_EXPERTS,), 0.01),
        "w_gate": nrm(ks[23], (DEPTH, N_EXPERTS, D_MODEL, EXPERT_HIDDEN), D_MODEL ** -0.5),
        "w_up": nrm(ks[24], (DEPTH, N_EXPERTS, D_MODEL, EXPERT_HIDDEN), D_MODEL ** -0.5),
        "w_down": nrm(ks[25], (DEPTH, N_EXPERTS, EXPERT_HIDDEN, D_MODEL), EXPERT_HIDDEN ** -0.5),
    }


def reference(x, c, positions, w_ada, b_ada, norm_g, w_in_ab, w_out_ab, gla_w_gate2,
              gla_b_gate, gla_norm_g, nsa_q_gain, nsa_k_gain, nsa_cmp_pe, nsa_cmp_w1,
              nsa_cmp_w2, w_in_c, sgu_norm_g, sgu_w_s, sgu_b_s, w_out_c, w_router,
              router_bias, w_gate, w_up, w_down):
    cond = jax.nn.silu(c)
    for layer in range(DEPTH):
        mod = cond @ w_ada[layer] + b_ada[layer]
        shift_t, scale_t, gate_t, shift_c, scale_c, gate_c = jnp.split(mod, 6, axis=-1)
        h = modulate(rms_norm(x, norm_g[layer, 0]), shift_t, scale_t)
        i = layer // 2
        if layer % 2 == 0:
            y = mixer_ab(h, positions, w_in_ab[i], w_out_ab[i], gla_w_gate2[i], gla_b_gate[i],
                         gla_norm_g[i], nsa_q_gain[i], nsa_k_gain[i], nsa_cmp_pe[i],
                         nsa_cmp_w1[i], nsa_cmp_w2[i])
        else:
            y = mixer_c(h, w_in_c[i], sgu_norm_g[i], sgu_w_s[i], sgu_b_s[i], w_out_c[i])
        x = x + gate_t[:, None, :] * y.astype(x.dtype)
        h = modulate(rms_norm(x, norm_g[layer, 1]), shift_c, scale_c)
        x = x + gate_c[:, None, :] * moe(h, w_router, router_bias, w_gate[layer], w_up[layer], w_down[layer])
    return x
```

```python
import functools

import numpy as np
import jax
import jax.numpy as jnp
from jax import lax
from jax.experimental import pallas as pl
from jax.experimental.pallas import tpu as pltpu

D_MODEL = 1024
BATCH = 2
SEQ = 8192
DEPTH = 2
N_TOK = BATCH * SEQ

GLA_HEADS = 4
GLA_DK = 64
GLA_DV = 128
GLA_GATE_RANK = 16
GLA_TAU = 16.0
GLA_CHUNK = 64
NSA_HEADS = 8
NSA_KV_GROUPS = 2
NSA_HPG = NSA_HEADS // NSA_KV_GROUPS
NSA_DH = 64
CMP_LEN = 32
CMP_STRIDE = 16
CMP_HIDDEN = 256
SEL_BLOCK = 64
SEL_TOPK = 16
WINDOW = 512
ROPE_THETA = 500000.0
ROT_DIM = NSA_DH // 4
ROT_HALF = ROT_DIM // 2
SGU_CHUNK = 128
SGU_GROUPS = 8
SGU_WIDTH = 2048
SGU_GROUP_DIM = SGU_WIDTH // SGU_GROUPS
N_EXPERTS = 16
N_EXPERT_GROUPS = 4
EXPERTS_PER_GROUP = N_EXPERTS // N_EXPERT_GROUPS
MOE_TOPK = 2
EXPERT_HIDDEN = 512

GLA_QK_W = GLA_HEADS * GLA_DK
GLA_V_W = GLA_HEADS * GLA_DV
NSA_Q_W = NSA_HEADS * NSA_DH
NSA_KV_W = NSA_KV_GROUPS * NSA_DH
N_CMP = (SEQ - CMP_LEN) // CMP_STRIDE + 1
N_CMP_PAD = SEQ // CMP_STRIDE
N_SEL = SEQ // SEL_BLOCK

NORM_EPS = 1e-6
NEG_INF = -1e30
FORCE_BONUS = 1e4

LANES = 128
MIB = 1024 * 1024

F32 = jnp.float32
BF16 = jnp.bfloat16
HI = lax.Precision.HIGHEST
NT_DIMS = (((1,), (1,)), ((), ()))
TN_DIMS = (((0,), (0,)), ((), ()))


def _cparams(sem, vmem_mib=48):
    return pltpu.CompilerParams(dimension_semantics=sem, vmem_limit_bytes=vmem_mib * MIB)


def _rms_mod(x, g, shift, scale):
    y = x * lax.rsqrt(jnp.mean(x * x, axis=-1, keepdims=True) + NORM_EPS) * g
    return y * (1 + scale) + shift


def _silu(x):
    return x * jax.nn.sigmoid(x)


def _log_sigmoid(z):
    return jnp.minimum(z, 0.0) - jnp.log1p(jnp.exp(-jnp.abs(z)))


ADA_TN = 1536
ADA_ROWS = 8


def _ada_kernel(c_ref, w_ref, b_ref, o_ref):
    cond = _silu(c_ref[...])
    o_ref[0] = jnp.dot(cond, w_ref[0], precision=HI, preferred_element_type=F32) + b_ref[0]


def ada_modulation(c, w_ada, b_ada):
    c8 = jnp.zeros((ADA_ROWS, D_MODEL), F32).at[:BATCH].set(c)
    width = 6 * D_MODEL
    out = pl.pallas_call(
        _ada_kernel,
        grid=(DEPTH, width // ADA_TN),
        in_specs=[
            pl.BlockSpec((ADA_ROWS, D_MODEL), lambda l, j: (0, 0)),
            pl.BlockSpec((1, D_MODEL, ADA_TN), lambda l, j: (l, 0, j)),
            pl.BlockSpec((1, 1, ADA_TN), lambda l, j: (l, 0, j)),
        ],
        out_specs=pl.BlockSpec((1, ADA_ROWS, ADA_TN), lambda l, j: (l, 0, j)),
        out_shape=jax.ShapeDtypeStruct((DEPTH, ADA_ROWS, width), F32),
        compiler_params=_cparams(("arbitrary", "arbitrary")),
        name="ada_modulation",
    )(c8, w_ada, b_ada.reshape(DEPTH, 1, width))
    return out[:, :BATCH].reshape(DEPTH, BATCH, 6, D_MODEL)


INPROJ_TM = 512
INPROJ_WIDTHS = (2 * GLA_QK_W, GLA_V_W, GLA_V_W, NSA_Q_W, 6 * NSA_KV_W, LANES)


def _arrange_w_in(w_in):
    o = np.cumsum((0, GLA_QK_W, GLA_QK_W, GLA_V_W, GLA_GATE_RANK, GLA_V_W, NSA_Q_W, 6 * NSA_KV_W, NSA_HEADS * 3))
    gq_gk = w_in[:, o[0]:o[2]]
    gv = w_in[:, o[2]:o[3]]
    glr = w_in[:, o[3]:o[4]]
    gr = w_in[:, o[4]:o[5]]
    nq = w_in[:, o[5]:o[6]]
    nkv = w_in[:, o[6]:o[7]]
    ng = w_in[:, o[7]:o[8]]
    pad = jnp.zeros((D_MODEL, LANES - GLA_GATE_RANK - NSA_HEADS * 3), w_in.dtype)
    return jnp.concatenate([gq_gk, gv, gr, nq, nkv, glr, ng, pad], axis=1).astype(BF16)


def _inproj0_kernel(x_ref, g_ref, sh_ref, sc_ref, w_ref, *o_refs):
    h = _rms_mod(x_ref[0], g_ref[...], sh_ref[0], sc_ref[0]).astype(BF16)
    off = 0
    for o_ref, wd in zip(o_refs, INPROJ_WIDTHS):
        o_ref[0] = jnp.dot(h, w_ref[:, off:off + wd], preferred_element_type=F32)
        off += wd


def inproj0(x, g, shift, scale, w_arranged):
    tm = INPROJ_TM
    wtot = sum(INPROJ_WIDTHS)
    row = lambda b, i: (b, i, 0)
    vec = lambda b, i: (b, 0, 0)
    return pl.pallas_call(
        _inproj0_kernel,
        grid=(BATCH, SEQ // tm),
        in_specs=[
            pl.BlockSpec((1, tm, D_MODEL), row),
            pl.BlockSpec((1, D_MODEL), lambda b, i: (0, 0)),
            pl.BlockSpec((1, 1, D_MODEL), vec),
            pl.BlockSpec((1, 1, D_MODEL), vec),
            pl.BlockSpec((D_MODEL, wtot), lambda b, i: (0, 0)),
        ],
        out_specs=[pl.BlockSpec((1, tm, wd), row) for wd in INPROJ_WIDTHS],
        out_shape=[jax.ShapeDtypeStruct((BATCH, SEQ, wd), F32) for wd in INPROJ_WIDTHS],
        compiler_params=_cparams(("arbitrary", "arbitrary")),
        name="inproj0",
    )(x, g.reshape(1, D_MODEL), shift.reshape(BATCH, 1, D_MODEL), scale.reshape(BATCH, 1, D_MODEL), w_arranged)


GLA_TG = 512


def _gla_kernel(qk_ref, v_ref, r_ref, misc_ref, w2_ref, bg_ref, og_ref, o_ref, st_ref, la_ref):
    C = GLA_CHUNK

    @pl.when(pl.program_id(1) == 0)
    def _():
        st_ref[...] = jnp.zeros_like(st_ref)

    z = jnp.dot(misc_ref[0], w2_ref[...], precision=HI, preferred_element_type=F32) + bg_ref[...]
    la_ref[...] = _log_sigmoid(z) / GLA_TAU

    ri = lax.broadcasted_iota(jnp.int32, (C, C), 0)
    ci = lax.broadcasted_iota(jnp.int32, (C, C), 1)
    causal = ri >= ci
    tril = causal.astype(F32)
    lane = lax.broadcasted_iota(jnp.int32, (1, GLA_QK_W), 1)
    og = og_ref[...]

    def chunk(c, carry):
        r0 = pl.multiple_of(c * C, C)
        rows = pl.ds(r0, C)
        bc = jnp.dot(tril, la_ref[rows, :], precision=HI, preferred_element_type=F32)
        b_mid = bc[C // 2:C // 2 + 1, :]
        b_last = bc[C - 1:C, :]
        q = qk_ref[0, rows, 0:GLA_QK_W] * (GLA_DK ** -0.5)
        k = qk_ref[0, rows, GLA_QK_W:2 * GLA_QK_W]
        qd = q * jnp.exp(bc - b_mid)
        kd = (k * jnp.exp(b_mid - bc)).astype(BF16)
        kl = k * jnp.exp(b_last - bc)
        qb = q * jnp.exp(bc)
        st = st_ref[...]
        st_b = st.astype(BF16)
        new_st = st * jnp.exp(b_last)
        for h in range(GLA_HEADS):
            in_head = (lane >= h * GLA_DK) & (lane < (h + 1) * GLA_DK)
            vcols = slice(h * GLA_DV, (h + 1) * GLA_DV)
            s = lax.dot_general(jnp.where(in_head, qd, 0.0).astype(BF16), kd, NT_DIMS, preferred_element_type=F32)
            s = jnp.where(causal, s, 0.0)
            vh = v_ref[0, rows, vcols].astype(BF16)
            o = jnp.dot(s.astype(BF16), vh, preferred_element_type=F32)
            o += lax.dot_general(jnp.where(in_head, qb, 0.0).astype(BF16), st_b, NT_DIMS, preferred_element_type=F32)
            new_st += lax.dot_general(vh, jnp.where(in_head, kl, 0.0).astype(BF16), TN_DIMS, preferred_element_type=F32)
            on = o * lax.rsqrt(jnp.mean(o * o, axis=-1, keepdims=True) + NORM_EPS) * og
            o_ref[0, rows, vcols] = on * _silu(r_ref[0, rows, vcols])
        st_ref[...] = new_st
        return carry

    lax.fori_loop(0, GLA_TG // C, chunk, 0)


def gla_mixer(qk, v, r, misc, w_gate2, b_gate, out_g):
    tg = GLA_TG
    w2 = jnp.zeros((LANES, GLA_QK_W), F32).at[:GLA_GATE_RANK].set(w_gate2)
    row = lambda b, i: (b, i, 0)
    const = lambda b, i: (0, 0)
    return pl.pallas_call(
        _gla_kernel,
        grid=(BATCH, SEQ // tg),
        in_specs=[
            pl.BlockSpec((1, tg, 2 * GLA_QK_W), row),
            pl.BlockSpec((1, tg, GLA_V_W), row),
            pl.BlockSpec((1, tg, GLA_V_W), row),
            pl.BlockSpec((1, tg, LANES), row),
            pl.BlockSpec((LANES, GLA_QK_W), const),
            pl.BlockSpec((1, GLA_QK_W), const),
            pl.BlockSpec((1, GLA_DV), const),
        ],
        out_specs=pl.BlockSpec((1, tg, GLA_V_W), row),
        out_shape=jax.ShapeDtypeStruct((BATCH, SEQ, GLA_V_W), F32),
        scratch_shapes=[pltpu.VMEM((GLA_DV, GLA_QK_W), F32), pltpu.VMEM((tg, GLA_QK_W), F32)],
        compiler_params=_cparams(("arbitrary", "arbitrary")),
        name="gla_mixer",
    )(qk, v, r, misc, w2, b_gate.reshape(1, GLA_QK_W), out_g.reshape(1, GLA_DV))


POS_SIDE = 128


def _rope_table_kernel(freq_ref, pos_ref, cos_ref, sin_ref):
    pos = pos_ref[...].astype(F32)
    for f in range(ROT_HALF):
        ang = pos * freq_ref[f]
        cos_ref[f] = jnp.cos(ang)
        sin_ref[f] = jnp.sin(ang)


def rope_tables(positions):
    inv_freq = jnp.float32(ROPE_THETA) ** (-jnp.arange(ROT_HALF, dtype=F32) / ROT_HALF)
    shp = jax.ShapeDtypeStruct((ROT_HALF, POS_SIDE, POS_SIDE), F32)
    cos, sin = pl.pallas_call(
        _rope_table_kernel,
        in_specs=[pl.BlockSpec(memory_space=pltpu.SMEM), pl.BlockSpec(memory_space=pltpu.VMEM)],
        out_specs=[pl.BlockSpec(memory_space=pltpu.VMEM)] * 2,
        out_shape=[shp, shp],
        name="rope_tables",
    )(inv_freq, positions.reshape(POS_SIDE, POS_SIDE))
    cos = cos.reshape(ROT_HALF, N_TOK).T
    sin = sin.reshape(ROT_HALF, N_TOK).T
    one = jnp.ones((N_TOK, NSA_DH - ROT_DIM), F32)
    zero = jnp.zeros((N_TOK, NSA_DH - ROT_DIM), F32)
    z8 = jnp.zeros((N_TOK, ROT_HALF), F32)
    c64 = jnp.concatenate([cos, cos, one], axis=1)
    sm64 = jnp.concatenate([-sin, z8, zero], axis=1)
    sp64 = jnp.concatenate([z8, sin, zero], axis=1)
    two = lambda t: jnp.concatenate([t, t], axis=1).reshape(BATCH, SEQ, LANES)
    return two(c64), two(sm64), two(sp64)


def _block_diag_ones(width):
    h = np.arange(width) // NSA_DH
    return jnp.asarray((h[:, None] == h[None, :]).astype(np.float32))


def _head_norm_rope(x, gain, bd, c, sm, sp):
    width = x.shape[-1]
    reps = width // LANES
    ss = jnp.dot(x * x, bd, precision=HI, preferred_element_type=F32)
    y = x * lax.rsqrt(ss * (1.0 / NSA_DH) + NORM_EPS) * gain
    tile = lambda t: jnp.concatenate([t] * reps, axis=1) if reps > 1 else t
    return (y * tile(c) + pltpu.roll(y, width - ROT_HALF, 1) * tile(sm) + pltpu.roll(y, ROT_HALF, 1) * tile(sp))


PREP_TM = 512


def _prep_kernel(q_ref, ks_ref, kw_ref, c_ref, sm_ref, sp_ref, gq_ref, gk_ref, bd_ref, qo_ref, kso_ref, kwo_ref):
    c, sm, sp = c_ref[0], sm_ref[0], sp_ref[0]
    bd = bd_ref[...]
    bd1 = bd_ref[0:LANES, 0:LANES]
    qo_ref[0] = _head_norm_rope(q_ref[0], gq_ref[...], bd, c, sm, sp) * (NSA_DH ** -0.5)
    kso_ref[0] = _head_norm_rope(ks_ref[0], gk_ref[0:1, :], bd1, c, sm, sp)
    kwo_ref[0] = _head_norm_rope(kw_ref[0], gk_ref[1:2, :], bd1, c, sm, sp)


def nsa_prep(nq, nkv, tabs, q_gain, k_gain):
    tm = PREP_TM
    row = lambda b, i: (b, i, 0)
    const = lambda b, i: (0, 0)
    gq = jnp.tile(q_gain, NSA_HEADS).reshape(1, NSA_Q_W)
    gk = jnp.stack([jnp.tile(k_gain[1], NSA_KV_GROUPS), jnp.tile(k_gain[2], NSA_KV_GROUPS)])
    return pl.pallas_call(
        _prep_kernel,
        grid=(BATCH, SEQ // tm),
        in_specs=[
            pl.BlockSpec((1, tm, NSA_Q_W), row),
            pl.BlockSpec((1, tm, NSA_KV_W), lambda b, i: (b, i, 2)),
            pl.BlockSpec((1, tm, NSA_KV_W), lambda b, i: (b, i, 4)),
            pl.BlockSpec((1, tm, LANES), row),
            pl.BlockSpec((1, tm, LANES), row),
            pl.BlockSpec((1, tm, LANES), row),
            pl.BlockSpec((1, NSA_Q_W), const),
            pl.BlockSpec((2, NSA_KV_W), const),
            pl.BlockSpec((NSA_Q_W, NSA_Q_W), const),
        ],
        out_specs=[pl.BlockSpec((1, tm, NSA_Q_W), row), pl.BlockSpec((1, tm, NSA_KV_W), row),
                   pl.BlockSpec((1, tm, NSA_KV_W), row)],
        out_shape=[jax.ShapeDtypeStruct((BATCH, SEQ, NSA_Q_W), F32), jax.ShapeDtypeStruct((BATCH, SEQ, NSA_KV_W), F32),
                   jax.ShapeDtypeStruct((BATCH, SEQ, NSA_KV_W), F32)],
        compiler_params=_cparams(("arbitrary", "arbitrary")),
        name="nsa_prep",
    )(nq, nkv, nkv, *tabs, gq, gk, _block_diag_ones(NSA_Q_W))


SEG_W = CMP_STRIDE * NSA_DH


def _cmp_kernel(xk_ref, xv_ref, pe_ref, w1_ref, w2_ref, gain_ref, c_ref, sm_ref, sp_ref, bd_ref, ko_ref, vo_ref):
    def compress(x_ref, kv):
        out = jnp.zeros((N_CMP_PAD, LANES), F32)
        for g in range(NSA_KV_GROUPS):
            x = x_ref[0, g]
            ha = jnp.dot(x + pe_ref[kv, 0], w1_ref[kv, 0:SEG_W, :], precision=HI, preferred_element_type=F32)
            hb = jnp.dot(x + pe_ref[kv, 1], w1_ref[kv, SEG_W:2 * SEG_W, :], precision=HI, preferred_element_type=F32)
            hid = ha + pltpu.roll(hb, N_CMP_PAD - 1, 0)
            out += jnp.dot(jax.nn.gelu(hid), w2_ref[kv, g], precision=HI, preferred_element_type=F32)
        return out

    ko_ref[0] = _head_norm_rope(compress(xk_ref, 0), gain_ref[...], bd_ref[...], c_ref[0], sm_ref[0], sp_ref[0])
    vo_ref[0] = compress(xv_ref, 1)


def nsa_compress(xk, xv, cmp_pe, cmp_w1, cmp_w2, k_gain0, ctabs):
    pe = cmp_pe.reshape(2, 2, 1, SEG_W)
    w2 = jnp.zeros((2, NSA_KV_GROUPS, CMP_HIDDEN, LANES), F32)
    for g in range(NSA_KV_GROUPS):
        w2 = w2.at[:, g, :, g * NSA_DH:(g + 1) * NSA_DH].set(cmp_w2)
    seg = pl.BlockSpec((1, NSA_KV_GROUPS, N_CMP_PAD, SEG_W), lambda b: (b, 0, 0, 0))
    tab = pl.BlockSpec((1, N_CMP_PAD, LANES), lambda b: (b, 0, 0))
    full = lambda shape: pl.BlockSpec(shape, lambda b: (0,) * len(shape))
    return pl.pallas_call(
        _cmp_kernel,
        grid=(BATCH,),
        in_specs=[seg, seg, full((2, 2, 1, SEG_W)), full((2, 2 * SEG_W, CMP_HIDDEN)),
                  full((2, NSA_KV_GROUPS, CMP_HIDDEN, LANES)), full((1, LANES)), tab, tab, tab, full((LANES, LANES))],
        out_specs=[tab, tab],
        out_shape=[jax.ShapeDtypeStruct((BATCH, N_CMP_PAD, LANES), F32)] * 2,
        compiler_params=_cparams(("arbitrary",)),
        name="nsa_compress",
    )(xk, xv, pe, cmp_w1, w2, jnp.tile(k_gain0, NSA_KV_GROUPS).reshape(1, LANES), *ctabs, _block_diag_ones(LANES))


CA_TQ = 256
SUBLANES = 8


def _cover_t():
    ci = np.arange(N_CMP_PAD)[None, :] * CMP_STRIDE
    sj = np.arange(N_SEL)[:, None] * SEL_BLOCK
    valid = np.arange(N_CMP_PAD)[None, :] < N_CMP
    return jnp.asarray(((ci < sj + SEL_BLOCK) & (ci + CMP_LEN > sj) & valid).astype(np.float32))


def _cattn_kernel(q_ref, kc_ref, vct_ref, cov_ref, gl_ref, o_ref, sel_ref):
    tq = CA_TQ
    q0 = pl.program_id(2) * tq
    kc = kc_ref[0, 0]
    vct = vct_ref[0, 0].astype(BF16)
    cend = lax.broadcasted_iota(jnp.int32, (N_CMP_PAD, tq), 0) * CMP_STRIDE + (CMP_LEN - 1)
    tc = q0 + lax.broadcasted_iota(jnp.int32, (N_CMP_PAD, tq), 1)
    cmask = cend <= tc
    psum = jnp.zeros((N_CMP_PAD, tq), F32)
    for h in range(NSA_HPG):
        s = jnp.dot(kc, q_ref[0, h], precision=HI, preferred_element_type=F32)
        s = jnp.where(cmask, s, NEG_INF)
        m = jnp.max(s, axis=0, keepdims=True)
        e = jnp.where(cmask, jnp.exp(s - m), 0.0)
        l = jnp.sum(e, axis=0, keepdims=True)
        p = e / jnp.where(l > 0.0, l, 1.0)
        psum += p
        gate = jax.nn.sigmoid(gl_ref[0, h, 0:1, :])
        o_ref[0, h] = jnp.dot(vct, p.astype(BF16), preferred_element_type=F32) * gate

    imp = jnp.dot(cov_ref[...], psum, precision=HI, preferred_element_type=F32)
    jj = lax.broadcasted_iota(jnp.int32, (N_SEL, tq), 0)
    tt = q0 + lax.broadcasted_iota(jnp.int32, (N_SEL, tq), 1)
    cur = jnp.right_shift(tt, 6)
    forced = (jj == 0) | (jj == cur) | (jj == cur - 1)
    valid = jj * SEL_BLOCK <= tt
    score = jnp.where(valid, imp + jnp.where(forced, FORCE_BONUS, 0.0), NEG_INF)

    n_slab = N_SEL // SUBLANES
    sc = [score[SUBLANES * a:SUBLANES * (a + 1)] for a in range(n_slab)]
    rk = [jnp.zeros((SUBLANES, tq), F32) for _ in range(n_slab)]
    sub = lax.broadcasted_iota(jnp.int32, (SUBLANES, tq), 0)
    for jp in range(N_SEL):
        a0, r0 = divmod(jp, SUBLANES)
        row = sc[a0][r0:r0 + 1]
        for a in range(n_slab):
            gt = jnp.where(row > sc[a], 1.0, 0.0)
            ge = jnp.where(row >= sc[a], 1.0, 0.0)
            if a < a0:
                rk[a] = rk[a] + gt
            elif a > a0:
                rk[a] = rk[a] + ge
            else:
                rk[a] = rk[a] + jnp.where(sub > r0, ge, gt)
    for a in range(n_slab):
        rows = slice(SUBLANES * a, SUBLANES * (a + 1))
        sel_ref[0, 0, rows, :] = jnp.where((rk[a] < SEL_TOPK) & valid[rows], 1.0, 0.0)


def nsa_cmp_attn(q_t, kcmp, vcmp_t, gl_t):
    tq = CA_TQ
    return pl.pallas_call(
        _cattn_kernel,
        grid=(BATCH, NSA_KV_GROUPS, SEQ // tq),
        in_specs=[
            pl.BlockSpec((1, NSA_HPG, NSA_DH, tq), lambda b, g, i: (b, g, 0, i)),
            pl.BlockSpec((1, 1, N_CMP_PAD, NSA_DH), lambda b, g, i: (b, g, 0, 0)),
            pl.BlockSpec((1, 1, NSA_DH, N_CMP_PAD), lambda b, g, i: (b, g, 0, 0)),
            pl.BlockSpec((N_SEL, N_CMP_PAD), lambda b, g, i: (0, 0)),
            pl.BlockSpec((1, NSA_HPG, 3, tq), lambda b, g, i: (b, g, 0, i)),
        ],
        out_specs=[pl.BlockSpec((1, NSA_HPG, NSA_DH, tq), lambda b, g, i: (b, g, 0, i)),
                   pl.BlockSpec((1, 1, N_SEL, tq), lambda b, g, i: (b, g, 0, i))],
        out_shape=[jax.ShapeDtypeStruct((BATCH, NSA_HEADS, NSA_DH, SEQ), F32),
                   jax.ShapeDtypeStruct((BATCH, NSA_KV_GROUPS, N_SEL, SEQ), F32)],
        compiler_params=_cparams(("arbitrary", "arbitrary", "arbitrary")),
        name="nsa_cmp_attn",
    )(q_t, kcmp, vcmp_t, _cover_t(), gl_t)


SA_TQ = 256
SA_TK = 512
M_INIT = -1e20


def _sattn_kernel(q_ref, k_ref, vt_ref, sel_ref, gl_ref, prev_ref, o_ref, m_ref, l_ref, acc_ref, bias_ref):
    tq, tk = SA_TQ, SA_TK
    i = pl.program_id(2)
    kt = pl.program_id(3)

    @pl.when(kt == 0)
    def _():
        m_ref[...] = jnp.full(m_ref.shape, M_INIT, F32)
        l_ref[...] = jnp.zeros_like(l_ref)
        acc_ref[...] = jnp.zeros_like(acc_ref)

    @pl.when(kt * tk < (i + 1) * tq)
    def _():
        blocks = tk // SEL_BLOCK
        selrows = sel_ref[0, 0, pl.ds(pl.multiple_of(kt * blocks, blocks), blocks), :]
        kpos = kt * tk + lax.broadcasted_iota(jnp.int32, (SEL_BLOCK, tq), 0)
        tt = i * tq + lax.broadcasted_iota(jnp.int32, (SEL_BLOCK, tq), 1)
        for jb in range(blocks):
            ok = (selrows[jb:jb + 1, :] > 0.5) & (kpos + jb * SEL_BLOCK <= tt)
            bias_ref[jb * SEL_BLOCK:(jb + 1) * SEL_BLOCK, :] = jnp.where(ok, 0.0, NEG_INF)
        k = k_ref[0, 0].astype(BF16)
        vt = vt_ref[0, 0].astype(BF16)
        for h in range(NSA_HPG):
            s = jnp.dot(k, q_ref[0, h].astype(BF16), preferred_element_type=F32) + bias_ref[...]
            m_prev = m_ref[h]
            m_new = jnp.maximum(m_prev, jnp.max(s, axis=0, keepdims=True))
            alpha = jnp.exp(m_prev - m_new)
            p = jnp.exp(s - m_new)
            l_ref[h] = alpha * l_ref[h] + jnp.sum(p, axis=0, keepdims=True)
            acc_ref[h] = alpha * acc_ref[h] + jnp.dot(vt, p.astype(BF16), preferred_element_type=F32)
            m_ref[h] = m_new

    @pl.when(kt == pl.num_programs(3) - 1)
    def _():
        for h in range(NSA_HPG):
            gate = jax.nn.sigmoid(gl_ref[0, h, 1:2, :])
            o_ref[0, h] = prev_ref[0, h] + acc_ref[h] / l_ref[h] * gate


def nsa_sel_attn(q_t, ksel, vsel_t, sel_t, gl_t, prev):
    tq, tk = SA_TQ, SA_TK
    last_kt = lambda i: ((i + 1) * tq - 1) // tk
    qspec = pl.BlockSpec((1, NSA_HPG, NSA_DH, tq), lambda b, g, i, kt: (b, g, 0, i))
    return pl.pallas_call(
        _sattn_kernel,
        grid=(BATCH, NSA_KV_GROUPS, SEQ // tq, SEQ // tk),
        in_specs=[
            qspec,
            pl.BlockSpec((1, 1, tk, NSA_DH), lambda b, g, i, kt: (b, g, jnp.minimum(kt, last_kt(i)), 0)),
            pl.BlockSpec((1, 1, NSA_DH, tk), lambda b, g, i, kt: (b, g, 0, jnp.minimum(kt, last_kt(i)))),
            pl.BlockSpec((1, 1, N_SEL, tq), lambda b, g, i, kt: (b, g, 0, i)),
            pl.BlockSpec((1, NSA_HPG, 3, tq), lambda b, g, i, kt: (b, g, 0, i)),
            qspec,
        ],
        out_specs=qspec,
        out_shape=jax.ShapeDtypeStruct((BATCH, NSA_HEADS, NSA_DH, SEQ), F32),
        scratch_shapes=[pltpu.VMEM((NSA_HPG, 1, tq), F32), pltpu.VMEM((NSA_HPG, 1, tq), F32),
                        pltpu.VMEM((NSA_HPG, NSA_DH, tq), F32), pltpu.VMEM((tk, tq), F32)],
        input_output_aliases={5: 0},
        compiler_params=_cparams(("arbitrary", "arbitrary", "arbitrary", "arbitrary")),
        name="nsa_sel_attn",
    )(q_t, ksel, vsel_t, sel_t, gl_t, prev)


WA_TQ = 256
WA_TILES = WINDOW // WA_TQ + 1


def _window_bias():
    kl = np.arange(WA_TILES * WA_TQ)[:, None]
    ql = np.arange(WA_TQ)[None, :]
    diff = ql - kl + WINDOW
    return jnp.asarray(np.where((diff >= 0) & (diff < WINDOW), 0.0, NEG_INF).astype(np.float32))


def _wattn_kernel(q_ref, k0_ref, k1_ref, k2_ref, v0_ref, v1_ref, v2_ref, bias_ref, gl_ref, prev_ref, o_ref):
    tq = WA_TQ
    i = pl.program_id(2)
    k_refs = (k0_ref, k1_ref, k2_ref)
    v_refs = (v0_ref, v1_ref, v2_ref)
    biases = []
    for d in range(WA_TILES):
        in_seq = i - (WA_TILES - 1) + d >= 0
        biases.append(jnp.where(in_seq, bias_ref[d * tq:(d + 1) * tq, :], NEG_INF))
    ks = [r[0, 0].astype(BF16) for r in k_refs]
    vs = [r[0, 0].astype(BF16) for r in v_refs]
    for h in range(NSA_HPG):
        q = q_ref[0, h].astype(BF16)
        ss = [jnp.dot(ks[d], q, preferred_element_type=F32) + biases[d] for d in range(WA_TILES)]
        m = functools.reduce(jnp.maximum, [jnp.max(s, axis=0, keepdims=True) for s in ss])
        ps = [jnp.exp(s - m) for s in ss]
        l = functools.reduce(jnp.add, [jnp.sum(p, axis=0, keepdims=True) for p in ps])
        acc = functools.reduce(jnp.add, [jnp.dot(vs[d], ps[d].astype(BF16), preferred_element_type=F32)
                                         for d in range(WA_TILES)])
        gate = jax.nn.sigmoid(gl_ref[0, h, 2:3, :])
        o_ref[0, h] = prev_ref[0, h] + acc / l * gate


def nsa_win_attn(q_t, kwin, vwin_t, gl_t, prev):
    tq = WA_TQ
    qspec = pl.BlockSpec((1, NSA_HPG, NSA_DH, tq), lambda b, g, i: (b, g, 0, i))
    tile = lambda d: (lambda i: jnp.maximum(i - (WA_TILES - 1) + d, 0))
    kspec = lambda d: pl.BlockSpec((1, 1, tq, NSA_DH), lambda b, g, i: (b, g, tile(d)(i), 0))
    vspec = lambda d: pl.BlockSpec((1, 1, NSA_DH, tq), lambda b, g, i: (b, g, 0, tile(d)(i)))
    return pl.pallas_call(
        _wattn_kernel,
        grid=(BATCH, NSA_KV_GROUPS, SEQ // tq),
        in_specs=[qspec] + [kspec(d) for d in range(WA_TILES)] + [vspec(d) for d in range(WA_TILES)] + [
            pl.BlockSpec((WA_TILES * tq, tq), lambda b, g, i: (0, 0)),
            pl.BlockSpec((1, NSA_HPG, 3, tq), lambda b, g, i: (b, g, 0, i)),
            qspec,
        ],
        out_specs=qspec,
        out_shape=jax.ShapeDtypeStruct((BATCH, NSA_HEADS, NSA_DH, SEQ), F32),
        input_output_aliases={2 * WA_TILES + 3: 0},
        compiler_params=_cparams(("arbitrary", "arbitrary", "arbitrary")),
        name="nsa_win_attn",
    )(q_t, *([kwin] * WA_TILES), *([vwin_t] * WA_TILES), _window_bias(), gl_t, prev)


def nsa_mixer(nq, nkv, misc, positions, q_gain, k_gain, cmp_pe, cmp_w1, cmp_w2):
    tabs = rope_tables(positions)
    q_r, ks_r, kw_r = nsa_prep(nq, nkv, tabs, q_gain, k_gain)
    group_major = lambda t: t.reshape(BATCH, SEQ, NSA_KV_GROUPS, NSA_DH).transpose(0, 2, 1, 3)
    group_major_t = lambda t: t.reshape(BATCH, SEQ, NSA_KV_GROUPS, NSA_DH).transpose(0, 2, 3, 1)
    col = lambda n: nkv[..., n * NSA_KV_W:(n + 1) * NSA_KV_W]
    segs = lambda t: group_major(t).reshape(BATCH, NSA_KV_GROUPS, N_CMP_PAD, SEG_W)
    last = jnp.minimum(jnp.arange(N_CMP_PAD) * CMP_STRIDE + CMP_LEN - 1, SEQ - 1)
    ctabs = tuple(t[:, last] for t in tabs)
    kcmp, vcmp = nsa_compress(segs(col(0)), segs(col(1)), cmp_pe, cmp_w1, cmp_w2, k_gain[0], ctabs)
    kcmp = kcmp.reshape(BATCH, N_CMP_PAD, NSA_KV_GROUPS, NSA_DH).transpose(0, 2, 1, 3)
    vcmp_t = vcmp.reshape(BATCH, N_CMP_PAD, NSA_KV_GROUPS, NSA_DH).transpose(0, 2, 3, 1)
    q_t = q_r.reshape(BATCH, SEQ, NSA_HEADS, NSA_DH).transpose(0, 2, 3, 1)
    gl_t = misc[..., GLA_GATE_RANK:GLA_GATE_RANK + NSA_HEADS * 3].reshape(BATCH, SEQ, NSA_HEADS, 3).transpose(0, 2, 3, 1)
    o_t, sel_t = nsa_cmp_attn(q_t, kcmp, vcmp_t, gl_t)
    o_t = nsa_sel_attn(q_t, group_major(ks_r), group_major_t(col(3)), sel_t, gl_t, o_t)
    o_t = nsa_win_attn(q_t, group_major(kw_r), group_major_t(col(5)), gl_t, o_t)
    return o_t.transpose(0, 3, 1, 2).reshape(BATCH, SEQ, NSA_Q_W)


ROUTE_ROWS = 8


def _top2_sum(a, b, c, d):
    hi1, lo1 = jnp.maximum(a, b), jnp.minimum(a, b)
    hi2, lo2 = jnp.maximum(c, d), jnp.minimum(c, d)
    return jnp.maximum(hi1, hi2) + jnp.maximum(jnp.minimum(hi1, hi2), jnp.maximum(lo1, lo2))


def _moe_prenorm_route(xn, g_ref, sh_ref, sc_ref, wr_ref, rb_ref, h_ref, route_ref):
    h = _rms_mod(xn, g_ref[...], sh_ref[0], sc_ref[0])
    h_ref[0] = h
    logits = lax.dot_general(wr_ref[...], h, NT_DIMS, precision=HI, preferred_element_type=F32)
    scores = jax.nn.sigmoid(logits)
    sel = scores + rb_ref[...]
    epg = EXPERTS_PER_GROUP
    srow = lambda e: sel[e:e + 1, :]
    grp = [_top2_sum(*[srow(epg * g + r) for r in range(epg)]) for g in range(N_EXPERT_GROUPS)]
    best, gi = grp[0], jnp.zeros_like(grp[0], dtype=jnp.int32)
    for g in range(1, N_EXPERT_GROUPS):
        better = grp[g] > best
        gi = jnp.where(better, g, gi)
        best = jnp.where(better, grp[g], best)

    def in_group(mat, r):
        out = mat[r:r + 1, :]
        for g in range(1, N_EXPERT_GROUPS):
            out = jnp.where(gi == g, mat[epg * g + r:epg * g + r + 1, :], out)
        return out

    v = [in_group(sel, r) for r in range(epg)]
    sc = [in_group(scores, r) for r in range(epg)]
    b1, i1, w1 = v[0], jnp.zeros_like(gi), sc[0]
    for r in range(1, epg):
        better = v[r] > b1
        i1 = jnp.where(better, r, i1)
        w1 = jnp.where(better, sc[r], w1)
        b1 = jnp.where(better, v[r], b1)
    b2 = jnp.full_like(b1, -3e38)
    i2, w2 = jnp.zeros_like(gi), jnp.zeros_like(w1)
    for r in range(epg):
        better = (i1 != r) & (v[r] > b2)
        i2 = jnp.where(better, r, i2)
        w2 = jnp.where(better, sc[r], w2)
        b2 = jnp.where(better, v[r], b2)
    tot = w1 + w2
    zero = jnp.zeros_like(w1)
    route_ref[0] = jnp.concatenate(
        [(gi * epg + i1).astype(F32), (gi * epg + i2).astype(F32), w1 / tot, w2 / tot] + [zero] * (ROUTE_ROWS - 4), axis=0)


def _route_specs(tm, row, vec, const):
    in_specs = [pl.BlockSpec((1, D_MODEL), const), pl.BlockSpec((1, 1, D_MODEL), vec), pl.BlockSpec((1, 1, D_MODEL), vec),
                pl.BlockSpec((N_EXPERTS, D_MODEL), const), pl.BlockSpec((N_EXPERTS, 1), const)]
    out_specs = [pl.BlockSpec((1, tm, D_MODEL), row), pl.BlockSpec((1, ROUTE_ROWS, tm), lambda b, i: (b, 0, i))]
    out_shape = [jax.ShapeDtypeStruct((BATCH, SEQ, D_MODEL), F32), jax.ShapeDtypeStruct((BATCH, ROUTE_ROWS, SEQ), F32)]
    return in_specs, out_specs, out_shape


def _route_args(g, shift, scale, w_router, router_bias):
    return (g.reshape(1, D_MODEL), shift.reshape(BATCH, 1, D_MODEL), scale.reshape(BATCH, 1, D_MODEL),
            w_router.T, router_bias.reshape(N_EXPERTS, 1))


OUTPROJ_TM = 512


def _outproj0_kernel(oa_ref, ob_ref, w_ref, x_ref, gate_ref, g_ref, sh_ref, sc_ref, wr_ref, rb_ref,
                     xo_ref, h_ref, route_ref):
    y = jnp.dot(oa_ref[0].astype(BF16), w_ref[0:GLA_V_W, :], preferred_element_type=F32)
    y += jnp.dot(ob_ref[0].astype(BF16), w_ref[GLA_V_W:GLA_V_W + NSA_Q_W, :], preferred_element_type=F32)
    xn = x_ref[0] + gate_ref[0] * y
    xo_ref[0] = xn
    _moe_prenorm_route(xn, g_ref, sh_ref, sc_ref, wr_ref, rb_ref, h_ref, route_ref)


def outproj0(o_a, o_b, w_out, x, gate, route_args):
    tm = OUTPROJ_TM
    row = lambda b, i: (b, i, 0)
    vec = lambda b, i: (b, 0, 0)
    const = lambda b, i: (0, 0)
    r_in, r_out, r_shape = _route_specs(tm, row, vec, const)
    return pl.pallas_call(
        _outproj0_kernel,
        grid=(BATCH, SEQ // tm),
        in_specs=[pl.BlockSpec((1, tm, GLA_V_W), row), pl.BlockSpec((1, tm, NSA_Q_W), row),
                  pl.BlockSpec((GLA_V_W + NSA_Q_W, D_MODEL), const), pl.BlockSpec((1, tm, D_MODEL), row),
                  pl.BlockSpec((1, 1, D_MODEL), vec)] + r_in,
        out_specs=[pl.BlockSpec((1, tm, D_MODEL), row)] + r_out,
        out_shape=[jax.ShapeDtypeStruct((BATCH, SEQ, D_MODEL), F32)] + r_shape,
        compiler_params=_cparams(("arbitrary", "arbitrary")),
        name="outproj0",
    )(o_a, o_b, w_out.astype(BF16), x, gate.reshape(BATCH, 1, D_MODEL), *route_args)


GMLP_TM = 256


def _gmlp_kernel(x_ref, g1_ref, sh1_ref, sc1_ref, win_ref, ng_ref, ws_ref, bs_ref, wout_ref, gate_ref,
                 g_ref, sh_ref, sc_ref, wr_ref, rb_ref, xo_ref, h_ref, route_ref, gated_ref):
    x = x_ref[0]
    h = _rms_mod(x, g1_ref[...], sh1_ref[0], sc1_ref[0]).astype(BF16)
    u = jax.nn.gelu(jnp.dot(h, win_ref[:, 0:SGU_WIDTH], preferred_element_type=F32))
    v = jax.nn.gelu(jnp.dot(h, win_ref[:, SGU_WIDTH:2 * SGU_WIDTH], preferred_element_type=F32))
    v = (v * lax.rsqrt(jnp.mean(v * v, axis=-1, keepdims=True) + NORM_EPS) * ng_ref[...]).astype(BF16)
    ri = lax.broadcasted_iota(jnp.int32, (SGU_CHUNK, SGU_CHUNK), 0)
    ci = lax.broadcasted_iota(jnp.int32, (SGU_CHUNK, SGU_CHUNK), 1)
    for g in range(SGU_GROUPS):
        w = jnp.where(ri >= ci, ws_ref[g], 0.0).astype(BF16)
        cols = slice(g * SGU_GROUP_DIM, (g + 1) * SGU_GROUP_DIM)
        for c in range(GMLP_TM // SGU_CHUNK):
            rows = slice(c * SGU_CHUNK, (c + 1) * SGU_CHUNK)
            mix = jnp.dot(w, v[rows, cols], preferred_element_type=F32) + bs_ref[:, g:g + 1]
            gated_ref[rows, cols] = (u[rows, cols] * mix).astype(BF16)
    y = jnp.dot(gated_ref[...], wout_ref[...], preferred_element_type=F32)
    xn = x + gate_ref[0] * y
    xo_ref[0] = xn
    _moe_prenorm_route(xn, g_ref, sh_ref, sc_ref, wr_ref, rb_ref, h_ref, route_ref)


def gmlp_layer(x, g1, shift1, scale1, w_in, norm_g, w_s, b_s, w_out, gate, route_args):
    tm = GMLP_TM
    row = lambda b, i: (b, i, 0)
    vec = lambda b, i: (b, 0, 0)
    const = lambda b, i: (0, 0)
    r_in, r_out, r_shape = _route_specs(tm, row, vec, const)
    vspec = pl.BlockSpec((1, 1, D_MODEL), vec)
    return pl.pallas_call(
        _gmlp_kernel,
        grid=(BATCH, SEQ // tm),
        in_specs=[pl.BlockSpec((1, tm, D_MODEL), row), pl.BlockSpec((1, D_MODEL), const), vspec, vspec,
                  pl.BlockSpec((D_MODEL, 2 * SGU_WIDTH), const), pl.BlockSpec((1, SGU_WIDTH), const),
                  pl.BlockSpec((SGU_GROUPS, SGU_CHUNK, SGU_CHUNK), lambda b, i: (0, 0, 0)),
                  pl.BlockSpec((SGU_CHUNK, SGU_GROUPS), const), pl.BlockSpec((SGU_WIDTH, D_MODEL), const), vspec] + r_in,
        out_specs=[pl.BlockSpec((1, tm, D_MODEL), row)] + r_out,
        out_shape=[jax.ShapeDtypeStruct((BATCH, SEQ, D_MODEL), F32)] + r_shape,
        scratch_shapes=[pltpu.VMEM((tm, SGU_WIDTH), BF16)],
        compiler_params=_cparams(("arbitrary", "arbitrary"), vmem_mib=56),
        name="gmlp_layer",
    )(x, g1.reshape(1, D_MODEL), shift1.reshape(BATCH, 1, D_MODEL), scale1.reshape(BATCH, 1, D_MODEL),
      w_in.astype(BF16), norm_g.reshape(1, SGU_WIDTH), w_s, b_s.T, w_out.astype(BF16),
      gate.reshape(BATCH, 1, D_MODEL), *route_args)


MOE_TM = 256
MOE_ROWS = MOE_TOPK * N_TOK + N_EXPERTS * MOE_TM
MOE_TILES = MOE_ROWS // MOE_TM


def moe_dispatch(route):
    tm = MOE_TM
    n_asg = MOE_TOPK * N_TOK
    rec = route.transpose(1, 0, 2).reshape(ROUTE_ROWS, N_TOK)
    e = rec[0:2].astype(jnp.int32).reshape(n_asg)
    w = rec[2:4].reshape(n_asg)
    onehot = (e[:, None] == jnp.arange(N_EXPERTS, dtype=jnp.int32)[None, :]).astype(jnp.int32)
    csum = jnp.cumsum(onehot, axis=0)
    rank = jnp.sum(csum * onehot, axis=1) - 1
    counts = csum[-1]
    padded = (counts + tm - 1) // tm * tm
    ends = jnp.cumsum(padded)
    pos = (ends - padded)[e] + rank
    n_tiles = (ends[-1] // tm).astype(jnp.int32).reshape(1)
    tile_e = jnp.minimum(jnp.searchsorted(ends, jnp.arange(MOE_TILES, dtype=jnp.int32) * tm, side="right"),
                         N_EXPERTS - 1).astype(jnp.int32)
    a = jnp.arange(n_asg, dtype=jnp.int32)
    src = jnp.zeros((MOE_ROWS,), jnp.int32).at[pos].set(a % N_TOK)
    dst = jnp.full((MOE_ROWS,), -1, jnp.int32).at[pos].set(a)
    wrow = jnp.zeros((MOE_ROWS,), F32).at[pos].set(w)
    return (tile_e, n_tiles, src.reshape(MOE_TILES, 1, tm), dst.reshape(MOE_TILES, 1, tm),
            wrow.reshape(MOE_TILES, tm, 1))


def _moe_kernel(te_ref, nt_ref, src_ref, dst_ref, w_ref, h_hbm, wg_ref, wu_ref, wd_ref, out_hbm,
                xbuf, ybuf, gsem, ssem):
    tm = MOE_TM

    def row_in(r, src_row):
        return pltpu.make_async_copy(h_hbm.at[pl.ds(src_row, 1), :], xbuf.at[pl.ds(r, 1), :], gsem)

    def row_out(r, dst_row):
        return pltpu.make_async_copy(ybuf.at[pl.ds(r, 1), :], out_hbm.at[pl.ds(dst_row, 1), :], ssem)

    def each_row(fn):
        def body(r, carry):
            fn(r)
            return carry
        lax.fori_loop(0, tm, body, 0)

    @pl.when(pl.program_id(0) < nt_ref[0])
    def _():
        each_row(lambda r: row_in(r, src_ref[0, 0, r]).start())
        each_row(lambda r: row_in(r, 0).wait())
        x = xbuf[...].astype(BF16)
        gate = jnp.dot(x, wg_ref[0, 0], preferred_element_type=F32)
        up = jnp.dot(x, wu_ref[0, 0], preferred_element_type=F32)
        hid = (_silu(gate) * up).astype(BF16)
        ybuf[...] = jnp.dot(hid, wd_ref[0, 0], preferred_element_type=F32) * w_ref[0]
        def real_rows(fn):
            def body(r):
                @pl.when(dst_ref[0, 0, r] >= 0)
                def _():
                    fn(r)
            each_row(body)

        real_rows(lambda r: row_out(r, dst_ref[0, 0, r]).start())
        real_rows(lambda r: row_out(r, 0).wait())


def moe_experts(h, plan, w_gate, w_up, w_down, layer):
    tm = MOE_TM
    tile_e, n_tiles, src, dst, wrow = plan
    idx = pl.BlockSpec((1, 1, tm), lambda t, te, nt: (t, 0, 0), memory_space=pltpu.SMEM)
    wspec = lambda k, n: pl.BlockSpec((1, 1, k, n), lambda t, te, nt: (layer, te[t], 0, 0))
    return pl.pallas_call(
        _moe_kernel,
        grid_spec=pltpu.PrefetchScalarGridSpec(
            num_scalar_prefetch=2,
            grid=(MOE_TILES,),
            in_specs=[idx, idx, pl.BlockSpec((1, tm, 1), lambda t, te, nt: (t, 0, 0)),
                      pl.BlockSpec(memory_space=pl.ANY),
                      wspec(D_MODEL, EXPERT_HIDDEN), wspec(D_MODEL, EXPERT_HIDDEN), wspec(EXPERT_HIDDEN, D_MODEL)],
            out_specs=pl.BlockSpec(memory_space=pl.ANY),
            scratch_shapes=[pltpu.VMEM((tm, D_MODEL), F32), pltpu.VMEM((tm, D_MODEL), F32),
                            pltpu.SemaphoreType.DMA(()), pltpu.SemaphoreType.DMA(())],
        ),
        out_shape=jax.ShapeDtypeStruct((MOE_TOPK * N_TOK, D_MODEL), F32),
        compiler_params=_cparams(("arbitrary",)),
        name="moe_experts",
    )(tile_e, n_tiles, src, dst, wrow, h, w_gate, w_up, w_down)


COMB_TM = 512


def _combine_kernel(x_ref, y0_ref, y1_ref, gate_ref, o_ref):
    o_ref[0] = x_ref[0] + gate_ref[0] * (y0_ref[...] + y1_ref[...])


def moe_combine(x, y2, gate):
    tm = COMB_TM
    per_b = SEQ // tm
    return pl.pallas_call(
        _combine_kernel,
        grid=(BATCH, per_b),
        in_specs=[pl.BlockSpec((1, tm, D_MODEL), lambda b, i: (b, i, 0)),
                  pl.BlockSpec((tm, D_MODEL), lambda b, i: (b * per_b + i, 0)),
                  pl.BlockSpec((tm, D_MODEL), lambda b, i: (N_TOK // tm + b * per_b + i, 0)),
                  pl.BlockSpec((1, 1, D_MODEL), lambda b, i: (b, 0, 0))],
        out_specs=pl.BlockSpec((1, tm, D_MODEL), lambda b, i: (b, i, 0)),
        out_shape=jax.ShapeDtypeStruct((BATCH, SEQ, D_MODEL), F32),
        compiler_params=_cparams(("arbitrary", "arbitrary")),
        name="moe_combine",
    )(x, y2, y2, gate.reshape(BATCH, 1, D_MODEL))


def moe_layer(x, h, route, gate, w_gate, w_up, w_down, layer):
    y2 = moe_experts(h.reshape(N_TOK, D_MODEL), moe_dispatch(route), w_gate, w_up, w_down, layer)
    return moe_combine(x, y2, gate)


def kernel(x, c, positions, w_ada, b_ada, norm_g, w_in_ab, w_out_ab, gla_w_gate2, gla_b_gate, gla_norm_g, nsa_q_gain, nsa_k_gain, nsa_cmp_pe, nsa_cmp_w1, nsa_cmp_w2, w_in_c, sgu_norm_g, sgu_w_s, sgu_b_s, w_out_c, w_router, router_bias, w_gate, w_up, w_down):
    mod = ada_modulation(c, w_ada, b_ada)
    qk, gv, gr, nq, nkv, misc = inproj0(x, norm_g[0, 0], mod[0, :, 0], mod[0, :, 1], _arrange_w_in(w_in_ab[0]))
    o_a = gla_mixer(qk, gv, gr, misc, gla_w_gate2[0], gla_b_gate[0], gla_norm_g[0])
    o_b = nsa_mixer(nq, nkv, misc, positions, nsa_q_gain[0], nsa_k_gain[0], nsa_cmp_pe[0], nsa_cmp_w1[0], nsa_cmp_w2[0])
    wg, wu, wd = w_gate.astype(BF16), w_up.astype(BF16), w_down.astype(BF16)
    route_args = lambda l: _route_args(norm_g[l, 1], mod[l, :, 3], mod[l, :, 4], w_router, router_bias)
    x1, h, route = outproj0(o_a, o_b, w_out_ab[0], x, mod[0, :, 2], route_args(0))
    x2 = moe_layer(x1, h, route, mod[0, :, 5], wg, wu, wd, 0)
    x3, h, route = gmlp_layer(x2, norm_g[1, 0], mod[1, :, 0], mod[1, :, 1], w_in_c[0], sgu_norm_g[0], sgu_w_s[0],
                              sgu_b_s[0], w_out_c[0], mod[1, :, 2], route_args(1))
    return moe_layer(x3, h, route, mod[1, :, 5], wg, wu, wd, 1)
```

```python
import functools

import numpy as np
import jax
import jax.numpy as jnp
from jax import lax
from jax.experimental import pallas as pl
from jax.experimental.pallas import tpu as pltpu

D_MODEL = 1024
BATCH = 2
SEQ = 8192
DEPTH = 2
N_TOK = BATCH * SEQ

GLA_HEADS = 4
GLA_DK = 64
GLA_DV = 128
GLA_GATE_RANK = 16
GLA_TAU = 16.0
GLA_CHUNK = 64
NSA_HEADS = 8
NSA_KV_GROUPS = 2
NSA_HPG = NSA_HEADS // NSA_KV_GROUPS
NSA_DH = 64
CMP_LEN = 32
CMP_STRIDE = 16
CMP_HIDDEN = 256
SEL_BLOCK = 64
SEL_TOPK = 16
WINDOW = 512
ROPE_THETA = 500000.0
ROT_DIM = NSA_DH // 4
ROT_HALF = ROT_DIM // 2
SGU_CHUNK = 128
SGU_GROUPS = 8
SGU_WIDTH = 2048
SGU_GROUP_DIM = SGU_WIDTH // SGU_GROUPS
N_EXPERTS = 16
N_EXPERT_GROUPS = 4
EXPERTS_PER_GROUP = N_EXPERTS // N_EXPERT_GROUPS
MOE_TOPK = 2
EXPERT_HIDDEN = 512

GLA_QK_W = GLA_HEADS * GLA_DK
GLA_V_W = GLA_HEADS * GLA_DV
NSA_Q_W = NSA_HEADS * NSA_DH
NSA_KV_W = NSA_KV_GROUPS * NSA_DH
N_CMP = (SEQ - CMP_LEN) // CMP_STRIDE + 1
N_CMP_PAD = SEQ // CMP_STRIDE
N_SEL = SEQ // SEL_BLOCK

NORM_EPS = 1e-6
NEG_INF = -1e30
FORCE_BONUS = 1e4

LANES = 128
MIB = 1024 * 1024

F32 = jnp.float32
BF16 = jnp.bfloat16
HI = lax.Precision.HIGHEST
NT_DIMS = (((1,), (1,)), ((), ()))
TN_DIMS = (((0,), (0,)), ((), ()))


def _cparams(sem, vmem_mib=48):
    return pltpu.CompilerParams(dimension_semantics=sem, vmem_limit_bytes=vmem_mib * MIB)


def _rms_mod(x, g, shift, scale):
    y = x * lax.rsqrt(jnp.mean(x * x, axis=-1, keepdims=True) + NORM_EPS) * g
    return y * (1 + scale) + shift


def _silu(x):
    return x * jax.nn.sigmoid(x)


def _log_sigmoid(z):
    return jnp.minimum(z, 0.0) - jnp.log1p(jnp.exp(-jnp.abs(z)))


ADA_TN = 1536
ADA_ROWS = 8


def _ada_kernel(c_ref, w_ref, b_ref, o_ref):
    cond = _silu(c_ref[...])
    o_ref[0] = jnp.dot(cond, w_ref[0], precision=HI, preferred_element_type=F32) + b_ref[0]


def ada_modulation(c, w_ada, b_ada):
    c8 = jnp.zeros((ADA_ROWS, D_MODEL), F32).at[:BATCH].set(c)
    width = 6 * D_MODEL
    out = pl.pallas_call(
        _ada_kernel,
        grid=(DEPTH, width // ADA_TN),
        in_specs=[
            pl.BlockSpec((ADA_ROWS, D_MODEL), lambda l, j: (0, 0)),
            pl.BlockSpec((1, D_MODEL, ADA_TN), lambda l, j: (l, 0, j)),
            pl.BlockSpec((1, 1, ADA_TN), lambda l, j: (l, 0, j)),
        ],
        out_specs=pl.BlockSpec((1, ADA_ROWS, ADA_TN), lambda l, j: (l, 0, j)),
        out_shape=jax.ShapeDtypeStruct((DEPTH, ADA_ROWS, width), F32),
        compiler_params=_cparams(("arbitrary", "arbitrary")),
        name="ada_modulation",
    )(c8, w_ada, b_ada.reshape(DEPTH, 1, width))
    return out[:, :BATCH].reshape(DEPTH, BATCH, 6, D_MODEL)


INPROJ_TM = 512
INPROJ_WIDTHS = (2 * GLA_QK_W, GLA_V_W, GLA_V_W, NSA_Q_W, 6 * NSA_KV_W, LANES)


def _arrange_w_in(w_in):
    o = np.cumsum((0, GLA_QK_W, GLA_QK_W, GLA_V_W, GLA_GATE_RANK, GLA_V_W, NSA_Q_W, 6 * NSA_KV_W, NSA_HEADS * 3))
    gq_gk = w_in[:, o[0]:o[2]]
    gv = w_in[:, o[2]:o[3]]
    glr = w_in[:, o[3]:o[4]]
    gr = w_in[:, o[4]:o[5]]
    nq = w_in[:, o[5]:o[6]]
    nkv = w_in[:, o[6]:o[7]]
    ng = w_in[:, o[7]:o[8]]
    pad = jnp.zeros((D_MODEL, LANES - GLA_GATE_RANK - NSA_HEADS * 3), w_in.dtype)
    return jnp.concatenate([gq_gk, gv, gr, nq, nkv, glr, ng, pad], axis=1).astype(BF16)


def _inproj0_kernel(x_ref, g_ref, sh_ref, sc_ref, w_ref, *o_refs):
    h = _rms_mod(x_ref[0], g_ref[...], sh_ref[0], sc_ref[0]).astype(BF16)
    off = 0
    for o_ref, wd in zip(o_refs, INPROJ_WIDTHS):
        o_ref[0] = jnp.dot(h, w_ref[:, off:off + wd], preferred_element_type=F32)
        off += wd


def inproj0(x, g, shift, scale, w_arranged):
    tm = INPROJ_TM
    wtot = sum(INPROJ_WIDTHS)
    row = lambda b, i: (b, i, 0)
    vec = lambda b, i: (b, 0, 0)
    return pl.pallas_call(
        _inproj0_kernel,
        grid=(BATCH, SEQ // tm),
        in_specs=[
            pl.BlockSpec((1, tm, D_MODEL), row),
            pl.BlockSpec((1, D_MODEL), lambda b, i: (0, 0)),
            pl.BlockSpec((1, 1, D_MODEL), vec),
            pl.BlockSpec((1, 1, D_MODEL), vec),
            pl.BlockSpec((D_MODEL, wtot), lambda b, i: (0, 0)),
        ],
        out_specs=[pl.BlockSpec((1, tm, wd), row) for wd in INPROJ_WIDTHS],
        out_shape=[jax.ShapeDtypeStruct((BATCH, SEQ, wd), F32) for wd in INPROJ_WIDTHS],
        compiler_params=_cparams(("arbitrary", "arbitrary")),
        name="inproj0",
    )(x, g.reshape(1, D_MODEL), shift.reshape(BATCH, 1, D_MODEL), scale.reshape(BATCH, 1, D_MODEL), w_arranged)


GLA_TG = 512


def _gla_kernel(qk_ref, v_ref, r_ref, misc_ref, w2_ref, bg_ref, og_ref, o_ref, st_ref, la_ref):
    C = GLA_CHUNK

    @pl.when(pl.program_id(1) == 0)
    def _():
        st_ref[...] = jnp.zeros_like(st_ref)

    z = jnp.dot(misc_ref[0], w2_ref[...], precision=HI, preferred_element_type=F32) + bg_ref[...]
    la_ref[...] = _log_sigmoid(z) / GLA_TAU

    ri = lax.broadcasted_iota(jnp.int32, (C, C), 0)
    ci = lax.broadcasted_iota(jnp.int32, (C, C), 1)
    causal = ri >= ci
    tril = causal.astype(F32)
    lane = lax.broadcasted_iota(jnp.int32, (1, GLA_QK_W), 1)
    og = og_ref[...]

    def chunk(c, carry):
        r0 = pl.multiple_of(c * C, C)
        rows = pl.ds(r0, C)
        bc = jnp.dot(tril, la_ref[rows, :], precision=HI, preferred_element_type=F32)
        b_mid = bc[C // 2:C // 2 + 1, :]
        b_last = bc[C - 1:C, :]
        q = qk_ref[0, rows, 0:GLA_QK_W] * (GLA_DK ** -0.5)
        k = qk_ref[0, rows, GLA_QK_W:2 * GLA_QK_W]
        qd = q * jnp.exp(bc - b_mid)
        kd = (k * jnp.exp(b_mid - bc)).astype(BF16)
        kl = k * jnp.exp(b_last - bc)
        qb = q * jnp.exp(bc)
        st = st_ref[...]
        st_b = st.astype(BF16)
        new_st = st * jnp.exp(b_last)
        for h in range(GLA_HEADS):
            in_head = (lane >= h * GLA_DK) & (lane < (h + 1) * GLA_DK)
            vcols = slice(h * GLA_DV, (h + 1) * GLA_DV)
            s = lax.dot_general(jnp.where(in_head, qd, 0.0).astype(BF16), kd, NT_DIMS, preferred_element_type=F32)
            s = jnp.where(causal, s, 0.0)
            vh = v_ref[0, rows, vcols].astype(BF16)
            o = jnp.dot(s.astype(BF16), vh, preferred_element_type=F32)
            o += lax.dot_general(jnp.where(in_head, qb, 0.0).astype(BF16), st_b, NT_DIMS, preferred_element_type=F32)
            new_st += lax.dot_general(vh, jnp.where(in_head, kl, 0.0).astype(BF16), TN_DIMS, preferred_element_type=F32)
            on = o * lax.rsqrt(jnp.mean(o * o, axis=-1, keepdims=True) + NORM_EPS) * og
            o_ref[0, rows, vcols] = on * _silu(r_ref[0, rows, vcols])
        st_ref[...] = new_st
        return carry

    lax.fori_loop(0, GLA_TG // C, chunk, 0)


def gla_mixer(qk, v, r, misc, w_gate2, b_gate, out_g):
    tg = GLA_TG
    w2 = jnp.zeros((LANES, GLA_QK_W), F32).at[:GLA_GATE_RANK].set(w_gate2)
    row = lambda b, i: (b, i, 0)
    const = lambda b, i: (0, 0)
    return pl.pallas_call(
        _gla_kernel,
        grid=(BATCH, SEQ // tg),
        in_specs=[
            pl.BlockSpec((1, tg, 2 * GLA_QK_W), row),
            pl.BlockSpec((1, tg, GLA_V_W), row),
            pl.BlockSpec((1, tg, GLA_V_W), row),
            pl.BlockSpec((1, tg, LANES), row),
            pl.BlockSpec((LANES, GLA_QK_W), const),
            pl.BlockSpec((1, GLA_QK_W), const),
            pl.BlockSpec((1, GLA_DV), const),
        ],
        out_specs=pl.BlockSpec((1, tg, GLA_V_W), row),
        out_shape=jax.ShapeDtypeStruct((BATCH, SEQ, GLA_V_W), F32),
        scratch_shapes=[pltpu.VMEM((GLA_DV, GLA_QK_W), F32), pltpu.VMEM((tg, GLA_QK_W), F32)],
        compiler_params=_cparams(("arbitrary", "arbitrary")),
        name="gla_mixer",
    )(qk, v, r, misc, w2, b_gate.reshape(1, GLA_QK_W), out_g.reshape(1, GLA_DV))


POS_SIDE = 128


def _rope_table_kernel(freq_ref, pos_ref, cos_ref, sin_ref):
    pos = pos_ref[...].astype(F32)
    for f in range(ROT_HALF):
        ang = pos * freq_ref[f]
        cos_ref[f] = jnp.cos(ang)
        sin_ref[f] = jnp.sin(ang)


def rope_tables(positions):
    inv_freq = jnp.float32(ROPE_THETA) ** (-jnp.arange(ROT_HALF, dtype=F32) / ROT_HALF)
    shp = jax.ShapeDtypeStruct((ROT_HALF, POS_SIDE, POS_SIDE), F32)
    cos, sin = pl.pallas_call(
        _rope_table_kernel,
        in_specs=[pl.BlockSpec(memory_space=pltpu.SMEM), pl.BlockSpec(memory_space=pltpu.VMEM)],
        out_specs=[pl.BlockSpec(memory_space=pltpu.VMEM)] * 2,
        out_shape=[shp, shp],
        name="rope_tables",
    )(inv_freq, positions.reshape(POS_SIDE, POS_SIDE))
    cos = cos.reshape(ROT_HALF, N_TOK).T
    sin = sin.reshape(ROT_HALF, N_TOK).T
    one = jnp.ones((N_TOK, NSA_DH - ROT_DIM), F32)
    zero = jnp.zeros((N_TOK, NSA_DH - ROT_DIM), F32)
    z8 = jnp.zeros((N_TOK, ROT_HALF), F32)
    c64 = jnp.concatenate([cos, cos, one], axis=1)
    sm64 = jnp.concatenate([-sin, z8, zero], axis=1)
    sp64 = jnp.concatenate([z8, sin, zero], axis=1)
    two = lambda t: jnp.concatenate([t, t], axis=1).reshape(BATCH, SEQ, LANES)
    return two(c64), two(sm64), two(sp64)


def _block_diag_ones(width):
    h = np.arange(width) // NSA_DH
    return jnp.asarray((h[:, None] == h[None, :]).astype(np.float32))


def _head_norm_rope(x, gain, bd, c, sm, sp):
    width = x.shape[-1]
    reps = width // LANES
    ss = jnp.dot(x * x, bd, precision=HI, preferred_element_type=F32)
    y = x * lax.rsqrt(ss * (1.0 / NSA_DH) + NORM_EPS) * gain
    tile = lambda t: jnp.concatenate([t] * reps, axis=1) if reps > 1 else t
    return (y * tile(c) + pltpu.roll(y, width - ROT_HALF, 1) * tile(sm) + pltpu.roll(y, ROT_HALF, 1) * tile(sp))


PREP_TM = 512


def _prep_kernel(q_ref, ks_ref, kw_ref, c_ref, sm_ref, sp_ref, gq_ref, gk_ref, bd_ref, qo_ref, kso_ref, kwo_ref):
    c, sm, sp = c_ref[0], sm_ref[0], sp_ref[0]
    bd = bd_ref[...]
    bd1 = bd_ref[0:LANES, 0:LANES]
    qo_ref[0] = _head_norm_rope(q_ref[0], gq_ref[...], bd, c, sm, sp) * (NSA_DH ** -0.5)
    kso_ref[0] = _head_norm_rope(ks_ref[0], gk_ref[0:1, :], bd1, c, sm, sp)
    kwo_ref[0] = _head_norm_rope(kw_ref[0], gk_ref[1:2, :], bd1, c, sm, sp)


def nsa_prep(nq, nkv, tabs, q_gain, k_gain):
    tm = PREP_TM
    row = lambda b, i: (b, i, 0)
    const = lambda b, i: (0, 0)
    gq = jnp.tile(q_gain, NSA_HEADS).reshape(1, NSA_Q_W)
    gk = jnp.stack([jnp.tile(k_gain[1], NSA_KV_GROUPS), jnp.tile(k_gain[2], NSA_KV_GROUPS)])
    return pl.pallas_call(
        _prep_kernel,
        grid=(BATCH, SEQ // tm),
        in_specs=[
            pl.BlockSpec((1, tm, NSA_Q_W), row),
            pl.BlockSpec((1, tm, NSA_KV_W), lambda b, i: (b, i, 2)),
            pl.BlockSpec((1, tm, NSA_KV_W), lambda b, i: (b, i, 4)),
            pl.BlockSpec((1, tm, LANES), row),
            pl.BlockSpec((1, tm, LANES), row),
            pl.BlockSpec((1, tm, LANES), row),
            pl.BlockSpec((1, NSA_Q_W), const),
            pl.BlockSpec((2, NSA_KV_W), const),
            pl.BlockSpec((NSA_Q_W, NSA_Q_W), const),
        ],
        out_specs=[pl.BlockSpec((1, tm, NSA_Q_W), row), pl.BlockSpec((1, tm, NSA_KV_W), row),
                   pl.BlockSpec((1, tm, NSA_KV_W), row)],
        out_shape=[jax.ShapeDtypeStruct((BATCH, SEQ, NSA_Q_W), F32), jax.ShapeDtypeStruct((BATCH, SEQ, NSA_KV_W), F32),
                   jax.ShapeDtypeStruct((BATCH, SEQ, NSA_KV_W), F32)],
        compiler_params=_cparams(("arbitrary", "arbitrary")),
        name="nsa_prep",
    )(nq, nkv, nkv, *tabs, gq, gk, _block_diag_ones(NSA_Q_W))


SEG_W = CMP_STRIDE * NSA_DH


def _cmp_kernel(xk_ref, xv_ref, pe_ref, w1_ref, w2_ref, gain_ref, c_ref, sm_ref, sp_ref, bd_ref, ko_ref, vo_ref):
    def compress(x_ref, kv):
        out = jnp.zeros((N_CMP_PAD, LANES), F32)
        for g in range(NSA_KV_GROUPS):
            x = x_ref[0, g]
            ha = jnp.dot(x + pe_ref[kv, 0], w1_ref[kv, 0:SEG_W, :], precision=HI, preferred_element_type=F32)
            hb = jnp.dot(x + pe_ref[kv, 1], w1_ref[kv, SEG_W:2 * SEG_W, :], precision=HI, preferred_element_type=F32)
            hid = ha + pltpu.roll(hb, N_CMP_PAD - 1, 0)
            out += jnp.dot(jax.nn.gelu(hid), w2_ref[kv, g], precision=HI, preferred_element_type=F32)
        return out

    ko_ref[0] = _head_norm_rope(compress(xk_ref, 0), gain_ref[...], bd_ref[...], c_ref[0], sm_ref[0], sp_ref[0])
    vo_ref[0] = compress(xv_ref, 1)


def nsa_compress(xk, xv, cmp_pe, cmp_w1, cmp_w2, k_gain0, ctabs):
    pe = cmp_pe.reshape(2, 2, 1, SEG_W)
    w2 = jnp.zeros((2, NSA_KV_GROUPS, CMP_HIDDEN, LANES), F32)
    for g in range(NSA_KV_GROUPS):
        w2 = w2.at[:, g, :, g * NSA_DH:(g + 1) * NSA_DH].set(cmp_w2)
    seg = pl.BlockSpec((1, NSA_KV_GROUPS, N_CMP_PAD, SEG_W), lambda b: (b, 0, 0, 0))
    tab = pl.BlockSpec((1, N_CMP_PAD, LANES), lambda b: (b, 0, 0))
    full = lambda shape: pl.BlockSpec(shape, lambda b: (0,) * len(shape))
    return pl.pallas_call(
        _cmp_kernel,
        grid=(BATCH,),
        in_specs=[seg, seg, full((2, 2, 1, SEG_W)), full((2, 2 * SEG_W, CMP_HIDDEN)),
                  full((2, NSA_KV_GROUPS, CMP_HIDDEN, LANES)), full((1, LANES)), tab, tab, tab, full((LANES, LANES))],
        out_specs=[tab, tab],
        out_shape=[jax.ShapeDtypeStruct((BATCH, N_CMP_PAD, LANES), F32)] * 2,
        compiler_params=_cparams(("arbitrary",)),
        name="nsa_compress",
    )(xk, xv, pe, cmp_w1, w2, jnp.tile(k_gain0, NSA_KV_GROUPS).reshape(1, LANES), *ctabs, _block_diag_ones(LANES))


CA_TQ = 256
SUBLANES = 8


def _cover_t():
    ci = np.arange(N_CMP_PAD)[None, :] * CMP_STRIDE
    sj = np.arange(N_SEL)[:, None] * SEL_BLOCK
    valid = np.arange(N_CMP_PAD)[None, :] < N_CMP
    return jnp.asarray(((ci < sj + SEL_BLOCK) & (ci + CMP_LEN > sj) & valid).astype(np.float32))


def _cattn_kernel(q_ref, kc_ref, vct_ref, cov_ref, gl_ref, o_ref, sel_ref):
    tq = CA_TQ
    q0 = pl.program_id(2) * tq
    kc = kc_ref[0, 0]
    vct = vct_ref[0, 0].astype(BF16)
    cend = lax.broadcasted_iota(jnp.int32, (N_CMP_PAD, tq), 0) * CMP_STRIDE + (CMP_LEN - 1)
    tc = q0 + lax.broadcasted_iota(jnp.int32, (N_CMP_PAD, tq), 1)
    cmask = cend <= tc
    psum = jnp.zeros((N_CMP_PAD, tq), F32)
    for h in range(NSA_HPG):
        s = jnp.dot(kc, q_ref[0, h], precision=HI, preferred_element_type=F32)
        s = jnp.where(cmask, s, NEG_INF)
        m = jnp.max(s, axis=0, keepdims=True)
        e = jnp.where(cmask, jnp.exp(s - m), 0.0)
        l = jnp.sum(e, axis=0, keepdims=True)
        p = e / jnp.where(l > 0.0, l, 1.0)
        psum += p
        gate = jax.nn.sigmoid(gl_ref[0, h, 0:1, :])
        o_ref[0, h] = jnp.dot(vct, p.astype(BF16), preferred_element_type=F32) * gate

    imp = jnp.dot(cov_ref[...], psum, precision=HI, preferred_element_type=F32)
    jj = lax.broadcasted_iota(jnp.int32, (N_SEL, tq), 0)
    tt = q0 + lax.broadcasted_iota(jnp.int32, (N_SEL, tq), 1)
    cur = jnp.right_shift(tt, 6)
    forced = (jj == 0) | (jj == cur) | (jj == cur - 1)
    valid = jj * SEL_BLOCK <= tt
    score = jnp.where(valid, imp + jnp.where(forced, FORCE_BONUS, 0.0), NEG_INF)

    n_slab = N_SEL // SUBLANES
    sc = [score[SUBLANES * a:SUBLANES * (a + 1)] for a in range(n_slab)]
    rk = [jnp.zeros((SUBLANES, tq), F32) for _ in range(n_slab)]
    sub = lax.broadcasted_iota(jnp.int32, (SUBLANES, tq), 0)
    for jp in range(N_SEL):
        a0, r0 = divmod(jp, SUBLANES)
        row = sc[a0][r0:r0 + 1]
        for a in range(n_slab):
            gt = jnp.where(row > sc[a], 1.0, 0.0)
            ge = jnp.where(row >= sc[a], 1.0, 0.0)
            if a < a0:
                rk[a] = rk[a] + gt
            elif a > a0:
                rk[a] = rk[a] + ge
            else:
                rk[a] = rk[a] + jnp.where(sub > r0, ge, gt)
    for a in range(n_slab):
        rows = slice(SUBLANES * a, SUBLANES * (a + 1))
        sel_ref[0, 0, rows, :] = jnp.where((rk[a] < SEL_TOPK) & valid[rows], 1.0, 0.0)


def nsa_cmp_attn(q_t, kcmp, vcmp_t, gl_t):
    tq = CA_TQ
    return pl.pallas_call(
        _cattn_kernel,
        grid=(BATCH, NSA_KV_GROUPS, SEQ // tq),
        in_specs=[
            pl.BlockSpec((1, NSA_HPG, NSA_DH, tq), lambda b, g, i: (b, g, 0, i)),
            pl.BlockSpec((1, 1, N_CMP_PAD, NSA_DH), lambda b, g, i: (b, g, 0, 0)),
            pl.BlockSpec((1, 1, NSA_DH, N_CMP_PAD), lambda b, g, i: (b, g, 0, 0)),
            pl.BlockSpec((N_SEL, N_CMP_PAD), lambda b, g, i: (0, 0)),
            pl.BlockSpec((1, NSA_HPG, 3, tq), lambda b, g, i: (b, g, 0, i)),
        ],
        out_specs=[pl.BlockSpec((1, NSA_HPG, NSA_DH, tq), lambda b, g, i: (b, g, 0, i)),
                   pl.BlockSpec((1, 1, N_SEL, tq), lambda b, g, i: (b, g, 0, i))],
        out_shape=[jax.ShapeDtypeStruct((BATCH, NSA_HEADS, NSA_DH, SEQ), F32),
                   jax.ShapeDtypeStruct((BATCH, NSA_KV_GROUPS, N_SEL, SEQ), F32)],
        compiler_params=_cparams(("arbitrary", "arbitrary", "arbitrary")),
        name="nsa_cmp_attn",
    )(q_t, kcmp, vcmp_t, _cover_t(), gl_t)


SA_TQ = 256
SA_TK = 512
M_INIT = -1e20


def _sattn_kernel(q_ref, k_ref, vt_ref, sel_ref, gl_ref, prev_ref, o_ref, m_ref, l_ref, acc_ref, bias_ref):
    tq, tk = SA_TQ, SA_TK
    i = pl.program_id(2)
    kt = pl.program_id(3)

    @pl.when(kt == 0)
    def _():
        m_ref[...] = jnp.full(m_ref.shape, M_INIT, F32)
        l_ref[...] = jnp.zeros_like(l_ref)
        acc_ref[...] = jnp.zeros_like(acc_ref)

    @pl.when(kt * tk < (i + 1) * tq)
    def _():
        blocks = tk // SEL_BLOCK
        selrows = sel_ref[0, 0, pl.ds(pl.multiple_of(kt * blocks, blocks), blocks), :]
        kpos = kt * tk + lax.broadcasted_iota(jnp.int32, (SEL_BLOCK, tq), 0)
        tt = i * tq + lax.broadcasted_iota(jnp.int32, (SEL_BLOCK, tq), 1)
        for jb in range(blocks):
            ok = (selrows[jb:jb + 1, :] > 0.5) & (kpos + jb * SEL_BLOCK <= tt)
            bias_ref[jb * SEL_BLOCK:(jb + 1) * SEL_BLOCK, :] = jnp.where(ok, 0.0, NEG_INF)
        k = k_ref[0, 0].astype(BF16)
        vt = vt_ref[0, 0].astype(BF16)
        for h in range(NSA_HPG):
            s = jnp.dot(k, q_ref[0, h].astype(BF16), preferred_element_type=F32) + bias_ref[...]
            m_prev = m_ref[h]
            m_new = jnp.maximum(m_prev, jnp.max(s, axis=0, keepdims=True))
            alpha = jnp.exp(m_prev - m_new)
            p = jnp.exp(s - m_new)
            l_ref[h] = alpha * l_ref[h] + jnp.sum(p, axis=0, keepdims=True)
            acc_ref[h] = alpha * acc_ref[h] + jnp.dot(vt, p.astype(BF16), preferred_element_type=F32)
            m_ref[h] = m_new

    @pl.when(kt == pl.num_programs(3) - 1)
    def _():
        for h in range(NSA_HPG):
            gate = jax.nn.sigmoid(gl_ref[0, h, 1:2, :])
            o_ref[0, h] = prev_ref[0, h] + acc_ref[h] / l_ref[h] * gate


def nsa_sel_attn(q_t, ksel, vsel_t, sel_t, gl_t, prev):
    tq, tk = SA_TQ, SA_TK
    last_kt = lambda i: ((i + 1) * tq - 1) // tk
    qspec = pl.BlockSpec((1, NSA_HPG, NSA_DH, tq), lambda b, g, i, kt: (b, g, 0, i))
    return pl.pallas_call(
        _sattn_kernel,
        grid=(BATCH, NSA_KV_GROUPS, SEQ // tq, SEQ // tk),
        in_specs=[
            qspec,
            pl.BlockSpec((1, 1, tk, NSA_DH), lambda b, g, i, kt: (b, g, jnp.minimum(kt, last_kt(i)), 0)),
            pl.BlockSpec((1, 1, NSA_DH, tk), lambda b, g, i, kt: (b, g, 0, jnp.minimum(kt, last_kt(i)))),
            pl.BlockSpec((1, 1, N_SEL, tq), lambda b, g, i, kt: (b, g, 0, i)),
            pl.BlockSpec((1, NSA_HPG, 3, tq), lambda b, g, i, kt: (b, g, 0, i)),
            qspec,
        ],
        out_specs=qspec,
        out_shape=jax.ShapeDtypeStruct((BATCH, NSA_HEADS, NSA_DH, SEQ), F32),
        scratch_shapes=[pltpu.VMEM((NSA_HPG, 1, tq), F32), pltpu.VMEM((NSA_HPG, 1, tq), F32),
                        pltpu.VMEM((NSA_HPG, NSA_DH, tq), F32), pltpu.VMEM((tk, tq), F32)],
        input_output_aliases={5: 0},
        compiler_params=_cparams(("arbitrary", "arbitrary", "arbitrary", "arbitrary")),
        name="nsa_sel_attn",
    )(q_t, ksel, vsel_t, sel_t, gl_t, prev)


WA_TQ = 256
WA_TILES = WINDOW // WA_TQ + 1


def _window_bias():
    kl = np.arange(WA_TILES * WA_TQ)[:, None]
    ql = np.arange(WA_TQ)[None, :]
    diff = ql - kl + WINDOW
    return jnp.asarray(np.where((diff >= 0) & (diff < WINDOW), 0.0, NEG_INF).astype(np.float32))


def _wattn_kernel(q_ref, k0_ref, k1_ref, k2_ref, v0_ref, v1_ref, v2_ref, bias_ref, gl_ref, prev_ref, o_ref):
    tq = WA_TQ
    i = pl.program_id(2)
    k_refs = (k0_ref, k1_ref, k2_ref)
    v_refs = (v0_ref, v1_ref, v2_ref)
    biases = []
    for d in range(WA_TILES):
        in_seq = i - (WA_TILES - 1) + d >= 0
        biases.append(jnp.where(in_seq, bias_ref[d * tq:(d + 1) * tq, :], NEG_INF))
    ks = [r[0, 0].astype(BF16) for r in k_refs]
    vs = [r[0, 0].astype(BF16) for r in v_refs]
    for h in range(NSA_HPG):
        q = q_ref[0, h].astype(BF16)
        ss = [jnp.dot(ks[d], q, preferred_element_type=F32) + biases[d] for d in range(WA_TILES)]
        m = functools.reduce(jnp.maximum, [jnp.max(s, axis=0, keepdims=True) for s in ss])
        ps = [jnp.exp(s - m) for s in ss]
        l = functools.reduce(jnp.add, [jnp.sum(p, axis=0, keepdims=True) for p in ps])
        acc = functools.reduce(jnp.add, [jnp.dot(vs[d], ps[d].astype(BF16), preferred_element_type=F32)
                                         for d in range(WA_TILES)])
        gate = jax.nn.sigmoid(gl_ref[0, h, 2:3, :])
        o_ref[0, h] = prev_ref[0, h] + acc / l * gate


def nsa_win_attn(q_t, kwin, vwin_t, gl_t, prev):
    tq = WA_TQ
    qspec = pl.BlockSpec((1, NSA_HPG, NSA_DH, tq), lambda b, g, i: (b, g, 0, i))
    tile = lambda d: (lambda i: jnp.maximum(i - (WA_TILES - 1) + d, 0))
    kspec = lambda d: pl.BlockSpec((1, 1, tq, NSA_DH), lambda b, g, i: (b, g, tile(d)(i), 0))
    vspec = lambda d: pl.BlockSpec((1, 1, NSA_DH, tq), lambda b, g, i: (b, g, 0, tile(d)(i)))
    return pl.pallas_call(
        _wattn_kernel,
        grid=(BATCH, NSA_KV_GROUPS, SEQ // tq),
        in_specs=[qspec] + [kspec(d) for d in range(WA_TILES)] + [vspec(d) for d in range(WA_TILES)] + [
            pl.BlockSpec((WA_TILES * tq, tq), lambda b, g, i: (0, 0)),
            pl.BlockSpec((1, NSA_HPG, 3, tq), lambda b, g, i: (b, g, 0, i)),
            qspec,
        ],
        out_specs=qspec,
        out_shape=jax.ShapeDtypeStruct((BATCH, NSA_HEADS, NSA_DH, SEQ), F32),
        input_output_aliases={2 * WA_TILES + 3: 0},
        compiler_params=_cparams(("arbitrary", "arbitrary", "arbitrary")),
        name="nsa_win_attn",
    )(q_t, *([kwin] * WA_TILES), *([vwin_t] * WA_TILES), _window_bias(), gl_t, prev)


def nsa_mixer(nq, nkv, misc, positions, q_gain, k_gain, cmp_pe, cmp_w1, cmp_w2):
    tabs = rope_tables(positions)
    q_r, ks_r, kw_r = nsa_prep(nq, nkv, tabs, q_gain, k_gain)
    group_major = lambda t: t.reshape(BATCH, SEQ, NSA_KV_GROUPS, NSA_DH).transpose(0, 2, 1, 3)
    group_major_t = lambda t: t.reshape(BATCH, SEQ, NSA_KV_GROUPS, NSA_DH).transpose(0, 2, 3, 1)
    col = lambda n: nkv[..., n * NSA_KV_W:(n + 1) * NSA_KV_W]
    segs = lambda t: group_major(t).reshape(BATCH, NSA_KV_GROUPS, N_CMP_PAD, SEG_W)
    last = jnp.minimum(jnp.arange(N_CMP_PAD) * CMP_STRIDE + CMP_LEN - 1, SEQ - 1)
    ctabs = tuple(t[:, last] for t in tabs)
    kcmp, vcmp = nsa_compress(segs(col(0)), segs(col(1)), cmp_pe, cmp_w1, cmp_w2, k_gain[0], ctabs)
    kcmp = kcmp.reshape(BATCH, N_CMP_PAD, NSA_KV_GROUPS, NSA_DH).transpose(0, 2, 1, 3)
    vcmp_t = vcmp.reshape(BATCH, N_CMP_PAD, NSA_KV_GROUPS, NSA_DH).transpose(0, 2, 3, 1)
    q_t = q_r.reshape(BATCH, SEQ, NSA_HEADS, NSA_DH).transpose(0, 2, 3, 1)
    gl_t = misc[..., GLA_GATE_RANK:GLA_GATE_RANK + NSA_HEADS * 3].reshape(BATCH, SEQ, NSA_HEADS, 3).transpose(0, 2, 3, 1)
    o_t, sel_t = nsa_cmp_attn(q_t, kcmp, vcmp_t, gl_t)
    o_t = nsa_sel_attn(q_t, group_major(ks_r), group_major_t(col(3)), sel_t, gl_t, o_t)
    o_t = nsa_win_attn(q_t, group_major(kw_r), group_major_t(col(5)), gl_t, o_t)
    return o_t.transpose(0, 3, 1, 2).reshape(BATCH, SEQ, NSA_Q_W)


ROUTE_ROWS = 8
HX_W = D_MODEL + 3 * LANES


def _top2_sum(a, b, c, d):
    hi1, lo1 = jnp.maximum(a, b), jnp.minimum(a, b)
    hi2, lo2 = jnp.maximum(c, d), jnp.minimum(c, d)
    return jnp.maximum(hi1, hi2) + jnp.maximum(jnp.minimum(hi1, hi2), jnp.maximum(lo1, lo2))


def _moe_prenorm_route(xn, g_ref, sh_ref, sc_ref, wr_ref, rb_ref, hx_ref, route_ref):
    h = _rms_mod(xn, g_ref[...], sh_ref[0], sc_ref[0])
    logits = lax.dot_general(wr_ref[...], h, NT_DIMS, precision=HI, preferred_element_type=F32)
    scores = jax.nn.sigmoid(logits)
    sel = scores + rb_ref[...]
    epg = EXPERTS_PER_GROUP
    srow = lambda e: sel[e:e + 1, :]
    grp = [_top2_sum(*[srow(epg * g + r) for r in range(epg)]) for g in range(N_EXPERT_GROUPS)]
    best, gi = grp[0], jnp.zeros_like(grp[0], dtype=jnp.int32)
    for g in range(1, N_EXPERT_GROUPS):
        better = grp[g] > best
        gi = jnp.where(better, g, gi)
        best = jnp.where(better, grp[g], best)

    def in_group(mat, r):
        out = mat[r:r + 1, :]
        for g in range(1, N_EXPERT_GROUPS):
            out = jnp.where(gi == g, mat[epg * g + r:epg * g + r + 1, :], out)
        return out

    v = [in_group(sel, r) for r in range(epg)]
    sc = [in_group(scores, r) for r in range(epg)]
    b1, i1, w1 = v[0], jnp.zeros_like(gi), sc[0]
    for r in range(1, epg):
        better = v[r] > b1
        i1 = jnp.where(better, r, i1)
        w1 = jnp.where(better, sc[r], w1)
        b1 = jnp.where(better, v[r], b1)
    b2 = jnp.full_like(b1, -3e38)
    i2, w2 = jnp.zeros_like(gi), jnp.zeros_like(w1)
    for r in range(epg):
        better = (i1 != r) & (v[r] > b2)
        i2 = jnp.where(better, r, i2)
        w2 = jnp.where(better, sc[r], w2)
        b2 = jnp.where(better, v[r], b2)
    tot = w1 + w2
    w1, w2 = w1 / tot, w2 / tot
    zero = jnp.zeros_like(w1)
    route_ref[0] = jnp.concatenate([gi.astype(F32)] + [zero] * (ROUTE_ROWS - 1), axis=0)
    wrows = [jnp.where(i1 == r, w1, jnp.where(i2 == r, w2, 0.0)) for r in range(epg)]
    wmat = jnp.concatenate(wrows + [jnp.zeros((LANES - epg, w1.shape[1]), F32)], axis=0).T
    w_hi = wmat.astype(BF16)
    rest = wmat - w_hi.astype(F32)
    w_mid = rest.astype(BF16)
    w_lo = (rest - w_mid.astype(F32)).astype(BF16)
    hx_ref[0, :, 0:D_MODEL] = h.astype(BF16)
    for n, part in enumerate((w_hi, w_mid, w_lo)):
        hx_ref[0, :, D_MODEL + n * LANES:D_MODEL + (n + 1) * LANES] = part


def _route_specs(tm, row, vec, const):
    in_specs = [pl.BlockSpec((1, D_MODEL), const), pl.BlockSpec((1, 1, D_MODEL), vec), pl.BlockSpec((1, 1, D_MODEL), vec),
                pl.BlockSpec((N_EXPERTS, D_MODEL), const), pl.BlockSpec((N_EXPERTS, 1), const)]
    out_specs = [pl.BlockSpec((1, tm, HX_W), row), pl.BlockSpec((1, ROUTE_ROWS, tm), lambda b, i: (b, 0, i))]
    out_shape = [jax.ShapeDtypeStruct((BATCH, SEQ, HX_W), BF16), jax.ShapeDtypeStruct((BATCH, ROUTE_ROWS, SEQ), F32)]
    return in_specs, out_specs, out_shape


def _route_args(g, shift, scale, w_router, router_bias):
    return (g.reshape(1, D_MODEL), shift.reshape(BATCH, 1, D_MODEL), scale.reshape(BATCH, 1, D_MODEL),
            w_router.T, router_bias.reshape(N_EXPERTS, 1))


OUTPROJ_TM = 512


def _outproj0_kernel(oa_ref, ob_ref, w_ref, x_ref, gate_ref, g_ref, sh_ref, sc_ref, wr_ref, rb_ref,
                     xo_ref, h_ref, route_ref):
    y = jnp.dot(oa_ref[0].astype(BF16), w_ref[0:GLA_V_W, :], preferred_element_type=F32)
    y += jnp.dot(ob_ref[0].astype(BF16), w_ref[GLA_V_W:GLA_V_W + NSA_Q_W, :], preferred_element_type=F32)
    xn = x_ref[0] + gate_ref[0] * y
    xo_ref[0] = xn
    _moe_prenorm_route(xn, g_ref, sh_ref, sc_ref, wr_ref, rb_ref, h_ref, route_ref)


def outproj0(o_a, o_b, w_out, x, gate, route_args):
    tm = OUTPROJ_TM
    row = lambda b, i: (b, i, 0)
    vec = lambda b, i: (b, 0, 0)
    const = lambda b, i: (0, 0)
    r_in, r_out, r_shape = _route_specs(tm, row, vec, const)
    return pl.pallas_call(
        _outproj0_kernel,
        grid=(BATCH, SEQ // tm),
        in_specs=[pl.BlockSpec((1, tm, GLA_V_W), row), pl.BlockSpec((1, tm, NSA_Q_W), row),
                  pl.BlockSpec((GLA_V_W + NSA_Q_W, D_MODEL), const), pl.BlockSpec((1, tm, D_MODEL), row),
                  pl.BlockSpec((1, 1, D_MODEL), vec)] + r_in,
        out_specs=[pl.BlockSpec((1, tm, D_MODEL), row)] + r_out,
        out_shape=[jax.ShapeDtypeStruct((BATCH, SEQ, D_MODEL), F32)] + r_shape,
        compiler_params=_cparams(("arbitrary", "arbitrary")),
        name="outproj0",
    )(o_a, o_b, w_out.astype(BF16), x, gate.reshape(BATCH, 1, D_MODEL), *route_args)


GMLP_TM = 256


def _gmlp_kernel(x_ref, g1_ref, sh1_ref, sc1_ref, win_ref, ng_ref, ws_ref, bs_ref, wout_ref, gate_ref,
                 g_ref, sh_ref, sc_ref, wr_ref, rb_ref, xo_ref, h_ref, route_ref, gated_ref):
    x = x_ref[0]
    h = _rms_mod(x, g1_ref[...], sh1_ref[0], sc1_ref[0]).astype(BF16)
    u = jax.nn.gelu(jnp.dot(h, win_ref[:, 0:SGU_WIDTH], preferred_element_type=F32))
    v = jax.nn.gelu(jnp.dot(h, win_ref[:, SGU_WIDTH:2 * SGU_WIDTH], preferred_element_type=F32))
    v = (v * lax.rsqrt(jnp.mean(v * v, axis=-1, keepdims=True) + NORM_EPS) * ng_ref[...]).astype(BF16)
    ri = lax.broadcasted_iota(jnp.int32, (SGU_CHUNK, SGU_CHUNK), 0)
    ci = lax.broadcasted_iota(jnp.int32, (SGU_CHUNK, SGU_CHUNK), 1)
    for g in range(SGU_GROUPS):
        w = jnp.where(ri >= ci, ws_ref[g], 0.0).astype(BF16)
        cols = slice(g * SGU_GROUP_DIM, (g + 1) * SGU_GROUP_DIM)
        for c in range(GMLP_TM // SGU_CHUNK):
            rows = slice(c * SGU_CHUNK, (c + 1) * SGU_CHUNK)
            mix = jnp.dot(w, v[rows, cols], preferred_element_type=F32) + bs_ref[:, g:g + 1]
            gated_ref[rows, cols] = (u[rows, cols] * mix).astype(BF16)
    y = jnp.dot(gated_ref[...], wout_ref[...], preferred_element_type=F32)
    xn = x + gate_ref[0] * y
    xo_ref[0] = xn
    _moe_prenorm_route(xn, g_ref, sh_ref, sc_ref, wr_ref, rb_ref, h_ref, route_ref)


def gmlp_layer(x, g1, shift1, scale1, w_in, norm_g, w_s, b_s, w_out, gate, route_args):
    tm = GMLP_TM
    row = lambda b, i: (b, i, 0)
    vec = lambda b, i: (b, 0, 0)
    const = lambda b, i: (0, 0)
    r_in, r_out, r_shape = _route_specs(tm, row, vec, const)
    vspec = pl.BlockSpec((1, 1, D_MODEL), vec)
    return pl.pallas_call(
        _gmlp_kernel,
        grid=(BATCH, SEQ // tm),
        in_specs=[pl.BlockSpec((1, tm, D_MODEL), row), pl.BlockSpec((1, D_MODEL), const), vspec, vspec,
                  pl.BlockSpec((D_MODEL, 2 * SGU_WIDTH), const), pl.BlockSpec((1, SGU_WIDTH), const),
                  pl.BlockSpec((SGU_GROUPS, SGU_CHUNK, SGU_CHUNK), lambda b, i: (0, 0, 0)),
                  pl.BlockSpec((SGU_CHUNK, SGU_GROUPS), const), pl.BlockSpec((SGU_WIDTH, D_MODEL), const), vspec] + r_in,
        out_specs=[pl.BlockSpec((1, tm, D_MODEL), row)] + r_out,
        out_shape=[jax.ShapeDtypeStruct((BATCH, SEQ, D_MODEL), F32)] + r_shape,
        scratch_shapes=[pltpu.VMEM((tm, SGU_WIDTH), BF16)],
        compiler_params=_cparams(("arbitrary", "arbitrary"), vmem_mib=56),
        name="gmlp_layer",
    )(x, g1.reshape(1, D_MODEL), shift1.reshape(BATCH, 1, D_MODEL), scale1.reshape(BATCH, 1, D_MODEL),
      w_in.astype(BF16), norm_g.reshape(1, SGU_WIDTH), w_s, b_s.T, w_out.astype(BF16),
      gate.reshape(BATCH, 1, D_MODEL), *route_args)


MOE_TM = 256
MOE_CHUNK = 512
MOE_SORTED = N_TOK + N_EXPERT_GROUPS * MOE_TM
MOE_TILES = MOE_SORTED // MOE_TM
MOE_CHUNKS = N_TOK // MOE_CHUNK
MOE_PAIRS = MOE_TILES + N_EXPERT_GROUPS * MOE_CHUNKS
FLAG_ACTIVE, FLAG_FIRST, FLAG_LAST, FLAG_ZERO = 1, 2, 4, 8


def _plan_kernel(gi_ref, rank_ref, before_ref):
    gi = gi_ref[...]
    r = lax.broadcasted_iota(jnp.int32, (POS_SIDE, POS_SIDE), 0)
    c = lax.broadcasted_iota(jnp.int32, (POS_SIDE, POS_SIDE), 1)
    upper = jnp.where(r <= c, 1.0, 0.0)
    lower_strict = jnp.where(c < r, 1.0, 0.0)
    rank = jnp.zeros((POS_SIDE, POS_SIDE), F32)
    for g in range(N_EXPERT_GROUPS):
        member = jnp.where(gi == g, 1.0, 0.0)
        in_row = jnp.dot(member, upper, precision=HI, preferred_element_type=F32)
        row_total = jnp.broadcast_to(in_row[:, POS_SIDE - 1:POS_SIDE], (POS_SIDE, POS_SIDE))
        before = jnp.dot(lower_strict, row_total, precision=HI, preferred_element_type=F32)
        before_ref[g] = before
        rank += member * (before + in_row - 1.0)
    rank_ref[...] = rank


def moe_plan(route):
    tm = MOE_TM
    i32 = jnp.int32
    gi_f = route[:, 0, :].reshape(POS_SIDE, POS_SIDE)
    rank, before = pl.pallas_call(
        _plan_kernel,
        out_shape=[jax.ShapeDtypeStruct((POS_SIDE, POS_SIDE), F32),
                   jax.ShapeDtypeStruct((N_EXPERT_GROUPS, POS_SIDE, POS_SIDE), F32)],
        name="moe_plan",
    )(gi_f)
    gi = gi_f.reshape(N_TOK).astype(i32)
    groups = jnp.arange(N_EXPERT_GROUPS, dtype=i32)
    member = gi[None, :] == groups[:, None]
    tot = jnp.sum(member, axis=1).astype(i32)
    padded = (tot + tm - 1) // tm * tm
    gend = jnp.cumsum(padded).astype(i32)
    gstart = gend - padded
    pos = jnp.sum(jnp.where(member, gstart[:, None], 0), axis=0).astype(i32) + rank.reshape(N_TOK).astype(i32)
    rows_per_chunk = MOE_CHUNK // POS_SIDE
    cnt_end = jnp.concatenate([before[:, rows_per_chunk::rows_per_chunk, 0].astype(i32), tot[:, None]], axis=1)
    t = jnp.arange(MOE_TILES, dtype=i32)
    n_used = gend[-1] // tm
    tile_g = jnp.minimum(jnp.sum(gend[None, :] <= (t * tm)[:, None], axis=1), N_EXPERT_GROUPS - 1).astype(i32)
    k0 = t * tm - gstart[tile_g]
    k1 = jnp.minimum(k0 + tm, tot[tile_g]) - 1
    ce = cnt_end[tile_g]
    c_lo = jnp.sum(ce <= k0[:, None], axis=1).astype(i32)
    c_hi = jnp.sum(ce <= k1[:, None], axis=1).astype(i32)
    npairs = jnp.where(t < n_used, c_hi - c_lo + 1, 0)
    pend = jnp.cumsum(npairs).astype(i32)
    pstart = pend - npairs
    total = pend[-1]
    l = jnp.arange(MOE_PAIRS, dtype=i32)
    real = l < total
    lt = jnp.minimum(l, total - 1)
    tile_l = jnp.sum(pend[None, :] <= lt[:, None], axis=1).astype(i32)
    chunk_l = c_lo[tile_l] + lt - pstart[tile_l]
    spare_tile = jnp.minimum(n_used + (l - total), MOE_TILES - 1)
    flags = jnp.where(real, FLAG_ACTIVE + jnp.where(lt == pstart[tile_l], FLAG_FIRST, 0)
                      + jnp.where(lt == pend[tile_l] - 1, FLAG_LAST, 0),
                      jnp.where(spare_tile >= n_used, FLAG_ZERO, 0)).astype(i32)
    tile_sched = jnp.where(real, tile_l, spare_tile).astype(i32)
    by_tile = (tile_sched, chunk_l.astype(i32), flags, tile_g[tile_sched])
    cc = jnp.arange(MOE_CHUNKS, dtype=i32)
    is_pair = (cc[:, None] >= c_lo[None, :]) & (cc[:, None] <= c_hi[None, :]) & (t[None, :] < n_used)
    seen = jnp.cumsum(is_pair.reshape(-1).astype(i32))
    flat = jnp.sum(seen[None, :] <= lt[:, None], axis=1).astype(i32)
    chunk_c, tile_c = flat // MOE_TILES, flat % MOE_TILES
    prev_c = jnp.concatenate([jnp.full((1,), -1, i32), chunk_c[:-1]])
    next_c = jnp.concatenate([chunk_c[1:], jnp.full((1,), -1, i32)])
    flags_c = jnp.where(real, FLAG_ACTIVE + jnp.where(chunk_c != prev_c, FLAG_FIRST, 0)
                        + jnp.where((chunk_c != next_c) | (l == total - 1), FLAG_LAST, 0), 0).astype(i32)
    by_chunk = (tile_c, chunk_c, flags_c)
    return pos.reshape(MOE_CHUNKS, 1, MOE_CHUNK), by_tile, by_chunk


def _one_hot_rows(pos_row, tile):
    rows = tile * MOE_TM + lax.broadcasted_iota(jnp.int32, (MOE_TM, MOE_CHUNK), 0)
    return jnp.where(pos_row == rows, 1.0, 0.0).astype(BF16)


def _moe_kernel(tile_ref, chunk_ref, flag_ref, grp_ref, pos_ref, hx_ref, wg_ref, wu_ref, wd_ref, y_ref, acc_ref):
    l = pl.program_id(0)
    flags = flag_ref[l]

    @pl.when((flags & FLAG_FIRST) != 0)
    def _():
        acc_ref[...] = jnp.zeros_like(acc_ref)

    @pl.when((flags & FLAG_ACTIVE) != 0)
    def _():
        acc_ref[...] += jnp.dot(_one_hot_rows(pos_ref[0], tile_ref[l]), hx_ref[0], preferred_element_type=F32)

    @pl.when((flags & FLAG_LAST) != 0)
    def _():
        x = acc_ref[:, 0:D_MODEL].astype(BF16)
        w = functools.reduce(jnp.add, [acc_ref[:, D_MODEL + n * LANES:D_MODEL + (n + 1) * LANES] for n in range(3)])
        y = jnp.zeros((MOE_TM, D_MODEL), F32)
        for r in range(EXPERTS_PER_GROUP):
            gate = jnp.dot(x, wg_ref[0, 0, r], preferred_element_type=F32)
            up = jnp.dot(x, wu_ref[0, 0, r], preferred_element_type=F32)
            hid = (_silu(gate) * up * w[:, r:r + 1]).astype(BF16)
            y += jnp.dot(hid, wd_ref[0, 0, r], preferred_element_type=F32)
        y_ref[...] = y

    @pl.when((flags & FLAG_ZERO) != 0)
    def _():
        y_ref[...] = jnp.zeros_like(y_ref)


def moe_experts(hx, pos, by_tile, w_gate, w_up, w_down, layer):
    grouped = lambda w: w.reshape(DEPTH, N_EXPERT_GROUPS, EXPERTS_PER_GROUP, *w.shape[2:])
    wspec = lambda k, n: pl.BlockSpec((1, 1, EXPERTS_PER_GROUP, k, n), lambda l, t, c, f, g: (layer, g[l], 0, 0, 0))
    return pl.pallas_call(
        _moe_kernel,
        grid_spec=pltpu.PrefetchScalarGridSpec(
            num_scalar_prefetch=4,
            grid=(MOE_PAIRS,),
            in_specs=[pl.BlockSpec((1, 1, MOE_CHUNK), lambda l, t, c, f, g: (c[l], 0, 0)),
                      pl.BlockSpec((1, MOE_CHUNK, HX_W), lambda l, t, c, f, g: (c[l], 0, 0)),
                      wspec(D_MODEL, EXPERT_HIDDEN), wspec(D_MODEL, EXPERT_HIDDEN), wspec(EXPERT_HIDDEN, D_MODEL)],
            out_specs=pl.BlockSpec((MOE_TM, D_MODEL), lambda l, t, c, f, g: (t[l], 0)),
            scratch_shapes=[pltpu.VMEM((MOE_TM, HX_W), F32)],
        ),
        out_shape=jax.ShapeDtypeStruct((MOE_SORTED, D_MODEL), F32),
        compiler_params=_cparams(("arbitrary",), vmem_mib=56),
        name="moe_experts",
    )(*by_tile, pos, hx.reshape(MOE_CHUNKS, MOE_CHUNK, HX_W), grouped(w_gate), grouped(w_up), grouped(w_down))


def _moe_combine_kernel(tile_ref, chunk_ref, flag_ref, pos_ref, y_ref, x_ref, gate_ref, o_ref, acc_ref):
    l = pl.program_id(0)
    flags = flag_ref[l]

    @pl.when((flags & FLAG_FIRST) != 0)
    def _():
        acc_ref[...] = jnp.zeros_like(acc_ref)

    @pl.when((flags & FLAG_ACTIVE) != 0)
    def _():
        onehot = _one_hot_rows(pos_ref[0], tile_ref[l])
        y = y_ref[...]
        y_hi = y.astype(BF16)
        y_lo = (y - y_hi.astype(F32)).astype(BF16)
        acc_ref[...] += (lax.dot_general(onehot, y_hi, TN_DIMS, preferred_element_type=F32)
                         + lax.dot_general(onehot, y_lo, TN_DIMS, preferred_element_type=F32))

    @pl.when((flags & FLAG_LAST) != 0)
    def _():
        o_ref[0] = x_ref[0] + gate_ref[0] * acc_ref[...]


def moe_combine(x, y_sorted, pos, by_chunk, gate):
    per_b = SEQ // MOE_CHUNK
    tok = lambda l, t, c, f: (c[l] // per_b, c[l] % per_b, 0)
    return pl.pallas_call(
        _moe_combine_kernel,
        grid_spec=pltpu.PrefetchScalarGridSpec(
            num_scalar_prefetch=3,
            grid=(MOE_PAIRS,),
            in_specs=[pl.BlockSpec((1, 1, MOE_CHUNK), lambda l, t, c, f: (c[l], 0, 0)),
                      pl.BlockSpec((MOE_TM, D_MODEL), lambda l, t, c, f: (t[l], 0)),
                      pl.BlockSpec((1, MOE_CHUNK, D_MODEL), tok),
                      pl.BlockSpec((1, 1, D_MODEL), lambda l, t, c, f: (c[l] // per_b, 0, 0))],
            out_specs=pl.BlockSpec((1, MOE_CHUNK, D_MODEL), tok),
            scratch_shapes=[pltpu.VMEM((MOE_CHUNK, D_MODEL), F32)],
        ),
        out_shape=jax.ShapeDtypeStruct((BATCH, SEQ, D_MODEL), F32),
        compiler_params=_cparams(("arbitrary",)),
        name="moe_combine",
    )(*by_chunk, pos, y_sorted, x, gate.reshape(BATCH, 1, D_MODEL))


def moe_layer(x, hx, route, gate, w_gate, w_up, w_down, layer):
    pos, by_tile, by_chunk = moe_plan(route)
    y_sorted = moe_experts(hx, pos, by_tile, w_gate, w_up, w_down, layer)
    return moe_combine(x, y_sorted, pos, by_chunk, gate)


def kernel(x, c, positions, w_ada, b_ada, norm_g, w_in_ab, w_out_ab, gla_w_gate2, gla_b_gate, gla_norm_g, nsa_q_gain, nsa_k_gain, nsa_cmp_pe, nsa_cmp_w1, nsa_cmp_w2, w_in_c, sgu_norm_g, sgu_w_s, sgu_b_s, w_out_c, w_router, router_bias, w_gate, w_up, w_down):
    mod = ada_modulation(c, w_ada, b_ada)
    qk, gv, gr, nq, nkv, misc = inproj0(x, norm_g[0, 0], mod[0, :, 0], mod[0, :, 1], _arrange_w_in(w_in_ab[0]))
    o_a = gla_mixer(qk, gv, gr, misc, gla_w_gate2[0], gla_b_gate[0], gla_norm_g[0])
    o_b = nsa_mixer(nq, nkv, misc, positions, nsa_q_gain[0], nsa_k_gain[0], nsa_cmp_pe[0], nsa_cmp_w1[0], nsa_cmp_w2[0])
    wg, wu, wd = w_gate.astype(BF16), w_up.astype(BF16), w_down.astype(BF16)
    route_args = lambda l: _route_args(norm_g[l, 1], mod[l, :, 3], mod[l, :, 4], w_router, router_bias)
    x1, h, route = outproj0(o_a, o_b, w_out_ab[0], x, mod[0, :, 2], route_args(0))
    x2 = moe_layer(x1, h, route, mod[0, :, 5], wg, wu, wd, 0)
    x3, h, route = gmlp_layer(x2, norm_g[1, 0], mod[1, :, 0], mod[1, :, 1], w_in_c[0], sgu_norm_g[0], sgu_w_s[0],
                              sgu_b_s[0], w_out_c[0], mod[1, :, 2], route_args(1))
    return moe_layer(x3, h, route, mod[1, :, 5], wg, wu, wd, 1)
```

```python
import functools

import numpy as np
import jax
import jax.numpy as jnp
from jax import lax
from jax.experimental import pallas as pl
from jax.experimental.pallas import tpu as pltpu

D_MODEL = 1024
BATCH = 2
SEQ = 8192
DEPTH = 2
N_TOK = BATCH * SEQ

GLA_HEADS = 4
GLA_DK = 64
GLA_DV = 128
GLA_GATE_RANK = 16
GLA_TAU = 16.0
GLA_CHUNK = 64
NSA_HEADS = 8
NSA_KV_GROUPS = 2
NSA_HPG = NSA_HEADS // NSA_KV_GROUPS
NSA_DH = 64
CMP_LEN = 32
CMP_STRIDE = 16
CMP_HIDDEN = 256
SEL_BLOCK = 64
SEL_TOPK = 16
WINDOW = 512
ROPE_THETA = 500000.0
ROT_DIM = NSA_DH // 4
ROT_HALF = ROT_DIM // 2
SGU_CHUNK = 128
SGU_GROUPS = 8
SGU_WIDTH = 2048
SGU_GROUP_DIM = SGU_WIDTH // SGU_GROUPS
N_EXPERTS = 16
N_EXPERT_GROUPS = 4
EXPERTS_PER_GROUP = N_EXPERTS // N_EXPERT_GROUPS
MOE_TOPK = 2
EXPERT_HIDDEN = 512

GLA_QK_W = GLA_HEADS * GLA_DK
GLA_V_W = GLA_HEADS * GLA_DV
NSA_Q_W = NSA_HEADS * NSA_DH
NSA_KV_W = NSA_KV_GROUPS * NSA_DH
N_CMP = (SEQ - CMP_LEN) // CMP_STRIDE + 1
N_CMP_PAD = SEQ // CMP_STRIDE
N_SEL = SEQ // SEL_BLOCK

NORM_EPS = 1e-6
NEG_INF = -1e30
FORCE_BONUS = 1e4

LANES = 128
MIB = 1024 * 1024

F32 = jnp.float32
BF16 = jnp.bfloat16
HI = lax.Precision.HIGHEST
NT_DIMS = (((1,), (1,)), ((), ()))
TN_DIMS = (((0,), (0,)), ((), ()))


def _cparams(sem, vmem_mib=48):
    return pltpu.CompilerParams(dimension_semantics=sem, vmem_limit_bytes=vmem_mib * MIB)


def _rms_mod(x, g, shift, scale):
    y = x * lax.rsqrt(jnp.mean(x * x, axis=-1, keepdims=True) + NORM_EPS) * g
    return y * (1 + scale) + shift


def _silu(x):
    return x * jax.nn.sigmoid(x)


def _log_sigmoid(z):
    return jnp.minimum(z, 0.0) - jnp.log1p(jnp.exp(-jnp.abs(z)))


ADA_TN = 1536
ADA_ROWS = 8


def _ada_kernel(c_ref, w_ref, b_ref, o_ref):
    cond = _silu(c_ref[...])
    o_ref[0] = jnp.dot(cond, w_ref[0], precision=HI, preferred_element_type=F32) + b_ref[0]


def ada_modulation(c, w_ada, b_ada):
    c8 = jnp.zeros((ADA_ROWS, D_MODEL), F32).at[:BATCH].set(c)
    width = 6 * D_MODEL
    out = pl.pallas_call(
        _ada_kernel,
        grid=(DEPTH, width // ADA_TN),
        in_specs=[
            pl.BlockSpec((ADA_ROWS, D_MODEL), lambda l, j: (0, 0)),
            pl.BlockSpec((1, D_MODEL, ADA_TN), lambda l, j: (l, 0, j)),
            pl.BlockSpec((1, 1, ADA_TN), lambda l, j: (l, 0, j)),
        ],
        out_specs=pl.BlockSpec((1, ADA_ROWS, ADA_TN), lambda l, j: (l, 0, j)),
        out_shape=jax.ShapeDtypeStruct((DEPTH, ADA_ROWS, width), F32),
        compiler_params=_cparams(("arbitrary", "arbitrary")),
        name="ada_modulation",
    )(c8, w_ada, b_ada.reshape(DEPTH, 1, width))
    return out[:, :BATCH].reshape(DEPTH, BATCH, 6, D_MODEL)


INPROJ_TM = 512
INPROJ_WIDTHS = (2 * GLA_QK_W, GLA_V_W, GLA_V_W, NSA_Q_W, 6 * NSA_KV_W, LANES)


def _arrange_w_in(w_in):
    o = np.cumsum((0, GLA_QK_W, GLA_QK_W, GLA_V_W, GLA_GATE_RANK, GLA_V_W, NSA_Q_W, 6 * NSA_KV_W, NSA_HEADS * 3))
    gq_gk = w_in[:, o[0]:o[2]]
    gv = w_in[:, o[2]:o[3]]
    glr = w_in[:, o[3]:o[4]]
    gr = w_in[:, o[4]:o[5]]
    nq = w_in[:, o[5]:o[6]]
    nkv = w_in[:, o[6]:o[7]]
    ng = w_in[:, o[7]:o[8]]
    pad = jnp.zeros((D_MODEL, LANES - GLA_GATE_RANK - NSA_HEADS * 3), w_in.dtype)
    return jnp.concatenate([gq_gk, gv, gr, nq, nkv, glr, ng, pad], axis=1).astype(BF16)


def _inproj0_kernel(x_ref, g_ref, sh_ref, sc_ref, w_ref, *o_refs):
    h = _rms_mod(x_ref[0], g_ref[...], sh_ref[0], sc_ref[0]).astype(BF16)
    off = 0
    for o_ref, wd in zip(o_refs, INPROJ_WIDTHS):
        o_ref[0] = jnp.dot(h, w_ref[:, off:off + wd], preferred_element_type=F32)
        off += wd


def inproj0(x, g, shift, scale, w_arranged):
    tm = INPROJ_TM
    wtot = sum(INPROJ_WIDTHS)
    row = lambda b, i: (b, i, 0)
    vec = lambda b, i: (b, 0, 0)
    return pl.pallas_call(
        _inproj0_kernel,
        grid=(BATCH, SEQ // tm),
        in_specs=[
            pl.BlockSpec((1, tm, D_MODEL), row),
            pl.BlockSpec((1, D_MODEL), lambda b, i: (0, 0)),
            pl.BlockSpec((1, 1, D_MODEL), vec),
            pl.BlockSpec((1, 1, D_MODEL), vec),
            pl.BlockSpec((D_MODEL, wtot), lambda b, i: (0, 0)),
        ],
        out_specs=[pl.BlockSpec((1, tm, wd), row) for wd in INPROJ_WIDTHS],
        out_shape=[jax.ShapeDtypeStruct((BATCH, SEQ, wd), F32) for wd in INPROJ_WIDTHS],
        compiler_params=_cparams(("arbitrary", "arbitrary")),
        name="inproj0",
    )(x, g.reshape(1, D_MODEL), shift.reshape(BATCH, 1, D_MODEL), scale.reshape(BATCH, 1, D_MODEL), w_arranged)


GLA_TG = 512


def _gla_kernel(qk_ref, v_ref, r_ref, misc_ref, w2_ref, bg_ref, og_ref, o_ref, st_ref, la_ref):
    C = GLA_CHUNK

    @pl.when(pl.program_id(1) == 0)
    def _():
        st_ref[...] = jnp.zeros_like(st_ref)

    z = jnp.dot(misc_ref[0], w2_ref[...], precision=HI, preferred_element_type=F32) + bg_ref[...]
    la_ref[...] = _log_sigmoid(z) / GLA_TAU

    ri = lax.broadcasted_iota(jnp.int32, (C, C), 0)
    ci = lax.broadcasted_iota(jnp.int32, (C, C), 1)
    causal = ri >= ci
    tril = causal.astype(F32)
    lane = lax.broadcasted_iota(jnp.int32, (1, GLA_QK_W), 1)
    og = og_ref[...]

    def chunk(c, carry):
        r0 = pl.multiple_of(c * C, C)
        rows = pl.ds(r0, C)
        bc = jnp.dot(tril, la_ref[rows, :], precision=HI, preferred_element_type=F32)
        b_mid = bc[C // 2:C // 2 + 1, :]
        b_last = bc[C - 1:C, :]
        q = qk_ref[0, rows, 0:GLA_QK_W] * (GLA_DK ** -0.5)
        k = qk_ref[0, rows, GLA_QK_W:2 * GLA_QK_W]
        qd = q * jnp.exp(bc - b_mid)
        kd = (k * jnp.exp(b_mid - bc)).astype(BF16)
        kl = k * jnp.exp(b_last - bc)
        qb = q * jnp.exp(bc)
        st = st_ref[...]
        st_b = st.astype(BF16)
        new_st = st * jnp.exp(b_last)
        for h in range(GLA_HEADS):
            in_head = (lane >= h * GLA_DK) & (lane < (h + 1) * GLA_DK)
            vcols = slice(h * GLA_DV, (h + 1) * GLA_DV)
            s = lax.dot_general(jnp.where(in_head, qd, 0.0).astype(BF16), kd, NT_DIMS, preferred_element_type=F32)
            s = jnp.where(causal, s, 0.0)
            vh = v_ref[0, rows, vcols].astype(BF16)
            o = jnp.dot(s.astype(BF16), vh, preferred_element_type=F32)
            o += lax.dot_general(jnp.where(in_head, qb, 0.0).astype(BF16), st_b, NT_DIMS, preferred_element_type=F32)
            new_st += lax.dot_general(vh, jnp.where(in_head, kl, 0.0).astype(BF16), TN_DIMS, preferred_element_type=F32)
            on = o * lax.rsqrt(jnp.mean(o * o, axis=-1, keepdims=True) + NORM_EPS) * og
            o_ref[0, rows, vcols] = on * _silu(r_ref[0, rows, vcols])
        st_ref[...] = new_st
        return carry

    lax.fori_loop(0, GLA_TG // C, chunk, 0)


def gla_mixer(qk, v, r, misc, w_gate2, b_gate, out_g):
    tg = GLA_TG
    w2 = jnp.zeros((LANES, GLA_QK_W), F32).at[:GLA_GATE_RANK].set(w_gate2)
    row = lambda b, i: (b, i, 0)
    const = lambda b, i: (0, 0)
    return pl.pallas_call(
        _gla_kernel,
        grid=(BATCH, SEQ // tg),
        in_specs=[
            pl.BlockSpec((1, tg, 2 * GLA_QK_W), row),
            pl.BlockSpec((1, tg, GLA_V_W), row),
            pl.BlockSpec((1, tg, GLA_V_W), row),
            pl.BlockSpec((1, tg, LANES), row),
            pl.BlockSpec((LANES, GLA_QK_W), const),
            pl.BlockSpec((1, GLA_QK_W), const),
            pl.BlockSpec((1, GLA_DV), const),
        ],
        out_specs=pl.BlockSpec((1, tg, GLA_V_W), row),
        out_shape=jax.ShapeDtypeStruct((BATCH, SEQ, GLA_V_W), F32),
        scratch_shapes=[pltpu.VMEM((GLA_DV, GLA_QK_W), F32), pltpu.VMEM((tg, GLA_QK_W), F32)],
        compiler_params=_cparams(("arbitrary", "arbitrary")),
        name="gla_mixer",
    )(qk, v, r, misc, w2, b_gate.reshape(1, GLA_QK_W), out_g.reshape(1, GLA_DV))


POS_SIDE = 128


def _rope_table_kernel(freq_ref, pos_ref, cos_ref, sin_ref):
    pos = pos_ref[...].astype(F32)
    for f in range(ROT_HALF):
        ang = pos * freq_ref[f]
        cos_ref[f] = jnp.cos(ang)
        sin_ref[f] = jnp.sin(ang)


def rope_tables(positions):
    inv_freq = jnp.float32(ROPE_THETA) ** (-jnp.arange(ROT_HALF, dtype=F32) / ROT_HALF)
    shp = jax.ShapeDtypeStruct((ROT_HALF, POS_SIDE, POS_SIDE), F32)
    cos, sin = pl.pallas_call(
        _rope_table_kernel,
        in_specs=[pl.BlockSpec(memory_space=pltpu.SMEM), pl.BlockSpec(memory_space=pltpu.VMEM)],
        out_specs=[pl.BlockSpec(memory_space=pltpu.VMEM)] * 2,
        out_shape=[shp, shp],
        name="rope_tables",
    )(inv_freq, positions.reshape(POS_SIDE, POS_SIDE))
    cos = cos.reshape(ROT_HALF, N_TOK).T
    sin = sin.reshape(ROT_HALF, N_TOK).T
    one = jnp.ones((N_TOK, NSA_DH - ROT_DIM), F32)
    zero = jnp.zeros((N_TOK, NSA_DH - ROT_DIM), F32)
    z8 = jnp.zeros((N_TOK, ROT_HALF), F32)
    c64 = jnp.concatenate([cos, cos, one], axis=1)
    sm64 = jnp.concatenate([-sin, z8, zero], axis=1)
    sp64 = jnp.concatenate([z8, sin, zero], axis=1)
    two = lambda t: jnp.concatenate([t, t], axis=1).reshape(BATCH, SEQ, LANES)
    return two(c64), two(sm64), two(sp64)


def _block_diag_ones(width):
    h = np.arange(width) // NSA_DH
    return jnp.asarray((h[:, None] == h[None, :]).astype(np.float32))


def _head_norm_rope(x, gain, bd, c, sm, sp):
    width = x.shape[-1]
    reps = width // LANES
    ss = jnp.dot(x * x, bd, precision=HI, preferred_element_type=F32)
    y = x * lax.rsqrt(ss * (1.0 / NSA_DH) + NORM_EPS) * gain
    tile = lambda t: jnp.concatenate([t] * reps, axis=1) if reps > 1 else t
    return (y * tile(c) + pltpu.roll(y, width - ROT_HALF, 1) * tile(sm) + pltpu.roll(y, ROT_HALF, 1) * tile(sp))


PREP_TM = 512


def _prep_kernel(q_ref, ks_ref, kw_ref, c_ref, sm_ref, sp_ref, gq_ref, gk_ref, bd_ref, qo_ref, kso_ref, kwo_ref):
    c, sm, sp = c_ref[0], sm_ref[0], sp_ref[0]
    bd = bd_ref[...]
    bd1 = bd_ref[0:LANES, 0:LANES]
    qo_ref[0] = _head_norm_rope(q_ref[0], gq_ref[...], bd, c, sm, sp) * (NSA_DH ** -0.5)
    kso_ref[0] = _head_norm_rope(ks_ref[0], gk_ref[0:1, :], bd1, c, sm, sp)
    kwo_ref[0] = _head_norm_rope(kw_ref[0], gk_ref[1:2, :], bd1, c, sm, sp)


def nsa_prep(nq, nkv, tabs, q_gain, k_gain):
    tm = PREP_TM
    row = lambda b, i: (b, i, 0)
    const = lambda b, i: (0, 0)
    gq = jnp.tile(q_gain, NSA_HEADS).reshape(1, NSA_Q_W)
    gk = jnp.stack([jnp.tile(k_gain[1], NSA_KV_GROUPS), jnp.tile(k_gain[2], NSA_KV_GROUPS)])
    return pl.pallas_call(
        _prep_kernel,
        grid=(BATCH, SEQ // tm),
        in_specs=[
            pl.BlockSpec((1, tm, NSA_Q_W), row),
            pl.BlockSpec((1, tm, NSA_KV_W), lambda b, i: (b, i, 2)),
            pl.BlockSpec((1, tm, NSA_KV_W), lambda b, i: (b, i, 4)),
            pl.BlockSpec((1, tm, LANES), row),
            pl.BlockSpec((1, tm, LANES), row),
            pl.BlockSpec((1, tm, LANES), row),
            pl.BlockSpec((1, NSA_Q_W), const),
            pl.BlockSpec((2, NSA_KV_W), const),
            pl.BlockSpec((NSA_Q_W, NSA_Q_W), const),
        ],
        out_specs=[pl.BlockSpec((1, tm, NSA_Q_W), row), pl.BlockSpec((1, tm, NSA_KV_W), row),
                   pl.BlockSpec((1, tm, NSA_KV_W), row)],
        out_shape=[jax.ShapeDtypeStruct((BATCH, SEQ, NSA_Q_W), F32), jax.ShapeDtypeStruct((BATCH, SEQ, NSA_KV_W), F32),
                   jax.ShapeDtypeStruct((BATCH, SEQ, NSA_KV_W), F32)],
        compiler_params=_cparams(("arbitrary", "arbitrary")),
        name="nsa_prep",
    )(nq, nkv, nkv, *tabs, gq, gk, _block_diag_ones(NSA_Q_W))


SEG_W = CMP_STRIDE * NSA_DH


def _cmp_kernel(xk_ref, xv_ref, pe_ref, w1_ref, w2_ref, gain_ref, c_ref, sm_ref, sp_ref, bd_ref, ko_ref, vo_ref):
    def compress(x_ref, kv):
        out = jnp.zeros((N_CMP_PAD, LANES), F32)
        for g in range(NSA_KV_GROUPS):
            x = x_ref[0, g]
            ha = jnp.dot(x + pe_ref[kv, 0], w1_ref[kv, 0:SEG_W, :], precision=HI, preferred_element_type=F32)
            hb = jnp.dot(x + pe_ref[kv, 1], w1_ref[kv, SEG_W:2 * SEG_W, :], precision=HI, preferred_element_type=F32)
            hid = ha + pltpu.roll(hb, N_CMP_PAD - 1, 0)
            out += jnp.dot(jax.nn.gelu(hid), w2_ref[kv, g], precision=HI, preferred_element_type=F32)
        return out

    ko_ref[0] = _head_norm_rope(compress(xk_ref, 0), gain_ref[...], bd_ref[...], c_ref[0], sm_ref[0], sp_ref[0])
    vo_ref[0] = compress(xv_ref, 1)


def nsa_compress(xk, xv, cmp_pe, cmp_w1, cmp_w2, k_gain0, ctabs):
    pe = cmp_pe.reshape(2, 2, 1, SEG_W)
    w2 = jnp.zeros((2, NSA_KV_GROUPS, CMP_HIDDEN, LANES), F32)
    for g in range(NSA_KV_GROUPS):
        w2 = w2.at[:, g, :, g * NSA_DH:(g + 1) * NSA_DH].set(cmp_w2)
    seg = pl.BlockSpec((1, NSA_KV_GROUPS, N_CMP_PAD, SEG_W), lambda b: (b, 0, 0, 0))
    tab = pl.BlockSpec((1, N_CMP_PAD, LANES), lambda b: (b, 0, 0))
    full = lambda shape: pl.BlockSpec(shape, lambda b: (0,) * len(shape))
    return pl.pallas_call(
        _cmp_kernel,
        grid=(BATCH,),
        in_specs=[seg, seg, full((2, 2, 1, SEG_W)), full((2, 2 * SEG_W, CMP_HIDDEN)),
                  full((2, NSA_KV_GROUPS, CMP_HIDDEN, LANES)), full((1, LANES)), tab, tab, tab, full((LANES, LANES))],
        out_specs=[tab, tab],
        out_shape=[jax.ShapeDtypeStruct((BATCH, N_CMP_PAD, LANES), F32)] * 2,
        compiler_params=_cparams(("arbitrary",)),
        name="nsa_compress",
    )(xk, xv, pe, cmp_w1, w2, jnp.tile(k_gain0, NSA_KV_GROUPS).reshape(1, LANES), *ctabs, _block_diag_ones(LANES))


CA_TQ = 256
SUBLANES = 8


CA_COLS = NSA_HPG * CA_TQ
CMP_PER_SEL = SEL_BLOCK // CMP_STRIDE


def split3_keys(k):
    hi = k.astype(BF16)
    lo = (k - hi.astype(F32)).astype(BF16)
    return jnp.concatenate([hi, lo, hi], axis=-1)


def _cattn_kernel(q_ref, kc_ref, vct_ref, gl_ref, o_ref, sel_ref, q3_ref, ps_ref):
    tq = CA_TQ
    q0 = pl.program_id(2) * tq
    lanes4 = lambda t: jnp.concatenate([t] * NSA_HPG, axis=1)
    for h in range(NSA_HPG):
        q = q_ref[0, h]
        hi = q.astype(BF16)
        lo = (q - hi.astype(F32)).astype(BF16)
        for n, part in enumerate((hi, hi, lo)):
            q3_ref[n * NSA_DH:(n + 1) * NSA_DH, h * tq:(h + 1) * tq] = part
    s = jnp.dot(kc_ref[0, 0], q3_ref[...], preferred_element_type=F32)
    cend = lax.broadcasted_iota(jnp.int32, (N_CMP_PAD, tq), 0) * CMP_STRIDE + (CMP_LEN - 1)
    tc = q0 + lax.broadcasted_iota(jnp.int32, (N_CMP_PAD, tq), 1)
    cmask = lanes4(cend <= tc)
    s = jnp.where(cmask, s, NEG_INF)
    m = jnp.max(s, axis=0, keepdims=True)
    e = jnp.where(cmask, jnp.exp(s - m), 0.0)
    l = jnp.sum(e, axis=0, keepdims=True)
    p = e / jnp.where(l > 0.0, l, 1.0)
    gate = jnp.concatenate([jax.nn.sigmoid(gl_ref[0, h, 0:1, :]) for h in range(NSA_HPG)], axis=1)
    o = jnp.dot(vct_ref[0, 0], p.astype(BF16), preferred_element_type=F32) * gate
    for h in range(NSA_HPG):
        o_ref[0, h] = o[:, h * tq:(h + 1) * tq]
    psum = functools.reduce(jnp.add, [p[:, h * tq:(h + 1) * tq] for h in range(NSA_HPG)])
    for n in range(tq // LANES):
        ps_ref[n] = psum[:, n * LANES:(n + 1) * LANES]

    jj = lax.broadcasted_iota(jnp.int32, (N_SEL, tq), 0)
    every4th = lambda r: jnp.concatenate([ps_ref[n, pl.ds(r, N_SEL, stride=CMP_PER_SEL), :] for n in range(tq // LANES)],
                                         axis=1)
    starts_in = [every4th(r) for r in range(CMP_PER_SEL)]
    from_prev = jnp.where(jj >= 1, pltpu.roll(starts_in[CMP_PER_SEL - 1], 1, 0), 0.0)
    imp = functools.reduce(jnp.add, starts_in) + from_prev
    tt = q0 + lax.broadcasted_iota(jnp.int32, (N_SEL, tq), 1)
    cur = jnp.right_shift(tt, 6)
    forced = (jj == 0) | (jj == cur) | (jj == cur - 1)
    valid = jj * SEL_BLOCK <= tt
    score = jnp.where(valid, imp + jnp.where(forced, FORCE_BONUS, 0.0), NEG_INF)

    n_slab = N_SEL // SUBLANES
    sc = [score[SUBLANES * a:SUBLANES * (a + 1)] for a in range(n_slab)]
    rk = [jnp.zeros((SUBLANES, tq), F32) for _ in range(n_slab)]
    sub = lax.broadcasted_iota(jnp.int32, (SUBLANES, tq), 0)
    for jp in range(N_SEL):
        a0, r0 = divmod(jp, SUBLANES)
        row = sc[a0][r0:r0 + 1]
        for a in range(n_slab):
            gt = jnp.where(row > sc[a], 1.0, 0.0)
            ge = jnp.where(row >= sc[a], 1.0, 0.0)
            if a < a0:
                rk[a] = rk[a] + gt
            elif a > a0:
                rk[a] = rk[a] + ge
            else:
                rk[a] = rk[a] + jnp.where(sub > r0, ge, gt)
    for a in range(n_slab):
        rows = slice(SUBLANES * a, SUBLANES * (a + 1))
        sel_ref[0, 0, rows, :] = jnp.where((rk[a] < SEL_TOPK) & valid[rows], 1.0, 0.0)


def nsa_cmp_attn(q_t, kcmp, vcmp_t, gl_t):
    tq = CA_TQ
    return pl.pallas_call(
        _cattn_kernel,
        grid=(BATCH, NSA_KV_GROUPS, SEQ // tq),
        in_specs=[
            pl.BlockSpec((1, NSA_HPG, NSA_DH, tq), lambda b, g, i: (b, g, 0, i)),
            pl.BlockSpec((1, 1, N_CMP_PAD, 3 * NSA_DH), lambda b, g, i: (b, g, 0, 0)),
            pl.BlockSpec((1, 1, NSA_DH, N_CMP_PAD), lambda b, g, i: (b, g, 0, 0)),
            pl.BlockSpec((1, NSA_HPG, 3, tq), lambda b, g, i: (b, g, 0, i)),
        ],
        out_specs=[pl.BlockSpec((1, NSA_HPG, NSA_DH, tq), lambda b, g, i: (b, g, 0, i)),
                   pl.BlockSpec((1, 1, N_SEL, tq), lambda b, g, i: (b, g, 0, i))],
        out_shape=[jax.ShapeDtypeStruct((BATCH, NSA_HEADS, NSA_DH, SEQ), F32),
                   jax.ShapeDtypeStruct((BATCH, NSA_KV_GROUPS, N_SEL, SEQ), F32)],
        scratch_shapes=[pltpu.VMEM((3 * NSA_DH, CA_COLS), BF16), pltpu.VMEM((tq // LANES, N_CMP_PAD, LANES), F32)],
        compiler_params=_cparams(("arbitrary", "arbitrary", "arbitrary")),
        name="nsa_cmp_attn",
    )(q_t, split3_keys(kcmp), vcmp_t.astype(BF16), gl_t)


SA_TQ = 256
SA_TK = 512
M_INIT = -1e20


SA_COLS = NSA_HPG * SA_TQ
SA_BLOCKS = SA_TK // SEL_BLOCK
SA_MASK_ROWS = 16


def sel_key_slab(ksel):
    blk = (np.arange(SEQ) % SA_TK) // SEL_BLOCK
    onehot = (blk[:, None] == np.arange(LANES - NSA_DH)[None, :]).astype(np.float32)
    onehot = jnp.broadcast_to(jnp.asarray(onehot, BF16), ksel.shape[:3] + (LANES - NSA_DH,))
    return jnp.concatenate([ksel.astype(BF16), onehot], axis=-1)


def _sattn_kernel(q_ref, k_ref, vt_ref, sel_ref, gl_ref, prev_ref, o_ref, qa_ref, acc_ref):
    tq, tk = SA_TQ, SA_TK
    i = pl.program_id(2)
    for h in range(NSA_HPG):
        qa_ref[0:NSA_DH, h * tq:(h + 1) * tq] = q_ref[0, h]
    qa_ref[NSA_DH:LANES, :] = jnp.zeros((LANES - NSA_DH, SA_COLS), BF16)
    acc_ref[...] = jnp.zeros_like(acc_ref)
    lanes4 = lambda t: jnp.concatenate([t] * NSA_HPG, axis=1)

    def key_tile(kt, m_prev, l_prev, diagonal):
        selrows = sel_ref[0, 0, pl.ds(pl.multiple_of(kt * SA_BLOCKS, SA_BLOCKS), SA_BLOCKS), :]
        bias = jnp.where(selrows > 0.5, 0.0, NEG_INF)
        bias = jnp.concatenate([bias, jnp.zeros((SA_MASK_ROWS - SA_BLOCKS, tq), F32)], axis=0)
        qa_ref[NSA_DH:NSA_DH + SA_MASK_ROWS, :] = lanes4(bias).astype(BF16)
        keys = pl.ds(pl.multiple_of(kt * tk, tk), tk)
        s = jnp.dot(k_ref[0, 0, keys, :], qa_ref[...], preferred_element_type=F32)
        if diagonal:
            kpos = kt * tk + lax.broadcasted_iota(jnp.int32, (tk, tq), 0)
            tt = i * tq + lax.broadcasted_iota(jnp.int32, (tk, tq), 1)
            s = s + lanes4(jnp.where(kpos <= tt, 0.0, NEG_INF))
        m_new = jnp.maximum(m_prev, jnp.max(s, axis=0, keepdims=True))
        alpha = jnp.exp(m_prev - m_new)
        p = jnp.exp(s - m_new)
        l_new = alpha * l_prev + jnp.sum(p, axis=0, keepdims=True)
        acc_ref[...] = alpha * acc_ref[...] + jnp.dot(vt_ref[0, 0, :, keys], p.astype(BF16), preferred_element_type=F32)
        return m_new, l_new

    n_full = (i * tq) // tk
    m0 = jnp.full((1, SA_COLS), M_INIT, F32)
    l0 = jnp.zeros((1, SA_COLS), F32)
    m, l = lax.fori_loop(0, n_full, lambda kt, c: key_tile(kt, c[0], c[1], False), (m0, l0))
    m, l = key_tile(n_full, m, l, True)
    gate = jnp.concatenate([jax.nn.sigmoid(gl_ref[0, h, 1:2, :]) for h in range(NSA_HPG)], axis=1)
    out = acc_ref[...] / l * gate
    for h in range(NSA_HPG):
        o_ref[0, h] = prev_ref[0, h] + out[:, h * tq:(h + 1) * tq]


def nsa_sel_attn(q_t, k_slab, vsel_t, sel_t, gl_t, prev):
    tq = SA_TQ
    ospec = pl.BlockSpec((1, NSA_HPG, NSA_DH, tq), lambda b, g, i: (b, g, 0, i))
    return pl.pallas_call(
        _sattn_kernel,
        grid=(BATCH, NSA_KV_GROUPS, SEQ // tq),
        in_specs=[
            ospec,
            pl.BlockSpec((1, 1, SEQ, LANES), lambda b, g, i: (b, g, 0, 0)),
            pl.BlockSpec((1, 1, NSA_DH, SEQ), lambda b, g, i: (b, g, 0, 0)),
            pl.BlockSpec((1, 1, N_SEL, tq), lambda b, g, i: (b, g, 0, i)),
            pl.BlockSpec((1, NSA_HPG, 3, tq), lambda b, g, i: (b, g, 0, i)),
            ospec,
        ],
        out_specs=ospec,
        out_shape=jax.ShapeDtypeStruct((BATCH, NSA_HEADS, NSA_DH, SEQ), F32),
        scratch_shapes=[pltpu.VMEM((LANES, SA_COLS), BF16), pltpu.VMEM((NSA_DH, SA_COLS), F32)],
        input_output_aliases={5: 0},
        compiler_params=_cparams(("arbitrary", "arbitrary", "arbitrary")),
        name="nsa_sel_attn",
    )(q_t, k_slab, vsel_t, sel_t, gl_t, prev)


WA_TQ = 256
WA_TILES = WINDOW // WA_TQ + 1


def _window_bias():
    kl = np.arange(WA_TILES * WA_TQ)[:, None]
    ql = np.arange(WA_TQ)[None, :]
    diff = ql - kl + WINDOW
    return jnp.asarray(np.where((diff >= 0) & (diff < WINDOW), 0.0, NEG_INF).astype(np.float32))


def _wattn_kernel(q_ref, k0_ref, k1_ref, k2_ref, v0_ref, v1_ref, v2_ref, bias_ref, gl_ref, prev_ref, o_ref):
    tq = WA_TQ
    i = pl.program_id(2)
    k_refs = (k0_ref, k1_ref, k2_ref)
    v_refs = (v0_ref, v1_ref, v2_ref)
    lanes4 = lambda t: jnp.concatenate([t] * NSA_HPG, axis=1)
    q = jnp.concatenate([q_ref[0, h] for h in range(NSA_HPG)], axis=1)
    ss = []
    for d in range(WA_TILES):
        in_seq = i - (WA_TILES - 1) + d >= 0
        bias = jnp.where(in_seq, bias_ref[d * tq:(d + 1) * tq, :], NEG_INF)
        ss.append(jnp.dot(k_refs[d][0, 0], q, preferred_element_type=F32) + lanes4(bias))
    m = functools.reduce(jnp.maximum, [jnp.max(s, axis=0, keepdims=True) for s in ss])
    ps = [jnp.exp(s - m) for s in ss]
    l = functools.reduce(jnp.add, [jnp.sum(p, axis=0, keepdims=True) for p in ps])
    acc = functools.reduce(jnp.add, [jnp.dot(v_refs[d][0, 0], ps[d].astype(BF16), preferred_element_type=F32)
                                     for d in range(WA_TILES)])
    gate = jnp.concatenate([jax.nn.sigmoid(gl_ref[0, h, 2:3, :]) for h in range(NSA_HPG)], axis=1)
    out = acc / l * gate
    for h in range(NSA_HPG):
        o_ref[0, h] = prev_ref[0, h] + out[:, h * tq:(h + 1) * tq]


def nsa_win_attn(q_t, kwin, vwin_t, gl_t, prev):
    tq = WA_TQ
    qspec = pl.BlockSpec((1, NSA_HPG, NSA_DH, tq), lambda b, g, i: (b, g, 0, i))
    tile = lambda d: (lambda i: jnp.maximum(i - (WA_TILES - 1) + d, 0))
    kspec = lambda d: pl.BlockSpec((1, 1, tq, NSA_DH), lambda b, g, i: (b, g, tile(d)(i), 0))
    vspec = lambda d: pl.BlockSpec((1, 1, NSA_DH, tq), lambda b, g, i: (b, g, 0, tile(d)(i)))
    return pl.pallas_call(
        _wattn_kernel,
        grid=(BATCH, NSA_KV_GROUPS, SEQ // tq),
        in_specs=[qspec] + [kspec(d) for d in range(WA_TILES)] + [vspec(d) for d in range(WA_TILES)] + [
            pl.BlockSpec((WA_TILES * tq, tq), lambda b, g, i: (0, 0)),
            pl.BlockSpec((1, NSA_HPG, 3, tq), lambda b, g, i: (b, g, 0, i)),
            qspec,
        ],
        out_specs=qspec,
        out_shape=jax.ShapeDtypeStruct((BATCH, NSA_HEADS, NSA_DH, SEQ), F32),
        input_output_aliases={2 * WA_TILES + 3: 0},
        compiler_params=_cparams(("arbitrary", "arbitrary", "arbitrary")),
        name="nsa_win_attn",
    )(q_t, *([kwin] * WA_TILES), *([vwin_t] * WA_TILES), _window_bias(), gl_t, prev)


def nsa_mixer(nq, nkv, misc, positions, q_gain, k_gain, cmp_pe, cmp_w1, cmp_w2):
    tabs = rope_tables(positions)
    q_r, ks_r, kw_r = nsa_prep(nq, nkv, tabs, q_gain, k_gain)
    group_major = lambda t: t.reshape(BATCH, SEQ, NSA_KV_GROUPS, NSA_DH).transpose(0, 2, 1, 3)
    group_major_t = lambda t: t.reshape(BATCH, SEQ, NSA_KV_GROUPS, NSA_DH).transpose(0, 2, 3, 1)
    col = lambda n: nkv[..., n * NSA_KV_W:(n + 1) * NSA_KV_W]
    segs = lambda t: group_major(t).reshape(BATCH, NSA_KV_GROUPS, N_CMP_PAD, SEG_W)
    last = jnp.minimum(jnp.arange(N_CMP_PAD) * CMP_STRIDE + CMP_LEN - 1, SEQ - 1)
    ctabs = tuple(t[:, last] for t in tabs)
    kcmp, vcmp = nsa_compress(segs(col(0)), segs(col(1)), cmp_pe, cmp_w1, cmp_w2, k_gain[0], ctabs)
    kcmp = kcmp.reshape(BATCH, N_CMP_PAD, NSA_KV_GROUPS, NSA_DH).transpose(0, 2, 1, 3)
    vcmp_t = vcmp.reshape(BATCH, N_CMP_PAD, NSA_KV_GROUPS, NSA_DH).transpose(0, 2, 3, 1)
    q_t = q_r.reshape(BATCH, SEQ, NSA_HEADS, NSA_DH).transpose(0, 2, 3, 1)
    gl_t = misc[..., GLA_GATE_RANK:GLA_GATE_RANK + NSA_HEADS * 3].reshape(BATCH, SEQ, NSA_HEADS, 3).transpose(0, 2, 3, 1)
    o_t, sel_t = nsa_cmp_attn(q_t, kcmp, vcmp_t, gl_t)
    q_t16 = q_t.astype(BF16)
    o_t = nsa_sel_attn(q_t16, sel_key_slab(group_major(ks_r)), group_major_t(col(3)).astype(BF16), sel_t, gl_t, o_t)
    o_t = nsa_win_attn(q_t16, group_major(kw_r).astype(BF16), group_major_t(col(5)).astype(BF16), gl_t, o_t)
    return o_t.transpose(0, 3, 1, 2).reshape(BATCH, SEQ, NSA_Q_W)


ROUTE_ROWS = 8
HX_W = D_MODEL + 3 * LANES


def _top2_sum(a, b, c, d):
    hi1, lo1 = jnp.maximum(a, b), jnp.minimum(a, b)
    hi2, lo2 = jnp.maximum(c, d), jnp.minimum(c, d)
    return jnp.maximum(hi1, hi2) + jnp.maximum(jnp.minimum(hi1, hi2), jnp.maximum(lo1, lo2))


def _moe_prenorm_route(xn, g_ref, sh_ref, sc_ref, wr_ref, rb_ref, hx_ref, route_ref):
    h = _rms_mod(xn, g_ref[...], sh_ref[0], sc_ref[0])
    logits = lax.dot_general(wr_ref[...], h, NT_DIMS, precision=HI, preferred_element_type=F32)
    scores = jax.nn.sigmoid(logits)
    sel = scores + rb_ref[...]
    epg = EXPERTS_PER_GROUP
    srow = lambda e: sel[e:e + 1, :]
    grp = [_top2_sum(*[srow(epg * g + r) for r in range(epg)]) for g in range(N_EXPERT_GROUPS)]
    best, gi = grp[0], jnp.zeros_like(grp[0], dtype=jnp.int32)
    for g in range(1, N_EXPERT_GROUPS):
        better = grp[g] > best
        gi = jnp.where(better, g, gi)
        best = jnp.where(better, grp[g], best)

    def in_group(mat, r):
        out = mat[r:r + 1, :]
        for g in range(1, N_EXPERT_GROUPS):
            out = jnp.where(gi == g, mat[epg * g + r:epg * g + r + 1, :], out)
        return out

    v = [in_group(sel, r) for r in range(epg)]
    sc = [in_group(scores, r) for r in range(epg)]
    b1, i1, w1 = v[0], jnp.zeros_like(gi), sc[0]
    for r in range(1, epg):
        better = v[r] > b1
        i1 = jnp.where(better, r, i1)
        w1 = jnp.where(better, sc[r], w1)
        b1 = jnp.where(better, v[r], b1)
    b2 = jnp.full_like(b1, -3e38)
    i2, w2 = jnp.zeros_like(gi), jnp.zeros_like(w1)
    for r in range(epg):
        better = (i1 != r) & (v[r] > b2)
        i2 = jnp.where(better, r, i2)
        w2 = jnp.where(better, sc[r], w2)
        b2 = jnp.where(better, v[r], b2)
    tot = w1 + w2
    w1, w2 = w1 / tot, w2 / tot
    zero = jnp.zeros_like(w1)
    route_ref[0] = jnp.concatenate([gi.astype(F32)] + [zero] * (ROUTE_ROWS - 1), axis=0)
    wrows = [jnp.where(i1 == r, w1, jnp.where(i2 == r, w2, 0.0)) for r in range(epg)]
    wmat = jnp.concatenate(wrows + [jnp.zeros((LANES - epg, w1.shape[1]), F32)], axis=0).T
    w_hi = wmat.astype(BF16)
    rest = wmat - w_hi.astype(F32)
    w_mid = rest.astype(BF16)
    w_lo = (rest - w_mid.astype(F32)).astype(BF16)
    hx_ref[0, :, 0:D_MODEL] = h.astype(BF16)
    for n, part in enumerate((w_hi, w_mid, w_lo)):
        hx_ref[0, :, D_MODEL + n * LANES:D_MODEL + (n + 1) * LANES] = part


def _route_specs(tm, row, vec, const):
    in_specs = [pl.BlockSpec((1, D_MODEL), const), pl.BlockSpec((1, 1, D_MODEL), vec), pl.BlockSpec((1, 1, D_MODEL), vec),
                pl.BlockSpec((N_EXPERTS, D_MODEL), const), pl.BlockSpec((N_EXPERTS, 1), const)]
    out_specs = [pl.BlockSpec((1, tm, HX_W), row), pl.BlockSpec((1, ROUTE_ROWS, tm), lambda b, i: (b, 0, i))]
    out_shape = [jax.ShapeDtypeStruct((BATCH, SEQ, HX_W), BF16), jax.ShapeDtypeStruct((BATCH, ROUTE_ROWS, SEQ), F32)]
    return in_specs, out_specs, out_shape


def _route_args(g, shift, scale, w_router, router_bias):
    return (g.reshape(1, D_MODEL), shift.reshape(BATCH, 1, D_MODEL), scale.reshape(BATCH, 1, D_MODEL),
            w_router.T, router_bias.reshape(N_EXPERTS, 1))


OUTPROJ_TM = 512


def _outproj0_kernel(oa_ref, ob_ref, w_ref, x_ref, gate_ref, g_ref, sh_ref, sc_ref, wr_ref, rb_ref,
                     xo_ref, h_ref, route_ref):
    y = jnp.dot(oa_ref[0].astype(BF16), w_ref[0:GLA_V_W, :], preferred_element_type=F32)
    y += jnp.dot(ob_ref[0].astype(BF16), w_ref[GLA_V_W:GLA_V_W + NSA_Q_W, :], preferred_element_type=F32)
    xn = x_ref[0] + gate_ref[0] * y
    xo_ref[0] = xn
    _moe_prenorm_route(xn, g_ref, sh_ref, sc_ref, wr_ref, rb_ref, h_ref, route_ref)


def outproj0(o_a, o_b, w_out, x, gate, route_args):
    tm = OUTPROJ_TM
    row = lambda b, i: (b, i, 0)
    vec = lambda b, i: (b, 0, 0)
    const = lambda b, i: (0, 0)
    r_in, r_out, r_shape = _route_specs(tm, row, vec, const)
    return pl.pallas_call(
        _outproj0_kernel,
        grid=(BATCH, SEQ // tm),
        in_specs=[pl.BlockSpec((1, tm, GLA_V_W), row), pl.BlockSpec((1, tm, NSA_Q_W), row),
                  pl.BlockSpec((GLA_V_W + NSA_Q_W, D_MODEL), const), pl.BlockSpec((1, tm, D_MODEL), row),
                  pl.BlockSpec((1, 1, D_MODEL), vec)] + r_in,
        out_specs=[pl.BlockSpec((1, tm, D_MODEL), row)] + r_out,
        out_shape=[jax.ShapeDtypeStruct((BATCH, SEQ, D_MODEL), F32)] + r_shape,
        compiler_params=_cparams(("arbitrary", "arbitrary")),
        name="outproj0",
    )(o_a, o_b, w_out.astype(BF16), x, gate.reshape(BATCH, 1, D_MODEL), *route_args)


GMLP_TM = 256


def _gmlp_kernel(x_ref, g1_ref, sh1_ref, sc1_ref, win_ref, ng_ref, ws_ref, bs_ref, wout_ref, gate_ref,
                 g_ref, sh_ref, sc_ref, wr_ref, rb_ref, xo_ref, h_ref, route_ref, gated_ref):
    x = x_ref[0]
    h = _rms_mod(x, g1_ref[...], sh1_ref[0], sc1_ref[0]).astype(BF16)
    u = jax.nn.gelu(jnp.dot(h, win_ref[:, 0:SGU_WIDTH], preferred_element_type=F32))
    v = jax.nn.gelu(jnp.dot(h, win_ref[:, SGU_WIDTH:2 * SGU_WIDTH], preferred_element_type=F32))
    v = (v * lax.rsqrt(jnp.mean(v * v, axis=-1, keepdims=True) + NORM_EPS) * ng_ref[...]).astype(BF16)
    ri = lax.broadcasted_iota(jnp.int32, (SGU_CHUNK, SGU_CHUNK), 0)
    ci = lax.broadcasted_iota(jnp.int32, (SGU_CHUNK, SGU_CHUNK), 1)
    for g in range(SGU_GROUPS):
        w = jnp.where(ri >= ci, ws_ref[g], 0.0).astype(BF16)
        cols = slice(g * SGU_GROUP_DIM, (g + 1) * SGU_GROUP_DIM)
        for c in range(GMLP_TM // SGU_CHUNK):
            rows = slice(c * SGU_CHUNK, (c + 1) * SGU_CHUNK)
            mix = jnp.dot(w, v[rows, cols], preferred_element_type=F32) + bs_ref[:, g:g + 1]
            gated_ref[rows, cols] = (u[rows, cols] * mix).astype(BF16)
    y = jnp.dot(gated_ref[...], wout_ref[...], preferred_element_type=F32)
    xn = x + gate_ref[0] * y
    xo_ref[0] = xn
    _moe_prenorm_route(xn, g_ref, sh_ref, sc_ref, wr_ref, rb_ref, h_ref, route_ref)


def gmlp_layer(x, g1, shift1, scale1, w_in, norm_g, w_s, b_s, w_out, gate, route_args):
    tm = GMLP_TM
    row = lambda b, i: (b, i, 0)
    vec = lambda b, i: (b, 0, 0)
    const = lambda b, i: (0, 0)
    r_in, r_out, r_shape = _route_specs(tm, row, vec, const)
    vspec = pl.BlockSpec((1, 1, D_MODEL), vec)
    return pl.pallas_call(
        _gmlp_kernel,
        grid=(BATCH, SEQ // tm),
        in_specs=[pl.BlockSpec((1, tm, D_MODEL), row), pl.BlockSpec((1, D_MODEL), const), vspec, vspec,
                  pl.BlockSpec((D_MODEL, 2 * SGU_WIDTH), const), pl.BlockSpec((1, SGU_WIDTH), const),
                  pl.BlockSpec((SGU_GROUPS, SGU_CHUNK, SGU_CHUNK), lambda b, i: (0, 0, 0)),
                  pl.BlockSpec((SGU_CHUNK, SGU_GROUPS), const), pl.BlockSpec((SGU_WIDTH, D_MODEL), const), vspec] + r_in,
        out_specs=[pl.BlockSpec((1, tm, D_MODEL), row)] + r_out,
        out_shape=[jax.ShapeDtypeStruct((BATCH, SEQ, D_MODEL), F32)] + r_shape,
        scratch_shapes=[pltpu.VMEM((tm, SGU_WIDTH), BF16)],
        compiler_params=_cparams(("arbitrary", "arbitrary"), vmem_mib=56),
        name="gmlp_layer",
    )(x, g1.reshape(1, D_MODEL), shift1.reshape(BATCH, 1, D_MODEL), scale1.reshape(BATCH, 1, D_MODEL),
      w_in.astype(BF16), norm_g.reshape(1, SGU_WIDTH), w_s, b_s.T, w_out.astype(BF16),
      gate.reshape(BATCH, 1, D_MODEL), *route_args)


MOE_TM = 256
MOE_CHUNK = 512
MOE_SORTED = N_TOK + N_EXPERT_GROUPS * MOE_TM
MOE_TILES = MOE_SORTED // MOE_TM
MOE_CHUNKS = N_TOK // MOE_CHUNK
MOE_PAIRS = MOE_TILES + N_EXPERT_GROUPS * MOE_CHUNKS
FLAG_ACTIVE, FLAG_FIRST, FLAG_LAST, FLAG_ZERO = 1, 2, 4, 8


def _plan_kernel(gi_ref, rank_ref, before_ref):
    gi = gi_ref[...]
    r = lax.broadcasted_iota(jnp.int32, (POS_SIDE, POS_SIDE), 0)
    c = lax.broadcasted_iota(jnp.int32, (POS_SIDE, POS_SIDE), 1)
    upper = jnp.where(r <= c, 1.0, 0.0)
    lower_strict = jnp.where(c < r, 1.0, 0.0)
    rank = jnp.zeros((POS_SIDE, POS_SIDE), F32)
    for g in range(N_EXPERT_GROUPS):
        member = jnp.where(gi == g, 1.0, 0.0)
        in_row = jnp.dot(member, upper, precision=HI, preferred_element_type=F32)
        row_total = jnp.broadcast_to(in_row[:, POS_SIDE - 1:POS_SIDE], (POS_SIDE, POS_SIDE))
        before = jnp.dot(lower_strict, row_total, precision=HI, preferred_element_type=F32)
        before_ref[g] = before
        rank += member * (before + in_row - 1.0)
    rank_ref[...] = rank


def moe_plan(route):
    tm = MOE_TM
    i32 = jnp.int32
    gi_f = route[:, 0, :].reshape(POS_SIDE, POS_SIDE)
    rank, before = pl.pallas_call(
        _plan_kernel,
        out_shape=[jax.ShapeDtypeStruct((POS_SIDE, POS_SIDE), F32),
                   jax.ShapeDtypeStruct((N_EXPERT_GROUPS, POS_SIDE, POS_SIDE), F32)],
        name="moe_plan",
    )(gi_f)
    gi = gi_f.reshape(N_TOK).astype(i32)
    groups = jnp.arange(N_EXPERT_GROUPS, dtype=i32)
    member = gi[None, :] == groups[:, None]
    tot = jnp.sum(member, axis=1).astype(i32)
    padded = (tot + tm - 1) // tm * tm
    gend = jnp.cumsum(padded).astype(i32)
    gstart = gend - padded
    pos = jnp.sum(jnp.where(member, gstart[:, None], 0), axis=0).astype(i32) + rank.reshape(N_TOK).astype(i32)
    rows_per_chunk = MOE_CHUNK // POS_SIDE
    cnt_end = jnp.concatenate([before[:, rows_per_chunk::rows_per_chunk, 0].astype(i32), tot[:, None]], axis=1)
    t = jnp.arange(MOE_TILES, dtype=i32)
    n_used = gend[-1] // tm
    tile_g = jnp.minimum(jnp.sum(gend[None, :] <= (t * tm)[:, None], axis=1), N_EXPERT_GROUPS - 1).astype(i32)
    k0 = t * tm - gstart[tile_g]
    k1 = jnp.minimum(k0 + tm, tot[tile_g]) - 1
    ce = cnt_end[tile_g]
    c_lo = jnp.sum(ce <= k0[:, None], axis=1).astype(i32)
    c_hi = jnp.sum(ce <= k1[:, None], axis=1).astype(i32)
    npairs = jnp.where(t < n_used, c_hi - c_lo + 1, 0)
    pend = jnp.cumsum(npairs).astype(i32)
    pstart = pend - npairs
    total = pend[-1]
    l = jnp.arange(MOE_PAIRS, dtype=i32)
    real = l < total
    lt = jnp.minimum(l, total - 1)
    tile_l = jnp.sum(pend[None, :] <= lt[:, None], axis=1).astype(i32)
    chunk_l = c_lo[tile_l] + lt - pstart[tile_l]
    spare_tile = jnp.minimum(n_used + (l - total), MOE_TILES - 1)
    flags = jnp.where(real, FLAG_ACTIVE + jnp.where(lt == pstart[tile_l], FLAG_FIRST, 0)
                      + jnp.where(lt == pend[tile_l] - 1, FLAG_LAST, 0),
                      jnp.where(spare_tile >= n_used, FLAG_ZERO, 0)).astype(i32)
    tile_sched = jnp.where(real, tile_l, spare_tile).astype(i32)
    by_tile = (tile_sched, chunk_l.astype(i32), flags, tile_g[tile_sched])
    cc = jnp.arange(MOE_CHUNKS, dtype=i32)
    is_pair = (cc[:, None] >= c_lo[None, :]) & (cc[:, None] <= c_hi[None, :]) & (t[None, :] < n_used)
    seen = jnp.cumsum(is_pair.reshape(-1).astype(i32))
    flat = jnp.sum(seen[None, :] <= lt[:, None], axis=1).astype(i32)
    chunk_c, tile_c = flat // MOE_TILES, flat % MOE_TILES
    prev_c = jnp.concatenate([jnp.full((1,), -1, i32), chunk_c[:-1]])
    next_c = jnp.concatenate([chunk_c[1:], jnp.full((1,), -1, i32)])
    flags_c = jnp.where(real, FLAG_ACTIVE + jnp.where(chunk_c != prev_c, FLAG_FIRST, 0)
                        + jnp.where((chunk_c != next_c) | (l == total - 1), FLAG_LAST, 0), 0).astype(i32)
    by_chunk = (tile_c, chunk_c, flags_c)
    return pos.reshape(MOE_CHUNKS, 1, MOE_CHUNK), by_tile, by_chunk


def _one_hot_rows(pos_row, tile):
    rows = tile * MOE_TM + lax.broadcasted_iota(jnp.int32, (MOE_TM, MOE_CHUNK), 0)
    return jnp.where(pos_row == rows, 1.0, 0.0).astype(BF16)


def _moe_kernel(tile_ref, chunk_ref, flag_ref, grp_ref, pos_ref, hx_ref, wg_ref, wu_ref, wd_ref, y_ref, acc_ref):
    l = pl.program_id(0)
    flags = flag_ref[l]

    @pl.when((flags & FLAG_FIRST) != 0)
    def _():
        acc_ref[...] = jnp.zeros_like(acc_ref)

    @pl.when((flags & FLAG_ACTIVE) != 0)
    def _():
        acc_ref[...] += jnp.dot(_one_hot_rows(pos_ref[0], tile_ref[l]), hx_ref[0], preferred_element_type=F32)

    @pl.when((flags & FLAG_LAST) != 0)
    def _():
        x = acc_ref[:, 0:D_MODEL].astype(BF16)
        w = functools.reduce(jnp.add, [acc_ref[:, D_MODEL + n * LANES:D_MODEL + (n + 1) * LANES] for n in range(3)])
        y = jnp.zeros((MOE_TM, D_MODEL), F32)
        for r in range(EXPERTS_PER_GROUP):
            gate = jnp.dot(x, wg_ref[0, 0, r], preferred_element_type=F32)
            up = jnp.dot(x, wu_ref[0, 0, r], preferred_element_type=F32)
            hid = (_silu(gate) * up * w[:, r:r + 1]).astype(BF16)
            y += jnp.dot(hid, wd_ref[0, 0, r], preferred_element_type=F32)
        y_ref[...] = y

    @pl.when((flags & FLAG_ZERO) != 0)
    def _():
        y_ref[...] = jnp.zeros_like(y_ref)


def moe_experts(hx, pos, by_tile, w_gate, w_up, w_down, layer):
    grouped = lambda w: w.reshape(DEPTH, N_EXPERT_GROUPS, EXPERTS_PER_GROUP, *w.shape[2:])
    wspec = lambda k, n: pl.BlockSpec((1, 1, EXPERTS_PER_GROUP, k, n), lambda l, t, c, f, g: (layer, g[l], 0, 0, 0))
    return pl.pallas_call(
        _moe_kernel,
        grid_spec=pltpu.PrefetchScalarGridSpec(
            num_scalar_prefetch=4,
            grid=(MOE_PAIRS,),
            in_specs=[pl.BlockSpec((1, 1, MOE_CHUNK), lambda l, t, c, f, g: (c[l], 0, 0)),
                      pl.BlockSpec((1, MOE_CHUNK, HX_W), lambda l, t, c, f, g: (c[l], 0, 0)),
                      wspec(D_MODEL, EXPERT_HIDDEN), wspec(D_MODEL, EXPERT_HIDDEN), wspec(EXPERT_HIDDEN, D_MODEL)],
            out_specs=pl.BlockSpec((MOE_TM, D_MODEL), lambda l, t, c, f, g: (t[l], 0)),
            scratch_shapes=[pltpu.VMEM((MOE_TM, HX_W), F32)],
        ),
        out_shape=jax.ShapeDtypeStruct((MOE_SORTED, D_MODEL), F32),
        compiler_params=_cparams(("arbitrary",), vmem_mib=56),
        name="moe_experts",
    )(*by_tile, pos, hx.reshape(MOE_CHUNKS, MOE_CHUNK, HX_W), grouped(w_gate), grouped(w_up), grouped(w_down))


def _moe_combine_kernel(tile_ref, chunk_ref, flag_ref, pos_ref, y_ref, x_ref, gate_ref, o_ref, acc_ref):
    l = pl.program_id(0)
    flags = flag_ref[l]

    @pl.when((flags & FLAG_FIRST) != 0)
    def _():
        acc_ref[...] = jnp.zeros_like(acc_ref)

    @pl.when((flags & FLAG_ACTIVE) != 0)
    def _():
        onehot = _one_hot_rows(pos_ref[0], tile_ref[l])
        y = y_ref[...]
        y_hi = y.astype(BF16)
        y_lo = (y - y_hi.astype(F32)).astype(BF16)
        acc_ref[...] += (lax.dot_general(onehot, y_hi, TN_DIMS, preferred_element_type=F32)
                         + lax.dot_general(onehot, y_lo, TN_DIMS, preferred_element_type=F32))

    @pl.when((flags & FLAG_LAST) != 0)
    def _():
        o_ref[0] = x_ref[0] + gate_ref[0] * acc_ref[...]


def moe_combine(x, y_sorted, pos, by_chunk, gate):
    per_b = SEQ // MOE_CHUNK
    tok = lambda l, t, c, f: (c[l] // per_b, c[l] % per_b, 0)
    return pl.pallas_call(
        _moe_combine_kernel,
        grid_spec=pltpu.PrefetchScalarGridSpec(
            num_scalar_prefetch=3,
            grid=(MOE_PAIRS,),
            in_specs=[pl.BlockSpec((1, 1, MOE_CHUNK), lambda l, t, c, f: (c[l], 0, 0)),
                      pl.BlockSpec((MOE_TM, D_MODEL), lambda l, t, c, f: (t[l], 0)),
                      pl.BlockSpec((1, MOE_CHUNK, D_MODEL), tok),
                      pl.BlockSpec((1, 1, D_MODEL), lambda l, t, c, f: (c[l] // per_b, 0, 0))],
            out_specs=pl.BlockSpec((1, MOE_CHUNK, D_MODEL), tok),
            scratch_shapes=[pltpu.VMEM((MOE_CHUNK, D_MODEL), F32)],
        ),
        out_shape=jax.ShapeDtypeStruct((BATCH, SEQ, D_MODEL), F32),
        compiler_params=_cparams(("arbitrary",)),
        name="moe_combine",
    )(*by_chunk, pos, y_sorted, x, gate.reshape(BATCH, 1, D_MODEL))


def moe_layer(x, hx, route, gate, w_gate, w_up, w_down, layer):
    pos, by_tile, by_chunk = moe_plan(route)
    y_sorted = moe_experts(hx, pos, by_tile, w_gate, w_up, w_down, layer)
    return moe_combine(x, y_sorted, pos, by_chunk, gate)


def kernel(x, c, positions, w_ada, b_ada, norm_g, w_in_ab, w_out_ab, gla_w_gate2, gla_b_gate, gla_norm_g, nsa_q_gain, nsa_k_gain, nsa_cmp_pe, nsa_cmp_w1, nsa_cmp_w2, w_in_c, sgu_norm_g, sgu_w_s, sgu_b_s, w_out_c, w_router, router_bias, w_gate, w_up, w_down):
    mod = ada_modulation(c, w_ada, b_ada)
    qk, gv, gr, nq, nkv, misc = inproj0(x, norm_g[0, 0], mod[0, :, 0], mod[0, :, 1], _arrange_w_in(w_in_ab[0]))
    o_a = gla_mixer(qk, gv, gr, misc, gla_w_gate2[0], gla_b_gate[0], gla_norm_g[0])
    o_b = nsa_mixer(nq, nkv, misc, positions, nsa_q_gain[0], nsa_k_gain[0], nsa_cmp_pe[0], nsa_cmp_w1[0], nsa_cmp_w2[0])
    wg, wu, wd = w_gate.astype(BF16), w_up.astype(BF16), w_down.astype(BF16)
    route_args = lambda l: _route_args(norm_g[l, 1], mod[l, :, 3], mod[l, :, 4], w_router, router_bias)
    x1, h, route = outproj0(o_a, o_b, w_out_ab[0], x, mod[0, :, 2], route_args(0))
    x2 = moe_layer(x1, h, route, mod[0, :, 5], wg, wu, wd, 0)
    x3, h, route = gmlp_layer(x2, norm_g[1, 0], mod[1, :, 0], mod[1, :, 1], w_in_c[0], sgu_norm_g[0], sgu_w_s[0],
                              sgu_b_s[0], w_out_c[0], mod[1, :, 2], route_args(1))
    return moe_layer(x3, h, route, mod[1, :, 5], wg, wu, wd, 1)
```

```python
import functools

import numpy as np
import jax
import jax.numpy as jnp
from jax import lax
from jax.experimental import pallas as pl
from jax.experimental.pallas import tpu as pltpu

D_MODEL = 1024
BATCH = 2
SEQ = 8192
DEPTH = 2
N_TOK = BATCH * SEQ

GLA_HEADS = 4
GLA_DK = 64
GLA_DV = 128
GLA_GATE_RANK = 16
GLA_TAU = 16.0
GLA_CHUNK = 64
NSA_HEADS = 8
NSA_KV_GROUPS = 2
NSA_HPG = NSA_HEADS // NSA_KV_GROUPS
NSA_DH = 64
CMP_LEN = 32
CMP_STRIDE = 16
CMP_HIDDEN = 256
SEL_BLOCK = 64
SEL_TOPK = 16
WINDOW = 512
ROPE_THETA = 500000.0
ROT_DIM = NSA_DH // 4
ROT_HALF = ROT_DIM // 2
SGU_CHUNK = 128
SGU_GROUPS = 8
SGU_WIDTH = 2048
SGU_GROUP_DIM = SGU_WIDTH // SGU_GROUPS
N_EXPERTS = 16
N_EXPERT_GROUPS = 4
EXPERTS_PER_GROUP = N_EXPERTS // N_EXPERT_GROUPS
MOE_TOPK = 2
EXPERT_HIDDEN = 512

GLA_QK_W = GLA_HEADS * GLA_DK
GLA_V_W = GLA_HEADS * GLA_DV
NSA_Q_W = NSA_HEADS * NSA_DH
NSA_KV_W = NSA_KV_GROUPS * NSA_DH
N_CMP = (SEQ - CMP_LEN) // CMP_STRIDE + 1
N_CMP_PAD = SEQ // CMP_STRIDE
N_SEL = SEQ // SEL_BLOCK

NORM_EPS = 1e-6
NEG_INF = -1e30
FORCE_BONUS = 1e4

LANES = 128
MIB = 1024 * 1024

F32 = jnp.float32
BF16 = jnp.bfloat16
HI = lax.Precision.HIGHEST
NT_DIMS = (((1,), (1,)), ((), ()))
TN_DIMS = (((0,), (0,)), ((), ()))


def _cparams(sem, vmem_mib=48):
    return pltpu.CompilerParams(dimension_semantics=sem, vmem_limit_bytes=vmem_mib * MIB)


def _rms_mod(x, g, shift, scale):
    y = x * lax.rsqrt(jnp.mean(x * x, axis=-1, keepdims=True) + NORM_EPS) * g
    return y * (1 + scale) + shift


def _silu(x):
    return x * jax.nn.sigmoid(x)


def _log_sigmoid(z):
    return jnp.minimum(z, 0.0) - jnp.log1p(jnp.exp(-jnp.abs(z)))


ADA_TN = 1536
ADA_ROWS = 8


def _ada_kernel(c_ref, w_ref, b_ref, o_ref):
    cond = _silu(c_ref[...])
    o_ref[0] = jnp.dot(cond, w_ref[0], precision=HI, preferred_element_type=F32) + b_ref[0]


def ada_modulation(c, w_ada, b_ada):
    c8 = jnp.zeros((ADA_ROWS, D_MODEL), F32).at[:BATCH].set(c)
    width = 6 * D_MODEL
    out = pl.pallas_call(
        _ada_kernel,
        grid=(DEPTH, width // ADA_TN),
        in_specs=[
            pl.BlockSpec((ADA_ROWS, D_MODEL), lambda l, j: (0, 0)),
            pl.BlockSpec((1, D_MODEL, ADA_TN), lambda l, j: (l, 0, j)),
            pl.BlockSpec((1, 1, ADA_TN), lambda l, j: (l, 0, j)),
        ],
        out_specs=pl.BlockSpec((1, ADA_ROWS, ADA_TN), lambda l, j: (l, 0, j)),
        out_shape=jax.ShapeDtypeStruct((DEPTH, ADA_ROWS, width), F32),
        compiler_params=_cparams(("arbitrary", "arbitrary")),
        name="ada_modulation",
    )(c8, w_ada, b_ada.reshape(DEPTH, 1, width))
    return out[:, :BATCH].reshape(DEPTH, BATCH, 6, D_MODEL)


INPROJ_TM = 512
INPROJ_WIDTHS = (2 * GLA_QK_W, GLA_V_W, GLA_V_W, NSA_Q_W, 6 * NSA_KV_W, LANES)


def _arrange_w_in(w_in):
    o = np.cumsum((0, GLA_QK_W, GLA_QK_W, GLA_V_W, GLA_GATE_RANK, GLA_V_W, NSA_Q_W, 6 * NSA_KV_W, NSA_HEADS * 3))
    gq_gk = w_in[:, o[0]:o[2]]
    gv = w_in[:, o[2]:o[3]]
    glr = w_in[:, o[3]:o[4]]
    gr = w_in[:, o[4]:o[5]]
    nq = w_in[:, o[5]:o[6]]
    nkv = w_in[:, o[6]:o[7]]
    ng = w_in[:, o[7]:o[8]]
    pad = jnp.zeros((D_MODEL, LANES - GLA_GATE_RANK - NSA_HEADS * 3), w_in.dtype)
    return jnp.concatenate([gq_gk, gv, gr, nq, nkv, glr, ng, pad], axis=1).astype(BF16)


def _inproj0_kernel(x_ref, g_ref, sh_ref, sc_ref, w_ref, *o_refs):
    h = _rms_mod(x_ref[0], g_ref[...], sh_ref[0], sc_ref[0]).astype(BF16)
    off = 0
    for o_ref, wd in zip(o_refs, INPROJ_WIDTHS):
        o_ref[0] = jnp.dot(h, w_ref[:, off:off + wd], preferred_element_type=F32)
        off += wd


def inproj0(x, g, shift, scale, w_arranged):
    tm = INPROJ_TM
    wtot = sum(INPROJ_WIDTHS)
    row = lambda b, i: (b, i, 0)
    vec = lambda b, i: (b, 0, 0)
    return pl.pallas_call(
        _inproj0_kernel,
        grid=(BATCH, SEQ // tm),
        in_specs=[
            pl.BlockSpec((1, tm, D_MODEL), row),
            pl.BlockSpec((1, D_MODEL), lambda b, i: (0, 0)),
            pl.BlockSpec((1, 1, D_MODEL), vec),
            pl.BlockSpec((1, 1, D_MODEL), vec),
            pl.BlockSpec((D_MODEL, wtot), lambda b, i: (0, 0)),
        ],
        out_specs=[pl.BlockSpec((1, tm, wd), row) for wd in INPROJ_WIDTHS],
        out_shape=[jax.ShapeDtypeStruct((BATCH, SEQ, wd), F32) for wd in INPROJ_WIDTHS],
        compiler_params=_cparams(("arbitrary", "arbitrary")),
        name="inproj0",
    )(x, g.reshape(1, D_MODEL), shift.reshape(BATCH, 1, D_MODEL), scale.reshape(BATCH, 1, D_MODEL), w_arranged)


GLA_TG = 512


def _gla_kernel(qk_ref, v_ref, r_ref, misc_ref, w2_ref, bg_ref, og_ref, o_ref, st_ref, la_ref):
    C = GLA_CHUNK

    @pl.when(pl.program_id(1) == 0)
    def _():
        st_ref[...] = jnp.zeros_like(st_ref)

    z = jnp.dot(misc_ref[0], w2_ref[...], precision=HI, preferred_element_type=F32) + bg_ref[...]
    la_ref[...] = _log_sigmoid(z) / GLA_TAU

    ri = lax.broadcasted_iota(jnp.int32, (C, C), 0)
    ci = lax.broadcasted_iota(jnp.int32, (C, C), 1)
    causal = ri >= ci
    tril = causal.astype(F32)
    lane = lax.broadcasted_iota(jnp.int32, (1, GLA_QK_W), 1)
    og = og_ref[...]

    def chunk(c, carry):
        r0 = pl.multiple_of(c * C, C)
        rows = pl.ds(r0, C)
        bc = jnp.dot(tril, la_ref[rows, :], precision=HI, preferred_element_type=F32)
        b_mid = bc[C // 2:C // 2 + 1, :]
        b_last = bc[C - 1:C, :]
        q = qk_ref[0, rows, 0:GLA_QK_W] * (GLA_DK ** -0.5)
        k = qk_ref[0, rows, GLA_QK_W:2 * GLA_QK_W]
        qd = q * jnp.exp(bc - b_mid)
        kd = (k * jnp.exp(b_mid - bc)).astype(BF16)
        kl = k * jnp.exp(b_last - bc)
        qb = q * jnp.exp(bc)
        st = st_ref[...]
        st_b = st.astype(BF16)
        new_st = st * jnp.exp(b_last)
        for h in range(GLA_HEADS):
            in_head = (lane >= h * GLA_DK) & (lane < (h + 1) * GLA_DK)
            vcols = slice(h * GLA_DV, (h + 1) * GLA_DV)
            s = lax.dot_general(jnp.where(in_head, qd, 0.0).astype(BF16), kd, NT_DIMS, preferred_element_type=F32)
            s = jnp.where(causal, s, 0.0)
            vh = v_ref[0, rows, vcols].astype(BF16)
            o = jnp.dot(s.astype(BF16), vh, preferred_element_type=F32)
            o += lax.dot_general(jnp.where(in_head, qb, 0.0).astype(BF16), st_b, NT_DIMS, preferred_element_type=F32)
            new_st += lax.dot_general(vh, jnp.where(in_head, kl, 0.0).astype(BF16), TN_DIMS, preferred_element_type=F32)
            on = o * lax.rsqrt(jnp.mean(o * o, axis=-1, keepdims=True) + NORM_EPS) * og
            o_ref[0, rows, vcols] = on * _silu(r_ref[0, rows, vcols])
        st_ref[...] = new_st
        return carry

    lax.fori_loop(0, GLA_TG // C, chunk, 0)


def gla_mixer(qk, v, r, misc, w_gate2, b_gate, out_g):
    tg = GLA_TG
    w2 = jnp.zeros((LANES, GLA_QK_W), F32).at[:GLA_GATE_RANK].set(w_gate2)
    row = lambda b, i: (b, i, 0)
    const = lambda b, i: (0, 0)
    return pl.pallas_call(
        _gla_kernel,
        grid=(BATCH, SEQ // tg),
        in_specs=[
            pl.BlockSpec((1, tg, 2 * GLA_QK_W), row),
            pl.BlockSpec((1, tg, GLA_V_W), row),
            pl.BlockSpec((1, tg, GLA_V_W), row),
            pl.BlockSpec((1, tg, LANES), row),
            pl.BlockSpec((LANES, GLA_QK_W), const),
            pl.BlockSpec((1, GLA_QK_W), const),
            pl.BlockSpec((1, GLA_DV), const),
        ],
        out_specs=pl.BlockSpec((1, tg, GLA_V_W), row),
        out_shape=jax.ShapeDtypeStruct((BATCH, SEQ, GLA_V_W), F32),
        scratch_shapes=[pltpu.VMEM((GLA_DV, GLA_QK_W), F32), pltpu.VMEM((tg, GLA_QK_W), F32)],
        compiler_params=_cparams(("arbitrary", "arbitrary")),
        name="gla_mixer",
    )(qk, v, r, misc, w2, b_gate.reshape(1, GLA_QK_W), out_g.reshape(1, GLA_DV))


POS_SIDE = 128


def _rope_table_kernel(freq_ref, pos_ref, cos_ref, sin_ref):
    pos = pos_ref[...].astype(F32)
    for f in range(ROT_HALF):
        ang = pos * freq_ref[f]
        cos_ref[f] = jnp.cos(ang)
        sin_ref[f] = jnp.sin(ang)


def rope_tables(positions):
    inv_freq = jnp.float32(ROPE_THETA) ** (-jnp.arange(ROT_HALF, dtype=F32) / ROT_HALF)
    shp = jax.ShapeDtypeStruct((ROT_HALF, POS_SIDE, POS_SIDE), F32)
    cos, sin = pl.pallas_call(
        _rope_table_kernel,
        in_specs=[pl.BlockSpec(memory_space=pltpu.SMEM), pl.BlockSpec(memory_space=pltpu.VMEM)],
        out_specs=[pl.BlockSpec(memory_space=pltpu.VMEM)] * 2,
        out_shape=[shp, shp],
        name="rope_tables",
    )(inv_freq, positions.reshape(POS_SIDE, POS_SIDE))
    cos = cos.reshape(ROT_HALF, N_TOK).T
    sin = sin.reshape(ROT_HALF, N_TOK).T
    one = jnp.ones((N_TOK, NSA_DH - ROT_DIM), F32)
    zero = jnp.zeros((N_TOK, NSA_DH - ROT_DIM), F32)
    z8 = jnp.zeros((N_TOK, ROT_HALF), F32)
    c64 = jnp.concatenate([cos, cos, one], axis=1)
    sm64 = jnp.concatenate([-sin, z8, zero], axis=1)
    sp64 = jnp.concatenate([z8, sin, zero], axis=1)
    two = lambda t: jnp.concatenate([t, t], axis=1).reshape(BATCH, SEQ, LANES)
    return two(c64), two(sm64), two(sp64)


def _block_diag_ones(width):
    h = np.arange(width) // NSA_DH
    return jnp.asarray((h[:, None] == h[None, :]).astype(np.float32))


def _head_norm_rope(x, gain, bd, c, sm, sp):
    width = x.shape[-1]
    reps = width // LANES
    ss = jnp.dot(x * x, bd, precision=HI, preferred_element_type=F32)
    y = x * lax.rsqrt(ss * (1.0 / NSA_DH) + NORM_EPS) * gain
    tile = lambda t: jnp.concatenate([t] * reps, axis=1) if reps > 1 else t
    return (y * tile(c) + pltpu.roll(y, width - ROT_HALF, 1) * tile(sm) + pltpu.roll(y, ROT_HALF, 1) * tile(sp))


PREP_TM = 512


def _prep_kernel(q_ref, ks_ref, kw_ref, c_ref, sm_ref, sp_ref, gq_ref, gk_ref, bd_ref, qo_ref, kso_ref, kwo_ref):
    c, sm, sp = c_ref[0], sm_ref[0], sp_ref[0]
    bd = bd_ref[...]
    bd1 = bd_ref[0:LANES, 0:LANES]
    qo_ref[0] = _head_norm_rope(q_ref[0], gq_ref[...], bd, c, sm, sp) * (NSA_DH ** -0.5)
    kso_ref[0] = _head_norm_rope(ks_ref[0], gk_ref[0:1, :], bd1, c, sm, sp)
    kwo_ref[0] = _head_norm_rope(kw_ref[0], gk_ref[1:2, :], bd1, c, sm, sp)


def nsa_prep(nq, nkv, tabs, q_gain, k_gain):
    tm = PREP_TM
    row = lambda b, i: (b, i, 0)
    const = lambda b, i: (0, 0)
    gq = jnp.tile(q_gain, NSA_HEADS).reshape(1, NSA_Q_W)
    gk = jnp.stack([jnp.tile(k_gain[1], NSA_KV_GROUPS), jnp.tile(k_gain[2], NSA_KV_GROUPS)])
    return pl.pallas_call(
        _prep_kernel,
        grid=(BATCH, SEQ // tm),
        in_specs=[
            pl.BlockSpec((1, tm, NSA_Q_W), row),
            pl.BlockSpec((1, tm, NSA_KV_W), lambda b, i: (b, i, 2)),
            pl.BlockSpec((1, tm, NSA_KV_W), lambda b, i: (b, i, 4)),
            pl.BlockSpec((1, tm, LANES), row),
            pl.BlockSpec((1, tm, LANES), row),
            pl.BlockSpec((1, tm, LANES), row),
            pl.BlockSpec((1, NSA_Q_W), const),
            pl.BlockSpec((2, NSA_KV_W), const),
            pl.BlockSpec((NSA_Q_W, NSA_Q_W), const),
        ],
        out_specs=[pl.BlockSpec((1, tm, NSA_Q_W), row), pl.BlockSpec((1, tm, NSA_KV_W), row),
                   pl.BlockSpec((1, tm, NSA_KV_W), row)],
        out_shape=[jax.ShapeDtypeStruct((BATCH, SEQ, NSA_Q_W), F32), jax.ShapeDtypeStruct((BATCH, SEQ, NSA_KV_W), F32),
                   jax.ShapeDtypeStruct((BATCH, SEQ, NSA_KV_W), F32)],
        compiler_params=_cparams(("arbitrary", "arbitrary")),
        name="nsa_prep",
    )(nq, nkv, nkv, *tabs, gq, gk, _block_diag_ones(NSA_Q_W))


SEG_W = CMP_STRIDE * NSA_DH


def _cmp_kernel(xk_ref, xv_ref, pe_ref, w1_ref, w2_ref, gain_ref, c_ref, sm_ref, sp_ref, bd_ref, ko_ref, vo_ref):
    def compress(x_ref, kv):
        out = jnp.zeros((N_CMP_PAD, LANES), F32)
        for g in range(NSA_KV_GROUPS):
            x = x_ref[0, g]
            ha = jnp.dot(x + pe_ref[kv, 0], w1_ref[kv, 0:SEG_W, :], precision=HI, preferred_element_type=F32)
            hb = jnp.dot(x + pe_ref[kv, 1], w1_ref[kv, SEG_W:2 * SEG_W, :], precision=HI, preferred_element_type=F32)
            hid = ha + pltpu.roll(hb, N_CMP_PAD - 1, 0)
            out += jnp.dot(jax.nn.gelu(hid), w2_ref[kv, g], precision=HI, preferred_element_type=F32)
        return out

    ko_ref[0] = _head_norm_rope(compress(xk_ref, 0), gain_ref[...], bd_ref[...], c_ref[0], sm_ref[0], sp_ref[0])
    vo_ref[0] = compress(xv_ref, 1)


def nsa_compress(xk, xv, cmp_pe, cmp_w1, cmp_w2, k_gain0, ctabs):
    pe = cmp_pe.reshape(2, 2, 1, SEG_W)
    w2 = jnp.zeros((2, NSA_KV_GROUPS, CMP_HIDDEN, LANES), F32)
    for g in range(NSA_KV_GROUPS):
        w2 = w2.at[:, g, :, g * NSA_DH:(g + 1) * NSA_DH].set(cmp_w2)
    seg = pl.BlockSpec((1, NSA_KV_GROUPS, N_CMP_PAD, SEG_W), lambda b: (b, 0, 0, 0))
    tab = pl.BlockSpec((1, N_CMP_PAD, LANES), lambda b: (b, 0, 0))
    full = lambda shape: pl.BlockSpec(shape, lambda b: (0,) * len(shape))
    return pl.pallas_call(
        _cmp_kernel,
        grid=(BATCH,),
        in_specs=[seg, seg, full((2, 2, 1, SEG_W)), full((2, 2 * SEG_W, CMP_HIDDEN)),
                  full((2, NSA_KV_GROUPS, CMP_HIDDEN, LANES)), full((1, LANES)), tab, tab, tab, full((LANES, LANES))],
        out_specs=[tab, tab],
        out_shape=[jax.ShapeDtypeStruct((BATCH, N_CMP_PAD, LANES), F32)] * 2,
        compiler_params=_cparams(("arbitrary",)),
        name="nsa_compress",
    )(xk, xv, pe, cmp_w1, w2, jnp.tile(k_gain0, NSA_KV_GROUPS).reshape(1, LANES), *ctabs, _block_diag_ones(LANES))


CA_TQ = 256
SUBLANES = 8


CA_COLS = NSA_HPG * CA_TQ
CMP_PER_SEL = SEL_BLOCK // CMP_STRIDE


def split3_keys(k):
    hi = k.astype(BF16)
    lo = (k - hi.astype(F32)).astype(BF16)
    return jnp.concatenate([hi, lo, hi], axis=-1)


def _cattn_kernel(q_ref, kc_ref, vct_ref, gl_ref, o_ref, sel_ref, q3_ref, ps_ref):
    tq = CA_TQ
    q0 = pl.program_id(2) * tq
    lanes4 = lambda t: jnp.concatenate([t] * NSA_HPG, axis=1)
    for h in range(NSA_HPG):
        q = q_ref[0, h]
        hi = q.astype(BF16)
        lo = (q - hi.astype(F32)).astype(BF16)
        for n, part in enumerate((hi, hi, lo)):
            q3_ref[n * NSA_DH:(n + 1) * NSA_DH, h * tq:(h + 1) * tq] = part
    s = jnp.dot(kc_ref[0, 0], q3_ref[...], preferred_element_type=F32)
    cend = lax.broadcasted_iota(jnp.int32, (N_CMP_PAD, tq), 0) * CMP_STRIDE + (CMP_LEN - 1)
    tc = q0 + lax.broadcasted_iota(jnp.int32, (N_CMP_PAD, tq), 1)
    cmask = lanes4(cend <= tc)
    s = jnp.where(cmask, s, NEG_INF)
    m = jnp.max(s, axis=0, keepdims=True)
    e = jnp.where(cmask, jnp.exp(s - m), 0.0)
    l = jnp.sum(e, axis=0, keepdims=True)
    p = e / jnp.where(l > 0.0, l, 1.0)
    gate = jnp.concatenate([jax.nn.sigmoid(gl_ref[0, h, 0:1, :]) for h in range(NSA_HPG)], axis=1)
    o = jnp.dot(vct_ref[0, 0], p.astype(BF16), preferred_element_type=F32) * gate
    for h in range(NSA_HPG):
        o_ref[0, h] = o[:, h * tq:(h + 1) * tq]
    psum = functools.reduce(jnp.add, [p[:, h * tq:(h + 1) * tq] for h in range(NSA_HPG)])
    for n in range(tq // LANES):
        ps_ref[n] = psum[:, n * LANES:(n + 1) * LANES]

    jj = lax.broadcasted_iota(jnp.int32, (N_SEL, tq), 0)
    every4th = lambda r: jnp.concatenate([ps_ref[n, pl.ds(r, N_SEL, stride=CMP_PER_SEL), :] for n in range(tq // LANES)],
                                         axis=1)
    starts_in = [every4th(r) for r in range(CMP_PER_SEL)]
    from_prev = jnp.where(jj >= 1, pltpu.roll(starts_in[CMP_PER_SEL - 1], 1, 0), 0.0)
    imp = functools.reduce(jnp.add, starts_in) + from_prev
    tt = q0 + lax.broadcasted_iota(jnp.int32, (N_SEL, tq), 1)
    cur = jnp.right_shift(tt, 6)
    forced = (jj == 0) | (jj == cur) | (jj == cur - 1)
    valid = jj * SEL_BLOCK <= tt
    score = jnp.where(valid, imp + jnp.where(forced, FORCE_BONUS, 0.0), NEG_INF)

    n_slab = N_SEL // SUBLANES
    sc = [score[SUBLANES * a:SUBLANES * (a + 1)] for a in range(n_slab)]
    rk = [jnp.zeros((SUBLANES, tq), F32) for _ in range(n_slab)]
    sub = lax.broadcasted_iota(jnp.int32, (SUBLANES, tq), 0)
    for jp in range(N_SEL):
        a0, r0 = divmod(jp, SUBLANES)
        row = sc[a0][r0:r0 + 1]
        for a in range(n_slab):
            gt = jnp.where(row > sc[a], 1.0, 0.0)
            ge = jnp.where(row >= sc[a], 1.0, 0.0)
            if a < a0:
                rk[a] = rk[a] + gt
            elif a > a0:
                rk[a] = rk[a] + ge
            else:
                rk[a] = rk[a] + jnp.where(sub > r0, ge, gt)
    for a in range(n_slab):
        rows = slice(SUBLANES * a, SUBLANES * (a + 1))
        sel_ref[0, 0, rows, :] = jnp.where((rk[a] < SEL_TOPK) & valid[rows], 1.0, 0.0)


def nsa_cmp_attn(q_t, kcmp, vcmp_t, gl_t):
    tq = CA_TQ
    return pl.pallas_call(
        _cattn_kernel,
        grid=(BATCH, NSA_KV_GROUPS, SEQ // tq),
        in_specs=[
            pl.BlockSpec((1, NSA_HPG, NSA_DH, tq), lambda b, g, i: (b, g, 0, i)),
            pl.BlockSpec((1, 1, N_CMP_PAD, 3 * NSA_DH), lambda b, g, i: (b, g, 0, 0)),
            pl.BlockSpec((1, 1, NSA_DH, N_CMP_PAD), lambda b, g, i: (b, g, 0, 0)),
            pl.BlockSpec((1, NSA_HPG, 3, tq), lambda b, g, i: (b, g, 0, i)),
        ],
        out_specs=[pl.BlockSpec((1, NSA_HPG, NSA_DH, tq), lambda b, g, i: (b, g, 0, i)),
                   pl.BlockSpec((1, 1, N_SEL, tq), lambda b, g, i: (b, g, 0, i))],
        out_shape=[jax.ShapeDtypeStruct((BATCH, NSA_HEADS, NSA_DH, SEQ), F32),
                   jax.ShapeDtypeStruct((BATCH, NSA_KV_GROUPS, N_SEL, SEQ), F32)],
        scratch_shapes=[pltpu.VMEM((3 * NSA_DH, CA_COLS), BF16), pltpu.VMEM((tq // LANES, N_CMP_PAD, LANES), F32)],
        compiler_params=_cparams(("arbitrary", "arbitrary", "arbitrary")),
        name="nsa_cmp_attn",
    )(q_t, split3_keys(kcmp), vcmp_t.astype(BF16), gl_t)


SA_TQ = 256
SA_TK = 1024
SA_PARTS = 2
SA_PART = SA_TK // SA_PARTS
M_INIT = -1e20


SA_COLS = NSA_HPG * SA_TQ
SA_BLOCKS = SA_TK // SEL_BLOCK


VT_ROWS = NSA_DH + 16


def value_slab_t(v_t):
    ones = jnp.ones(v_t.shape[:2] + (1, SEQ), BF16)
    zeros = jnp.zeros(v_t.shape[:2] + (VT_ROWS - NSA_DH - 1, SEQ), BF16)
    return jnp.concatenate([v_t.astype(BF16), ones, zeros], axis=2)


def sel_key_slab(ksel):
    blk = (np.arange(SEQ) % SA_TK) // SEL_BLOCK
    onehot = (blk[:, None] == np.arange(LANES - NSA_DH)[None, :]).astype(np.float32)
    onehot = jnp.broadcast_to(jnp.asarray(onehot, BF16), ksel.shape[:3] + (LANES - NSA_DH,))
    return jnp.concatenate([ksel.astype(BF16), onehot], axis=-1)


def _sattn_kernel(q_ref, k_ref, vt_ref, sel_ref, gl_ref, prev_ref, o_ref, qa_ref, acc_ref):
    tq, tk = SA_TQ, SA_TK
    i = pl.program_id(2)
    for h in range(NSA_HPG):
        qa_ref[0:NSA_DH, h * tq:(h + 1) * tq] = q_ref[0, h]
    qa_ref[NSA_DH:LANES, :] = jnp.zeros((LANES - NSA_DH, SA_COLS), BF16)
    acc_ref[...] = jnp.zeros_like(acc_ref)
    lanes4 = lambda t: jnp.concatenate([t] * NSA_HPG, axis=1)

    def key_tile(kt, m_prev, diagonal):
        selrows = sel_ref[0, 0, pl.ds(pl.multiple_of(kt * SA_BLOCKS, SA_BLOCKS), SA_BLOCKS), :]
        qa_ref[NSA_DH:NSA_DH + SA_BLOCKS, :] = lanes4(jnp.where(selrows > 0.5, 0.0, NEG_INF)).astype(BF16)
        ss = []
        for part in range(SA_PARTS):
            keys = pl.ds(pl.multiple_of(kt * tk + part * SA_PART, SA_PART), SA_PART)
            s = jnp.dot(k_ref[0, 0, keys, :], qa_ref[...], preferred_element_type=F32)
            if diagonal:
                kpos = kt * tk + part * SA_PART + lax.broadcasted_iota(jnp.int32, (SA_PART, tq), 0)
                tt = i * tq + lax.broadcasted_iota(jnp.int32, (SA_PART, tq), 1)
                s = s + lanes4(jnp.where(kpos <= tt, 0.0, NEG_INF))
            ss.append(s.astype(BF16))
        m_tile = functools.reduce(jnp.maximum, [jnp.max(s, axis=0, keepdims=True) for s in ss])
        m_new = jnp.maximum(m_prev, m_tile.astype(F32))
        acc = jnp.exp(m_prev - m_new) * acc_ref[...]
        for part in range(SA_PARTS):
            keys = pl.ds(pl.multiple_of(kt * tk + part * SA_PART, SA_PART), SA_PART)
            p = jnp.exp(ss[part] - m_new.astype(BF16))
            acc += jnp.dot(vt_ref[0, 0, :, keys], p, preferred_element_type=F32)
        acc_ref[...] = acc
        return m_new

    n_full = (i * tq) // tk
    m = lax.fori_loop(0, n_full, lambda kt, m: key_tile(kt, m, False), jnp.full((1, SA_COLS), M_INIT, F32))
    key_tile(n_full, m, True)
    gate = jnp.concatenate([jax.nn.sigmoid(gl_ref[0, h, 1:2, :]) for h in range(NSA_HPG)], axis=1)
    out = acc_ref[0:NSA_DH, :] / acc_ref[NSA_DH:NSA_DH + 1, :] * gate
    for h in range(NSA_HPG):
        o_ref[0, h] = prev_ref[0, h] + out[:, h * tq:(h + 1) * tq]


def nsa_sel_attn(q_t, k_slab, vsel_t, sel_t, gl_t, prev):
    tq = SA_TQ
    ospec = pl.BlockSpec((1, NSA_HPG, NSA_DH, tq), lambda b, g, i: (b, g, 0, i))
    return pl.pallas_call(
        _sattn_kernel,
        grid=(BATCH, NSA_KV_GROUPS, SEQ // tq),
        in_specs=[
            ospec,
            pl.BlockSpec((1, 1, SEQ, LANES), lambda b, g, i: (b, g, 0, 0)),
            pl.BlockSpec((1, 1, VT_ROWS, SEQ), lambda b, g, i: (b, g, 0, 0)),
            pl.BlockSpec((1, 1, N_SEL, tq), lambda b, g, i: (b, g, 0, i)),
            pl.BlockSpec((1, NSA_HPG, 3, tq), lambda b, g, i: (b, g, 0, i)),
            ospec,
        ],
        out_specs=ospec,
        out_shape=jax.ShapeDtypeStruct((BATCH, NSA_HEADS, NSA_DH, SEQ), F32),
        scratch_shapes=[pltpu.VMEM((LANES, SA_COLS), BF16), pltpu.VMEM((VT_ROWS, SA_COLS), F32)],
        input_output_aliases={5: 0},
        compiler_params=_cparams(("arbitrary", "arbitrary", "arbitrary")),
        name="nsa_sel_attn",
    )(q_t, k_slab, vsel_t, sel_t, gl_t, prev)


WA_TQ = 256
WA_TILES = WINDOW // WA_TQ + 1


def _window_bias():
    kl = np.arange(WA_TILES * WA_TQ)[:, None]
    ql = np.arange(WA_TQ)[None, :]
    diff = ql - kl + WINDOW
    return jnp.asarray(np.where((diff >= 0) & (diff < WINDOW), 0.0, NEG_INF).astype(np.float32))


def _wattn_kernel(q_ref, k0_ref, k1_ref, k2_ref, v0_ref, v1_ref, v2_ref, bias_ref, gl_ref, prev_ref, o_ref):
    tq = WA_TQ
    i = pl.program_id(2)
    k_refs = (k0_ref, k1_ref, k2_ref)
    v_refs = (v0_ref, v1_ref, v2_ref)
    lanes4 = lambda t: jnp.concatenate([t] * NSA_HPG, axis=1)
    q = jnp.concatenate([q_ref[0, h] for h in range(NSA_HPG)], axis=1)
    ss = []
    for d in range(WA_TILES):
        in_seq = i - (WA_TILES - 1) + d >= 0
        bias = jnp.where(in_seq, bias_ref[d * tq:(d + 1) * tq, :], NEG_INF)
        ss.append((jnp.dot(k_refs[d][0, 0], q, preferred_element_type=F32) + lanes4(bias)).astype(BF16))
    m = functools.reduce(jnp.maximum, [jnp.max(s, axis=0, keepdims=True) for s in ss])
    acc = functools.reduce(jnp.add, [jnp.dot(v_refs[d][0, 0], jnp.exp(ss[d] - m), preferred_element_type=F32)
                                     for d in range(WA_TILES)])
    gate = jnp.concatenate([jax.nn.sigmoid(gl_ref[0, h, 2:3, :]) for h in range(NSA_HPG)], axis=1)
    out = acc[0:NSA_DH] / acc[NSA_DH:NSA_DH + 1] * gate
    for h in range(NSA_HPG):
        o_ref[0, h] = prev_ref[0, h] + out[:, h * tq:(h + 1) * tq]


def nsa_win_attn(q_t, kwin, vwin_t, gl_t, prev):
    tq = WA_TQ
    qspec = pl.BlockSpec((1, NSA_HPG, NSA_DH, tq), lambda b, g, i: (b, g, 0, i))
    tile = lambda d: (lambda i: jnp.maximum(i - (WA_TILES - 1) + d, 0))
    kspec = lambda d: pl.BlockSpec((1, 1, tq, NSA_DH), lambda b, g, i: (b, g, tile(d)(i), 0))
    vspec = lambda d: pl.BlockSpec((1, 1, VT_ROWS, tq), lambda b, g, i: (b, g, 0, tile(d)(i)))
    return pl.pallas_call(
        _wattn_kernel,
        grid=(BATCH, NSA_KV_GROUPS, SEQ // tq),
        in_specs=[qspec] + [kspec(d) for d in range(WA_TILES)] + [vspec(d) for d in range(WA_TILES)] + [
            pl.BlockSpec((WA_TILES * tq, tq), lambda b, g, i: (0, 0)),
            pl.BlockSpec((1, NSA_HPG, 3, tq), lambda b, g, i: (b, g, 0, i)),
            qspec,
        ],
        out_specs=qspec,
        out_shape=jax.ShapeDtypeStruct((BATCH, NSA_HEADS, NSA_DH, SEQ), F32),
        input_output_aliases={2 * WA_TILES + 3: 0},
        compiler_params=_cparams(("arbitrary", "arbitrary", "arbitrary")),
        name="nsa_win_attn",
    )(q_t, *([kwin] * WA_TILES), *([vwin_t] * WA_TILES), _window_bias(), gl_t, prev)


def nsa_mixer(nq, nkv, misc, positions, q_gain, k_gain, cmp_pe, cmp_w1, cmp_w2):
    tabs = rope_tables(positions)
    q_r, ks_r, kw_r = nsa_prep(nq, nkv, tabs, q_gain, k_gain)
    group_major = lambda t: t.reshape(BATCH, SEQ, NSA_KV_GROUPS, NSA_DH).transpose(0, 2, 1, 3)
    group_major_t = lambda t: t.reshape(BATCH, SEQ, NSA_KV_GROUPS, NSA_DH).transpose(0, 2, 3, 1)
    col = lambda n: nkv[..., n * NSA_KV_W:(n + 1) * NSA_KV_W]
    segs = lambda t: group_major(t).reshape(BATCH, NSA_KV_GROUPS, N_CMP_PAD, SEG_W)
    last = jnp.minimum(jnp.arange(N_CMP_PAD) * CMP_STRIDE + CMP_LEN - 1, SEQ - 1)
    ctabs = tuple(t[:, last] for t in tabs)
    kcmp, vcmp = nsa_compress(segs(col(0)), segs(col(1)), cmp_pe, cmp_w1, cmp_w2, k_gain[0], ctabs)
    kcmp = kcmp.reshape(BATCH, N_CMP_PAD, NSA_KV_GROUPS, NSA_DH).transpose(0, 2, 1, 3)
    vcmp_t = vcmp.reshape(BATCH, N_CMP_PAD, NSA_KV_GROUPS, NSA_DH).transpose(0, 2, 3, 1)
    q_t = q_r.reshape(BATCH, SEQ, NSA_HEADS, NSA_DH).transpose(0, 2, 3, 1)
    gl_t = misc[..., GLA_GATE_RANK:GLA_GATE_RANK + NSA_HEADS * 3].reshape(BATCH, SEQ, NSA_HEADS, 3).transpose(0, 2, 3, 1)
    o_t, sel_t = nsa_cmp_attn(q_t, kcmp, vcmp_t, gl_t)
    q_t16 = q_t.astype(BF16)
    o_t = nsa_sel_attn(q_t16, sel_key_slab(group_major(ks_r)), value_slab_t(group_major_t(col(3))), sel_t, gl_t, o_t)
    o_t = nsa_win_attn(q_t16, group_major(kw_r).astype(BF16), value_slab_t(group_major_t(col(5))), gl_t, o_t)
    return o_t.transpose(0, 3, 1, 2).reshape(BATCH, SEQ, NSA_Q_W)


ROUTE_ROWS = 8
HX_W = D_MODEL + 3 * LANES


def _top2_sum(a, b, c, d):
    hi1, lo1 = jnp.maximum(a, b), jnp.minimum(a, b)
    hi2, lo2 = jnp.maximum(c, d), jnp.minimum(c, d)
    return jnp.maximum(hi1, hi2) + jnp.maximum(jnp.minimum(hi1, hi2), jnp.maximum(lo1, lo2))


def _moe_prenorm_route(xn, g_ref, sh_ref, sc_ref, wr_ref, rb_ref, hx_ref, route_ref):
    h = _rms_mod(xn, g_ref[...], sh_ref[0], sc_ref[0])
    logits = lax.dot_general(wr_ref[...], h, NT_DIMS, precision=HI, preferred_element_type=F32)
    scores = jax.nn.sigmoid(logits)
    sel = scores + rb_ref[...]
    epg = EXPERTS_PER_GROUP
    srow = lambda e: sel[e:e + 1, :]
    grp = [_top2_sum(*[srow(epg * g + r) for r in range(epg)]) for g in range(N_EXPERT_GROUPS)]
    best, gi = grp[0], jnp.zeros_like(grp[0], dtype=jnp.int32)
    for g in range(1, N_EXPERT_GROUPS):
        better = grp[g] > best
        gi = jnp.where(better, g, gi)
        best = jnp.where(better, grp[g], best)

    def in_group(mat, r):
        out = mat[r:r + 1, :]
        for g in range(1, N_EXPERT_GROUPS):
            out = jnp.where(gi == g, mat[epg * g + r:epg * g + r + 1, :], out)
        return out

    v = [in_group(sel, r) for r in range(epg)]
    sc = [in_group(scores, r) for r in range(epg)]
    b1, i1, w1 = v[0], jnp.zeros_like(gi), sc[0]
    for r in range(1, epg):
        better = v[r] > b1
        i1 = jnp.where(better, r, i1)
        w1 = jnp.where(better, sc[r], w1)
        b1 = jnp.where(better, v[r], b1)
    b2 = jnp.full_like(b1, -3e38)
    i2, w2 = jnp.zeros_like(gi), jnp.zeros_like(w1)
    for r in range(epg):
        better = (i1 != r) & (v[r] > b2)
        i2 = jnp.where(better, r, i2)
        w2 = jnp.where(better, sc[r], w2)
        b2 = jnp.where(better, v[r], b2)
    tot = w1 + w2
    w1, w2 = w1 / tot, w2 / tot
    zero = jnp.zeros_like(w1)
    route_ref[0] = jnp.concatenate([gi.astype(F32)] + [zero] * (ROUTE_ROWS - 1), axis=0)
    wrows = [jnp.where(i1 == r, w1, jnp.where(i2 == r, w2, 0.0)) for r in range(epg)]
    wmat = jnp.concatenate(wrows + [jnp.zeros((LANES - epg, w1.shape[1]), F32)], axis=0).T
    w_hi = wmat.astype(BF16)
    rest = wmat - w_hi.astype(F32)
    w_mid = rest.astype(BF16)
    w_lo = (rest - w_mid.astype(F32)).astype(BF16)
    hx_ref[0, :, 0:D_MODEL] = h.astype(BF16)
    for n, part in enumerate((w_hi, w_mid, w_lo)):
        hx_ref[0, :, D_MODEL + n * LANES:D_MODEL + (n + 1) * LANES] = part


def _route_specs(tm, row, vec, const):
    in_specs = [pl.BlockSpec((1, D_MODEL), const), pl.BlockSpec((1, 1, D_MODEL), vec), pl.BlockSpec((1, 1, D_MODEL), vec),
                pl.BlockSpec((N_EXPERTS, D_MODEL), const), pl.BlockSpec((N_EXPERTS, 1), const)]
    out_specs = [pl.BlockSpec((1, tm, HX_W), row), pl.BlockSpec((1, ROUTE_ROWS, tm), lambda b, i: (b, 0, i))]
    out_shape = [jax.ShapeDtypeStruct((BATCH, SEQ, HX_W), BF16), jax.ShapeDtypeStruct((BATCH, ROUTE_ROWS, SEQ), F32)]
    return in_specs, out_specs, out_shape


def _route_args(g, shift, scale, w_router, router_bias):
    return (g.reshape(1, D_MODEL), shift.reshape(BATCH, 1, D_MODEL), scale.reshape(BATCH, 1, D_MODEL),
            w_router.T, router_bias.reshape(N_EXPERTS, 1))


OUTPROJ_TM = 512


def _outproj0_kernel(oa_ref, ob_ref, w_ref, x_ref, gate_ref, g_ref, sh_ref, sc_ref, wr_ref, rb_ref,
                     xo_ref, h_ref, route_ref):
    y = jnp.dot(oa_ref[0].astype(BF16), w_ref[0:GLA_V_W, :], preferred_element_type=F32)
    y += jnp.dot(ob_ref[0].astype(BF16), w_ref[GLA_V_W:GLA_V_W + NSA_Q_W, :], preferred_element_type=F32)
    xn = x_ref[0] + gate_ref[0] * y
    xo_ref[0] = xn
    _moe_prenorm_route(xn, g_ref, sh_ref, sc_ref, wr_ref, rb_ref, h_ref, route_ref)


def outproj0(o_a, o_b, w_out, x, gate, route_args):
    tm = OUTPROJ_TM
    row = lambda b, i: (b, i, 0)
    vec = lambda b, i: (b, 0, 0)
    const = lambda b, i: (0, 0)
    r_in, r_out, r_shape = _route_specs(tm, row, vec, const)
    return pl.pallas_call(
        _outproj0_kernel,
        grid=(BATCH, SEQ // tm),
        in_specs=[pl.BlockSpec((1, tm, GLA_V_W), row), pl.BlockSpec((1, tm, NSA_Q_W), row),
                  pl.BlockSpec((GLA_V_W + NSA_Q_W, D_MODEL), const), pl.BlockSpec((1, tm, D_MODEL), row),
                  pl.BlockSpec((1, 1, D_MODEL), vec)] + r_in,
        out_specs=[pl.BlockSpec((1, tm, D_MODEL), row)] + r_out,
        out_shape=[jax.ShapeDtypeStruct((BATCH, SEQ, D_MODEL), F32)] + r_shape,
        compiler_params=_cparams(("arbitrary", "arbitrary")),
        name="outproj0",
    )(o_a, o_b, w_out.astype(BF16), x, gate.reshape(BATCH, 1, D_MODEL), *route_args)


GMLP_TM = 256


def _gmlp_kernel(x_ref, g1_ref, sh1_ref, sc1_ref, win_ref, ng_ref, ws_ref, bs_ref, wout_ref, gate_ref,
                 g_ref, sh_ref, sc_ref, wr_ref, rb_ref, xo_ref, h_ref, route_ref, gated_ref):
    x = x_ref[0]
    h = _rms_mod(x, g1_ref[...], sh1_ref[0], sc1_ref[0]).astype(BF16)
    u = jax.nn.gelu(jnp.dot(h, win_ref[:, 0:SGU_WIDTH], preferred_element_type=F32))
    v = jax.nn.gelu(jnp.dot(h, win_ref[:, SGU_WIDTH:2 * SGU_WIDTH], preferred_element_type=F32))
    v = (v * lax.rsqrt(jnp.mean(v * v, axis=-1, keepdims=True) + NORM_EPS) * ng_ref[...]).astype(BF16)
    ri = lax.broadcasted_iota(jnp.int32, (SGU_CHUNK, SGU_CHUNK), 0)
    ci = lax.broadcasted_iota(jnp.int32, (SGU_CHUNK, SGU_CHUNK), 1)
    for g in range(SGU_GROUPS):
        w = jnp.where(ri >= ci, ws_ref[g], 0.0).astype(BF16)
        cols = slice(g * SGU_GROUP_DIM, (g + 1) * SGU_GROUP_DIM)
        for c in range(GMLP_TM // SGU_CHUNK):
            rows = slice(c * SGU_CHUNK, (c + 1) * SGU_CHUNK)
            mix = jnp.dot(w, v[rows, cols], preferred_element_type=F32) + bs_ref[:, g:g + 1]
            gated_ref[rows, cols] = (u[rows, cols] * mix).astype(BF16)
    y = jnp.dot(gated_ref[...], wout_ref[...], preferred_element_type=F32)
    xn = x + gate_ref[0] * y
    xo_ref[0] = xn
    _moe_prenorm_route(xn, g_ref, sh_ref, sc_ref, wr_ref, rb_ref, h_ref, route_ref)


def gmlp_layer(x, g1, shift1, scale1, w_in, norm_g, w_s, b_s, w_out, gate, route_args):
    tm = GMLP_TM
    row = lambda b, i: (b, i, 0)
    vec = lambda b, i: (b, 0, 0)
    const = lambda b, i: (0, 0)
    r_in, r_out, r_shape = _route_specs(tm, row, vec, const)
    vspec = pl.BlockSpec((1, 1, D_MODEL), vec)
    return pl.pallas_call(
        _gmlp_kernel,
        grid=(BATCH, SEQ // tm),
        in_specs=[pl.BlockSpec((1, tm, D_MODEL), row), pl.BlockSpec((1, D_MODEL), const), vspec, vspec,
                  pl.BlockSpec((D_MODEL, 2 * SGU_WIDTH), const), pl.BlockSpec((1, SGU_WIDTH), const),
                  pl.BlockSpec((SGU_GROUPS, SGU_CHUNK, SGU_CHUNK), lambda b, i: (0, 0, 0)),
                  pl.BlockSpec((SGU_CHUNK, SGU_GROUPS), const), pl.BlockSpec((SGU_WIDTH, D_MODEL), const), vspec] + r_in,
        out_specs=[pl.BlockSpec((1, tm, D_MODEL), row)] + r_out,
        out_shape=[jax.ShapeDtypeStruct((BATCH, SEQ, D_MODEL), F32)] + r_shape,
        scratch_shapes=[pltpu.VMEM((tm, SGU_WIDTH), BF16)],
        compiler_params=_cparams(("arbitrary", "arbitrary"), vmem_mib=56),
        name="gmlp_layer",
    )(x, g1.reshape(1, D_MODEL), shift1.reshape(BATCH, 1, D_MODEL), scale1.reshape(BATCH, 1, D_MODEL),
      w_in.astype(BF16), norm_g.reshape(1, SGU_WIDTH), w_s, b_s.T, w_out.astype(BF16),
      gate.reshape(BATCH, 1, D_MODEL), *route_args)


MOE_TM = 256
MOE_CHUNK = 512
MOE_SORTED = N_TOK + N_EXPERT_GROUPS * MOE_TM
MOE_TILES = MOE_SORTED // MOE_TM
MOE_CHUNKS = N_TOK // MOE_CHUNK
MOE_PAIRS = MOE_TILES + N_EXPERT_GROUPS * MOE_CHUNKS
FLAG_ACTIVE, FLAG_FIRST, FLAG_LAST, FLAG_ZERO = 1, 2, 4, 8


def _plan_kernel(gi_ref, rank_ref, before_ref):
    gi = gi_ref[...]
    r = lax.broadcasted_iota(jnp.int32, (POS_SIDE, POS_SIDE), 0)
    c = lax.broadcasted_iota(jnp.int32, (POS_SIDE, POS_SIDE), 1)
    upper = jnp.where(r <= c, 1.0, 0.0)
    lower_strict = jnp.where(c < r, 1.0, 0.0)
    rank = jnp.zeros((POS_SIDE, POS_SIDE), F32)
    for g in range(N_EXPERT_GROUPS):
        member = jnp.where(gi == g, 1.0, 0.0)
        in_row = jnp.dot(member, upper, precision=HI, preferred_element_type=F32)
        row_total = jnp.broadcast_to(in_row[:, POS_SIDE - 1:POS_SIDE], (POS_SIDE, POS_SIDE))
        before = jnp.dot(lower_strict, row_total, precision=HI, preferred_element_type=F32)
        before_ref[g] = before
        rank += member * (before + in_row - 1.0)
    rank_ref[...] = rank


def moe_plan(route):
    tm = MOE_TM
    i32 = jnp.int32
    gi_f = route[:, 0, :].reshape(POS_SIDE, POS_SIDE)
    rank, before = pl.pallas_call(
        _plan_kernel,
        out_shape=[jax.ShapeDtypeStruct((POS_SIDE, POS_SIDE), F32),
                   jax.ShapeDtypeStruct((N_EXPERT_GROUPS, POS_SIDE, POS_SIDE), F32)],
        name="moe_plan",
    )(gi_f)
    gi = gi_f.reshape(N_TOK).astype(i32)
    groups = jnp.arange(N_EXPERT_GROUPS, dtype=i32)
    member = gi[None, :] == groups[:, None]
    tot = jnp.sum(member, axis=1).astype(i32)
    padded = (tot + tm - 1) // tm * tm
    gend = jnp.cumsum(padded).astype(i32)
    gstart = gend - padded
    pos = jnp.sum(jnp.where(member, gstart[:, None], 0), axis=0).astype(i32) + rank.reshape(N_TOK).astype(i32)
    rows_per_chunk = MOE_CHUNK // POS_SIDE
    cnt_end = jnp.concatenate([before[:, rows_per_chunk::rows_per_chunk, 0].astype(i32), tot[:, None]], axis=1)
    t = jnp.arange(MOE_TILES, dtype=i32)
    n_used = gend[-1] // tm
    tile_g = jnp.minimum(jnp.sum(gend[None, :] <= (t * tm)[:, None], axis=1), N_EXPERT_GROUPS - 1).astype(i32)
    k0 = t * tm - gstart[tile_g]
    k1 = jnp.minimum(k0 + tm, tot[tile_g]) - 1
    ce = cnt_end[tile_g]
    c_lo = jnp.sum(ce <= k0[:, None], axis=1).astype(i32)
    c_hi = jnp.sum(ce <= k1[:, None], axis=1).astype(i32)
    npairs = jnp.where(t < n_used, c_hi - c_lo + 1, 0)
    pend = jnp.cumsum(npairs).astype(i32)
    pstart = pend - npairs
    total = pend[-1]
    l = jnp.arange(MOE_PAIRS, dtype=i32)
    real = l < total
    lt = jnp.minimum(l, total - 1)
    tile_l = jnp.sum(pend[None, :] <= lt[:, None], axis=1).astype(i32)
    chunk_l = c_lo[tile_l] + lt - pstart[tile_l]
    spare_tile = jnp.minimum(n_used + (l - total), MOE_TILES - 1)
    flags = jnp.where(real, FLAG_ACTIVE + jnp.where(lt == pstart[tile_l], FLAG_FIRST, 0)
                      + jnp.where(lt == pend[tile_l] - 1, FLAG_LAST, 0),
                      jnp.where(spare_tile >= n_used, FLAG_ZERO, 0)).astype(i32)
    tile_sched = jnp.where(real, tile_l, spare_tile).astype(i32)
    by_tile = (tile_sched, chunk_l.astype(i32), flags, tile_g[tile_sched])
    cc = jnp.arange(MOE_CHUNKS, dtype=i32)
    is_pair = (cc[:, None] >= c_lo[None, :]) & (cc[:, None] <= c_hi[None, :]) & (t[None, :] < n_used)
    seen = jnp.cumsum(is_pair.reshape(-1).astype(i32))
    flat = jnp.sum(seen[None, :] <= lt[:, None], axis=1).astype(i32)
    chunk_c, tile_c = flat // MOE_TILES, flat % MOE_TILES
    prev_c = jnp.concatenate([jnp.full((1,), -1, i32), chunk_c[:-1]])
    next_c = jnp.concatenate([chunk_c[1:], jnp.full((1,), -1, i32)])
    flags_c = jnp.where(real, FLAG_ACTIVE + jnp.where(chunk_c != prev_c, FLAG_FIRST, 0)
                        + jnp.where((chunk_c != next_c) | (l == total - 1), FLAG_LAST, 0), 0).astype(i32)
    by_chunk = (tile_c, chunk_c, flags_c)
    return pos.reshape(MOE_CHUNKS, 1, MOE_CHUNK), by_tile, by_chunk


def _one_hot_rows(pos_row, tile):
    rows = tile * MOE_TM + lax.broadcasted_iota(jnp.int32, (MOE_TM, MOE_CHUNK), 0)
    return jnp.where(pos_row == rows, 1.0, 0.0).astype(BF16)


def _moe_kernel(tile_ref, chunk_ref, flag_ref, grp_ref, pos_ref, hx_ref, wg_ref, wu_ref, wd_ref, y_ref, acc_ref):
    l = pl.program_id(0)
    flags = flag_ref[l]

    @pl.when((flags & FLAG_FIRST) != 0)
    def _():
        acc_ref[...] = jnp.zeros_like(acc_ref)

    @pl.when((flags & FLAG_ACTIVE) != 0)
    def _():
        acc_ref[...] += jnp.dot(_one_hot_rows(pos_ref[0], tile_ref[l]), hx_ref[0], preferred_element_type=F32)

    @pl.when((flags & FLAG_LAST) != 0)
    def _():
        x = acc_ref[:, 0:D_MODEL].astype(BF16)
        w = functools.reduce(jnp.add, [acc_ref[:, D_MODEL + n * LANES:D_MODEL + (n + 1) * LANES] for n in range(3)])
        y = jnp.zeros((MOE_TM, D_MODEL), F32)
        for r in range(EXPERTS_PER_GROUP):
            gate = jnp.dot(x, wg_ref[0, 0, r], preferred_element_type=F32)
            up = jnp.dot(x, wu_ref[0, 0, r], preferred_element_type=F32)
            hid = (_silu(gate) * up * w[:, r:r + 1]).astype(BF16)
            y += jnp.dot(hid, wd_ref[0, 0, r], preferred_element_type=F32)
        y_ref[...] = y

    @pl.when((flags & FLAG_ZERO) != 0)
    def _():
        y_ref[...] = jnp.zeros_like(y_ref)


def moe_experts(hx, pos, by_tile, w_gate, w_up, w_down, layer):
    grouped = lambda w: w.reshape(DEPTH, N_EXPERT_GROUPS, EXPERTS_PER_GROUP, *w.shape[2:])
    wspec = lambda k, n: pl.BlockSpec((1, 1, EXPERTS_PER_GROUP, k, n), lambda l, t, c, f, g: (layer, g[l], 0, 0, 0))
    return pl.pallas_call(
        _moe_kernel,
        grid_spec=pltpu.PrefetchScalarGridSpec(
            num_scalar_prefetch=4,
            grid=(MOE_PAIRS,),
            in_specs=[pl.BlockSpec((1, 1, MOE_CHUNK), lambda l, t, c, f, g: (c[l], 0, 0)),
                      pl.BlockSpec((1, MOE_CHUNK, HX_W), lambda l, t, c, f, g: (c[l], 0, 0)),
                      wspec(D_MODEL, EXPERT_HIDDEN), wspec(D_MODEL, EXPERT_HIDDEN), wspec(EXPERT_HIDDEN, D_MODEL)],
            out_specs=pl.BlockSpec((MOE_TM, D_MODEL), lambda l, t, c, f, g: (t[l], 0)),
            scratch_shapes=[pltpu.VMEM((MOE_TM, HX_W), F32)],
        ),
        out_shape=jax.ShapeDtypeStruct((MOE_SORTED, D_MODEL), F32),
        compiler_params=_cparams(("arbitrary",), vmem_mib=56),
        name="moe_experts",
    )(*by_tile, pos, hx.reshape(MOE_CHUNKS, MOE_CHUNK, HX_W), grouped(w_gate), grouped(w_up), grouped(w_down))


def _moe_combine_kernel(tile_ref, chunk_ref, flag_ref, pos_ref, y_ref, x_ref, gate_ref, o_ref, acc_ref):
    l = pl.program_id(0)
    flags = flag_ref[l]

    @pl.when((flags & FLAG_FIRST) != 0)
    def _():
        acc_ref[...] = jnp.zeros_like(acc_ref)

    @pl.when((flags & FLAG_ACTIVE) != 0)
    def _():
        onehot = _one_hot_rows(pos_ref[0], tile_ref[l])
        y = y_ref[...]
        y_hi = y.astype(BF16)
        y_lo = (y - y_hi.astype(F32)).astype(BF16)
        acc_ref[...] += (lax.dot_general(onehot, y_hi, TN_DIMS, preferred_element_type=F32)
                         + lax.dot_general(onehot, y_lo, TN_DIMS, preferred_element_type=F32))

    @pl.when((flags & FLAG_LAST) != 0)
    def _():
        o_ref[0] = x_ref[0] + gate_ref[0] * acc_ref[...]


def moe_combine(x, y_sorted, pos, by_chunk, gate):
    per_b = SEQ // MOE_CHUNK
    tok = lambda l, t, c, f: (c[l] // per_b, c[l] % per_b, 0)
    return pl.pallas_call(
        _moe_combine_kernel,
        grid_spec=pltpu.PrefetchScalarGridSpec(
            num_scalar_prefetch=3,
            grid=(MOE_PAIRS,),
            in_specs=[pl.BlockSpec((1, 1, MOE_CHUNK), lambda l, t, c, f: (c[l], 0, 0)),
                      pl.BlockSpec((MOE_TM, D_MODEL), lambda l, t, c, f: (t[l], 0)),
                      pl.BlockSpec((1, MOE_CHUNK, D_MODEL), tok),
                      pl.BlockSpec((1, 1, D_MODEL), lambda l, t, c, f: (c[l] // per_b, 0, 0))],
            out_specs=pl.BlockSpec((1, MOE_CHUNK, D_MODEL), tok),
            scratch_shapes=[pltpu.VMEM((MOE_CHUNK, D_MODEL), F32)],
        ),
        out_shape=jax.ShapeDtypeStruct((BATCH, SEQ, D_MODEL), F32),
        compiler_params=_cparams(("arbitrary",)),
        name="moe_combine",
    )(*by_chunk, pos, y_sorted, x, gate.reshape(BATCH, 1, D_MODEL))


def moe_layer(x, hx, route, gate, w_gate, w_up, w_down, layer):
    pos, by_tile, by_chunk = moe_plan(route)
    y_sorted = moe_experts(hx, pos, by_tile, w_gate, w_up, w_down, layer)
    return moe_combine(x, y_sorted, pos, by_chunk, gate)


def kernel(x, c, positions, w_ada, b_ada, norm_g, w_in_ab, w_out_ab, gla_w_gate2, gla_b_gate, gla_norm_g, nsa_q_gain, nsa_k_gain, nsa_cmp_pe, nsa_cmp_w1, nsa_cmp_w2, w_in_c, sgu_norm_g, sgu_w_s, sgu_b_s, w_out_c, w_router, router_bias, w_gate, w_up, w_down):
    mod = ada_modulation(c, w_ada, b_ada)
    qk, gv, gr, nq, nkv, misc = inproj0(x, norm_g[0, 0], mod[0, :, 0], mod[0, :, 1], _arrange_w_in(w_in_ab[0]))
    o_a = gla_mixer(qk, gv, gr, misc, gla_w_gate2[0], gla_b_gate[0], gla_norm_g[0])
    o_b = nsa_mixer(nq, nkv, misc, positions, nsa_q_gain[0], nsa_k_gain[0], nsa_cmp_pe[0], nsa_cmp_w1[0], nsa_cmp_w2[0])
    wg, wu, wd = w_gate.astype(BF16), w_up.astype(BF16), w_down.astype(BF16)
    route_args = lambda l: _route_args(norm_g[l, 1], mod[l, :, 3], mod[l, :, 4], w_router, router_bias)
    x1, h, route = outproj0(o_a, o_b, w_out_ab[0], x, mod[0, :, 2], route_args(0))
    x2 = moe_layer(x1, h, route, mod[0, :, 5], wg, wu, wd, 0)
    x3, h, route = gmlp_layer(x2, norm_g[1, 0], mod[1, :, 0], mod[1, :, 1], w_in_c[0], sgu_norm_g[0], sgu_w_s[0],
                              sgu_b_s[0], w_out_c[0], mod[1, :, 2], route_args(1))
    return moe_layer(x3, h, route, mod[1, :, 5], wg, wu, wd, 1)
```

```python
import functools

import numpy as np
import jax
import jax.numpy as jnp
from jax import lax
from jax.experimental import pallas as pl
from jax.experimental.pallas import tpu as pltpu

D_MODEL = 1024
BATCH = 2
SEQ = 8192
DEPTH = 2
N_TOK = BATCH * SEQ

GLA_HEADS = 4
GLA_DK = 64
GLA_DV = 128
GLA_GATE_RANK = 16
GLA_TAU = 16.0
GLA_CHUNK = 64
NSA_HEADS = 8
NSA_KV_GROUPS = 2
NSA_HPG = NSA_HEADS // NSA_KV_GROUPS
NSA_DH = 64
CMP_LEN = 32
CMP_STRIDE = 16
CMP_HIDDEN = 256
SEL_BLOCK = 64
SEL_TOPK = 16
WINDOW = 512
ROPE_THETA = 500000.0
ROT_DIM = NSA_DH // 4
ROT_HALF = ROT_DIM // 2
SGU_CHUNK = 128
SGU_GROUPS = 8
SGU_WIDTH = 2048
SGU_GROUP_DIM = SGU_WIDTH // SGU_GROUPS
N_EXPERTS = 16
N_EXPERT_GROUPS = 4
EXPERTS_PER_GROUP = N_EXPERTS // N_EXPERT_GROUPS
MOE_TOPK = 2
EXPERT_HIDDEN = 512

GLA_QK_W = GLA_HEADS * GLA_DK
GLA_V_W = GLA_HEADS * GLA_DV
NSA_Q_W = NSA_HEADS * NSA_DH
NSA_KV_W = NSA_KV_GROUPS * NSA_DH
N_CMP = (SEQ - CMP_LEN) // CMP_STRIDE + 1
N_CMP_PAD = SEQ // CMP_STRIDE
N_SEL = SEQ // SEL_BLOCK

NORM_EPS = 1e-6
NEG_INF = -1e30
FORCE_BONUS = 1e4

LANES = 128
MIB = 1024 * 1024

F32 = jnp.float32
BF16 = jnp.bfloat16
HI = lax.Precision.HIGHEST
NT_DIMS = (((1,), (1,)), ((), ()))
TN_DIMS = (((0,), (0,)), ((), ()))


def _cparams(sem, vmem_mib=48):
    return pltpu.CompilerParams(dimension_semantics=sem, vmem_limit_bytes=vmem_mib * MIB)


def _rms_mod(x, g, shift, scale):
    y = x * lax.rsqrt(jnp.mean(x * x, axis=-1, keepdims=True) + NORM_EPS) * g
    return y * (1 + scale) + shift


def _silu(x):
    return x * jax.nn.sigmoid(x)


def _log_sigmoid(z):
    return jnp.minimum(z, 0.0) - jnp.log1p(jnp.exp(-jnp.abs(z)))


ADA_TN = 1536
ADA_ROWS = 8


def _ada_kernel(c_ref, w_ref, b_ref, o_ref):
    cond = _silu(c_ref[...])
    o_ref[0] = jnp.dot(cond, w_ref[0], precision=HI, preferred_element_type=F32) + b_ref[0]


def ada_modulation(c, w_ada, b_ada):
    c8 = jnp.zeros((ADA_ROWS, D_MODEL), F32).at[:BATCH].set(c)
    width = 6 * D_MODEL
    out = pl.pallas_call(
        _ada_kernel,
        grid=(DEPTH, width // ADA_TN),
        in_specs=[
            pl.BlockSpec((ADA_ROWS, D_MODEL), lambda l, j: (0, 0)),
            pl.BlockSpec((1, D_MODEL, ADA_TN), lambda l, j: (l, 0, j)),
            pl.BlockSpec((1, 1, ADA_TN), lambda l, j: (l, 0, j)),
        ],
        out_specs=pl.BlockSpec((1, ADA_ROWS, ADA_TN), lambda l, j: (l, 0, j)),
        out_shape=jax.ShapeDtypeStruct((DEPTH, ADA_ROWS, width), F32),
        compiler_params=_cparams(("arbitrary", "arbitrary")),
        name="ada_modulation",
    )(c8, w_ada, b_ada.reshape(DEPTH, 1, width))
    return out[:, :BATCH].reshape(DEPTH, BATCH, 6, D_MODEL)


INPROJ_TM = 512
INPROJ_WIDTHS = (2 * GLA_QK_W, GLA_V_W, GLA_V_W, NSA_Q_W, 6 * NSA_KV_W, LANES)


def _arrange_w_in(w_in):
    o = np.cumsum((0, GLA_QK_W, GLA_QK_W, GLA_V_W, GLA_GATE_RANK, GLA_V_W, NSA_Q_W, 6 * NSA_KV_W, NSA_HEADS * 3))
    gq_gk = w_in[:, o[0]:o[2]]
    gv = w_in[:, o[2]:o[3]]
    glr = w_in[:, o[3]:o[4]]
    gr = w_in[:, o[4]:o[5]]
    nq = w_in[:, o[5]:o[6]]
    nkv = w_in[:, o[6]:o[7]]
    ng = w_in[:, o[7]:o[8]]
    pad = jnp.zeros((D_MODEL, LANES - GLA_GATE_RANK - NSA_HEADS * 3), w_in.dtype)
    return jnp.concatenate([gq_gk, gv, gr, nq, nkv, glr, ng, pad], axis=1).astype(BF16)


def _inproj0_kernel(x_ref, g_ref, sh_ref, sc_ref, w_ref, *o_refs):
    h = _rms_mod(x_ref[0], g_ref[...], sh_ref[0], sc_ref[0]).astype(BF16)
    off = 0
    for o_ref, wd in zip(o_refs, INPROJ_WIDTHS):
        o_ref[0] = jnp.dot(h, w_ref[:, off:off + wd], preferred_element_type=F32)
        off += wd


def inproj0(x, g, shift, scale, w_arranged):
    tm = INPROJ_TM
    wtot = sum(INPROJ_WIDTHS)
    row = lambda b, i: (b, i, 0)
    vec = lambda b, i: (b, 0, 0)
    return pl.pallas_call(
        _inproj0_kernel,
        grid=(BATCH, SEQ // tm),
        in_specs=[
            pl.BlockSpec((1, tm, D_MODEL), row),
            pl.BlockSpec((1, D_MODEL), lambda b, i: (0, 0)),
            pl.BlockSpec((1, 1, D_MODEL), vec),
            pl.BlockSpec((1, 1, D_MODEL), vec),
            pl.BlockSpec((D_MODEL, wtot), lambda b, i: (0, 0)),
        ],
        out_specs=[pl.BlockSpec((1, tm, wd), row) for wd in INPROJ_WIDTHS],
        out_shape=[jax.ShapeDtypeStruct((BATCH, SEQ, wd), F32) for wd in INPROJ_WIDTHS],
        compiler_params=_cparams(("arbitrary", "arbitrary")),
        name="inproj0",
    )(x, g.reshape(1, D_MODEL), shift.reshape(BATCH, 1, D_MODEL), scale.reshape(BATCH, 1, D_MODEL), w_arranged)


GLA_TG = 512


def _gla_chunk_sums():
    i = np.arange(GLA_TG)[:, None]
    j = np.arange(GLA_TG)[None, :]
    same = (i // GLA_CHUNK) == (j // GLA_CHUNK)
    m3 = np.concatenate([same & (j <= i), same & (j % GLA_CHUNK <= GLA_CHUNK // 2), same], axis=0).astype(np.float32)
    return jnp.asarray(np.concatenate([m3, m3], axis=1), BF16)


def _gla_kernel(qk_ref, v_ref, r_ref, misc_ref, w2_ref, bg_ref, og_ref, sums_ref, o_ref, st_ref):
    C, tg = GLA_CHUNK, GLA_TG

    @pl.when(pl.program_id(1) == 0)
    def _():
        st_ref[...] = jnp.zeros_like(st_ref)

    z = jnp.dot(misc_ref[0], w2_ref[...], precision=HI, preferred_element_type=F32) + bg_ref[...]
    la = _log_sigmoid(z) / GLA_TAU
    la_hi = la.astype(BF16)
    la_lo = (la - la_hi.astype(F32)).astype(BF16)
    sums = jnp.dot(sums_ref[...], jnp.concatenate([la_hi, la_lo], axis=0), preferred_element_type=F32)
    bc, b_mid, b_last = sums[0:tg], sums[tg:2 * tg], sums[2 * tg:3 * tg]

    q = qk_ref[0, :, 0:GLA_QK_W] * (GLA_DK ** -0.5)
    k = qk_ref[0, :, GLA_QK_W:2 * GLA_QK_W]
    qd = q * jnp.exp(bc - b_mid)
    kd = (k * jnp.exp(b_mid - bc)).astype(BF16)
    kl = k * jnp.exp(b_last - bc)
    qb = q * jnp.exp(bc)
    dec = jnp.exp(b_last)
    lane = lax.broadcasted_iota(jnp.int32, (1, GLA_QK_W), 1)
    heads = [(lane >= h * GLA_DK) & (lane < (h + 1) * GLA_DK) for h in range(GLA_HEADS)]
    qd_h = [jnp.where(m, qd, 0.0).astype(BF16) for m in heads]
    qb_h = [jnp.where(m, qb, 0.0).astype(BF16) for m in heads]
    kl_h = [jnp.where(m, kl, 0.0).astype(BF16) for m in heads]
    stack = lambda per_head, rows: jnp.concatenate([t[rows] for t in per_head], axis=0)
    stacked_row = lax.broadcasted_iota(jnp.int32, (GLA_HEADS * C, C), 0)
    causal = (stacked_row & (C - 1)) >= lax.broadcasted_iota(jnp.int32, (GLA_HEADS * C, C), 1)
    og = og_ref[...]

    st = st_ref[...]
    for c in range(tg // C):
        rows = slice(c * C, (c + 1) * C)
        v = v_ref[0, rows, :].astype(BF16)
        s = lax.dot_general(stack(qd_h, rows), kd[rows], NT_DIMS, preferred_element_type=F32)
        s = jnp.where(causal, s, 0.0).astype(BF16)
        o_intra = jnp.dot(s, v, preferred_element_type=F32)
        o_inter = lax.dot_general(stack(qb_h, rows), st.astype(BF16), NT_DIMS, preferred_element_type=F32)
        v_stack = jnp.concatenate([v[:, h * GLA_DV:(h + 1) * GLA_DV] for h in range(GLA_HEADS)], axis=0)
        st = st * dec[c * C:c * C + 1] + lax.dot_general(v_stack, stack(kl_h, rows), TN_DIMS, preferred_element_type=F32)
        for h in range(GLA_HEADS):
            hrows = slice(h * C, (h + 1) * C)
            vcols = slice(h * GLA_DV, (h + 1) * GLA_DV)
            o = o_intra[hrows, vcols] + o_inter[hrows]
            on = o * lax.rsqrt(jnp.mean(o * o, axis=-1, keepdims=True) + NORM_EPS) * og
            o_ref[0, rows, vcols] = on * _silu(r_ref[0, rows, vcols])
    st_ref[...] = st


def gla_mixer(qk, v, r, misc, w_gate2, b_gate, out_g):
    tg = GLA_TG
    w2 = jnp.zeros((LANES, GLA_QK_W), F32).at[:GLA_GATE_RANK].set(w_gate2)
    row = lambda b, i: (b, i, 0)
    const = lambda b, i: (0, 0)
    return pl.pallas_call(
        _gla_kernel,
        grid=(BATCH, SEQ // tg),
        in_specs=[
            pl.BlockSpec((1, tg, 2 * GLA_QK_W), row),
            pl.BlockSpec((1, tg, GLA_V_W), row),
            pl.BlockSpec((1, tg, GLA_V_W), row),
            pl.BlockSpec((1, tg, LANES), row),
            pl.BlockSpec((LANES, GLA_QK_W), const),
            pl.BlockSpec((1, GLA_QK_W), const),
            pl.BlockSpec((1, GLA_DV), const),
            pl.BlockSpec((3 * tg, 2 * tg), const),
        ],
        out_specs=pl.BlockSpec((1, tg, GLA_V_W), row),
        out_shape=jax.ShapeDtypeStruct((BATCH, SEQ, GLA_V_W), F32),
        scratch_shapes=[pltpu.VMEM((GLA_DV, GLA_QK_W), F32)],
        compiler_params=_cparams(("arbitrary", "arbitrary")),
        name="gla_mixer",
    )(qk, v, r, misc, w2, b_gate.reshape(1, GLA_QK_W), out_g.reshape(1, GLA_DV), _gla_chunk_sums())


POS_SIDE = 128


def _rope_table_kernel(freq_ref, pos_ref, cos_ref, sin_ref):
    pos = pos_ref[...].astype(F32)
    for f in range(ROT_HALF):
        ang = pos * freq_ref[f]
        cos_ref[f] = jnp.cos(ang)
        sin_ref[f] = jnp.sin(ang)


def rope_tables(positions):
    inv_freq = jnp.float32(ROPE_THETA) ** (-jnp.arange(ROT_HALF, dtype=F32) / ROT_HALF)
    shp = jax.ShapeDtypeStruct((ROT_HALF, POS_SIDE, POS_SIDE), F32)
    cos, sin = pl.pallas_call(
        _rope_table_kernel,
        in_specs=[pl.BlockSpec(memory_space=pltpu.SMEM), pl.BlockSpec(memory_space=pltpu.VMEM)],
        out_specs=[pl.BlockSpec(memory_space=pltpu.VMEM)] * 2,
        out_shape=[shp, shp],
        name="rope_tables",
    )(inv_freq, positions.reshape(POS_SIDE, POS_SIDE))
    return jnp.concatenate([cos, sin], axis=0).reshape(ROT_DIM, N_TOK).T.reshape(BATCH, SEQ, ROT_DIM)


def _rope_placement():
    place = np.zeros((ROT_DIM, 3 * LANES), np.float32)
    const = np.zeros((1, 3 * LANES), np.float32)
    for lane in range(LANES):
        i = lane % NSA_DH
        if i < ROT_HALF:
            place[i, lane] = 1.0
            place[ROT_HALF + i, LANES + lane] = -1.0
        elif i < ROT_DIM:
            place[i - ROT_HALF, lane] = 1.0
            place[i, 2 * LANES + lane] = 1.0
        else:
            const[0, lane] = 1.0
    return jnp.asarray(place), jnp.asarray(const)


def _lane_tables(cs, place_ref, const_ref):
    tab = jnp.dot(cs, place_ref[...], precision=HI, preferred_element_type=F32) + const_ref[...]
    return tab[:, 0:LANES], tab[:, LANES:2 * LANES], tab[:, 2 * LANES:3 * LANES]


def _block_diag_ones2(width):
    h = np.arange(width) // NSA_DH
    bd = (h[:, None] == h[None, :]).astype(np.float32)
    return jnp.asarray(np.concatenate([bd, bd], axis=0), BF16)


def _head_norm_rope(x, gain, bd2, c, sm, sp):
    width = x.shape[-1]
    reps = width // LANES
    sq = x * x
    sq_hi = sq.astype(BF16)
    sq_lo = (sq - sq_hi.astype(F32)).astype(BF16)
    ss = jnp.dot(jnp.concatenate([sq_hi, sq_lo], axis=1), bd2, preferred_element_type=F32)
    y = x * lax.rsqrt(ss * (1.0 / NSA_DH) + NORM_EPS) * gain
    tile = lambda t: jnp.concatenate([t] * reps, axis=1) if reps > 1 else t
    return (y * tile(c) + pltpu.roll(y, width - ROT_HALF, 1) * tile(sm) + pltpu.roll(y, ROT_HALF, 1) * tile(sp))


PREP_TM = 512


def _prep_kernel(q_ref, ks_ref, kw_ref, cs_ref, place_ref, const_ref, gq_ref, gk_ref, bdq_ref, bdk_ref,
                 qo_ref, kso_ref, kwo_ref):
    c, sm, sp = _lane_tables(cs_ref[0], place_ref, const_ref)
    bdk = bdk_ref[...]
    qo_ref[0] = _head_norm_rope(q_ref[0], gq_ref[...], bdq_ref[...], c, sm, sp) * (NSA_DH ** -0.5)
    kso_ref[0] = _head_norm_rope(ks_ref[0], gk_ref[0:1, :], bdk, c, sm, sp)
    kwo_ref[0] = _head_norm_rope(kw_ref[0], gk_ref[1:2, :], bdk, c, sm, sp)


def nsa_prep(nq, nkv, cs, q_gain, k_gain):
    tm = PREP_TM
    row = lambda b, i: (b, i, 0)
    const = lambda b, i: (0, 0)
    gq = jnp.tile(q_gain, NSA_HEADS).reshape(1, NSA_Q_W)
    gk = jnp.stack([jnp.tile(k_gain[1], NSA_KV_GROUPS), jnp.tile(k_gain[2], NSA_KV_GROUPS)])
    return pl.pallas_call(
        _prep_kernel,
        grid=(BATCH, SEQ // tm),
        in_specs=[
            pl.BlockSpec((1, tm, NSA_Q_W), row),
            pl.BlockSpec((1, tm, NSA_KV_W), lambda b, i: (b, i, 2)),
            pl.BlockSpec((1, tm, NSA_KV_W), lambda b, i: (b, i, 4)),
            pl.BlockSpec((1, tm, ROT_DIM), row),
            pl.BlockSpec((ROT_DIM, 3 * LANES), const),
            pl.BlockSpec((1, 3 * LANES), const),
            pl.BlockSpec((1, NSA_Q_W), const),
            pl.BlockSpec((2, NSA_KV_W), const),
            pl.BlockSpec((2 * NSA_Q_W, NSA_Q_W), const),
            pl.BlockSpec((2 * NSA_KV_W, NSA_KV_W), const),
        ],
        out_specs=[pl.BlockSpec((1, tm, NSA_Q_W), row), pl.BlockSpec((1, tm, NSA_KV_W), row),
                   pl.BlockSpec((1, tm, NSA_KV_W), row)],
        out_shape=[jax.ShapeDtypeStruct((BATCH, SEQ, NSA_Q_W), F32), jax.ShapeDtypeStruct((BATCH, SEQ, NSA_KV_W), F32),
                   jax.ShapeDtypeStruct((BATCH, SEQ, NSA_KV_W), F32)],
        compiler_params=_cparams(("arbitrary", "arbitrary")),
        name="nsa_prep",
    )(nq, nkv, nkv, cs, *_rope_placement(), gq, gk, _block_diag_ones2(NSA_Q_W), _block_diag_ones2(NSA_KV_W))


SEG_W = CMP_STRIDE * NSA_DH


def _cmp_kernel(xk_ref, xv_ref, pe_ref, w1_ref, w2_ref, gain_ref, cs_ref, place_ref, const_ref, bd_ref, ko_ref, vo_ref):
    def compress(x_ref, kv):
        out = jnp.zeros((N_CMP_PAD, LANES), F32)
        for g in range(NSA_KV_GROUPS):
            x = x_ref[0, g]
            ha = jnp.dot(x + pe_ref[kv, 0], w1_ref[kv, 0:SEG_W, :], precision=HI, preferred_element_type=F32)
            hb = jnp.dot(x + pe_ref[kv, 1], w1_ref[kv, SEG_W:2 * SEG_W, :], precision=HI, preferred_element_type=F32)
            hid = ha + pltpu.roll(hb, N_CMP_PAD - 1, 0)
            out += jnp.dot(jax.nn.gelu(hid), w2_ref[kv, g], precision=HI, preferred_element_type=F32)
        return out

    c, sm, sp = _lane_tables(cs_ref[0], place_ref, const_ref)
    ko_ref[0] = _head_norm_rope(compress(xk_ref, 0), gain_ref[...], bd_ref[...], c, sm, sp)
    vo_ref[0] = compress(xv_ref, 1)


def nsa_compress(xk, xv, cmp_pe, cmp_w1, cmp_w2, k_gain0, cs_last):
    pe = cmp_pe.reshape(2, 2, 1, SEG_W)
    w2 = jnp.zeros((2, NSA_KV_GROUPS, CMP_HIDDEN, LANES), F32)
    for g in range(NSA_KV_GROUPS):
        w2 = w2.at[:, g, :, g * NSA_DH:(g + 1) * NSA_DH].set(cmp_w2)
    seg = pl.BlockSpec((1, NSA_KV_GROUPS, N_CMP_PAD, SEG_W), lambda b: (b, 0, 0, 0))
    tab = pl.BlockSpec((1, N_CMP_PAD, LANES), lambda b: (b, 0, 0))
    full = lambda shape: pl.BlockSpec(shape, lambda b: (0,) * len(shape))
    return pl.pallas_call(
        _cmp_kernel,
        grid=(BATCH,),
        in_specs=[seg, seg, full((2, 2, 1, SEG_W)), full((2, 2 * SEG_W, CMP_HIDDEN)),
                  full((2, NSA_KV_GROUPS, CMP_HIDDEN, LANES)), full((1, LANES)),
                  pl.BlockSpec((1, N_CMP_PAD, ROT_DIM), lambda b: (b, 0, 0)), full((ROT_DIM, 3 * LANES)),
                  full((1, 3 * LANES)), full((2 * LANES, LANES))],
        out_specs=[tab, tab],
        out_shape=[jax.ShapeDtypeStruct((BATCH, N_CMP_PAD, LANES), F32)] * 2,
        compiler_params=_cparams(("arbitrary",)),
        name="nsa_compress",
    )(xk, xv, pe, cmp_w1, w2, jnp.tile(k_gain0, NSA_KV_GROUPS).reshape(1, LANES), cs_last, *_rope_placement(),
      _block_diag_ones2(LANES))


CA_TQ = 256
SUBLANES = 8


CA_COLS = NSA_HPG * CA_TQ
CMP_PER_SEL = SEL_BLOCK // CMP_STRIDE


def split3_keys(k):
    hi = k.astype(BF16)
    lo = (k - hi.astype(F32)).astype(BF16)
    return jnp.concatenate([hi, lo, hi], axis=-1)


def _cattn_kernel(q_ref, kc_ref, vct_ref, gl_ref, o_ref, sel_ref, q3_ref, ps_ref):
    tq = CA_TQ
    q0 = pl.program_id(2) * tq
    lanes4 = lambda t: jnp.concatenate([t] * NSA_HPG, axis=1)
    for h in range(NSA_HPG):
        q = q_ref[0, h]
        hi = q.astype(BF16)
        lo = (q - hi.astype(F32)).astype(BF16)
        for n, part in enumerate((hi, hi, lo)):
            q3_ref[n * NSA_DH:(n + 1) * NSA_DH, h * tq:(h + 1) * tq] = part
    s = jnp.dot(kc_ref[0, 0], q3_ref[...], preferred_element_type=F32)
    cend = lax.broadcasted_iota(jnp.int32, (N_CMP_PAD, tq), 0) * CMP_STRIDE + (CMP_LEN - 1)
    tc = q0 + lax.broadcasted_iota(jnp.int32, (N_CMP_PAD, tq), 1)
    cmask = lanes4(cend <= tc)
    s = jnp.where(cmask, s, NEG_INF)
    m = jnp.max(s, axis=0, keepdims=True)
    e = jnp.where(cmask, jnp.exp(s - m), 0.0)
    l = jnp.sum(e, axis=0, keepdims=True)
    p = e / jnp.where(l > 0.0, l, 1.0)
    gate = jnp.concatenate([jax.nn.sigmoid(gl_ref[0, h, 0:1, :]) for h in range(NSA_HPG)], axis=1)
    o = jnp.dot(vct_ref[0, 0], p.astype(BF16), preferred_element_type=F32) * gate
    for h in range(NSA_HPG):
        o_ref[0, h] = o[:, h * tq:(h + 1) * tq]
    psum = functools.reduce(jnp.add, [p[:, h * tq:(h + 1) * tq] for h in range(NSA_HPG)])
    for n in range(tq // LANES):
        ps_ref[n] = psum[:, n * LANES:(n + 1) * LANES]

    jj = lax.broadcasted_iota(jnp.int32, (N_SEL, tq), 0)
    every4th = lambda r: jnp.concatenate([ps_ref[n, pl.ds(r, N_SEL, stride=CMP_PER_SEL), :] for n in range(tq // LANES)],
                                         axis=1)
    starts_in = [every4th(r) for r in range(CMP_PER_SEL)]
    from_prev = jnp.where(jj >= 1, pltpu.roll(starts_in[CMP_PER_SEL - 1], 1, 0), 0.0)
    imp = functools.reduce(jnp.add, starts_in) + from_prev
    tt = q0 + lax.broadcasted_iota(jnp.int32, (N_SEL, tq), 1)
    cur = jnp.right_shift(tt, 6)
    forced = (jj == 0) | (jj == cur) | (jj == cur - 1)
    valid = jj * SEL_BLOCK <= tt
    score = jnp.where(valid, imp + jnp.where(forced, FORCE_BONUS, 0.0), NEG_INF)

    n_slab = N_SEL // SUBLANES
    sc = [score[SUBLANES * a:SUBLANES * (a + 1)] for a in range(n_slab)]
    rk = [jnp.zeros((SUBLANES, tq), F32) for _ in range(n_slab)]
    sub = lax.broadcasted_iota(jnp.int32, (SUBLANES, tq), 0)
    for jp in range(N_SEL):
        a0, r0 = divmod(jp, SUBLANES)
        row = sc[a0][r0:r0 + 1]
        for a in range(n_slab):
            gt = jnp.where(row > sc[a], 1.0, 0.0)
            ge = jnp.where(row >= sc[a], 1.0, 0.0)
            if a < a0:
                rk[a] = rk[a] + gt
            elif a > a0:
                rk[a] = rk[a] + ge
            else:
                rk[a] = rk[a] + jnp.where(sub > r0, ge, gt)
    for a in range(n_slab):
        rows = slice(SUBLANES * a, SUBLANES * (a + 1))
        sel_ref[0, 0, rows, :] = jnp.where((rk[a] < SEL_TOPK) & valid[rows], 1.0, 0.0)


def nsa_cmp_attn(q_t, kcmp, vcmp_t, gl_t):
    tq = CA_TQ
    return pl.pallas_call(
        _cattn_kernel,
        grid=(BATCH, NSA_KV_GROUPS, SEQ // tq),
        in_specs=[
            pl.BlockSpec((1, NSA_HPG, NSA_DH, tq), lambda b, g, i: (b, g, 0, i)),
            pl.BlockSpec((1, 1, N_CMP_PAD, 3 * NSA_DH), lambda b, g, i: (b, g, 0, 0)),
            pl.BlockSpec((1, 1, NSA_DH, N_CMP_PAD), lambda b, g, i: (b, g, 0, 0)),
            pl.BlockSpec((1, NSA_HPG, 3, tq), lambda b, g, i: (b, g, 0, i)),
        ],
        out_specs=[pl.BlockSpec((1, NSA_HPG, NSA_DH, tq), lambda b, g, i: (b, g, 0, i)),
                   pl.BlockSpec((1, 1, N_SEL, tq), lambda b, g, i: (b, g, 0, i))],
        out_shape=[jax.ShapeDtypeStruct((BATCH, NSA_HEADS, NSA_DH, SEQ), F32),
                   jax.ShapeDtypeStruct((BATCH, NSA_KV_GROUPS, N_SEL, SEQ), F32)],
        scratch_shapes=[pltpu.VMEM((3 * NSA_DH, CA_COLS), BF16), pltpu.VMEM((tq // LANES, N_CMP_PAD, LANES), F32)],
        compiler_params=_cparams(("arbitrary", "arbitrary", "arbitrary")),
        name="nsa_cmp_attn",
    )(q_t, split3_keys(kcmp), vcmp_t.astype(BF16), gl_t)


SA_TQ = 256
SA_TK = 1024
SA_PARTS = 2
SA_PART = SA_TK // SA_PARTS
M_INIT = -1e20


SA_COLS = NSA_HPG * SA_TQ
SA_BLOCKS = SA_TK // SEL_BLOCK


VT_ROWS = NSA_DH + 16


def value_slab_t(v_t):
    ones = jnp.ones(v_t.shape[:2] + (1, SEQ), BF16)
    zeros = jnp.zeros(v_t.shape[:2] + (VT_ROWS - NSA_DH - 1, SEQ), BF16)
    return jnp.concatenate([v_t.astype(BF16), ones, zeros], axis=2)


def sel_key_slab(ksel):
    blk = (np.arange(SEQ) % SA_TK) // SEL_BLOCK
    onehot = (blk[:, None] == np.arange(LANES - NSA_DH)[None, :]).astype(np.float32)
    onehot = jnp.broadcast_to(jnp.asarray(onehot, BF16), ksel.shape[:3] + (LANES - NSA_DH,))
    return jnp.concatenate([ksel.astype(BF16), onehot], axis=-1)


def _sattn_kernel(q_ref, k_ref, vt_ref, sel_ref, gl_ref, prev_ref, o_ref, qa_ref, acc_ref):
    tq, tk = SA_TQ, SA_TK
    i = pl.program_id(2)
    for h in range(NSA_HPG):
        qa_ref[0:NSA_DH, h * tq:(h + 1) * tq] = q_ref[0, h]
    qa_ref[NSA_DH:LANES, :] = jnp.zeros((LANES - NSA_DH, SA_COLS), BF16)
    acc_ref[...] = jnp.zeros_like(acc_ref)
    lanes4 = lambda t: jnp.concatenate([t] * NSA_HPG, axis=1)

    def key_tile(kt, m_prev, diagonal):
        selrows = sel_ref[0, 0, pl.ds(pl.multiple_of(kt * SA_BLOCKS, SA_BLOCKS), SA_BLOCKS), :]
        qa_ref[NSA_DH:NSA_DH + SA_BLOCKS, :] = lanes4(jnp.where(selrows > 0.5, 0.0, NEG_INF)).astype(BF16)
        ss = []
        for part in range(SA_PARTS):
            keys = pl.ds(pl.multiple_of(kt * tk + part * SA_PART, SA_PART), SA_PART)
            s = jnp.dot(k_ref[0, 0, keys, :], qa_ref[...], preferred_element_type=F32)
            if diagonal:
                kpos = kt * tk + part * SA_PART + lax.broadcasted_iota(jnp.int32, (SA_PART, tq), 0)
                tt = i * tq + lax.broadcasted_iota(jnp.int32, (SA_PART, tq), 1)
                s = s + lanes4(jnp.where(kpos <= tt, 0.0, NEG_INF))
            ss.append(s.astype(BF16))
        m_tile = functools.reduce(jnp.maximum, [jnp.max(s, axis=0, keepdims=True) for s in ss])
        m_new = jnp.maximum(m_prev, m_tile.astype(F32))
        acc = jnp.exp(m_prev - m_new) * acc_ref[...]
        for part in range(SA_PARTS):
            keys = pl.ds(pl.multiple_of(kt * tk + part * SA_PART, SA_PART), SA_PART)
            p = jnp.exp(ss[part] - m_new.astype(BF16))
            acc += jnp.dot(vt_ref[0, 0, :, keys], p, preferred_element_type=F32)
        acc_ref[...] = acc
        return m_new

    n_full = (i * tq) // tk
    m = lax.fori_loop(0, n_full, lambda kt, m: key_tile(kt, m, False), jnp.full((1, SA_COLS), M_INIT, F32))
    key_tile(n_full, m, True)
    gate = jnp.concatenate([jax.nn.sigmoid(gl_ref[0, h, 1:2, :]) for h in range(NSA_HPG)], axis=1)
    out = acc_ref[0:NSA_DH, :] / acc_ref[NSA_DH:NSA_DH + 1, :] * gate
    for h in range(NSA_HPG):
        o_ref[0, h] = prev_ref[0, h] + out[:, h * tq:(h + 1) * tq]


def nsa_sel_attn(q_t, k_slab, vsel_t, sel_t, gl_t, prev):
    tq = SA_TQ
    ospec = pl.BlockSpec((1, NSA_HPG, NSA_DH, tq), lambda b, g, i: (b, g, 0, i))
    return pl.pallas_call(
        _sattn_kernel,
        grid=(BATCH, NSA_KV_GROUPS, SEQ // tq),
        in_specs=[
            ospec,
            pl.BlockSpec((1, 1, SEQ, LANES), lambda b, g, i: (b, g, 0, 0)),
            pl.BlockSpec((1, 1, VT_ROWS, SEQ), lambda b, g, i: (b, g, 0, 0)),
            pl.BlockSpec((1, 1, N_SEL, tq), lambda b, g, i: (b, g, 0, i)),
            pl.BlockSpec((1, NSA_HPG, 3, tq), lambda b, g, i: (b, g, 0, i)),
            ospec,
        ],
        out_specs=ospec,
        out_shape=jax.ShapeDtypeStruct((BATCH, NSA_HEADS, NSA_DH, SEQ), F32),
        scratch_shapes=[pltpu.VMEM((LANES, SA_COLS), BF16), pltpu.VMEM((VT_ROWS, SA_COLS), F32)],
        input_output_aliases={5: 0},
        compiler_params=_cparams(("arbitrary", "arbitrary", "arbitrary")),
        name="nsa_sel_attn",
    )(q_t, k_slab, vsel_t, sel_t, gl_t, prev)


WA_TQ = 256
WA_TILES = WINDOW // WA_TQ + 1


def _window_bias():
    kl = np.arange(WA_TILES * WA_TQ)[:, None]
    ql = np.arange(WA_TQ)[None, :]
    diff = ql - kl + WINDOW
    return jnp.asarray(np.where((diff >= 0) & (diff < WINDOW), 0.0, NEG_INF).astype(np.float32))


def _wattn_kernel(q_ref, k0_ref, k1_ref, k2_ref, v0_ref, v1_ref, v2_ref, bias_ref, gl_ref, prev_ref, o_ref):
    tq = WA_TQ
    i = pl.program_id(2)
    k_refs = (k0_ref, k1_ref, k2_ref)
    v_refs = (v0_ref, v1_ref, v2_ref)
    lanes4 = lambda t: jnp.concatenate([t] * NSA_HPG, axis=1)
    q = jnp.concatenate([q_ref[0, h] for h in range(NSA_HPG)], axis=1)
    ss = []
    for d in range(WA_TILES):
        in_seq = i - (WA_TILES - 1) + d >= 0
        bias = jnp.where(in_seq, bias_ref[d * tq:(d + 1) * tq, :], NEG_INF)
        ss.append((jnp.dot(k_refs[d][0, 0], q, preferred_element_type=F32) + lanes4(bias)).astype(BF16))
    m = functools.reduce(jnp.maximum, [jnp.max(s, axis=0, keepdims=True) for s in ss])
    acc = functools.reduce(jnp.add, [jnp.dot(v_refs[d][0, 0], jnp.exp(ss[d] - m), preferred_element_type=F32)
                                     for d in range(WA_TILES)])
    gate = jnp.concatenate([jax.nn.sigmoid(gl_ref[0, h, 2:3, :]) for h in range(NSA_HPG)], axis=1)
    out = acc[0:NSA_DH] / acc[NSA_DH:NSA_DH + 1] * gate
    for h in range(NSA_HPG):
        o_ref[0, h] = prev_ref[0, h] + out[:, h * tq:(h + 1) * tq]


def nsa_win_attn(q_t, kwin, vwin_t, gl_t, prev):
    tq = WA_TQ
    qspec = pl.BlockSpec((1, NSA_HPG, NSA_DH, tq), lambda b, g, i: (b, g, 0, i))
    tile = lambda d: (lambda i: jnp.maximum(i - (WA_TILES - 1) + d, 0))
    kspec = lambda d: pl.BlockSpec((1, 1, tq, NSA_DH), lambda b, g, i: (b, g, tile(d)(i), 0))
    vspec = lambda d: pl.BlockSpec((1, 1, VT_ROWS, tq), lambda b, g, i: (b, g, 0, tile(d)(i)))
    return pl.pallas_call(
        _wattn_kernel,
        grid=(BATCH, NSA_KV_GROUPS, SEQ // tq),
        in_specs=[qspec] + [kspec(d) for d in range(WA_TILES)] + [vspec(d) for d in range(WA_TILES)] + [
            pl.BlockSpec((WA_TILES * tq, tq), lambda b, g, i: (0, 0)),
            pl.BlockSpec((1, NSA_HPG, 3, tq), lambda b, g, i: (b, g, 0, i)),
            qspec,
        ],
        out_specs=qspec,
        out_shape=jax.ShapeDtypeStruct((BATCH, NSA_HEADS, NSA_DH, SEQ), F32),
        input_output_aliases={2 * WA_TILES + 3: 0},
        compiler_params=_cparams(("arbitrary", "arbitrary", "arbitrary")),
        name="nsa_win_attn",
    )(q_t, *([kwin] * WA_TILES), *([vwin_t] * WA_TILES), _window_bias(), gl_t, prev)


def nsa_mixer(nq, nkv, misc, positions, q_gain, k_gain, cmp_pe, cmp_w1, cmp_w2):
    cs = rope_tables(positions)
    q_r, ks_r, kw_r = nsa_prep(nq, nkv, cs, q_gain, k_gain)
    group_major = lambda t: t.reshape(BATCH, SEQ, NSA_KV_GROUPS, NSA_DH).transpose(0, 2, 1, 3)
    group_major_t = lambda t: t.reshape(BATCH, SEQ, NSA_KV_GROUPS, NSA_DH).transpose(0, 2, 3, 1)
    col = lambda n: nkv[..., n * NSA_KV_W:(n + 1) * NSA_KV_W]
    segs = lambda t: group_major(t).reshape(BATCH, NSA_KV_GROUPS, N_CMP_PAD, SEG_W)
    last = jnp.minimum(jnp.arange(N_CMP_PAD) * CMP_STRIDE + CMP_LEN - 1, SEQ - 1)
    kcmp, vcmp = nsa_compress(segs(col(0)), segs(col(1)), cmp_pe, cmp_w1, cmp_w2, k_gain[0], cs[:, last])
    kcmp = kcmp.reshape(BATCH, N_CMP_PAD, NSA_KV_GROUPS, NSA_DH).transpose(0, 2, 1, 3)
    vcmp_t = vcmp.reshape(BATCH, N_CMP_PAD, NSA_KV_GROUPS, NSA_DH).transpose(0, 2, 3, 1)
    q_t = q_r.reshape(BATCH, SEQ, NSA_HEADS, NSA_DH).transpose(0, 2, 3, 1)
    gl_t = misc[..., GLA_GATE_RANK:GLA_GATE_RANK + NSA_HEADS * 3].reshape(BATCH, SEQ, NSA_HEADS, 3).transpose(0, 2, 3, 1)
    o_t, sel_t = nsa_cmp_attn(q_t, kcmp, vcmp_t, gl_t)
    q_t16 = q_t.astype(BF16)
    o_t = nsa_sel_attn(q_t16, sel_key_slab(group_major(ks_r)), value_slab_t(group_major_t(col(3))), sel_t, gl_t, o_t)
    o_t = nsa_win_attn(q_t16, group_major(kw_r).astype(BF16), value_slab_t(group_major_t(col(5))), gl_t, o_t)
    return o_t.transpose(0, 3, 1, 2).reshape(BATCH, SEQ, NSA_Q_W)


ROUTE_ROWS = 8
HX_W = D_MODEL + 3 * LANES


def _top2_sum(a, b, c, d):
    hi1, lo1 = jnp.maximum(a, b), jnp.minimum(a, b)
    hi2, lo2 = jnp.maximum(c, d), jnp.minimum(c, d)
    return jnp.maximum(hi1, hi2) + jnp.maximum(jnp.minimum(hi1, hi2), jnp.maximum(lo1, lo2))


def _moe_prenorm_route(xn, g_ref, sh_ref, sc_ref, wr_ref, rb_ref, hx_ref, route_ref):
    h = _rms_mod(xn, g_ref[...], sh_ref[0], sc_ref[0])
    logits = lax.dot_general(wr_ref[...], h, NT_DIMS, precision=HI, preferred_element_type=F32)
    scores = jax.nn.sigmoid(logits)
    sel = scores + rb_ref[...]
    epg = EXPERTS_PER_GROUP
    srow = lambda e: sel[e:e + 1, :]
    grp = [_top2_sum(*[srow(epg * g + r) for r in range(epg)]) for g in range(N_EXPERT_GROUPS)]
    best, gi = grp[0], jnp.zeros_like(grp[0], dtype=jnp.int32)
    for g in range(1, N_EXPERT_GROUPS):
        better = grp[g] > best
        gi = jnp.where(better, g, gi)
        best = jnp.where(better, grp[g], best)

    def in_group(mat, r):
        out = mat[r:r + 1, :]
        for g in range(1, N_EXPERT_GROUPS):
            out = jnp.where(gi == g, mat[epg * g + r:epg * g + r + 1, :], out)
        return out

    v = [in_group(sel, r) for r in range(epg)]
    sc = [in_group(scores, r) for r in range(epg)]
    b1, i1, w1 = v[0], jnp.zeros_like(gi), sc[0]
    for r in range(1, epg):
        better = v[r] > b1
        i1 = jnp.where(better, r, i1)
        w1 = jnp.where(better, sc[r], w1)
        b1 = jnp.where(better, v[r], b1)
    b2 = jnp.full_like(b1, -3e38)
    i2, w2 = jnp.zeros_like(gi), jnp.zeros_like(w1)
    for r in range(epg):
        better = (i1 != r) & (v[r] > b2)
        i2 = jnp.where(better, r, i2)
        w2 = jnp.where(better, sc[r], w2)
        b2 = jnp.where(better, v[r], b2)
    tot = w1 + w2
    w1, w2 = w1 / tot, w2 / tot
    zero = jnp.zeros_like(w1)
    route_ref[0] = jnp.concatenate([gi.astype(F32)] + [zero] * (ROUTE_ROWS - 1), axis=0)
    wrows = [jnp.where(i1 == r, w1, jnp.where(i2 == r, w2, 0.0)) for r in range(epg)]
    wmat = jnp.concatenate(wrows + [jnp.zeros((LANES - epg, w1.shape[1]), F32)], axis=0).T
    w_hi = wmat.astype(BF16)
    rest = wmat - w_hi.astype(F32)
    w_mid = rest.astype(BF16)
    w_lo = (rest - w_mid.astype(F32)).astype(BF16)
    hx_ref[0, :, 0:D_MODEL] = h.astype(BF16)
    for n, part in enumerate((w_hi, w_mid, w_lo)):
        hx_ref[0, :, D_MODEL + n * LANES:D_MODEL + (n + 1) * LANES] = part


def _route_specs(tm, row, vec, const):
    in_specs = [pl.BlockSpec((1, D_MODEL), const), pl.BlockSpec((1, 1, D_MODEL), vec), pl.BlockSpec((1, 1, D_MODEL), vec),
                pl.BlockSpec((N_EXPERTS, D_MODEL), const), pl.BlockSpec((N_EXPERTS, 1), const)]
    out_specs = [pl.BlockSpec((1, tm, HX_W), row), pl.BlockSpec((1, ROUTE_ROWS, tm), lambda b, i: (b, 0, i))]
    out_shape = [jax.ShapeDtypeStruct((BATCH, SEQ, HX_W), BF16), jax.ShapeDtypeStruct((BATCH, ROUTE_ROWS, SEQ), F32)]
    return in_specs, out_specs, out_shape


def _route_args(g, shift, scale, w_router, router_bias):
    return (g.reshape(1, D_MODEL), shift.reshape(BATCH, 1, D_MODEL), scale.reshape(BATCH, 1, D_MODEL),
            w_router.T, router_bias.reshape(N_EXPERTS, 1))


OUTPROJ_TM = 512


def _outproj0_kernel(oa_ref, ob_ref, w_ref, x_ref, gate_ref, g_ref, sh_ref, sc_ref, wr_ref, rb_ref,
                     xo_ref, h_ref, route_ref):
    y = jnp.dot(oa_ref[0].astype(BF16), w_ref[0:GLA_V_W, :], preferred_element_type=F32)
    y += jnp.dot(ob_ref[0].astype(BF16), w_ref[GLA_V_W:GLA_V_W + NSA_Q_W, :], preferred_element_type=F32)
    xn = x_ref[0] + gate_ref[0] * y
    xo_ref[0] = xn
    _moe_prenorm_route(xn, g_ref, sh_ref, sc_ref, wr_ref, rb_ref, h_ref, route_ref)


def outproj0(o_a, o_b, w_out, x, gate, route_args):
    tm = OUTPROJ_TM
    row = lambda b, i: (b, i, 0)
    vec = lambda b, i: (b, 0, 0)
    const = lambda b, i: (0, 0)
    r_in, r_out, r_shape = _route_specs(tm, row, vec, const)
    return pl.pallas_call(
        _outproj0_kernel,
        grid=(BATCH, SEQ // tm),
        in_specs=[pl.BlockSpec((1, tm, GLA_V_W), row), pl.BlockSpec((1, tm, NSA_Q_W), row),
                  pl.BlockSpec((GLA_V_W + NSA_Q_W, D_MODEL), const), pl.BlockSpec((1, tm, D_MODEL), row),
                  pl.BlockSpec((1, 1, D_MODEL), vec)] + r_in,
        out_specs=[pl.BlockSpec((1, tm, D_MODEL), row)] + r_out,
        out_shape=[jax.ShapeDtypeStruct((BATCH, SEQ, D_MODEL), F32)] + r_shape,
        compiler_params=_cparams(("arbitrary", "arbitrary")),
        name="outproj0",
    )(o_a, o_b, w_out.astype(BF16), x, gate.reshape(BATCH, 1, D_MODEL), *route_args)


GMLP_TM = 512


def _gmlp_kernel(x_ref, g1_ref, sh1_ref, sc1_ref, win_ref, ng_ref, ws_ref, bs_ref, wout_ref, gate_ref,
                 g_ref, sh_ref, sc_ref, wr_ref, rb_ref, xo_ref, h_ref, route_ref, gated_ref, v_ref):
    x = x_ref[0]
    h = _rms_mod(x, g1_ref[...], sh1_ref[0], sc1_ref[0]).astype(BF16)
    group_cols = lambda g: slice(g * SGU_GROUP_DIM, (g + 1) * SGU_GROUP_DIM)
    ssq = jnp.zeros((GMLP_TM, LANES), F32)
    for g in range(SGU_GROUPS):
        lo = SGU_WIDTH + g * SGU_GROUP_DIM
        v = jax.nn.gelu(jnp.dot(h, win_ref[:, lo:lo + SGU_GROUP_DIM], preferred_element_type=F32))
        v_ref[:, group_cols(g)] = v
        ssq += functools.reduce(jnp.add, [v[:, n * LANES:(n + 1) * LANES] ** 2 for n in range(SGU_GROUP_DIM // LANES)])
    rs = lax.rsqrt(jnp.sum(ssq, axis=-1, keepdims=True) * (1.0 / SGU_WIDTH) + NORM_EPS)
    ri = lax.broadcasted_iota(jnp.int32, (SGU_CHUNK, SGU_CHUNK), 0)
    ci = lax.broadcasted_iota(jnp.int32, (SGU_CHUNK, SGU_CHUNK), 1)
    for g in range(SGU_GROUPS):
        cols = group_cols(g)
        u = jax.nn.gelu(jnp.dot(h, win_ref[:, cols], preferred_element_type=F32))
        vn = (v_ref[:, cols] * rs * ng_ref[:, cols]).astype(BF16)
        w = jnp.where(ri >= ci, ws_ref[g], 0.0).astype(BF16)
        for c in range(GMLP_TM // SGU_CHUNK):
            rows = slice(c * SGU_CHUNK, (c + 1) * SGU_CHUNK)
            mix = jnp.dot(w, vn[rows], preferred_element_type=F32) + bs_ref[:, g:g + 1]
            gated_ref[rows, cols] = (u[rows] * mix).astype(BF16)
    y = jnp.dot(gated_ref[...], wout_ref[...], preferred_element_type=F32)
    xn = x + gate_ref[0] * y
    xo_ref[0] = xn
    _moe_prenorm_route(xn, g_ref, sh_ref, sc_ref, wr_ref, rb_ref, h_ref, route_ref)


def gmlp_layer(x, g1, shift1, scale1, w_in, norm_g, w_s, b_s, w_out, gate, route_args):
    tm = GMLP_TM
    row = lambda b, i: (b, i, 0)
    vec = lambda b, i: (b, 0, 0)
    const = lambda b, i: (0, 0)
    r_in, r_out, r_shape = _route_specs(tm, row, vec, const)
    vspec = pl.BlockSpec((1, 1, D_MODEL), vec)
    return pl.pallas_call(
        _gmlp_kernel,
        grid=(BATCH, SEQ // tm),
        in_specs=[pl.BlockSpec((1, tm, D_MODEL), row), pl.BlockSpec((1, D_MODEL), const), vspec, vspec,
                  pl.BlockSpec((D_MODEL, 2 * SGU_WIDTH), const), pl.BlockSpec((1, SGU_WIDTH), const),
                  pl.BlockSpec((SGU_GROUPS, SGU_CHUNK, SGU_CHUNK), lambda b, i: (0, 0, 0)),
                  pl.BlockSpec((SGU_CHUNK, SGU_GROUPS), const), pl.BlockSpec((SGU_WIDTH, D_MODEL), const), vspec] + r_in,
        out_specs=[pl.BlockSpec((1, tm, D_MODEL), row)] + r_out,
        out_shape=[jax.ShapeDtypeStruct((BATCH, SEQ, D_MODEL), F32)] + r_shape,
        scratch_shapes=[pltpu.VMEM((tm, SGU_WIDTH), BF16), pltpu.VMEM((tm, SGU_WIDTH), F32)],
        compiler_params=_cparams(("arbitrary", "arbitrary"), vmem_mib=56),
        name="gmlp_layer",
    )(x, g1.reshape(1, D_MODEL), shift1.reshape(BATCH, 1, D_MODEL), scale1.reshape(BATCH, 1, D_MODEL),
      w_in.astype(BF16), norm_g.reshape(1, SGU_WIDTH), w_s, b_s.T, w_out.astype(BF16),
      gate.reshape(BATCH, 1, D_MODEL), *route_args)


MOE_TM = 256
MOE_CHUNK = 512
MOE_SORTED = N_TOK + N_EXPERT_GROUPS * MOE_TM
MOE_TILES = MOE_SORTED // MOE_TM
MOE_CHUNKS = N_TOK // MOE_CHUNK
MOE_PAIRS = MOE_TILES + N_EXPERT_GROUPS * MOE_CHUNKS
FLAG_ACTIVE, FLAG_FIRST, FLAG_LAST, FLAG_ZERO = 1, 2, 4, 8


def _plan_kernel(gi_ref, rank_ref, before_ref):
    gi = gi_ref[...]
    r = lax.broadcasted_iota(jnp.int32, (POS_SIDE, POS_SIDE), 0)
    c = lax.broadcasted_iota(jnp.int32, (POS_SIDE, POS_SIDE), 1)
    upper = jnp.where(r <= c, 1.0, 0.0)
    lower_strict = jnp.where(c < r, 1.0, 0.0)
    rank = jnp.zeros((POS_SIDE, POS_SIDE), F32)
    for g in range(N_EXPERT_GROUPS):
        member = jnp.where(gi == g, 1.0, 0.0)
        in_row = jnp.dot(member, upper, precision=HI, preferred_element_type=F32)
        row_total = jnp.broadcast_to(in_row[:, POS_SIDE - 1:POS_SIDE], (POS_SIDE, POS_SIDE))
        before = jnp.dot(lower_strict, row_total, precision=HI, preferred_element_type=F32)
        before_ref[g] = before
        rank += member * (before + in_row - 1.0)
    rank_ref[...] = rank


def moe_plan(route):
    tm = MOE_TM
    i32 = jnp.int32
    gi_f = route[:, 0, :].reshape(POS_SIDE, POS_SIDE)
    rank, before = pl.pallas_call(
        _plan_kernel,
        out_shape=[jax.ShapeDtypeStruct((POS_SIDE, POS_SIDE), F32),
                   jax.ShapeDtypeStruct((N_EXPERT_GROUPS, POS_SIDE, POS_SIDE), F32)],
        name="moe_plan",
    )(gi_f)
    gi = gi_f.reshape(N_TOK).astype(i32)
    groups = jnp.arange(N_EXPERT_GROUPS, dtype=i32)
    member = gi[None, :] == groups[:, None]
    tot = jnp.sum(member, axis=1).astype(i32)
    padded = (tot + tm - 1) // tm * tm
    gend = jnp.cumsum(padded).astype(i32)
    gstart = gend - padded
    pos = jnp.sum(jnp.where(member, gstart[:, None], 0), axis=0).astype(i32) + rank.reshape(N_TOK).astype(i32)
    rows_per_chunk = MOE_CHUNK // POS_SIDE
    cnt_end = jnp.concatenate([before[:, rows_per_chunk::rows_per_chunk, 0].astype(i32), tot[:, None]], axis=1)
    t = jnp.arange(MOE_TILES, dtype=i32)
    n_used = gend[-1] // tm
    tile_g = jnp.minimum(jnp.sum(gend[None, :] <= (t * tm)[:, None], axis=1), N_EXPERT_GROUPS - 1).astype(i32)
    k0 = t * tm - gstart[tile_g]
    k1 = jnp.minimum(k0 + tm, tot[tile_g]) - 1
    ce = cnt_end[tile_g]
    c_lo = jnp.sum(ce <= k0[:, None], axis=1).astype(i32)
    c_hi = jnp.sum(ce <= k1[:, None], axis=1).astype(i32)
    npairs = jnp.where(t < n_used, c_hi - c_lo + 1, 0)
    pend = jnp.cumsum(npairs).astype(i32)
    pstart = pend - npairs
    total = pend[-1]
    l = jnp.arange(MOE_PAIRS, dtype=i32)
    real = l < total
    lt = jnp.minimum(l, total - 1)
    tile_l = jnp.sum(pend[None, :] <= lt[:, None], axis=1).astype(i32)
    chunk_l = c_lo[tile_l] + lt - pstart[tile_l]
    spare_tile = jnp.minimum(n_used + (l - total), MOE_TILES - 1)
    flags = jnp.where(real, FLAG_ACTIVE + jnp.where(lt == pstart[tile_l], FLAG_FIRST, 0)
                      + jnp.where(lt == pend[tile_l] - 1, FLAG_LAST, 0),
                      jnp.where(spare_tile >= n_used, FLAG_ZERO, 0)).astype(i32)
    tile_sched = jnp.where(real, tile_l, spare_tile).astype(i32)
    by_tile = (tile_sched, chunk_l.astype(i32), flags, tile_g[tile_sched])
    cc = jnp.arange(MOE_CHUNKS, dtype=i32)
    is_pair = (cc[:, None] >= c_lo[None, :]) & (cc[:, None] <= c_hi[None, :]) & (t[None, :] < n_used)
    seen = jnp.cumsum(is_pair.reshape(-1).astype(i32))
    flat = jnp.sum(seen[None, :] <= lt[:, None], axis=1).astype(i32)
    chunk_c, tile_c = flat // MOE_TILES, flat % MOE_TILES
    prev_c = jnp.concatenate([jnp.full((1,), -1, i32), chunk_c[:-1]])
    next_c = jnp.concatenate([chunk_c[1:], jnp.full((1,), -1, i32)])
    flags_c = jnp.where(real, FLAG_ACTIVE + jnp.where(chunk_c != prev_c, FLAG_FIRST, 0)
                        + jnp.where((chunk_c != next_c) | (l == total - 1), FLAG_LAST, 0), 0).astype(i32)
    by_chunk = (tile_c, chunk_c, flags_c)
    return pos.reshape(MOE_CHUNKS, 1, MOE_CHUNK), by_tile, by_chunk


def _one_hot_rows(pos_row, tile):
    rows = tile * MOE_TM + lax.broadcasted_iota(jnp.int32, (MOE_TM, MOE_CHUNK), 0)
    return jnp.where(pos_row == rows, 1.0, 0.0).astype(BF16)


def _moe_kernel(tile_ref, chunk_ref, flag_ref, grp_ref, pos_ref, hx_ref, wg_ref, wu_ref, wd_ref, y_ref, acc_ref):
    l = pl.program_id(0)
    flags = flag_ref[l]

    @pl.when((flags & FLAG_FIRST) != 0)
    def _():
        acc_ref[...] = jnp.zeros_like(acc_ref)

    @pl.when((flags & FLAG_ACTIVE) != 0)
    def _():
        acc_ref[...] += jnp.dot(_one_hot_rows(pos_ref[0], tile_ref[l]), hx_ref[0], preferred_element_type=F32)

    @pl.when((flags & FLAG_LAST) != 0)
    def _():
        x = acc_ref[:, 0:D_MODEL].astype(BF16)
        w = functools.reduce(jnp.add, [acc_ref[:, D_MODEL + n * LANES:D_MODEL + (n + 1) * LANES] for n in range(3)])
        y = jnp.zeros((MOE_TM, D_MODEL), F32)
        for r in range(EXPERTS_PER_GROUP):
            gate = jnp.dot(x, wg_ref[0, 0, r], preferred_element_type=F32)
            up = jnp.dot(x, wu_ref[0, 0, r], preferred_element_type=F32)
            hid = (_silu(gate) * up * w[:, r:r + 1]).astype(BF16)
            y += jnp.dot(hid, wd_ref[0, 0, r], preferred_element_type=F32)
        y_ref[...] = y

    @pl.when((flags & FLAG_ZERO) != 0)
    def _():
        y_ref[...] = jnp.zeros_like(y_ref)


def moe_experts(hx, pos, by_tile, w_gate, w_up, w_down, layer):
    grouped = lambda w: w.reshape(DEPTH, N_EXPERT_GROUPS, EXPERTS_PER_GROUP, *w.shape[2:])
    wspec = lambda k, n: pl.BlockSpec((1, 1, EXPERTS_PER_GROUP, k, n), lambda l, t, c, f, g: (layer, g[l], 0, 0, 0))
    return pl.pallas_call(
        _moe_kernel,
        grid_spec=pltpu.PrefetchScalarGridSpec(
            num_scalar_prefetch=4,
            grid=(MOE_PAIRS,),
            in_specs=[pl.BlockSpec((1, 1, MOE_CHUNK), lambda l, t, c, f, g: (c[l], 0, 0)),
                      pl.BlockSpec((1, MOE_CHUNK, HX_W), lambda l, t, c, f, g: (c[l], 0, 0)),
                      wspec(D_MODEL, EXPERT_HIDDEN), wspec(D_MODEL, EXPERT_HIDDEN), wspec(EXPERT_HIDDEN, D_MODEL)],
            out_specs=pl.BlockSpec((MOE_TM, D_MODEL), lambda l, t, c, f, g: (t[l], 0)),
            scratch_shapes=[pltpu.VMEM((MOE_TM, HX_W), F32)],
        ),
        out_shape=jax.ShapeDtypeStruct((MOE_SORTED, D_MODEL), F32),
        compiler_params=_cparams(("arbitrary",), vmem_mib=56),
        name="moe_experts",
    )(*by_tile, pos, hx.reshape(MOE_CHUNKS, MOE_CHUNK, HX_W), grouped(w_gate), grouped(w_up), grouped(w_down))


def _moe_combine_kernel(tile_ref, chunk_ref, flag_ref, pos_ref, y_ref, x_ref, gate_ref, o_ref, acc_ref):
    l = pl.program_id(0)
    flags = flag_ref[l]

    @pl.when((flags & FLAG_FIRST) != 0)
    def _():
        acc_ref[...] = jnp.zeros_like(acc_ref)

    @pl.when((flags & FLAG_ACTIVE) != 0)
    def _():
        onehot = _one_hot_rows(pos_ref[0], tile_ref[l])
        y = y_ref[...]
        y_hi = y.astype(BF16)
        y_lo = (y - y_hi.astype(F32)).astype(BF16)
        acc_ref[...] += (lax.dot_general(onehot, y_hi, TN_DIMS, preferred_element_type=F32)
                         + lax.dot_general(onehot, y_lo, TN_DIMS, preferred_element_type=F32))

    @pl.when((flags & FLAG_LAST) != 0)
    def _():
        o_ref[0] = x_ref[0] + gate_ref[0] * acc_ref[...]


def moe_combine(x, y_sorted, pos, by_chunk, gate):
    per_b = SEQ // MOE_CHUNK
    tok = lambda l, t, c, f: (c[l] // per_b, c[l] % per_b, 0)
    return pl.pallas_call(
        _moe_combine_kernel,
        grid_spec=pltpu.PrefetchScalarGridSpec(
            num_scalar_prefetch=3,
            grid=(MOE_PAIRS,),
            in_specs=[pl.BlockSpec((1, 1, MOE_CHUNK), lambda l, t, c, f: (c[l], 0, 0)),
                      pl.BlockSpec((MOE_TM, D_MODEL), lambda l, t, c, f: (t[l], 0)),
                      pl.BlockSpec((1, MOE_CHUNK, D_MODEL), tok),
                      pl.BlockSpec((1, 1, D_MODEL), lambda l, t, c, f: (c[l] // per_b, 0, 0))],
            out_specs=pl.BlockSpec((1, MOE_CHUNK, D_MODEL), tok),
            scratch_shapes=[pltpu.VMEM((MOE_CHUNK, D_MODEL), F32)],
        ),
        out_shape=jax.ShapeDtypeStruct((BATCH, SEQ, D_MODEL), F32),
        compiler_params=_cparams(("arbitrary",)),
        name="moe_combine",
    )(*by_chunk, pos, y_sorted, x, gate.reshape(BATCH, 1, D_MODEL))


def moe_layer(x, hx, route, gate, w_gate, w_up, w_down, layer):
    pos, by_tile, by_chunk = moe_plan(route)
    y_sorted = moe_experts(hx, pos, by_tile, w_gate, w_up, w_down, layer)
    return moe_combine(x, y_sorted, pos, by_chunk, gate)


def kernel(x, c, positions, w_ada, b_ada, norm_g, w_in_ab, w_out_ab, gla_w_gate2, gla_b_gate, gla_norm_g, nsa_q_gain, nsa_k_gain, nsa_cmp_pe, nsa_cmp_w1, nsa_cmp_w2, w_in_c, sgu_norm_g, sgu_w_s, sgu_b_s, w_out_c, w_router, router_bias, w_gate, w_up, w_down):
    mod = ada_modulation(c, w_ada, b_ada)
    qk, gv, gr, nq, nkv, misc = inproj0(x, norm_g[0, 0], mod[0, :, 0], mod[0, :, 1], _arrange_w_in(w_in_ab[0]))
    o_a = gla_mixer(qk, gv, gr, misc, gla_w_gate2[0], gla_b_gate[0], gla_norm_g[0])
    o_b = nsa_mixer(nq, nkv, misc, positions, nsa_q_gain[0], nsa_k_gain[0], nsa_cmp_pe[0], nsa_cmp_w1[0], nsa_cmp_w2[0])
    wg, wu, wd = w_gate.astype(BF16), w_up.astype(BF16), w_down.astype(BF16)
    route_args = lambda l: _route_args(norm_g[l, 1], mod[l, :, 3], mod[l, :, 4], w_router, router_bias)
    x1, h, route = outproj0(o_a, o_b, w_out_ab[0], x, mod[0, :, 2], route_args(0))
    x2 = moe_layer(x1, h, route, mod[0, :, 5], wg, wu, wd, 0)
    x3, h, route = gmlp_layer(x2, norm_g[1, 0], mod[1, :, 0], mod[1, :, 1], w_in_c[0], sgu_norm_g[0], sgu_w_s[0],
                              sgu_b_s[0], w_out_c[0], mod[1, :, 2], route_args(1))
    return moe_layer(x3, h, route, mod[1, :, 5], wg, wu, wd, 1)
```

```python
import functools

import numpy as np
import jax
import jax.numpy as jnp
from jax import lax
from jax.experimental import pallas as pl
from jax.experimental.pallas import tpu as pltpu

D_MODEL = 1024
BATCH = 2
SEQ = 8192
DEPTH = 2
N_TOK = BATCH * SEQ

GLA_HEADS = 4
GLA_DK = 64
GLA_DV = 128
GLA_GATE_RANK = 16
GLA_TAU = 16.0
GLA_CHUNK = 64
NSA_HEADS = 8
NSA_KV_GROUPS = 2
NSA_HPG = NSA_HEADS // NSA_KV_GROUPS
NSA_DH = 64
CMP_LEN = 32
CMP_STRIDE = 16
CMP_HIDDEN = 256
SEL_BLOCK = 64
SEL_TOPK = 16
WINDOW = 512
ROPE_THETA = 500000.0
ROT_DIM = NSA_DH // 4
ROT_HALF = ROT_DIM // 2
SGU_CHUNK = 128
SGU_GROUPS = 8
SGU_WIDTH = 2048
SGU_GROUP_DIM = SGU_WIDTH // SGU_GROUPS
N_EXPERTS = 16
N_EXPERT_GROUPS = 4
EXPERTS_PER_GROUP = N_EXPERTS // N_EXPERT_GROUPS
MOE_TOPK = 2
EXPERT_HIDDEN = 512

GLA_QK_W = GLA_HEADS * GLA_DK
GLA_V_W = GLA_HEADS * GLA_DV
NSA_Q_W = NSA_HEADS * NSA_DH
NSA_KV_W = NSA_KV_GROUPS * NSA_DH
N_CMP = (SEQ - CMP_LEN) // CMP_STRIDE + 1
N_CMP_PAD = SEQ // CMP_STRIDE
N_SEL = SEQ // SEL_BLOCK

NORM_EPS = 1e-6
NEG_INF = -1e30
FORCE_BONUS = 1e4

LANES = 128
MIB = 1024 * 1024

F32 = jnp.float32
BF16 = jnp.bfloat16
HI = lax.Precision.HIGHEST
NT_DIMS = (((1,), (1,)), ((), ()))
TN_DIMS = (((0,), (0,)), ((), ()))


def _cparams(sem, vmem_mib=48):
    return pltpu.CompilerParams(dimension_semantics=sem, vmem_limit_bytes=vmem_mib * MIB)


def _rms_mod(x, g, shift, scale):
    y = x * lax.rsqrt(jnp.mean(x * x, axis=-1, keepdims=True) + NORM_EPS) * g
    return y * (1 + scale) + shift


def _silu(x):
    return x * jax.nn.sigmoid(x)


def _log_sigmoid(z):
    return jnp.minimum(z, 0.0) - jnp.log1p(jnp.exp(-jnp.abs(z)))


ADA_TN = 1536
ADA_ROWS = 8


def _ada_kernel(c_ref, w_ref, b_ref, o_ref):
    cond = _silu(c_ref[...])
    o_ref[0] = jnp.dot(cond, w_ref[0], precision=HI, preferred_element_type=F32) + b_ref[0]


def ada_modulation(c, w_ada, b_ada):
    c8 = jnp.zeros((ADA_ROWS, D_MODEL), F32).at[:BATCH].set(c)
    width = 6 * D_MODEL
    out = pl.pallas_call(
        _ada_kernel,
        grid=(DEPTH, width // ADA_TN),
        in_specs=[
            pl.BlockSpec((ADA_ROWS, D_MODEL), lambda l, j: (0, 0)),
            pl.BlockSpec((1, D_MODEL, ADA_TN), lambda l, j: (l, 0, j)),
            pl.BlockSpec((1, 1, ADA_TN), lambda l, j: (l, 0, j)),
        ],
        out_specs=pl.BlockSpec((1, ADA_ROWS, ADA_TN), lambda l, j: (l, 0, j)),
        out_shape=jax.ShapeDtypeStruct((DEPTH, ADA_ROWS, width), F32),
        compiler_params=_cparams(("arbitrary", "arbitrary")),
        name="ada_modulation",
    )(c8, w_ada, b_ada.reshape(DEPTH, 1, width))
    return out[:, :BATCH].reshape(DEPTH, BATCH, 6, D_MODEL)


INPROJ_TM = 512
INPROJ_WIDTHS = (2 * GLA_QK_W, GLA_V_W, GLA_V_W, NSA_Q_W, 6 * NSA_KV_W, LANES)


def _arrange_w_in(w_in):
    o = np.cumsum((0, GLA_QK_W, GLA_QK_W, GLA_V_W, GLA_GATE_RANK, GLA_V_W, NSA_Q_W, 6 * NSA_KV_W, NSA_HEADS * 3))
    gq_gk = w_in[:, o[0]:o[2]]
    gv = w_in[:, o[2]:o[3]]
    glr = w_in[:, o[3]:o[4]]
    gr = w_in[:, o[4]:o[5]]
    nq = w_in[:, o[5]:o[6]]
    nkv = w_in[:, o[6]:o[7]]
    ng = w_in[:, o[7]:o[8]]
    pad = jnp.zeros((D_MODEL, LANES - GLA_GATE_RANK - NSA_HEADS * 3), w_in.dtype)
    return jnp.concatenate([gq_gk, gv, gr, nq, nkv, glr, ng, pad], axis=1).astype(BF16)


def _inproj0_kernel(x_ref, g_ref, sh_ref, sc_ref, w_ref, *o_refs):
    h = _rms_mod(x_ref[0], g_ref[...], sh_ref[0], sc_ref[0]).astype(BF16)
    off = 0
    for o_ref, wd in zip(o_refs, INPROJ_WIDTHS):
        o_ref[0] = jnp.dot(h, w_ref[:, off:off + wd], preferred_element_type=F32)
        off += wd


def inproj0(x, g, shift, scale, w_arranged):
    tm = INPROJ_TM
    wtot = sum(INPROJ_WIDTHS)
    row = lambda b, i: (b, i, 0)
    vec = lambda b, i: (b, 0, 0)
    return pl.pallas_call(
        _inproj0_kernel,
        grid=(BATCH, SEQ // tm),
        in_specs=[
            pl.BlockSpec((1, tm, D_MODEL), row),
            pl.BlockSpec((1, D_MODEL), lambda b, i: (0, 0)),
            pl.BlockSpec((1, 1, D_MODEL), vec),
            pl.BlockSpec((1, 1, D_MODEL), vec),
            pl.BlockSpec((D_MODEL, wtot), lambda b, i: (0, 0)),
        ],
        out_specs=[pl.BlockSpec((1, tm, wd), row) for wd in INPROJ_WIDTHS],
        out_shape=[jax.ShapeDtypeStruct((BATCH, SEQ, wd), F32) for wd in INPROJ_WIDTHS],
        compiler_params=_cparams(("arbitrary", "arbitrary")),
        name="inproj0",
    )(x, g.reshape(1, D_MODEL), shift.reshape(BATCH, 1, D_MODEL), scale.reshape(BATCH, 1, D_MODEL), w_arranged)


GLA_TG = 512


def _gla_chunk_sums():
    i = np.arange(GLA_TG)[:, None]
    j = np.arange(GLA_TG)[None, :]
    same = (i // GLA_CHUNK) == (j // GLA_CHUNK)
    m3 = np.concatenate([same & (j <= i), same & (j % GLA_CHUNK <= GLA_CHUNK // 2), same], axis=0).astype(np.float32)
    return jnp.asarray(np.concatenate([m3, m3], axis=1), BF16)


def _gla_kernel(qk_ref, v_ref, r_ref, misc_ref, w2_ref, bg_ref, og_ref, sums_ref, o_ref, st_ref):
    C, tg = GLA_CHUNK, GLA_TG

    @pl.when(pl.program_id(1) == 0)
    def _():
        st_ref[...] = jnp.zeros_like(st_ref)

    z = jnp.dot(misc_ref[0], w2_ref[...], precision=HI, preferred_element_type=F32) + bg_ref[...]
    la = _log_sigmoid(z) / GLA_TAU
    la_hi = la.astype(BF16)
    la_lo = (la - la_hi.astype(F32)).astype(BF16)
    sums = jnp.dot(sums_ref[...], jnp.concatenate([la_hi, la_lo], axis=0), preferred_element_type=F32)
    bc, b_mid, b_last = sums[0:tg], sums[tg:2 * tg], sums[2 * tg:3 * tg]

    q = qk_ref[0, :, 0:GLA_QK_W] * (GLA_DK ** -0.5)
    k = qk_ref[0, :, GLA_QK_W:2 * GLA_QK_W]
    qd = q * jnp.exp(bc - b_mid)
    kd = (k * jnp.exp(b_mid - bc)).astype(BF16)
    kl = k * jnp.exp(b_last - bc)
    qb = q * jnp.exp(bc)
    dec = jnp.exp(b_last)
    lane = lax.broadcasted_iota(jnp.int32, (1, GLA_QK_W), 1)
    heads = [(lane >= h * GLA_DK) & (lane < (h + 1) * GLA_DK) for h in range(GLA_HEADS)]
    qd_h = [jnp.where(m, qd, 0.0).astype(BF16) for m in heads]
    qb_h = [jnp.where(m, qb, 0.0).astype(BF16) for m in heads]
    kl_h = [jnp.where(m, kl, 0.0).astype(BF16) for m in heads]
    stack = lambda per_head, rows: jnp.concatenate([t[rows] for t in per_head], axis=0)
    stacked_row = lax.broadcasted_iota(jnp.int32, (GLA_HEADS * C, C), 0)
    causal = (stacked_row & (C - 1)) >= lax.broadcasted_iota(jnp.int32, (GLA_HEADS * C, C), 1)
    og = og_ref[...]

    st = st_ref[...]
    for c in range(tg // C):
        rows = slice(c * C, (c + 1) * C)
        v = v_ref[0, rows, :].astype(BF16)
        s = lax.dot_general(stack(qd_h, rows), kd[rows], NT_DIMS, preferred_element_type=F32)
        s = jnp.where(causal, s, 0.0).astype(BF16)
        o_intra = jnp.dot(s, v, preferred_element_type=F32)
        o_inter = lax.dot_general(stack(qb_h, rows), st.astype(BF16), NT_DIMS, preferred_element_type=F32)
        v_stack = jnp.concatenate([v[:, h * GLA_DV:(h + 1) * GLA_DV] for h in range(GLA_HEADS)], axis=0)
        st = st * dec[c * C:c * C + 1] + lax.dot_general(v_stack, stack(kl_h, rows), TN_DIMS, preferred_element_type=F32)
        for h in range(GLA_HEADS):
            hrows = slice(h * C, (h + 1) * C)
            vcols = slice(h * GLA_DV, (h + 1) * GLA_DV)
            o = o_intra[hrows, vcols] + o_inter[hrows]
            on = o * lax.rsqrt(jnp.mean(o * o, axis=-1, keepdims=True) + NORM_EPS) * og
            o_ref[0, rows, vcols] = on * _silu(r_ref[0, rows, vcols])
    st_ref[...] = st


def gla_mixer(qk, v, r, misc, w_gate2, b_gate, out_g):
    tg = GLA_TG
    w2 = jnp.zeros((LANES, GLA_QK_W), F32).at[:GLA_GATE_RANK].set(w_gate2)
    row = lambda b, i: (b, i, 0)
    const = lambda b, i: (0, 0)
    return pl.pallas_call(
        _gla_kernel,
        grid=(BATCH, SEQ // tg),
        in_specs=[
            pl.BlockSpec((1, tg, 2 * GLA_QK_W), row),
            pl.BlockSpec((1, tg, GLA_V_W), row),
            pl.BlockSpec((1, tg, GLA_V_W), row),
            pl.BlockSpec((1, tg, LANES), row),
            pl.BlockSpec((LANES, GLA_QK_W), const),
            pl.BlockSpec((1, GLA_QK_W), const),
            pl.BlockSpec((1, GLA_DV), const),
            pl.BlockSpec((3 * tg, 2 * tg), const),
        ],
        out_specs=pl.BlockSpec((1, tg, GLA_V_W), row),
        out_shape=jax.ShapeDtypeStruct((BATCH, SEQ, GLA_V_W), F32),
        scratch_shapes=[pltpu.VMEM((GLA_DV, GLA_QK_W), F32)],
        compiler_params=_cparams(("arbitrary", "arbitrary")),
        name="gla_mixer",
    )(qk, v, r, misc, w2, b_gate.reshape(1, GLA_QK_W), out_g.reshape(1, GLA_DV), _gla_chunk_sums())


POS_SIDE = 128


def _rope_table_kernel(freq_ref, pos_ref, cos_ref, sin_ref):
    pos = pos_ref[...].astype(F32)
    for f in range(ROT_HALF):
        ang = pos * freq_ref[f]
        cos_ref[f] = jnp.cos(ang)
        sin_ref[f] = jnp.sin(ang)


def rope_tables(positions):
    inv_freq = jnp.float32(ROPE_THETA) ** (-jnp.arange(ROT_HALF, dtype=F32) / ROT_HALF)
    shp = jax.ShapeDtypeStruct((ROT_HALF, POS_SIDE, POS_SIDE), F32)
    cos, sin = pl.pallas_call(
        _rope_table_kernel,
        in_specs=[pl.BlockSpec(memory_space=pltpu.SMEM), pl.BlockSpec(memory_space=pltpu.VMEM)],
        out_specs=[pl.BlockSpec(memory_space=pltpu.VMEM)] * 2,
        out_shape=[shp, shp],
        name="rope_tables",
    )(inv_freq, positions.reshape(POS_SIDE, POS_SIDE))
    return jnp.concatenate([cos, sin], axis=0).reshape(ROT_DIM, N_TOK).T.reshape(BATCH, SEQ, ROT_DIM)


def _rope_placement():
    place = np.zeros((ROT_DIM, 3 * LANES), np.float32)
    const = np.zeros((1, 3 * LANES), np.float32)
    for lane in range(LANES):
        i = lane % NSA_DH
        if i < ROT_HALF:
            place[i, lane] = 1.0
            place[ROT_HALF + i, LANES + lane] = -1.0
        elif i < ROT_DIM:
            place[i - ROT_HALF, lane] = 1.0
            place[i, 2 * LANES + lane] = 1.0
        else:
            const[0, lane] = 1.0
    return jnp.asarray(place), jnp.asarray(const)


def _lane_tables(cs, place_ref, const_ref):
    tab = jnp.dot(cs, place_ref[...], precision=HI, preferred_element_type=F32) + const_ref[...]
    return tab[:, 0:LANES], tab[:, LANES:2 * LANES], tab[:, 2 * LANES:3 * LANES]


def _block_diag_ones2(width):
    h = np.arange(width) // NSA_DH
    bd = (h[:, None] == h[None, :]).astype(np.float32)
    return jnp.asarray(np.concatenate([bd, bd], axis=0), BF16)


def _head_norm_rope(x, gain, bd2, c, sm, sp):
    width = x.shape[-1]
    reps = width // LANES
    sq = x * x
    sq_hi = sq.astype(BF16)
    sq_lo = (sq - sq_hi.astype(F32)).astype(BF16)
    ss = jnp.dot(jnp.concatenate([sq_hi, sq_lo], axis=1), bd2, preferred_element_type=F32)
    y = x * lax.rsqrt(ss * (1.0 / NSA_DH) + NORM_EPS) * gain
    tile = lambda t: jnp.concatenate([t] * reps, axis=1) if reps > 1 else t
    return (y * tile(c) + pltpu.roll(y, width - ROT_HALF, 1) * tile(sm) + pltpu.roll(y, ROT_HALF, 1) * tile(sp))


PREP_TM = 512


def _prep_kernel(q_ref, ks_ref, kw_ref, cs_ref, place_ref, const_ref, gq_ref, gk_ref, bdq_ref, bdk_ref,
                 qo_ref, kso_ref, kwo_ref):
    c, sm, sp = _lane_tables(cs_ref[0], place_ref, const_ref)
    bdk = bdk_ref[...]
    qo_ref[0] = _head_norm_rope(q_ref[0], gq_ref[...], bdq_ref[...], c, sm, sp) * (NSA_DH ** -0.5)
    kso_ref[0] = _head_norm_rope(ks_ref[0], gk_ref[0:1, :], bdk, c, sm, sp)
    kwo_ref[0] = _head_norm_rope(kw_ref[0], gk_ref[1:2, :], bdk, c, sm, sp)


def nsa_prep(nq, nkv, cs, q_gain, k_gain):
    tm = PREP_TM
    row = lambda b, i: (b, i, 0)
    const = lambda b, i: (0, 0)
    gq = jnp.tile(q_gain, NSA_HEADS).reshape(1, NSA_Q_W)
    gk = jnp.stack([jnp.tile(k_gain[1], NSA_KV_GROUPS), jnp.tile(k_gain[2], NSA_KV_GROUPS)])
    return pl.pallas_call(
        _prep_kernel,
        grid=(BATCH, SEQ // tm),
        in_specs=[
            pl.BlockSpec((1, tm, NSA_Q_W), row),
            pl.BlockSpec((1, tm, NSA_KV_W), lambda b, i: (b, i, 2)),
            pl.BlockSpec((1, tm, NSA_KV_W), lambda b, i: (b, i, 4)),
            pl.BlockSpec((1, tm, ROT_DIM), row),
            pl.BlockSpec((ROT_DIM, 3 * LANES), const),
            pl.BlockSpec((1, 3 * LANES), const),
            pl.BlockSpec((1, NSA_Q_W), const),
            pl.BlockSpec((2, NSA_KV_W), const),
            pl.BlockSpec((2 * NSA_Q_W, NSA_Q_W), const),
            pl.BlockSpec((2 * NSA_KV_W, NSA_KV_W), const),
        ],
        out_specs=[pl.BlockSpec((1, tm, NSA_Q_W), row), pl.BlockSpec((1, tm, NSA_KV_W), row),
                   pl.BlockSpec((1, tm, NSA_KV_W), row)],
        out_shape=[jax.ShapeDtypeStruct((BATCH, SEQ, NSA_Q_W), F32), jax.ShapeDtypeStruct((BATCH, SEQ, NSA_KV_W), F32),
                   jax.ShapeDtypeStruct((BATCH, SEQ, NSA_KV_W), F32)],
        compiler_params=_cparams(("arbitrary", "arbitrary")),
        name="nsa_prep",
    )(nq, nkv, nkv, cs, *_rope_placement(), gq, gk, _block_diag_ones2(NSA_Q_W), _block_diag_ones2(NSA_KV_W))


SEG_W = CMP_STRIDE * NSA_DH


def _cmp_kernel(xk_ref, xv_ref, pe_ref, w1_ref, w2_ref, gain_ref, cs_ref, place_ref, const_ref, bd_ref, ko_ref, vo_ref):
    def compress(x_ref, kv):
        out = jnp.zeros((N_CMP_PAD, LANES), F32)
        for g in range(NSA_KV_GROUPS):
            x = x_ref[0, g]
            ha = jnp.dot(x + pe_ref[kv, 0], w1_ref[kv, 0:SEG_W, :], precision=HI, preferred_element_type=F32)
            hb = jnp.dot(x + pe_ref[kv, 1], w1_ref[kv, SEG_W:2 * SEG_W, :], precision=HI, preferred_element_type=F32)
            hid = ha + pltpu.roll(hb, N_CMP_PAD - 1, 0)
            out += jnp.dot(jax.nn.gelu(hid), w2_ref[kv, g], precision=HI, preferred_element_type=F32)
        return out

    c, sm, sp = _lane_tables(cs_ref[0], place_ref, const_ref)
    ko_ref[0] = _head_norm_rope(compress(xk_ref, 0), gain_ref[...], bd_ref[...], c, sm, sp)
    vo_ref[0] = compress(xv_ref, 1)


def nsa_compress(xk, xv, cmp_pe, cmp_w1, cmp_w2, k_gain0, cs_last):
    pe = cmp_pe.reshape(2, 2, 1, SEG_W)
    w2 = jnp.zeros((2, NSA_KV_GROUPS, CMP_HIDDEN, LANES), F32)
    for g in range(NSA_KV_GROUPS):
        w2 = w2.at[:, g, :, g * NSA_DH:(g + 1) * NSA_DH].set(cmp_w2)
    seg = pl.BlockSpec((1, NSA_KV_GROUPS, N_CMP_PAD, SEG_W), lambda b: (b, 0, 0, 0))
    tab = pl.BlockSpec((1, N_CMP_PAD, LANES), lambda b: (b, 0, 0))
    full = lambda shape: pl.BlockSpec(shape, lambda b: (0,) * len(shape))
    return pl.pallas_call(
        _cmp_kernel,
        grid=(BATCH,),
        in_specs=[seg, seg, full((2, 2, 1, SEG_W)), full((2, 2 * SEG_W, CMP_HIDDEN)),
                  full((2, NSA_KV_GROUPS, CMP_HIDDEN, LANES)), full((1, LANES)),
                  pl.BlockSpec((1, N_CMP_PAD, ROT_DIM), lambda b: (b, 0, 0)), full((ROT_DIM, 3 * LANES)),
                  full((1, 3 * LANES)), full((2 * LANES, LANES))],
        out_specs=[tab, tab],
        out_shape=[jax.ShapeDtypeStruct((BATCH, N_CMP_PAD, LANES), F32)] * 2,
        compiler_params=_cparams(("arbitrary",)),
        name="nsa_compress",
    )(xk, xv, pe, cmp_w1, w2, jnp.tile(k_gain0, NSA_KV_GROUPS).reshape(1, LANES), cs_last, *_rope_placement(),
      _block_diag_ones2(LANES))


CA_TQ = 256
SUBLANES = 8


CA_COLS = NSA_HPG * CA_TQ
CMP_PER_SEL = SEL_BLOCK // CMP_STRIDE


def split3_keys(k):
    hi = k.astype(BF16)
    lo = (k - hi.astype(F32)).astype(BF16)
    return jnp.concatenate([hi, lo, hi], axis=-1)


def _top_k_rows(score, k):
    rows, cols = score.shape
    row = lax.broadcasted_iota(jnp.int32, (rows, cols), 0).astype(F32)
    taken = jnp.zeros((rows, cols), F32)
    left = score
    for _ in range(k):
        top = jnp.max(left, axis=0, keepdims=True)
        first = jnp.min(jnp.where(left == top, row, float(rows)), axis=0, keepdims=True)
        hit = row == first
        taken = jnp.where(hit, 1.0, taken)
        left = jnp.where(hit, -jnp.inf, left)
    return taken


def _cattn_kernel(q_ref, kc_ref, vct_ref, gl_ref, o_ref, sel_ref, q3_ref, ps_ref):
    tq = CA_TQ
    q0 = pl.program_id(2) * tq
    lanes4 = lambda t: jnp.concatenate([t] * NSA_HPG, axis=1)
    for h in range(NSA_HPG):
        q = q_ref[0, h]
        hi = q.astype(BF16)
        lo = (q - hi.astype(F32)).astype(BF16)
        for n, part in enumerate((hi, hi, lo)):
            q3_ref[n * NSA_DH:(n + 1) * NSA_DH, h * tq:(h + 1) * tq] = part
    s = jnp.dot(kc_ref[0, 0], q3_ref[...], preferred_element_type=F32)
    cend = lax.broadcasted_iota(jnp.int32, (N_CMP_PAD, tq), 0) * CMP_STRIDE + (CMP_LEN - 1)
    tc = q0 + lax.broadcasted_iota(jnp.int32, (N_CMP_PAD, tq), 1)
    cmask = lanes4(cend <= tc)
    s = jnp.where(cmask, s, NEG_INF)
    m = jnp.max(s, axis=0, keepdims=True)
    e = jnp.where(cmask, jnp.exp(s - m), 0.0)
    l = jnp.sum(e, axis=0, keepdims=True)
    p = e / jnp.where(l > 0.0, l, 1.0)
    gate = jnp.concatenate([jax.nn.sigmoid(gl_ref[0, h, 0:1, :]) for h in range(NSA_HPG)], axis=1)
    o = jnp.dot(vct_ref[0, 0], p.astype(BF16), preferred_element_type=F32) * gate
    for h in range(NSA_HPG):
        o_ref[0, h] = o[:, h * tq:(h + 1) * tq]
    psum = functools.reduce(jnp.add, [p[:, h * tq:(h + 1) * tq] for h in range(NSA_HPG)])
    for n in range(tq // LANES):
        ps_ref[n] = psum[:, n * LANES:(n + 1) * LANES]

    jj = lax.broadcasted_iota(jnp.int32, (N_SEL, tq), 0)
    every4th = lambda r: jnp.concatenate([ps_ref[n, pl.ds(r, N_SEL, stride=CMP_PER_SEL), :] for n in range(tq // LANES)],
                                         axis=1)
    starts_in = [every4th(r) for r in range(CMP_PER_SEL)]
    from_prev = jnp.where(jj >= 1, pltpu.roll(starts_in[CMP_PER_SEL - 1], 1, 0), 0.0)
    imp = functools.reduce(jnp.add, starts_in) + from_prev
    tt = q0 + lax.broadcasted_iota(jnp.int32, (N_SEL, tq), 1)
    cur = jnp.right_shift(tt, 6)
    forced = (jj == 0) | (jj == cur) | (jj == cur - 1)
    valid = jj * SEL_BLOCK <= tt
    score = jnp.where(valid, imp + jnp.where(forced, FORCE_BONUS, 0.0), NEG_INF)

    sel_ref[0, 0] = jnp.where(valid, _top_k_rows(score, SEL_TOPK), 0.0)


def nsa_cmp_attn(q_t, kcmp, vcmp_t, gl_t):
    tq = CA_TQ
    return pl.pallas_call(
        _cattn_kernel,
        grid=(BATCH, NSA_KV_GROUPS, SEQ // tq),
        in_specs=[
            pl.BlockSpec((1, NSA_HPG, NSA_DH, tq), lambda b, g, i: (b, g, 0, i)),
            pl.BlockSpec((1, 1, N_CMP_PAD, 3 * NSA_DH), lambda b, g, i: (b, g, 0, 0)),
            pl.BlockSpec((1, 1, NSA_DH, N_CMP_PAD), lambda b, g, i: (b, g, 0, 0)),
            pl.BlockSpec((1, NSA_HPG, 3, tq), lambda b, g, i: (b, g, 0, i)),
        ],
        out_specs=[pl.BlockSpec((1, NSA_HPG, NSA_DH, tq), lambda b, g, i: (b, g, 0, i)),
                   pl.BlockSpec((1, 1, N_SEL, tq), lambda b, g, i: (b, g, 0, i))],
        out_shape=[jax.ShapeDtypeStruct((BATCH, NSA_HEADS, NSA_DH, SEQ), F32),
                   jax.ShapeDtypeStruct((BATCH, NSA_KV_GROUPS, N_SEL, SEQ), F32)],
        scratch_shapes=[pltpu.VMEM((3 * NSA_DH, CA_COLS), BF16), pltpu.VMEM((tq // LANES, N_CMP_PAD, LANES), F32)],
        compiler_params=_cparams(("arbitrary", "arbitrary", "arbitrary")),
        name="nsa_cmp_attn",
    )(q_t, split3_keys(kcmp), vcmp_t.astype(BF16), gl_t)


SA_TQ = 256
SA_TK = 1024
SA_PARTS = 2
SA_PART = SA_TK // SA_PARTS
M_INIT = -1e20


SA_COLS = NSA_HPG * SA_TQ
SA_BLOCKS = SA_TK // SEL_BLOCK


VT_ROWS = NSA_DH + 16


def value_slab_t(v_t):
    ones = jnp.ones(v_t.shape[:2] + (1, SEQ), BF16)
    zeros = jnp.zeros(v_t.shape[:2] + (VT_ROWS - NSA_DH - 1, SEQ), BF16)
    return jnp.concatenate([v_t.astype(BF16), ones, zeros], axis=2)


def sel_key_slab(ksel):
    blk = (np.arange(SEQ) % SA_TK) // SEL_BLOCK
    onehot = (blk[:, None] == np.arange(LANES - NSA_DH)[None, :]).astype(np.float32)
    onehot = jnp.broadcast_to(jnp.asarray(onehot, BF16), ksel.shape[:3] + (LANES - NSA_DH,))
    return jnp.concatenate([ksel.astype(BF16), onehot], axis=-1)


def _sattn_kernel(q_ref, k_ref, vt_ref, sel_ref, gl_ref, prev_ref, o_ref, qa_ref, acc_ref):
    tq, tk = SA_TQ, SA_TK
    i = pl.program_id(2)
    for h in range(NSA_HPG):
        qa_ref[0:NSA_DH, h * tq:(h + 1) * tq] = q_ref[0, h]
    qa_ref[NSA_DH:LANES, :] = jnp.zeros((LANES - NSA_DH, SA_COLS), BF16)
    acc_ref[...] = jnp.zeros_like(acc_ref)
    lanes4 = lambda t: jnp.concatenate([t] * NSA_HPG, axis=1)

    def key_tile(kt, m_prev, diagonal):
        selrows = sel_ref[0, 0, pl.ds(pl.multiple_of(kt * SA_BLOCKS, SA_BLOCKS), SA_BLOCKS), :]
        qa_ref[NSA_DH:NSA_DH + SA_BLOCKS, :] = lanes4(jnp.where(selrows > 0.5, 0.0, NEG_INF)).astype(BF16)
        ss = []
        for part in range(SA_PARTS):
            keys = pl.ds(pl.multiple_of(kt * tk + part * SA_PART, SA_PART), SA_PART)
            s = jnp.dot(k_ref[0, 0, keys, :], qa_ref[...], preferred_element_type=F32)
            if diagonal:
                kpos = kt * tk + part * SA_PART + lax.broadcasted_iota(jnp.int32, (SA_PART, tq), 0)
                tt = i * tq + lax.broadcasted_iota(jnp.int32, (SA_PART, tq), 1)
                s = s + lanes4(jnp.where(kpos <= tt, 0.0, NEG_INF))
            ss.append(s.astype(BF16))
        m_tile = functools.reduce(jnp.maximum, [jnp.max(s, axis=0, keepdims=True) for s in ss])
        m_new = jnp.maximum(m_prev, m_tile.astype(F32))
        acc = jnp.exp(m_prev - m_new) * acc_ref[...]
        for part in range(SA_PARTS):
            keys = pl.ds(pl.multiple_of(kt * tk + part * SA_PART, SA_PART), SA_PART)
            p = jnp.exp(ss[part] - m_new.astype(BF16))
            acc += jnp.dot(vt_ref[0, 0, :, keys], p, preferred_element_type=F32)
        acc_ref[...] = acc
        return m_new

    n_full = (i * tq) // tk
    m = lax.fori_loop(0, n_full, lambda kt, m: key_tile(kt, m, False), jnp.full((1, SA_COLS), M_INIT, F32))
    key_tile(n_full, m, True)
    gate = jnp.concatenate([jax.nn.sigmoid(gl_ref[0, h, 1:2, :]) for h in range(NSA_HPG)], axis=1)
    out = acc_ref[0:NSA_DH, :] / acc_ref[NSA_DH:NSA_DH + 1, :] * gate
    for h in range(NSA_HPG):
        o_ref[0, h] = prev_ref[0, h] + out[:, h * tq:(h + 1) * tq]


def nsa_sel_attn(q_t, k_slab, vsel_t, sel_t, gl_t, prev):
    tq = SA_TQ
    ospec = pl.BlockSpec((1, NSA_HPG, NSA_DH, tq), lambda b, g, i: (b, g, 0, i))
    return pl.pallas_call(
        _sattn_kernel,
        grid=(BATCH, NSA_KV_GROUPS, SEQ // tq),
        in_specs=[
            ospec,
            pl.BlockSpec((1, 1, SEQ, LANES), lambda b, g, i: (b, g, 0, 0)),
            pl.BlockSpec((1, 1, VT_ROWS, SEQ), lambda b, g, i: (b, g, 0, 0)),
            pl.BlockSpec((1, 1, N_SEL, tq), lambda b, g, i: (b, g, 0, i)),
            pl.BlockSpec((1, NSA_HPG, 3, tq), lambda b, g, i: (b, g, 0, i)),
            ospec,
        ],
        out_specs=ospec,
        out_shape=jax.ShapeDtypeStruct((BATCH, NSA_HEADS, NSA_DH, SEQ), F32),
        scratch_shapes=[pltpu.VMEM((LANES, SA_COLS), BF16), pltpu.VMEM((VT_ROWS, SA_COLS), F32)],
        input_output_aliases={5: 0},
        compiler_params=_cparams(("arbitrary", "arbitrary", "arbitrary")),
        name="nsa_sel_attn",
    )(q_t, k_slab, vsel_t, sel_t, gl_t, prev)


WA_TQ = 256
WA_TILES = WINDOW // WA_TQ + 1


def _window_bias():
    kl = np.arange(WA_TILES * WA_TQ)[:, None]
    ql = np.arange(WA_TQ)[None, :]
    diff = ql - kl + WINDOW
    return jnp.asarray(np.where((diff >= 0) & (diff < WINDOW), 0.0, NEG_INF).astype(np.float32))


def _wattn_kernel(q_ref, k0_ref, k1_ref, k2_ref, v0_ref, v1_ref, v2_ref, bias_ref, gl_ref, prev_ref, o_ref):
    tq = WA_TQ
    i = pl.program_id(2)
    k_refs = (k0_ref, k1_ref, k2_ref)
    v_refs = (v0_ref, v1_ref, v2_ref)
    lanes4 = lambda t: jnp.concatenate([t] * NSA_HPG, axis=1)
    q = jnp.concatenate([q_ref[0, h] for h in range(NSA_HPG)], axis=1)
    ss = []
    for d in range(WA_TILES):
        in_seq = i - (WA_TILES - 1) + d >= 0
        bias = jnp.where(in_seq, bias_ref[d * tq:(d + 1) * tq, :], NEG_INF)
        ss.append((jnp.dot(k_refs[d][0, 0], q, preferred_element_type=F32) + lanes4(bias)).astype(BF16))
    m = functools.reduce(jnp.maximum, [jnp.max(s, axis=0, keepdims=True) for s in ss])
    acc = functools.reduce(jnp.add, [jnp.dot(v_refs[d][0, 0], jnp.exp(ss[d] - m), preferred_element_type=F32)
                                     for d in range(WA_TILES)])
    gate = jnp.concatenate([jax.nn.sigmoid(gl_ref[0, h, 2:3, :]) for h in range(NSA_HPG)], axis=1)
    out = acc[0:NSA_DH] / acc[NSA_DH:NSA_DH + 1] * gate
    for h in range(NSA_HPG):
        o_ref[0, h] = prev_ref[0, h] + out[:, h * tq:(h + 1) * tq]


def nsa_win_attn(q_t, kwin, vwin_t, gl_t, prev):
    tq = WA_TQ
    qspec = pl.BlockSpec((1, NSA_HPG, NSA_DH, tq), lambda b, g, i: (b, g, 0, i))
    tile = lambda d: (lambda i: jnp.maximum(i - (WA_TILES - 1) + d, 0))
    kspec = lambda d: pl.BlockSpec((1, 1, tq, NSA_DH), lambda b, g, i: (b, g, tile(d)(i), 0))
    vspec = lambda d: pl.BlockSpec((1, 1, VT_ROWS, tq), lambda b, g, i: (b, g, 0, tile(d)(i)))
    return pl.pallas_call(
        _wattn_kernel,
        grid=(BATCH, NSA_KV_GROUPS, SEQ // tq),
        in_specs=[qspec] + [kspec(d) for d in range(WA_TILES)] + [vspec(d) for d in range(WA_TILES)] + [
            pl.BlockSpec((WA_TILES * tq, tq), lambda b, g, i: (0, 0)),
            pl.BlockSpec((1, NSA_HPG, 3, tq), lambda b, g, i: (b, g, 0, i)),
            qspec,
        ],
        out_specs=qspec,
        out_shape=jax.ShapeDtypeStruct((BATCH, NSA_HEADS, NSA_DH, SEQ), F32),
        input_output_aliases={2 * WA_TILES + 3: 0},
        compiler_params=_cparams(("arbitrary", "arbitrary", "arbitrary")),
        name="nsa_win_attn",
    )(q_t, *([kwin] * WA_TILES), *([vwin_t] * WA_TILES), _window_bias(), gl_t, prev)


def nsa_mixer(nq, nkv, misc, positions, q_gain, k_gain, cmp_pe, cmp_w1, cmp_w2):
    cs = rope_tables(positions)
    q_r, ks_r, kw_r = nsa_prep(nq, nkv, cs, q_gain, k_gain)
    group_major = lambda t: t.reshape(BATCH, SEQ, NSA_KV_GROUPS, NSA_DH).transpose(0, 2, 1, 3)
    group_major_t = lambda t: t.reshape(BATCH, SEQ, NSA_KV_GROUPS, NSA_DH).transpose(0, 2, 3, 1)
    col = lambda n: nkv[..., n * NSA_KV_W:(n + 1) * NSA_KV_W]
    segs = lambda t: group_major(t).reshape(BATCH, NSA_KV_GROUPS, N_CMP_PAD, SEG_W)
    last = jnp.minimum(jnp.arange(N_CMP_PAD) * CMP_STRIDE + CMP_LEN - 1, SEQ - 1)
    kcmp, vcmp = nsa_compress(segs(col(0)), segs(col(1)), cmp_pe, cmp_w1, cmp_w2, k_gain[0], cs[:, last])
    kcmp = kcmp.reshape(BATCH, N_CMP_PAD, NSA_KV_GROUPS, NSA_DH).transpose(0, 2, 1, 3)
    vcmp_t = vcmp.reshape(BATCH, N_CMP_PAD, NSA_KV_GROUPS, NSA_DH).transpose(0, 2, 3, 1)
    q_t = q_r.reshape(BATCH, SEQ, NSA_HEADS, NSA_DH).transpose(0, 2, 3, 1)
    gl_t = misc[..., GLA_GATE_RANK:GLA_GATE_RANK + NSA_HEADS * 3].reshape(BATCH, SEQ, NSA_HEADS, 3).transpose(0, 2, 3, 1)
    o_t, sel_t = nsa_cmp_attn(q_t, kcmp, vcmp_t, gl_t)
    q_t16 = q_t.astype(BF16)
    o_t = nsa_sel_attn(q_t16, sel_key_slab(group_major(ks_r)), value_slab_t(group_major_t(col(3))), sel_t, gl_t, o_t)
    o_t = nsa_win_attn(q_t16, group_major(kw_r).astype(BF16), value_slab_t(group_major_t(col(5))), gl_t, o_t)
    return o_t.transpose(0, 3, 1, 2).reshape(BATCH, SEQ, NSA_Q_W)


ROUTE_ROWS = 8
HX_W = D_MODEL + 3 * LANES


def _top2_sum(a, b, c, d):
    hi1, lo1 = jnp.maximum(a, b), jnp.minimum(a, b)
    hi2, lo2 = jnp.maximum(c, d), jnp.minimum(c, d)
    return jnp.maximum(hi1, hi2) + jnp.maximum(jnp.minimum(hi1, hi2), jnp.maximum(lo1, lo2))


def _moe_prenorm_route(xn, g_ref, sh_ref, sc_ref, wr_ref, rb_ref, hx_ref, route_ref):
    h = _rms_mod(xn, g_ref[...], sh_ref[0], sc_ref[0])
    logits = lax.dot_general(wr_ref[...], h, NT_DIMS, precision=HI, preferred_element_type=F32)
    scores = jax.nn.sigmoid(logits)
    sel = scores + rb_ref[...]
    epg = EXPERTS_PER_GROUP
    srow = lambda e: sel[e:e + 1, :]
    grp = [_top2_sum(*[srow(epg * g + r) for r in range(epg)]) for g in range(N_EXPERT_GROUPS)]
    best, gi = grp[0], jnp.zeros_like(grp[0], dtype=jnp.int32)
    for g in range(1, N_EXPERT_GROUPS):
        better = grp[g] > best
        gi = jnp.where(better, g, gi)
        best = jnp.where(better, grp[g], best)

    def in_group(mat, r):
        out = mat[r:r + 1, :]
        for g in range(1, N_EXPERT_GROUPS):
            out = jnp.where(gi == g, mat[epg * g + r:epg * g + r + 1, :], out)
        return out

    v = [in_group(sel, r) for r in range(epg)]
    sc = [in_group(scores, r) for r in range(epg)]
    b1, i1, w1 = v[0], jnp.zeros_like(gi), sc[0]
    for r in range(1, epg):
        better = v[r] > b1
        i1 = jnp.where(better, r, i1)
        w1 = jnp.where(better, sc[r], w1)
        b1 = jnp.where(better, v[r], b1)
    b2 = jnp.full_like(b1, -3e38)
    i2, w2 = jnp.zeros_like(gi), jnp.zeros_like(w1)
    for r in range(epg):
        better = (i1 != r) & (v[r] > b2)
        i2 = jnp.where(better, r, i2)
        w2 = jnp.where(better, sc[r], w2)
        b2 = jnp.where(better, v[r], b2)
    tot = w1 + w2
    w1, w2 = w1 / tot, w2 / tot
    zero = jnp.zeros_like(w1)
    route_ref[0] = jnp.concatenate([gi.astype(F32)] + [zero] * (ROUTE_ROWS - 1), axis=0)
    wrows = [jnp.where(i1 == r, w1, jnp.where(i2 == r, w2, 0.0)) for r in range(epg)]
    wmat = jnp.concatenate(wrows + [jnp.zeros((LANES - epg, w1.shape[1]), F32)], axis=0).T
    w_hi = wmat.astype(BF16)
    rest = wmat - w_hi.astype(F32)
    w_mid = rest.astype(BF16)
    w_lo = (rest - w_mid.astype(F32)).astype(BF16)
    hx_ref[0, :, 0:D_MODEL] = h.astype(BF16)
    for n, part in enumerate((w_hi, w_mid, w_lo)):
        hx_ref[0, :, D_MODEL + n * LANES:D_MODEL + (n + 1) * LANES] = part


def _route_specs(tm, row, vec, const):
    in_specs = [pl.BlockSpec((1, D_MODEL), const), pl.BlockSpec((1, 1, D_MODEL), vec), pl.BlockSpec((1, 1, D_MODEL), vec),
                pl.BlockSpec((N_EXPERTS, D_MODEL), const), pl.BlockSpec((N_EXPERTS, 1), const)]
    out_specs = [pl.BlockSpec((1, tm, HX_W), row), pl.BlockSpec((1, ROUTE_ROWS, tm), lambda b, i: (b, 0, i))]
    out_shape = [jax.ShapeDtypeStruct((BATCH, SEQ, HX_W), BF16), jax.ShapeDtypeStruct((BATCH, ROUTE_ROWS, SEQ), F32)]
    return in_specs, out_specs, out_shape


def _route_args(g, shift, scale, w_router, router_bias):
    return (g.reshape(1, D_MODEL), shift.reshape(BATCH, 1, D_MODEL), scale.reshape(BATCH, 1, D_MODEL),
            w_router.T, router_bias.reshape(N_EXPERTS, 1))


OUTPROJ_TM = 512


def _outproj0_kernel(oa_ref, ob_ref, w_ref, x_ref, gate_ref, g_ref, sh_ref, sc_ref, wr_ref, rb_ref,
                     xo_ref, h_ref, route_ref):
    y = jnp.dot(oa_ref[0].astype(BF16), w_ref[0:GLA_V_W, :], preferred_element_type=F32)
    y += jnp.dot(ob_ref[0].astype(BF16), w_ref[GLA_V_W:GLA_V_W + NSA_Q_W, :], preferred_element_type=F32)
    xn = x_ref[0] + gate_ref[0] * y
    xo_ref[0] = xn
    _moe_prenorm_route(xn, g_ref, sh_ref, sc_ref, wr_ref, rb_ref, h_ref, route_ref)


def outproj0(o_a, o_b, w_out, x, gate, route_args):
    tm = OUTPROJ_TM
    row = lambda b, i: (b, i, 0)
    vec = lambda b, i: (b, 0, 0)
    const = lambda b, i: (0, 0)
    r_in, r_out, r_shape = _route_specs(tm, row, vec, const)
    return pl.pallas_call(
        _outproj0_kernel,
        grid=(BATCH, SEQ // tm),
        in_specs=[pl.BlockSpec((1, tm, GLA_V_W), row), pl.BlockSpec((1, tm, NSA_Q_W), row),
                  pl.BlockSpec((GLA_V_W + NSA_Q_W, D_MODEL), const), pl.BlockSpec((1, tm, D_MODEL), row),
                  pl.BlockSpec((1, 1, D_MODEL), vec)] + r_in,
        out_specs=[pl.BlockSpec((1, tm, D_MODEL), row)] + r_out,
        out_shape=[jax.ShapeDtypeStruct((BATCH, SEQ, D_MODEL), F32)] + r_shape,
        compiler_params=_cparams(("arbitrary", "arbitrary")),
        name="outproj0",
    )(o_a, o_b, w_out.astype(BF16), x, gate.reshape(BATCH, 1, D_MODEL), *route_args)


GMLP_TM = 512


def _gmlp_kernel(x_ref, g1_ref, sh1_ref, sc1_ref, win_ref, ng_ref, ws_ref, bs_ref, wout_ref, gate_ref,
                 g_ref, sh_ref, sc_ref, wr_ref, rb_ref, xo_ref, h_ref, route_ref, gated_ref, v_ref):
    x = x_ref[0]
    h = _rms_mod(x, g1_ref[...], sh1_ref[0], sc1_ref[0]).astype(BF16)
    group_cols = lambda g: slice(g * SGU_GROUP_DIM, (g + 1) * SGU_GROUP_DIM)
    ssq = jnp.zeros((GMLP_TM, LANES), F32)
    for g in range(SGU_GROUPS):
        lo = SGU_WIDTH + g * SGU_GROUP_DIM
        v = jax.nn.gelu(jnp.dot(h, win_ref[:, lo:lo + SGU_GROUP_DIM], preferred_element_type=F32))
        v_ref[:, group_cols(g)] = v
        ssq += functools.reduce(jnp.add, [v[:, n * LANES:(n + 1) * LANES] ** 2 for n in range(SGU_GROUP_DIM // LANES)])
    rs = lax.rsqrt(jnp.sum(ssq, axis=-1, keepdims=True) * (1.0 / SGU_WIDTH) + NORM_EPS)
    ri = lax.broadcasted_iota(jnp.int32, (SGU_CHUNK, SGU_CHUNK), 0)
    ci = lax.broadcasted_iota(jnp.int32, (SGU_CHUNK, SGU_CHUNK), 1)
    for g in range(SGU_GROUPS):
        cols = group_cols(g)
        u = jax.nn.gelu(jnp.dot(h, win_ref[:, cols], preferred_element_type=F32))
        vn = (v_ref[:, cols] * rs * ng_ref[:, cols]).astype(BF16)
        w = jnp.where(ri >= ci, ws_ref[g], 0.0).astype(BF16)
        for c in range(GMLP_TM // SGU_CHUNK):
            rows = slice(c * SGU_CHUNK, (c + 1) * SGU_CHUNK)
            mix = jnp.dot(w, vn[rows], preferred_element_type=F32) + bs_ref[:, g:g + 1]
            gated_ref[rows, cols] = (u[rows] * mix).astype(BF16)
    y = jnp.dot(gated_ref[...], wout_ref[...], preferred_element_type=F32)
    xn = x + gate_ref[0] * y
    xo_ref[0] = xn
    _moe_prenorm_route(xn, g_ref, sh_ref, sc_ref, wr_ref, rb_ref, h_ref, route_ref)


def gmlp_layer(x, g1, shift1, scale1, w_in, norm_g, w_s, b_s, w_out, gate, route_args):
    tm = GMLP_TM
    row = lambda b, i: (b, i, 0)
    vec = lambda b, i: (b, 0, 0)
    const = lambda b, i: (0, 0)
    r_in, r_out, r_shape = _route_specs(tm, row, vec, const)
    vspec = pl.BlockSpec((1, 1, D_MODEL), vec)
    return pl.pallas_call(
        _gmlp_kernel,
        grid=(BATCH, SEQ // tm),
        in_specs=[pl.BlockSpec((1, tm, D_MODEL), row), pl.BlockSpec((1, D_MODEL), const), vspec, vspec,
                  pl.BlockSpec((D_MODEL, 2 * SGU_WIDTH), const), pl.BlockSpec((1, SGU_WIDTH), const),
                  pl.BlockSpec((SGU_GROUPS, SGU_CHUNK, SGU_CHUNK), lambda b, i: (0, 0, 0)),
                  pl.BlockSpec((SGU_CHUNK, SGU_GROUPS), const), pl.BlockSpec((SGU_WIDTH, D_MODEL), const), vspec] + r_in,
        out_specs=[pl.BlockSpec((1, tm, D_MODEL), row)] + r_out,
        out_shape=[jax.ShapeDtypeStruct((BATCH, SEQ, D_MODEL), F32)] + r_shape,
        scratch_shapes=[pltpu.VMEM((tm, SGU_WIDTH), BF16), pltpu.VMEM((tm, SGU_WIDTH), F32)],
        compiler_params=_cparams(("arbitrary", "arbitrary"), vmem_mib=56),
        name="gmlp_layer",
    )(x, g1.reshape(1, D_MODEL), shift1.reshape(BATCH, 1, D_MODEL), scale1.reshape(BATCH, 1, D_MODEL),
      w_in.astype(BF16), norm_g.reshape(1, SGU_WIDTH), w_s, b_s.T, w_out.astype(BF16),
      gate.reshape(BATCH, 1, D_MODEL), *route_args)


MOE_TM = 256
MOE_CHUNK = 512
MOE_SORTED = N_TOK + N_EXPERT_GROUPS * MOE_TM
MOE_TILES = MOE_SORTED // MOE_TM
MOE_CHUNKS = N_TOK // MOE_CHUNK
MOE_PAIRS = MOE_TILES + N_EXPERT_GROUPS * MOE_CHUNKS
FLAG_ACTIVE, FLAG_FIRST, FLAG_LAST, FLAG_ZERO = 1, 2, 4, 8
EXP_WIN = 4
CMB_WIN = 8
MOE_TSTEPS = MOE_TILES + MOE_PAIRS // EXP_WIN
MOE_CSTEPS = MOE_CHUNKS + MOE_PAIRS // CMB_WIN


def _plan_kernel(gi_ref, rank_ref, before_ref):
    gi = gi_ref[...]
    r = lax.broadcasted_iota(jnp.int32, (POS_SIDE, POS_SIDE), 0)
    c = lax.broadcasted_iota(jnp.int32, (POS_SIDE, POS_SIDE), 1)
    upper = jnp.where(r <= c, 1.0, 0.0)
    lower_strict = jnp.where(c < r, 1.0, 0.0)
    rank = jnp.zeros((POS_SIDE, POS_SIDE), F32)
    for g in range(N_EXPERT_GROUPS):
        member = jnp.where(gi == g, 1.0, 0.0)
        in_row = jnp.dot(member, upper, precision=HI, preferred_element_type=F32)
        row_total = jnp.broadcast_to(in_row[:, POS_SIDE - 1:POS_SIDE], (POS_SIDE, POS_SIDE))
        before = jnp.dot(lower_strict, row_total, precision=HI, preferred_element_type=F32)
        before_ref[g] = before
        rank += member * (before + in_row - 1.0)
    rank_ref[...] = rank


def moe_plan(route):
    tm = MOE_TM
    i32 = jnp.int32
    gi_f = route[:, 0, :].reshape(POS_SIDE, POS_SIDE)
    rank, before = pl.pallas_call(
        _plan_kernel,
        out_shape=[jax.ShapeDtypeStruct((POS_SIDE, POS_SIDE), F32),
                   jax.ShapeDtypeStruct((N_EXPERT_GROUPS, POS_SIDE, POS_SIDE), F32)],
        name="moe_plan",
    )(gi_f)
    gi = gi_f.reshape(N_TOK).astype(i32)
    groups = jnp.arange(N_EXPERT_GROUPS, dtype=i32)
    member = gi[None, :] == groups[:, None]
    tot = jnp.sum(member, axis=1).astype(i32)
    padded = (tot + tm - 1) // tm * tm
    gend = jnp.cumsum(padded).astype(i32)
    gstart = gend - padded
    pos = jnp.sum(jnp.where(member, gstart[:, None], 0), axis=0).astype(i32) + rank.reshape(N_TOK).astype(i32)
    rows_per_chunk = MOE_CHUNK // POS_SIDE
    cnt_end = jnp.concatenate([before[:, rows_per_chunk::rows_per_chunk, 0].astype(i32), tot[:, None]], axis=1)
    t = jnp.arange(MOE_TILES, dtype=i32)
    n_used = gend[-1] // tm
    tile_g = jnp.minimum(jnp.sum(gend[None, :] <= (t * tm)[:, None], axis=1), N_EXPERT_GROUPS - 1).astype(i32)
    k0 = t * tm - gstart[tile_g]
    k1 = jnp.minimum(k0 + tm, tot[tile_g]) - 1
    ce = cnt_end[tile_g]
    c_lo = jnp.sum(ce <= k0[:, None], axis=1).astype(i32)
    c_hi = jnp.sum(ce <= k1[:, None], axis=1).astype(i32)
    npairs = jnp.where(t < n_used, c_hi - c_lo + 1, 0)
    pend = jnp.cumsum(npairs).astype(i32)
    pstart = pend - npairs
    total = pend[-1]
    l = jnp.arange(MOE_PAIRS, dtype=i32)
    real = l < total
    lt = jnp.minimum(l, total - 1)

    def windows(count, win, n_steps):
        per_item = (count + win - 1) // win
        end = jnp.cumsum(per_item).astype(i32)
        start = end - per_item
        s = jnp.arange(n_steps, dtype=i32)
        real_s = s < end[-1]
        sc = jnp.minimum(s, end[-1] - 1)
        item = jnp.sum(end[None, :] <= sc[:, None], axis=1).astype(i32)
        j = sc - start[item]
        flags_s = jnp.where(real_s, FLAG_ACTIVE + jnp.where(j == 0, FLAG_FIRST, 0)
                            + jnp.where(j == per_item[item] - 1, FLAG_LAST, 0), 0).astype(i32)
        return item, j, flags_s, real_s, s - end[-1]

    tile_s, j, flags, real_s, spare = windows(npairs, EXP_WIN, MOE_TSTEPS)
    c0 = c_lo[tile_s] + EXP_WIN * j
    n_valid = jnp.minimum(EXP_WIN, c_hi[tile_s] - c0 + 1).astype(i32)
    spare_tile = jnp.minimum(n_used + spare, MOE_TILES - 1)
    flags = jnp.where(real_s, flags, jnp.where(spare_tile >= n_used, FLAG_ZERO, 0)).astype(i32)
    tile_sched = jnp.where(real_s, tile_s, spare_tile).astype(i32)
    by_tile = (tile_sched, c0.astype(i32), n_valid, flags, tile_g[tile_sched])
    cc = jnp.arange(MOE_CHUNKS, dtype=i32)
    is_pair = (cc[:, None] >= c_lo[None, :]) & (cc[:, None] <= c_hi[None, :]) & (t[None, :] < n_used)
    seen = jnp.cumsum(is_pair.reshape(-1).astype(i32))
    flat = jnp.sum(seen[None, :] <= lt[:, None], axis=1).astype(i32)
    pair_tile = flat % MOE_TILES
    per_chunk = jnp.sum(is_pair, axis=1).astype(i32)
    first_pair = jnp.cumsum(per_chunk).astype(i32) - per_chunk
    chunk_s, j, flags_c, _, _ = windows(per_chunk, CMB_WIN, MOE_CSTEPS)
    base = first_pair[chunk_s] + CMB_WIN * j
    n_valid_c = jnp.minimum(CMB_WIN, per_chunk[chunk_s] - CMB_WIN * j).astype(i32)
    tiles_c = tuple(pair_tile[jnp.minimum(base + w, total - 1)] for w in range(CMB_WIN))
    by_chunk = (chunk_s, n_valid_c, flags_c) + tiles_c
    return pos.reshape(MOE_CHUNKS, 1, MOE_CHUNK), by_tile, by_chunk


def _one_hot_rows(pos_row, tile):
    rows = tile * MOE_TM + lax.broadcasted_iota(jnp.int32, (MOE_TM, MOE_CHUNK), 0)
    return jnp.where(pos_row == rows, 1.0, 0.0).astype(BF16)


def _moe_kernel(tile_ref, c0_ref, nv_ref, flag_ref, grp_ref, *refs):
    pos_refs, hx_refs = refs[0:EXP_WIN], refs[EXP_WIN:2 * EXP_WIN]
    wg_ref, wu_ref, wd_ref, y_ref, acc_ref = refs[2 * EXP_WIN:]
    l = pl.program_id(0)
    flags = flag_ref[l]

    @pl.when((flags & FLAG_FIRST) != 0)
    def _():
        acc_ref[...] = jnp.zeros_like(acc_ref)

    for w in range(EXP_WIN):
        @pl.when(((flags & FLAG_ACTIVE) != 0) & (w < nv_ref[l]))
        def _():
            onehot = _one_hot_rows(pos_refs[w][0], tile_ref[l])
            acc_ref[...] += jnp.dot(onehot, hx_refs[w][0], preferred_element_type=F32)

    @pl.when((flags & FLAG_LAST) != 0)
    def _():
        x = acc_ref[:, 0:D_MODEL].astype(BF16)
        w = functools.reduce(jnp.add, [acc_ref[:, D_MODEL + n * LANES:D_MODEL + (n + 1) * LANES] for n in range(3)])
        y = jnp.zeros((MOE_TM, D_MODEL), F32)
        for r in range(EXPERTS_PER_GROUP):
            gate = jnp.dot(x, wg_ref[0, 0, r], preferred_element_type=F32)
            up = jnp.dot(x, wu_ref[0, 0, r], preferred_element_type=F32)
            hid = (_silu(gate) * up * w[:, r:r + 1]).astype(BF16)
            y += jnp.dot(hid, wd_ref[0, 0, r], preferred_element_type=F32)
        y_ref[...] = y

    @pl.when((flags & FLAG_ZERO) != 0)
    def _():
        y_ref[...] = jnp.zeros_like(y_ref)


def moe_experts(hx, pos, by_tile, w_gate, w_up, w_down, layer):
    grouped = lambda w: w.reshape(DEPTH, N_EXPERT_GROUPS, EXPERTS_PER_GROUP, *w.shape[2:])
    wspec = lambda k, n: pl.BlockSpec((1, 1, EXPERTS_PER_GROUP, k, n), lambda l, t, c, n_, f, g: (layer, g[l], 0, 0, 0))
    chunk = lambda w: (lambda l, t, c, n_, f, g: (jnp.minimum(c[l] + w, MOE_CHUNKS - 1), 0, 0))
    hx_chunks = hx.reshape(MOE_CHUNKS, MOE_CHUNK, HX_W)
    return pl.pallas_call(
        _moe_kernel,
        grid_spec=pltpu.PrefetchScalarGridSpec(
            num_scalar_prefetch=5,
            grid=(MOE_TSTEPS,),
            in_specs=[pl.BlockSpec((1, 1, MOE_CHUNK), chunk(w)) for w in range(EXP_WIN)]
            + [pl.BlockSpec((1, MOE_CHUNK, HX_W), chunk(w)) for w in range(EXP_WIN)]
            + [wspec(D_MODEL, EXPERT_HIDDEN), wspec(D_MODEL, EXPERT_HIDDEN), wspec(EXPERT_HIDDEN, D_MODEL)],
            out_specs=pl.BlockSpec((MOE_TM, D_MODEL), lambda l, t, c, n_, f, g: (t[l], 0)),
            scratch_shapes=[pltpu.VMEM((MOE_TM, HX_W), F32)],
        ),
        out_shape=jax.ShapeDtypeStruct((MOE_SORTED, D_MODEL), F32),
        compiler_params=_cparams(("arbitrary",), vmem_mib=56),
        name="moe_experts",
    )(*by_tile, *([pos] * EXP_WIN), *([hx_chunks] * EXP_WIN), grouped(w_gate), grouped(w_up), grouped(w_down))


def _moe_combine_kernel(chunk_ref, nv_ref, flag_ref, *refs):
    tile_refs = refs[0:CMB_WIN]
    pos_ref = refs[CMB_WIN]
    y_refs = refs[CMB_WIN + 1:2 * CMB_WIN + 1]
    x_ref, gate_ref, o_ref, acc_ref = refs[2 * CMB_WIN + 1:]
    l = pl.program_id(0)
    flags = flag_ref[l]

    @pl.when((flags & FLAG_FIRST) != 0)
    def _():
        acc_ref[...] = jnp.zeros_like(acc_ref)

    for w in range(CMB_WIN):
        @pl.when(((flags & FLAG_ACTIVE) != 0) & (w < nv_ref[l]))
        def _():
            onehot = _one_hot_rows(pos_ref[0], tile_refs[w][l])
            y = y_refs[w][...]
            y_hi = y.astype(BF16)
            y_lo = (y - y_hi.astype(F32)).astype(BF16)
            acc_ref[...] += (lax.dot_general(onehot, y_hi, TN_DIMS, preferred_element_type=F32)
                             + lax.dot_general(onehot, y_lo, TN_DIMS, preferred_element_type=F32))

    @pl.when((flags & FLAG_LAST) != 0)
    def _():
        o_ref[0] = x_ref[0] + gate_ref[0] * acc_ref[...]


def moe_combine(x, y_sorted, pos, by_chunk, gate):
    per_b = SEQ // MOE_CHUNK
    tok = lambda l, c, *_: (c[l] // per_b, c[l] % per_b, 0)
    tile = lambda w: (lambda l, c, n_, f, *tiles: (tiles[w][l], 0))
    return pl.pallas_call(
        _moe_combine_kernel,
        grid_spec=pltpu.PrefetchScalarGridSpec(
            num_scalar_prefetch=3 + CMB_WIN,
            grid=(MOE_CSTEPS,),
            in_specs=[pl.BlockSpec((1, 1, MOE_CHUNK), lambda l, c, *_: (c[l], 0, 0))]
            + [pl.BlockSpec((MOE_TM, D_MODEL), tile(w)) for w in range(CMB_WIN)]
            + [pl.BlockSpec((1, MOE_CHUNK, D_MODEL), tok),
               pl.BlockSpec((1, 1, D_MODEL), lambda l, c, *_: (c[l] // per_b, 0, 0))],
            out_specs=pl.BlockSpec((1, MOE_CHUNK, D_MODEL), tok),
            scratch_shapes=[pltpu.VMEM((MOE_CHUNK, D_MODEL), F32)],
        ),
        out_shape=jax.ShapeDtypeStruct((BATCH, SEQ, D_MODEL), F32),
        compiler_params=_cparams(("arbitrary",)),
        name="moe_combine",
    )(*by_chunk, pos, *([y_sorted] * CMB_WIN), x, gate.reshape(BATCH, 1, D_MODEL))


def moe_layer(x, hx, route, gate, w_gate, w_up, w_down, layer):
    pos, by_tile, by_chunk = moe_plan(route)
    y_sorted = moe_experts(hx, pos, by_tile, w_gate, w_up, w_down, layer)
    return moe_combine(x, y_sorted, pos, by_chunk, gate)


def kernel(x, c, positions, w_ada, b_ada, norm_g, w_in_ab, w_out_ab, gla_w_gate2, gla_b_gate, gla_norm_g, nsa_q_gain, nsa_k_gain, nsa_cmp_pe, nsa_cmp_w1, nsa_cmp_w2, w_in_c, sgu_norm_g, sgu_w_s, sgu_b_s, w_out_c, w_router, router_bias, w_gate, w_up, w_down):
    mod = ada_modulation(c, w_ada, b_ada)
    qk, gv, gr, nq, nkv, misc = inproj0(x, norm_g[0, 0], mod[0, :, 0], mod[0, :, 1], _arrange_w_in(w_in_ab[0]))
    o_a = gla_mixer(qk, gv, gr, misc, gla_w_gate2[0], gla_b_gate[0], gla_norm_g[0])
    o_b = nsa_mixer(nq, nkv, misc, positions, nsa_q_gain[0], nsa_k_gain[0], nsa_cmp_pe[0], nsa_cmp_w1[0], nsa_cmp_w2[0])
    wg, wu, wd = w_gate.astype(BF16), w_up.astype(BF16), w_down.astype(BF16)
    route_args = lambda l: _route_args(norm_g[l, 1], mod[l, :, 3], mod[l, :, 4], w_router, router_bias)
    x1, h, route = outproj0(o_a, o_b, w_out_ab[0], x, mod[0, :, 2], route_args(0))
    x2 = moe_layer(x1, h, route, mod[0, :, 5], wg, wu, wd, 0)
    x3, h, route = gmlp_layer(x2, norm_g[1, 0], mod[1, :, 0], mod[1, :, 1], w_in_c[0], sgu_norm_g[0], sgu_w_s[0],
                              sgu_b_s[0], w_out_c[0], mod[1, :, 2], route_args(1))
    return moe_layer(x3, h, route, mod[1, :, 5], wg, wu, wd, 1)
```

```python
import functools

import numpy as np
import jax
import jax.numpy as jnp
from jax import lax
from jax.experimental import pallas as pl
from jax.experimental.pallas import tpu as pltpu

D_MODEL = 1024
BATCH = 2
SEQ = 8192
DEPTH = 2
N_TOK = BATCH * SEQ

GLA_HEADS = 4
GLA_DK = 64
GLA_DV = 128
GLA_GATE_RANK = 16
GLA_TAU = 16.0
GLA_CHUNK = 64
NSA_HEADS = 8
NSA_KV_GROUPS = 2
NSA_HPG = NSA_HEADS // NSA_KV_GROUPS
NSA_DH = 64
CMP_LEN = 32
CMP_STRIDE = 16
CMP_HIDDEN = 256
SEL_BLOCK = 64
SEL_TOPK = 16
WINDOW = 512
ROPE_THETA = 500000.0
ROT_DIM = NSA_DH // 4
ROT_HALF = ROT_DIM // 2
SGU_CHUNK = 128
SGU_GROUPS = 8
SGU_WIDTH = 2048
SGU_GROUP_DIM = SGU_WIDTH // SGU_GROUPS
N_EXPERTS = 16
N_EXPERT_GROUPS = 4
EXPERTS_PER_GROUP = N_EXPERTS // N_EXPERT_GROUPS
MOE_TOPK = 2
EXPERT_HIDDEN = 512

GLA_QK_W = GLA_HEADS * GLA_DK
GLA_V_W = GLA_HEADS * GLA_DV
NSA_Q_W = NSA_HEADS * NSA_DH
NSA_KV_W = NSA_KV_GROUPS * NSA_DH
N_CMP = (SEQ - CMP_LEN) // CMP_STRIDE + 1
N_CMP_PAD = SEQ // CMP_STRIDE
N_SEL = SEQ // SEL_BLOCK

NORM_EPS = 1e-6
NEG_INF = -1e30
FORCE_BONUS = 1e4

LANES = 128
MIB = 1024 * 1024

F32 = jnp.float32
BF16 = jnp.bfloat16
HI = lax.Precision.HIGHEST
NT_DIMS = (((1,), (1,)), ((), ()))
TN_DIMS = (((0,), (0,)), ((), ()))


def _cparams(sem, vmem_mib=48):
    return pltpu.CompilerParams(dimension_semantics=sem, vmem_limit_bytes=vmem_mib * MIB)


def _rms_mod(x, g, shift, scale):
    y = x * lax.rsqrt(jnp.mean(x * x, axis=-1, keepdims=True) + NORM_EPS) * g
    return y * (1 + scale) + shift


def _silu(x):
    return x * jax.nn.sigmoid(x)


def _log_sigmoid(z):
    return jnp.minimum(z, 0.0) - jnp.log1p(jnp.exp(-jnp.abs(z)))


ADA_TN = 1536
ADA_ROWS = 8


def _ada_kernel(c_ref, w_ref, b_ref, o_ref):
    cond = _silu(c_ref[...])
    o_ref[0] = jnp.dot(cond, w_ref[0], precision=HI, preferred_element_type=F32) + b_ref[0]


def ada_modulation(c, w_ada, b_ada):
    c8 = jnp.zeros((ADA_ROWS, D_MODEL), F32).at[:BATCH].set(c)
    width = 6 * D_MODEL
    out = pl.pallas_call(
        _ada_kernel,
        grid=(DEPTH, width // ADA_TN),
        in_specs=[
            pl.BlockSpec((ADA_ROWS, D_MODEL), lambda l, j: (0, 0)),
            pl.BlockSpec((1, D_MODEL, ADA_TN), lambda l, j: (l, 0, j)),
            pl.BlockSpec((1, 1, ADA_TN), lambda l, j: (l, 0, j)),
        ],
        out_specs=pl.BlockSpec((1, ADA_ROWS, ADA_TN), lambda l, j: (l, 0, j)),
        out_shape=jax.ShapeDtypeStruct((DEPTH, ADA_ROWS, width), F32),
        compiler_params=_cparams(("arbitrary", "arbitrary")),
        name="ada_modulation",
    )(c8, w_ada, b_ada.reshape(DEPTH, 1, width))
    return out[:, :BATCH].reshape(DEPTH, BATCH, 6, D_MODEL)


INPROJ_TM = 512
INPROJ_WIDTHS = (2 * GLA_QK_W, GLA_V_W, GLA_V_W, NSA_Q_W, 6 * NSA_KV_W, LANES)


def _arrange_w_in(w_in):
    o = np.cumsum((0, GLA_QK_W, GLA_QK_W, GLA_V_W, GLA_GATE_RANK, GLA_V_W, NSA_Q_W, 6 * NSA_KV_W, NSA_HEADS * 3))
    gq_gk = w_in[:, o[0]:o[2]]
    gv = w_in[:, o[2]:o[3]]
    glr = w_in[:, o[3]:o[4]]
    gr = w_in[:, o[4]:o[5]]
    nq = w_in[:, o[5]:o[6]]
    nkv = w_in[:, o[6]:o[7]]
    ng = w_in[:, o[7]:o[8]]
    pad = jnp.zeros((D_MODEL, LANES - GLA_GATE_RANK - NSA_HEADS * 3), w_in.dtype)
    return jnp.concatenate([gq_gk, gv, gr, nq, nkv, glr, ng, pad], axis=1).astype(BF16)


def _inproj0_kernel(x_ref, g_ref, sh_ref, sc_ref, w_ref, *o_refs):
    h = _rms_mod(x_ref[0], g_ref[...], sh_ref[0], sc_ref[0]).astype(BF16)
    off = 0
    for o_ref, wd in zip(o_refs, INPROJ_WIDTHS):
        o_ref[0] = jnp.dot(h, w_ref[:, off:off + wd], preferred_element_type=F32)
        off += wd


def inproj0(x, g, shift, scale, w_arranged):
    tm = INPROJ_TM
    wtot = sum(INPROJ_WIDTHS)
    row = lambda b, i: (b, i, 0)
    vec = lambda b, i: (b, 0, 0)
    return pl.pallas_call(
        _inproj0_kernel,
        grid=(BATCH, SEQ // tm),
        in_specs=[
            pl.BlockSpec((1, tm, D_MODEL), row),
            pl.BlockSpec((1, D_MODEL), lambda b, i: (0, 0)),
            pl.BlockSpec((1, 1, D_MODEL), vec),
            pl.BlockSpec((1, 1, D_MODEL), vec),
            pl.BlockSpec((D_MODEL, wtot), lambda b, i: (0, 0)),
        ],
        out_specs=[pl.BlockSpec((1, tm, wd), row) for wd in INPROJ_WIDTHS],
        out_shape=[jax.ShapeDtypeStruct((BATCH, SEQ, wd), F32) for wd in INPROJ_WIDTHS],
        compiler_params=_cparams(("arbitrary", "arbitrary")),
        name="inproj0",
    )(x, g.reshape(1, D_MODEL), shift.reshape(BATCH, 1, D_MODEL), scale.reshape(BATCH, 1, D_MODEL), w_arranged)


GLA_TG = 512


def _gla_chunk_sums():
    i = np.arange(GLA_TG)[:, None]
    j = np.arange(GLA_TG)[None, :]
    same = (i // GLA_CHUNK) == (j // GLA_CHUNK)
    m3 = np.concatenate([same & (j <= i), same & (j % GLA_CHUNK <= GLA_CHUNK // 2), same], axis=0).astype(np.float32)
    return jnp.asarray(np.concatenate([m3, m3], axis=1), BF16)


def _gla_kernel(qk_ref, v_ref, r_ref, misc_ref, w2_ref, bg_ref, og_ref, sums_ref, o_ref, st_ref):
    C, tg = GLA_CHUNK, GLA_TG

    @pl.when(pl.program_id(1) == 0)
    def _():
        st_ref[...] = jnp.zeros_like(st_ref)

    z = jnp.dot(misc_ref[0], w2_ref[...], precision=HI, preferred_element_type=F32) + bg_ref[...]
    la = _log_sigmoid(z) / GLA_TAU
    la_hi = la.astype(BF16)
    la_lo = (la - la_hi.astype(F32)).astype(BF16)
    sums = jnp.dot(sums_ref[...], jnp.concatenate([la_hi, la_lo], axis=0), preferred_element_type=F32)
    bc, b_mid, b_last = sums[0:tg], sums[tg:2 * tg], sums[2 * tg:3 * tg]

    q = qk_ref[0, :, 0:GLA_QK_W] * (GLA_DK ** -0.5)
    k = qk_ref[0, :, GLA_QK_W:2 * GLA_QK_W]
    qd = q * jnp.exp(bc - b_mid)
    kd = (k * jnp.exp(b_mid - bc)).astype(BF16)
    kl = k * jnp.exp(b_last - bc)
    qb = q * jnp.exp(bc)
    dec = jnp.exp(b_last)
    lane = lax.broadcasted_iota(jnp.int32, (1, GLA_QK_W), 1)
    heads = [(lane >= h * GLA_DK) & (lane < (h + 1) * GLA_DK) for h in range(GLA_HEADS)]
    qd_h = [jnp.where(m, qd, 0.0).astype(BF16) for m in heads]
    qb_h = [jnp.where(m, qb, 0.0).astype(BF16) for m in heads]
    kl_h = [jnp.where(m, kl, 0.0).astype(BF16) for m in heads]
    stack = lambda per_head, rows: jnp.concatenate([t[rows] for t in per_head], axis=0)
    stacked_row = lax.broadcasted_iota(jnp.int32, (GLA_HEADS * C, C), 0)
    causal = (stacked_row & (C - 1)) >= lax.broadcasted_iota(jnp.int32, (GLA_HEADS * C, C), 1)
    og = og_ref[...]

    st = st_ref[...]
    for c in range(tg // C):
        rows = slice(c * C, (c + 1) * C)
        v = v_ref[0, rows, :].astype(BF16)
        s = lax.dot_general(stack(qd_h, rows), kd[rows], NT_DIMS, preferred_element_type=F32)
        s = jnp.where(causal, s, 0.0).astype(BF16)
        o_intra = jnp.dot(s, v, preferred_element_type=F32)
        o_inter = lax.dot_general(stack(qb_h, rows), st.astype(BF16), NT_DIMS, preferred_element_type=F32)
        v_stack = jnp.concatenate([v[:, h * GLA_DV:(h + 1) * GLA_DV] for h in range(GLA_HEADS)], axis=0)
        st = st * dec[c * C:c * C + 1] + lax.dot_general(v_stack, stack(kl_h, rows), TN_DIMS, preferred_element_type=F32)
        for h in range(GLA_HEADS):
            hrows = slice(h * C, (h + 1) * C)
            vcols = slice(h * GLA_DV, (h + 1) * GLA_DV)
            o = o_intra[hrows, vcols] + o_inter[hrows]
            on = o * lax.rsqrt(jnp.mean(o * o, axis=-1, keepdims=True) + NORM_EPS) * og
            o_ref[0, rows, vcols] = on * _silu(r_ref[0, rows, vcols])
    st_ref[...] = st


def gla_mixer(qk, v, r, misc, w_gate2, b_gate, out_g):
    tg = GLA_TG
    w2 = jnp.zeros((LANES, GLA_QK_W), F32).at[:GLA_GATE_RANK].set(w_gate2)
    row = lambda b, i: (b, i, 0)
    const = lambda b, i: (0, 0)
    return pl.pallas_call(
        _gla_kernel,
        grid=(BATCH, SEQ // tg),
        in_specs=[
            pl.BlockSpec((1, tg, 2 * GLA_QK_W), row),
            pl.BlockSpec((1, tg, GLA_V_W), row),
            pl.BlockSpec((1, tg, GLA_V_W), row),
            pl.BlockSpec((1, tg, LANES), row),
            pl.BlockSpec((LANES, GLA_QK_W), const),
            pl.BlockSpec((1, GLA_QK_W), const),
            pl.BlockSpec((1, GLA_DV), const),
            pl.BlockSpec((3 * tg, 2 * tg), const),
        ],
        out_specs=pl.BlockSpec((1, tg, GLA_V_W), row),
        out_shape=jax.ShapeDtypeStruct((BATCH, SEQ, GLA_V_W), F32),
        scratch_shapes=[pltpu.VMEM((GLA_DV, GLA_QK_W), F32)],
        compiler_params=_cparams(("arbitrary", "arbitrary")),
        name="gla_mixer",
    )(qk, v, r, misc, w2, b_gate.reshape(1, GLA_QK_W), out_g.reshape(1, GLA_DV), _gla_chunk_sums())


POS_SIDE = 128


def _rope_table_kernel(freq_ref, pos_ref, cos_ref, sin_ref):
    pos = pos_ref[...].astype(F32)
    for f in range(ROT_HALF):
        ang = pos * freq_ref[f]
        cos_ref[f] = jnp.cos(ang)
        sin_ref[f] = jnp.sin(ang)


def rope_tables(positions):
    inv_freq = jnp.float32(ROPE_THETA) ** (-jnp.arange(ROT_HALF, dtype=F32) / ROT_HALF)
    shp = jax.ShapeDtypeStruct((ROT_HALF, POS_SIDE, POS_SIDE), F32)
    cos, sin = pl.pallas_call(
        _rope_table_kernel,
        in_specs=[pl.BlockSpec(memory_space=pltpu.SMEM), pl.BlockSpec(memory_space=pltpu.VMEM)],
        out_specs=[pl.BlockSpec(memory_space=pltpu.VMEM)] * 2,
        out_shape=[shp, shp],
        name="rope_tables",
    )(inv_freq, positions.reshape(POS_SIDE, POS_SIDE))
    return jnp.concatenate([cos, sin], axis=0).reshape(ROT_DIM, N_TOK).T.reshape(BATCH, SEQ, ROT_DIM)


def _rope_placement():
    place = np.zeros((ROT_DIM, 3 * LANES), np.float32)
    const = np.zeros((1, 3 * LANES), np.float32)
    for lane in range(LANES):
        i = lane % NSA_DH
        if i < ROT_HALF:
            place[i, lane] = 1.0
            place[ROT_HALF + i, LANES + lane] = -1.0
        elif i < ROT_DIM:
            place[i - ROT_HALF, lane] = 1.0
            place[i, 2 * LANES + lane] = 1.0
        else:
            const[0, lane] = 1.0
    return jnp.asarray(place), jnp.asarray(const)


def _lane_tables(cs, place_ref, const_ref):
    tab = jnp.dot(cs, place_ref[...], precision=HI, preferred_element_type=F32) + const_ref[...]
    return tab[:, 0:LANES], tab[:, LANES:2 * LANES], tab[:, 2 * LANES:3 * LANES]


def _block_diag_ones2(width):
    h = np.arange(width) // NSA_DH
    bd = (h[:, None] == h[None, :]).astype(np.float32)
    return jnp.asarray(np.concatenate([bd, bd], axis=0), BF16)


def _head_norm_rope(x, gain, bd2, c, sm, sp):
    width = x.shape[-1]
    reps = width // LANES
    sq = x * x
    sq_hi = sq.astype(BF16)
    sq_lo = (sq - sq_hi.astype(F32)).astype(BF16)
    ss = jnp.dot(jnp.concatenate([sq_hi, sq_lo], axis=1), bd2, preferred_element_type=F32)
    y = x * lax.rsqrt(ss * (1.0 / NSA_DH) + NORM_EPS) * gain
    tile = lambda t: jnp.concatenate([t] * reps, axis=1) if reps > 1 else t
    return (y * tile(c) + pltpu.roll(y, width - ROT_HALF, 1) * tile(sm) + pltpu.roll(y, ROT_HALF, 1) * tile(sp))


PREP_TM = 512


def _prep_kernel(q_ref, ks_ref, kw_ref, cs_ref, place_ref, const_ref, gq_ref, gk_ref, bdq_ref, bdk_ref,
                 qo_ref, kso_ref, kwo_ref):
    c, sm, sp = _lane_tables(cs_ref[0], place_ref, const_ref)
    bdk = bdk_ref[...]
    q = _head_norm_rope(q_ref[0], gq_ref[...], bdq_ref[...], c, sm, sp) * (NSA_DH ** -0.5)
    qo_ref[0] = q.T.reshape(NSA_HEADS, NSA_DH, PREP_TM)
    kso_ref[0] = _head_norm_rope(ks_ref[0], gk_ref[0:1, :], bdk, c, sm, sp)
    kwo_ref[0] = _head_norm_rope(kw_ref[0], gk_ref[1:2, :], bdk, c, sm, sp)


def nsa_prep(nq, nkv, cs, q_gain, k_gain):
    tm = PREP_TM
    row = lambda b, i: (b, i, 0)
    const = lambda b, i: (0, 0)
    gq = jnp.tile(q_gain, NSA_HEADS).reshape(1, NSA_Q_W)
    gk = jnp.stack([jnp.tile(k_gain[1], NSA_KV_GROUPS), jnp.tile(k_gain[2], NSA_KV_GROUPS)])
    return pl.pallas_call(
        _prep_kernel,
        grid=(BATCH, SEQ // tm),
        in_specs=[
            pl.BlockSpec((1, tm, NSA_Q_W), row),
            pl.BlockSpec((1, tm, NSA_KV_W), lambda b, i: (b, i, 2)),
            pl.BlockSpec((1, tm, NSA_KV_W), lambda b, i: (b, i, 4)),
            pl.BlockSpec((1, tm, ROT_DIM), row),
            pl.BlockSpec((ROT_DIM, 3 * LANES), const),
            pl.BlockSpec((1, 3 * LANES), const),
            pl.BlockSpec((1, NSA_Q_W), const),
            pl.BlockSpec((2, NSA_KV_W), const),
            pl.BlockSpec((2 * NSA_Q_W, NSA_Q_W), const),
            pl.BlockSpec((2 * NSA_KV_W, NSA_KV_W), const),
        ],
        out_specs=[pl.BlockSpec((1, NSA_HEADS, NSA_DH, tm), lambda b, i: (b, 0, 0, i)),
                   pl.BlockSpec((1, tm, NSA_KV_W), row), pl.BlockSpec((1, tm, NSA_KV_W), row)],
        out_shape=[jax.ShapeDtypeStruct((BATCH, NSA_HEADS, NSA_DH, SEQ), F32),
                   jax.ShapeDtypeStruct((BATCH, SEQ, NSA_KV_W), F32), jax.ShapeDtypeStruct((BATCH, SEQ, NSA_KV_W), F32)],
        compiler_params=_cparams(("arbitrary", "arbitrary")),
        name="nsa_prep",
    )(nq, nkv, nkv, cs, *_rope_placement(), gq, gk, _block_diag_ones2(NSA_Q_W), _block_diag_ones2(NSA_KV_W))


SEG_W = CMP_STRIDE * NSA_DH


def _cmp_kernel(xk_ref, xv_ref, pe_ref, w1_ref, w2_ref, gain_ref, cs_ref, place_ref, const_ref, bd_ref, ko_ref, vo_ref):
    def compress(x_ref, kv):
        out = jnp.zeros((N_CMP_PAD, LANES), F32)
        for g in range(NSA_KV_GROUPS):
            x = x_ref[0, g]
            ha = jnp.dot(x + pe_ref[kv, 0], w1_ref[kv, 0:SEG_W, :], precision=HI, preferred_element_type=F32)
            hb = jnp.dot(x + pe_ref[kv, 1], w1_ref[kv, SEG_W:2 * SEG_W, :], precision=HI, preferred_element_type=F32)
            hid = ha + pltpu.roll(hb, N_CMP_PAD - 1, 0)
            out += jnp.dot(jax.nn.gelu(hid), w2_ref[kv, g], precision=HI, preferred_element_type=F32)
        return out

    c, sm, sp = _lane_tables(cs_ref[0], place_ref, const_ref)
    ko_ref[0] = _head_norm_rope(compress(xk_ref, 0), gain_ref[...], bd_ref[...], c, sm, sp)
    vo_ref[0] = compress(xv_ref, 1)


def nsa_compress(xk, xv, cmp_pe, cmp_w1, cmp_w2, k_gain0, cs_last):
    pe = cmp_pe.reshape(2, 2, 1, SEG_W)
    w2 = jnp.zeros((2, NSA_KV_GROUPS, CMP_HIDDEN, LANES), F32)
    for g in range(NSA_KV_GROUPS):
        w2 = w2.at[:, g, :, g * NSA_DH:(g + 1) * NSA_DH].set(cmp_w2)
    seg = pl.BlockSpec((1, NSA_KV_GROUPS, N_CMP_PAD, SEG_W), lambda b: (b, 0, 0, 0))
    tab = pl.BlockSpec((1, N_CMP_PAD, LANES), lambda b: (b, 0, 0))
    full = lambda shape: pl.BlockSpec(shape, lambda b: (0,) * len(shape))
    return pl.pallas_call(
        _cmp_kernel,
        grid=(BATCH,),
        in_specs=[seg, seg, full((2, 2, 1, SEG_W)), full((2, 2 * SEG_W, CMP_HIDDEN)),
                  full((2, NSA_KV_GROUPS, CMP_HIDDEN, LANES)), full((1, LANES)),
                  pl.BlockSpec((1, N_CMP_PAD, ROT_DIM), lambda b: (b, 0, 0)), full((ROT_DIM, 3 * LANES)),
                  full((1, 3 * LANES)), full((2 * LANES, LANES))],
        out_specs=[tab, tab],
        out_shape=[jax.ShapeDtypeStruct((BATCH, N_CMP_PAD, LANES), F32)] * 2,
        compiler_params=_cparams(("arbitrary",)),
        name="nsa_compress",
    )(xk, xv, pe, cmp_w1, w2, jnp.tile(k_gain0, NSA_KV_GROUPS).reshape(1, LANES), cs_last, *_rope_placement(),
      _block_diag_ones2(LANES))


CA_TQ = 256
SUBLANES = 8


CA_COLS = NSA_HPG * CA_TQ
CMP_PER_SEL = SEL_BLOCK // CMP_STRIDE


def split3_keys(k):
    hi = k.astype(BF16)
    lo = (k - hi.astype(F32)).astype(BF16)
    return jnp.concatenate([hi, lo, hi], axis=-1)


def _top_k_rows(score, k):
    rows, cols = score.shape
    row = lax.broadcasted_iota(jnp.int32, (rows, cols), 0).astype(F32)
    taken = jnp.zeros((rows, cols), F32)
    left = score
    for _ in range(k):
        top = jnp.max(left, axis=0, keepdims=True)
        first = jnp.min(jnp.where(left == top, row, float(rows)), axis=0, keepdims=True)
        hit = row == first
        taken = jnp.where(hit, 1.0, taken)
        left = jnp.where(hit, -jnp.inf, left)
    return taken


def _cattn_kernel(q_ref, kc_ref, vct_ref, gl_ref, o_ref, sel_ref, q3_ref, ps_ref):
    tq = CA_TQ
    q0 = pl.program_id(1) * tq
    lanes4 = lambda t: jnp.concatenate([t] * NSA_HPG, axis=1)
    cend = lax.broadcasted_iota(jnp.int32, (N_CMP_PAD, tq), 0) * CMP_STRIDE + (CMP_LEN - 1)
    tc = q0 + lax.broadcasted_iota(jnp.int32, (N_CMP_PAD, tq), 1)
    cmask = lanes4(cend <= tc)
    jj = lax.broadcasted_iota(jnp.int32, (N_SEL, tq), 0)
    tt = q0 + lax.broadcasted_iota(jnp.int32, (N_SEL, tq), 1)
    cur = jnp.right_shift(tt, 6)
    forced = (jj == 0) | (jj == cur) | (jj == cur - 1)
    valid = jj * SEL_BLOCK <= tt

    for g in range(NSA_KV_GROUPS):
        heads = range(g * NSA_HPG, (g + 1) * NSA_HPG)
        for n, h in enumerate(heads):
            q = q_ref[0, h]
            hi = q.astype(BF16)
            lo = (q - hi.astype(F32)).astype(BF16)
            for t, part in enumerate((hi, hi, lo)):
                q3_ref[g, t * NSA_DH:(t + 1) * NSA_DH, n * tq:(n + 1) * tq] = part
        s = jnp.dot(kc_ref[0, g], q3_ref[g], preferred_element_type=F32)
        s = jnp.where(cmask, s, NEG_INF)
        m = jnp.max(s, axis=0, keepdims=True)
        e = jnp.where(cmask, jnp.exp(s - m), 0.0)
        l = jnp.sum(e, axis=0, keepdims=True)
        p = e / jnp.where(l > 0.0, l, 1.0)
        gate = jnp.concatenate([jax.nn.sigmoid(gl_ref[0, h, 0:1, :]) for h in heads], axis=1)
        o = jnp.dot(vct_ref[0, g], p.astype(BF16), preferred_element_type=F32) * gate
        for n, h in enumerate(heads):
            o_ref[0, h] = o[:, n * tq:(n + 1) * tq]
        psum = functools.reduce(jnp.add, [p[:, n * tq:(n + 1) * tq] for n in range(NSA_HPG)])
        for n in range(tq // LANES):
            ps_ref[g, n] = psum[:, n * LANES:(n + 1) * LANES]

        every4th = lambda r: jnp.concatenate(
            [ps_ref[g, n, pl.ds(r, N_SEL, stride=CMP_PER_SEL), :] for n in range(tq // LANES)], axis=1)
        starts_in = [every4th(r) for r in range(CMP_PER_SEL)]
        from_prev = jnp.where(jj >= 1, pltpu.roll(starts_in[CMP_PER_SEL - 1], 1, 0), 0.0)
        imp = functools.reduce(jnp.add, starts_in) + from_prev
        score = jnp.where(valid, imp + jnp.where(forced, FORCE_BONUS, 0.0), NEG_INF)
        sel_ref[0, g] = jnp.where(valid, _top_k_rows(score, SEL_TOPK), 0.0)


def nsa_cmp_attn(q_t, kcmp, vcmp_t, gl_t):
    tq = CA_TQ
    G = NSA_KV_GROUPS
    return pl.pallas_call(
        _cattn_kernel,
        grid=(BATCH, SEQ // tq),
        in_specs=[
            pl.BlockSpec((1, NSA_HEADS, NSA_DH, tq), lambda b, i: (b, 0, 0, i)),
            pl.BlockSpec((1, G, N_CMP_PAD, 3 * NSA_DH), lambda b, i: (b, 0, 0, 0)),
            pl.BlockSpec((1, G, NSA_DH, N_CMP_PAD), lambda b, i: (b, 0, 0, 0)),
            pl.BlockSpec((1, NSA_HEADS, 3, tq), lambda b, i: (b, 0, 0, i)),
        ],
        out_specs=[pl.BlockSpec((1, NSA_HEADS, NSA_DH, tq), lambda b, i: (b, 0, 0, i)),
                   pl.BlockSpec((1, G, N_SEL, tq), lambda b, i: (b, 0, 0, i))],
        out_shape=[jax.ShapeDtypeStruct((BATCH, NSA_HEADS, NSA_DH, SEQ), F32),
                   jax.ShapeDtypeStruct((BATCH, G, N_SEL, SEQ), F32)],
        scratch_shapes=[pltpu.VMEM((G, 3 * NSA_DH, CA_COLS), BF16), pltpu.VMEM((G, tq // LANES, N_CMP_PAD, LANES), F32)],
        compiler_params=_cparams(("arbitrary", "arbitrary")),
        name="nsa_cmp_attn",
    )(q_t, split3_keys(kcmp), vcmp_t.astype(BF16), gl_t)


SA_TQ = 256
SA_TK = 1024
SA_PARTS = 2
SA_PART = SA_TK // SA_PARTS
M_INIT = -1e20


SA_COLS = NSA_HPG * SA_TQ
SA_BLOCKS = SA_TK // SEL_BLOCK


VT_ROWS = NSA_DH + 16


def value_slab_t(v_t):
    ones = jnp.ones(v_t.shape[:2] + (1, SEQ), BF16)
    zeros = jnp.zeros(v_t.shape[:2] + (VT_ROWS - NSA_DH - 1, SEQ), BF16)
    return jnp.concatenate([v_t.astype(BF16), ones, zeros], axis=2)


def sel_key_slab(ksel):
    blk = (np.arange(SEQ) % SA_TK) // SEL_BLOCK
    onehot = (blk[:, None] == np.arange(LANES - NSA_DH)[None, :]).astype(np.float32)
    onehot = jnp.broadcast_to(jnp.asarray(onehot, BF16), ksel.shape[:3] + (LANES - NSA_DH,))
    return jnp.concatenate([ksel.astype(BF16), onehot], axis=-1)


def _sattn_kernel(q_ref, k_ref, vt_ref, sel_ref, gl_ref, prev_ref, o_ref, qa_ref, acc_ref):
    tq, tk = SA_TQ, SA_TK
    i = pl.program_id(1)
    groups = range(NSA_KV_GROUPS)
    for g in groups:
        for h in range(NSA_HPG):
            qa_ref[g, 0:NSA_DH, h * tq:(h + 1) * tq] = q_ref[0, g * NSA_HPG + h].astype(BF16)
        qa_ref[g, NSA_DH:LANES, :] = jnp.zeros((LANES - NSA_DH, SA_COLS), BF16)
    acc_ref[...] = jnp.zeros_like(acc_ref)
    lanes4 = lambda t: jnp.concatenate([t] * NSA_HPG, axis=1)

    def key_tile(g, kt, m_prev, diagonal):
        selrows = sel_ref[0, g, pl.ds(pl.multiple_of(kt * SA_BLOCKS, SA_BLOCKS), SA_BLOCKS), :]
        qa_ref[g, NSA_DH:NSA_DH + SA_BLOCKS, :] = lanes4(jnp.where(selrows > 0.5, 0.0, NEG_INF)).astype(BF16)
        ss = []
        for part in range(SA_PARTS):
            keys = pl.ds(pl.multiple_of(kt * tk + part * SA_PART, SA_PART), SA_PART)
            s = jnp.dot(k_ref[0, g, keys, :], qa_ref[g], preferred_element_type=F32)
            if diagonal:
                kpos = kt * tk + part * SA_PART + lax.broadcasted_iota(jnp.int32, (SA_PART, tq), 0)
                tt = i * tq + lax.broadcasted_iota(jnp.int32, (SA_PART, tq), 1)
                s = s + lanes4(jnp.where(kpos <= tt, 0.0, NEG_INF))
            ss.append(s.astype(BF16))
        m_tile = functools.reduce(jnp.maximum, [jnp.max(s, axis=0, keepdims=True) for s in ss])
        m_new = jnp.maximum(m_prev, m_tile.astype(F32))
        acc = jnp.exp(m_prev - m_new) * acc_ref[g]
        for part in range(SA_PARTS):
            keys = pl.ds(pl.multiple_of(kt * tk + part * SA_PART, SA_PART), SA_PART)
            p = jnp.exp(ss[part] - m_new.astype(BF16))
            acc += jnp.dot(vt_ref[0, g, :, keys], p, preferred_element_type=F32)
        acc_ref[g] = acc
        return m_new

    n_full = (i * tq) // tk
    m0 = tuple(jnp.full((1, SA_COLS), M_INIT, F32) for _ in groups)
    ms = lax.fori_loop(0, n_full, lambda kt, ms: tuple(key_tile(g, kt, ms[g], False) for g in groups), m0)
    for g in groups:
        key_tile(g, n_full, ms[g], True)
        heads = range(g * NSA_HPG, (g + 1) * NSA_HPG)
        gate = jnp.concatenate([jax.nn.sigmoid(gl_ref[0, h, 1:2, :]) for h in heads], axis=1)
        out = acc_ref[g, 0:NSA_DH, :] / acc_ref[g, NSA_DH:NSA_DH + 1, :] * gate
        for n, h in enumerate(heads):
            o_ref[0, h] = prev_ref[0, h] + out[:, n * tq:(n + 1) * tq]


def nsa_sel_attn(q_t, k_slab, vsel_t, sel_t, gl_t, prev):
    tq = SA_TQ
    G = NSA_KV_GROUPS
    ospec = pl.BlockSpec((1, NSA_HEADS, NSA_DH, tq), lambda b, i: (b, 0, 0, i))
    return pl.pallas_call(
        _sattn_kernel,
        grid=(BATCH, SEQ // tq),
        in_specs=[
            ospec,
            pl.BlockSpec((1, G, SEQ, LANES), lambda b, i: (b, 0, 0, 0)),
            pl.BlockSpec((1, G, VT_ROWS, SEQ), lambda b, i: (b, 0, 0, 0)),
            pl.BlockSpec((1, G, N_SEL, tq), lambda b, i: (b, 0, 0, i)),
            pl.BlockSpec((1, NSA_HEADS, 3, tq), lambda b, i: (b, 0, 0, i)),
            ospec,
        ],
        out_specs=ospec,
        out_shape=jax.ShapeDtypeStruct((BATCH, NSA_HEADS, NSA_DH, SEQ), F32),
        scratch_shapes=[pltpu.VMEM((G, LANES, SA_COLS), BF16), pltpu.VMEM((G, VT_ROWS, SA_COLS), F32)],
        input_output_aliases={5: 0},
        compiler_params=_cparams(("arbitrary", "arbitrary")),
        name="nsa_sel_attn",
    )(q_t, k_slab, vsel_t, sel_t, gl_t, prev)


WA_TQ = 256
WA_TILES = WINDOW // WA_TQ + 1


def _window_bias():
    kl = np.arange(WA_TILES * WA_TQ)[:, None]
    ql = np.arange(WA_TQ)[None, :]
    diff = ql - kl + WINDOW
    return jnp.asarray(np.where((diff >= 0) & (diff < WINDOW), 0.0, NEG_INF).astype(np.float32))


def _wattn_kernel(q_ref, k0_ref, k1_ref, k2_ref, v0_ref, v1_ref, v2_ref, bias_ref, gl_ref, prev_ref, o_ref):
    tq = WA_TQ
    i = pl.program_id(1)
    k_refs = (k0_ref, k1_ref, k2_ref)
    v_refs = (v0_ref, v1_ref, v2_ref)
    lanes4 = lambda t: jnp.concatenate([t] * NSA_HPG, axis=1)
    biases = []
    for d in range(WA_TILES):
        in_seq = i - (WA_TILES - 1) + d >= 0
        biases.append(lanes4(jnp.where(in_seq, bias_ref[d * tq:(d + 1) * tq, :], NEG_INF)))
    for g in range(NSA_KV_GROUPS):
        heads = range(g * NSA_HPG, (g + 1) * NSA_HPG)
        q = jnp.concatenate([q_ref[0, h] for h in heads], axis=1).astype(BF16)
        ss = [(jnp.dot(k_refs[d][0, g], q, preferred_element_type=F32) + biases[d]).astype(BF16) for d in range(WA_TILES)]
        m = functools.reduce(jnp.maximum, [jnp.max(s, axis=0, keepdims=True) for s in ss])
        acc = functools.reduce(jnp.add, [jnp.dot(v_refs[d][0, g], jnp.exp(ss[d] - m), preferred_element_type=F32)
                                         for d in range(WA_TILES)])
        gate = jnp.concatenate([jax.nn.sigmoid(gl_ref[0, h, 2:3, :]) for h in heads], axis=1)
        out = acc[0:NSA_DH] / acc[NSA_DH:NSA_DH + 1] * gate
        for n, h in enumerate(heads):
            o_ref[0, h] = prev_ref[0, h] + out[:, n * tq:(n + 1) * tq]


def nsa_win_attn(q_t, kwin, vwin_t, gl_t, prev):
    tq = WA_TQ
    G = NSA_KV_GROUPS
    qspec = pl.BlockSpec((1, NSA_HEADS, NSA_DH, tq), lambda b, i: (b, 0, 0, i))
    tile = lambda d: (lambda i: jnp.maximum(i - (WA_TILES - 1) + d, 0))
    kspec = lambda d: pl.BlockSpec((1, G, tq, NSA_DH), lambda b, i: (b, 0, tile(d)(i), 0))
    vspec = lambda d: pl.BlockSpec((1, G, VT_ROWS, tq), lambda b, i: (b, 0, 0, tile(d)(i)))
    return pl.pallas_call(
        _wattn_kernel,
        grid=(BATCH, SEQ // tq),
        in_specs=[qspec] + [kspec(d) for d in range(WA_TILES)] + [vspec(d) for d in range(WA_TILES)] + [
            pl.BlockSpec((WA_TILES * tq, tq), lambda b, i: (0, 0)),
            pl.BlockSpec((1, NSA_HEADS, 3, tq), lambda b, i: (b, 0, 0, i)),
            qspec,
        ],
        out_specs=qspec,
        out_shape=jax.ShapeDtypeStruct((BATCH, NSA_HEADS, NSA_DH, SEQ), F32),
        input_output_aliases={2 * WA_TILES + 3: 0},
        compiler_params=_cparams(("arbitrary", "arbitrary")),
        name="nsa_win_attn",
    )(q_t, *([kwin] * WA_TILES), *([vwin_t] * WA_TILES), _window_bias(), gl_t, prev)


def nsa_mixer(nq, nkv, misc, positions, q_gain, k_gain, cmp_pe, cmp_w1, cmp_w2):
    cs = rope_tables(positions)
    q_t, ks_r, kw_r = nsa_prep(nq, nkv, cs, q_gain, k_gain)
    group_major = lambda t: t.reshape(BATCH, SEQ, NSA_KV_GROUPS, NSA_DH).transpose(0, 2, 1, 3)
    group_major_t = lambda t: t.reshape(BATCH, SEQ, NSA_KV_GROUPS, NSA_DH).transpose(0, 2, 3, 1)
    col = lambda n: nkv[..., n * NSA_KV_W:(n + 1) * NSA_KV_W]
    segs = lambda t: group_major(t).reshape(BATCH, NSA_KV_GROUPS, N_CMP_PAD, SEG_W)
    last = jnp.minimum(jnp.arange(N_CMP_PAD) * CMP_STRIDE + CMP_LEN - 1, SEQ - 1)
    kcmp, vcmp = nsa_compress(segs(col(0)), segs(col(1)), cmp_pe, cmp_w1, cmp_w2, k_gain[0], cs[:, last])
    kcmp = kcmp.reshape(BATCH, N_CMP_PAD, NSA_KV_GROUPS, NSA_DH).transpose(0, 2, 1, 3)
    vcmp_t = vcmp.reshape(BATCH, N_CMP_PAD, NSA_KV_GROUPS, NSA_DH).transpose(0, 2, 3, 1)
    gl_t = misc[..., GLA_GATE_RANK:GLA_GATE_RANK + NSA_HEADS * 3].reshape(BATCH, SEQ, NSA_HEADS, 3).transpose(0, 2, 3, 1)
    o_t, sel_t = nsa_cmp_attn(q_t, kcmp, vcmp_t, gl_t)
    o_t = nsa_sel_attn(q_t, sel_key_slab(group_major(ks_r)), value_slab_t(group_major_t(col(3))), sel_t, gl_t, o_t)
    o_t = nsa_win_attn(q_t, group_major(kw_r).astype(BF16), value_slab_t(group_major_t(col(5))), gl_t, o_t)
    return o_t


ROUTE_ROWS = 8
HX_W = D_MODEL + 3 * LANES


def _top2_sum(a, b, c, d):
    hi1, lo1 = jnp.maximum(a, b), jnp.minimum(a, b)
    hi2, lo2 = jnp.maximum(c, d), jnp.minimum(c, d)
    return jnp.maximum(hi1, hi2) + jnp.maximum(jnp.minimum(hi1, hi2), jnp.maximum(lo1, lo2))


def _moe_prenorm_route(xn, g_ref, sh_ref, sc_ref, wr_ref, rb_ref, hx_ref, route_ref):
    h = _rms_mod(xn, g_ref[...], sh_ref[0], sc_ref[0])
    logits = lax.dot_general(wr_ref[...], h, NT_DIMS, precision=HI, preferred_element_type=F32)
    scores = jax.nn.sigmoid(logits)
    sel = scores + rb_ref[...]
    epg = EXPERTS_PER_GROUP
    srow = lambda e: sel[e:e + 1, :]
    grp = [_top2_sum(*[srow(epg * g + r) for r in range(epg)]) for g in range(N_EXPERT_GROUPS)]
    best, gi = grp[0], jnp.zeros_like(grp[0], dtype=jnp.int32)
    for g in range(1, N_EXPERT_GROUPS):
        better = grp[g] > best
        gi = jnp.where(better, g, gi)
        best = jnp.where(better, grp[g], best)

    def in_group(mat, r):
        out = mat[r:r + 1, :]
        for g in range(1, N_EXPERT_GROUPS):
            out = jnp.where(gi == g, mat[epg * g + r:epg * g + r + 1, :], out)
        return out

    v = [in_group(sel, r) for r in range(epg)]
    sc = [in_group(scores, r) for r in range(epg)]
    b1, i1, w1 = v[0], jnp.zeros_like(gi), sc[0]
    for r in range(1, epg):
        better = v[r] > b1
        i1 = jnp.where(better, r, i1)
        w1 = jnp.where(better, sc[r], w1)
        b1 = jnp.where(better, v[r], b1)
    b2 = jnp.full_like(b1, -3e38)
    i2, w2 = jnp.zeros_like(gi), jnp.zeros_like(w1)
    for r in range(epg):
        better = (i1 != r) & (v[r] > b2)
        i2 = jnp.where(better, r, i2)
        w2 = jnp.where(better, sc[r], w2)
        b2 = jnp.where(better, v[r], b2)
    tot = w1 + w2
    w1, w2 = w1 / tot, w2 / tot
    zero = jnp.zeros_like(w1)
    route_ref[0] = jnp.concatenate([gi.astype(F32)] + [zero] * (ROUTE_ROWS - 1), axis=0)
    wrows = [jnp.where(i1 == r, w1, jnp.where(i2 == r, w2, 0.0)) for r in range(epg)]
    wmat = jnp.concatenate(wrows + [jnp.zeros((LANES - epg, w1.shape[1]), F32)], axis=0).T
    w_hi = wmat.astype(BF16)
    rest = wmat - w_hi.astype(F32)
    w_mid = rest.astype(BF16)
    w_lo = (rest - w_mid.astype(F32)).astype(BF16)
    hx_ref[0, :, 0:D_MODEL] = h.astype(BF16)
    for n, part in enumerate((w_hi, w_mid, w_lo)):
        hx_ref[0, :, D_MODEL + n * LANES:D_MODEL + (n + 1) * LANES] = part


def _route_specs(tm, row, vec, const):
    in_specs = [pl.BlockSpec((1, D_MODEL), const), pl.BlockSpec((1, 1, D_MODEL), vec), pl.BlockSpec((1, 1, D_MODEL), vec),
                pl.BlockSpec((N_EXPERTS, D_MODEL), const), pl.BlockSpec((N_EXPERTS, 1), const)]
    out_specs = [pl.BlockSpec((1, tm, HX_W), row), pl.BlockSpec((1, ROUTE_ROWS, tm), lambda b, i: (b, 0, i))]
    out_shape = [jax.ShapeDtypeStruct((BATCH, SEQ, HX_W), BF16), jax.ShapeDtypeStruct((BATCH, ROUTE_ROWS, SEQ), F32)]
    return in_specs, out_specs, out_shape


def _route_args(g, shift, scale, w_router, router_bias):
    return (g.reshape(1, D_MODEL), shift.reshape(BATCH, 1, D_MODEL), scale.reshape(BATCH, 1, D_MODEL),
            w_router.T, router_bias.reshape(N_EXPERTS, 1))


OUTPROJ_TM = 512


def _outproj0_kernel(oa_ref, ob_ref, w_ref, x_ref, gate_ref, g_ref, sh_ref, sc_ref, wr_ref, rb_ref,
                     xo_ref, h_ref, route_ref):
    y = jnp.dot(oa_ref[0].astype(BF16), w_ref[0:GLA_V_W, :], preferred_element_type=F32)
    ob_t = ob_ref[0].reshape(NSA_Q_W, OUTPROJ_TM).astype(BF16)
    y += lax.dot_general(ob_t, w_ref[GLA_V_W:GLA_V_W + NSA_Q_W, :], TN_DIMS, preferred_element_type=F32)
    xn = x_ref[0] + gate_ref[0] * y
    xo_ref[0] = xn
    _moe_prenorm_route(xn, g_ref, sh_ref, sc_ref, wr_ref, rb_ref, h_ref, route_ref)


def outproj0(o_a, o_b, w_out, x, gate, route_args):
    tm = OUTPROJ_TM
    row = lambda b, i: (b, i, 0)
    vec = lambda b, i: (b, 0, 0)
    const = lambda b, i: (0, 0)
    r_in, r_out, r_shape = _route_specs(tm, row, vec, const)
    return pl.pallas_call(
        _outproj0_kernel,
        grid=(BATCH, SEQ // tm),
        in_specs=[pl.BlockSpec((1, tm, GLA_V_W), row), pl.BlockSpec((1, NSA_HEADS, NSA_DH, tm), lambda b, i: (b, 0, 0, i)),
                  pl.BlockSpec((GLA_V_W + NSA_Q_W, D_MODEL), const), pl.BlockSpec((1, tm, D_MODEL), row),
                  pl.BlockSpec((1, 1, D_MODEL), vec)] + r_in,
        out_specs=[pl.BlockSpec((1, tm, D_MODEL), row)] + r_out,
        out_shape=[jax.ShapeDtypeStruct((BATCH, SEQ, D_MODEL), F32)] + r_shape,
        compiler_params=_cparams(("arbitrary", "arbitrary")),
        name="outproj0",
    )(o_a, o_b, w_out.astype(BF16), x, gate.reshape(BATCH, 1, D_MODEL), *route_args)


GMLP_TM = 512


def _gmlp_kernel(x_ref, g1_ref, sh1_ref, sc1_ref, win_ref, ng_ref, ws_ref, bs_ref, wout_ref, gate_ref,
                 g_ref, sh_ref, sc_ref, wr_ref, rb_ref, xo_ref, h_ref, route_ref, gated_ref, v_ref):
    x = x_ref[0]
    h = _rms_mod(x, g1_ref[...], sh1_ref[0], sc1_ref[0]).astype(BF16)
    group_cols = lambda g: slice(g * SGU_GROUP_DIM, (g + 1) * SGU_GROUP_DIM)
    ssq = jnp.zeros((GMLP_TM, LANES), F32)
    for g in range(SGU_GROUPS):
        lo = SGU_WIDTH + g * SGU_GROUP_DIM
        v = jax.nn.gelu(jnp.dot(h, win_ref[:, lo:lo + SGU_GROUP_DIM], preferred_element_type=F32))
        v_ref[:, group_cols(g)] = v
        ssq += functools.reduce(jnp.add, [v[:, n * LANES:(n + 1) * LANES] ** 2 for n in range(SGU_GROUP_DIM // LANES)])
    rs = lax.rsqrt(jnp.sum(ssq, axis=-1, keepdims=True) * (1.0 / SGU_WIDTH) + NORM_EPS)
    ri = lax.broadcasted_iota(jnp.int32, (SGU_CHUNK, SGU_CHUNK), 0)
    ci = lax.broadcasted_iota(jnp.int32, (SGU_CHUNK, SGU_CHUNK), 1)
    for g in range(SGU_GROUPS):
        cols = group_cols(g)
        u = jax.nn.gelu(jnp.dot(h, win_ref[:, cols], preferred_element_type=F32))
        vn = (v_ref[:, cols] * rs * ng_ref[:, cols]).astype(BF16)
        w = jnp.where(ri >= ci, ws_ref[g], 0.0).astype(BF16)
        for c in range(GMLP_TM // SGU_CHUNK):
            rows = slice(c * SGU_CHUNK, (c + 1) * SGU_CHUNK)
            mix = jnp.dot(w, vn[rows], preferred_element_type=F32) + bs_ref[:, g:g + 1]
            gated_ref[rows, cols] = (u[rows] * mix).astype(BF16)
    y = jnp.dot(gated_ref[...], wout_ref[...], preferred_element_type=F32)
    xn = x + gate_ref[0] * y
    xo_ref[0] = xn
    _moe_prenorm_route(xn, g_ref, sh_ref, sc_ref, wr_ref, rb_ref, h_ref, route_ref)


def gmlp_layer(x, g1, shift1, scale1, w_in, norm_g, w_s, b_s, w_out, gate, route_args):
    tm = GMLP_TM
    row = lambda b, i: (b, i, 0)
    vec = lambda b, i: (b, 0, 0)
    const = lambda b, i: (0, 0)
    r_in, r_out, r_shape = _route_specs(tm, row, vec, const)
    vspec = pl.BlockSpec((1, 1, D_MODEL), vec)
    return pl.pallas_call(
        _gmlp_kernel,
        grid=(BATCH, SEQ // tm),
        in_specs=[pl.BlockSpec((1, tm, D_MODEL), row), pl.BlockSpec((1, D_MODEL), const), vspec, vspec,
                  pl.BlockSpec((D_MODEL, 2 * SGU_WIDTH), const), pl.BlockSpec((1, SGU_WIDTH), const),
                  pl.BlockSpec((SGU_GROUPS, SGU_CHUNK, SGU_CHUNK), lambda b, i: (0, 0, 0)),
                  pl.BlockSpec((SGU_CHUNK, SGU_GROUPS), const), pl.BlockSpec((SGU_WIDTH, D_MODEL), const), vspec] + r_in,
        out_specs=[pl.BlockSpec((1, tm, D_MODEL), row)] + r_out,
        out_shape=[jax.ShapeDtypeStruct((BATCH, SEQ, D_MODEL), F32)] + r_shape,
        scratch_shapes=[pltpu.VMEM((tm, SGU_WIDTH), BF16), pltpu.VMEM((tm, SGU_WIDTH), F32)],
        compiler_params=_cparams(("arbitrary", "arbitrary"), vmem_mib=56),
        name="gmlp_layer",
    )(x, g1.reshape(1, D_MODEL), shift1.reshape(BATCH, 1, D_MODEL), scale1.reshape(BATCH, 1, D_MODEL),
      w_in.astype(BF16), norm_g.reshape(1, SGU_WIDTH), w_s, b_s.T, w_out.astype(BF16),
      gate.reshape(BATCH, 1, D_MODEL), *route_args)


MOE_TM = 256
MOE_CHUNK = 512
MOE_SORTED = N_TOK + N_EXPERT_GROUPS * MOE_TM
MOE_TILES = MOE_SORTED // MOE_TM
MOE_CHUNKS = N_TOK // MOE_CHUNK
MOE_PAIRS = MOE_TILES + N_EXPERT_GROUPS * MOE_CHUNKS
FLAG_ACTIVE, FLAG_FIRST, FLAG_LAST, FLAG_ZERO = 1, 2, 4, 8
EXP_WIN = 4
CMB_WIN = 8
MOE_TSTEPS = MOE_TILES + MOE_PAIRS // EXP_WIN
MOE_CSTEPS = MOE_CHUNKS + MOE_PAIRS // CMB_WIN


def _plan_kernel(gi_ref, rank_ref, before_ref):
    gi = gi_ref[...]
    r = lax.broadcasted_iota(jnp.int32, (POS_SIDE, POS_SIDE), 0)
    c = lax.broadcasted_iota(jnp.int32, (POS_SIDE, POS_SIDE), 1)
    upper = jnp.where(r <= c, 1.0, 0.0)
    lower_strict = jnp.where(c < r, 1.0, 0.0)
    rank = jnp.zeros((POS_SIDE, POS_SIDE), F32)
    for g in range(N_EXPERT_GROUPS):
        member = jnp.where(gi == g, 1.0, 0.0)
        in_row = jnp.dot(member, upper, precision=HI, preferred_element_type=F32)
        row_total = jnp.broadcast_to(in_row[:, POS_SIDE - 1:POS_SIDE], (POS_SIDE, POS_SIDE))
        before = jnp.dot(lower_strict, row_total, precision=HI, preferred_element_type=F32)
        before_ref[g] = before
        rank += member * (before + in_row - 1.0)
    rank_ref[...] = rank


def moe_plan(route):
    tm = MOE_TM
    i32 = jnp.int32
    gi_f = route[:, 0, :].reshape(POS_SIDE, POS_SIDE)
    rank, before = pl.pallas_call(
        _plan_kernel,
        out_shape=[jax.ShapeDtypeStruct((POS_SIDE, POS_SIDE), F32),
                   jax.ShapeDtypeStruct((N_EXPERT_GROUPS, POS_SIDE, POS_SIDE), F32)],
        name="moe_plan",
    )(gi_f)
    gi = gi_f.reshape(N_TOK).astype(i32)
    groups = jnp.arange(N_EXPERT_GROUPS, dtype=i32)
    member = gi[None, :] == groups[:, None]
    tot = jnp.sum(member, axis=1).astype(i32)
    padded = (tot + tm - 1) // tm * tm
    gend = jnp.cumsum(padded).astype(i32)
    gstart = gend - padded
    pos = jnp.sum(jnp.where(member, gstart[:, None], 0), axis=0).astype(i32) + rank.reshape(N_TOK).astype(i32)
    rows_per_chunk = MOE_CHUNK // POS_SIDE
    cnt_end = jnp.concatenate([before[:, rows_per_chunk::rows_per_chunk, 0].astype(i32), tot[:, None]], axis=1)
    t = jnp.arange(MOE_TILES, dtype=i32)
    n_used = gend[-1] // tm
    tile_g = jnp.minimum(jnp.sum(gend[None, :] <= (t * tm)[:, None], axis=1), N_EXPERT_GROUPS - 1).astype(i32)
    k0 = t * tm - gstart[tile_g]
    k1 = jnp.minimum(k0 + tm, tot[tile_g]) - 1
    ce = cnt_end[tile_g]
    c_lo = jnp.sum(ce <= k0[:, None], axis=1).astype(i32)
    c_hi = jnp.sum(ce <= k1[:, None], axis=1).astype(i32)
    npairs = jnp.where(t < n_used, c_hi - c_lo + 1, 0)
    pend = jnp.cumsum(npairs).astype(i32)
    pstart = pend - npairs
    total = pend[-1]
    l = jnp.arange(MOE_PAIRS, dtype=i32)
    real = l < total
    lt = jnp.minimum(l, total - 1)

    def windows(count, win, n_steps):
        per_item = (count + win - 1) // win
        end = jnp.cumsum(per_item).astype(i32)
        start = end - per_item
        s = jnp.arange(n_steps, dtype=i32)
        real_s = s < end[-1]
        sc = jnp.minimum(s, end[-1] - 1)
        item = jnp.sum(end[None, :] <= sc[:, None], axis=1).astype(i32)
        j = sc - start[item]
        flags_s = jnp.where(real_s, FLAG_ACTIVE + jnp.where(j == 0, FLAG_FIRST, 0)
                            + jnp.where(j == per_item[item] - 1, FLAG_LAST, 0), 0).astype(i32)
        return item, j, flags_s, real_s, s - end[-1]

    tile_s, j, flags, real_s, spare = windows(npairs, EXP_WIN, MOE_TSTEPS)
    c0 = c_lo[tile_s] + EXP_WIN * j
    n_valid = jnp.minimum(EXP_WIN, c_hi[tile_s] - c0 + 1).astype(i32)
    spare_tile = jnp.minimum(n_used + spare, MOE_TILES - 1)
    flags = jnp.where(real_s, flags, jnp.where(spare_tile >= n_used, FLAG_ZERO, 0)).astype(i32)
    tile_sched = jnp.where(real_s, tile_s, spare_tile).astype(i32)
    by_tile = (tile_sched, c0.astype(i32), n_valid, flags, tile_g[tile_sched])
    cc = jnp.arange(MOE_CHUNKS, dtype=i32)
    is_pair = (cc[:, None] >= c_lo[None, :]) & (cc[:, None] <= c_hi[None, :]) & (t[None, :] < n_used)
    seen = jnp.cumsum(is_pair.reshape(-1).astype(i32))
    flat = jnp.sum(seen[None, :] <= lt[:, None], axis=1).astype(i32)
    pair_tile = flat % MOE_TILES
    per_chunk = jnp.sum(is_pair, axis=1).astype(i32)
    first_pair = jnp.cumsum(per_chunk).astype(i32) - per_chunk
    chunk_s, j, flags_c, _, _ = windows(per_chunk, CMB_WIN, MOE_CSTEPS)
    base = first_pair[chunk_s] + CMB_WIN * j
    n_valid_c = jnp.minimum(CMB_WIN, per_chunk[chunk_s] - CMB_WIN * j).astype(i32)
    tiles_c = tuple(pair_tile[jnp.minimum(base + w, total - 1)] for w in range(CMB_WIN))
    by_chunk = (chunk_s, n_valid_c, flags_c) + tiles_c
    return pos.reshape(MOE_CHUNKS, 1, MOE_CHUNK), by_tile, by_chunk


def _one_hot_rows(pos_row, tile):
    rows = tile * MOE_TM + lax.broadcasted_iota(jnp.int32, (MOE_TM, MOE_CHUNK), 0)
    return jnp.where(pos_row == rows, 1.0, 0.0).astype(BF16)


def _moe_kernel(tile_ref, c0_ref, nv_ref, flag_ref, grp_ref, *refs):
    pos_refs, hx_refs = refs[0:EXP_WIN], refs[EXP_WIN:2 * EXP_WIN]
    wg_ref, wu_ref, wd_ref, y_ref, acc_ref = refs[2 * EXP_WIN:]
    l = pl.program_id(0)
    flags = flag_ref[l]

    @pl.when((flags & FLAG_FIRST) != 0)
    def _():
        acc_ref[...] = jnp.zeros_like(acc_ref)

    for w in range(EXP_WIN):
        @pl.when(((flags & FLAG_ACTIVE) != 0) & (w < nv_ref[l]))
        def _():
            onehot = _one_hot_rows(pos_refs[w][0], tile_ref[l])
            acc_ref[...] += jnp.dot(onehot, hx_refs[w][0], preferred_element_type=F32)

    @pl.when((flags & FLAG_LAST) != 0)
    def _():
        x = acc_ref[:, 0:D_MODEL].astype(BF16)
        w = functools.reduce(jnp.add, [acc_ref[:, D_MODEL + n * LANES:D_MODEL + (n + 1) * LANES] for n in range(3)])
        y = jnp.zeros((MOE_TM, D_MODEL), F32)
        for r in range(EXPERTS_PER_GROUP):
            gate = jnp.dot(x, wg_ref[0, 0, r], preferred_element_type=F32)
            up = jnp.dot(x, wu_ref[0, 0, r], preferred_element_type=F32)
            hid = (_silu(gate) * up * w[:, r:r + 1]).astype(BF16)
            y += jnp.dot(hid, wd_ref[0, 0, r], preferred_element_type=F32)
        y_ref[...] = y

    @pl.when((flags & FLAG_ZERO) != 0)
    def _():
        y_ref[...] = jnp.zeros_like(y_ref)


def moe_experts(hx, pos, by_tile, w_gate, w_up, w_down, layer):
    grouped = lambda w: w.reshape(DEPTH, N_EXPERT_GROUPS, EXPERTS_PER_GROUP, *w.shape[2:])
    wspec = lambda k, n: pl.BlockSpec((1, 1, EXPERTS_PER_GROUP, k, n), lambda l, t, c, n_, f, g: (layer, g[l], 0, 0, 0))
    chunk = lambda w: (lambda l, t, c, n_, f, g: (jnp.minimum(c[l] + w, MOE_CHUNKS - 1), 0, 0))
    hx_chunks = hx.reshape(MOE_CHUNKS, MOE_CHUNK, HX_W)
    return pl.pallas_call(
        _moe_kernel,
        grid_spec=pltpu.PrefetchScalarGridSpec(
            num_scalar_prefetch=5,
            grid=(MOE_TSTEPS,),
            in_specs=[pl.BlockSpec((1, 1, MOE_CHUNK), chunk(w)) for w in range(EXP_WIN)]
            + [pl.BlockSpec((1, MOE_CHUNK, HX_W), chunk(w)) for w in range(EXP_WIN)]
            + [wspec(D_MODEL, EXPERT_HIDDEN), wspec(D_MODEL, EXPERT_HIDDEN), wspec(EXPERT_HIDDEN, D_MODEL)],
            out_specs=pl.BlockSpec((MOE_TM, D_MODEL), lambda l, t, c, n_, f, g: (t[l], 0)),
            scratch_shapes=[pltpu.VMEM((MOE_TM, HX_W), F32)],
        ),
        out_shape=jax.ShapeDtypeStruct((MOE_SORTED, D_MODEL), F32),
        compiler_params=_cparams(("arbitrary",), vmem_mib=56),
        name="moe_experts",
    )(*by_tile, *([pos] * EXP_WIN), *([hx_chunks] * EXP_WIN), grouped(w_gate), grouped(w_up), grouped(w_down))


def _moe_combine_kernel(chunk_ref, nv_ref, flag_ref, *refs):
    tile_refs = refs[0:CMB_WIN]
    pos_ref = refs[CMB_WIN]
    y_refs = refs[CMB_WIN + 1:2 * CMB_WIN + 1]
    x_ref, gate_ref, o_ref, acc_ref = refs[2 * CMB_WIN + 1:]
    l = pl.program_id(0)
    flags = flag_ref[l]

    @pl.when((flags & FLAG_FIRST) != 0)
    def _():
        acc_ref[...] = jnp.zeros_like(acc_ref)

    for w in range(CMB_WIN):
        @pl.when(((flags & FLAG_ACTIVE) != 0) & (w < nv_ref[l]))
        def _():
            onehot = _one_hot_rows(pos_ref[0], tile_refs[w][l])
            y = y_refs[w][...]
            y_hi = y.astype(BF16)
            y_lo = (y - y_hi.astype(F32)).astype(BF16)
            acc_ref[...] += (lax.dot_general(onehot, y_hi, TN_DIMS, preferred_element_type=F32)
                             + lax.dot_general(onehot, y_lo, TN_DIMS, preferred_element_type=F32))

    @pl.when((flags & FLAG_LAST) != 0)
    def _():
        o_ref[0] = x_ref[0] + gate_ref[0] * acc_ref[...]


def moe_combine(x, y_sorted, pos, by_chunk, gate):
    per_b = SEQ // MOE_CHUNK
    tok = lambda l, c, *_: (c[l] // per_b, c[l] % per_b, 0)
    tile = lambda w: (lambda l, c, n_, f, *tiles: (tiles[w][l], 0))
    return pl.pallas_call(
        _moe_combine_kernel,
        grid_spec=pltpu.PrefetchScalarGridSpec(
            num_scalar_prefetch=3 + CMB_WIN,
            grid=(MOE_CSTEPS,),
            in_specs=[pl.BlockSpec((1, 1, MOE_CHUNK), lambda l, c, *_: (c[l], 0, 0))]
            + [pl.BlockSpec((MOE_TM, D_MODEL), tile(w)) for w in range(CMB_WIN)]
            + [pl.BlockSpec((1, MOE_CHUNK, D_MODEL), tok),
               pl.BlockSpec((1, 1, D_MODEL), lambda l, c, *_: (c[l] // per_b, 0, 0))],
            out_specs=pl.BlockSpec((1, MOE_CHUNK, D_MODEL), tok),
            scratch_shapes=[pltpu.VMEM((MOE_CHUNK, D_MODEL), F32)],
        ),
        out_shape=jax.ShapeDtypeStruct((BATCH, SEQ, D_MODEL), F32),
        compiler_params=_cparams(("arbitrary",)),
        name="moe_combine",
    )(*by_chunk, pos, *([y_sorted] * CMB_WIN), x, gate.reshape(BATCH, 1, D_MODEL))


def moe_layer(x, hx, route, gate, w_gate, w_up, w_down, layer):
    pos, by_tile, by_chunk = moe_plan(route)
    y_sorted = moe_experts(hx, pos, by_tile, w_gate, w_up, w_down, layer)
    return moe_combine(x, y_sorted, pos, by_chunk, gate)


def kernel(x, c, positions, w_ada, b_ada, norm_g, w_in_ab, w_out_ab, gla_w_gate2, gla_b_gate, gla_norm_g, nsa_q_gain, nsa_k_gain, nsa_cmp_pe, nsa_cmp_w1, nsa_cmp_w2, w_in_c, sgu_norm_g, sgu_w_s, sgu_b_s, w_out_c, w_router, router_bias, w_gate, w_up, w_down):
    mod = ada_modulation(c, w_ada, b_ada)
    qk, gv, gr, nq, nkv, misc = inproj0(x, norm_g[0, 0], mod[0, :, 0], mod[0, :, 1], _arrange_w_in(w_in_ab[0]))
    o_a = gla_mixer(qk, gv, gr, misc, gla_w_gate2[0], gla_b_gate[0], gla_norm_g[0])
    o_b = nsa_mixer(nq, nkv, misc, positions, nsa_q_gain[0], nsa_k_gain[0], nsa_cmp_pe[0], nsa_cmp_w1[0], nsa_cmp_w2[0])
    wg, wu, wd = w_gate.astype(BF16), w_up.astype(BF16), w_down.astype(BF16)
    route_args = lambda l: _route_args(norm_g[l, 1], mod[l, :, 3], mod[l, :, 4], w_router, router_bias)
    x1, h, route = outproj0(o_a, o_b, w_out_ab[0], x, mod[0, :, 2], route_args(0))
    x2 = moe_layer(x1, h, route, mod[0, :, 5], wg, wu, wd, 0)
    x3, h, route = gmlp_layer(x2, norm_g[1, 0], mod[1, :, 0], mod[1, :, 1], w_in_c[0], sgu_norm_g[0], sgu_w_s[0],
                              sgu_b_s[0], w_out_c[0], mod[1, :, 2], route_args(1))
    return moe_layer(x3, h, route, mod[1, :, 5], wg, wu, wd, 1)
```

```python
import functools

import numpy as np
import jax
import jax.numpy as jnp
from jax import lax
from jax.experimental import pallas as pl
from jax.experimental.pallas import tpu as pltpu

D_MODEL = 1024
BATCH = 2
SEQ = 8192
DEPTH = 2
N_TOK = BATCH * SEQ

GLA_HEADS = 4
GLA_DK = 64
GLA_DV = 128
GLA_GATE_RANK = 16
GLA_TAU = 16.0
GLA_CHUNK = 64
NSA_HEADS = 8
NSA_KV_GROUPS = 2
NSA_HPG = NSA_HEADS // NSA_KV_GROUPS
NSA_DH = 64
CMP_LEN = 32
CMP_STRIDE = 16
CMP_HIDDEN = 256
SEL_BLOCK = 64
SEL_TOPK = 16
WINDOW = 512
ROPE_THETA = 500000.0
ROT_DIM = NSA_DH // 4
ROT_HALF = ROT_DIM // 2
SGU_CHUNK = 128
SGU_GROUPS = 8
SGU_WIDTH = 2048
SGU_GROUP_DIM = SGU_WIDTH // SGU_GROUPS
N_EXPERTS = 16
N_EXPERT_GROUPS = 4
EXPERTS_PER_GROUP = N_EXPERTS // N_EXPERT_GROUPS
MOE_TOPK = 2
EXPERT_HIDDEN = 512

GLA_QK_W = GLA_HEADS * GLA_DK
GLA_V_W = GLA_HEADS * GLA_DV
NSA_Q_W = NSA_HEADS * NSA_DH
NSA_KV_W = NSA_KV_GROUPS * NSA_DH
N_CMP = (SEQ - CMP_LEN) // CMP_STRIDE + 1
N_CMP_PAD = SEQ // CMP_STRIDE
N_SEL = SEQ // SEL_BLOCK

NORM_EPS = 1e-6
NEG_INF = -1e30
FORCE_BONUS = 1e4

LANES = 128
MIB = 1024 * 1024

F32 = jnp.float32
BF16 = jnp.bfloat16
HI = lax.Precision.HIGHEST
NT_DIMS = (((1,), (1,)), ((), ()))
TN_DIMS = (((0,), (0,)), ((), ()))


def _cparams(sem, vmem_mib=48):
    return pltpu.CompilerParams(dimension_semantics=sem, vmem_limit_bytes=vmem_mib * MIB)


def _rms_mod(x, g, shift, scale):
    y = x * lax.rsqrt(jnp.mean(x * x, axis=-1, keepdims=True) + NORM_EPS) * g
    return y * (1 + scale) + shift


def _silu(x):
    return x * jax.nn.sigmoid(x)


def _log_sigmoid(z):
    return jnp.minimum(z, 0.0) - jnp.log1p(jnp.exp(-jnp.abs(z)))


ADA_TN = 1536
ADA_ROWS = 8


def _ada_kernel(c_ref, w_ref, b_ref, o_ref):
    cond = _silu(c_ref[...])
    o_ref[0] = jnp.dot(cond, w_ref[0], precision=HI, preferred_element_type=F32) + b_ref[0]


def ada_modulation(c, w_ada, b_ada):
    c8 = jnp.zeros((ADA_ROWS, D_MODEL), F32).at[:BATCH].set(c)
    width = 6 * D_MODEL
    out = pl.pallas_call(
        _ada_kernel,
        grid=(DEPTH, width // ADA_TN),
        in_specs=[
            pl.BlockSpec((ADA_ROWS, D_MODEL), lambda l, j: (0, 0)),
            pl.BlockSpec((1, D_MODEL, ADA_TN), lambda l, j: (l, 0, j)),
            pl.BlockSpec((1, 1, ADA_TN), lambda l, j: (l, 0, j)),
        ],
        out_specs=pl.BlockSpec((1, ADA_ROWS, ADA_TN), lambda l, j: (l, 0, j)),
        out_shape=jax.ShapeDtypeStruct((DEPTH, ADA_ROWS, width), F32),
        compiler_params=_cparams(("arbitrary", "arbitrary")),
        name="ada_modulation",
    )(c8, w_ada, b_ada.reshape(DEPTH, 1, width))
    return out[:, :BATCH].reshape(DEPTH, BATCH, 6, D_MODEL)


INPROJ_TM = 512
INPROJ_WIDTHS = (2 * GLA_QK_W, GLA_V_W, GLA_V_W, NSA_Q_W, 6 * NSA_KV_W, LANES)


def _arrange_w_in(w_in):
    o = np.cumsum((0, GLA_QK_W, GLA_QK_W, GLA_V_W, GLA_GATE_RANK, GLA_V_W, NSA_Q_W, 6 * NSA_KV_W, NSA_HEADS * 3))
    gq_gk = w_in[:, o[0]:o[2]]
    gv = w_in[:, o[2]:o[3]]
    glr = w_in[:, o[3]:o[4]]
    gr = w_in[:, o[4]:o[5]]
    nq = w_in[:, o[5]:o[6]]
    nkv = w_in[:, o[6]:o[7]]
    ng = w_in[:, o[7]:o[8]]
    pad = jnp.zeros((D_MODEL, LANES - GLA_GATE_RANK - NSA_HEADS * 3), w_in.dtype)
    return jnp.concatenate([gq_gk, gv, gr, nq, nkv, glr, ng, pad], axis=1).astype(BF16)


def _inproj0_kernel(x_ref, g_ref, sh_ref, sc_ref, w_ref, *o_refs):
    h = _rms_mod(x_ref[0], g_ref[...], sh_ref[0], sc_ref[0]).astype(BF16)
    off = 0
    for o_ref, wd in zip(o_refs, INPROJ_WIDTHS):
        o_ref[0] = jnp.dot(h, w_ref[:, off:off + wd], preferred_element_type=F32)
        off += wd


def inproj0(x, g, shift, scale, w_arranged):
    tm = INPROJ_TM
    wtot = sum(INPROJ_WIDTHS)
    row = lambda b, i: (b, i, 0)
    vec = lambda b, i: (b, 0, 0)
    return pl.pallas_call(
        _inproj0_kernel,
        grid=(BATCH, SEQ // tm),
        in_specs=[
            pl.BlockSpec((1, tm, D_MODEL), row),
            pl.BlockSpec((1, D_MODEL), lambda b, i: (0, 0)),
            pl.BlockSpec((1, 1, D_MODEL), vec),
            pl.BlockSpec((1, 1, D_MODEL), vec),
            pl.BlockSpec((D_MODEL, wtot), lambda b, i: (0, 0)),
        ],
        out_specs=[pl.BlockSpec((1, tm, wd), row) for wd in INPROJ_WIDTHS],
        out_shape=[jax.ShapeDtypeStruct((BATCH, SEQ, wd), F32) for wd in INPROJ_WIDTHS],
        compiler_params=_cparams(("arbitrary", "arbitrary")),
        name="inproj0",
    )(x, g.reshape(1, D_MODEL), shift.reshape(BATCH, 1, D_MODEL), scale.reshape(BATCH, 1, D_MODEL), w_arranged)


GLA_TG = 512


def _gla_chunk_sums():
    i = np.arange(GLA_TG)[:, None]
    j = np.arange(GLA_TG)[None, :]
    same = (i // GLA_CHUNK) == (j // GLA_CHUNK)
    m3 = np.concatenate([same & (j <= i), same & (j % GLA_CHUNK <= GLA_CHUNK // 2), same], axis=0).astype(np.float32)
    return jnp.asarray(np.concatenate([m3, m3], axis=1), BF16)


def _gla_kernel(qk_ref, v_ref, r_ref, misc_ref, w2_ref, bg_ref, og_ref, sums_ref, o_ref, st_ref):
    C, tg = GLA_CHUNK, GLA_TG

    @pl.when(pl.program_id(1) == 0)
    def _():
        st_ref[...] = jnp.zeros_like(st_ref)

    z = jnp.dot(misc_ref[0], w2_ref[...], precision=HI, preferred_element_type=F32) + bg_ref[...]
    la = _log_sigmoid(z) / GLA_TAU
    la_hi = la.astype(BF16)
    la_lo = (la - la_hi.astype(F32)).astype(BF16)
    sums = jnp.dot(sums_ref[...], jnp.concatenate([la_hi, la_lo], axis=0), preferred_element_type=F32)
    bc, b_mid, b_last = sums[0:tg], sums[tg:2 * tg], sums[2 * tg:3 * tg]

    q = qk_ref[0, :, 0:GLA_QK_W] * (GLA_DK ** -0.5)
    k = qk_ref[0, :, GLA_QK_W:2 * GLA_QK_W]
    qd = q * jnp.exp(bc - b_mid)
    kd = (k * jnp.exp(b_mid - bc)).astype(BF16)
    kl = k * jnp.exp(b_last - bc)
    qb = q * jnp.exp(bc)
    dec = jnp.exp(b_last)
    lane = lax.broadcasted_iota(jnp.int32, (1, GLA_QK_W), 1)
    heads = [(lane >= h * GLA_DK) & (lane < (h + 1) * GLA_DK) for h in range(GLA_HEADS)]
    qd_h = [jnp.where(m, qd, 0.0).astype(BF16) for m in heads]
    qb_h = [jnp.where(m, qb, 0.0).astype(BF16) for m in heads]
    kl_h = [jnp.where(m, kl, 0.0).astype(BF16) for m in heads]
    stack = lambda per_head, rows: jnp.concatenate([t[rows] for t in per_head], axis=0)
    stacked_row = lax.broadcasted_iota(jnp.int32, (GLA_HEADS * C, C), 0)
    causal = (stacked_row & (C - 1)) >= lax.broadcasted_iota(jnp.int32, (GLA_HEADS * C, C), 1)
    og = og_ref[...]

    st = st_ref[...]
    for c in range(tg // C):
        rows = slice(c * C, (c + 1) * C)
        v = v_ref[0, rows, :].astype(BF16)
        s = lax.dot_general(stack(qd_h, rows), kd[rows], NT_DIMS, preferred_element_type=F32)
        s = jnp.where(causal, s, 0.0).astype(BF16)
        o_intra = jnp.dot(s, v, preferred_element_type=F32)
        o_inter = lax.dot_general(stack(qb_h, rows), st.astype(BF16), NT_DIMS, preferred_element_type=F32)
        v_stack = jnp.concatenate([v[:, h * GLA_DV:(h + 1) * GLA_DV] for h in range(GLA_HEADS)], axis=0)
        st = st * dec[c * C:c * C + 1] + lax.dot_general(v_stack, stack(kl_h, rows), TN_DIMS, preferred_element_type=F32)
        for h in range(GLA_HEADS):
            hrows = slice(h * C, (h + 1) * C)
            vcols = slice(h * GLA_DV, (h + 1) * GLA_DV)
            o = o_intra[hrows, vcols] + o_inter[hrows]
            on = o * lax.rsqrt(jnp.mean(o * o, axis=-1, keepdims=True) + NORM_EPS) * og
            o_ref[0, rows, vcols] = on * _silu(r_ref[0, rows, vcols])
    st_ref[...] = st


def gla_mixer(qk, v, r, misc, w_gate2, b_gate, out_g):
    tg = GLA_TG
    w2 = jnp.zeros((LANES, GLA_QK_W), F32).at[:GLA_GATE_RANK].set(w_gate2)
    row = lambda b, i: (b, i, 0)
    const = lambda b, i: (0, 0)
    return pl.pallas_call(
        _gla_kernel,
        grid=(BATCH, SEQ // tg),
        in_specs=[
            pl.BlockSpec((1, tg, 2 * GLA_QK_W), row),
            pl.BlockSpec((1, tg, GLA_V_W), row),
            pl.BlockSpec((1, tg, GLA_V_W), row),
            pl.BlockSpec((1, tg, LANES), row),
            pl.BlockSpec((LANES, GLA_QK_W), const),
            pl.BlockSpec((1, GLA_QK_W), const),
            pl.BlockSpec((1, GLA_DV), const),
            pl.BlockSpec((3 * tg, 2 * tg), const),
        ],
        out_specs=pl.BlockSpec((1, tg, GLA_V_W), row),
        out_shape=jax.ShapeDtypeStruct((BATCH, SEQ, GLA_V_W), F32),
        scratch_shapes=[pltpu.VMEM((GLA_DV, GLA_QK_W), F32)],
        compiler_params=_cparams(("arbitrary", "arbitrary")),
        name="gla_mixer",
    )(qk, v, r, misc, w2, b_gate.reshape(1, GLA_QK_W), out_g.reshape(1, GLA_DV), _gla_chunk_sums())


POS_SIDE = 128


def _rope_table_kernel(freq_ref, pos_ref, cos_ref, sin_ref):
    pos = pos_ref[...].astype(F32)
    for f in range(ROT_HALF):
        ang = pos * freq_ref[f]
        cos_ref[f] = jnp.cos(ang)
        sin_ref[f] = jnp.sin(ang)


def rope_tables(positions):
    inv_freq = jnp.float32(ROPE_THETA) ** (-jnp.arange(ROT_HALF, dtype=F32) / ROT_HALF)
    shp = jax.ShapeDtypeStruct((ROT_HALF, POS_SIDE, POS_SIDE), F32)
    cos, sin = pl.pallas_call(
        _rope_table_kernel,
        in_specs=[pl.BlockSpec(memory_space=pltpu.SMEM), pl.BlockSpec(memory_space=pltpu.VMEM)],
        out_specs=[pl.BlockSpec(memory_space=pltpu.VMEM)] * 2,
        out_shape=[shp, shp],
        name="rope_tables",
    )(inv_freq, positions.reshape(POS_SIDE, POS_SIDE))
    return jnp.concatenate([cos, sin], axis=0).reshape(ROT_DIM, N_TOK).T.reshape(BATCH, SEQ, ROT_DIM)


def _rope_placement():
    place = np.zeros((ROT_DIM, 3 * LANES), np.float32)
    const = np.zeros((1, 3 * LANES), np.float32)
    for lane in range(LANES):
        i = lane % NSA_DH
        if i < ROT_HALF:
            place[i, lane] = 1.0
            place[ROT_HALF + i, LANES + lane] = -1.0
        elif i < ROT_DIM:
            place[i - ROT_HALF, lane] = 1.0
            place[i, 2 * LANES + lane] = 1.0
        else:
            const[0, lane] = 1.0
    return jnp.asarray(place), jnp.asarray(const)


def _lane_tables(cs, place_ref, const_ref):
    tab = jnp.dot(cs, place_ref[...], precision=HI, preferred_element_type=F32) + const_ref[...]
    return tab[:, 0:LANES], tab[:, LANES:2 * LANES], tab[:, 2 * LANES:3 * LANES]


def _block_diag_ones2(width):
    h = np.arange(width) // NSA_DH
    bd = (h[:, None] == h[None, :]).astype(np.float32)
    return jnp.asarray(np.concatenate([bd, bd], axis=0), BF16)


def _head_norm_rope(x, gain, bd2, c, sm, sp):
    width = x.shape[-1]
    reps = width // LANES
    sq = x * x
    sq_hi = sq.astype(BF16)
    sq_lo = (sq - sq_hi.astype(F32)).astype(BF16)
    ss = jnp.dot(jnp.concatenate([sq_hi, sq_lo], axis=1), bd2, preferred_element_type=F32)
    y = x * lax.rsqrt(ss * (1.0 / NSA_DH) + NORM_EPS) * gain
    tile = lambda t: jnp.concatenate([t] * reps, axis=1) if reps > 1 else t
    return (y * tile(c) + pltpu.roll(y, width - ROT_HALF, 1) * tile(sm) + pltpu.roll(y, ROT_HALF, 1) * tile(sp))


PREP_TM = 512


def _prep_kernel(q_ref, ks_ref, kw_ref, cs_ref, place_ref, const_ref, gq_ref, gk_ref, bdq_ref, bdk_ref,
                 qo_ref, kso_ref, kwo_ref):
    c, sm, sp = _lane_tables(cs_ref[0], place_ref, const_ref)
    bdk = bdk_ref[...]
    q = _head_norm_rope(q_ref[0], gq_ref[...], bdq_ref[...], c, sm, sp) * (NSA_DH ** -0.5)
    qo_ref[0] = q.T.reshape(NSA_HEADS, NSA_DH, PREP_TM)
    kso_ref[0] = _head_norm_rope(ks_ref[0], gk_ref[0:1, :], bdk, c, sm, sp)
    kwo_ref[0] = _head_norm_rope(kw_ref[0], gk_ref[1:2, :], bdk, c, sm, sp)


def nsa_prep(nq, nkv, cs, q_gain, k_gain):
    tm = PREP_TM
    row = lambda b, i: (b, i, 0)
    const = lambda b, i: (0, 0)
    gq = jnp.tile(q_gain, NSA_HEADS).reshape(1, NSA_Q_W)
    gk = jnp.stack([jnp.tile(k_gain[1], NSA_KV_GROUPS), jnp.tile(k_gain[2], NSA_KV_GROUPS)])
    return pl.pallas_call(
        _prep_kernel,
        grid=(BATCH, SEQ // tm),
        in_specs=[
            pl.BlockSpec((1, tm, NSA_Q_W), row),
            pl.BlockSpec((1, tm, NSA_KV_W), lambda b, i: (b, i, 2)),
            pl.BlockSpec((1, tm, NSA_KV_W), lambda b, i: (b, i, 4)),
            pl.BlockSpec((1, tm, ROT_DIM), row),
            pl.BlockSpec((ROT_DIM, 3 * LANES), const),
            pl.BlockSpec((1, 3 * LANES), const),
            pl.BlockSpec((1, NSA_Q_W), const),
            pl.BlockSpec((2, NSA_KV_W), const),
            pl.BlockSpec((2 * NSA_Q_W, NSA_Q_W), const),
            pl.BlockSpec((2 * NSA_KV_W, NSA_KV_W), const),
        ],
        out_specs=[pl.BlockSpec((1, NSA_HEADS, NSA_DH, tm), lambda b, i: (b, 0, 0, i)),
                   pl.BlockSpec((1, tm, NSA_KV_W), row), pl.BlockSpec((1, tm, NSA_KV_W), row)],
        out_shape=[jax.ShapeDtypeStruct((BATCH, NSA_HEADS, NSA_DH, SEQ), F32),
                   jax.ShapeDtypeStruct((BATCH, SEQ, NSA_KV_W), F32), jax.ShapeDtypeStruct((BATCH, SEQ, NSA_KV_W), F32)],
        compiler_params=_cparams(("arbitrary", "arbitrary")),
        name="nsa_prep",
    )(nq, nkv, nkv, cs, *_rope_placement(), gq, gk, _block_diag_ones2(NSA_Q_W), _block_diag_ones2(NSA_KV_W))


SEG_W = CMP_STRIDE * NSA_DH


def _cmp_kernel(xk_ref, xv_ref, pe_ref, w1_ref, w2_ref, gain_ref, cs_ref, place_ref, const_ref, bd_ref, ko_ref, vo_ref):
    def compress(x_ref, kv):
        out = jnp.zeros((N_CMP_PAD, LANES), F32)
        for g in range(NSA_KV_GROUPS):
            x = x_ref[0, g]
            ha = jnp.dot(x + pe_ref[kv, 0], w1_ref[kv, 0:SEG_W, :], precision=HI, preferred_element_type=F32)
            hb = jnp.dot(x + pe_ref[kv, 1], w1_ref[kv, SEG_W:2 * SEG_W, :], precision=HI, preferred_element_type=F32)
            hid = ha + pltpu.roll(hb, N_CMP_PAD - 1, 0)
            out += jnp.dot(jax.nn.gelu(hid), w2_ref[kv, g], precision=HI, preferred_element_type=F32)
        return out

    c, sm, sp = _lane_tables(cs_ref[0], place_ref, const_ref)
    ko_ref[0] = _head_norm_rope(compress(xk_ref, 0), gain_ref[...], bd_ref[...], c, sm, sp)
    vo_ref[0] = compress(xv_ref, 1)


def nsa_compress(xk, xv, cmp_pe, cmp_w1, cmp_w2, k_gain0, cs_last):
    pe = cmp_pe.reshape(2, 2, 1, SEG_W)
    w2 = jnp.zeros((2, NSA_KV_GROUPS, CMP_HIDDEN, LANES), F32)
    for g in range(NSA_KV_GROUPS):
        w2 = w2.at[:, g, :, g * NSA_DH:(g + 1) * NSA_DH].set(cmp_w2)
    seg = pl.BlockSpec((1, NSA_KV_GROUPS, N_CMP_PAD, SEG_W), lambda b: (b, 0, 0, 0))
    tab = pl.BlockSpec((1, N_CMP_PAD, LANES), lambda b: (b, 0, 0))
    full = lambda shape: pl.BlockSpec(shape, lambda b: (0,) * len(shape))
    return pl.pallas_call(
        _cmp_kernel,
        grid=(BATCH,),
        in_specs=[seg, seg, full((2, 2, 1, SEG_W)), full((2, 2 * SEG_W, CMP_HIDDEN)),
                  full((2, NSA_KV_GROUPS, CMP_HIDDEN, LANES)), full((1, LANES)),
                  pl.BlockSpec((1, N_CMP_PAD, ROT_DIM), lambda b: (b, 0, 0)), full((ROT_DIM, 3 * LANES)),
                  full((1, 3 * LANES)), full((2 * LANES, LANES))],
        out_specs=[tab, tab],
        out_shape=[jax.ShapeDtypeStruct((BATCH, N_CMP_PAD, LANES), F32)] * 2,
        compiler_params=_cparams(("arbitrary",)),
        name="nsa_compress",
    )(xk, xv, pe, cmp_w1, w2, jnp.tile(k_gain0, NSA_KV_GROUPS).reshape(1, LANES), cs_last, *_rope_placement(),
      _block_diag_ones2(LANES))


CA_TQ = 256
SUBLANES = 8


CA_COLS = NSA_HPG * CA_TQ
CMP_PER_SEL = SEL_BLOCK // CMP_STRIDE


def split3_keys(k):
    hi = k.astype(BF16)
    lo = (k - hi.astype(F32)).astype(BF16)
    return jnp.concatenate([hi, lo, hi], axis=-1)


def _top_k_rows(score, k):
    rows, cols = score.shape
    row = lax.broadcasted_iota(jnp.int32, (rows, cols), 0).astype(F32)
    taken = jnp.zeros((rows, cols), F32)
    left = score
    for _ in range(k):
        top = jnp.max(left, axis=0, keepdims=True)
        first = jnp.min(jnp.where(left == top, row, float(rows)), axis=0, keepdims=True)
        hit = row == first
        taken = jnp.where(hit, 1.0, taken)
        left = jnp.where(hit, -jnp.inf, left)
    return taken


def _cattn_kernel(q_ref, kc_ref, vct_ref, gl_ref, o_ref, sel_ref, q3_ref, ps_ref):
    tq = CA_TQ
    q0 = pl.program_id(1) * tq
    lanes4 = lambda t: jnp.concatenate([t] * NSA_HPG, axis=1)
    cend = lax.broadcasted_iota(jnp.int32, (N_CMP_PAD, tq), 0) * CMP_STRIDE + (CMP_LEN - 1)
    tc = q0 + lax.broadcasted_iota(jnp.int32, (N_CMP_PAD, tq), 1)
    cmask = lanes4(cend <= tc)
    jj = lax.broadcasted_iota(jnp.int32, (N_SEL, tq), 0)
    tt = q0 + lax.broadcasted_iota(jnp.int32, (N_SEL, tq), 1)
    cur = jnp.right_shift(tt, 6)
    forced = (jj == 0) | (jj == cur) | (jj == cur - 1)
    valid = jj * SEL_BLOCK <= tt

    for g in range(NSA_KV_GROUPS):
        heads = range(g * NSA_HPG, (g + 1) * NSA_HPG)
        for n, h in enumerate(heads):
            q = q_ref[0, h]
            hi = q.astype(BF16)
            lo = (q - hi.astype(F32)).astype(BF16)
            for t, part in enumerate((hi, hi, lo)):
                q3_ref[g, t * NSA_DH:(t + 1) * NSA_DH, n * tq:(n + 1) * tq] = part
        s = jnp.dot(kc_ref[0, g], q3_ref[g], preferred_element_type=F32)
        s = jnp.where(cmask, s, NEG_INF)
        m = jnp.max(s, axis=0, keepdims=True)
        e = jnp.where(cmask, jnp.exp(s - m), 0.0)
        l = jnp.sum(e, axis=0, keepdims=True)
        p = e / jnp.where(l > 0.0, l, 1.0)
        gate = jnp.concatenate([jax.nn.sigmoid(gl_ref[0, h, 0:1, :]) for h in heads], axis=1)
        o = jnp.dot(vct_ref[0, g], p.astype(BF16), preferred_element_type=F32) * gate
        for n, h in enumerate(heads):
            o_ref[0, h] = o[:, n * tq:(n + 1) * tq]
        psum = functools.reduce(jnp.add, [p[:, n * tq:(n + 1) * tq] for n in range(NSA_HPG)])
        for n in range(tq // LANES):
            ps_ref[g, n] = psum[:, n * LANES:(n + 1) * LANES]

        every4th = lambda r: jnp.concatenate(
            [ps_ref[g, n, pl.ds(r, N_SEL, stride=CMP_PER_SEL), :] for n in range(tq // LANES)], axis=1)
        starts_in = [every4th(r) for r in range(CMP_PER_SEL)]
        from_prev = jnp.where(jj >= 1, pltpu.roll(starts_in[CMP_PER_SEL - 1], 1, 0), 0.0)
        imp = functools.reduce(jnp.add, starts_in) + from_prev
        score = jnp.where(valid, imp + jnp.where(forced, FORCE_BONUS, 0.0), NEG_INF)
        sel_ref[0, g] = jnp.where(valid, _top_k_rows(score, SEL_TOPK), 0.0)


def nsa_cmp_attn(q_t, kcmp, vcmp_t, gl_t):
    tq = CA_TQ
    G = NSA_KV_GROUPS
    return pl.pallas_call(
        _cattn_kernel,
        grid=(BATCH, SEQ // tq),
        in_specs=[
            pl.BlockSpec((1, NSA_HEADS, NSA_DH, tq), lambda b, i: (b, 0, 0, i)),
            pl.BlockSpec((1, G, N_CMP_PAD, 3 * NSA_DH), lambda b, i: (b, 0, 0, 0)),
            pl.BlockSpec((1, G, NSA_DH, N_CMP_PAD), lambda b, i: (b, 0, 0, 0)),
            pl.BlockSpec((1, NSA_HEADS, 3, tq), lambda b, i: (b, 0, 0, i)),
        ],
        out_specs=[pl.BlockSpec((1, NSA_HEADS, NSA_DH, tq), lambda b, i: (b, 0, 0, i)),
                   pl.BlockSpec((1, G, N_SEL, tq), lambda b, i: (b, 0, 0, i))],
        out_shape=[jax.ShapeDtypeStruct((BATCH, NSA_HEADS, NSA_DH, SEQ), F32),
                   jax.ShapeDtypeStruct((BATCH, G, N_SEL, SEQ), F32)],
        scratch_shapes=[pltpu.VMEM((G, 3 * NSA_DH, CA_COLS), BF16), pltpu.VMEM((G, tq // LANES, N_CMP_PAD, LANES), F32)],
        compiler_params=_cparams(("arbitrary", "arbitrary")),
        name="nsa_cmp_attn",
    )(q_t, split3_keys(kcmp), vcmp_t.astype(BF16), gl_t)


SA_TQ = 256
SA_TK = 1024
SA_PARTS = 2
SA_PART = SA_TK // SA_PARTS
M_INIT = -1e20


SA_COLS = NSA_HPG * SA_TQ
SA_BLOCKS = SA_TK // SEL_BLOCK


VT_ROWS = NSA_DH + 16


def value_slab_t(v_t):
    ones = jnp.ones(v_t.shape[:2] + (1, SEQ), BF16)
    zeros = jnp.zeros(v_t.shape[:2] + (VT_ROWS - NSA_DH - 1, SEQ), BF16)
    return jnp.concatenate([v_t.astype(BF16), ones, zeros], axis=2)


def sel_key_slab(ksel):
    blk = (np.arange(SEQ) % SA_TK) // SEL_BLOCK
    onehot = (blk[:, None] == np.arange(LANES - NSA_DH)[None, :]).astype(np.float32)
    onehot = jnp.broadcast_to(jnp.asarray(onehot, BF16), ksel.shape[:3] + (LANES - NSA_DH,))
    return jnp.concatenate([ksel.astype(BF16), onehot], axis=-1)


def _sattn_kernel(q_ref, k_ref, vt_ref, sel_ref, gl_ref, prev_ref, o_ref, qa_ref, acc_ref, s_ref, m_ref):
    tq, tk = SA_TQ, SA_TK
    i = pl.program_id(1)
    groups = range(NSA_KV_GROUPS)
    slots = range(2)
    for g in groups:
        for h in range(NSA_HPG):
            q = q_ref[0, g * NSA_HPG + h].astype(BF16)
            for slot in slots:
                qa_ref[slot, g, 0:NSA_DH, h * tq:(h + 1) * tq] = q
        for slot in slots:
            qa_ref[slot, g, NSA_DH:LANES, :] = jnp.zeros((LANES - NSA_DH, SA_COLS), BF16)
    acc_ref[...] = jnp.zeros_like(acc_ref)
    lanes4 = lambda t: jnp.concatenate([t] * NSA_HPG, axis=1)
    part_keys = lambda kt, part: pl.ds(pl.multiple_of(kt * tk + part * SA_PART, SA_PART), SA_PART)

    def scores(kt, slot):
        for g in groups:
            selrows = sel_ref[0, g, pl.ds(pl.multiple_of(kt * SA_BLOCKS, SA_BLOCKS), SA_BLOCKS), :]
            qa_ref[slot, g, NSA_DH:NSA_DH + SA_BLOCKS, :] = lanes4(jnp.where(selrows > 0.5, 0.0, NEG_INF)).astype(BF16)
            for part in range(SA_PARTS):
                s = jnp.dot(k_ref[0, g, part_keys(kt, part), :], qa_ref[slot, g], preferred_element_type=F32)
                s_ref[slot, g, part] = s.astype(BF16)

    def absorb(kt, slot, ms):
        out = []
        for g in groups:
            ss = [s_ref[slot, g, part] for part in range(SA_PARTS)]
            m_tile = functools.reduce(jnp.maximum, [jnp.max(s, axis=0, keepdims=True) for s in ss])
            m_new = jnp.maximum(ms[g], m_tile.astype(F32))
            acc = jnp.exp(ms[g] - m_new) * acc_ref[g]
            for part in range(SA_PARTS):
                p = jnp.exp(ss[part] - m_new.astype(BF16))
                acc += jnp.dot(vt_ref[0, g, :, part_keys(kt, part)], p, preferred_element_type=F32)
            acc_ref[g] = acc
            out.append(m_new)
        return tuple(out)

    def two_tiles(j, ms):
        kt = 2 * j
        scores(kt + 1, 1)
        ms = absorb(kt, 0, ms)
        scores(kt + 2, 0)
        return absorb(kt + 1, 1, ms)

    n_full = (i * tq) // tk
    scores(0, 0)
    m0 = tuple(jnp.full((1, SA_COLS), M_INIT, F32) for _ in groups)
    ms = lax.fori_loop(0, n_full // 2, two_tiles, m0)
    for g in groups:
        m_ref[g] = ms[g]

    def last_tile(slot):
        start = i * tq - n_full * tk
        part, row0 = start // SA_PART, pl.multiple_of(start % SA_PART, tq)
        tri = lax.broadcasted_iota(jnp.int32, (tq, tq), 0) <= lax.broadcasted_iota(jnp.int32, (tq, tq), 1)
        bias = lanes4(jnp.where(tri, 0.0, NEG_INF)).astype(BF16)
        for g in groups:
            s_ref[slot, g, part, pl.ds(row0, tq), :] += bias
        for g, m in enumerate(absorb(n_full, slot, tuple(m_ref[g] for g in groups))):
            m_ref[g] = m

    @pl.when(n_full % 2 == 0)
    def _():
        last_tile(0)

    @pl.when(n_full % 2 == 1)
    def _():
        scores(n_full, 1)
        for g, m in enumerate(absorb(n_full - 1, 0, tuple(m_ref[g] for g in groups))):
            m_ref[g] = m
        last_tile(1)

    for g in groups:
        heads = range(g * NSA_HPG, (g + 1) * NSA_HPG)
        gate = jnp.concatenate([jax.nn.sigmoid(gl_ref[0, h, 1:2, :]) for h in heads], axis=1)
        out = acc_ref[g, 0:NSA_DH, :] / acc_ref[g, NSA_DH:NSA_DH + 1, :] * gate
        for n, h in enumerate(heads):
            o_ref[0, h] = prev_ref[0, h] + out[:, n * tq:(n + 1) * tq]


def nsa_sel_attn(q_t, k_slab, vsel_t, sel_t, gl_t, prev):
    tq = SA_TQ
    G = NSA_KV_GROUPS
    ospec = pl.BlockSpec((1, NSA_HEADS, NSA_DH, tq), lambda b, i: (b, 0, 0, i))
    return pl.pallas_call(
        _sattn_kernel,
        grid=(BATCH, SEQ // tq),
        in_specs=[
            ospec,
            pl.BlockSpec((1, G, SEQ, LANES), lambda b, i: (b, 0, 0, 0)),
            pl.BlockSpec((1, G, VT_ROWS, SEQ), lambda b, i: (b, 0, 0, 0)),
            pl.BlockSpec((1, G, N_SEL, tq), lambda b, i: (b, 0, 0, i)),
            pl.BlockSpec((1, NSA_HEADS, 3, tq), lambda b, i: (b, 0, 0, i)),
            ospec,
        ],
        out_specs=ospec,
        out_shape=jax.ShapeDtypeStruct((BATCH, NSA_HEADS, NSA_DH, SEQ), F32),
        scratch_shapes=[pltpu.VMEM((2, G, LANES, SA_COLS), BF16), pltpu.VMEM((G, VT_ROWS, SA_COLS), F32),
                        pltpu.VMEM((2, G, SA_PARTS, SA_PART, SA_COLS), BF16), pltpu.VMEM((G, 1, SA_COLS), F32)],
        input_output_aliases={5: 0},
        compiler_params=_cparams(("arbitrary", "arbitrary")),
        name="nsa_sel_attn",
    )(q_t, k_slab, vsel_t, sel_t, gl_t, prev)


WA_TQ = 256
WA_TILES = WINDOW // WA_TQ + 1


def _window_bias():
    kl = np.arange(WA_TILES * WA_TQ)[:, None]
    ql = np.arange(WA_TQ)[None, :]
    diff = ql - kl + WINDOW
    return jnp.asarray(np.where((diff >= 0) & (diff < WINDOW), 0.0, NEG_INF).astype(np.float32))


def _wattn_kernel(q_ref, k0_ref, k1_ref, k2_ref, v0_ref, v1_ref, v2_ref, bias_ref, gl_ref, prev_ref, o_ref):
    tq = WA_TQ
    i = pl.program_id(1)
    k_refs = (k0_ref, k1_ref, k2_ref)
    v_refs = (v0_ref, v1_ref, v2_ref)
    lanes4 = lambda t: jnp.concatenate([t] * NSA_HPG, axis=1)
    biases = []
    for d in range(WA_TILES):
        in_seq = i - (WA_TILES - 1) + d >= 0
        biases.append(lanes4(jnp.where(in_seq, bias_ref[d * tq:(d + 1) * tq, :], NEG_INF)))
    for g in range(NSA_KV_GROUPS):
        heads = range(g * NSA_HPG, (g + 1) * NSA_HPG)
        q = jnp.concatenate([q_ref[0, h] for h in heads], axis=1).astype(BF16)
        ss = [(jnp.dot(k_refs[d][0, g], q, preferred_element_type=F32) + biases[d]).astype(BF16) for d in range(WA_TILES)]
        m = functools.reduce(jnp.maximum, [jnp.max(s, axis=0, keepdims=True) for s in ss])
        acc = functools.reduce(jnp.add, [jnp.dot(v_refs[d][0, g], jnp.exp(ss[d] - m), preferred_element_type=F32)
                                         for d in range(WA_TILES)])
        gate = jnp.concatenate([jax.nn.sigmoid(gl_ref[0, h, 2:3, :]) for h in heads], axis=1)
        out = acc[0:NSA_DH] / acc[NSA_DH:NSA_DH + 1] * gate
        for n, h in enumerate(heads):
            o_ref[0, h] = prev_ref[0, h] + out[:, n * tq:(n + 1) * tq]


def nsa_win_attn(q_t, kwin, vwin_t, gl_t, prev):
    tq = WA_TQ
    G = NSA_KV_GROUPS
    qspec = pl.BlockSpec((1, NSA_HEADS, NSA_DH, tq), lambda b, i: (b, 0, 0, i))
    tile = lambda d: (lambda i: jnp.maximum(i - (WA_TILES - 1) + d, 0))
    kspec = lambda d: pl.BlockSpec((1, G, tq, NSA_DH), lambda b, i: (b, 0, tile(d)(i), 0))
    vspec = lambda d: pl.BlockSpec((1, G, VT_ROWS, tq), lambda b, i: (b, 0, 0, tile(d)(i)))
    return pl.pallas_call(
        _wattn_kernel,
        grid=(BATCH, SEQ // tq),
        in_specs=[qspec] + [kspec(d) for d in range(WA_TILES)] + [vspec(d) for d in range(WA_TILES)] + [
            pl.BlockSpec((WA_TILES * tq, tq), lambda b, i: (0, 0)),
            pl.BlockSpec((1, NSA_HEADS, 3, tq), lambda b, i: (b, 0, 0, i)),
            qspec,
        ],
        out_specs=qspec,
        out_shape=jax.ShapeDtypeStruct((BATCH, NSA_HEADS, NSA_DH, SEQ), F32),
        input_output_aliases={2 * WA_TILES + 3: 0},
        compiler_params=_cparams(("arbitrary", "arbitrary")),
        name="nsa_win_attn",
    )(q_t, *([kwin] * WA_TILES), *([vwin_t] * WA_TILES), _window_bias(), gl_t, prev)


def nsa_mixer(nq, nkv, misc, positions, q_gain, k_gain, cmp_pe, cmp_w1, cmp_w2):
    cs = rope_tables(positions)
    q_t, ks_r, kw_r = nsa_prep(nq, nkv, cs, q_gain, k_gain)
    group_major = lambda t: t.reshape(BATCH, SEQ, NSA_KV_GROUPS, NSA_DH).transpose(0, 2, 1, 3)
    group_major_t = lambda t: t.reshape(BATCH, SEQ, NSA_KV_GROUPS, NSA_DH).transpose(0, 2, 3, 1)
    col = lambda n: nkv[..., n * NSA_KV_W:(n + 1) * NSA_KV_W]
    segs = lambda t: group_major(t).reshape(BATCH, NSA_KV_GROUPS, N_CMP_PAD, SEG_W)
    last = jnp.minimum(jnp.arange(N_CMP_PAD) * CMP_STRIDE + CMP_LEN - 1, SEQ - 1)
    kcmp, vcmp = nsa_compress(segs(col(0)), segs(col(1)), cmp_pe, cmp_w1, cmp_w2, k_gain[0], cs[:, last])
    kcmp = kcmp.reshape(BATCH, N_CMP_PAD, NSA_KV_GROUPS, NSA_DH).transpose(0, 2, 1, 3)
    vcmp_t = vcmp.reshape(BATCH, N_CMP_PAD, NSA_KV_GROUPS, NSA_DH).transpose(0, 2, 3, 1)
    gl_t = misc[..., GLA_GATE_RANK:GLA_GATE_RANK + NSA_HEADS * 3].reshape(BATCH, SEQ, NSA_HEADS, 3).transpose(0, 2, 3, 1)
    o_t, sel_t = nsa_cmp_attn(q_t, kcmp, vcmp_t, gl_t)
    o_t = nsa_sel_attn(q_t, sel_key_slab(group_major(ks_r)), value_slab_t(group_major_t(col(3))), sel_t, gl_t, o_t)
    o_t = nsa_win_attn(q_t, group_major(kw_r).astype(BF16), value_slab_t(group_major_t(col(5))), gl_t, o_t)
    return o_t


ROUTE_ROWS = 8
HX_W = D_MODEL + 3 * LANES


def _top2_sum(a, b, c, d):
    hi1, lo1 = jnp.maximum(a, b), jnp.minimum(a, b)
    hi2, lo2 = jnp.maximum(c, d), jnp.minimum(c, d)
    return jnp.maximum(hi1, hi2) + jnp.maximum(jnp.minimum(hi1, hi2), jnp.maximum(lo1, lo2))


def _moe_prenorm_route(xn, g_ref, sh_ref, sc_ref, wr_ref, rb_ref, hx_ref, route_ref):
    h = _rms_mod(xn, g_ref[...], sh_ref[0], sc_ref[0])
    logits = lax.dot_general(wr_ref[...], h, NT_DIMS, precision=HI, preferred_element_type=F32)
    scores = jax.nn.sigmoid(logits)
    sel = scores + rb_ref[...]
    epg = EXPERTS_PER_GROUP
    srow = lambda e: sel[e:e + 1, :]
    grp = [_top2_sum(*[srow(epg * g + r) for r in range(epg)]) for g in range(N_EXPERT_GROUPS)]
    best, gi = grp[0], jnp.zeros_like(grp[0], dtype=jnp.int32)
    for g in range(1, N_EXPERT_GROUPS):
        better = grp[g] > best
        gi = jnp.where(better, g, gi)
        best = jnp.where(better, grp[g], best)

    def in_group(mat, r):
        out = mat[r:r + 1, :]
        for g in range(1, N_EXPERT_GROUPS):
            out = jnp.where(gi == g, mat[epg * g + r:epg * g + r + 1, :], out)
        return out

    v = [in_group(sel, r) for r in range(epg)]
    sc = [in_group(scores, r) for r in range(epg)]
    b1, i1, w1 = v[0], jnp.zeros_like(gi), sc[0]
    for r in range(1, epg):
        better = v[r] > b1
        i1 = jnp.where(better, r, i1)
        w1 = jnp.where(better, sc[r], w1)
        b1 = jnp.where(better, v[r], b1)
    b2 = jnp.full_like(b1, -3e38)
    i2, w2 = jnp.zeros_like(gi), jnp.zeros_like(w1)
    for r in range(epg):
        better = (i1 != r) & (v[r] > b2)
        i2 = jnp.where(better, r, i2)
        w2 = jnp.where(better, sc[r], w2)
        b2 = jnp.where(better, v[r], b2)
    tot = w1 + w2
    w1, w2 = w1 / tot, w2 / tot
    zero = jnp.zeros_like(w1)
    route_ref[0] = jnp.concatenate([gi.astype(F32)] + [zero] * (ROUTE_ROWS - 1), axis=0)
    wrows = [jnp.where(i1 == r, w1, jnp.where(i2 == r, w2, 0.0)) for r in range(epg)]
    wmat = jnp.concatenate(wrows + [jnp.zeros((LANES - epg, w1.shape[1]), F32)], axis=0).T
    w_hi = wmat.astype(BF16)
    rest = wmat - w_hi.astype(F32)
    w_mid = rest.astype(BF16)
    w_lo = (rest - w_mid.astype(F32)).astype(BF16)
    hx_ref[0, :, 0:D_MODEL] = h.astype(BF16)
    for n, part in enumerate((w_hi, w_mid, w_lo)):
        hx_ref[0, :, D_MODEL + n * LANES:D_MODEL + (n + 1) * LANES] = part


def _route_specs(tm, row, vec, const):
    in_specs = [pl.BlockSpec((1, D_MODEL), const), pl.BlockSpec((1, 1, D_MODEL), vec), pl.BlockSpec((1, 1, D_MODEL), vec),
                pl.BlockSpec((N_EXPERTS, D_MODEL), const), pl.BlockSpec((N_EXPERTS, 1), const)]
    out_specs = [pl.BlockSpec((1, tm, HX_W), row), pl.BlockSpec((1, ROUTE_ROWS, tm), lambda b, i: (b, 0, i))]
    out_shape = [jax.ShapeDtypeStruct((BATCH, SEQ, HX_W), BF16), jax.ShapeDtypeStruct((BATCH, ROUTE_ROWS, SEQ), F32)]
    return in_specs, out_specs, out_shape


def _route_args(g, shift, scale, w_router, router_bias):
    return (g.reshape(1, D_MODEL), shift.reshape(BATCH, 1, D_MODEL), scale.reshape(BATCH, 1, D_MODEL),
            w_router.T, router_bias.reshape(N_EXPERTS, 1))


OUTPROJ_TM = 512


def _outproj0_kernel(oa_ref, ob_ref, w_ref, x_ref, gate_ref, g_ref, sh_ref, sc_ref, wr_ref, rb_ref,
                     xo_ref, h_ref, route_ref):
    y = jnp.dot(oa_ref[0].astype(BF16), w_ref[0:GLA_V_W, :], preferred_element_type=F32)
    ob_t = ob_ref[0].reshape(NSA_Q_W, OUTPROJ_TM).astype(BF16)
    y += lax.dot_general(ob_t, w_ref[GLA_V_W:GLA_V_W + NSA_Q_W, :], TN_DIMS, preferred_element_type=F32)
    xn = x_ref[0] + gate_ref[0] * y
    xo_ref[0] = xn
    _moe_prenorm_route(xn, g_ref, sh_ref, sc_ref, wr_ref, rb_ref, h_ref, route_ref)


def outproj0(o_a, o_b, w_out, x, gate, route_args):
    tm = OUTPROJ_TM
    row = lambda b, i: (b, i, 0)
    vec = lambda b, i: (b, 0, 0)
    const = lambda b, i: (0, 0)
    r_in, r_out, r_shape = _route_specs(tm, row, vec, const)
    return pl.pallas_call(
        _outproj0_kernel,
        grid=(BATCH, SEQ // tm),
        in_specs=[pl.BlockSpec((1, tm, GLA_V_W), row), pl.BlockSpec((1, NSA_HEADS, NSA_DH, tm), lambda b, i: (b, 0, 0, i)),
                  pl.BlockSpec((GLA_V_W + NSA_Q_W, D_MODEL), const), pl.BlockSpec((1, tm, D_MODEL), row),
                  pl.BlockSpec((1, 1, D_MODEL), vec)] + r_in,
        out_specs=[pl.BlockSpec((1, tm, D_MODEL), row)] + r_out,
        out_shape=[jax.ShapeDtypeStruct((BATCH, SEQ, D_MODEL), F32)] + r_shape,
        compiler_params=_cparams(("arbitrary", "arbitrary")),
        name="outproj0",
    )(o_a, o_b, w_out.astype(BF16), x, gate.reshape(BATCH, 1, D_MODEL), *route_args)


GMLP_TM = 512


def _gmlp_kernel(x_ref, g1_ref, sh1_ref, sc1_ref, win_ref, ng_ref, ws_ref, bs_ref, wout_ref, gate_ref,
                 g_ref, sh_ref, sc_ref, wr_ref, rb_ref, xo_ref, h_ref, route_ref, gated_ref, v_ref):
    x = x_ref[0]
    h = _rms_mod(x, g1_ref[...], sh1_ref[0], sc1_ref[0]).astype(BF16)
    group_cols = lambda g: slice(g * SGU_GROUP_DIM, (g + 1) * SGU_GROUP_DIM)
    ssq = jnp.zeros((GMLP_TM, LANES), F32)
    for g in range(SGU_GROUPS):
        lo = SGU_WIDTH + g * SGU_GROUP_DIM
        v = jax.nn.gelu(jnp.dot(h, win_ref[:, lo:lo + SGU_GROUP_DIM], preferred_element_type=F32))
        v_ref[:, group_cols(g)] = v
        ssq += functools.reduce(jnp.add, [v[:, n * LANES:(n + 1) * LANES] ** 2 for n in range(SGU_GROUP_DIM // LANES)])
    rs = lax.rsqrt(jnp.sum(ssq, axis=-1, keepdims=True) * (1.0 / SGU_WIDTH) + NORM_EPS)
    ri = lax.broadcasted_iota(jnp.int32, (SGU_CHUNK, SGU_CHUNK), 0)
    ci = lax.broadcasted_iota(jnp.int32, (SGU_CHUNK, SGU_CHUNK), 1)
    for g in range(SGU_GROUPS):
        cols = group_cols(g)
        u = jax.nn.gelu(jnp.dot(h, win_ref[:, cols], preferred_element_type=F32))
        vn = (v_ref[:, cols] * rs * ng_ref[:, cols]).astype(BF16)
        w = jnp.where(ri >= ci, ws_ref[g], 0.0).astype(BF16)
        for c in range(GMLP_TM // SGU_CHUNK):
            rows = slice(c * SGU_CHUNK, (c + 1) * SGU_CHUNK)
            mix = jnp.dot(w, vn[rows], preferred_element_type=F32) + bs_ref[:, g:g + 1]
            gated_ref[rows, cols] = (u[rows] * mix).astype(BF16)
    y = jnp.dot(gated_ref[...], wout_ref[...], preferred_element_type=F32)
    xn = x + gate_ref[0] * y
    xo_ref[0] = xn
    _moe_prenorm_route(xn, g_ref, sh_ref, sc_ref, wr_ref, rb_ref, h_ref, route_ref)


def gmlp_layer(x, g1, shift1, scale1, w_in, norm_g, w_s, b_s, w_out, gate, route_args):
    tm = GMLP_TM
    row = lambda b, i: (b, i, 0)
    vec = lambda b, i: (b, 0, 0)
    const = lambda b, i: (0, 0)
    r_in, r_out, r_shape = _route_specs(tm, row, vec, const)
    vspec = pl.BlockSpec((1, 1, D_MODEL), vec)
    return pl.pallas_call(
        _gmlp_kernel,
        grid=(BATCH, SEQ // tm),
        in_specs=[pl.BlockSpec((1, tm, D_MODEL), row), pl.BlockSpec((1, D_MODEL), const), vspec, vspec,
                  pl.BlockSpec((D_MODEL, 2 * SGU_WIDTH), const), pl.BlockSpec((1, SGU_WIDTH), const),
                  pl.BlockSpec((SGU_GROUPS, SGU_CHUNK, SGU_CHUNK), lambda b, i: (0, 0, 0)),
                  pl.BlockSpec((SGU_CHUNK, SGU_GROUPS), const), pl.BlockSpec((SGU_WIDTH, D_MODEL), const), vspec] + r_in,
        out_specs=[pl.BlockSpec((1, tm, D_MODEL), row)] + r_out,
        out_shape=[jax.ShapeDtypeStruct((BATCH, SEQ, D_MODEL), F32)] + r_shape,
        scratch_shapes=[pltpu.VMEM((tm, SGU_WIDTH), BF16), pltpu.VMEM((tm, SGU_WIDTH), F32)],
        compiler_params=_cparams(("arbitrary", "arbitrary"), vmem_mib=56),
        name="gmlp_layer",
    )(x, g1.reshape(1, D_MODEL), shift1.reshape(BATCH, 1, D_MODEL), scale1.reshape(BATCH, 1, D_MODEL),
      w_in.astype(BF16), norm_g.reshape(1, SGU_WIDTH), w_s, b_s.T, w_out.astype(BF16),
      gate.reshape(BATCH, 1, D_MODEL), *route_args)


MOE_TM = 256
MOE_CHUNK = 512
MOE_SORTED = N_TOK + N_EXPERT_GROUPS * MOE_TM
MOE_TILES = MOE_SORTED // MOE_TM
MOE_CHUNKS = N_TOK // MOE_CHUNK
MOE_PAIRS = MOE_TILES + N_EXPERT_GROUPS * MOE_CHUNKS
FLAG_ACTIVE, FLAG_FIRST, FLAG_LAST, FLAG_ZERO = 1, 2, 4, 8
EXP_WIN = 4
CMB_WIN = 8
MOE_TSTEPS = MOE_TILES + MOE_PAIRS // EXP_WIN
MOE_CSTEPS = MOE_CHUNKS + MOE_PAIRS // CMB_WIN


def _plan_kernel(gi_ref, rank_ref, before_ref):
    gi = gi_ref[...]
    r = lax.broadcasted_iota(jnp.int32, (POS_SIDE, POS_SIDE), 0)
    c = lax.broadcasted_iota(jnp.int32, (POS_SIDE, POS_SIDE), 1)
    upper = jnp.where(r <= c, 1.0, 0.0)
    lower_strict = jnp.where(c < r, 1.0, 0.0)
    rank = jnp.zeros((POS_SIDE, POS_SIDE), F32)
    for g in range(N_EXPERT_GROUPS):
        member = jnp.where(gi == g, 1.0, 0.0)
        in_row = jnp.dot(member, upper, precision=HI, preferred_element_type=F32)
        row_total = jnp.broadcast_to(in_row[:, POS_SIDE - 1:POS_SIDE], (POS_SIDE, POS_SIDE))
        before = jnp.dot(lower_strict, row_total, precision=HI, preferred_element_type=F32)
        before_ref[g] = before
        rank += member * (before + in_row - 1.0)
    rank_ref[...] = rank


def moe_plan(route):
    tm = MOE_TM
    i32 = jnp.int32
    gi_f = route[:, 0, :].reshape(POS_SIDE, POS_SIDE)
    rank, before = pl.pallas_call(
        _plan_kernel,
        out_shape=[jax.ShapeDtypeStruct((POS_SIDE, POS_SIDE), F32),
                   jax.ShapeDtypeStruct((N_EXPERT_GROUPS, POS_SIDE, POS_SIDE), F32)],
        name="moe_plan",
    )(gi_f)
    gi = gi_f.reshape(N_TOK).astype(i32)
    groups = jnp.arange(N_EXPERT_GROUPS, dtype=i32)
    member = gi[None, :] == groups[:, None]
    tot = jnp.sum(member, axis=1).astype(i32)
    padded = (tot + tm - 1) // tm * tm
    gend = jnp.cumsum(padded).astype(i32)
    gstart = gend - padded
    pos = jnp.sum(jnp.where(member, gstart[:, None], 0), axis=0).astype(i32) + rank.reshape(N_TOK).astype(i32)
    rows_per_chunk = MOE_CHUNK // POS_SIDE
    cnt_end = jnp.concatenate([before[:, rows_per_chunk::rows_per_chunk, 0].astype(i32), tot[:, None]], axis=1)
    t = jnp.arange(MOE_TILES, dtype=i32)
    n_used = gend[-1] // tm
    tile_g = jnp.minimum(jnp.sum(gend[None, :] <= (t * tm)[:, None], axis=1), N_EXPERT_GROUPS - 1).astype(i32)
    k0 = t * tm - gstart[tile_g]
    k1 = jnp.minimum(k0 + tm, tot[tile_g]) - 1
    ce = cnt_end[tile_g]
    c_lo = jnp.sum(ce <= k0[:, None], axis=1).astype(i32)
    c_hi = jnp.sum(ce <= k1[:, None], axis=1).astype(i32)
    npairs = jnp.where(t < n_used, c_hi - c_lo + 1, 0)
    pend = jnp.cumsum(npairs).astype(i32)
    pstart = pend - npairs
    total = pend[-1]
    l = jnp.arange(MOE_PAIRS, dtype=i32)
    real = l < total
    lt = jnp.minimum(l, total - 1)

    def windows(count, win, n_steps):
        per_item = (count + win - 1) // win
        end = jnp.cumsum(per_item).astype(i32)
        start = end - per_item
        s = jnp.arange(n_steps, dtype=i32)
        real_s = s < end[-1]
        sc = jnp.minimum(s, end[-1] - 1)
        item = jnp.sum(end[None, :] <= sc[:, None], axis=1).astype(i32)
        j = sc - start[item]
        flags_s = jnp.where(real_s, FLAG_ACTIVE + jnp.where(j == 0, FLAG_FIRST, 0)
                            + jnp.where(j == per_item[item] - 1, FLAG_LAST, 0), 0).astype(i32)
        return item, j, flags_s, real_s, s - end[-1]

    tile_s, j, flags, real_s, spare = windows(npairs, EXP_WIN, MOE_TSTEPS)
    c0 = c_lo[tile_s] + EXP_WIN * j
    n_valid = jnp.minimum(EXP_WIN, c_hi[tile_s] - c0 + 1).astype(i32)
    spare_tile = jnp.minimum(n_used + spare, MOE_TILES - 1)
    flags = jnp.where(real_s, flags, jnp.where(spare_tile >= n_used, FLAG_ZERO, 0)).astype(i32)
    tile_sched = jnp.where(real_s, tile_s, spare_tile).astype(i32)
    by_tile = (tile_sched, c0.astype(i32), n_valid, flags, tile_g[tile_sched])
    cc = jnp.arange(MOE_CHUNKS, dtype=i32)
    is_pair = (cc[:, None] >= c_lo[None, :]) & (cc[:, None] <= c_hi[None, :]) & (t[None, :] < n_used)
    seen = jnp.cumsum(is_pair.reshape(-1).astype(i32))
    flat = jnp.sum(seen[None, :] <= lt[:, None], axis=1).astype(i32)
    pair_tile = flat % MOE_TILES
    per_chunk = jnp.sum(is_pair, axis=1).astype(i32)
    first_pair = jnp.cumsum(per_chunk).astype(i32) - per_chunk
    chunk_s, j, flags_c, _, _ = windows(per_chunk, CMB_WIN, MOE_CSTEPS)
    base = first_pair[chunk_s] + CMB_WIN * j
    n_valid_c = jnp.minimum(CMB_WIN, per_chunk[chunk_s] - CMB_WIN * j).astype(i32)
    tiles_c = tuple(pair_tile[jnp.minimum(base + w, total - 1)] for w in range(CMB_WIN))
    by_chunk = (chunk_s, n_valid_c, flags_c) + tiles_c
    return pos.reshape(MOE_CHUNKS, 1, MOE_CHUNK), by_tile, by_chunk


def _one_hot_rows(pos_row, tile):
    rows = tile * MOE_TM + lax.broadcasted_iota(jnp.int32, (MOE_TM, MOE_CHUNK), 0)
    return jnp.where(pos_row == rows, 1.0, 0.0).astype(BF16)


def _moe_kernel(tile_ref, c0_ref, nv_ref, flag_ref, grp_ref, *refs):
    pos_refs, hx_refs = refs[0:EXP_WIN], refs[EXP_WIN:2 * EXP_WIN]
    wg_ref, wu_ref, wd_ref, y_ref, acc_ref = refs[2 * EXP_WIN:]
    l = pl.program_id(0)
    flags = flag_ref[l]

    @pl.when((flags & FLAG_FIRST) != 0)
    def _():
        acc_ref[...] = jnp.zeros_like(acc_ref)

    for w in range(EXP_WIN):
        @pl.when(((flags & FLAG_ACTIVE) != 0) & (w < nv_ref[l]))
        def _():
            onehot = _one_hot_rows(pos_refs[w][0], tile_ref[l])
            acc_ref[...] += jnp.dot(onehot, hx_refs[w][0], preferred_element_type=F32)

    @pl.when((flags & FLAG_LAST) != 0)
    def _():
        x = acc_ref[:, 0:D_MODEL].astype(BF16)
        w = functools.reduce(jnp.add, [acc_ref[:, D_MODEL + n * LANES:D_MODEL + (n + 1) * LANES] for n in range(3)])
        y = jnp.zeros((MOE_TM, D_MODEL), F32)
        for r in range(EXPERTS_PER_GROUP):
            gate = jnp.dot(x, wg_ref[0, 0, r], preferred_element_type=F32)
            up = jnp.dot(x, wu_ref[0, 0, r], preferred_element_type=F32)
            hid = (_silu(gate) * up * w[:, r:r + 1]).astype(BF16)
            y += jnp.dot(hid, wd_ref[0, 0, r], preferred_element_type=F32)
        y_ref[...] = y

    @pl.when((flags & FLAG_ZERO) != 0)
    def _():
        y_ref[...] = jnp.zeros_like(y_ref)


def moe_experts(hx, pos, by_tile, w_gate, w_up, w_down, layer):
    grouped = lambda w: w.reshape(DEPTH, N_EXPERT_GROUPS, EXPERTS_PER_GROUP, *w.shape[2:])
    wspec = lambda k, n: pl.BlockSpec((1, 1, EXPERTS_PER_GROUP, k, n), lambda l, t, c, n_, f, g: (layer, g[l], 0, 0, 0))
    chunk = lambda w: (lambda l, t, c, n_, f, g: (jnp.minimum(c[l] + w, MOE_CHUNKS - 1), 0, 0))
    hx_chunks = hx.reshape(MOE_CHUNKS, MOE_CHUNK, HX_W)
    return pl.pallas_call(
        _moe_kernel,
        grid_spec=pltpu.PrefetchScalarGridSpec(
            num_scalar_prefetch=5,
            grid=(MOE_TSTEPS,),
            in_specs=[pl.BlockSpec((1, 1, MOE_CHUNK), chunk(w)) for w in range(EXP_WIN)]
            + [pl.BlockSpec((1, MOE_CHUNK, HX_W), chunk(w)) for w in range(EXP_WIN)]
            + [wspec(D_MODEL, EXPERT_HIDDEN), wspec(D_MODEL, EXPERT_HIDDEN), wspec(EXPERT_HIDDEN, D_MODEL)],
            out_specs=pl.BlockSpec((MOE_TM, D_MODEL), lambda l, t, c, n_, f, g: (t[l], 0)),
            scratch_shapes=[pltpu.VMEM((MOE_TM, HX_W), F32)],
        ),
        out_shape=jax.ShapeDtypeStruct((MOE_SORTED, D_MODEL), F32),
        compiler_params=_cparams(("arbitrary",), vmem_mib=56),
        name="moe_experts",
    )(*by_tile, *([pos] * EXP_WIN), *([hx_chunks] * EXP_WIN), grouped(w_gate), grouped(w_up), grouped(w_down))


def _moe_combine_kernel(chunk_ref, nv_ref, flag_ref, *refs):
    tile_refs = refs[0:CMB_WIN]
    pos_ref = refs[CMB_WIN]
    y_refs = refs[CMB_WIN + 1:2 * CMB_WIN + 1]
    x_ref, gate_ref, o_ref, acc_ref = refs[2 * CMB_WIN + 1:]
    l = pl.program_id(0)
    flags = flag_ref[l]

    @pl.when((flags & FLAG_FIRST) != 0)
    def _():
        acc_ref[...] = jnp.zeros_like(acc_ref)

    for w in range(CMB_WIN):
        @pl.when(((flags & FLAG_ACTIVE) != 0) & (w < nv_ref[l]))
        def _():
            onehot = _one_hot_rows(pos_ref[0], tile_refs[w][l])
            y = y_refs[w][...]
            y_hi = y.astype(BF16)
            y_lo = (y - y_hi.astype(F32)).astype(BF16)
            acc_ref[...] += (lax.dot_general(onehot, y_hi, TN_DIMS, preferred_element_type=F32)
                             + lax.dot_general(onehot, y_lo, TN_DIMS, preferred_element_type=F32))

    @pl.when((flags & FLAG_LAST) != 0)
    def _():
        o_ref[0] = x_ref[0] + gate_ref[0] * acc_ref[...]


def moe_combine(x, y_sorted, pos, by_chunk, gate):
    per_b = SEQ // MOE_CHUNK
    tok = lambda l, c, *_: (c[l] // per_b, c[l] % per_b, 0)
    tile = lambda w: (lambda l, c, n_, f, *tiles: (tiles[w][l], 0))
    return pl.pallas_call(
        _moe_combine_kernel,
        grid_spec=pltpu.PrefetchScalarGridSpec(
            num_scalar_prefetch=3 + CMB_WIN,
            grid=(MOE_CSTEPS,),
            in_specs=[pl.BlockSpec((1, 1, MOE_CHUNK), lambda l, c, *_: (c[l], 0, 0))]
            + [pl.BlockSpec((MOE_TM, D_MODEL), tile(w)) for w in range(CMB_WIN)]
            + [pl.BlockSpec((1, MOE_CHUNK, D_MODEL), tok),
               pl.BlockSpec((1, 1, D_MODEL), lambda l, c, *_: (c[l] // per_b, 0, 0))],
            out_specs=pl.BlockSpec((1, MOE_CHUNK, D_MODEL), tok),
            scratch_shapes=[pltpu.VMEM((MOE_CHUNK, D_MODEL), F32)],
        ),
        out_shape=jax.ShapeDtypeStruct((BATCH, SEQ, D_MODEL), F32),
        compiler_params=_cparams(("arbitrary",)),
        name="moe_combine",
    )(*by_chunk, pos, *([y_sorted] * CMB_WIN), x, gate.reshape(BATCH, 1, D_MODEL))


def moe_layer(x, hx, route, gate, w_gate, w_up, w_down, layer):
    pos, by_tile, by_chunk = moe_plan(route)
    y_sorted = moe_experts(hx, pos, by_tile, w_gate, w_up, w_down, layer)
    return moe_combine(x, y_sorted, pos, by_chunk, gate)


def kernel(x, c, positions, w_ada, b_ada, norm_g, w_in_ab, w_out_ab, gla_w_gate2, gla_b_gate, gla_norm_g, nsa_q_gain, nsa_k_gain, nsa_cmp_pe, nsa_cmp_w1, nsa_cmp_w2, w_in_c, sgu_norm_g, sgu_w_s, sgu_b_s, w_out_c, w_router, router_bias, w_gate, w_up, w_down):
    mod = ada_modulation(c, w_ada, b_ada)
    qk, gv, gr, nq, nkv, misc = inproj0(x, norm_g[0, 0], mod[0, :, 0], mod[0, :, 1], _arrange_w_in(w_in_ab[0]))
    o_a = gla_mixer(qk, gv, gr, misc, gla_w_gate2[0], gla_b_gate[0], gla_norm_g[0])
    o_b = nsa_mixer(nq, nkv, misc, positions, nsa_q_gain[0], nsa_k_gain[0], nsa_cmp_pe[0], nsa_cmp_w1[0], nsa_cmp_w2[0])
    wg, wu, wd = w_gate.astype(BF16), w_up.astype(BF16), w_down.astype(BF16)
    route_args = lambda l: _route_args(norm_g[l, 1], mod[l, :, 3], mod[l, :, 4], w_router, router_bias)
    x1, h, route = outproj0(o_a, o_b, w_out_ab[0], x, mod[0, :, 2], route_args(0))
    x2 = moe_layer(x1, h, route, mod[0, :, 5], wg, wu, wd, 0)
    x3, h, route = gmlp_layer(x2, norm_g[1, 0], mod[1, :, 0], mod[1, :, 1], w_in_c[0], sgu_norm_g[0], sgu_w_s[0],
                              sgu_b_s[0], w_out_c[0], mod[1, :, 2], route_args(1))
    return moe_layer(x3, h, route, mod[1, :, 5], wg, wu, wd, 1)
```

```python
import functools

import numpy as np
import jax
import jax.numpy as jnp
from jax import lax
from jax.experimental import pallas as pl
from jax.experimental.pallas import tpu as pltpu

D_MODEL = 1024
BATCH = 2
SEQ = 8192
DEPTH = 2
N_TOK = BATCH * SEQ

GLA_HEADS = 4
GLA_DK = 64
GLA_DV = 128
GLA_GATE_RANK = 16
GLA_TAU = 16.0
GLA_CHUNK = 64
NSA_HEADS = 8
NSA_KV_GROUPS = 2
NSA_HPG = NSA_HEADS // NSA_KV_GROUPS
NSA_DH = 64
CMP_LEN = 32
CMP_STRIDE = 16
CMP_HIDDEN = 256
SEL_BLOCK = 64
SEL_TOPK = 16
WINDOW = 512
ROPE_THETA = 500000.0
ROT_DIM = NSA_DH // 4
ROT_HALF = ROT_DIM // 2
SGU_CHUNK = 128
SGU_GROUPS = 8
SGU_WIDTH = 2048
SGU_GROUP_DIM = SGU_WIDTH // SGU_GROUPS
N_EXPERTS = 16
N_EXPERT_GROUPS = 4
EXPERTS_PER_GROUP = N_EXPERTS // N_EXPERT_GROUPS
MOE_TOPK = 2
EXPERT_HIDDEN = 512

GLA_QK_W = GLA_HEADS * GLA_DK
GLA_V_W = GLA_HEADS * GLA_DV
NSA_Q_W = NSA_HEADS * NSA_DH
NSA_KV_W = NSA_KV_GROUPS * NSA_DH
N_CMP = (SEQ - CMP_LEN) // CMP_STRIDE + 1
N_CMP_PAD = SEQ // CMP_STRIDE
N_SEL = SEQ // SEL_BLOCK

NORM_EPS = 1e-6
NEG_INF = -1e30
FORCE_BONUS = 1e4

LANES = 128
MIB = 1024 * 1024

F32 = jnp.float32
BF16 = jnp.bfloat16
HI = lax.Precision.HIGHEST
NT_DIMS = (((1,), (1,)), ((), ()))
TN_DIMS = (((0,), (0,)), ((), ()))


def _cparams(sem, vmem_mib=48):
    return pltpu.CompilerParams(dimension_semantics=sem, vmem_limit_bytes=vmem_mib * MIB)


def _rms_mod(x, g, shift, scale):
    y = x * lax.rsqrt(jnp.mean(x * x, axis=-1, keepdims=True) + NORM_EPS) * g
    return y * (1 + scale) + shift


def _silu(x):
    return x * jax.nn.sigmoid(x)


def _log_sigmoid(z):
    return jnp.minimum(z, 0.0) - jnp.log1p(jnp.exp(-jnp.abs(z)))


ADA_TN = 1536
ADA_ROWS = 8


def _ada_kernel(c_ref, w_ref, b_ref, o_ref):
    cond = _silu(c_ref[...])
    o_ref[0] = jnp.dot(cond, w_ref[0], precision=HI, preferred_element_type=F32) + b_ref[0]


def ada_modulation(c, w_ada, b_ada):
    c8 = jnp.zeros((ADA_ROWS, D_MODEL), F32).at[:BATCH].set(c)
    width = 6 * D_MODEL
    out = pl.pallas_call(
        _ada_kernel,
        grid=(DEPTH, width // ADA_TN),
        in_specs=[
            pl.BlockSpec((ADA_ROWS, D_MODEL), lambda l, j: (0, 0)),
            pl.BlockSpec((1, D_MODEL, ADA_TN), lambda l, j: (l, 0, j)),
            pl.BlockSpec((1, 1, ADA_TN), lambda l, j: (l, 0, j)),
        ],
        out_specs=pl.BlockSpec((1, ADA_ROWS, ADA_TN), lambda l, j: (l, 0, j)),
        out_shape=jax.ShapeDtypeStruct((DEPTH, ADA_ROWS, width), F32),
        compiler_params=_cparams(("arbitrary", "arbitrary")),
        name="ada_modulation",
    )(c8, w_ada, b_ada.reshape(DEPTH, 1, width))
    return out[:, :BATCH].reshape(DEPTH, BATCH, 6, D_MODEL)


INPROJ_TM = 512
INPROJ_WIDTHS = (2 * GLA_QK_W, GLA_V_W, GLA_V_W, NSA_Q_W, 6 * NSA_KV_W, LANES)


def _arrange_w_in(w_in):
    o = np.cumsum((0, GLA_QK_W, GLA_QK_W, GLA_V_W, GLA_GATE_RANK, GLA_V_W, NSA_Q_W, 6 * NSA_KV_W, NSA_HEADS * 3))
    gq_gk = w_in[:, o[0]:o[2]]
    gv = w_in[:, o[2]:o[3]]
    glr = w_in[:, o[3]:o[4]]
    gr = w_in[:, o[4]:o[5]]
    nq = w_in[:, o[5]:o[6]]
    nkv = w_in[:, o[6]:o[7]]
    ng = w_in[:, o[7]:o[8]]
    pad = jnp.zeros((D_MODEL, LANES - GLA_GATE_RANK - NSA_HEADS * 3), w_in.dtype)
    return jnp.concatenate([gq_gk, gv, gr, nq, nkv, glr, ng, pad], axis=1).astype(BF16)


def _inproj0_kernel(x_ref, g_ref, sh_ref, sc_ref, w_ref, *o_refs):
    h = _rms_mod(x_ref[0], g_ref[...], sh_ref[0], sc_ref[0]).astype(BF16)
    off = 0
    for o_ref, wd in zip(o_refs, INPROJ_WIDTHS):
        o_ref[0] = jnp.dot(h, w_ref[:, off:off + wd], preferred_element_type=F32)
        off += wd


def inproj0(x, g, shift, scale, w_arranged):
    tm = INPROJ_TM
    wtot = sum(INPROJ_WIDTHS)
    row = lambda b, i: (b, i, 0)
    vec = lambda b, i: (b, 0, 0)
    return pl.pallas_call(
        _inproj0_kernel,
        grid=(BATCH, SEQ // tm),
        in_specs=[
            pl.BlockSpec((1, tm, D_MODEL), row),
            pl.BlockSpec((1, D_MODEL), lambda b, i: (0, 0)),
            pl.BlockSpec((1, 1, D_MODEL), vec),
            pl.BlockSpec((1, 1, D_MODEL), vec),
            pl.BlockSpec((D_MODEL, wtot), lambda b, i: (0, 0)),
        ],
        out_specs=[pl.BlockSpec((1, tm, wd), row) for wd in INPROJ_WIDTHS],
        out_shape=[jax.ShapeDtypeStruct((BATCH, SEQ, wd), F32) for wd in INPROJ_WIDTHS],
        compiler_params=_cparams(("arbitrary", "arbitrary")),
        name="inproj0",
    )(x, g.reshape(1, D_MODEL), shift.reshape(BATCH, 1, D_MODEL), scale.reshape(BATCH, 1, D_MODEL), w_arranged)


GLA_TG = 512


def _gla_chunk_sums():
    i = np.arange(GLA_TG)[:, None]
    j = np.arange(GLA_TG)[None, :]
    same = (i // GLA_CHUNK) == (j // GLA_CHUNK)
    m3 = np.concatenate([same & (j <= i), same & (j % GLA_CHUNK <= GLA_CHUNK // 2), same], axis=0).astype(np.float32)
    return jnp.asarray(np.concatenate([m3, m3], axis=1), BF16)


def _gla_kernel(qk_ref, v_ref, r_ref, misc_ref, w2_ref, bg_ref, og_ref, sums_ref, o_ref, st_ref):
    C, tg = GLA_CHUNK, GLA_TG

    @pl.when(pl.program_id(1) == 0)
    def _():
        st_ref[...] = jnp.zeros_like(st_ref)

    z = jnp.dot(misc_ref[0], w2_ref[...], precision=HI, preferred_element_type=F32) + bg_ref[...]
    la = _log_sigmoid(z) / GLA_TAU
    la_hi = la.astype(BF16)
    la_lo = (la - la_hi.astype(F32)).astype(BF16)
    sums = jnp.dot(sums_ref[...], jnp.concatenate([la_hi, la_lo], axis=0), preferred_element_type=F32)
    bc, b_mid, b_last = sums[0:tg], sums[tg:2 * tg], sums[2 * tg:3 * tg]

    q = qk_ref[0, :, 0:GLA_QK_W] * (GLA_DK ** -0.5)
    k = qk_ref[0, :, GLA_QK_W:2 * GLA_QK_W]
    qd = q * jnp.exp(bc - b_mid)
    kd = (k * jnp.exp(b_mid - bc)).astype(BF16)
    kl = k * jnp.exp(b_last - bc)
    qb = q * jnp.exp(bc)
    dec = jnp.exp(b_last)
    lane = lax.broadcasted_iota(jnp.int32, (1, GLA_QK_W), 1)
    heads = [(lane >= h * GLA_DK) & (lane < (h + 1) * GLA_DK) for h in range(GLA_HEADS)]
    qd_h = [jnp.where(m, qd, 0.0).astype(BF16) for m in heads]
    qb_h = [jnp.where(m, qb, 0.0).astype(BF16) for m in heads]
    kl_h = [jnp.where(m, kl, 0.0).astype(BF16) for m in heads]
    stack = lambda per_head, rows: jnp.concatenate([t[rows] for t in per_head], axis=0)
    stacked_row = lax.broadcasted_iota(jnp.int32, (GLA_HEADS * C, C), 0)
    causal = (stacked_row & (C - 1)) >= lax.broadcasted_iota(jnp.int32, (GLA_HEADS * C, C), 1)
    og = og_ref[...]

    st = st_ref[...]
    for c in range(tg // C):
        rows = slice(c * C, (c + 1) * C)
        v = v_ref[0, rows, :].astype(BF16)
        s = lax.dot_general(stack(qd_h, rows), kd[rows], NT_DIMS, preferred_element_type=F32)
        s = jnp.where(causal, s, 0.0).astype(BF16)
        o_intra = jnp.dot(s, v, preferred_element_type=F32)
        o_inter = lax.dot_general(stack(qb_h, rows), st.astype(BF16), NT_DIMS, preferred_element_type=F32)
        v_stack = jnp.concatenate([v[:, h * GLA_DV:(h + 1) * GLA_DV] for h in range(GLA_HEADS)], axis=0)
        st = st * dec[c * C:c * C + 1] + lax.dot_general(v_stack, stack(kl_h, rows), TN_DIMS, preferred_element_type=F32)
        for h in range(GLA_HEADS):
            hrows = slice(h * C, (h + 1) * C)
            vcols = slice(h * GLA_DV, (h + 1) * GLA_DV)
            o = o_intra[hrows, vcols] + o_inter[hrows]
            on = o * lax.rsqrt(jnp.mean(o * o, axis=-1, keepdims=True) + NORM_EPS) * og
            o_ref[0, rows, vcols] = on * _silu(r_ref[0, rows, vcols])
    st_ref[...] = st


def gla_mixer(qk, v, r, misc, w_gate2, b_gate, out_g):
    tg = GLA_TG
    w2 = jnp.zeros((LANES, GLA_QK_W), F32).at[:GLA_GATE_RANK].set(w_gate2)
    row = lambda b, i: (b, i, 0)
    const = lambda b, i: (0, 0)
    return pl.pallas_call(
        _gla_kernel,
        grid=(BATCH, SEQ // tg),
        in_specs=[
            pl.BlockSpec((1, tg, 2 * GLA_QK_W), row),
            pl.BlockSpec((1, tg, GLA_V_W), row),
            pl.BlockSpec((1, tg, GLA_V_W), row),
            pl.BlockSpec((1, tg, LANES), row),
            pl.BlockSpec((LANES, GLA_QK_W), const),
            pl.BlockSpec((1, GLA_QK_W), const),
            pl.BlockSpec((1, GLA_DV), const),
            pl.BlockSpec((3 * tg, 2 * tg), const),
        ],
        out_specs=pl.BlockSpec((1, tg, GLA_V_W), row),
        out_shape=jax.ShapeDtypeStruct((BATCH, SEQ, GLA_V_W), F32),
        scratch_shapes=[pltpu.VMEM((GLA_DV, GLA_QK_W), F32)],
        compiler_params=_cparams(("arbitrary", "arbitrary")),
        name="gla_mixer",
    )(qk, v, r, misc, w2, b_gate.reshape(1, GLA_QK_W), out_g.reshape(1, GLA_DV), _gla_chunk_sums())


POS_SIDE = 128


def _rope_table_kernel(freq_ref, pos_ref, cos_ref, sin_ref):
    pos = pos_ref[...].astype(F32)
    for f in range(ROT_HALF):
        ang = pos * freq_ref[f]
        cos_ref[f] = jnp.cos(ang)
        sin_ref[f] = jnp.sin(ang)


def rope_tables(positions):
    inv_freq = jnp.float32(ROPE_THETA) ** (-jnp.arange(ROT_HALF, dtype=F32) / ROT_HALF)
    shp = jax.ShapeDtypeStruct((ROT_HALF, POS_SIDE, POS_SIDE), F32)
    cos, sin = pl.pallas_call(
        _rope_table_kernel,
        in_specs=[pl.BlockSpec(memory_space=pltpu.SMEM), pl.BlockSpec(memory_space=pltpu.VMEM)],
        out_specs=[pl.BlockSpec(memory_space=pltpu.VMEM)] * 2,
        out_shape=[shp, shp],
        name="rope_tables",
    )(inv_freq, positions.reshape(POS_SIDE, POS_SIDE))
    return jnp.concatenate([cos, sin], axis=0).reshape(ROT_DIM, N_TOK).T.reshape(BATCH, SEQ, ROT_DIM)


def _rope_placement():
    place = np.zeros((ROT_DIM, 3 * LANES), np.float32)
    const = np.zeros((1, 3 * LANES), np.float32)
    for lane in range(LANES):
        i = lane % NSA_DH
        if i < ROT_HALF:
            place[i, lane] = 1.0
            place[ROT_HALF + i, LANES + lane] = -1.0
        elif i < ROT_DIM:
            place[i - ROT_HALF, lane] = 1.0
            place[i, 2 * LANES + lane] = 1.0
        else:
            const[0, lane] = 1.0
    return jnp.asarray(place), jnp.asarray(const)


def _lane_tables(cs, place_ref, const_ref):
    tab = jnp.dot(cs, place_ref[...], precision=HI, preferred_element_type=F32) + const_ref[...]
    return tab[:, 0:LANES], tab[:, LANES:2 * LANES], tab[:, 2 * LANES:3 * LANES]


def _block_diag_ones2(width):
    h = np.arange(width) // NSA_DH
    bd = (h[:, None] == h[None, :]).astype(np.float32)
    return jnp.asarray(np.concatenate([bd, bd], axis=0), BF16)


def _head_norm_rope(x, gain, bd2, c, sm, sp):
    width = x.shape[-1]
    reps = width // LANES
    sq = x * x
    sq_hi = sq.astype(BF16)
    sq_lo = (sq - sq_hi.astype(F32)).astype(BF16)
    ss = jnp.dot(jnp.concatenate([sq_hi, sq_lo], axis=1), bd2, preferred_element_type=F32)
    y = x * lax.rsqrt(ss * (1.0 / NSA_DH) + NORM_EPS) * gain
    tile = lambda t: jnp.concatenate([t] * reps, axis=1) if reps > 1 else t
    return (y * tile(c) + pltpu.roll(y, width - ROT_HALF, 1) * tile(sm) + pltpu.roll(y, ROT_HALF, 1) * tile(sp))


PREP_TM = 512


def _prep_kernel(q_ref, ks_ref, kw_ref, cs_ref, place_ref, const_ref, gq_ref, gk_ref, bdq_ref, bdk_ref,
                 qo_ref, kso_ref, kwo_ref):
    c, sm, sp = _lane_tables(cs_ref[0], place_ref, const_ref)
    bdk = bdk_ref[...]
    q = _head_norm_rope(q_ref[0], gq_ref[...], bdq_ref[...], c, sm, sp) * (NSA_DH ** -0.5)
    qo_ref[0] = q.T.reshape(NSA_HEADS, NSA_DH, PREP_TM)
    kso_ref[0] = _head_norm_rope(ks_ref[0], gk_ref[0:1, :], bdk, c, sm, sp)
    kwo_ref[0] = _head_norm_rope(kw_ref[0], gk_ref[1:2, :], bdk, c, sm, sp)


def nsa_prep(nq, nkv, cs, q_gain, k_gain):
    tm = PREP_TM
    row = lambda b, i: (b, i, 0)
    const = lambda b, i: (0, 0)
    gq = jnp.tile(q_gain, NSA_HEADS).reshape(1, NSA_Q_W)
    gk = jnp.stack([jnp.tile(k_gain[1], NSA_KV_GROUPS), jnp.tile(k_gain[2], NSA_KV_GROUPS)])
    return pl.pallas_call(
        _prep_kernel,
        grid=(BATCH, SEQ // tm),
        in_specs=[
            pl.BlockSpec((1, tm, NSA_Q_W), row),
            pl.BlockSpec((1, tm, NSA_KV_W), lambda b, i: (b, i, 2)),
            pl.BlockSpec((1, tm, NSA_KV_W), lambda b, i: (b, i, 4)),
            pl.BlockSpec((1, tm, ROT_DIM), row),
            pl.BlockSpec((ROT_DIM, 3 * LANES), const),
            pl.BlockSpec((1, 3 * LANES), const),
            pl.BlockSpec((1, NSA_Q_W), const),
            pl.BlockSpec((2, NSA_KV_W), const),
            pl.BlockSpec((2 * NSA_Q_W, NSA_Q_W), const),
            pl.BlockSpec((2 * NSA_KV_W, NSA_KV_W), const),
        ],
        out_specs=[pl.BlockSpec((1, NSA_HEADS, NSA_DH, tm), lambda b, i: (b, 0, 0, i)),
                   pl.BlockSpec((1, tm, NSA_KV_W), row), pl.BlockSpec((1, tm, NSA_KV_W), row)],
        out_shape=[jax.ShapeDtypeStruct((BATCH, NSA_HEADS, NSA_DH, SEQ), F32),
                   jax.ShapeDtypeStruct((BATCH, SEQ, NSA_KV_W), F32), jax.ShapeDtypeStruct((BATCH, SEQ, NSA_KV_W), F32)],
        compiler_params=_cparams(("arbitrary", "arbitrary")),
        name="nsa_prep",
    )(nq, nkv, nkv, cs, *_rope_placement(), gq, gk, _block_diag_ones2(NSA_Q_W), _block_diag_ones2(NSA_KV_W))


SEG_W = CMP_STRIDE * NSA_DH


def _cmp_kernel(xk_ref, xv_ref, pe_ref, w1_ref, w2_ref, gain_ref, cs_ref, place_ref, const_ref, bd_ref, ko_ref, vo_ref):
    def compress(x_ref, kv):
        out = jnp.zeros((N_CMP_PAD, LANES), F32)
        for g in range(NSA_KV_GROUPS):
            x = x_ref[0, g]
            ha = jnp.dot(x + pe_ref[kv, 0], w1_ref[kv, 0:SEG_W, :], precision=HI, preferred_element_type=F32)
            hb = jnp.dot(x + pe_ref[kv, 1], w1_ref[kv, SEG_W:2 * SEG_W, :], precision=HI, preferred_element_type=F32)
            hid = ha + pltpu.roll(hb, N_CMP_PAD - 1, 0)
            out += jnp.dot(jax.nn.gelu(hid), w2_ref[kv, g], precision=HI, preferred_element_type=F32)
        return out

    c, sm, sp = _lane_tables(cs_ref[0], place_ref, const_ref)
    ko_ref[0] = _head_norm_rope(compress(xk_ref, 0), gain_ref[...], bd_ref[...], c, sm, sp)
    vo_ref[0] = compress(xv_ref, 1)


def nsa_compress(xk, xv, cmp_pe, cmp_w1, cmp_w2, k_gain0, cs_last):
    pe = cmp_pe.reshape(2, 2, 1, SEG_W)
    w2 = jnp.zeros((2, NSA_KV_GROUPS, CMP_HIDDEN, LANES), F32)
    for g in range(NSA_KV_GROUPS):
        w2 = w2.at[:, g, :, g * NSA_DH:(g + 1) * NSA_DH].set(cmp_w2)
    seg = pl.BlockSpec((1, NSA_KV_GROUPS, N_CMP_PAD, SEG_W), lambda b: (b, 0, 0, 0))
    tab = pl.BlockSpec((1, N_CMP_PAD, LANES), lambda b: (b, 0, 0))
    full = lambda shape: pl.BlockSpec(shape, lambda b: (0,) * len(shape))
    return pl.pallas_call(
        _cmp_kernel,
        grid=(BATCH,),
        in_specs=[seg, seg, full((2, 2, 1, SEG_W)), full((2, 2 * SEG_W, CMP_HIDDEN)),
                  full((2, NSA_KV_GROUPS, CMP_HIDDEN, LANES)), full((1, LANES)),
                  pl.BlockSpec((1, N_CMP_PAD, ROT_DIM), lambda b: (b, 0, 0)), full((ROT_DIM, 3 * LANES)),
                  full((1, 3 * LANES)), full((2 * LANES, LANES))],
        out_specs=[tab, tab],
        out_shape=[jax.ShapeDtypeStruct((BATCH, N_CMP_PAD, LANES), F32)] * 2,
        compiler_params=_cparams(("arbitrary",)),
        name="nsa_compress",
    )(xk, xv, pe, cmp_w1, w2, jnp.tile(k_gain0, NSA_KV_GROUPS).reshape(1, LANES), cs_last, *_rope_placement(),
      _block_diag_ones2(LANES))


CA_TQ = 256
SUBLANES = 8


CA_COLS = NSA_HPG * CA_TQ
CMP_PER_SEL = SEL_BLOCK // CMP_STRIDE


def split3_keys(k):
    hi = k.astype(BF16)
    lo = (k - hi.astype(F32)).astype(BF16)
    return jnp.concatenate([hi, lo, hi], axis=-1)


def _top_k_rows(score, k):
    rows, cols = score.shape
    row = lax.broadcasted_iota(jnp.int32, (rows, cols), 0).astype(F32)
    taken = jnp.zeros((rows, cols), F32)
    left = score
    for _ in range(k):
        top = jnp.max(left, axis=0, keepdims=True)
        first = jnp.min(jnp.where(left == top, row, float(rows)), axis=0, keepdims=True)
        hit = row == first
        taken = jnp.where(hit, 1.0, taken)
        left = jnp.where(hit, -jnp.inf, left)
    return taken


def _cattn_kernel(q_ref, kc_ref, vct_ref, gl_ref, o_ref, sel_ref, q3_ref, ps_ref):
    tq = CA_TQ
    q0 = pl.program_id(1) * tq
    lanes4 = lambda t: jnp.concatenate([t] * NSA_HPG, axis=1)
    cend = lax.broadcasted_iota(jnp.int32, (N_CMP_PAD, tq), 0) * CMP_STRIDE + (CMP_LEN - 1)
    tc = q0 + lax.broadcasted_iota(jnp.int32, (N_CMP_PAD, tq), 1)
    cmask = lanes4(cend <= tc)
    jj = lax.broadcasted_iota(jnp.int32, (N_SEL, tq), 0)
    tt = q0 + lax.broadcasted_iota(jnp.int32, (N_SEL, tq), 1)
    cur = jnp.right_shift(tt, 6)
    forced = (jj == 0) | (jj == cur) | (jj == cur - 1)
    valid = jj * SEL_BLOCK <= tt

    for g in range(NSA_KV_GROUPS):
        heads = range(g * NSA_HPG, (g + 1) * NSA_HPG)
        for n, h in enumerate(heads):
            q = q_ref[0, h]
            hi = q.astype(BF16)
            lo = (q - hi.astype(F32)).astype(BF16)
            for t, part in enumerate((hi, hi, lo)):
                q3_ref[g, t * NSA_DH:(t + 1) * NSA_DH, n * tq:(n + 1) * tq] = part
        s = jnp.dot(kc_ref[0, g], q3_ref[g], preferred_element_type=F32)
        s = jnp.where(cmask, s, NEG_INF)
        m = jnp.max(s, axis=0, keepdims=True)
        e = jnp.where(cmask, jnp.exp(s - m), 0.0)
        l = jnp.sum(e, axis=0, keepdims=True)
        p = e / jnp.where(l > 0.0, l, 1.0)
        gate = jnp.concatenate([jax.nn.sigmoid(gl_ref[0, h, 0:1, :]) for h in heads], axis=1)
        o = jnp.dot(vct_ref[0, g], p.astype(BF16), preferred_element_type=F32) * gate
        for n, h in enumerate(heads):
            o_ref[0, h] = o[:, n * tq:(n + 1) * tq]
        psum = functools.reduce(jnp.add, [p[:, n * tq:(n + 1) * tq] for n in range(NSA_HPG)])
        for n in range(tq // LANES):
            ps_ref[g, n] = psum[:, n * LANES:(n + 1) * LANES]

        every4th = lambda r: jnp.concatenate(
            [ps_ref[g, n, pl.ds(r, N_SEL, stride=CMP_PER_SEL), :] for n in range(tq // LANES)], axis=1)
        starts_in = [every4th(r) for r in range(CMP_PER_SEL)]
        from_prev = jnp.where(jj >= 1, pltpu.roll(starts_in[CMP_PER_SEL - 1], 1, 0), 0.0)
        imp = functools.reduce(jnp.add, starts_in) + from_prev
        score = jnp.where(valid, imp + jnp.where(forced, FORCE_BONUS, 0.0), NEG_INF)
        sel_ref[0, g] = jnp.where(valid, _top_k_rows(score, SEL_TOPK), 0.0)


def nsa_cmp_attn(q_t, kcmp, vcmp_t, gl_t):
    tq = CA_TQ
    G = NSA_KV_GROUPS
    return pl.pallas_call(
        _cattn_kernel,
        grid=(BATCH, SEQ // tq),
        in_specs=[
            pl.BlockSpec((1, NSA_HEADS, NSA_DH, tq), lambda b, i: (b, 0, 0, i)),
            pl.BlockSpec((1, G, N_CMP_PAD, 3 * NSA_DH), lambda b, i: (b, 0, 0, 0)),
            pl.BlockSpec((1, G, NSA_DH, N_CMP_PAD), lambda b, i: (b, 0, 0, 0)),
            pl.BlockSpec((1, NSA_HEADS, 3, tq), lambda b, i: (b, 0, 0, i)),
        ],
        out_specs=[pl.BlockSpec((1, NSA_HEADS, NSA_DH, tq), lambda b, i: (b, 0, 0, i)),
                   pl.BlockSpec((1, G, N_SEL, tq), lambda b, i: (b, 0, 0, i))],
        out_shape=[jax.ShapeDtypeStruct((BATCH, NSA_HEADS, NSA_DH, SEQ), F32),
                   jax.ShapeDtypeStruct((BATCH, G, N_SEL, SEQ), F32)],
        scratch_shapes=[pltpu.VMEM((G, 3 * NSA_DH, CA_COLS), BF16), pltpu.VMEM((G, tq // LANES, N_CMP_PAD, LANES), F32)],
        compiler_params=_cparams(("arbitrary", "arbitrary")),
        name="nsa_cmp_attn",
    )(q_t, split3_keys(kcmp), vcmp_t.astype(BF16), gl_t)


SA_TQ = 256
SA_TK = 1024
SA_PARTS = 2
SA_PART = SA_TK // SA_PARTS
M_INIT = -1e20


SA_COLS = NSA_HPG * SA_TQ
SA_BLOCKS = SA_TK // SEL_BLOCK


VT_ROWS = NSA_DH + 16


def value_slab_t(v_t):
    ones = jnp.ones(v_t.shape[:2] + (1, SEQ), BF16)
    zeros = jnp.zeros(v_t.shape[:2] + (VT_ROWS - NSA_DH - 1, SEQ), BF16)
    return jnp.concatenate([v_t.astype(BF16), ones, zeros], axis=2)


def sel_key_slab(ksel):
    blk = (np.arange(SEQ) % SA_TK) // SEL_BLOCK
    onehot = (blk[:, None] == np.arange(LANES - NSA_DH)[None, :]).astype(np.float32)
    onehot = jnp.broadcast_to(jnp.asarray(onehot, BF16), ksel.shape[:3] + (LANES - NSA_DH,))
    return jnp.concatenate([ksel.astype(BF16), onehot], axis=-1)


def _sattn_kernel(q_ref, k_ref, vt_ref, sel_ref, gl_ref, prev_ref, o_ref, qa_ref, acc_ref, s_ref, m_ref):
    tq, tk = SA_TQ, SA_TK
    i = pl.program_id(1)
    groups = range(NSA_KV_GROUPS)
    slots = range(2)
    for g in groups:
        for h in range(NSA_HPG):
            q = q_ref[0, g * NSA_HPG + h].astype(BF16)
            for slot in slots:
                qa_ref[slot, g, 0:NSA_DH, h * tq:(h + 1) * tq] = q
        for slot in slots:
            qa_ref[slot, g, NSA_DH:LANES, :] = jnp.zeros((LANES - NSA_DH, SA_COLS), BF16)
    acc_ref[...] = jnp.zeros_like(acc_ref)
    lanes4 = lambda t: jnp.concatenate([t] * NSA_HPG, axis=1)
    part_keys = lambda kt, part: pl.ds(pl.multiple_of(kt * tk + part * SA_PART, SA_PART), SA_PART)

    def scores(kt, slot):
        for g in groups:
            selrows = sel_ref[0, g, pl.ds(pl.multiple_of(kt * SA_BLOCKS, SA_BLOCKS), SA_BLOCKS), :]
            qa_ref[slot, g, NSA_DH:NSA_DH + SA_BLOCKS, :] = lanes4(jnp.where(selrows > 0.5, 0.0, NEG_INF)).astype(BF16)
            for part in range(SA_PARTS):
                s = jnp.dot(k_ref[0, g, part_keys(kt, part), :], qa_ref[slot, g], preferred_element_type=F32)
                s_ref[slot, g, part] = s.astype(BF16)

    def absorb(kt, slot, ms):
        out = []
        for g in groups:
            ss = [s_ref[slot, g, part] for part in range(SA_PARTS)]
            m_tile = functools.reduce(jnp.maximum, [jnp.max(s, axis=0, keepdims=True) for s in ss])
            m_new = jnp.maximum(ms[g], m_tile.astype(F32))
            acc = jnp.exp(ms[g] - m_new) * acc_ref[g]
            for part in range(SA_PARTS):
                p = jnp.exp(ss[part] - m_new.astype(BF16))
                acc += jnp.dot(vt_ref[0, g, :, part_keys(kt, part)], p, preferred_element_type=F32)
            acc_ref[g] = acc
            out.append(m_new)
        return tuple(out)

    def two_tiles(j, ms):
        kt = 2 * j
        scores(kt + 1, 1)
        ms = absorb(kt, 0, ms)
        scores(kt + 2, 0)
        return absorb(kt + 1, 1, ms)

    n_full = (i * tq) // tk
    scores(0, 0)
    m0 = tuple(jnp.full((1, SA_COLS), M_INIT, F32) for _ in groups)
    ms = lax.fori_loop(0, n_full // 2, two_tiles, m0)
    for g in groups:
        m_ref[g] = ms[g]

    def last_tile(slot):
        start = i * tq - n_full * tk
        part, row0 = start // SA_PART, pl.multiple_of(start % SA_PART, tq)
        tri = lax.broadcasted_iota(jnp.int32, (tq, tq), 0) <= lax.broadcasted_iota(jnp.int32, (tq, tq), 1)
        bias = lanes4(jnp.where(tri, 0.0, NEG_INF)).astype(BF16)
        for g in groups:
            s_ref[slot, g, part, pl.ds(row0, tq), :] += bias
        for g, m in enumerate(absorb(n_full, slot, tuple(m_ref[g] for g in groups))):
            m_ref[g] = m

    @pl.when(n_full % 2 == 0)
    def _():
        last_tile(0)

    @pl.when(n_full % 2 == 1)
    def _():
        scores(n_full, 1)
        for g, m in enumerate(absorb(n_full - 1, 0, tuple(m_ref[g] for g in groups))):
            m_ref[g] = m
        last_tile(1)

    for g in groups:
        heads = range(g * NSA_HPG, (g + 1) * NSA_HPG)
        gate = jnp.concatenate([jax.nn.sigmoid(gl_ref[0, h, 1:2, :]) for h in heads], axis=1)
        out = acc_ref[g, 0:NSA_DH, :] / acc_ref[g, NSA_DH:NSA_DH + 1, :] * gate
        for n, h in enumerate(heads):
            o_ref[0, h] = prev_ref[0, h] + out[:, n * tq:(n + 1) * tq]


def nsa_sel_attn(q_t, k_slab, vsel_t, sel_t, gl_t, prev):
    tq = SA_TQ
    G = NSA_KV_GROUPS
    ospec = pl.BlockSpec((1, NSA_HEADS, NSA_DH, tq), lambda b, i: (b, 0, 0, i))
    return pl.pallas_call(
        _sattn_kernel,
        grid=(BATCH, SEQ // tq),
        in_specs=[
            ospec,
            pl.BlockSpec((1, G, SEQ, LANES), lambda b, i: (b, 0, 0, 0)),
            pl.BlockSpec((1, G, VT_ROWS, SEQ), lambda b, i: (b, 0, 0, 0)),
            pl.BlockSpec((1, G, N_SEL, tq), lambda b, i: (b, 0, 0, i)),
            pl.BlockSpec((1, NSA_HEADS, 3, tq), lambda b, i: (b, 0, 0, i)),
            ospec,
        ],
        out_specs=ospec,
        out_shape=jax.ShapeDtypeStruct((BATCH, NSA_HEADS, NSA_DH, SEQ), F32),
        scratch_shapes=[pltpu.VMEM((2, G, LANES, SA_COLS), BF16), pltpu.VMEM((G, VT_ROWS, SA_COLS), F32),
                        pltpu.VMEM((2, G, SA_PARTS, SA_PART, SA_COLS), BF16), pltpu.VMEM((G, 1, SA_COLS), F32)],
        input_output_aliases={5: 0},
        compiler_params=_cparams(("arbitrary", "arbitrary")),
        name="nsa_sel_attn",
    )(q_t, k_slab, vsel_t, sel_t, gl_t, prev)


WA_TQ = 256
WA_TILES = WINDOW // WA_TQ + 1


def _window_bias():
    kl = np.arange(WA_TILES * WA_TQ)[:, None]
    ql = np.arange(WA_TQ)[None, :]
    diff = ql - kl + WINDOW
    return jnp.asarray(np.where((diff >= 0) & (diff < WINDOW), 0.0, NEG_INF).astype(np.float32))


def _wattn_kernel(q_ref, k0_ref, k1_ref, k2_ref, v0_ref, v1_ref, v2_ref, bias_ref, gl_ref, prev_ref, o_ref):
    tq = WA_TQ
    i = pl.program_id(1)
    k_refs = (k0_ref, k1_ref, k2_ref)
    v_refs = (v0_ref, v1_ref, v2_ref)
    lanes4 = lambda t: jnp.concatenate([t] * NSA_HPG, axis=1)
    biases = []
    for d in range(WA_TILES):
        in_seq = i - (WA_TILES - 1) + d >= 0
        biases.append(lanes4(jnp.where(in_seq, bias_ref[d * tq:(d + 1) * tq, :], NEG_INF)))
    for g in range(NSA_KV_GROUPS):
        heads = range(g * NSA_HPG, (g + 1) * NSA_HPG)
        q = jnp.concatenate([q_ref[0, h] for h in heads], axis=1).astype(BF16)
        ss = [(jnp.dot(k_refs[d][0, g], q, preferred_element_type=F32) + biases[d]).astype(BF16) for d in range(WA_TILES)]
        m = functools.reduce(jnp.maximum, [jnp.max(s, axis=0, keepdims=True) for s in ss])
        acc = functools.reduce(jnp.add, [jnp.dot(v_refs[d][0, g], jnp.exp(ss[d] - m), preferred_element_type=F32)
                                         for d in range(WA_TILES)])
        gate = jnp.concatenate([jax.nn.sigmoid(gl_ref[0, h, 2:3, :]) for h in heads], axis=1)
        out = acc[0:NSA_DH] / acc[NSA_DH:NSA_DH + 1] * gate
        for n, h in enumerate(heads):
            o_ref[0, h] = prev_ref[0, h] + out[:, n * tq:(n + 1) * tq]


def nsa_win_attn(q_t, kwin, vwin_t, gl_t, prev):
    tq = WA_TQ
    G = NSA_KV_GROUPS
    qspec = pl.BlockSpec((1, NSA_HEADS, NSA_DH, tq), lambda b, i: (b, 0, 0, i))
    tile = lambda d: (lambda i: jnp.maximum(i - (WA_TILES - 1) + d, 0))
    kspec = lambda d: pl.BlockSpec((1, G, tq, NSA_DH), lambda b, i: (b, 0, tile(d)(i), 0))
    vspec = lambda d: pl.BlockSpec((1, G, VT_ROWS, tq), lambda b, i: (b, 0, 0, tile(d)(i)))
    return pl.pallas_call(
        _wattn_kernel,
        grid=(BATCH, SEQ // tq),
        in_specs=[qspec] + [kspec(d) for d in range(WA_TILES)] + [vspec(d) for d in range(WA_TILES)] + [
            pl.BlockSpec((WA_TILES * tq, tq), lambda b, i: (0, 0)),
            pl.BlockSpec((1, NSA_HEADS, 3, tq), lambda b, i: (b, 0, 0, i)),
            qspec,
        ],
        out_specs=qspec,
        out_shape=jax.ShapeDtypeStruct((BATCH, NSA_HEADS, NSA_DH, SEQ), F32),
        input_output_aliases={2 * WA_TILES + 3: 0},
        compiler_params=_cparams(("arbitrary", "arbitrary")),
        name="nsa_win_attn",
    )(q_t, *([kwin] * WA_TILES), *([vwin_t] * WA_TILES), _window_bias(), gl_t, prev)


def nsa_mixer(nq, nkv, misc, positions, q_gain, k_gain, cmp_pe, cmp_w1, cmp_w2):
    cs = rope_tables(positions)
    q_t, ks_r, kw_r = nsa_prep(nq, nkv, cs, q_gain, k_gain)
    group_major = lambda t: t.reshape(BATCH, SEQ, NSA_KV_GROUPS, NSA_DH).transpose(0, 2, 1, 3)
    group_major_t = lambda t: t.reshape(BATCH, SEQ, NSA_KV_GROUPS, NSA_DH).transpose(0, 2, 3, 1)
    col = lambda n: nkv[..., n * NSA_KV_W:(n + 1) * NSA_KV_W]
    segs = lambda t: group_major(t).reshape(BATCH, NSA_KV_GROUPS, N_CMP_PAD, SEG_W)
    last = jnp.minimum(jnp.arange(N_CMP_PAD) * CMP_STRIDE + CMP_LEN - 1, SEQ - 1)
    kcmp, vcmp = nsa_compress(segs(col(0)), segs(col(1)), cmp_pe, cmp_w1, cmp_w2, k_gain[0], cs[:, last])
    kcmp = kcmp.reshape(BATCH, N_CMP_PAD, NSA_KV_GROUPS, NSA_DH).transpose(0, 2, 1, 3)
    vcmp_t = vcmp.reshape(BATCH, N_CMP_PAD, NSA_KV_GROUPS, NSA_DH).transpose(0, 2, 3, 1)
    gl_t = misc[..., GLA_GATE_RANK:GLA_GATE_RANK + NSA_HEADS * 3].reshape(BATCH, SEQ, NSA_HEADS, 3).transpose(0, 2, 3, 1)
    o_t, sel_t = nsa_cmp_attn(q_t, kcmp, vcmp_t, gl_t)
    o_t = nsa_sel_attn(q_t, sel_key_slab(group_major(ks_r)), value_slab_t(group_major_t(col(3))), sel_t, gl_t, o_t)
    o_t = nsa_win_attn(q_t, group_major(kw_r).astype(BF16), value_slab_t(group_major_t(col(5))), gl_t, o_t)
    return o_t


ROUTE_ROWS = 8
HX_W = D_MODEL + 3 * LANES


def _top2_sum(a, b, c, d):
    hi1, lo1 = jnp.maximum(a, b), jnp.minimum(a, b)
    hi2, lo2 = jnp.maximum(c, d), jnp.minimum(c, d)
    return jnp.maximum(hi1, hi2) + jnp.maximum(jnp.minimum(hi1, hi2), jnp.maximum(lo1, lo2))


def _moe_prenorm_route(xn, g_ref, sh_ref, sc_ref, wr_ref, rb_ref, hx_ref, route_ref):
    h = _rms_mod(xn, g_ref[...], sh_ref[0], sc_ref[0])
    logits = jnp.dot(h, wr_ref[...], precision=HI, preferred_element_type=F32).T[0:N_EXPERTS]
    scores = jax.nn.sigmoid(logits)
    sel = scores + rb_ref[...]
    epg = EXPERTS_PER_GROUP
    srow = lambda e: sel[e:e + 1, :]
    grp = [_top2_sum(*[srow(epg * g + r) for r in range(epg)]) for g in range(N_EXPERT_GROUPS)]
    best, gi = grp[0], jnp.zeros_like(grp[0], dtype=jnp.int32)
    for g in range(1, N_EXPERT_GROUPS):
        better = grp[g] > best
        gi = jnp.where(better, g, gi)
        best = jnp.where(better, grp[g], best)

    def in_group(mat, r):
        out = mat[r:r + 1, :]
        for g in range(1, N_EXPERT_GROUPS):
            out = jnp.where(gi == g, mat[epg * g + r:epg * g + r + 1, :], out)
        return out

    v = [in_group(sel, r) for r in range(epg)]
    sc = [in_group(scores, r) for r in range(epg)]
    b1, i1, w1 = v[0], jnp.zeros_like(gi), sc[0]
    for r in range(1, epg):
        better = v[r] > b1
        i1 = jnp.where(better, r, i1)
        w1 = jnp.where(better, sc[r], w1)
        b1 = jnp.where(better, v[r], b1)
    b2 = jnp.full_like(b1, -3e38)
    i2, w2 = jnp.zeros_like(gi), jnp.zeros_like(w1)
    for r in range(epg):
        better = (i1 != r) & (v[r] > b2)
        i2 = jnp.where(better, r, i2)
        w2 = jnp.where(better, sc[r], w2)
        b2 = jnp.where(better, v[r], b2)
    tot = w1 + w2
    w1, w2 = w1 / tot, w2 / tot
    zero = jnp.zeros_like(w1)
    route_ref[0] = jnp.concatenate([gi.astype(F32)] + [zero] * (ROUTE_ROWS - 1), axis=0)
    wrows = [jnp.where(i1 == r, w1, jnp.where(i2 == r, w2, 0.0)) for r in range(epg)]
    wmat = jnp.concatenate(wrows + [jnp.zeros((LANES - epg, w1.shape[1]), F32)], axis=0).T
    w_hi = wmat.astype(BF16)
    rest = wmat - w_hi.astype(F32)
    w_mid = rest.astype(BF16)
    w_lo = (rest - w_mid.astype(F32)).astype(BF16)
    hx_ref[0, :, 0:D_MODEL] = h.astype(BF16)
    for n, part in enumerate((w_hi, w_mid, w_lo)):
        hx_ref[0, :, D_MODEL + n * LANES:D_MODEL + (n + 1) * LANES] = part


def _route_specs(tm, row, vec, const):
    in_specs = [pl.BlockSpec((1, D_MODEL), const), pl.BlockSpec((1, 1, D_MODEL), vec), pl.BlockSpec((1, 1, D_MODEL), vec),
                pl.BlockSpec((D_MODEL, LANES), const), pl.BlockSpec((N_EXPERTS, 1), const)]
    out_specs = [pl.BlockSpec((1, tm, HX_W), row), pl.BlockSpec((1, ROUTE_ROWS, tm), lambda b, i: (b, 0, i))]
    out_shape = [jax.ShapeDtypeStruct((BATCH, SEQ, HX_W), BF16), jax.ShapeDtypeStruct((BATCH, ROUTE_ROWS, SEQ), F32)]
    return in_specs, out_specs, out_shape


def _route_args(g, shift, scale, w_router, router_bias):
    return (g.reshape(1, D_MODEL), shift.reshape(BATCH, 1, D_MODEL), scale.reshape(BATCH, 1, D_MODEL),
            jnp.pad(w_router, ((0, 0), (0, LANES - N_EXPERTS))), router_bias.reshape(N_EXPERTS, 1))


OUTPROJ_TM = 512


def _outproj0_kernel(oa_ref, ob_ref, w_ref, x_ref, gate_ref, g_ref, sh_ref, sc_ref, wr_ref, rb_ref,
                     xo_ref, h_ref, route_ref):
    y = jnp.dot(oa_ref[0].astype(BF16), w_ref[0:GLA_V_W, :], preferred_element_type=F32)
    ob_t = ob_ref[0].reshape(NSA_Q_W, OUTPROJ_TM).astype(BF16)
    y += lax.dot_general(ob_t, w_ref[GLA_V_W:GLA_V_W + NSA_Q_W, :], TN_DIMS, preferred_element_type=F32)
    xn = x_ref[0] + gate_ref[0] * y
    xo_ref[0] = xn
    _moe_prenorm_route(xn, g_ref, sh_ref, sc_ref, wr_ref, rb_ref, h_ref, route_ref)


def outproj0(o_a, o_b, w_out, x, gate, route_args):
    tm = OUTPROJ_TM
    row = lambda b, i: (b, i, 0)
    vec = lambda b, i: (b, 0, 0)
    const = lambda b, i: (0, 0)
    r_in, r_out, r_shape = _route_specs(tm, row, vec, const)
    return pl.pallas_call(
        _outproj0_kernel,
        grid=(BATCH, SEQ // tm),
        in_specs=[pl.BlockSpec((1, tm, GLA_V_W), row), pl.BlockSpec((1, NSA_HEADS, NSA_DH, tm), lambda b, i: (b, 0, 0, i)),
                  pl.BlockSpec((GLA_V_W + NSA_Q_W, D_MODEL), const), pl.BlockSpec((1, tm, D_MODEL), row),
                  pl.BlockSpec((1, 1, D_MODEL), vec)] + r_in,
        out_specs=[pl.BlockSpec((1, tm, D_MODEL), row)] + r_out,
        out_shape=[jax.ShapeDtypeStruct((BATCH, SEQ, D_MODEL), F32)] + r_shape,
        compiler_params=_cparams(("arbitrary", "arbitrary")),
        name="outproj0",
    )(o_a, o_b, w_out.astype(BF16), x, gate.reshape(BATCH, 1, D_MODEL), *route_args)


GMLP_TM = 512


def _gmlp_kernel(x_ref, g1_ref, sh1_ref, sc1_ref, win_ref, ng_ref, ws_ref, bs_ref, wout_ref, gate_ref,
                 g_ref, sh_ref, sc_ref, wr_ref, rb_ref, xo_ref, h_ref, route_ref, gated_ref, v_ref):
    x = x_ref[0]
    h = _rms_mod(x, g1_ref[...], sh1_ref[0], sc1_ref[0]).astype(BF16)
    group_cols = lambda g: slice(g * SGU_GROUP_DIM, (g + 1) * SGU_GROUP_DIM)
    ssq = jnp.zeros((GMLP_TM, LANES), F32)
    for g in range(SGU_GROUPS):
        lo = SGU_WIDTH + g * SGU_GROUP_DIM
        v = jax.nn.gelu(jnp.dot(h, win_ref[:, lo:lo + SGU_GROUP_DIM], preferred_element_type=F32))
        v_ref[:, group_cols(g)] = v
        ssq += functools.reduce(jnp.add, [v[:, n * LANES:(n + 1) * LANES] ** 2 for n in range(SGU_GROUP_DIM // LANES)])
    rs = lax.rsqrt(jnp.sum(ssq, axis=-1, keepdims=True) * (1.0 / SGU_WIDTH) + NORM_EPS)
    ri = lax.broadcasted_iota(jnp.int32, (SGU_CHUNK, SGU_CHUNK), 0)
    ci = lax.broadcasted_iota(jnp.int32, (SGU_CHUNK, SGU_CHUNK), 1)
    for g in range(SGU_GROUPS):
        cols = group_cols(g)
        u = jax.nn.gelu(jnp.dot(h, win_ref[:, cols], preferred_element_type=F32))
        vn = (v_ref[:, cols] * rs * ng_ref[:, cols]).astype(BF16)
        w = jnp.where(ri >= ci, ws_ref[g], 0.0).astype(BF16)
        for c in range(GMLP_TM // SGU_CHUNK):
            rows = slice(c * SGU_CHUNK, (c + 1) * SGU_CHUNK)
            mix = jnp.dot(w, vn[rows], preferred_element_type=F32) + bs_ref[:, g:g + 1]
            gated_ref[rows, cols] = (u[rows] * mix).astype(BF16)
    y = jnp.dot(gated_ref[...], wout_ref[...], preferred_element_type=F32)
    xn = x + gate_ref[0] * y
    xo_ref[0] = xn
    _moe_prenorm_route(xn, g_ref, sh_ref, sc_ref, wr_ref, rb_ref, h_ref, route_ref)


def gmlp_layer(x, g1, shift1, scale1, w_in, norm_g, w_s, b_s, w_out, gate, route_args):
    tm = GMLP_TM
    row = lambda b, i: (b, i, 0)
    vec = lambda b, i: (b, 0, 0)
    const = lambda b, i: (0, 0)
    r_in, r_out, r_shape = _route_specs(tm, row, vec, const)
    vspec = pl.BlockSpec((1, 1, D_MODEL), vec)
    return pl.pallas_call(
        _gmlp_kernel,
        grid=(BATCH, SEQ // tm),
        in_specs=[pl.BlockSpec((1, tm, D_MODEL), row), pl.BlockSpec((1, D_MODEL), const), vspec, vspec,
                  pl.BlockSpec((D_MODEL, 2 * SGU_WIDTH), const), pl.BlockSpec((1, SGU_WIDTH), const),
                  pl.BlockSpec((SGU_GROUPS, SGU_CHUNK, SGU_CHUNK), lambda b, i: (0, 0, 0)),
                  pl.BlockSpec((SGU_CHUNK, SGU_GROUPS), const), pl.BlockSpec((SGU_WIDTH, D_MODEL), const), vspec] + r_in,
        out_specs=[pl.BlockSpec((1, tm, D_MODEL), row)] + r_out,
        out_shape=[jax.ShapeDtypeStruct((BATCH, SEQ, D_MODEL), F32)] + r_shape,
        scratch_shapes=[pltpu.VMEM((tm, SGU_WIDTH), BF16), pltpu.VMEM((tm, SGU_WIDTH), F32)],
        compiler_params=_cparams(("arbitrary", "arbitrary"), vmem_mib=56),
        name="gmlp_layer",
    )(x, g1.reshape(1, D_MODEL), shift1.reshape(BATCH, 1, D_MODEL), scale1.reshape(BATCH, 1, D_MODEL),
      w_in.astype(BF16), norm_g.reshape(1, SGU_WIDTH), w_s, b_s.T, w_out.astype(BF16),
      gate.reshape(BATCH, 1, D_MODEL), *route_args)


MOE_TM = 256
MOE_CHUNK = 512
MOE_SORTED = N_TOK + N_EXPERT_GROUPS * MOE_TM
MOE_TILES = MOE_SORTED // MOE_TM
MOE_CHUNKS = N_TOK // MOE_CHUNK
MOE_PAIRS = MOE_TILES + N_EXPERT_GROUPS * MOE_CHUNKS
FLAG_ACTIVE, FLAG_FIRST, FLAG_LAST, FLAG_ZERO = 1, 2, 4, 8
EXP_WIN = 4
CMB_WIN = 8
MOE_TSTEPS = MOE_TILES + MOE_PAIRS // EXP_WIN
MOE_CSTEPS = MOE_CHUNKS + MOE_PAIRS // CMB_WIN


def _plan_kernel(gi_ref, rank_ref, before_ref):
    gi = gi_ref[...]
    r = lax.broadcasted_iota(jnp.int32, (POS_SIDE, POS_SIDE), 0)
    c = lax.broadcasted_iota(jnp.int32, (POS_SIDE, POS_SIDE), 1)
    upper = jnp.where(r <= c, 1.0, 0.0)
    lower_strict = jnp.where(c < r, 1.0, 0.0)
    rank = jnp.zeros((POS_SIDE, POS_SIDE), F32)
    for g in range(N_EXPERT_GROUPS):
        member = jnp.where(gi == g, 1.0, 0.0)
        in_row = jnp.dot(member, upper, precision=HI, preferred_element_type=F32)
        row_total = jnp.broadcast_to(in_row[:, POS_SIDE - 1:POS_SIDE], (POS_SIDE, POS_SIDE))
        before = jnp.dot(lower_strict, row_total, precision=HI, preferred_element_type=F32)
        before_ref[g] = before
        rank += member * (before + in_row - 1.0)
    rank_ref[...] = rank


def moe_plan(route):
    tm = MOE_TM
    i32 = jnp.int32
    gi_f = route[:, 0, :].reshape(POS_SIDE, POS_SIDE)
    rank, before = pl.pallas_call(
        _plan_kernel,
        out_shape=[jax.ShapeDtypeStruct((POS_SIDE, POS_SIDE), F32),
                   jax.ShapeDtypeStruct((N_EXPERT_GROUPS, POS_SIDE, POS_SIDE), F32)],
        name="moe_plan",
    )(gi_f)
    gi = gi_f.reshape(N_TOK).astype(i32)
    groups = jnp.arange(N_EXPERT_GROUPS, dtype=i32)
    member = gi[None, :] == groups[:, None]
    tot = jnp.sum(member, axis=1).astype(i32)
    padded = (tot + tm - 1) // tm * tm
    gend = jnp.cumsum(padded).astype(i32)
    gstart = gend - padded
    pos = jnp.sum(jnp.where(member, gstart[:, None], 0), axis=0).astype(i32) + rank.reshape(N_TOK).astype(i32)
    rows_per_chunk = MOE_CHUNK // POS_SIDE
    cnt_end = jnp.concatenate([before[:, rows_per_chunk::rows_per_chunk, 0].astype(i32), tot[:, None]], axis=1)
    t = jnp.arange(MOE_TILES, dtype=i32)
    n_used = gend[-1] // tm
    tile_g = jnp.minimum(jnp.sum(gend[None, :] <= (t * tm)[:, None], axis=1), N_EXPERT_GROUPS - 1).astype(i32)
    k0 = t * tm - gstart[tile_g]
    k1 = jnp.minimum(k0 + tm, tot[tile_g]) - 1
    ce = cnt_end[tile_g]
    c_lo = jnp.sum(ce <= k0[:, None], axis=1).astype(i32)
    c_hi = jnp.sum(ce <= k1[:, None], axis=1).astype(i32)
    npairs = jnp.where(t < n_used, c_hi - c_lo + 1, 0)
    pend = jnp.cumsum(npairs).astype(i32)
    pstart = pend - npairs
    total = pend[-1]
    l = jnp.arange(MOE_PAIRS, dtype=i32)
    real = l < total
    lt = jnp.minimum(l, total - 1)

    def windows(count, win, n_steps):
        per_item = (count + win - 1) // win
        end = jnp.cumsum(per_item).astype(i32)
        start = end - per_item
        s = jnp.arange(n_steps, dtype=i32)
        real_s = s < end[-1]
        sc = jnp.minimum(s, end[-1] - 1)
        item = jnp.sum(end[None, :] <= sc[:, None], axis=1).astype(i32)
        j = sc - start[item]
        flags_s = jnp.where(real_s, FLAG_ACTIVE + jnp.where(j == 0, FLAG_FIRST, 0)
                            + jnp.where(j == per_item[item] - 1, FLAG_LAST, 0), 0).astype(i32)
        return item, j, flags_s, real_s, s - end[-1]

    tile_s, j, flags, real_s, spare = windows(npairs, EXP_WIN, MOE_TSTEPS)
    c0 = c_lo[tile_s] + EXP_WIN * j
    n_valid = jnp.minimum(EXP_WIN, c_hi[tile_s] - c0 + 1).astype(i32)
    spare_tile = jnp.minimum(n_used + spare, MOE_TILES - 1)
    flags = jnp.where(real_s, flags, jnp.where(spare_tile >= n_used, FLAG_ZERO, 0)).astype(i32)
    tile_sched = jnp.where(real_s, tile_s, spare_tile).astype(i32)
    by_tile = (tile_sched, c0.astype(i32), n_valid, flags, tile_g[tile_sched])
    cc = jnp.arange(MOE_CHUNKS, dtype=i32)
    is_pair = (cc[:, None] >= c_lo[None, :]) & (cc[:, None] <= c_hi[None, :]) & (t[None, :] < n_used)
    seen = jnp.cumsum(is_pair.reshape(-1).astype(i32))
    flat = jnp.sum(seen[None, :] <= lt[:, None], axis=1).astype(i32)
    pair_tile = flat % MOE_TILES
    per_chunk = jnp.sum(is_pair, axis=1).astype(i32)
    first_pair = jnp.cumsum(per_chunk).astype(i32) - per_chunk
    chunk_s, j, flags_c, _, _ = windows(per_chunk, CMB_WIN, MOE_CSTEPS)
    base = first_pair[chunk_s] + CMB_WIN * j
    n_valid_c = jnp.minimum(CMB_WIN, per_chunk[chunk_s] - CMB_WIN * j).astype(i32)
    tiles_c = tuple(pair_tile[jnp.minimum(base + w, total - 1)] for w in range(CMB_WIN))
    by_chunk = (chunk_s, n_valid_c, flags_c) + tiles_c
    return pos.reshape(MOE_CHUNKS, 1, MOE_CHUNK), by_tile, by_chunk


def _one_hot_rows(pos_row, tile):
    rows = tile * MOE_TM + lax.broadcasted_iota(jnp.int32, (MOE_TM, MOE_CHUNK), 0)
    return jnp.where(pos_row == rows, 1.0, 0.0).astype(BF16)


def _moe_kernel(tile_ref, c0_ref, nv_ref, flag_ref, grp_ref, *refs):
    pos_refs, hx_refs = refs[0:EXP_WIN], refs[EXP_WIN:2 * EXP_WIN]
    wg_ref, wu_ref, wd_ref, y_ref, acc_ref = refs[2 * EXP_WIN:]
    l = pl.program_id(0)
    flags = flag_ref[l]

    @pl.when((flags & FLAG_FIRST) != 0)
    def _():
        acc_ref[...] = jnp.zeros_like(acc_ref)

    for w in range(EXP_WIN):
        @pl.when(((flags & FLAG_ACTIVE) != 0) & (w < nv_ref[l]))
        def _():
            onehot = _one_hot_rows(pos_refs[w][0], tile_ref[l])
            acc_ref[...] += jnp.dot(onehot, hx_refs[w][0], preferred_element_type=F32)

    @pl.when((flags & FLAG_LAST) != 0)
    def _():
        x = acc_ref[:, 0:D_MODEL].astype(BF16)
        w = functools.reduce(jnp.add, [acc_ref[:, D_MODEL + n * LANES:D_MODEL + (n + 1) * LANES] for n in range(3)])
        y = jnp.zeros((MOE_TM, D_MODEL), F32)
        for r in range(EXPERTS_PER_GROUP):
            gate = jnp.dot(x, wg_ref[0, 0, r], preferred_element_type=F32)
            up = jnp.dot(x, wu_ref[0, 0, r], preferred_element_type=F32)
            hid = (_silu(gate) * up * w[:, r:r + 1]).astype(BF16)
            y += jnp.dot(hid, wd_ref[0, 0, r], preferred_element_type=F32)
        y_ref[...] = y

    @pl.when((flags & FLAG_ZERO) != 0)
    def _():
        y_ref[...] = jnp.zeros_like(y_ref)


def moe_experts(hx, pos, by_tile, w_gate, w_up, w_down, layer):
    grouped = lambda w: w.reshape(DEPTH, N_EXPERT_GROUPS, EXPERTS_PER_GROUP, *w.shape[2:])
    wspec = lambda k, n: pl.BlockSpec((1, 1, EXPERTS_PER_GROUP, k, n), lambda l, t, c, n_, f, g: (layer, g[l], 0, 0, 0))
    chunk = lambda w: (lambda l, t, c, n_, f, g: (jnp.minimum(c[l] + w, MOE_CHUNKS - 1), 0, 0))
    hx_chunks = hx.reshape(MOE_CHUNKS, MOE_CHUNK, HX_W)
    return pl.pallas_call(
        _moe_kernel,
        grid_spec=pltpu.PrefetchScalarGridSpec(
            num_scalar_prefetch=5,
            grid=(MOE_TSTEPS,),
            in_specs=[pl.BlockSpec((1, 1, MOE_CHUNK), chunk(w)) for w in range(EXP_WIN)]
            + [pl.BlockSpec((1, MOE_CHUNK, HX_W), chunk(w)) for w in range(EXP_WIN)]
            + [wspec(D_MODEL, EXPERT_HIDDEN), wspec(D_MODEL, EXPERT_HIDDEN), wspec(EXPERT_HIDDEN, D_MODEL)],
            out_specs=pl.BlockSpec((MOE_TM, D_MODEL), lambda l, t, c, n_, f, g: (t[l], 0)),
            scratch_shapes=[pltpu.VMEM((MOE_TM, HX_W), F32)],
        ),
        out_shape=jax.ShapeDtypeStruct((MOE_SORTED, D_MODEL), F32),
        compiler_params=_cparams(("arbitrary",), vmem_mib=56),
        name="moe_experts",
    )(*by_tile, *([pos] * EXP_WIN), *([hx_chunks] * EXP_WIN), grouped(w_gate), grouped(w_up), grouped(w_down))


def _moe_combine_kernel(chunk_ref, nv_ref, flag_ref, *refs):
    tile_refs = refs[0:CMB_WIN]
    pos_ref = refs[CMB_WIN]
    y_refs = refs[CMB_WIN + 1:2 * CMB_WIN + 1]
    x_ref, gate_ref, o_ref, acc_ref = refs[2 * CMB_WIN + 1:]
    l = pl.program_id(0)
    flags = flag_ref[l]

    @pl.when((flags & FLAG_FIRST) != 0)
    def _():
        acc_ref[...] = jnp.zeros_like(acc_ref)

    for w in range(CMB_WIN):
        @pl.when(((flags & FLAG_ACTIVE) != 0) & (w < nv_ref[l]))
        def _():
            onehot = _one_hot_rows(pos_ref[0], tile_refs[w][l])
            y = y_refs[w][...]
            y_hi = y.astype(BF16)
            y_lo = (y - y_hi.astype(F32)).astype(BF16)
            acc_ref[...] += (lax.dot_general(onehot, y_hi, TN_DIMS, preferred_element_type=F32)
                             + lax.dot_general(onehot, y_lo, TN_DIMS, preferred_element_type=F32))

    @pl.when((flags & FLAG_LAST) != 0)
    def _():
        o_ref[0] = x_ref[0] + gate_ref[0] * acc_ref[...]


def moe_combine(x, y_sorted, pos, by_chunk, gate):
    per_b = SEQ // MOE_CHUNK
    tok = lambda l, c, *_: (c[l] // per_b, c[l] % per_b, 0)
    tile = lambda w: (lambda l, c, n_, f, *tiles: (tiles[w][l], 0))
    return pl.pallas_call(
        _moe_combine_kernel,
        grid_spec=pltpu.PrefetchScalarGridSpec(
            num_scalar_prefetch=3 + CMB_WIN,
            grid=(MOE_CSTEPS,),
            in_specs=[pl.BlockSpec((1, 1, MOE_CHUNK), lambda l, c, *_: (c[l], 0, 0))]
            + [pl.BlockSpec((MOE_TM, D_MODEL), tile(w)) for w in range(CMB_WIN)]
            + [pl.BlockSpec((1, MOE_CHUNK, D_MODEL), tok),
               pl.BlockSpec((1, 1, D_MODEL), lambda l, c, *_: (c[l] // per_b, 0, 0))],
            out_specs=pl.BlockSpec((1, MOE_CHUNK, D_MODEL), tok),
            scratch_shapes=[pltpu.VMEM((MOE_CHUNK, D_MODEL), F32)],
        ),
        out_shape=jax.ShapeDtypeStruct((BATCH, SEQ, D_MODEL), F32),
        compiler_params=_cparams(("arbitrary",)),
        name="moe_combine",
    )(*by_chunk, pos, *([y_sorted] * CMB_WIN), x, gate.reshape(BATCH, 1, D_MODEL))


def moe_layer(x, hx, route, gate, w_gate, w_up, w_down, layer):
    pos, by_tile, by_chunk = moe_plan(route)
    y_sorted = moe_experts(hx, pos, by_tile, w_gate, w_up, w_down, layer)
    return moe_combine(x, y_sorted, pos, by_chunk, gate)


def kernel(x, c, positions, w_ada, b_ada, norm_g, w_in_ab, w_out_ab, gla_w_gate2, gla_b_gate, gla_norm_g, nsa_q_gain, nsa_k_gain, nsa_cmp_pe, nsa_cmp_w1, nsa_cmp_w2, w_in_c, sgu_norm_g, sgu_w_s, sgu_b_s, w_out_c, w_router, router_bias, w_gate, w_up, w_down):
    mod = ada_modulation(c, w_ada, b_ada)
    qk, gv, gr, nq, nkv, misc = inproj0(x, norm_g[0, 0], mod[0, :, 0], mod[0, :, 1], _arrange_w_in(w_in_ab[0]))
    o_a = gla_mixer(qk, gv, gr, misc, gla_w_gate2[0], gla_b_gate[0], gla_norm_g[0])
    o_b = nsa_mixer(nq, nkv, misc, positions, nsa_q_gain[0], nsa_k_gain[0], nsa_cmp_pe[0], nsa_cmp_w1[0], nsa_cmp_w2[0])
    wg, wu, wd = w_gate.astype(BF16), w_up.astype(BF16), w_down.astype(BF16)
    route_args = lambda l: _route_args(norm_g[l, 1], mod[l, :, 3], mod[l, :, 4], w_router, router_bias)
    x1, h, route = outproj0(o_a, o_b, w_out_ab[0], x, mod[0, :, 2], route_args(0))
    x2 = moe_layer(x1, h, route, mod[0, :, 5], wg, wu, wd, 0)
    x3, h, route = gmlp_layer(x2, norm_g[1, 0], mod[1, :, 0], mod[1, :, 1], w_in_c[0], sgu_norm_g[0], sgu_w_s[0],
                              sgu_b_s[0], w_out_c[0], mod[1, :, 2], route_args(1))
    return moe_layer(x3, h, route, mod[1, :, 5], wg, wu, wd, 1)
```

```python
import functools

import numpy as np
import jax
import jax.numpy as jnp
from jax import lax
from jax.experimental import pallas as pl
from jax.experimental.pallas import tpu as pltpu

D_MODEL = 1024
BATCH = 2
SEQ = 8192
DEPTH = 2
N_TOK = BATCH * SEQ

GLA_HEADS = 4
GLA_DK = 64
GLA_DV = 128
GLA_GATE_RANK = 16
GLA_TAU = 16.0
GLA_CHUNK = 64
NSA_HEADS = 8
NSA_KV_GROUPS = 2
NSA_HPG = NSA_HEADS // NSA_KV_GROUPS
NSA_DH = 64
CMP_LEN = 32
CMP_STRIDE = 16
CMP_HIDDEN = 256
SEL_BLOCK = 64
SEL_TOPK = 16
WINDOW = 512
ROPE_THETA = 500000.0
ROT_DIM = NSA_DH // 4
ROT_HALF = ROT_DIM // 2
SGU_CHUNK = 128
SGU_GROUPS = 8
SGU_WIDTH = 2048
SGU_GROUP_DIM = SGU_WIDTH // SGU_GROUPS
N_EXPERTS = 16
N_EXPERT_GROUPS = 4
EXPERTS_PER_GROUP = N_EXPERTS // N_EXPERT_GROUPS
MOE_TOPK = 2
EXPERT_HIDDEN = 512

GLA_QK_W = GLA_HEADS * GLA_DK
GLA_V_W = GLA_HEADS * GLA_DV
NSA_Q_W = NSA_HEADS * NSA_DH
NSA_KV_W = NSA_KV_GROUPS * NSA_DH
N_CMP = (SEQ - CMP_LEN) // CMP_STRIDE + 1
N_CMP_PAD = SEQ // CMP_STRIDE
N_SEL = SEQ // SEL_BLOCK

NORM_EPS = 1e-6
NEG_INF = -1e30
FORCE_BONUS = 1e4

LANES = 128
MIB = 1024 * 1024

F32 = jnp.float32
BF16 = jnp.bfloat16
HI = lax.Precision.HIGHEST
NT_DIMS = (((1,), (1,)), ((), ()))
TN_DIMS = (((0,), (0,)), ((), ()))


def _cparams(sem, vmem_mib=48):
    return pltpu.CompilerParams(dimension_semantics=sem, vmem_limit_bytes=vmem_mib * MIB)


def _rms_mod(x, g, shift, scale):
    y = x * lax.rsqrt(jnp.mean(x * x, axis=-1, keepdims=True) + NORM_EPS) * g
    return y * (1 + scale) + shift


def _silu(x):
    return x * jax.nn.sigmoid(x)


def _log_sigmoid(z):
    return jnp.minimum(z, 0.0) - jnp.log1p(jnp.exp(-jnp.abs(z)))


ADA_TN = 1536
ADA_ROWS = 8


def _ada_kernel(c_ref, w_ref, b_ref, o_ref):
    cond = _silu(c_ref[...])
    o_ref[0] = jnp.dot(cond, w_ref[0], precision=HI, preferred_element_type=F32) + b_ref[0]


def ada_modulation(c, w_ada, b_ada):
    c8 = jnp.zeros((ADA_ROWS, D_MODEL), F32).at[:BATCH].set(c)
    width = 6 * D_MODEL
    out = pl.pallas_call(
        _ada_kernel,
        grid=(DEPTH, width // ADA_TN),
        in_specs=[
            pl.BlockSpec((ADA_ROWS, D_MODEL), lambda l, j: (0, 0)),
            pl.BlockSpec((1, D_MODEL, ADA_TN), lambda l, j: (l, 0, j)),
            pl.BlockSpec((1, 1, ADA_TN), lambda l, j: (l, 0, j)),
        ],
        out_specs=pl.BlockSpec((1, ADA_ROWS, ADA_TN), lambda l, j: (l, 0, j)),
        out_shape=jax.ShapeDtypeStruct((DEPTH, ADA_ROWS, width), F32),
        compiler_params=_cparams(("arbitrary", "arbitrary")),
        name="ada_modulation",
    )(c8, w_ada, b_ada.reshape(DEPTH, 1, width))
    return out[:, :BATCH].reshape(DEPTH, BATCH, 6, D_MODEL)


INPROJ_TM = 512
INPROJ_WIDTHS = (2 * GLA_QK_W, GLA_V_W, GLA_V_W, NSA_Q_W, 6 * NSA_KV_W, LANES)


def _arrange_w_in(w_in):
    o = np.cumsum((0, GLA_QK_W, GLA_QK_W, GLA_V_W, GLA_GATE_RANK, GLA_V_W, NSA_Q_W, 6 * NSA_KV_W, NSA_HEADS * 3))
    gq_gk = w_in[:, o[0]:o[2]]
    gv = w_in[:, o[2]:o[3]]
    glr = w_in[:, o[3]:o[4]]
    gr = w_in[:, o[4]:o[5]]
    nq = w_in[:, o[5]:o[6]]
    nkv = w_in[:, o[6]:o[7]]
    ng = w_in[:, o[7]:o[8]]
    pad = jnp.zeros((D_MODEL, LANES - GLA_GATE_RANK - NSA_HEADS * 3), w_in.dtype)
    return jnp.concatenate([gq_gk, gv, gr, nq, nkv, glr, ng, pad], axis=1).astype(BF16)


def _inproj0_kernel(x_ref, g_ref, sh_ref, sc_ref, w_ref, *o_refs):
    h = _rms_mod(x_ref[0], g_ref[...], sh_ref[0], sc_ref[0]).astype(BF16)
    off = 0
    for o_ref, wd in zip(o_refs, INPROJ_WIDTHS):
        o_ref[0] = jnp.dot(h, w_ref[:, off:off + wd], preferred_element_type=F32)
        off += wd


def inproj0(x, g, shift, scale, w_arranged):
    tm = INPROJ_TM
    wtot = sum(INPROJ_WIDTHS)
    row = lambda b, i: (b, i, 0)
    vec = lambda b, i: (b, 0, 0)
    return pl.pallas_call(
        _inproj0_kernel,
        grid=(BATCH, SEQ // tm),
        in_specs=[
            pl.BlockSpec((1, tm, D_MODEL), row),
            pl.BlockSpec((1, D_MODEL), lambda b, i: (0, 0)),
            pl.BlockSpec((1, 1, D_MODEL), vec),
            pl.BlockSpec((1, 1, D_MODEL), vec),
            pl.BlockSpec((D_MODEL, wtot), lambda b, i: (0, 0)),
        ],
        out_specs=[pl.BlockSpec((1, tm, wd), row) for wd in INPROJ_WIDTHS],
        out_shape=[jax.ShapeDtypeStruct((BATCH, SEQ, wd), F32) for wd in INPROJ_WIDTHS],
        compiler_params=_cparams(("arbitrary", "arbitrary")),
        name="inproj0",
    )(x, g.reshape(1, D_MODEL), shift.reshape(BATCH, 1, D_MODEL), scale.reshape(BATCH, 1, D_MODEL), w_arranged)


GLA_TG = 512


def _gla_chunk_sums():
    i = np.arange(GLA_TG)[:, None]
    j = np.arange(GLA_TG)[None, :]
    same = (i // GLA_CHUNK) == (j // GLA_CHUNK)
    m3 = np.concatenate([same & (j <= i), same & (j % GLA_CHUNK <= GLA_CHUNK // 2), same], axis=0).astype(np.float32)
    return jnp.asarray(np.concatenate([m3, m3], axis=1), BF16)


def _gla_kernel(qk_ref, v_ref, r_ref, misc_ref, w2_ref, bg_ref, og_ref, sums_ref, o_ref, st_ref):
    C, tg = GLA_CHUNK, GLA_TG

    @pl.when(pl.program_id(0) == 0)
    def _():
        st_ref[...] = jnp.zeros_like(st_ref)

    lane = lax.broadcasted_iota(jnp.int32, (1, GLA_QK_W), 1)
    heads = [(lane >= h * GLA_DK) & (lane < (h + 1) * GLA_DK) for h in range(GLA_HEADS)]
    stack = lambda per_head, rows: jnp.concatenate([t[rows] for t in per_head], axis=0)
    stacked_row = lax.broadcasted_iota(jnp.int32, (GLA_HEADS * C, C), 0)
    causal = (stacked_row & (C - 1)) >= lax.broadcasted_iota(jnp.int32, (GLA_HEADS * C, C), 1)
    og = og_ref[...]

    def prepare(b):
        z = jnp.dot(misc_ref[b], w2_ref[...], precision=HI, preferred_element_type=F32) + bg_ref[...]
        la = _log_sigmoid(z) / GLA_TAU
        la_hi = la.astype(BF16)
        la_lo = (la - la_hi.astype(F32)).astype(BF16)
        sums = jnp.dot(sums_ref[...], jnp.concatenate([la_hi, la_lo], axis=0), preferred_element_type=F32)
        bc, b_mid, b_last = sums[0:tg], sums[tg:2 * tg], sums[2 * tg:3 * tg]
        q = qk_ref[b, :, 0:GLA_QK_W] * (GLA_DK ** -0.5)
        k = qk_ref[b, :, GLA_QK_W:2 * GLA_QK_W]
        qd = q * jnp.exp(bc - b_mid)
        kl = k * jnp.exp(b_last - bc)
        qb = q * jnp.exp(bc)
        per_head = lambda t: [jnp.where(m, t, 0.0).astype(BF16) for m in heads]
        return dict(kd=(k * jnp.exp(b_mid - bc)).astype(BF16), dec=jnp.exp(b_last), qd_h=per_head(qd),
                    qb_h=per_head(qb), kl_h=per_head(kl))

    batches = range(BATCH)
    pre = [prepare(b) for b in batches]
    st = [st_ref[b] for b in batches]
    for c in range(tg // C):
        rows = slice(c * C, (c + 1) * C)
        for b in batches:
            p = pre[b]
            v = v_ref[b, rows, :].astype(BF16)
            s = lax.dot_general(stack(p["qd_h"], rows), p["kd"][rows], NT_DIMS, preferred_element_type=F32)
            s = jnp.where(causal, s, 0.0).astype(BF16)
            o_intra = jnp.dot(s, v, preferred_element_type=F32)
            o_inter = lax.dot_general(stack(p["qb_h"], rows), st[b].astype(BF16), NT_DIMS, preferred_element_type=F32)
            v_stack = jnp.concatenate([v[:, h * GLA_DV:(h + 1) * GLA_DV] for h in range(GLA_HEADS)], axis=0)
            st[b] = st[b] * p["dec"][c * C:c * C + 1] + lax.dot_general(v_stack, stack(p["kl_h"], rows), TN_DIMS,
                                                                       preferred_element_type=F32)
            for h in range(GLA_HEADS):
                hrows = slice(h * C, (h + 1) * C)
                vcols = slice(h * GLA_DV, (h + 1) * GLA_DV)
                o = o_intra[hrows, vcols] + o_inter[hrows]
                on = o * lax.rsqrt(jnp.mean(o * o, axis=-1, keepdims=True) + NORM_EPS) * og
                o_ref[b, rows, vcols] = on * _silu(r_ref[b, rows, vcols])
    for b in batches:
        st_ref[b] = st[b]


def gla_mixer(qk, v, r, misc, w_gate2, b_gate, out_g):
    tg = GLA_TG
    w2 = jnp.zeros((LANES, GLA_QK_W), F32).at[:GLA_GATE_RANK].set(w_gate2)
    row = lambda i: (0, i, 0)
    const = lambda i: (0, 0)
    return pl.pallas_call(
        _gla_kernel,
        grid=(SEQ // tg,),
        in_specs=[
            pl.BlockSpec((BATCH, tg, 2 * GLA_QK_W), row),
            pl.BlockSpec((BATCH, tg, GLA_V_W), row),
            pl.BlockSpec((BATCH, tg, GLA_V_W), row),
            pl.BlockSpec((BATCH, tg, LANES), row),
            pl.BlockSpec((LANES, GLA_QK_W), const),
            pl.BlockSpec((1, GLA_QK_W), const),
            pl.BlockSpec((1, GLA_DV), const),
            pl.BlockSpec((3 * tg, 2 * tg), const),
        ],
        out_specs=pl.BlockSpec((BATCH, tg, GLA_V_W), row),
        out_shape=jax.ShapeDtypeStruct((BATCH, SEQ, GLA_V_W), F32),
        scratch_shapes=[pltpu.VMEM((BATCH, GLA_DV, GLA_QK_W), F32)],
        compiler_params=_cparams(("arbitrary",)),
        name="gla_mixer",
    )(qk, v, r, misc, w2, b_gate.reshape(1, GLA_QK_W), out_g.reshape(1, GLA_DV), _gla_chunk_sums())


POS_SIDE = 128


def _rope_table_kernel(freq_ref, pos_ref, cos_ref, sin_ref):
    pos = pos_ref[...].astype(F32)
    for f in range(ROT_HALF):
        ang = pos * freq_ref[f]
        cos_ref[f] = jnp.cos(ang)
        sin_ref[f] = jnp.sin(ang)


def rope_tables(positions):
    inv_freq = jnp.float32(ROPE_THETA) ** (-jnp.arange(ROT_HALF, dtype=F32) / ROT_HALF)
    shp = jax.ShapeDtypeStruct((ROT_HALF, POS_SIDE, POS_SIDE), F32)
    cos, sin = pl.pallas_call(
        _rope_table_kernel,
        in_specs=[pl.BlockSpec(memory_space=pltpu.SMEM), pl.BlockSpec(memory_space=pltpu.VMEM)],
        out_specs=[pl.BlockSpec(memory_space=pltpu.VMEM)] * 2,
        out_shape=[shp, shp],
        name="rope_tables",
    )(inv_freq, positions.reshape(POS_SIDE, POS_SIDE))
    return jnp.concatenate([cos, sin], axis=0).reshape(ROT_DIM, N_TOK).T.reshape(BATCH, SEQ, ROT_DIM)


def _rope_placement():
    place = np.zeros((ROT_DIM, 3 * LANES), np.float32)
    const = np.zeros((1, 3 * LANES), np.float32)
    for lane in range(LANES):
        i = lane % NSA_DH
        if i < ROT_HALF:
            place[i, lane] = 1.0
            place[ROT_HALF + i, LANES + lane] = -1.0
        elif i < ROT_DIM:
            place[i - ROT_HALF, lane] = 1.0
            place[i, 2 * LANES + lane] = 1.0
        else:
            const[0, lane] = 1.0
    return jnp.asarray(place), jnp.asarray(const)


def _lane_tables(cs, place_ref, const_ref):
    tab = jnp.dot(cs, place_ref[...], precision=HI, preferred_element_type=F32) + const_ref[...]
    return tab[:, 0:LANES], tab[:, LANES:2 * LANES], tab[:, 2 * LANES:3 * LANES]


def _block_diag_ones2(width):
    h = np.arange(width) // NSA_DH
    bd = (h[:, None] == h[None, :]).astype(np.float32)
    return jnp.asarray(np.concatenate([bd, bd], axis=0), BF16)


def _head_norm_rope(x, gain, bd2, c, sm, sp):
    width = x.shape[-1]
    reps = width // LANES
    sq = x * x
    sq_hi = sq.astype(BF16)
    sq_lo = (sq - sq_hi.astype(F32)).astype(BF16)
    ss = jnp.dot(jnp.concatenate([sq_hi, sq_lo], axis=1), bd2, preferred_element_type=F32)
    y = x * lax.rsqrt(ss * (1.0 / NSA_DH) + NORM_EPS) * gain
    tile = lambda t: jnp.concatenate([t] * reps, axis=1) if reps > 1 else t
    return (y * tile(c) + pltpu.roll(y, width - ROT_HALF, 1) * tile(sm) + pltpu.roll(y, ROT_HALF, 1) * tile(sp))


PREP_TM = 512


def _prep_kernel(q_ref, ks_ref, kw_ref, cs_ref, place_ref, const_ref, gq_ref, gk_ref, bdq_ref, bdk_ref,
                 qo_ref, kso_ref, kwo_ref):
    c, sm, sp = _lane_tables(cs_ref[0], place_ref, const_ref)
    bdk = bdk_ref[...]
    q = _head_norm_rope(q_ref[0], gq_ref[...], bdq_ref[...], c, sm, sp) * (NSA_DH ** -0.5)
    qo_ref[0] = q.T.reshape(NSA_HEADS, NSA_DH, PREP_TM)
    kso_ref[0] = _head_norm_rope(ks_ref[0], gk_ref[0:1, :], bdk, c, sm, sp)
    kwo_ref[0] = _head_norm_rope(kw_ref[0], gk_ref[1:2, :], bdk, c, sm, sp)


def nsa_prep(nq, nkv, cs, q_gain, k_gain):
    tm = PREP_TM
    row = lambda b, i: (b, i, 0)
    const = lambda b, i: (0, 0)
    gq = jnp.tile(q_gain, NSA_HEADS).reshape(1, NSA_Q_W)
    gk = jnp.stack([jnp.tile(k_gain[1], NSA_KV_GROUPS), jnp.tile(k_gain[2], NSA_KV_GROUPS)])
    return pl.pallas_call(
        _prep_kernel,
        grid=(BATCH, SEQ // tm),
        in_specs=[
            pl.BlockSpec((1, tm, NSA_Q_W), row),
            pl.BlockSpec((1, tm, NSA_KV_W), lambda b, i: (b, i, 2)),
            pl.BlockSpec((1, tm, NSA_KV_W), lambda b, i: (b, i, 4)),
            pl.BlockSpec((1, tm, ROT_DIM), row),
            pl.BlockSpec((ROT_DIM, 3 * LANES), const),
            pl.BlockSpec((1, 3 * LANES), const),
            pl.BlockSpec((1, NSA_Q_W), const),
            pl.BlockSpec((2, NSA_KV_W), const),
            pl.BlockSpec((2 * NSA_Q_W, NSA_Q_W), const),
            pl.BlockSpec((2 * NSA_KV_W, NSA_KV_W), const),
        ],
        out_specs=[pl.BlockSpec((1, NSA_HEADS, NSA_DH, tm), lambda b, i: (b, 0, 0, i)),
                   pl.BlockSpec((1, tm, NSA_KV_W), row), pl.BlockSpec((1, tm, NSA_KV_W), row)],
        out_shape=[jax.ShapeDtypeStruct((BATCH, NSA_HEADS, NSA_DH, SEQ), F32),
                   jax.ShapeDtypeStruct((BATCH, SEQ, NSA_KV_W), F32), jax.ShapeDtypeStruct((BATCH, SEQ, NSA_KV_W), F32)],
        compiler_params=_cparams(("arbitrary", "arbitrary")),
        name="nsa_prep",
    )(nq, nkv, nkv, cs, *_rope_placement(), gq, gk, _block_diag_ones2(NSA_Q_W), _block_diag_ones2(NSA_KV_W))


SEG_W = CMP_STRIDE * NSA_DH


def _cmp_kernel(xk_ref, xv_ref, pe_ref, w1_ref, w2_ref, gain_ref, cs_ref, place_ref, const_ref, bd_ref, ko_ref, vo_ref):
    def compress(x_ref, kv):
        out = jnp.zeros((N_CMP_PAD, LANES), F32)
        for g in range(NSA_KV_GROUPS):
            x = x_ref[0, g]
            ha = jnp.dot(x + pe_ref[kv, 0], w1_ref[kv, 0:SEG_W, :], precision=HI, preferred_element_type=F32)
            hb = jnp.dot(x + pe_ref[kv, 1], w1_ref[kv, SEG_W:2 * SEG_W, :], precision=HI, preferred_element_type=F32)
            hid = ha + pltpu.roll(hb, N_CMP_PAD - 1, 0)
            out += jnp.dot(jax.nn.gelu(hid), w2_ref[kv, g], precision=HI, preferred_element_type=F32)
        return out

    c, sm, sp = _lane_tables(cs_ref[0], place_ref, const_ref)
    ko_ref[0] = _head_norm_rope(compress(xk_ref, 0), gain_ref[...], bd_ref[...], c, sm, sp)
    vo_ref[0] = compress(xv_ref, 1)


def nsa_compress(xk, xv, cmp_pe, cmp_w1, cmp_w2, k_gain0, cs_last):
    pe = cmp_pe.reshape(2, 2, 1, SEG_W)
    w2 = jnp.zeros((2, NSA_KV_GROUPS, CMP_HIDDEN, LANES), F32)
    for g in range(NSA_KV_GROUPS):
        w2 = w2.at[:, g, :, g * NSA_DH:(g + 1) * NSA_DH].set(cmp_w2)
    seg = pl.BlockSpec((1, NSA_KV_GROUPS, N_CMP_PAD, SEG_W), lambda b: (b, 0, 0, 0))
    tab = pl.BlockSpec((1, N_CMP_PAD, LANES), lambda b: (b, 0, 0))
    full = lambda shape: pl.BlockSpec(shape, lambda b: (0,) * len(shape))
    return pl.pallas_call(
        _cmp_kernel,
        grid=(BATCH,),
        in_specs=[seg, seg, full((2, 2, 1, SEG_W)), full((2, 2 * SEG_W, CMP_HIDDEN)),
                  full((2, NSA_KV_GROUPS, CMP_HIDDEN, LANES)), full((1, LANES)),
                  pl.BlockSpec((1, N_CMP_PAD, ROT_DIM), lambda b: (b, 0, 0)), full((ROT_DIM, 3 * LANES)),
                  full((1, 3 * LANES)), full((2 * LANES, LANES))],
        out_specs=[tab, tab],
        out_shape=[jax.ShapeDtypeStruct((BATCH, N_CMP_PAD, LANES), F32)] * 2,
        compiler_params=_cparams(("arbitrary",)),
        name="nsa_compress",
    )(xk, xv, pe, cmp_w1, w2, jnp.tile(k_gain0, NSA_KV_GROUPS).reshape(1, LANES), cs_last, *_rope_placement(),
      _block_diag_ones2(LANES))


CA_TQ = 256
SUBLANES = 8


CA_COLS = NSA_HPG * CA_TQ
CMP_PER_SEL = SEL_BLOCK // CMP_STRIDE
TOPK_BANDS = 4


def split3_keys(k):
    hi = k.astype(BF16)
    lo = (k - hi.astype(F32)).astype(BF16)
    return jnp.concatenate([hi, lo, hi], axis=-1)


def _top_k_rows(score, k):
    rows, cols = score.shape
    row = lax.broadcasted_iota(jnp.int32, (rows, cols), 0).astype(F32)
    taken = jnp.zeros((rows, cols), F32)
    left = score
    for _ in range(k):
        top = jnp.max(left, axis=0, keepdims=True)
        first = jnp.min(jnp.where(left == top, row, float(rows)), axis=0, keepdims=True)
        hit = row == first
        taken = jnp.where(hit, 1.0, taken)
        left = jnp.where(hit, -jnp.inf, left)
    return taken


def _cattn_kernel(q_ref, kc_ref, vct_ref, gl_ref, o_ref, sel_ref, q3_ref, ps_ref):
    tq = CA_TQ
    q0 = pl.program_id(1) * tq
    lanes4 = lambda t: jnp.concatenate([t] * NSA_HPG, axis=1)
    cend = lax.broadcasted_iota(jnp.int32, (N_CMP_PAD, tq), 0) * CMP_STRIDE + (CMP_LEN - 1)
    tc = q0 + lax.broadcasted_iota(jnp.int32, (N_CMP_PAD, tq), 1)
    cmask = lanes4(cend <= tc)
    jj = lax.broadcasted_iota(jnp.int32, (N_SEL, tq), 0)
    tt = q0 + lax.broadcasted_iota(jnp.int32, (N_SEL, tq), 1)
    cur = jnp.right_shift(tt, 6)
    forced = (jj == 0) | (jj == cur) | (jj == cur - 1)
    valid = jj * SEL_BLOCK <= tt

    for g in range(NSA_KV_GROUPS):
        heads = range(g * NSA_HPG, (g + 1) * NSA_HPG)
        for n, h in enumerate(heads):
            q = q_ref[0, h]
            hi = q.astype(BF16)
            lo = (q - hi.astype(F32)).astype(BF16)
            for t, part in enumerate((hi, hi, lo)):
                q3_ref[g, t * NSA_DH:(t + 1) * NSA_DH, n * tq:(n + 1) * tq] = part
        s = jnp.dot(kc_ref[0, g], q3_ref[g], preferred_element_type=F32)
        s = jnp.where(cmask, s, NEG_INF)
        m = jnp.max(s, axis=0, keepdims=True)
        e = jnp.where(cmask, jnp.exp(s - m), 0.0)
        l = jnp.sum(e, axis=0, keepdims=True)
        p = e / jnp.where(l > 0.0, l, 1.0)
        gate = jnp.concatenate([jax.nn.sigmoid(gl_ref[0, h, 0:1, :]) for h in heads], axis=1)
        o = jnp.dot(vct_ref[0, g], p.astype(BF16), preferred_element_type=F32) * gate
        for n, h in enumerate(heads):
            o_ref[0, h] = o[:, n * tq:(n + 1) * tq]
        psum = functools.reduce(jnp.add, [p[:, n * tq:(n + 1) * tq] for n in range(NSA_HPG)])
        for n in range(tq // LANES):
            ps_ref[g, n] = psum[:, n * LANES:(n + 1) * LANES]

        every4th = lambda r: jnp.concatenate(
            [ps_ref[g, n, pl.ds(r, N_SEL, stride=CMP_PER_SEL), :] for n in range(tq // LANES)], axis=1)
        starts_in = [every4th(r) for r in range(CMP_PER_SEL)]
        from_prev = jnp.where(jj >= 1, pltpu.roll(starts_in[CMP_PER_SEL - 1], 1, 0), 0.0)
        imp = functools.reduce(jnp.add, starts_in) + from_prev
        score = jnp.where(valid, imp + jnp.where(forced, FORCE_BONUS, 0.0), NEG_INF)
        step = pl.program_id(1)
        steps_per_band = (SEQ // tq) // TOPK_BANDS
        for band in range(TOPK_BANDS):
            n_rows = (band + 1) * (N_SEL // TOPK_BANDS)

            @pl.when((step >= band * steps_per_band) & (step < (band + 1) * steps_per_band))
            def _():
                taken = _top_k_rows(score[0:n_rows], SEL_TOPK)
                sel_ref[0, g, 0:n_rows, :] = jnp.where(valid[0:n_rows], taken, 0.0)
                if n_rows < N_SEL:
                    sel_ref[0, g, n_rows:N_SEL, :] = jnp.zeros((N_SEL - n_rows, tq), F32)


def nsa_cmp_attn(q_t, kcmp, vcmp_t, gl_t):
    tq = CA_TQ
    G = NSA_KV_GROUPS
    return pl.pallas_call(
        _cattn_kernel,
        grid=(BATCH, SEQ // tq),
        in_specs=[
            pl.BlockSpec((1, NSA_HEADS, NSA_DH, tq), lambda b, i: (b, 0, 0, i)),
            pl.BlockSpec((1, G, N_CMP_PAD, 3 * NSA_DH), lambda b, i: (b, 0, 0, 0)),
            pl.BlockSpec((1, G, NSA_DH, N_CMP_PAD), lambda b, i: (b, 0, 0, 0)),
            pl.BlockSpec((1, NSA_HEADS, 3, tq), lambda b, i: (b, 0, 0, i)),
        ],
        out_specs=[pl.BlockSpec((1, NSA_HEADS, NSA_DH, tq), lambda b, i: (b, 0, 0, i)),
                   pl.BlockSpec((1, G, N_SEL, tq), lambda b, i: (b, 0, 0, i))],
        out_shape=[jax.ShapeDtypeStruct((BATCH, NSA_HEADS, NSA_DH, SEQ), F32),
                   jax.ShapeDtypeStruct((BATCH, G, N_SEL, SEQ), F32)],
        scratch_shapes=[pltpu.VMEM((G, 3 * NSA_DH, CA_COLS), BF16), pltpu.VMEM((G, tq // LANES, N_CMP_PAD, LANES), F32)],
        compiler_params=_cparams(("arbitrary", "arbitrary")),
        name="nsa_cmp_attn",
    )(q_t, split3_keys(kcmp), vcmp_t.astype(BF16), gl_t)


SA_TQ = 256
SA_TK = 1024
SA_PARTS = 2
SA_PART = SA_TK // SA_PARTS
M_INIT = -1e20


SA_COLS = NSA_HPG * SA_TQ
SA_BLOCKS = SA_TK // SEL_BLOCK


VT_ROWS = NSA_DH + 16


def value_slab_t(v_t):
    ones = jnp.ones(v_t.shape[:2] + (1, SEQ), BF16)
    zeros = jnp.zeros(v_t.shape[:2] + (VT_ROWS - NSA_DH - 1, SEQ), BF16)
    return jnp.concatenate([v_t.astype(BF16), ones, zeros], axis=2)


def sel_key_slab(ksel):
    blk = (np.arange(SEQ) % SA_TK) // SEL_BLOCK
    onehot = (blk[:, None] == np.arange(LANES - NSA_DH)[None, :]).astype(np.float32)
    onehot = jnp.broadcast_to(jnp.asarray(onehot, BF16), ksel.shape[:3] + (LANES - NSA_DH,))
    return jnp.concatenate([ksel.astype(BF16), onehot], axis=-1)


def _sattn_kernel(q_ref, k_ref, vt_ref, sel_ref, gl_ref, prev_ref, o_ref, qa_ref, acc_ref, s_ref, m_ref):
    tq, tk = SA_TQ, SA_TK
    i = pl.program_id(1)
    groups = range(NSA_KV_GROUPS)
    slots = range(2)
    for g in groups:
        for h in range(NSA_HPG):
            q = q_ref[0, g * NSA_HPG + h].astype(BF16)
            for slot in slots:
                qa_ref[slot, g, 0:NSA_DH, h * tq:(h + 1) * tq] = q
        for slot in slots:
            qa_ref[slot, g, NSA_DH:LANES, :] = jnp.zeros((LANES - NSA_DH, SA_COLS), BF16)
    acc_ref[...] = jnp.zeros_like(acc_ref)
    lanes4 = lambda t: jnp.concatenate([t] * NSA_HPG, axis=1)
    part_keys = lambda kt, part: pl.ds(pl.multiple_of(kt * tk + part * SA_PART, SA_PART), SA_PART)

    def scores(kt, slot):
        for g in groups:
            selrows = sel_ref[0, g, pl.ds(pl.multiple_of(kt * SA_BLOCKS, SA_BLOCKS), SA_BLOCKS), :]
            qa_ref[slot, g, NSA_DH:NSA_DH + SA_BLOCKS, :] = lanes4(jnp.where(selrows > 0.5, 0.0, NEG_INF)).astype(BF16)
            for part in range(SA_PARTS):
                s = jnp.dot(k_ref[0, g, part_keys(kt, part), :], qa_ref[slot, g], preferred_element_type=F32)
                s_ref[slot, g, part] = s.astype(BF16)

    def absorb(kt, slot, ms):
        out = []
        for g in groups:
            ss = [s_ref[slot, g, part] for part in range(SA_PARTS)]
            m_tile = functools.reduce(jnp.maximum, [jnp.max(s, axis=0, keepdims=True) for s in ss])
            m_new = jnp.maximum(ms[g], m_tile.astype(F32))
            acc = jnp.exp(ms[g] - m_new) * acc_ref[g]
            for part in range(SA_PARTS):
                p = jnp.exp(ss[part] - m_new.astype(BF16))
                acc += jnp.dot(vt_ref[0, g, :, part_keys(kt, part)], p, preferred_element_type=F32)
            acc_ref[g] = acc
            out.append(m_new)
        return tuple(out)

    def two_tiles(j, ms):
        kt = 2 * j
        scores(kt + 1, 1)
        ms = absorb(kt, 0, ms)
        scores(kt + 2, 0)
        return absorb(kt + 1, 1, ms)

    n_full = (i * tq) // tk
    scores(0, 0)
    m0 = tuple(jnp.full((1, SA_COLS), M_INIT, F32) for _ in groups)
    ms = lax.fori_loop(0, n_full // 2, two_tiles, m0)
    for g in groups:
        m_ref[g] = ms[g]

    def last_tile(slot):
        start = i * tq - n_full * tk
        part, row0 = start // SA_PART, pl.multiple_of(start % SA_PART, tq)
        tri = lax.broadcasted_iota(jnp.int32, (tq, tq), 0) <= lax.broadcasted_iota(jnp.int32, (tq, tq), 1)
        bias = lanes4(jnp.where(tri, 0.0, NEG_INF)).astype(BF16)
        for g in groups:
            s_ref[slot, g, part, pl.ds(row0, tq), :] += bias
        for g, m in enumerate(absorb(n_full, slot, tuple(m_ref[g] for g in groups))):
            m_ref[g] = m

    @pl.when(n_full % 2 == 0)
    def _():
        last_tile(0)

    @pl.when(n_full % 2 == 1)
    def _():
        scores(n_full, 1)
        for g, m in enumerate(absorb(n_full - 1, 0, tuple(m_ref[g] for g in groups))):
            m_ref[g] = m
        last_tile(1)

    for g in groups:
        heads = range(g * NSA_HPG, (g + 1) * NSA_HPG)
        gate = jnp.concatenate([jax.nn.sigmoid(gl_ref[0, h, 1:2, :]) for h in heads], axis=1)
        out = acc_ref[g, 0:NSA_DH, :] / acc_ref[g, NSA_DH:NSA_DH + 1, :] * gate
        for n, h in enumerate(heads):
            o_ref[0, h] = prev_ref[0, h] + out[:, n * tq:(n + 1) * tq]


def nsa_sel_attn(q_t, k_slab, vsel_t, sel_t, gl_t, prev):
    tq = SA_TQ
    G = NSA_KV_GROUPS
    ospec = pl.BlockSpec((1, NSA_HEADS, NSA_DH, tq), lambda b, i: (b, 0, 0, i))
    return pl.pallas_call(
        _sattn_kernel,
        grid=(BATCH, SEQ // tq),
        in_specs=[
            ospec,
            pl.BlockSpec((1, G, SEQ, LANES), lambda b, i: (b, 0, 0, 0)),
            pl.BlockSpec((1, G, VT_ROWS, SEQ), lambda b, i: (b, 0, 0, 0)),
            pl.BlockSpec((1, G, N_SEL, tq), lambda b, i: (b, 0, 0, i)),
            pl.BlockSpec((1, NSA_HEADS, 3, tq), lambda b, i: (b, 0, 0, i)),
            ospec,
        ],
        out_specs=ospec,
        out_shape=jax.ShapeDtypeStruct((BATCH, NSA_HEADS, NSA_DH, SEQ), F32),
        scratch_shapes=[pltpu.VMEM((2, G, LANES, SA_COLS), BF16), pltpu.VMEM((G, VT_ROWS, SA_COLS), F32),
                        pltpu.VMEM((2, G, SA_PARTS, SA_PART, SA_COLS), BF16), pltpu.VMEM((G, 1, SA_COLS), F32)],
        input_output_aliases={5: 0},
        compiler_params=_cparams(("arbitrary", "arbitrary")),
        name="nsa_sel_attn",
    )(q_t, k_slab, vsel_t, sel_t, gl_t, prev)


WA_TQ = 256
WA_TILES = WINDOW // WA_TQ + 1


def _window_bias():
    kl = np.arange(WA_TILES * WA_TQ)[:, None]
    ql = np.arange(WA_TQ)[None, :]
    diff = ql - kl + WINDOW
    return jnp.asarray(np.where((diff >= 0) & (diff < WINDOW), 0.0, NEG_INF).astype(np.float32))


def _wattn_kernel(q_ref, k0_ref, k1_ref, k2_ref, v0_ref, v1_ref, v2_ref, bias_ref, gl_ref, prev_ref, o_ref):
    tq = WA_TQ
    i = pl.program_id(1)
    k_refs = (k0_ref, k1_ref, k2_ref)
    v_refs = (v0_ref, v1_ref, v2_ref)
    lanes4 = lambda t: jnp.concatenate([t] * NSA_HPG, axis=1)
    biases = []
    for d in range(WA_TILES):
        in_seq = i - (WA_TILES - 1) + d >= 0
        biases.append(lanes4(jnp.where(in_seq, bias_ref[d * tq:(d + 1) * tq, :], NEG_INF)))
    for g in range(NSA_KV_GROUPS):
        heads = range(g * NSA_HPG, (g + 1) * NSA_HPG)
        q = jnp.concatenate([q_ref[0, h] for h in heads], axis=1).astype(BF16)
        ss = [(jnp.dot(k_refs[d][0, g], q, preferred_element_type=F32) + biases[d]).astype(BF16) for d in range(WA_TILES)]
        m = functools.reduce(jnp.maximum, [jnp.max(s, axis=0, keepdims=True) for s in ss])
        acc = functools.reduce(jnp.add, [jnp.dot(v_refs[d][0, g], jnp.exp(ss[d] - m), preferred_element_type=F32)
                                         for d in range(WA_TILES)])
        gate = jnp.concatenate([jax.nn.sigmoid(gl_ref[0, h, 2:3, :]) for h in heads], axis=1)
        out = acc[0:NSA_DH] / acc[NSA_DH:NSA_DH + 1] * gate
        for n, h in enumerate(heads):
            o_ref[0, h] = prev_ref[0, h] + out[:, n * tq:(n + 1) * tq]


def nsa_win_attn(q_t, kwin, vwin_t, gl_t, prev):
    tq = WA_TQ
    G = NSA_KV_GROUPS
    qspec = pl.BlockSpec((1, NSA_HEADS, NSA_DH, tq), lambda b, i: (b, 0, 0, i))
    tile = lambda d: (lambda i: jnp.maximum(i - (WA_TILES - 1) + d, 0))
    kspec = lambda d: pl.BlockSpec((1, G, tq, NSA_DH), lambda b, i: (b, 0, tile(d)(i), 0))
    vspec = lambda d: pl.BlockSpec((1, G, VT_ROWS, tq), lambda b, i: (b, 0, 0, tile(d)(i)))
    return pl.pallas_call(
        _wattn_kernel,
        grid=(BATCH, SEQ // tq),
        in_specs=[qspec] + [kspec(d) for d in range(WA_TILES)] + [vspec(d) for d in range(WA_TILES)] + [
            pl.BlockSpec((WA_TILES * tq, tq), lambda b, i: (0, 0)),
            pl.BlockSpec((1, NSA_HEADS, 3, tq), lambda b, i: (b, 0, 0, i)),
            qspec,
        ],
        out_specs=qspec,
        out_shape=jax.ShapeDtypeStruct((BATCH, NSA_HEADS, NSA_DH, SEQ), F32),
        input_output_aliases={2 * WA_TILES + 3: 0},
        compiler_params=_cparams(("arbitrary", "arbitrary")),
        name="nsa_win_attn",
    )(q_t, *([kwin] * WA_TILES), *([vwin_t] * WA_TILES), _window_bias(), gl_t, prev)


def nsa_mixer(nq, nkv, misc, positions, q_gain, k_gain, cmp_pe, cmp_w1, cmp_w2):
    cs = rope_tables(positions)
    q_t, ks_r, kw_r = nsa_prep(nq, nkv, cs, q_gain, k_gain)
    group_major = lambda t: t.reshape(BATCH, SEQ, NSA_KV_GROUPS, NSA_DH).transpose(0, 2, 1, 3)
    group_major_t = lambda t: t.reshape(BATCH, SEQ, NSA_KV_GROUPS, NSA_DH).transpose(0, 2, 3, 1)
    col = lambda n: nkv[..., n * NSA_KV_W:(n + 1) * NSA_KV_W]
    segs = lambda t: group_major(t).reshape(BATCH, NSA_KV_GROUPS, N_CMP_PAD, SEG_W)
    last = jnp.minimum(jnp.arange(N_CMP_PAD) * CMP_STRIDE + CMP_LEN - 1, SEQ - 1)
    kcmp, vcmp = nsa_compress(segs(col(0)), segs(col(1)), cmp_pe, cmp_w1, cmp_w2, k_gain[0], cs[:, last])
    kcmp = kcmp.reshape(BATCH, N_CMP_PAD, NSA_KV_GROUPS, NSA_DH).transpose(0, 2, 1, 3)
    vcmp_t = vcmp.reshape(BATCH, N_CMP_PAD, NSA_KV_GROUPS, NSA_DH).transpose(0, 2, 3, 1)
    gl_t = misc[..., GLA_GATE_RANK:GLA_GATE_RANK + NSA_HEADS * 3].reshape(BATCH, SEQ, NSA_HEADS, 3).transpose(0, 2, 3, 1)
    o_t, sel_t = nsa_cmp_attn(q_t, kcmp, vcmp_t, gl_t)
    o_t = nsa_sel_attn(q_t, sel_key_slab(group_major(ks_r)), value_slab_t(group_major_t(col(3))), sel_t, gl_t, o_t)
    o_t = nsa_win_attn(q_t, group_major(kw_r).astype(BF16), value_slab_t(group_major_t(col(5))), gl_t, o_t)
    return o_t


ROUTE_ROWS = 8
HX_W = D_MODEL + 3 * LANES


def _top2_sum(a, b, c, d):
    hi1, lo1 = jnp.maximum(a, b), jnp.minimum(a, b)
    hi2, lo2 = jnp.maximum(c, d), jnp.minimum(c, d)
    return jnp.maximum(hi1, hi2) + jnp.maximum(jnp.minimum(hi1, hi2), jnp.maximum(lo1, lo2))


def _moe_prenorm_route(xn, g_ref, sh_ref, sc_ref, wr_ref, rb_ref, hx_ref, route_ref):
    h = _rms_mod(xn, g_ref[...], sh_ref[0], sc_ref[0])
    logits = lax.dot_general(wr_ref[...], h, NT_DIMS, precision=HI, preferred_element_type=F32)
    scores = jax.nn.sigmoid(logits)
    sel = scores + rb_ref[...]
    epg = EXPERTS_PER_GROUP
    srow = lambda e: sel[e:e + 1, :]
    grp = [_top2_sum(*[srow(epg * g + r) for r in range(epg)]) for g in range(N_EXPERT_GROUPS)]
    best, gi = grp[0], jnp.zeros_like(grp[0], dtype=jnp.int32)
    for g in range(1, N_EXPERT_GROUPS):
        better = grp[g] > best
        gi = jnp.where(better, g, gi)
        best = jnp.where(better, grp[g], best)

    def in_group(mat, r):
        out = mat[r:r + 1, :]
        for g in range(1, N_EXPERT_GROUPS):
            out = jnp.where(gi == g, mat[epg * g + r:epg * g + r + 1, :], out)
        return out

    v = [in_group(sel, r) for r in range(epg)]
    sc = [in_group(scores, r) for r in range(epg)]
    b1, i1, w1 = v[0], jnp.zeros_like(gi), sc[0]
    for r in range(1, epg):
        better = v[r] > b1
        i1 = jnp.where(better, r, i1)
        w1 = jnp.where(better, sc[r], w1)
        b1 = jnp.where(better, v[r], b1)
    b2 = jnp.full_like(b1, -3e38)
    i2, w2 = jnp.zeros_like(gi), jnp.zeros_like(w1)
    for r in range(epg):
        better = (i1 != r) & (v[r] > b2)
        i2 = jnp.where(better, r, i2)
        w2 = jnp.where(better, sc[r], w2)
        b2 = jnp.where(better, v[r], b2)
    tot = w1 + w2
    w1, w2 = w1 / tot, w2 / tot
    zero = jnp.zeros_like(w1)
    route_ref[0] = jnp.concatenate([gi.astype(F32)] + [zero] * (ROUTE_ROWS - 1), axis=0)
    wrows = [jnp.where(i1 == r, w1, jnp.where(i2 == r, w2, 0.0)) for r in range(epg)]
    wmat = jnp.concatenate(wrows + [jnp.zeros((LANES - epg, w1.shape[1]), F32)], axis=0).T
    w_hi = wmat.astype(BF16)
    rest = wmat - w_hi.astype(F32)
    w_mid = rest.astype(BF16)
    w_lo = (rest - w_mid.astype(F32)).astype(BF16)
    hx_ref[0, :, 0:D_MODEL] = h.astype(BF16)
    for n, part in enumerate((w_hi, w_mid, w_lo)):
        hx_ref[0, :, D_MODEL + n * LANES:D_MODEL + (n + 1) * LANES] = part


def _route_specs(tm, row, vec, const):
    in_specs = [pl.BlockSpec((1, D_MODEL), const), pl.BlockSpec((1, 1, D_MODEL), vec), pl.BlockSpec((1, 1, D_MODEL), vec),
                pl.BlockSpec((N_EXPERTS, D_MODEL), const), pl.BlockSpec((N_EXPERTS, 1), const)]
    assert tm == MOE_CHUNK
    out_specs = [pl.BlockSpec((1, tm, HX_W), lambda b, i: (b * (SEQ // tm) + i, 0, 0)),
                 pl.BlockSpec((1, ROUTE_ROWS, tm), lambda b, i: (b, 0, i))]
    out_shape = [jax.ShapeDtypeStruct((MOE_CHUNKS, MOE_CHUNK, HX_W), BF16),
                 jax.ShapeDtypeStruct((BATCH, ROUTE_ROWS, SEQ), F32)]
    return in_specs, out_specs, out_shape


def _route_args(g, shift, scale, w_router, router_bias):
    return (g.reshape(1, D_MODEL), shift.reshape(BATCH, 1, D_MODEL), scale.reshape(BATCH, 1, D_MODEL),
            w_router.T, router_bias.reshape(N_EXPERTS, 1))


OUTPROJ_TM = 512


def _outproj0_kernel(oa_ref, ob_ref, w_ref, x_ref, gate_ref, g_ref, sh_ref, sc_ref, wr_ref, rb_ref,
                     xo_ref, h_ref, route_ref):
    y = jnp.dot(oa_ref[0].astype(BF16), w_ref[0:GLA_V_W, :], preferred_element_type=F32)
    ob_t = ob_ref[0].reshape(NSA_Q_W, OUTPROJ_TM).astype(BF16)
    y += lax.dot_general(ob_t, w_ref[GLA_V_W:GLA_V_W + NSA_Q_W, :], TN_DIMS, preferred_element_type=F32)
    xn = x_ref[0] + gate_ref[0] * y
    xo_ref[0] = xn
    _moe_prenorm_route(xn, g_ref, sh_ref, sc_ref, wr_ref, rb_ref, h_ref, route_ref)


def outproj0(o_a, o_b, w_out, x, gate, route_args):
    tm = OUTPROJ_TM
    row = lambda b, i: (b, i, 0)
    vec = lambda b, i: (b, 0, 0)
    const = lambda b, i: (0, 0)
    r_in, r_out, r_shape = _route_specs(tm, row, vec, const)
    return pl.pallas_call(
        _outproj0_kernel,
        grid=(BATCH, SEQ // tm),
        in_specs=[pl.BlockSpec((1, tm, GLA_V_W), row), pl.BlockSpec((1, NSA_HEADS, NSA_DH, tm), lambda b, i: (b, 0, 0, i)),
                  pl.BlockSpec((GLA_V_W + NSA_Q_W, D_MODEL), const), pl.BlockSpec((1, tm, D_MODEL), row),
                  pl.BlockSpec((1, 1, D_MODEL), vec)] + r_in,
        out_specs=[pl.BlockSpec((1, tm, D_MODEL), row)] + r_out,
        out_shape=[jax.ShapeDtypeStruct((BATCH, SEQ, D_MODEL), F32)] + r_shape,
        compiler_params=_cparams(("arbitrary", "arbitrary")),
        name="outproj0",
    )(o_a, o_b, w_out.astype(BF16), x, gate.reshape(BATCH, 1, D_MODEL), *route_args)


GMLP_TM = 512


def _gmlp_kernel(x_ref, g1_ref, sh1_ref, sc1_ref, win_ref, ng_ref, ws_ref, bs_ref, wout_ref, gate_ref,
                 g_ref, sh_ref, sc_ref, wr_ref, rb_ref, xo_ref, h_ref, route_ref, gated_ref, v_ref):
    x = x_ref[0]
    h = _rms_mod(x, g1_ref[...], sh1_ref[0], sc1_ref[0]).astype(BF16)
    group_cols = lambda g: slice(g * SGU_GROUP_DIM, (g + 1) * SGU_GROUP_DIM)
    ssq = jnp.zeros((GMLP_TM, LANES), F32)
    for g in range(SGU_GROUPS):
        lo = SGU_WIDTH + g * SGU_GROUP_DIM
        v = jax.nn.gelu(jnp.dot(h, win_ref[:, lo:lo + SGU_GROUP_DIM], preferred_element_type=F32))
        v_ref[:, group_cols(g)] = v
        ssq += functools.reduce(jnp.add, [v[:, n * LANES:(n + 1) * LANES] ** 2 for n in range(SGU_GROUP_DIM // LANES)])
    rs = lax.rsqrt(jnp.sum(ssq, axis=-1, keepdims=True) * (1.0 / SGU_WIDTH) + NORM_EPS)
    ri = lax.broadcasted_iota(jnp.int32, (SGU_CHUNK, SGU_CHUNK), 0)
    ci = lax.broadcasted_iota(jnp.int32, (SGU_CHUNK, SGU_CHUNK), 1)
    for g in range(SGU_GROUPS):
        cols = group_cols(g)
        u = jax.nn.gelu(jnp.dot(h, win_ref[:, cols], preferred_element_type=F32))
        vn = (v_ref[:, cols] * rs * ng_ref[:, cols]).astype(BF16)
        w = jnp.where(ri >= ci, ws_ref[g], 0.0).astype(BF16)
        for c in range(GMLP_TM // SGU_CHUNK):
            rows = slice(c * SGU_CHUNK, (c + 1) * SGU_CHUNK)
            mix = jnp.dot(w, vn[rows], preferred_element_type=F32) + bs_ref[:, g:g + 1]
            gated_ref[rows, cols] = (u[rows] * mix).astype(BF16)
    y = jnp.dot(gated_ref[...], wout_ref[...], preferred_element_type=F32)
    xn = x + gate_ref[0] * y
    xo_ref[0] = xn
    _moe_prenorm_route(xn, g_ref, sh_ref, sc_ref, wr_ref, rb_ref, h_ref, route_ref)


def gmlp_layer(x, g1, shift1, scale1, w_in, norm_g, w_s, b_s, w_out, gate, route_args):
    tm = GMLP_TM
    row = lambda b, i: (b, i, 0)
    vec = lambda b, i: (b, 0, 0)
    const = lambda b, i: (0, 0)
    r_in, r_out, r_shape = _route_specs(tm, row, vec, const)
    vspec = pl.BlockSpec((1, 1, D_MODEL), vec)
    return pl.pallas_call(
        _gmlp_kernel,
        grid=(BATCH, SEQ // tm),
        in_specs=[pl.BlockSpec((1, tm, D_MODEL), row), pl.BlockSpec((1, D_MODEL), const), vspec, vspec,
                  pl.BlockSpec((D_MODEL, 2 * SGU_WIDTH), const), pl.BlockSpec((1, SGU_WIDTH), const),
                  pl.BlockSpec((SGU_GROUPS, SGU_CHUNK, SGU_CHUNK), lambda b, i: (0, 0, 0)),
                  pl.BlockSpec((SGU_CHUNK, SGU_GROUPS), const), pl.BlockSpec((SGU_WIDTH, D_MODEL), const), vspec] + r_in,
        out_specs=[pl.BlockSpec((1, tm, D_MODEL), row)] + r_out,
        out_shape=[jax.ShapeDtypeStruct((BATCH, SEQ, D_MODEL), F32)] + r_shape,
        scratch_shapes=[pltpu.VMEM((tm, SGU_WIDTH), BF16), pltpu.VMEM((tm, SGU_WIDTH), F32)],
        compiler_params=_cparams(("arbitrary", "arbitrary"), vmem_mib=56),
        name="gmlp_layer",
    )(x, g1.reshape(1, D_MODEL), shift1.reshape(BATCH, 1, D_MODEL), scale1.reshape(BATCH, 1, D_MODEL),
      w_in.astype(BF16), norm_g.reshape(1, SGU_WIDTH), w_s, b_s.T, w_out.astype(BF16),
      gate.reshape(BATCH, 1, D_MODEL), *route_args)


MOE_TM = 256
MOE_CHUNK = 512
MOE_SORTED = N_TOK + N_EXPERT_GROUPS * MOE_TM
MOE_TILES = MOE_SORTED // MOE_TM
MOE_CHUNKS = N_TOK // MOE_CHUNK
MOE_PAIRS = MOE_TILES + N_EXPERT_GROUPS * MOE_CHUNKS
FLAG_ACTIVE, FLAG_FIRST, FLAG_LAST, FLAG_ZERO = 1, 2, 4, 8
EXP_WIN = 4
CMB_WIN = 8
MOE_TSTEPS = MOE_TILES + MOE_PAIRS // EXP_WIN
MOE_CSTEPS = MOE_CHUNKS + MOE_PAIRS // CMB_WIN


def _plan_kernel(gi_ref, rank_ref, before_ref):
    gi = gi_ref[...]
    r = lax.broadcasted_iota(jnp.int32, (POS_SIDE, POS_SIDE), 0)
    c = lax.broadcasted_iota(jnp.int32, (POS_SIDE, POS_SIDE), 1)
    upper = jnp.where(r <= c, 1.0, 0.0)
    lower_strict = jnp.where(c < r, 1.0, 0.0)
    rank = jnp.zeros((POS_SIDE, POS_SIDE), F32)
    for g in range(N_EXPERT_GROUPS):
        member = jnp.where(gi == g, 1.0, 0.0)
        in_row = jnp.dot(member, upper, precision=HI, preferred_element_type=F32)
        row_total = jnp.broadcast_to(in_row[:, POS_SIDE - 1:POS_SIDE], (POS_SIDE, POS_SIDE))
        before = jnp.dot(lower_strict, row_total, precision=HI, preferred_element_type=F32)
        before_ref[g] = before
        rank += member * (before + in_row - 1.0)
    rank_ref[...] = rank


def moe_plan(route):
    tm = MOE_TM
    i32 = jnp.int32
    gi_f = route[:, 0, :].reshape(POS_SIDE, POS_SIDE)
    rank, before = pl.pallas_call(
        _plan_kernel,
        out_shape=[jax.ShapeDtypeStruct((POS_SIDE, POS_SIDE), F32),
                   jax.ShapeDtypeStruct((N_EXPERT_GROUPS, POS_SIDE, POS_SIDE), F32)],
        name="moe_plan",
    )(gi_f)
    gi = gi_f.reshape(N_TOK).astype(i32)
    groups = jnp.arange(N_EXPERT_GROUPS, dtype=i32)
    member = gi[None, :] == groups[:, None]
    tot = jnp.sum(member, axis=1).astype(i32)
    padded = (tot + tm - 1) // tm * tm
    gend = jnp.cumsum(padded).astype(i32)
    gstart = gend - padded
    pos = jnp.sum(jnp.where(member, gstart[:, None], 0), axis=0).astype(i32) + rank.reshape(N_TOK).astype(i32)
    rows_per_chunk = MOE_CHUNK // POS_SIDE
    cnt_end = jnp.concatenate([before[:, rows_per_chunk::rows_per_chunk, 0].astype(i32), tot[:, None]], axis=1)
    t = jnp.arange(MOE_TILES, dtype=i32)
    n_used = gend[-1] // tm
    tile_g = jnp.minimum(jnp.sum(gend[None, :] <= (t * tm)[:, None], axis=1), N_EXPERT_GROUPS - 1).astype(i32)
    k0 = t * tm - gstart[tile_g]
    k1 = jnp.minimum(k0 + tm, tot[tile_g]) - 1
    ce = cnt_end[tile_g]
    c_lo = jnp.sum(ce <= k0[:, None], axis=1).astype(i32)
    c_hi = jnp.sum(ce <= k1[:, None], axis=1).astype(i32)
    npairs = jnp.where(t < n_used, c_hi - c_lo + 1, 0)
    pend = jnp.cumsum(npairs).astype(i32)
    pstart = pend - npairs
    total = pend[-1]
    l = jnp.arange(MOE_PAIRS, dtype=i32)
    real = l < total
    lt = jnp.minimum(l, total - 1)

    def windows(count, win, n_steps):
        per_item = (count + win - 1) // win
        end = jnp.cumsum(per_item).astype(i32)
        start = end - per_item
        s = jnp.arange(n_steps, dtype=i32)
        real_s = s < end[-1]
        sc = jnp.minimum(s, end[-1] - 1)
        item = jnp.sum(end[None, :] <= sc[:, None], axis=1).astype(i32)
        j = sc - start[item]
        flags_s = jnp.where(real_s, FLAG_ACTIVE + jnp.where(j == 0, FLAG_FIRST, 0)
                            + jnp.where(j == per_item[item] - 1, FLAG_LAST, 0), 0).astype(i32)
        return item, j, flags_s, real_s, s - end[-1]

    tile_s, j, flags, real_s, spare = windows(npairs, EXP_WIN, MOE_TSTEPS)
    c0 = c_lo[tile_s] + EXP_WIN * j
    n_valid = jnp.minimum(EXP_WIN, c_hi[tile_s] - c0 + 1).astype(i32)
    spare_tile = jnp.minimum(n_used + spare, MOE_TILES - 1)
    flags = jnp.where(real_s, flags, jnp.where(spare_tile >= n_used, FLAG_ZERO, 0)).astype(i32)
    tile_sched = jnp.where(real_s, tile_s, spare_tile).astype(i32)
    by_tile = (tile_sched, c0.astype(i32), n_valid, flags, tile_g[tile_sched])
    cc = jnp.arange(MOE_CHUNKS, dtype=i32)
    is_pair = (cc[:, None] >= c_lo[None, :]) & (cc[:, None] <= c_hi[None, :]) & (t[None, :] < n_used)
    seen = jnp.cumsum(is_pair.reshape(-1).astype(i32))
    flat = jnp.sum(seen[None, :] <= lt[:, None], axis=1).astype(i32)
    pair_tile = flat % MOE_TILES
    per_chunk = jnp.sum(is_pair, axis=1).astype(i32)
    first_pair = jnp.cumsum(per_chunk).astype(i32) - per_chunk
    chunk_s, j, flags_c, _, _ = windows(per_chunk, CMB_WIN, MOE_CSTEPS)
    base = first_pair[chunk_s] + CMB_WIN * j
    n_valid_c = jnp.minimum(CMB_WIN, per_chunk[chunk_s] - CMB_WIN * j).astype(i32)
    tiles_c = tuple(pair_tile[jnp.minimum(base + w, total - 1)] for w in range(CMB_WIN))
    by_chunk = (chunk_s, n_valid_c, flags_c) + tiles_c
    return pos.reshape(MOE_CHUNKS, 1, MOE_CHUNK), by_tile, by_chunk


def _one_hot_rows(pos_row, tile):
    rows = tile * MOE_TM + lax.broadcasted_iota(jnp.int32, (MOE_TM, MOE_CHUNK), 0)
    return jnp.where(pos_row == rows, 1.0, 0.0).astype(BF16)


def _moe_kernel(tile_ref, c0_ref, nv_ref, flag_ref, grp_ref, *refs):
    pos_refs, hx_refs = refs[0:EXP_WIN], refs[EXP_WIN:2 * EXP_WIN]
    wg_ref, wu_ref, wd_ref, y_ref, acc_ref = refs[2 * EXP_WIN:]
    l = pl.program_id(0)
    flags = flag_ref[l]

    @pl.when((flags & FLAG_FIRST) != 0)
    def _():
        acc_ref[...] = jnp.zeros_like(acc_ref)

    for w in range(EXP_WIN):
        @pl.when(((flags & FLAG_ACTIVE) != 0) & (w < nv_ref[l]))
        def _():
            onehot = _one_hot_rows(pos_refs[w][0], tile_ref[l])
            acc_ref[...] += jnp.dot(onehot, hx_refs[w][0], preferred_element_type=F32)

    @pl.when((flags & FLAG_LAST) != 0)
    def _():
        x = acc_ref[:, 0:D_MODEL].astype(BF16)
        w = functools.reduce(jnp.add, [acc_ref[:, D_MODEL + n * LANES:D_MODEL + (n + 1) * LANES] for n in range(3)])
        y = jnp.zeros((MOE_TM, D_MODEL), F32)
        for r in range(EXPERTS_PER_GROUP):
            gate = jnp.dot(x, wg_ref[0, 0, r], preferred_element_type=F32)
            up = jnp.dot(x, wu_ref[0, 0, r], preferred_element_type=F32)
            hid = (_silu(gate) * up * w[:, r:r + 1]).astype(BF16)
            y += jnp.dot(hid, wd_ref[0, 0, r], preferred_element_type=F32)
        y_ref[...] = y

    @pl.when((flags & FLAG_ZERO) != 0)
    def _():
        y_ref[...] = jnp.zeros_like(y_ref)


def moe_experts(hx, pos, by_tile, w_gate, w_up, w_down, layer):
    grouped = lambda w: w.reshape(DEPTH, N_EXPERT_GROUPS, EXPERTS_PER_GROUP, *w.shape[2:])
    wspec = lambda k, n: pl.BlockSpec((1, 1, EXPERTS_PER_GROUP, k, n), lambda l, t, c, n_, f, g: (layer, g[l], 0, 0, 0))
    chunk = lambda w: (lambda l, t, c, n_, f, g: (jnp.minimum(c[l] + w, MOE_CHUNKS - 1), 0, 0))
    return pl.pallas_call(
        _moe_kernel,
        grid_spec=pltpu.PrefetchScalarGridSpec(
            num_scalar_prefetch=5,
            grid=(MOE_TSTEPS,),
            in_specs=[pl.BlockSpec((1, 1, MOE_CHUNK), chunk(w)) for w in range(EXP_WIN)]
            + [pl.BlockSpec((1, MOE_CHUNK, HX_W), chunk(w)) for w in range(EXP_WIN)]
            + [wspec(D_MODEL, EXPERT_HIDDEN), wspec(D_MODEL, EXPERT_HIDDEN), wspec(EXPERT_HIDDEN, D_MODEL)],
            out_specs=pl.BlockSpec((MOE_TM, D_MODEL), lambda l, t, c, n_, f, g: (t[l], 0)),
            scratch_shapes=[pltpu.VMEM((MOE_TM, HX_W), F32)],
        ),
        out_shape=jax.ShapeDtypeStruct((MOE_SORTED, D_MODEL), F32),
        compiler_params=_cparams(("arbitrary",), vmem_mib=56),
        name="moe_experts",
    )(*by_tile, *([pos] * EXP_WIN), *([hx] * EXP_WIN), grouped(w_gate), grouped(w_up), grouped(w_down))


def _moe_combine_kernel(chunk_ref, nv_ref, flag_ref, *refs):
    tile_refs = refs[0:CMB_WIN]
    pos_ref = refs[CMB_WIN]
    y_refs = refs[CMB_WIN + 1:2 * CMB_WIN + 1]
    x_ref, gate_ref, o_ref, acc_ref = refs[2 * CMB_WIN + 1:]
    l = pl.program_id(0)
    flags = flag_ref[l]

    @pl.when((flags & FLAG_FIRST) != 0)
    def _():
        acc_ref[...] = jnp.zeros_like(acc_ref)

    for w in range(CMB_WIN):
        @pl.when(((flags & FLAG_ACTIVE) != 0) & (w < nv_ref[l]))
        def _():
            onehot = _one_hot_rows(pos_ref[0], tile_refs[w][l])
            y = y_refs[w][...]
            y_hi = y.astype(BF16)
            y_lo = (y - y_hi.astype(F32)).astype(BF16)
            acc_ref[...] += (lax.dot_general(onehot, y_hi, TN_DIMS, preferred_element_type=F32)
                             + lax.dot_general(onehot, y_lo, TN_DIMS, preferred_element_type=F32))

    @pl.when((flags & FLAG_LAST) != 0)
    def _():
        o_ref[0] = x_ref[0] + gate_ref[0] * acc_ref[...]


def moe_combine(x, y_sorted, pos, by_chunk, gate):
    per_b = SEQ // MOE_CHUNK
    tok = lambda l, c, *_: (c[l] // per_b, c[l] % per_b, 0)
    tile = lambda w: (lambda l, c, n_, f, *tiles: (tiles[w][l], 0))
    return pl.pallas_call(
        _moe_combine_kernel,
        grid_spec=pltpu.PrefetchScalarGridSpec(
            num_scalar_prefetch=3 + CMB_WIN,
            grid=(MOE_CSTEPS,),
            in_specs=[pl.BlockSpec((1, 1, MOE_CHUNK), lambda l, c, *_: (c[l], 0, 0))]
            + [pl.BlockSpec((MOE_TM, D_MODEL), tile(w)) for w in range(CMB_WIN)]
            + [pl.BlockSpec((1, MOE_CHUNK, D_MODEL), tok),
               pl.BlockSpec((1, 1, D_MODEL), lambda l, c, *_: (c[l] // per_b, 0, 0))],
            out_specs=pl.BlockSpec((1, MOE_CHUNK, D_MODEL), tok),
            scratch_shapes=[pltpu.VMEM((MOE_CHUNK, D_MODEL), F32)],
        ),
        out_shape=jax.ShapeDtypeStruct((BATCH, SEQ, D_MODEL), F32),
        compiler_params=_cparams(("arbitrary",)),
        name="moe_combine",
    )(*by_chunk, pos, *([y_sorted] * CMB_WIN), x, gate.reshape(BATCH, 1, D_MODEL))


def moe_layer(x, hx, route, gate, w_gate, w_up, w_down, layer):
    pos, by_tile, by_chunk = moe_plan(route)
    y_sorted = moe_experts(hx, pos, by_tile, w_gate, w_up, w_down, layer)
    return moe_combine(x, y_sorted, pos, by_chunk, gate)


def kernel(x, c, positions, w_ada, b_ada, norm_g, w_in_ab, w_out_ab, gla_w_gate2, gla_b_gate, gla_norm_g, nsa_q_gain, nsa_k_gain, nsa_cmp_pe, nsa_cmp_w1, nsa_cmp_w2, w_in_c, sgu_norm_g, sgu_w_s, sgu_b_s, w_out_c, w_router, router_bias, w_gate, w_up, w_down):
    mod = ada_modulation(c, w_ada, b_ada)
    qk, gv, gr, nq, nkv, misc = inproj0(x, norm_g[0, 0], mod[0, :, 0], mod[0, :, 1], _arrange_w_in(w_in_ab[0]))
    o_a = gla_mixer(qk, gv, gr, misc, gla_w_gate2[0], gla_b_gate[0], gla_norm_g[0])
    o_b = nsa_mixer(nq, nkv, misc, positions, nsa_q_gain[0], nsa_k_gain[0], nsa_cmp_pe[0], nsa_cmp_w1[0], nsa_cmp_w2[0])
    wg, wu, wd = w_gate.astype(BF16), w_up.astype(BF16), w_down.astype(BF16)
    route_args = lambda l: _route_args(norm_g[l, 1], mod[l, :, 3], mod[l, :, 4], w_router, router_bias)
    x1, h, route = outproj0(o_a, o_b, w_out_ab[0], x, mod[0, :, 2], route_args(0))
    x2 = moe_layer(x1, h, route, mod[0, :, 5], wg, wu, wd, 0)
    x3, h, route = gmlp_layer(x2, norm_g[1, 0], mod[1, :, 0], mod[1, :, 1], w_in_c[0], sgu_norm_g[0], sgu_w_s[0],
                              sgu_b_s[0], w_out_c[0], mod[1, :, 2], route_args(1))
    return moe_layer(x3, h, route, mod[1, :, 5], wg, wu, wd, 1)
```

```python
import functools

import numpy as np
import jax
import jax.numpy as jnp
from jax import lax
from jax.experimental import pallas as pl
from jax.experimental.pallas import tpu as pltpu

D_MODEL = 1024
BATCH = 2
SEQ = 8192
DEPTH = 2
N_TOK = BATCH * SEQ

GLA_HEADS = 4
GLA_DK = 64
GLA_DV = 128
GLA_GATE_RANK = 16
GLA_TAU = 16.0
GLA_CHUNK = 64
NSA_HEADS = 8
NSA_KV_GROUPS = 2
NSA_HPG = NSA_HEADS // NSA_KV_GROUPS
NSA_DH = 64
CMP_LEN = 32
CMP_STRIDE = 16
CMP_HIDDEN = 256
SEL_BLOCK = 64
SEL_TOPK = 16
WINDOW = 512
ROPE_THETA = 500000.0
ROT_DIM = NSA_DH // 4
ROT_HALF = ROT_DIM // 2
SGU_CHUNK = 128
SGU_GROUPS = 8
SGU_WIDTH = 2048
SGU_GROUP_DIM = SGU_WIDTH // SGU_GROUPS
N_EXPERTS = 16
N_EXPERT_GROUPS = 4
EXPERTS_PER_GROUP = N_EXPERTS // N_EXPERT_GROUPS
MOE_TOPK = 2
EXPERT_HIDDEN = 512

GLA_QK_W = GLA_HEADS * GLA_DK
GLA_V_W = GLA_HEADS * GLA_DV
NSA_Q_W = NSA_HEADS * NSA_DH
NSA_KV_W = NSA_KV_GROUPS * NSA_DH
N_CMP = (SEQ - CMP_LEN) // CMP_STRIDE + 1
N_CMP_PAD = SEQ // CMP_STRIDE
N_SEL = SEQ // SEL_BLOCK

NORM_EPS = 1e-6
NEG_INF = -1e30
FORCE_BONUS = 1e4

LANES = 128
MIB = 1024 * 1024

F32 = jnp.float32
BF16 = jnp.bfloat16
HI = lax.Precision.HIGHEST
NT_DIMS = (((1,), (1,)), ((), ()))
TN_DIMS = (((0,), (0,)), ((), ()))


def _cparams(sem, vmem_mib=48):
    return pltpu.CompilerParams(dimension_semantics=sem, vmem_limit_bytes=vmem_mib * MIB)


def _rms_mod(x, g, shift, scale):
    y = x * lax.rsqrt(jnp.mean(x * x, axis=-1, keepdims=True) + NORM_EPS) * g
    return y * (1 + scale) + shift


def _silu(x):
    return x * jax.nn.sigmoid(x)


def _log_sigmoid(z):
    return jnp.minimum(z, 0.0) - jnp.log1p(jnp.exp(-jnp.abs(z)))


ADA_TN = 1536
ADA_ROWS = 8


def _ada_kernel(c_ref, w_ref, b_ref, o_ref):
    cond = _silu(c_ref[...])
    o_ref[0] = jnp.dot(cond, w_ref[0], precision=HI, preferred_element_type=F32) + b_ref[0]


def ada_modulation(c, w_ada, b_ada):
    c8 = jnp.zeros((ADA_ROWS, D_MODEL), F32).at[:BATCH].set(c)
    width = 6 * D_MODEL
    out = pl.pallas_call(
        _ada_kernel,
        grid=(DEPTH, width // ADA_TN),
        in_specs=[
            pl.BlockSpec((ADA_ROWS, D_MODEL), lambda l, j: (0, 0)),
            pl.BlockSpec((1, D_MODEL, ADA_TN), lambda l, j: (l, 0, j)),
            pl.BlockSpec((1, 1, ADA_TN), lambda l, j: (l, 0, j)),
        ],
        out_specs=pl.BlockSpec((1, ADA_ROWS, ADA_TN), lambda l, j: (l, 0, j)),
        out_shape=jax.ShapeDtypeStruct((DEPTH, ADA_ROWS, width), F32),
        compiler_params=_cparams(("arbitrary", "arbitrary")),
        name="ada_modulation",
    )(c8, w_ada, b_ada.reshape(DEPTH, 1, width))
    return out[:, :BATCH].reshape(DEPTH, BATCH, 6, D_MODEL)


INPROJ_TM = 512
INPROJ_WIDTHS = (2 * GLA_QK_W, GLA_V_W, GLA_V_W, NSA_Q_W, 6 * NSA_KV_W, LANES)


def _arrange_w_in(w_in):
    o = np.cumsum((0, GLA_QK_W, GLA_QK_W, GLA_V_W, GLA_GATE_RANK, GLA_V_W, NSA_Q_W, 6 * NSA_KV_W, NSA_HEADS * 3))
    gq_gk = w_in[:, o[0]:o[2]]
    gv = w_in[:, o[2]:o[3]]
    glr = w_in[:, o[3]:o[4]]
    gr = w_in[:, o[4]:o[5]]
    nq = w_in[:, o[5]:o[6]]
    nkv = w_in[:, o[6]:o[7]]
    ng = w_in[:, o[7]:o[8]]
    pad = jnp.zeros((D_MODEL, LANES - GLA_GATE_RANK - NSA_HEADS * 3), w_in.dtype)
    return jnp.concatenate([gq_gk, gv, gr, nq, nkv, glr, ng, pad], axis=1).astype(BF16)


def _inproj0_kernel(x_ref, g_ref, sh_ref, sc_ref, w_ref, *o_refs):
    h = _rms_mod(x_ref[0], g_ref[...], sh_ref[0], sc_ref[0]).astype(BF16)
    off = 0
    for o_ref, wd in zip(o_refs, INPROJ_WIDTHS):
        o_ref[0] = jnp.dot(h, w_ref[:, off:off + wd], preferred_element_type=F32)
        off += wd


def inproj0(x, g, shift, scale, w_arranged):
    tm = INPROJ_TM
    wtot = sum(INPROJ_WIDTHS)
    row = lambda b, i: (b, i, 0)
    vec = lambda b, i: (b, 0, 0)
    return pl.pallas_call(
        _inproj0_kernel,
        grid=(BATCH, SEQ // tm),
        in_specs=[
            pl.BlockSpec((1, tm, D_MODEL), row),
            pl.BlockSpec((1, D_MODEL), lambda b, i: (0, 0)),
            pl.BlockSpec((1, 1, D_MODEL), vec),
            pl.BlockSpec((1, 1, D_MODEL), vec),
            pl.BlockSpec((D_MODEL, wtot), lambda b, i: (0, 0)),
        ],
        out_specs=[pl.BlockSpec((1, tm, wd), row) for wd in INPROJ_WIDTHS],
        out_shape=[jax.ShapeDtypeStruct((BATCH, SEQ, wd), F32) for wd in INPROJ_WIDTHS],
        compiler_params=_cparams(("arbitrary", "arbitrary")),
        name="inproj0",
    )(x, g.reshape(1, D_MODEL), shift.reshape(BATCH, 1, D_MODEL), scale.reshape(BATCH, 1, D_MODEL), w_arranged)


GLA_TG = 512


def _gla_chunk_sums():
    i = np.arange(GLA_TG)[:, None]
    j = np.arange(GLA_TG)[None, :]
    same = (i // GLA_CHUNK) == (j // GLA_CHUNK)
    m3 = np.concatenate([same & (j <= i), same & (j % GLA_CHUNK <= GLA_CHUNK // 2), same], axis=0).astype(np.float32)
    return jnp.asarray(np.concatenate([m3, m3], axis=1), BF16)


def _gla_kernel(qk_ref, v_ref, r_ref, misc_ref, w2_ref, bg_ref, og_ref, sums_ref, o_ref, st_ref):
    C, tg = GLA_CHUNK, GLA_TG

    @pl.when(pl.program_id(0) == 0)
    def _():
        st_ref[...] = jnp.zeros_like(st_ref)

    lane = lax.broadcasted_iota(jnp.int32, (1, GLA_QK_W), 1)
    heads = [(lane >= h * GLA_DK) & (lane < (h + 1) * GLA_DK) for h in range(GLA_HEADS)]
    stack = lambda per_head, rows: jnp.concatenate([t[rows] for t in per_head], axis=0)
    stacked_row = lax.broadcasted_iota(jnp.int32, (GLA_HEADS * C, C), 0)
    causal = (stacked_row & (C - 1)) >= lax.broadcasted_iota(jnp.int32, (GLA_HEADS * C, C), 1)
    og = og_ref[...]

    def prepare(b):
        z = jnp.dot(misc_ref[b], w2_ref[...], precision=HI, preferred_element_type=F32) + bg_ref[...]
        la = _log_sigmoid(z) / GLA_TAU
        la_hi = la.astype(BF16)
        la_lo = (la - la_hi.astype(F32)).astype(BF16)
        sums = jnp.dot(sums_ref[...], jnp.concatenate([la_hi, la_lo], axis=0), preferred_element_type=F32)
        bc, b_mid, b_last = sums[0:tg], sums[tg:2 * tg], sums[2 * tg:3 * tg]
        q = qk_ref[b, :, 0:GLA_QK_W] * (GLA_DK ** -0.5)
        k = qk_ref[b, :, GLA_QK_W:2 * GLA_QK_W]
        qd = q * jnp.exp(bc - b_mid)
        kl = k * jnp.exp(b_last - bc)
        qb = q * jnp.exp(bc)
        per_head = lambda t: [jnp.where(m, t, 0.0).astype(BF16) for m in heads]
        return dict(kd=(k * jnp.exp(b_mid - bc)).astype(BF16), dec=jnp.exp(b_last), qd_h=per_head(qd),
                    qb_h=per_head(qb), kl_h=per_head(kl))

    batches = range(BATCH)
    pre = [prepare(b) for b in batches]
    st = [st_ref[b] for b in batches]
    for c in range(tg // C):
        rows = slice(c * C, (c + 1) * C)
        for b in batches:
            p = pre[b]
            v = v_ref[b, rows, :].astype(BF16)
            s = lax.dot_general(stack(p["qd_h"], rows), p["kd"][rows], NT_DIMS, preferred_element_type=F32)
            s = jnp.where(causal, s, 0.0).astype(BF16)
            o_intra = jnp.dot(s, v, preferred_element_type=F32)
            o_inter = lax.dot_general(stack(p["qb_h"], rows), st[b].astype(BF16), NT_DIMS, preferred_element_type=F32)
            v_stack = jnp.concatenate([v[:, h * GLA_DV:(h + 1) * GLA_DV] for h in range(GLA_HEADS)], axis=0)
            st[b] = st[b] * p["dec"][c * C:c * C + 1] + lax.dot_general(v_stack, stack(p["kl_h"], rows), TN_DIMS,
                                                                       preferred_element_type=F32)
            for h in range(GLA_HEADS):
                hrows = slice(h * C, (h + 1) * C)
                vcols = slice(h * GLA_DV, (h + 1) * GLA_DV)
                o = o_intra[hrows, vcols] + o_inter[hrows]
                on = o * lax.rsqrt(jnp.mean(o * o, axis=-1, keepdims=True) + NORM_EPS) * og
                o_ref[b, rows, vcols] = on * _silu(r_ref[b, rows, vcols])
    for b in batches:
        st_ref[b] = st[b]


def gla_mixer(qk, v, r, misc, w_gate2, b_gate, out_g):
    tg = GLA_TG
    w2 = jnp.zeros((LANES, GLA_QK_W), F32).at[:GLA_GATE_RANK].set(w_gate2)
    row = lambda i: (0, i, 0)
    const = lambda i: (0, 0)
    return pl.pallas_call(
        _gla_kernel,
        grid=(SEQ // tg,),
        in_specs=[
            pl.BlockSpec((BATCH, tg, 2 * GLA_QK_W), row),
            pl.BlockSpec((BATCH, tg, GLA_V_W), row),
            pl.BlockSpec((BATCH, tg, GLA_V_W), row),
            pl.BlockSpec((BATCH, tg, LANES), row),
            pl.BlockSpec((LANES, GLA_QK_W), const),
            pl.BlockSpec((1, GLA_QK_W), const),
            pl.BlockSpec((1, GLA_DV), const),
            pl.BlockSpec((3 * tg, 2 * tg), const),
        ],
        out_specs=pl.BlockSpec((BATCH, tg, GLA_V_W), row),
        out_shape=jax.ShapeDtypeStruct((BATCH, SEQ, GLA_V_W), F32),
        scratch_shapes=[pltpu.VMEM((BATCH, GLA_DV, GLA_QK_W), F32)],
        compiler_params=_cparams(("arbitrary",)),
        name="gla_mixer",
    )(qk, v, r, misc, w2, b_gate.reshape(1, GLA_QK_W), out_g.reshape(1, GLA_DV), _gla_chunk_sums())


POS_SIDE = 128


def _rope_table_kernel(freq_ref, pos_ref, cos_ref, sin_ref):
    pos = pos_ref[...].astype(F32)
    for f in range(ROT_HALF):
        ang = pos * freq_ref[f]
        cos_ref[f] = jnp.cos(ang)
        sin_ref[f] = jnp.sin(ang)


def rope_tables(positions):
    inv_freq = jnp.float32(ROPE_THETA) ** (-jnp.arange(ROT_HALF, dtype=F32) / ROT_HALF)
    shp = jax.ShapeDtypeStruct((ROT_HALF, POS_SIDE, POS_SIDE), F32)
    cos, sin = pl.pallas_call(
        _rope_table_kernel,
        in_specs=[pl.BlockSpec(memory_space=pltpu.SMEM), pl.BlockSpec(memory_space=pltpu.VMEM)],
        out_specs=[pl.BlockSpec(memory_space=pltpu.VMEM)] * 2,
        out_shape=[shp, shp],
        name="rope_tables",
    )(inv_freq, positions.reshape(POS_SIDE, POS_SIDE))
    return jnp.concatenate([cos, sin], axis=0).reshape(ROT_DIM, N_TOK).T.reshape(BATCH, SEQ, ROT_DIM)


def _rope_placement():
    place = np.zeros((ROT_DIM, 3 * LANES), np.float32)
    const = np.zeros((1, 3 * LANES), np.float32)
    for lane in range(LANES):
        i = lane % NSA_DH
        if i < ROT_HALF:
            place[i, lane] = 1.0
            place[ROT_HALF + i, LANES + lane] = -1.0
        elif i < ROT_DIM:
            place[i - ROT_HALF, lane] = 1.0
            place[i, 2 * LANES + lane] = 1.0
        else:
            const[0, lane] = 1.0
    return jnp.asarray(place), jnp.asarray(const)


def _lane_tables(cs, place_ref, const_ref):
    tab = jnp.dot(cs, place_ref[...], precision=HI, preferred_element_type=F32) + const_ref[...]
    return tab[:, 0:LANES], tab[:, LANES:2 * LANES], tab[:, 2 * LANES:3 * LANES]


def _block_diag_ones2(width):
    h = np.arange(width) // NSA_DH
    bd = (h[:, None] == h[None, :]).astype(np.float32)
    return jnp.asarray(np.concatenate([bd, bd], axis=0), BF16)


def _head_norm_rope(x, gain, bd2, c, sm, sp):
    width = x.shape[-1]
    reps = width // LANES
    sq = x * x
    sq_hi = sq.astype(BF16)
    sq_lo = (sq - sq_hi.astype(F32)).astype(BF16)
    ss = jnp.dot(jnp.concatenate([sq_hi, sq_lo], axis=1), bd2, preferred_element_type=F32)
    y = x * lax.rsqrt(ss * (1.0 / NSA_DH) + NORM_EPS) * gain
    tile = lambda t: jnp.concatenate([t] * reps, axis=1) if reps > 1 else t
    return (y * tile(c) + pltpu.roll(y, width - ROT_HALF, 1) * tile(sm) + pltpu.roll(y, ROT_HALF, 1) * tile(sp))


PREP_TM = 512


def _prep_kernel(q_ref, kv_ref, cs_ref, place_ref, const_ref, gq_ref, gk_ref, bdq_ref, bdk_ref,
                 qo_ref, kso_ref, kwo_ref, vso_ref, vwo_ref):
    tm = PREP_TM
    c, sm, sp = _lane_tables(cs_ref[0], place_ref, const_ref)
    bdk = bdk_ref[...]
    q = _head_norm_rope(q_ref[0], gq_ref[...], bdq_ref[...], c, sm, sp) * (NSA_DH ** -0.5)
    qo_ref[0] = q.T.reshape(NSA_HEADS, NSA_DH, tm)
    kv_cols = lambda n: kv_ref[0, :, n * NSA_KV_W:(n + 1) * NSA_KV_W]
    ks = _head_norm_rope(kv_cols(2), gk_ref[0:1, :], bdk, c, sm, sp)
    kw = _head_norm_rope(kv_cols(4), gk_ref[1:2, :], bdk, c, sm, sp)
    lane = lax.broadcasted_iota(jnp.int32, (tm, LANES), 1)
    token = pl.program_id(1) * tm + lax.broadcasted_iota(jnp.int32, (tm, LANES), 0)
    block_col = NSA_DH + jnp.right_shift(token & (SA_TK - 1), 6)
    onehot = jnp.where(lane == block_col, 1.0, 0.0)
    for g in range(NSA_KV_GROUPS):
        to_front = lambda t: t if g == 0 else pltpu.roll(t, NSA_DH, 1)
        kso_ref[0, g] = jnp.where(lane < NSA_DH, to_front(ks), onehot).astype(BF16)
        kwo_ref[0, g] = jnp.where(lane < NSA_DH, to_front(kw), 0.0).astype(BF16)
    tail = jnp.where(lax.broadcasted_iota(jnp.int32, (VT_ROWS - NSA_DH, tm), 0) == 0, 1.0, 0.0)
    for n, out_ref in ((3, vso_ref), (5, vwo_ref)):
        v_t = kv_cols(n).T
        for g in range(NSA_KV_GROUPS):
            out_ref[0, g] = jnp.concatenate([v_t[g * NSA_DH:(g + 1) * NSA_DH], tail], axis=0).astype(BF16)


def nsa_prep(nq, nkv, cs, q_gain, k_gain):
    tm = PREP_TM
    G = NSA_KV_GROUPS
    row = lambda b, i: (b, i, 0)
    const = lambda b, i: (0, 0)
    gq = jnp.tile(q_gain, NSA_HEADS).reshape(1, NSA_Q_W)
    gk = jnp.stack([jnp.tile(k_gain[1], NSA_KV_GROUPS), jnp.tile(k_gain[2], NSA_KV_GROUPS)])
    kslab = pl.BlockSpec((1, G, tm, LANES), lambda b, i: (b, 0, i, 0))
    vslab = pl.BlockSpec((1, G, VT_ROWS, tm), lambda b, i: (b, 0, 0, i))
    return pl.pallas_call(
        _prep_kernel,
        grid=(BATCH, SEQ // tm),
        in_specs=[
            pl.BlockSpec((1, tm, NSA_Q_W), row),
            pl.BlockSpec((1, tm, 6 * NSA_KV_W), row),
            pl.BlockSpec((1, tm, ROT_DIM), row),
            pl.BlockSpec((ROT_DIM, 3 * LANES), const),
            pl.BlockSpec((1, 3 * LANES), const),
            pl.BlockSpec((1, NSA_Q_W), const),
            pl.BlockSpec((2, NSA_KV_W), const),
            pl.BlockSpec((2 * NSA_Q_W, NSA_Q_W), const),
            pl.BlockSpec((2 * NSA_KV_W, NSA_KV_W), const),
        ],
        out_specs=[pl.BlockSpec((1, NSA_HEADS, NSA_DH, tm), lambda b, i: (b, 0, 0, i)), kslab, kslab, vslab, vslab],
        out_shape=[jax.ShapeDtypeStruct((BATCH, NSA_HEADS, NSA_DH, SEQ), F32),
                   jax.ShapeDtypeStruct((BATCH, G, SEQ, LANES), BF16), jax.ShapeDtypeStruct((BATCH, G, SEQ, LANES), BF16),
                   jax.ShapeDtypeStruct((BATCH, G, VT_ROWS, SEQ), BF16), jax.ShapeDtypeStruct((BATCH, G, VT_ROWS, SEQ), BF16)],
        compiler_params=_cparams(("arbitrary", "arbitrary")),
        name="nsa_prep",
    )(nq, nkv, cs, *_rope_placement(), gq, gk, _block_diag_ones2(NSA_Q_W), _block_diag_ones2(NSA_KV_W))


SEG_W = CMP_STRIDE * NSA_DH


def _cmp_kernel(xk_ref, xv_ref, pe_ref, w1_ref, w2_ref, gain_ref, cs_ref, place_ref, const_ref, bd_ref, ko_ref, vo_ref):
    def compress(x_ref, kv):
        out = jnp.zeros((N_CMP_PAD, LANES), F32)
        for g in range(NSA_KV_GROUPS):
            x = x_ref[0, g]
            ha = jnp.dot(x + pe_ref[kv, 0], w1_ref[kv, 0:SEG_W, :], precision=HI, preferred_element_type=F32)
            hb = jnp.dot(x + pe_ref[kv, 1], w1_ref[kv, SEG_W:2 * SEG_W, :], precision=HI, preferred_element_type=F32)
            hid = ha + pltpu.roll(hb, N_CMP_PAD - 1, 0)
            out += jnp.dot(jax.nn.gelu(hid), w2_ref[kv, g], precision=HI, preferred_element_type=F32)
        return out

    c, sm, sp = _lane_tables(cs_ref[0], place_ref, const_ref)
    ko_ref[0] = _head_norm_rope(compress(xk_ref, 0), gain_ref[...], bd_ref[...], c, sm, sp)
    vo_ref[0] = compress(xv_ref, 1)


def nsa_compress(xk, xv, cmp_pe, cmp_w1, cmp_w2, k_gain0, cs_last):
    pe = cmp_pe.reshape(2, 2, 1, SEG_W)
    w2 = jnp.zeros((2, NSA_KV_GROUPS, CMP_HIDDEN, LANES), F32)
    for g in range(NSA_KV_GROUPS):
        w2 = w2.at[:, g, :, g * NSA_DH:(g + 1) * NSA_DH].set(cmp_w2)
    seg = pl.BlockSpec((1, NSA_KV_GROUPS, N_CMP_PAD, SEG_W), lambda b: (b, 0, 0, 0))
    tab = pl.BlockSpec((1, N_CMP_PAD, LANES), lambda b: (b, 0, 0))
    full = lambda shape: pl.BlockSpec(shape, lambda b: (0,) * len(shape))
    return pl.pallas_call(
        _cmp_kernel,
        grid=(BATCH,),
        in_specs=[seg, seg, full((2, 2, 1, SEG_W)), full((2, 2 * SEG_W, CMP_HIDDEN)),
                  full((2, NSA_KV_GROUPS, CMP_HIDDEN, LANES)), full((1, LANES)),
                  pl.BlockSpec((1, N_CMP_PAD, ROT_DIM), lambda b: (b, 0, 0)), full((ROT_DIM, 3 * LANES)),
                  full((1, 3 * LANES)), full((2 * LANES, LANES))],
        out_specs=[tab, tab],
        out_shape=[jax.ShapeDtypeStruct((BATCH, N_CMP_PAD, LANES), F32)] * 2,
        compiler_params=_cparams(("arbitrary",)),
        name="nsa_compress",
    )(xk, xv, pe, cmp_w1, w2, jnp.tile(k_gain0, NSA_KV_GROUPS).reshape(1, LANES), cs_last, *_rope_placement(),
      _block_diag_ones2(LANES))


CA_TQ = 256
SUBLANES = 8


CA_COLS = NSA_HPG * CA_TQ
CMP_PER_SEL = SEL_BLOCK // CMP_STRIDE
TOPK_BANDS = 4


def split3_keys(k):
    hi = k.astype(BF16)
    lo = (k - hi.astype(F32)).astype(BF16)
    return jnp.concatenate([hi, lo, hi], axis=-1)


def _top_k_rows(score, k):
    rows, cols = score.shape
    row = lax.broadcasted_iota(jnp.int32, (rows, cols), 0).astype(F32)
    taken = jnp.zeros((rows, cols), F32)
    left = score
    for _ in range(k):
        top = jnp.max(left, axis=0, keepdims=True)
        first = jnp.min(jnp.where(left == top, row, float(rows)), axis=0, keepdims=True)
        hit = row == first
        taken = jnp.where(hit, 1.0, taken)
        left = jnp.where(hit, -jnp.inf, left)
    return taken


def _cattn_kernel(q_ref, kc_ref, vct_ref, gl_ref, o_ref, sel_ref, q3_ref, ps_ref):
    tq = CA_TQ
    q0 = pl.program_id(1) * tq
    lanes4 = lambda t: jnp.concatenate([t] * NSA_HPG, axis=1)
    cend = lax.broadcasted_iota(jnp.int32, (N_CMP_PAD, tq), 0) * CMP_STRIDE + (CMP_LEN - 1)
    tc = q0 + lax.broadcasted_iota(jnp.int32, (N_CMP_PAD, tq), 1)
    cmask = lanes4(cend <= tc)
    jj = lax.broadcasted_iota(jnp.int32, (N_SEL, tq), 0)
    tt = q0 + lax.broadcasted_iota(jnp.int32, (N_SEL, tq), 1)
    cur = jnp.right_shift(tt, 6)
    forced = (jj == 0) | (jj == cur) | (jj == cur - 1)
    valid = jj * SEL_BLOCK <= tt

    for g in range(NSA_KV_GROUPS):
        heads = range(g * NSA_HPG, (g + 1) * NSA_HPG)
        for n, h in enumerate(heads):
            q = q_ref[0, h]
            hi = q.astype(BF16)
            lo = (q - hi.astype(F32)).astype(BF16)
            for t, part in enumerate((hi, hi, lo)):
                q3_ref[g, t * NSA_DH:(t + 1) * NSA_DH, n * tq:(n + 1) * tq] = part
        s = jnp.dot(kc_ref[0, g], q3_ref[g], preferred_element_type=F32)
        s = jnp.where(cmask, s, NEG_INF)
        m = jnp.max(s, axis=0, keepdims=True)
        e = jnp.where(cmask, jnp.exp(s - m), 0.0)
        l = jnp.sum(e, axis=0, keepdims=True)
        p = e / jnp.where(l > 0.0, l, 1.0)
        gate = jnp.concatenate([jax.nn.sigmoid(gl_ref[0, h, 0:1, :]) for h in heads], axis=1)
        o = jnp.dot(vct_ref[0, g], p.astype(BF16), preferred_element_type=F32) * gate
        for n, h in enumerate(heads):
            o_ref[0, h] = o[:, n * tq:(n + 1) * tq]
        psum = functools.reduce(jnp.add, [p[:, n * tq:(n + 1) * tq] for n in range(NSA_HPG)])
        for n in range(tq // LANES):
            ps_ref[g, n] = psum[:, n * LANES:(n + 1) * LANES]

        every4th = lambda r: jnp.concatenate(
            [ps_ref[g, n, pl.ds(r, N_SEL, stride=CMP_PER_SEL), :] for n in range(tq // LANES)], axis=1)
        starts_in = [every4th(r) for r in range(CMP_PER_SEL)]
        from_prev = jnp.where(jj >= 1, pltpu.roll(starts_in[CMP_PER_SEL - 1], 1, 0), 0.0)
        imp = functools.reduce(jnp.add, starts_in) + from_prev
        score = jnp.where(valid, imp + jnp.where(forced, FORCE_BONUS, 0.0), NEG_INF)
        step = pl.program_id(1)
        steps_per_band = (SEQ // tq) // TOPK_BANDS
        for band in range(TOPK_BANDS):
            n_rows = (band + 1) * (N_SEL // TOPK_BANDS)

            @pl.when((step >= band * steps_per_band) & (step < (band + 1) * steps_per_band))
            def _():
                taken = _top_k_rows(score[0:n_rows], SEL_TOPK)
                sel_ref[0, g, 0:n_rows, :] = jnp.where(valid[0:n_rows], taken, 0.0)
                if n_rows < N_SEL:
                    sel_ref[0, g, n_rows:N_SEL, :] = jnp.zeros((N_SEL - n_rows, tq), F32)


def nsa_cmp_attn(q_t, kcmp, vcmp_t, gl_t):
    tq = CA_TQ
    G = NSA_KV_GROUPS
    return pl.pallas_call(
        _cattn_kernel,
        grid=(BATCH, SEQ // tq),
        in_specs=[
            pl.BlockSpec((1, NSA_HEADS, NSA_DH, tq), lambda b, i: (b, 0, 0, i)),
            pl.BlockSpec((1, G, N_CMP_PAD, 3 * NSA_DH), lambda b, i: (b, 0, 0, 0)),
            pl.BlockSpec((1, G, NSA_DH, N_CMP_PAD), lambda b, i: (b, 0, 0, 0)),
            pl.BlockSpec((1, NSA_HEADS, 3, tq), lambda b, i: (b, 0, 0, i)),
        ],
        out_specs=[pl.BlockSpec((1, NSA_HEADS, NSA_DH, tq), lambda b, i: (b, 0, 0, i)),
                   pl.BlockSpec((1, G, N_SEL, tq), lambda b, i: (b, 0, 0, i))],
        out_shape=[jax.ShapeDtypeStruct((BATCH, NSA_HEADS, NSA_DH, SEQ), F32),
                   jax.ShapeDtypeStruct((BATCH, G, N_SEL, SEQ), F32)],
        scratch_shapes=[pltpu.VMEM((G, 3 * NSA_DH, CA_COLS), BF16), pltpu.VMEM((G, tq // LANES, N_CMP_PAD, LANES), F32)],
        compiler_params=_cparams(("arbitrary", "arbitrary")),
        name="nsa_cmp_attn",
    )(q_t, split3_keys(kcmp), vcmp_t.astype(BF16), gl_t)


SA_TQ = 256
SA_TK = 1024
SA_PARTS = 2
SA_PART = SA_TK // SA_PARTS
M_INIT = -1e20


SA_COLS = NSA_HPG * SA_TQ
SA_BLOCKS = SA_TK // SEL_BLOCK


VT_ROWS = NSA_DH + 16


def _sattn_kernel(q_ref, k_ref, vt_ref, sel_ref, gl_ref, prev_ref, o_ref, qa_ref, acc_ref, s_ref, m_ref):
    tq, tk = SA_TQ, SA_TK
    i = pl.program_id(1)
    groups = range(NSA_KV_GROUPS)
    slots = range(2)
    for g in groups:
        for h in range(NSA_HPG):
            q = q_ref[0, g * NSA_HPG + h].astype(BF16)
            for slot in slots:
                qa_ref[slot, g, 0:NSA_DH, h * tq:(h + 1) * tq] = q
        for slot in slots:
            qa_ref[slot, g, NSA_DH:LANES, :] = jnp.zeros((LANES - NSA_DH, SA_COLS), BF16)
    acc_ref[...] = jnp.zeros_like(acc_ref)
    lanes4 = lambda t: jnp.concatenate([t] * NSA_HPG, axis=1)
    part_keys = lambda kt, part: pl.ds(pl.multiple_of(kt * tk + part * SA_PART, SA_PART), SA_PART)

    def scores(kt, slot):
        for g in groups:
            selrows = sel_ref[0, g, pl.ds(pl.multiple_of(kt * SA_BLOCKS, SA_BLOCKS), SA_BLOCKS), :]
            qa_ref[slot, g, NSA_DH:NSA_DH + SA_BLOCKS, :] = lanes4(jnp.where(selrows > 0.5, 0.0, NEG_INF)).astype(BF16)
            for part in range(SA_PARTS):
                s = jnp.dot(k_ref[0, g, part_keys(kt, part), :], qa_ref[slot, g], preferred_element_type=F32)
                s_ref[slot, g, part] = s.astype(BF16)

    def absorb(kt, slot, ms):
        out = []
        for g in groups:
            ss = [s_ref[slot, g, part] for part in range(SA_PARTS)]
            m_tile = functools.reduce(jnp.maximum, [jnp.max(s, axis=0, keepdims=True) for s in ss])
            m_new = jnp.maximum(ms[g], m_tile.astype(F32))
            acc = jnp.exp(ms[g] - m_new) * acc_ref[g]
            for part in range(SA_PARTS):
                p = jnp.exp(ss[part] - m_new.astype(BF16))
                acc += jnp.dot(vt_ref[0, g, :, part_keys(kt, part)], p, preferred_element_type=F32)
            acc_ref[g] = acc
            out.append(m_new)
        return tuple(out)

    def two_tiles(j, ms):
        kt = 2 * j
        scores(kt + 1, 1)
        ms = absorb(kt, 0, ms)
        scores(kt + 2, 0)
        return absorb(kt + 1, 1, ms)

    n_full = (i * tq) // tk
    scores(0, 0)
    m0 = tuple(jnp.full((1, SA_COLS), M_INIT, F32) for _ in groups)
    ms = lax.fori_loop(0, n_full // 2, two_tiles, m0)
    for g in groups:
        m_ref[g] = ms[g]

    def last_tile(slot):
        start = i * tq - n_full * tk
        part, row0 = start // SA_PART, pl.multiple_of(start % SA_PART, tq)
        tri = lax.broadcasted_iota(jnp.int32, (tq, tq), 0) <= lax.broadcasted_iota(jnp.int32, (tq, tq), 1)
        bias = lanes4(jnp.where(tri, 0.0, NEG_INF)).astype(BF16)
        for g in groups:
            s_ref[slot, g, part, pl.ds(row0, tq), :] += bias
        for g, m in enumerate(absorb(n_full, slot, tuple(m_ref[g] for g in groups))):
            m_ref[g] = m

    @pl.when(n_full % 2 == 0)
    def _():
        last_tile(0)

    @pl.when(n_full % 2 == 1)
    def _():
        scores(n_full, 1)
        for g, m in enumerate(absorb(n_full - 1, 0, tuple(m_ref[g] for g in groups))):
            m_ref[g] = m
        last_tile(1)

    for g in groups:
        heads = range(g * NSA_HPG, (g + 1) * NSA_HPG)
        gate = jnp.concatenate([jax.nn.sigmoid(gl_ref[0, h, 1:2, :]) for h in heads], axis=1)
        out = acc_ref[g, 0:NSA_DH, :] / acc_ref[g, NSA_DH:NSA_DH + 1, :] * gate
        for n, h in enumerate(heads):
            o_ref[0, h] = prev_ref[0, h] + out[:, n * tq:(n + 1) * tq]


def nsa_sel_attn(q_t, k_slab, vsel_t, sel_t, gl_t, prev):
    tq = SA_TQ
    G = NSA_KV_GROUPS
    ospec = pl.BlockSpec((1, NSA_HEADS, NSA_DH, tq), lambda b, i: (b, 0, 0, i))
    return pl.pallas_call(
        _sattn_kernel,
        grid=(BATCH, SEQ // tq),
        in_specs=[
            ospec,
            pl.BlockSpec((1, G, SEQ, LANES), lambda b, i: (b, 0, 0, 0)),
            pl.BlockSpec((1, G, VT_ROWS, SEQ), lambda b, i: (b, 0, 0, 0)),
            pl.BlockSpec((1, G, N_SEL, tq), lambda b, i: (b, 0, 0, i)),
            pl.BlockSpec((1, NSA_HEADS, 3, tq), lambda b, i: (b, 0, 0, i)),
            ospec,
        ],
        out_specs=ospec,
        out_shape=jax.ShapeDtypeStruct((BATCH, NSA_HEADS, NSA_DH, SEQ), F32),
        scratch_shapes=[pltpu.VMEM((2, G, LANES, SA_COLS), BF16), pltpu.VMEM((G, VT_ROWS, SA_COLS), F32),
                        pltpu.VMEM((2, G, SA_PARTS, SA_PART, SA_COLS), BF16), pltpu.VMEM((G, 1, SA_COLS), F32)],
        input_output_aliases={5: 0},
        compiler_params=_cparams(("arbitrary", "arbitrary")),
        name="nsa_sel_attn",
    )(q_t, k_slab, vsel_t, sel_t, gl_t, prev)


WA_TQ = 256
WA_TILES = WINDOW // WA_TQ + 1


def _window_bias():
    kl = np.arange(WA_TILES * WA_TQ)[:, None]
    ql = np.arange(WA_TQ)[None, :]
    diff = ql - kl + WINDOW
    return jnp.asarray(np.where((diff >= 0) & (diff < WINDOW), 0.0, NEG_INF).astype(np.float32))


def _wattn_kernel(q_ref, k0_ref, k1_ref, k2_ref, v0_ref, v1_ref, v2_ref, bias_ref, gl_ref, prev_ref, o_ref):
    tq = WA_TQ
    i = pl.program_id(1)
    k_refs = (k0_ref, k1_ref, k2_ref)
    v_refs = (v0_ref, v1_ref, v2_ref)
    lanes4 = lambda t: jnp.concatenate([t] * NSA_HPG, axis=1)
    biases = []
    for d in range(WA_TILES):
        in_seq = i - (WA_TILES - 1) + d >= 0
        biases.append(lanes4(jnp.where(in_seq, bias_ref[d * tq:(d + 1) * tq, :], NEG_INF)))
    for g in range(NSA_KV_GROUPS):
        heads = range(g * NSA_HPG, (g + 1) * NSA_HPG)
        q = jnp.concatenate([q_ref[0, h] for h in heads], axis=1).astype(BF16)
        q = jnp.concatenate([q, jnp.zeros_like(q)], axis=0)
        ss = [(jnp.dot(k_refs[d][0, g], q, preferred_element_type=F32) + biases[d]).astype(BF16) for d in range(WA_TILES)]
        m = functools.reduce(jnp.maximum, [jnp.max(s, axis=0, keepdims=True) for s in ss])
        acc = functools.reduce(jnp.add, [jnp.dot(v_refs[d][0, g], jnp.exp(ss[d] - m), preferred_element_type=F32)
                                         for d in range(WA_TILES)])
        gate = jnp.concatenate([jax.nn.sigmoid(gl_ref[0, h, 2:3, :]) for h in heads], axis=1)
        out = acc[0:NSA_DH] / acc[NSA_DH:NSA_DH + 1] * gate
        for n, h in enumerate(heads):
            o_ref[0, h] = prev_ref[0, h] + out[:, n * tq:(n + 1) * tq]


def nsa_win_attn(q_t, kwin, vwin_t, gl_t, prev):
    tq = WA_TQ
    G = NSA_KV_GROUPS
    qspec = pl.BlockSpec((1, NSA_HEADS, NSA_DH, tq), lambda b, i: (b, 0, 0, i))
    tile = lambda d: (lambda i: jnp.maximum(i - (WA_TILES - 1) + d, 0))
    kspec = lambda d: pl.BlockSpec((1, G, tq, LANES), lambda b, i: (b, 0, tile(d)(i), 0))
    vspec = lambda d: pl.BlockSpec((1, G, VT_ROWS, tq), lambda b, i: (b, 0, 0, tile(d)(i)))
    return pl.pallas_call(
        _wattn_kernel,
        grid=(BATCH, SEQ // tq),
        in_specs=[qspec] + [kspec(d) for d in range(WA_TILES)] + [vspec(d) for d in range(WA_TILES)] + [
            pl.BlockSpec((WA_TILES * tq, tq), lambda b, i: (0, 0)),
            pl.BlockSpec((1, NSA_HEADS, 3, tq), lambda b, i: (b, 0, 0, i)),
            qspec,
        ],
        out_specs=qspec,
        out_shape=jax.ShapeDtypeStruct((BATCH, NSA_HEADS, NSA_DH, SEQ), F32),
        input_output_aliases={2 * WA_TILES + 3: 0},
        compiler_params=_cparams(("arbitrary", "arbitrary")),
        name="nsa_win_attn",
    )(q_t, *([kwin] * WA_TILES), *([vwin_t] * WA_TILES), _window_bias(), gl_t, prev)


def nsa_mixer(nq, nkv, misc, positions, q_gain, k_gain, cmp_pe, cmp_w1, cmp_w2):
    cs = rope_tables(positions)
    q_t, ksel, kwin, vsel_t, vwin_t = nsa_prep(nq, nkv, cs, q_gain, k_gain)
    group_major = lambda t: t.reshape(BATCH, SEQ, NSA_KV_GROUPS, NSA_DH).transpose(0, 2, 1, 3)
    col = lambda n: nkv[..., n * NSA_KV_W:(n + 1) * NSA_KV_W]
    segs = lambda t: group_major(t).reshape(BATCH, NSA_KV_GROUPS, N_CMP_PAD, SEG_W)
    last = jnp.minimum(jnp.arange(N_CMP_PAD) * CMP_STRIDE + CMP_LEN - 1, SEQ - 1)
    kcmp, vcmp = nsa_compress(segs(col(0)), segs(col(1)), cmp_pe, cmp_w1, cmp_w2, k_gain[0], cs[:, last])
    kcmp = kcmp.reshape(BATCH, N_CMP_PAD, NSA_KV_GROUPS, NSA_DH).transpose(0, 2, 1, 3)
    vcmp_t = vcmp.reshape(BATCH, N_CMP_PAD, NSA_KV_GROUPS, NSA_DH).transpose(0, 2, 3, 1)
    gl_t = misc[..., GLA_GATE_RANK:GLA_GATE_RANK + NSA_HEADS * 3].reshape(BATCH, SEQ, NSA_HEADS, 3).transpose(0, 2, 3, 1)
    o_t, sel_t = nsa_cmp_attn(q_t, kcmp, vcmp_t, gl_t)
    o_t = nsa_sel_attn(q_t, ksel, vsel_t, sel_t, gl_t, o_t)
    return nsa_win_attn(q_t, kwin, vwin_t, gl_t, o_t)


ROUTE_ROWS = 8
HX_W = D_MODEL + 3 * LANES


def _top2_sum(a, b, c, d):
    hi1, lo1 = jnp.maximum(a, b), jnp.minimum(a, b)
    hi2, lo2 = jnp.maximum(c, d), jnp.minimum(c, d)
    return jnp.maximum(hi1, hi2) + jnp.maximum(jnp.minimum(hi1, hi2), jnp.maximum(lo1, lo2))


def _moe_prenorm_route(xn, g_ref, sh_ref, sc_ref, wr_ref, rb_ref, hx_ref, route_ref):
    h = _rms_mod(xn, g_ref[...], sh_ref[0], sc_ref[0])
    logits = lax.dot_general(wr_ref[...], h, NT_DIMS, precision=HI, preferred_element_type=F32)
    scores = jax.nn.sigmoid(logits)
    sel = scores + rb_ref[...]
    epg = EXPERTS_PER_GROUP
    srow = lambda e: sel[e:e + 1, :]
    grp = [_top2_sum(*[srow(epg * g + r) for r in range(epg)]) for g in range(N_EXPERT_GROUPS)]
    best, gi = grp[0], jnp.zeros_like(grp[0], dtype=jnp.int32)
    for g in range(1, N_EXPERT_GROUPS):
        better = grp[g] > best
        gi = jnp.where(better, g, gi)
        best = jnp.where(better, grp[g], best)

    def in_group(mat, r):
        out = mat[r:r + 1, :]
        for g in range(1, N_EXPERT_GROUPS):
            out = jnp.where(gi == g, mat[epg * g + r:epg * g + r + 1, :], out)
        return out

    v = [in_group(sel, r) for r in range(epg)]
    sc = [in_group(scores, r) for r in range(epg)]
    b1, i1, w1 = v[0], jnp.zeros_like(gi), sc[0]
    for r in range(1, epg):
        better = v[r] > b1
        i1 = jnp.where(better, r, i1)
        w1 = jnp.where(better, sc[r], w1)
        b1 = jnp.where(better, v[r], b1)
    b2 = jnp.full_like(b1, -3e38)
    i2, w2 = jnp.zeros_like(gi), jnp.zeros_like(w1)
    for r in range(epg):
        better = (i1 != r) & (v[r] > b2)
        i2 = jnp.where(better, r, i2)
        w2 = jnp.where(better, sc[r], w2)
        b2 = jnp.where(better, v[r], b2)
    tot = w1 + w2
    w1, w2 = w1 / tot, w2 / tot
    zero = jnp.zeros_like(w1)
    route_ref[0] = jnp.concatenate([gi.astype(F32)] + [zero] * (ROUTE_ROWS - 1), axis=0)
    wrows = [jnp.where(i1 == r, w1, jnp.where(i2 == r, w2, 0.0)) for r in range(epg)]
    wmat = jnp.concatenate(wrows + [jnp.zeros((LANES - epg, w1.shape[1]), F32)], axis=0).T
    w_hi = wmat.astype(BF16)
    rest = wmat - w_hi.astype(F32)
    w_mid = rest.astype(BF16)
    w_lo = (rest - w_mid.astype(F32)).astype(BF16)
    hx_ref[0, :, 0:D_MODEL] = h.astype(BF16)
    for n, part in enumerate((w_hi, w_mid, w_lo)):
        hx_ref[0, :, D_MODEL + n * LANES:D_MODEL + (n + 1) * LANES] = part


def _route_specs(tm, row, vec, const):
    in_specs = [pl.BlockSpec((1, D_MODEL), const), pl.BlockSpec((1, 1, D_MODEL), vec), pl.BlockSpec((1, 1, D_MODEL), vec),
                pl.BlockSpec((N_EXPERTS, D_MODEL), const), pl.BlockSpec((N_EXPERTS, 1), const)]
    assert tm == MOE_CHUNK
    out_specs = [pl.BlockSpec((1, tm, HX_W), lambda b, i: (b * (SEQ // tm) + i, 0, 0)),
                 pl.BlockSpec((1, ROUTE_ROWS, tm), lambda b, i: (b, 0, i))]
    out_shape = [jax.ShapeDtypeStruct((MOE_CHUNKS, MOE_CHUNK, HX_W), BF16),
                 jax.ShapeDtypeStruct((BATCH, ROUTE_ROWS, SEQ), F32)]
    return in_specs, out_specs, out_shape


def _route_args(g, shift, scale, w_router, router_bias):
    return (g.reshape(1, D_MODEL), shift.reshape(BATCH, 1, D_MODEL), scale.reshape(BATCH, 1, D_MODEL),
            w_router.T, router_bias.reshape(N_EXPERTS, 1))


OUTPROJ_TM = 512


def _outproj0_kernel(oa_ref, ob_ref, w_ref, x_ref, gate_ref, g_ref, sh_ref, sc_ref, wr_ref, rb_ref,
                     xo_ref, h_ref, route_ref):
    y = jnp.dot(oa_ref[0].astype(BF16), w_ref[0:GLA_V_W, :], preferred_element_type=F32)
    ob_t = ob_ref[0].reshape(NSA_Q_W, OUTPROJ_TM).astype(BF16)
    y += lax.dot_general(ob_t, w_ref[GLA_V_W:GLA_V_W + NSA_Q_W, :], TN_DIMS, preferred_element_type=F32)
    xn = x_ref[0] + gate_ref[0] * y
    xo_ref[0] = xn
    _moe_prenorm_route(xn, g_ref, sh_ref, sc_ref, wr_ref, rb_ref, h_ref, route_ref)


def outproj0(o_a, o_b, w_out, x, gate, route_args):
    tm = OUTPROJ_TM
    row = lambda b, i: (b, i, 0)
    vec = lambda b, i: (b, 0, 0)
    const = lambda b, i: (0, 0)
    r_in, r_out, r_shape = _route_specs(tm, row, vec, const)
    return pl.pallas_call(
        _outproj0_kernel,
        grid=(BATCH, SEQ // tm),
        in_specs=[pl.BlockSpec((1, tm, GLA_V_W), row), pl.BlockSpec((1, NSA_HEADS, NSA_DH, tm), lambda b, i: (b, 0, 0, i)),
                  pl.BlockSpec((GLA_V_W + NSA_Q_W, D_MODEL), const), pl.BlockSpec((1, tm, D_MODEL), row),
                  pl.BlockSpec((1, 1, D_MODEL), vec)] + r_in,
        out_specs=[pl.BlockSpec((1, tm, D_MODEL), row)] + r_out,
        out_shape=[jax.ShapeDtypeStruct((BATCH, SEQ, D_MODEL), F32)] + r_shape,
        compiler_params=_cparams(("arbitrary", "arbitrary")),
        name="outproj0",
    )(o_a, o_b, w_out.astype(BF16), x, gate.reshape(BATCH, 1, D_MODEL), *route_args)


GMLP_TM = 512


def _gmlp_kernel(x_ref, g1_ref, sh1_ref, sc1_ref, win_ref, ng_ref, ws_ref, bs_ref, wout_ref, gate_ref,
                 g_ref, sh_ref, sc_ref, wr_ref, rb_ref, xo_ref, h_ref, route_ref, gated_ref, v_ref):
    x = x_ref[0]
    h = _rms_mod(x, g1_ref[...], sh1_ref[0], sc1_ref[0]).astype(BF16)
    group_cols = lambda g: slice(g * SGU_GROUP_DIM, (g + 1) * SGU_GROUP_DIM)
    ssq = jnp.zeros((GMLP_TM, LANES), F32)
    for g in range(SGU_GROUPS):
        lo = SGU_WIDTH + g * SGU_GROUP_DIM
        v = jax.nn.gelu(jnp.dot(h, win_ref[:, lo:lo + SGU_GROUP_DIM], preferred_element_type=F32))
        v_ref[:, group_cols(g)] = v
        ssq += functools.reduce(jnp.add, [v[:, n * LANES:(n + 1) * LANES] ** 2 for n in range(SGU_GROUP_DIM // LANES)])
    rs = lax.rsqrt(jnp.sum(ssq, axis=-1, keepdims=True) * (1.0 / SGU_WIDTH) + NORM_EPS)
    ri = lax.broadcasted_iota(jnp.int32, (SGU_CHUNK, SGU_CHUNK), 0)
    ci = lax.broadcasted_iota(jnp.int32, (SGU_CHUNK, SGU_CHUNK), 1)
    for g in range(SGU_GROUPS):
        cols = group_cols(g)
        u = jax.nn.gelu(jnp.dot(h, win_ref[:, cols], preferred_element_type=F32))
        vn = (v_ref[:, cols] * rs * ng_ref[:, cols]).astype(BF16)
        w = jnp.where(ri >= ci, ws_ref[g], 0.0).astype(BF16)
        for c in range(GMLP_TM // SGU_CHUNK):
            rows = slice(c * SGU_CHUNK, (c + 1) * SGU_CHUNK)
            mix = jnp.dot(w, vn[rows], preferred_element_type=F32) + bs_ref[:, g:g + 1]
            gated_ref[rows, cols] = (u[rows] * mix).astype(BF16)
    y = jnp.dot(gated_ref[...], wout_ref[...], preferred_element_type=F32)
    xn = x + gate_ref[0] * y
    xo_ref[0] = xn
    _moe_prenorm_route(xn, g_ref, sh_ref, sc_ref, wr_ref, rb_ref, h_ref, route_ref)


def gmlp_layer(x, g1, shift1, scale1, w_in, norm_g, w_s, b_s, w_out, gate, route_args):
    tm = GMLP_TM
    row = lambda b, i: (b, i, 0)
    vec = lambda b, i: (b, 0, 0)
    const = lambda b, i: (0, 0)
    r_in, r_out, r_shape = _route_specs(tm, row, vec, const)
    vspec = pl.BlockSpec((1, 1, D_MODEL), vec)
    return pl.pallas_call(
        _gmlp_kernel,
        grid=(BATCH, SEQ // tm),
        in_specs=[pl.BlockSpec((1, tm, D_MODEL), row), pl.BlockSpec((1, D_MODEL), const), vspec, vspec,
                  pl.BlockSpec((D_MODEL, 2 * SGU_WIDTH), const), pl.BlockSpec((1, SGU_WIDTH), const),
                  pl.BlockSpec((SGU_GROUPS, SGU_CHUNK, SGU_CHUNK), lambda b, i: (0, 0, 0)),
                  pl.BlockSpec((SGU_CHUNK, SGU_GROUPS), const), pl.BlockSpec((SGU_WIDTH, D_MODEL), const), vspec] + r_in,
        out_specs=[pl.BlockSpec((1, tm, D_MODEL), row)] + r_out,
        out_shape=[jax.ShapeDtypeStruct((BATCH, SEQ, D_MODEL), F32)] + r_shape,
        scratch_shapes=[pltpu.VMEM((tm, SGU_WIDTH), BF16), pltpu.VMEM((tm, SGU_WIDTH), F32)],
        compiler_params=_cparams(("arbitrary", "arbitrary"), vmem_mib=56),
        name="gmlp_layer",
    )(x, g1.reshape(1, D_MODEL), shift1.reshape(BATCH, 1, D_MODEL), scale1.reshape(BATCH, 1, D_MODEL),
      w_in.astype(BF16), norm_g.reshape(1, SGU_WIDTH), w_s, b_s.T, w_out.astype(BF16),
      gate.reshape(BATCH, 1, D_MODEL), *route_args)


MOE_TM = 256
MOE_CHUNK = 512
MOE_SORTED = N_TOK + N_EXPERT_GROUPS * MOE_TM
MOE_TILES = MOE_SORTED // MOE_TM
MOE_CHUNKS = N_TOK // MOE_CHUNK
MOE_PAIRS = MOE_TILES + N_EXPERT_GROUPS * MOE_CHUNKS
FLAG_ACTIVE, FLAG_FIRST, FLAG_LAST, FLAG_ZERO = 1, 2, 4, 8
EXP_WIN = 4
CMB_WIN = 8
MOE_TSTEPS = MOE_TILES + MOE_PAIRS // EXP_WIN
MOE_CSTEPS = MOE_CHUNKS + MOE_PAIRS // CMB_WIN


def _plan_kernel(gi_ref, rank_ref, before_ref):
    gi = gi_ref[...]
    r = lax.broadcasted_iota(jnp.int32, (POS_SIDE, POS_SIDE), 0)
    c = lax.broadcasted_iota(jnp.int32, (POS_SIDE, POS_SIDE), 1)
    upper = jnp.where(r <= c, 1.0, 0.0)
    lower_strict = jnp.where(c < r, 1.0, 0.0)
    rank = jnp.zeros((POS_SIDE, POS_SIDE), F32)
    for g in range(N_EXPERT_GROUPS):
        member = jnp.where(gi == g, 1.0, 0.0)
        in_row = jnp.dot(member, upper, precision=HI, preferred_element_type=F32)
        row_total = jnp.broadcast_to(in_row[:, POS_SIDE - 1:POS_SIDE], (POS_SIDE, POS_SIDE))
        before = jnp.dot(lower_strict, row_total, precision=HI, preferred_element_type=F32)
        before_ref[g] = before
        rank += member * (before + in_row - 1.0)
    rank_ref[...] = rank


def moe_plan(route):
    tm = MOE_TM
    i32 = jnp.int32
    gi_f = route[:, 0, :].reshape(POS_SIDE, POS_SIDE)
    rank, before = pl.pallas_call(
        _plan_kernel,
        out_shape=[jax.ShapeDtypeStruct((POS_SIDE, POS_SIDE), F32),
                   jax.ShapeDtypeStruct((N_EXPERT_GROUPS, POS_SIDE, POS_SIDE), F32)],
        name="moe_plan",
    )(gi_f)
    gi = gi_f.reshape(N_TOK).astype(i32)
    groups = jnp.arange(N_EXPERT_GROUPS, dtype=i32)
    member = gi[None, :] == groups[:, None]
    tot = jnp.sum(member, axis=1).astype(i32)
    padded = (tot + tm - 1) // tm * tm
    gend = jnp.cumsum(padded).astype(i32)
    gstart = gend - padded
    pos = jnp.sum(jnp.where(member, gstart[:, None], 0), axis=0).astype(i32) + rank.reshape(N_TOK).astype(i32)
    rows_per_chunk = MOE_CHUNK // POS_SIDE
    cnt_end = jnp.concatenate([before[:, rows_per_chunk::rows_per_chunk, 0].astype(i32), tot[:, None]], axis=1)
    t = jnp.arange(MOE_TILES, dtype=i32)
    n_used = gend[-1] // tm
    tile_g = jnp.minimum(jnp.sum(gend[None, :] <= (t * tm)[:, None], axis=1), N_EXPERT_GROUPS - 1).astype(i32)
    k0 = t * tm - gstart[tile_g]
    k1 = jnp.minimum(k0 + tm, tot[tile_g]) - 1
    ce = cnt_end[tile_g]
    c_lo = jnp.sum(ce <= k0[:, None], axis=1).astype(i32)
    c_hi = jnp.sum(ce <= k1[:, None], axis=1).astype(i32)
    npairs = jnp.where(t < n_used, c_hi - c_lo + 1, 0)
    pend = jnp.cumsum(npairs).astype(i32)
    pstart = pend - npairs
    total = pend[-1]
    l = jnp.arange(MOE_PAIRS, dtype=i32)
    real = l < total
    lt = jnp.minimum(l, total - 1)

    def windows(count, win, n_steps):
        per_item = (count + win - 1) // win
        end = jnp.cumsum(per_item).astype(i32)
        start = end - per_item
        s = jnp.arange(n_steps, dtype=i32)
        real_s = s < end[-1]
        sc = jnp.minimum(s, end[-1] - 1)
        item = jnp.sum(end[None, :] <= sc[:, None], axis=1).astype(i32)
        j = sc - start[item]
        flags_s = jnp.where(real_s, FLAG_ACTIVE + jnp.where(j == 0, FLAG_FIRST, 0)
                            + jnp.where(j == per_item[item] - 1, FLAG_LAST, 0), 0).astype(i32)
        return item, j, flags_s, real_s, s - end[-1]

    tile_s, j, flags, real_s, spare = windows(npairs, EXP_WIN, MOE_TSTEPS)
    c0 = c_lo[tile_s] + EXP_WIN * j
    n_valid = jnp.minimum(EXP_WIN, c_hi[tile_s] - c0 + 1).astype(i32)
    spare_tile = jnp.minimum(n_used + spare, MOE_TILES - 1)
    flags = jnp.where(real_s, flags, jnp.where(spare_tile >= n_used, FLAG_ZERO, 0)).astype(i32)
    tile_sched = jnp.where(real_s, tile_s, spare_tile).astype(i32)
    by_tile = (tile_sched, c0.astype(i32), n_valid, flags, tile_g[tile_sched])
    cc = jnp.arange(MOE_CHUNKS, dtype=i32)
    is_pair = (cc[:, None] >= c_lo[None, :]) & (cc[:, None] <= c_hi[None, :]) & (t[None, :] < n_used)
    seen = jnp.cumsum(is_pair.reshape(-1).astype(i32))
    flat = jnp.sum(seen[None, :] <= lt[:, None], axis=1).astype(i32)
    pair_tile = flat % MOE_TILES
    per_chunk = jnp.sum(is_pair, axis=1).astype(i32)
    first_pair = jnp.cumsum(per_chunk).astype(i32) - per_chunk
    chunk_s, j, flags_c, _, _ = windows(per_chunk, CMB_WIN, MOE_CSTEPS)
    base = first_pair[chunk_s] + CMB_WIN * j
    n_valid_c = jnp.minimum(CMB_WIN, per_chunk[chunk_s] - CMB_WIN * j).astype(i32)
    tiles_c = tuple(pair_tile[jnp.minimum(base + w, total - 1)] for w in range(CMB_WIN))
    by_chunk = (chunk_s, n_valid_c, flags_c) + tiles_c
    return pos.reshape(MOE_CHUNKS, 1, MOE_CHUNK), by_tile, by_chunk


def _one_hot_rows(pos_row, tile):
    rows = tile * MOE_TM + lax.broadcasted_iota(jnp.int32, (MOE_TM, MOE_CHUNK), 0)
    return jnp.where(pos_row == rows, 1.0, 0.0).astype(BF16)


def _moe_kernel(tile_ref, c0_ref, nv_ref, flag_ref, grp_ref, *refs):
    pos_refs, hx_refs = refs[0:EXP_WIN], refs[EXP_WIN:2 * EXP_WIN]
    wg_ref, wu_ref, wd_ref, y_ref, acc_ref = refs[2 * EXP_WIN:]
    l = pl.program_id(0)
    flags = flag_ref[l]

    @pl.when((flags & FLAG_FIRST) != 0)
    def _():
        acc_ref[...] = jnp.zeros_like(acc_ref)

    for w in range(EXP_WIN):
        @pl.when(((flags & FLAG_ACTIVE) != 0) & (w < nv_ref[l]))
        def _():
            onehot = _one_hot_rows(pos_refs[w][0], tile_ref[l])
            acc_ref[...] += jnp.dot(onehot, hx_refs[w][0], preferred_element_type=F32)

    @pl.when((flags & FLAG_LAST) != 0)
    def _():
        x = acc_ref[:, 0:D_MODEL].astype(BF16)
        w = functools.reduce(jnp.add, [acc_ref[:, D_MODEL + n * LANES:D_MODEL + (n + 1) * LANES] for n in range(3)])
        y = jnp.zeros((MOE_TM, D_MODEL), F32)
        for r in range(EXPERTS_PER_GROUP):
            gate = jnp.dot(x, wg_ref[0, 0, r], preferred_element_type=F32)
            up = jnp.dot(x, wu_ref[0, 0, r], preferred_element_type=F32)
            hid = (_silu(gate) * up * w[:, r:r + 1]).astype(BF16)
            y += jnp.dot(hid, wd_ref[0, 0, r], preferred_element_type=F32)
        y_ref[...] = y

    @pl.when((flags & FLAG_ZERO) != 0)
    def _():
        y_ref[...] = jnp.zeros_like(y_ref)


def moe_experts(hx, pos, by_tile, w_gate, w_up, w_down, layer):
    grouped = lambda w: w.reshape(DEPTH, N_EXPERT_GROUPS, EXPERTS_PER_GROUP, *w.shape[2:])
    wspec = lambda k, n: pl.BlockSpec((1, 1, EXPERTS_PER_GROUP, k, n), lambda l, t, c, n_, f, g: (layer, g[l], 0, 0, 0))
    chunk = lambda w: (lambda l, t, c, n_, f, g: (jnp.minimum(c[l] + w, MOE_CHUNKS - 1), 0, 0))
    return pl.pallas_call(
        _moe_kernel,
        grid_spec=pltpu.PrefetchScalarGridSpec(
            num_scalar_prefetch=5,
            grid=(MOE_TSTEPS,),
            in_specs=[pl.BlockSpec((1, 1, MOE_CHUNK), chunk(w)) for w in range(EXP_WIN)]
            + [pl.BlockSpec((1, MOE_CHUNK, HX_W), chunk(w)) for w in range(EXP_WIN)]
            + [wspec(D_MODEL, EXPERT_HIDDEN), wspec(D_MODEL, EXPERT_HIDDEN), wspec(EXPERT_HIDDEN, D_MODEL)],
            out_specs=pl.BlockSpec((MOE_TM, D_MODEL), lambda l, t, c, n_, f, g: (t[l], 0)),
            scratch_shapes=[pltpu.VMEM((MOE_TM, HX_W), F32)],
        ),
        out_shape=jax.ShapeDtypeStruct((MOE_SORTED, D_MODEL), F32),
        compiler_params=_cparams(("arbitrary",), vmem_mib=56),
        name="moe_experts",
    )(*by_tile, *([pos] * EXP_WIN), *([hx] * EXP_WIN), grouped(w_gate), grouped(w_up), grouped(w_down))


def _moe_combine_kernel(chunk_ref, nv_ref, flag_ref, *refs):
    tile_refs = refs[0:CMB_WIN]
    pos_ref = refs[CMB_WIN]
    y_refs = refs[CMB_WIN + 1:2 * CMB_WIN + 1]
    x_ref, gate_ref, o_ref, acc_ref = refs[2 * CMB_WIN + 1:]
    l = pl.program_id(0)
    flags = flag_ref[l]

    @pl.when((flags & FLAG_FIRST) != 0)
    def _():
        acc_ref[...] = jnp.zeros_like(acc_ref)

    for w in range(CMB_WIN):
        @pl.when(((flags & FLAG_ACTIVE) != 0) & (w < nv_ref[l]))
        def _():
            onehot = _one_hot_rows(pos_ref[0], tile_refs[w][l])
            y = y_refs[w][...]
            y_hi = y.astype(BF16)
            y_lo = (y - y_hi.astype(F32)).astype(BF16)
            acc_ref[...] += (lax.dot_general(onehot, y_hi, TN_DIMS, preferred_element_type=F32)
                             + lax.dot_general(onehot, y_lo, TN_DIMS, preferred_element_type=F32))

    @pl.when((flags & FLAG_LAST) != 0)
    def _():
        o_ref[0] = x_ref[0] + gate_ref[0] * acc_ref[...]


def moe_combine(x, y_sorted, pos, by_chunk, gate):
    per_b = SEQ // MOE_CHUNK
    tok = lambda l, c, *_: (c[l] // per_b, c[l] % per_b, 0)
    tile = lambda w: (lambda l, c, n_, f, *tiles: (tiles[w][l], 0))
    return pl.pallas_call(
        _moe_combine_kernel,
        grid_spec=pltpu.PrefetchScalarGridSpec(
            num_scalar_prefetch=3 + CMB_WIN,
            grid=(MOE_CSTEPS,),
            in_specs=[pl.BlockSpec((1, 1, MOE_CHUNK), lambda l, c, *_: (c[l], 0, 0))]
            + [pl.BlockSpec((MOE_TM, D_MODEL), tile(w)) for w in range(CMB_WIN)]
            + [pl.BlockSpec((1, MOE_CHUNK, D_MODEL), tok),
               pl.BlockSpec((1, 1, D_MODEL), lambda l, c, *_: (c[l] // per_b, 0, 0))],
            out_specs=pl.BlockSpec((1, MOE_CHUNK, D_MODEL), tok),
            scratch_shapes=[pltpu.VMEM((MOE_CHUNK, D_MODEL), F32)],
        ),
        out_shape=jax.ShapeDtypeStruct((BATCH, SEQ, D_MODEL), F32),
        compiler_params=_cparams(("arbitrary",)),
        name="moe_combine",
    )(*by_chunk, pos, *([y_sorted] * CMB_WIN), x, gate.reshape(BATCH, 1, D_MODEL))


def moe_layer(x, hx, route, gate, w_gate, w_up, w_down, layer):
    pos, by_tile, by_chunk = moe_plan(route)
    y_sorted = moe_experts(hx, pos, by_tile, w_gate, w_up, w_down, layer)
    return moe_combine(x, y_sorted, pos, by_chunk, gate)


def kernel(x, c, positions, w_ada, b_ada, norm_g, w_in_ab, w_out_ab, gla_w_gate2, gla_b_gate, gla_norm_g, nsa_q_gain, nsa_k_gain, nsa_cmp_pe, nsa_cmp_w1, nsa_cmp_w2, w_in_c, sgu_norm_g, sgu_w_s, sgu_b_s, w_out_c, w_router, router_bias, w_gate, w_up, w_down):
    mod = ada_modulation(c, w_ada, b_ada)
    qk, gv, gr, nq, nkv, misc = inproj0(x, norm_g[0, 0], mod[0, :, 0], mod[0, :, 1], _arrange_w_in(w_in_ab[0]))
    o_a = gla_mixer(qk, gv, gr, misc, gla_w_gate2[0], gla_b_gate[0], gla_norm_g[0])
    o_b = nsa_mixer(nq, nkv, misc, positions, nsa_q_gain[0], nsa_k_gain[0], nsa_cmp_pe[0], nsa_cmp_w1[0], nsa_cmp_w2[0])
    wg, wu, wd = w_gate.astype(BF16), w_up.astype(BF16), w_down.astype(BF16)
    route_args = lambda l: _route_args(norm_g[l, 1], mod[l, :, 3], mod[l, :, 4], w_router, router_bias)
    x1, h, route = outproj0(o_a, o_b, w_out_ab[0], x, mod[0, :, 2], route_args(0))
    x2 = moe_layer(x1, h, route, mod[0, :, 5], wg, wu, wd, 0)
    x3, h, route = gmlp_layer(x2, norm_g[1, 0], mod[1, :, 0], mod[1, :, 1], w_in_c[0], sgu_norm_g[0], sgu_w_s[0],
                              sgu_b_s[0], w_out_c[0], mod[1, :, 2], route_args(1))
    return moe_layer(x3, h, route, mod[1, :, 5], wg, wu, wd, 1)
```

```python
import functools

import numpy as np
import jax
import jax.numpy as jnp
from jax import lax
from jax.experimental import pallas as pl
from jax.experimental.pallas import tpu as pltpu

D_MODEL = 1024
BATCH = 2
SEQ = 8192
DEPTH = 2
N_TOK = BATCH * SEQ

GLA_HEADS = 4
GLA_DK = 64
GLA_DV = 128
GLA_GATE_RANK = 16
GLA_TAU = 16.0
GLA_CHUNK = 64
NSA_HEADS = 8
NSA_KV_GROUPS = 2
NSA_HPG = NSA_HEADS // NSA_KV_GROUPS
NSA_DH = 64
CMP_LEN = 32
CMP_STRIDE = 16
CMP_HIDDEN = 256
SEL_BLOCK = 64
SEL_TOPK = 16
WINDOW = 512
ROPE_THETA = 500000.0
ROT_DIM = NSA_DH // 4
ROT_HALF = ROT_DIM // 2
SGU_CHUNK = 128
SGU_GROUPS = 8
SGU_WIDTH = 2048
SGU_GROUP_DIM = SGU_WIDTH // SGU_GROUPS
N_EXPERTS = 16
N_EXPERT_GROUPS = 4
EXPERTS_PER_GROUP = N_EXPERTS // N_EXPERT_GROUPS
MOE_TOPK = 2
EXPERT_HIDDEN = 512

GLA_QK_W = GLA_HEADS * GLA_DK
GLA_V_W = GLA_HEADS * GLA_DV
NSA_Q_W = NSA_HEADS * NSA_DH
NSA_KV_W = NSA_KV_GROUPS * NSA_DH
N_CMP = (SEQ - CMP_LEN) // CMP_STRIDE + 1
N_CMP_PAD = SEQ // CMP_STRIDE
N_SEL = SEQ // SEL_BLOCK

NORM_EPS = 1e-6
NEG_INF = -1e30
FORCE_BONUS = 1e4

LANES = 128
MIB = 1024 * 1024

F32 = jnp.float32
BF16 = jnp.bfloat16
HI = lax.Precision.HIGHEST
NT_DIMS = (((1,), (1,)), ((), ()))
TN_DIMS = (((0,), (0,)), ((), ()))


def _cparams(sem, vmem_mib=48):
    return pltpu.CompilerParams(dimension_semantics=sem, vmem_limit_bytes=vmem_mib * MIB)


def _rms_mod(x, g, shift, scale):
    y = x * lax.rsqrt(jnp.mean(x * x, axis=-1, keepdims=True) + NORM_EPS) * g
    return y * (1 + scale) + shift


def _silu(x):
    return x * jax.nn.sigmoid(x)


def _log_sigmoid(z):
    return jnp.minimum(z, 0.0) - jnp.log1p(jnp.exp(-jnp.abs(z)))


ADA_TN = 1536
ADA_ROWS = 8


def _ada_kernel(c_ref, w_ref, b_ref, o_ref):
    cond = _silu(c_ref[...])
    o_ref[0] = jnp.dot(cond, w_ref[0], precision=HI, preferred_element_type=F32) + b_ref[0]


def ada_modulation(c, w_ada, b_ada):
    c8 = jnp.zeros((ADA_ROWS, D_MODEL), F32).at[:BATCH].set(c)
    width = 6 * D_MODEL
    out = pl.pallas_call(
        _ada_kernel,
        grid=(DEPTH, width // ADA_TN),
        in_specs=[
            pl.BlockSpec((ADA_ROWS, D_MODEL), lambda l, j: (0, 0)),
            pl.BlockSpec((1, D_MODEL, ADA_TN), lambda l, j: (l, 0, j)),
            pl.BlockSpec((1, 1, ADA_TN), lambda l, j: (l, 0, j)),
        ],
        out_specs=pl.BlockSpec((1, ADA_ROWS, ADA_TN), lambda l, j: (l, 0, j)),
        out_shape=jax.ShapeDtypeStruct((DEPTH, ADA_ROWS, width), F32),
        compiler_params=_cparams(("arbitrary", "arbitrary")),
        name="ada_modulation",
    )(c8, w_ada, b_ada.reshape(DEPTH, 1, width))
    return out[:, :BATCH].reshape(DEPTH, BATCH, 6, D_MODEL)


INPROJ_TM = 512
INPROJ_WIDTHS = (2 * GLA_QK_W, GLA_V_W, GLA_V_W, NSA_Q_W, 6 * NSA_KV_W, LANES)


def _arrange_w_in(w_in):
    o = np.cumsum((0, GLA_QK_W, GLA_QK_W, GLA_V_W, GLA_GATE_RANK, GLA_V_W, NSA_Q_W, 6 * NSA_KV_W, NSA_HEADS * 3))
    gq_gk = w_in[:, o[0]:o[2]]
    gv = w_in[:, o[2]:o[3]]
    glr = w_in[:, o[3]:o[4]]
    gr = w_in[:, o[4]:o[5]]
    nq = w_in[:, o[5]:o[6]]
    nkv = w_in[:, o[6]:o[7]]
    ng = w_in[:, o[7]:o[8]]
    pad = jnp.zeros((D_MODEL, LANES - GLA_GATE_RANK - NSA_HEADS * 3), w_in.dtype)
    return jnp.concatenate([gq_gk, gv, gr, nq, nkv, glr, ng, pad], axis=1).astype(BF16)


def _inproj0_kernel(x_ref, g_ref, sh_ref, sc_ref, w_ref, *o_refs):
    h = _rms_mod(x_ref[0], g_ref[...], sh_ref[0], sc_ref[0]).astype(BF16)
    off = 0
    for o_ref, wd in zip(o_refs, INPROJ_WIDTHS):
        o_ref[0] = jnp.dot(h, w_ref[:, off:off + wd], preferred_element_type=F32)
        off += wd


def inproj0(x, g, shift, scale, w_arranged):
    tm = INPROJ_TM
    wtot = sum(INPROJ_WIDTHS)
    row = lambda b, i: (b, i, 0)
    vec = lambda b, i: (b, 0, 0)
    return pl.pallas_call(
        _inproj0_kernel,
        grid=(BATCH, SEQ // tm),
        in_specs=[
            pl.BlockSpec((1, tm, D_MODEL), row),
            pl.BlockSpec((1, D_MODEL), lambda b, i: (0, 0)),
            pl.BlockSpec((1, 1, D_MODEL), vec),
            pl.BlockSpec((1, 1, D_MODEL), vec),
            pl.BlockSpec((D_MODEL, wtot), lambda b, i: (0, 0)),
        ],
        out_specs=[pl.BlockSpec((1, tm, wd), row) for wd in INPROJ_WIDTHS],
        out_shape=[jax.ShapeDtypeStruct((BATCH, SEQ, wd), F32) for wd in INPROJ_WIDTHS],
        compiler_params=_cparams(("arbitrary", "arbitrary")),
        name="inproj0",
    )(x, g.reshape(1, D_MODEL), shift.reshape(BATCH, 1, D_MODEL), scale.reshape(BATCH, 1, D_MODEL), w_arranged)


GLA_TG = 512


def _gla_chunk_sums():
    i = np.arange(GLA_TG)[:, None]
    j = np.arange(GLA_TG)[None, :]
    same = (i // GLA_CHUNK) == (j // GLA_CHUNK)
    m3 = np.concatenate([same & (j <= i), same & (j % GLA_CHUNK <= GLA_CHUNK // 2), same], axis=0).astype(np.float32)
    return jnp.asarray(np.concatenate([m3, m3], axis=1), BF16)


def _gla_kernel(qk_ref, v_ref, r_ref, misc_ref, w2_ref, bg_ref, og_ref, sums_ref, o_ref, st_ref):
    C, tg = GLA_CHUNK, GLA_TG

    @pl.when(pl.program_id(0) == 0)
    def _():
        st_ref[...] = jnp.zeros_like(st_ref)

    lane = lax.broadcasted_iota(jnp.int32, (1, GLA_QK_W), 1)
    heads = [(lane >= h * GLA_DK) & (lane < (h + 1) * GLA_DK) for h in range(GLA_HEADS)]
    stack = lambda per_head, rows: jnp.concatenate([t[rows] for t in per_head], axis=0)
    stacked_row = lax.broadcasted_iota(jnp.int32, (GLA_HEADS * C, C), 0)
    causal = (stacked_row & (C - 1)) >= lax.broadcasted_iota(jnp.int32, (GLA_HEADS * C, C), 1)
    og = og_ref[...]

    def prepare(b):
        z = jnp.dot(misc_ref[b], w2_ref[...], precision=HI, preferred_element_type=F32) + bg_ref[...]
        la = _log_sigmoid(z) / GLA_TAU
        la_hi = la.astype(BF16)
        la_lo = (la - la_hi.astype(F32)).astype(BF16)
        sums = jnp.dot(sums_ref[...], jnp.concatenate([la_hi, la_lo], axis=0), preferred_element_type=F32)
        bc, b_mid, b_last = sums[0:tg], sums[tg:2 * tg], sums[2 * tg:3 * tg]
        q = qk_ref[b, :, 0:GLA_QK_W] * (GLA_DK ** -0.5)
        k = qk_ref[b, :, GLA_QK_W:2 * GLA_QK_W]
        qd = q * jnp.exp(bc - b_mid)
        kl = k * jnp.exp(b_last - bc)
        qb = q * jnp.exp(bc)
        per_head = lambda t: [jnp.where(m, t, 0.0).astype(BF16) for m in heads]
        return dict(kd=(k * jnp.exp(b_mid - bc)).astype(BF16), dec=jnp.exp(b_last), qd_h=per_head(qd),
                    qb_h=per_head(qb), kl_h=per_head(kl))

    batches = range(BATCH)
    pre = [prepare(b) for b in batches]
    st = [st_ref[b] for b in batches]
    for c in range(tg // C):
        rows = slice(c * C, (c + 1) * C)
        for b in batches:
            p = pre[b]
            v = v_ref[b, rows, :].astype(BF16)
            s = lax.dot_general(stack(p["qd_h"], rows), p["kd"][rows], NT_DIMS, preferred_element_type=F32)
            s = jnp.where(causal, s, 0.0).astype(BF16)
            o_intra = jnp.dot(s, v, preferred_element_type=F32)
            o_inter = lax.dot_general(stack(p["qb_h"], rows), st[b].astype(BF16), NT_DIMS, preferred_element_type=F32)
            v_stack = jnp.concatenate([v[:, h * GLA_DV:(h + 1) * GLA_DV] for h in range(GLA_HEADS)], axis=0)
            st[b] = st[b] * p["dec"][c * C:c * C + 1] + lax.dot_general(v_stack, stack(p["kl_h"], rows), TN_DIMS,
                                                                       preferred_element_type=F32)
            for h in range(GLA_HEADS):
                hrows = slice(h * C, (h + 1) * C)
                vcols = slice(h * GLA_DV, (h + 1) * GLA_DV)
                o = o_intra[hrows, vcols] + o_inter[hrows]
                on = o * lax.rsqrt(jnp.mean(o * o, axis=-1, keepdims=True) + NORM_EPS) * og
                o_ref[b, rows, vcols] = on * _silu(r_ref[b, rows, vcols])
    for b in batches:
        st_ref[b] = st[b]


def gla_mixer(qk, v, r, misc, w_gate2, b_gate, out_g):
    tg = GLA_TG
    w2 = jnp.zeros((LANES, GLA_QK_W), F32).at[:GLA_GATE_RANK].set(w_gate2)
    row = lambda i: (0, i, 0)
    const = lambda i: (0, 0)
    return pl.pallas_call(
        _gla_kernel,
        grid=(SEQ // tg,),
        in_specs=[
            pl.BlockSpec((BATCH, tg, 2 * GLA_QK_W), row),
            pl.BlockSpec((BATCH, tg, GLA_V_W), row),
            pl.BlockSpec((BATCH, tg, GLA_V_W), row),
            pl.BlockSpec((BATCH, tg, LANES), row),
            pl.BlockSpec((LANES, GLA_QK_W), const),
            pl.BlockSpec((1, GLA_QK_W), const),
            pl.BlockSpec((1, GLA_DV), const),
            pl.BlockSpec((3 * tg, 2 * tg), const),
        ],
        out_specs=pl.BlockSpec((BATCH, tg, GLA_V_W), row),
        out_shape=jax.ShapeDtypeStruct((BATCH, SEQ, GLA_V_W), F32),
        scratch_shapes=[pltpu.VMEM((BATCH, GLA_DV, GLA_QK_W), F32)],
        compiler_params=_cparams(("arbitrary",)),
        name="gla_mixer",
    )(qk, v, r, misc, w2, b_gate.reshape(1, GLA_QK_W), out_g.reshape(1, GLA_DV), _gla_chunk_sums())


POS_SIDE = 128


def _rope_table_kernel(freq_ref, pos_ref, cos_ref, sin_ref):
    pos = pos_ref[...].astype(F32)
    for f in range(ROT_HALF):
        ang = pos * freq_ref[f]
        cos_ref[f] = jnp.cos(ang)
        sin_ref[f] = jnp.sin(ang)


def rope_tables(positions):
    inv_freq = jnp.float32(ROPE_THETA) ** (-jnp.arange(ROT_HALF, dtype=F32) / ROT_HALF)
    shp = jax.ShapeDtypeStruct((ROT_HALF, POS_SIDE, POS_SIDE), F32)
    cos, sin = pl.pallas_call(
        _rope_table_kernel,
        in_specs=[pl.BlockSpec(memory_space=pltpu.SMEM), pl.BlockSpec(memory_space=pltpu.VMEM)],
        out_specs=[pl.BlockSpec(memory_space=pltpu.VMEM)] * 2,
        out_shape=[shp, shp],
        name="rope_tables",
    )(inv_freq, positions.reshape(POS_SIDE, POS_SIDE))
    return jnp.concatenate([cos, sin], axis=0).reshape(ROT_DIM, N_TOK).T.reshape(BATCH, SEQ, ROT_DIM)


def _rope_placement():
    place = np.zeros((ROT_DIM, 3 * LANES), np.float32)
    const = np.zeros((1, 3 * LANES), np.float32)
    for lane in range(LANES):
        i = lane % NSA_DH
        if i < ROT_HALF:
            place[i, lane] = 1.0
            place[ROT_HALF + i, LANES + lane] = -1.0
        elif i < ROT_DIM:
            place[i - ROT_HALF, lane] = 1.0
            place[i, 2 * LANES + lane] = 1.0
        else:
            const[0, lane] = 1.0
    return jnp.asarray(place), jnp.asarray(const)


def _lane_tables(cs, place_ref, const_ref):
    tab = jnp.dot(cs, place_ref[...], precision=HI, preferred_element_type=F32) + const_ref[...]
    return tab[:, 0:LANES], tab[:, LANES:2 * LANES], tab[:, 2 * LANES:3 * LANES]


def _block_diag_ones2(width):
    h = np.arange(width) // NSA_DH
    bd = (h[:, None] == h[None, :]).astype(np.float32)
    return jnp.asarray(np.concatenate([bd, bd], axis=0), BF16)


def _head_norm_rope(x, gain, bd2, c, sm, sp):
    width = x.shape[-1]
    reps = width // LANES
    sq = x * x
    sq_hi = sq.astype(BF16)
    sq_lo = (sq - sq_hi.astype(F32)).astype(BF16)
    ss = jnp.dot(jnp.concatenate([sq_hi, sq_lo], axis=1), bd2, preferred_element_type=F32)
    y = x * lax.rsqrt(ss * (1.0 / NSA_DH) + NORM_EPS) * gain
    tile = lambda t: jnp.concatenate([t] * reps, axis=1) if reps > 1 else t
    return (y * tile(c) + pltpu.roll(y, width - ROT_HALF, 1) * tile(sm) + pltpu.roll(y, ROT_HALF, 1) * tile(sp))


PREP_TM = 512


def _prep_kernel(q_ref, kv_ref, cs_ref, place_ref, const_ref, gq_ref, gk_ref, bdq_ref, bdk_ref,
                 qo_ref, kso_ref, kwo_ref, vso_ref, vwo_ref):
    tm = PREP_TM
    c, sm, sp = _lane_tables(cs_ref[0], place_ref, const_ref)
    bdk = bdk_ref[...]
    q = _head_norm_rope(q_ref[0], gq_ref[...], bdq_ref[...], c, sm, sp) * (NSA_DH ** -0.5)
    qo_ref[0] = q.T.reshape(NSA_HEADS, NSA_DH, tm)
    kv_cols = lambda n: kv_ref[0, :, n * NSA_KV_W:(n + 1) * NSA_KV_W]
    ks = _head_norm_rope(kv_cols(2), gk_ref[0:1, :], bdk, c, sm, sp)
    kw = _head_norm_rope(kv_cols(4), gk_ref[1:2, :], bdk, c, sm, sp)
    lane = lax.broadcasted_iota(jnp.int32, (tm, LANES), 1)
    token = pl.program_id(1) * tm + lax.broadcasted_iota(jnp.int32, (tm, LANES), 0)
    block_col = NSA_DH + jnp.right_shift(token & (SA_TK - 1), 6)
    onehot = jnp.where(lane == block_col, 1.0, 0.0)
    for g in range(NSA_KV_GROUPS):
        to_front = lambda t: t if g == 0 else pltpu.roll(t, NSA_DH, 1)
        kso_ref[0, g] = jnp.where(lane < NSA_DH, to_front(ks), onehot).astype(BF16)
        kwo_ref[0, g] = jnp.where(lane < NSA_DH, to_front(kw), 0.0).astype(BF16)
    tail = jnp.where(lax.broadcasted_iota(jnp.int32, (VT_ROWS - NSA_DH, tm), 0) == 0, 1.0, 0.0)
    for n, out_ref in ((3, vso_ref), (5, vwo_ref)):
        v_t = kv_cols(n).T
        for g in range(NSA_KV_GROUPS):
            out_ref[0, g] = jnp.concatenate([v_t[g * NSA_DH:(g + 1) * NSA_DH], tail], axis=0).astype(BF16)


def nsa_prep(nq, nkv, cs, q_gain, k_gain):
    tm = PREP_TM
    G = NSA_KV_GROUPS
    row = lambda b, i: (b, i, 0)
    const = lambda b, i: (0, 0)
    gq = jnp.tile(q_gain, NSA_HEADS).reshape(1, NSA_Q_W)
    gk = jnp.stack([jnp.tile(k_gain[1], NSA_KV_GROUPS), jnp.tile(k_gain[2], NSA_KV_GROUPS)])
    kslab = pl.BlockSpec((1, G, tm, LANES), lambda b, i: (b, 0, i, 0))
    vslab = pl.BlockSpec((1, G, VT_ROWS, tm), lambda b, i: (b, 0, 0, i))
    return pl.pallas_call(
        _prep_kernel,
        grid=(BATCH, SEQ // tm),
        in_specs=[
            pl.BlockSpec((1, tm, NSA_Q_W), row),
            pl.BlockSpec((1, tm, 6 * NSA_KV_W), row),
            pl.BlockSpec((1, tm, ROT_DIM), row),
            pl.BlockSpec((ROT_DIM, 3 * LANES), const),
            pl.BlockSpec((1, 3 * LANES), const),
            pl.BlockSpec((1, NSA_Q_W), const),
            pl.BlockSpec((2, NSA_KV_W), const),
            pl.BlockSpec((2 * NSA_Q_W, NSA_Q_W), const),
            pl.BlockSpec((2 * NSA_KV_W, NSA_KV_W), const),
        ],
        out_specs=[pl.BlockSpec((1, NSA_HEADS, NSA_DH, tm), lambda b, i: (b, 0, 0, i)), kslab, kslab, vslab, vslab],
        out_shape=[jax.ShapeDtypeStruct((BATCH, NSA_HEADS, NSA_DH, SEQ), F32),
                   jax.ShapeDtypeStruct((BATCH, G, SEQ, LANES), BF16), jax.ShapeDtypeStruct((BATCH, G, SEQ, LANES), BF16),
                   jax.ShapeDtypeStruct((BATCH, G, VT_ROWS, SEQ), BF16), jax.ShapeDtypeStruct((BATCH, G, VT_ROWS, SEQ), BF16)],
        compiler_params=_cparams(("arbitrary", "arbitrary")),
        name="nsa_prep",
    )(nq, nkv, cs, *_rope_placement(), gq, gk, _block_diag_ones2(NSA_Q_W), _block_diag_ones2(NSA_KV_W))


SEG_W = CMP_STRIDE * NSA_DH


def _cmp_kernel(xk_ref, xv_ref, pe_ref, w1_ref, w2_ref, gain_ref, cs_ref, place_ref, const_ref, bd_ref, ko_ref, vo_ref):
    def compress(x_ref, kv):
        out = jnp.zeros((N_CMP_PAD, LANES), F32)
        for g in range(NSA_KV_GROUPS):
            x = x_ref[0, g]
            ha = jnp.dot(x + pe_ref[kv, 0], w1_ref[kv, 0:SEG_W, :], precision=HI, preferred_element_type=F32)
            hb = jnp.dot(x + pe_ref[kv, 1], w1_ref[kv, SEG_W:2 * SEG_W, :], precision=HI, preferred_element_type=F32)
            hid = ha + pltpu.roll(hb, N_CMP_PAD - 1, 0)
            out += jnp.dot(jax.nn.gelu(hid), w2_ref[kv, g], precision=HI, preferred_element_type=F32)
        return out

    c, sm, sp = _lane_tables(cs_ref[0], place_ref, const_ref)
    ko_ref[0] = _head_norm_rope(compress(xk_ref, 0), gain_ref[...], bd_ref[...], c, sm, sp)
    vo_ref[0] = compress(xv_ref, 1)


def nsa_compress(xk, xv, cmp_pe, cmp_w1, cmp_w2, k_gain0, cs_last):
    pe = cmp_pe.reshape(2, 2, 1, SEG_W)
    w2 = jnp.zeros((2, NSA_KV_GROUPS, CMP_HIDDEN, LANES), F32)
    for g in range(NSA_KV_GROUPS):
        w2 = w2.at[:, g, :, g * NSA_DH:(g + 1) * NSA_DH].set(cmp_w2)
    seg = pl.BlockSpec((1, NSA_KV_GROUPS, N_CMP_PAD, SEG_W), lambda b: (b, 0, 0, 0))
    tab = pl.BlockSpec((1, N_CMP_PAD, LANES), lambda b: (b, 0, 0))
    full = lambda shape: pl.BlockSpec(shape, lambda b: (0,) * len(shape))
    return pl.pallas_call(
        _cmp_kernel,
        grid=(BATCH,),
        in_specs=[seg, seg, full((2, 2, 1, SEG_W)), full((2, 2 * SEG_W, CMP_HIDDEN)),
                  full((2, NSA_KV_GROUPS, CMP_HIDDEN, LANES)), full((1, LANES)),
                  pl.BlockSpec((1, N_CMP_PAD, ROT_DIM), lambda b: (b, 0, 0)), full((ROT_DIM, 3 * LANES)),
                  full((1, 3 * LANES)), full((2 * LANES, LANES))],
        out_specs=[tab, tab],
        out_shape=[jax.ShapeDtypeStruct((BATCH, N_CMP_PAD, LANES), F32)] * 2,
        compiler_params=_cparams(("arbitrary",)),
        name="nsa_compress",
    )(xk, xv, pe, cmp_w1, w2, jnp.tile(k_gain0, NSA_KV_GROUPS).reshape(1, LANES), cs_last, *_rope_placement(),
      _block_diag_ones2(LANES))


CA_TQ = 256
SUBLANES = 8


CA_COLS = NSA_HPG * CA_TQ
CMP_PER_SEL = SEL_BLOCK // CMP_STRIDE
TOPK_BANDS = 4


def split3_keys(k):
    hi = k.astype(BF16)
    lo = (k - hi.astype(F32)).astype(BF16)
    return jnp.concatenate([hi, lo, hi], axis=-1)


def _top_k_rows(score, k):
    rows, cols = score.shape
    row = lax.broadcasted_iota(jnp.int32, (rows, cols), 0).astype(F32)
    taken = jnp.zeros((rows, cols), F32)
    left = score
    for _ in range(k):
        top = jnp.max(left, axis=0, keepdims=True)
        first = jnp.min(jnp.where(left == top, row, float(rows)), axis=0, keepdims=True)
        hit = row == first
        taken = jnp.where(hit, 1.0, taken)
        left = jnp.where(hit, -jnp.inf, left)
    return taken


def _cattn_kernel(q_ref, kc_ref, vct_ref, gl_ref, o_ref, sel_ref, q3_ref, ps_ref):
    tq = CA_TQ
    q0 = pl.program_id(1) * tq
    lanes4 = lambda t: jnp.concatenate([t] * NSA_HPG, axis=1)
    cend = lax.broadcasted_iota(jnp.int32, (N_CMP_PAD, tq), 0) * CMP_STRIDE + (CMP_LEN - 1)
    tc = q0 + lax.broadcasted_iota(jnp.int32, (N_CMP_PAD, tq), 1)
    cmask = lanes4(cend <= tc)
    jj = lax.broadcasted_iota(jnp.int32, (N_SEL, tq), 0)
    tt = q0 + lax.broadcasted_iota(jnp.int32, (N_SEL, tq), 1)
    cur = jnp.right_shift(tt, 6)
    forced = (jj == 0) | (jj == cur) | (jj == cur - 1)
    valid = jj * SEL_BLOCK <= tt

    for g in range(NSA_KV_GROUPS):
        heads = range(g * NSA_HPG, (g + 1) * NSA_HPG)
        for n, h in enumerate(heads):
            q = q_ref[0, h]
            hi = q.astype(BF16)
            lo = (q - hi.astype(F32)).astype(BF16)
            for t, part in enumerate((hi, hi, lo)):
                q3_ref[g, t * NSA_DH:(t + 1) * NSA_DH, n * tq:(n + 1) * tq] = part
        s = jnp.dot(kc_ref[0, g], q3_ref[g], preferred_element_type=F32)
        s = jnp.where(cmask, s, NEG_INF)
        m = jnp.max(s, axis=0, keepdims=True)
        e = jnp.where(cmask, jnp.exp(s - m), 0.0)
        l = jnp.sum(e, axis=0, keepdims=True)
        p = e / jnp.where(l > 0.0, l, 1.0)
        gate = jnp.concatenate([jax.nn.sigmoid(gl_ref[0, h, 0:1, :]) for h in heads], axis=1)
        o = jnp.dot(vct_ref[0, g], p.astype(BF16), preferred_element_type=F32) * gate
        for n, h in enumerate(heads):
            o_ref[0, h] = o[:, n * tq:(n + 1) * tq]
        psum = functools.reduce(jnp.add, [p[:, n * tq:(n + 1) * tq] for n in range(NSA_HPG)])
        for n in range(tq // LANES):
            ps_ref[g, n] = psum[:, n * LANES:(n + 1) * LANES]

        every4th = lambda r: jnp.concatenate(
            [ps_ref[g, n, pl.ds(r, N_SEL, stride=CMP_PER_SEL), :] for n in range(tq // LANES)], axis=1)
        starts_in = [every4th(r) for r in range(CMP_PER_SEL)]
        from_prev = jnp.where(jj >= 1, pltpu.roll(starts_in[CMP_PER_SEL - 1], 1, 0), 0.0)
        imp = functools.reduce(jnp.add, starts_in) + from_prev
        score = jnp.where(valid, imp + jnp.where(forced, FORCE_BONUS, 0.0), NEG_INF)
        step = pl.program_id(1)
        steps_per_band = (SEQ // tq) // TOPK_BANDS
        for band in range(TOPK_BANDS):
            n_rows = (band + 1) * (N_SEL // TOPK_BANDS)

            @pl.when((step >= band * steps_per_band) & (step < (band + 1) * steps_per_band))
            def _():
                taken = _top_k_rows(score[0:n_rows], SEL_TOPK)
                sel_ref[0, g, 0:n_rows, :] = jnp.where(valid[0:n_rows], taken, 0.0)
                if n_rows < N_SEL:
                    sel_ref[0, g, n_rows:N_SEL, :] = jnp.zeros((N_SEL - n_rows, tq), F32)


def nsa_cmp_attn(q_t, kcmp, vcmp_t, gl_t):
    tq = CA_TQ
    G = NSA_KV_GROUPS
    return pl.pallas_call(
        _cattn_kernel,
        grid=(BATCH, SEQ // tq),
        in_specs=[
            pl.BlockSpec((1, NSA_HEADS, NSA_DH, tq), lambda b, i: (b, 0, 0, i)),
            pl.BlockSpec((1, G, N_CMP_PAD, 3 * NSA_DH), lambda b, i: (b, 0, 0, 0)),
            pl.BlockSpec((1, G, NSA_DH, N_CMP_PAD), lambda b, i: (b, 0, 0, 0)),
            pl.BlockSpec((1, NSA_HEADS, 3, tq), lambda b, i: (b, 0, 0, i)),
        ],
        out_specs=[pl.BlockSpec((1, NSA_HEADS, NSA_DH, tq), lambda b, i: (b, 0, 0, i)),
                   pl.BlockSpec((1, G, N_SEL, tq), lambda b, i: (b, 0, 0, i))],
        out_shape=[jax.ShapeDtypeStruct((BATCH, NSA_HEADS, NSA_DH, SEQ), F32),
                   jax.ShapeDtypeStruct((BATCH, G, N_SEL, SEQ), F32)],
        scratch_shapes=[pltpu.VMEM((G, 3 * NSA_DH, CA_COLS), BF16), pltpu.VMEM((G, tq // LANES, N_CMP_PAD, LANES), F32)],
        compiler_params=_cparams(("arbitrary", "arbitrary")),
        name="nsa_cmp_attn",
    )(q_t, split3_keys(kcmp), vcmp_t.astype(BF16), gl_t)


SA_TQ = 256
SA_TK = 1024
SA_PARTS = 2
SA_PART = SA_TK // SA_PARTS
M_INIT = -1e20


SA_COLS = NSA_HPG * SA_TQ
SA_BLOCKS = SA_TK // SEL_BLOCK


VT_ROWS = NSA_DH + 16


def _sattn_kernel(q_ref, k_ref, vt_ref, sel_ref, gl_ref, prev_ref, o_ref, qa_ref, acc_ref, s_ref, m_ref):
    tq, tk = SA_TQ, SA_TK
    i = pl.program_id(1)
    groups = range(NSA_KV_GROUPS)
    slots = range(2)
    for g in groups:
        for h in range(NSA_HPG):
            q = q_ref[0, g * NSA_HPG + h].astype(BF16)
            for slot in slots:
                qa_ref[slot, g, 0:NSA_DH, h * tq:(h + 1) * tq] = q
        for slot in slots:
            qa_ref[slot, g, NSA_DH:LANES, :] = jnp.zeros((LANES - NSA_DH, SA_COLS), BF16)
    acc_ref[...] = jnp.zeros_like(acc_ref)
    lanes4 = lambda t: jnp.concatenate([t] * NSA_HPG, axis=1)
    part_keys = lambda kt, part: pl.ds(pl.multiple_of(kt * tk + part * SA_PART, SA_PART), SA_PART)

    def scores(kt, slot):
        for g in groups:
            selrows = sel_ref[0, g, pl.ds(pl.multiple_of(kt * SA_BLOCKS, SA_BLOCKS), SA_BLOCKS), :]
            qa_ref[slot, g, NSA_DH:NSA_DH + SA_BLOCKS, :] = lanes4(jnp.where(selrows > 0.5, 0.0, NEG_INF)).astype(BF16)
            for part in range(SA_PARTS):
                s = jnp.dot(k_ref[0, g, part_keys(kt, part), :], qa_ref[slot, g], preferred_element_type=F32)
                s_ref[slot, g, part] = s.astype(BF16)

    def absorb(kt, slot, ms):
        out = []
        for g in groups:
            ss = [s_ref[slot, g, part] for part in range(SA_PARTS)]
            m_tile = functools.reduce(jnp.maximum, [jnp.max(s, axis=0, keepdims=True) for s in ss])
            m_new = jnp.maximum(ms[g], m_tile.astype(F32))
            acc = jnp.exp(ms[g] - m_new) * acc_ref[g]
            for part in range(SA_PARTS):
                p = jnp.exp(ss[part] - m_new.astype(BF16))
                acc += jnp.dot(vt_ref[0, g, :, part_keys(kt, part)], p, preferred_element_type=F32)
            acc_ref[g] = acc
            out.append(m_new)
        return tuple(out)

    def two_tiles(j, ms):
        kt = 2 * j
        scores(kt + 1, 1)
        ms = absorb(kt, 0, ms)
        scores(kt + 2, 0)
        return absorb(kt + 1, 1, ms)

    n_full = (i * tq) // tk
    scores(0, 0)
    m0 = tuple(jnp.full((1, SA_COLS), M_INIT, F32) for _ in groups)
    ms = lax.fori_loop(0, n_full // 2, two_tiles, m0)
    for g in groups:
        m_ref[g] = ms[g]

    def last_tile(slot):
        start = i * tq - n_full * tk
        part, row0 = start // SA_PART, pl.multiple_of(start % SA_PART, tq)
        tri = lax.broadcasted_iota(jnp.int32, (tq, tq), 0) <= lax.broadcasted_iota(jnp.int32, (tq, tq), 1)
        bias = lanes4(jnp.where(tri, 0.0, NEG_INF)).astype(BF16)
        for g in groups:
            s_ref[slot, g, part, pl.ds(row0, tq), :] += bias
        for g, m in enumerate(absorb(n_full, slot, tuple(m_ref[g] for g in groups))):
            m_ref[g] = m

    @pl.when(n_full % 2 == 0)
    def _():
        last_tile(0)

    @pl.when(n_full % 2 == 1)
    def _():
        scores(n_full, 1)
        for g, m in enumerate(absorb(n_full - 1, 0, tuple(m_ref[g] for g in groups))):
            m_ref[g] = m
        last_tile(1)

    for g in groups:
        heads = range(g * NSA_HPG, (g + 1) * NSA_HPG)
        gate = jnp.concatenate([jax.nn.sigmoid(gl_ref[0, h, 1:2, :]) for h in heads], axis=1)
        out = acc_ref[g, 0:NSA_DH, :] / acc_ref[g, NSA_DH:NSA_DH + 1, :] * gate
        for n, h in enumerate(heads):
            o_ref[0, h] = prev_ref[0, h] + out[:, n * tq:(n + 1) * tq]


def nsa_sel_attn(q_t, k_slab, vsel_t, sel_t, gl_t, prev):
    tq = SA_TQ
    G = NSA_KV_GROUPS
    ospec = pl.BlockSpec((1, NSA_HEADS, NSA_DH, tq), lambda b, i: (b, 0, 0, i))
    return pl.pallas_call(
        _sattn_kernel,
        grid=(BATCH, SEQ // tq),
        in_specs=[
            ospec,
            pl.BlockSpec((1, G, SEQ, LANES), lambda b, i: (b, 0, 0, 0)),
            pl.BlockSpec((1, G, VT_ROWS, SEQ), lambda b, i: (b, 0, 0, 0)),
            pl.BlockSpec((1, G, N_SEL, tq), lambda b, i: (b, 0, 0, i)),
            pl.BlockSpec((1, NSA_HEADS, 3, tq), lambda b, i: (b, 0, 0, i)),
            ospec,
        ],
        out_specs=ospec,
        out_shape=jax.ShapeDtypeStruct((BATCH, NSA_HEADS, NSA_DH, SEQ), F32),
        scratch_shapes=[pltpu.VMEM((2, G, LANES, SA_COLS), BF16), pltpu.VMEM((G, VT_ROWS, SA_COLS), F32),
                        pltpu.VMEM((2, G, SA_PARTS, SA_PART, SA_COLS), BF16), pltpu.VMEM((G, 1, SA_COLS), F32)],
        input_output_aliases={5: 0},
        compiler_params=_cparams(("arbitrary", "arbitrary")),
        name="nsa_sel_attn",
    )(q_t, k_slab, vsel_t, sel_t, gl_t, prev)


WA_TQ = 256
WA_TILES = WINDOW // WA_TQ + 1


def _window_bias():
    kl = np.arange(WA_TILES * WA_TQ)[:, None]
    ql = np.arange(WA_TQ)[None, :]
    diff = ql - kl + WINDOW
    return jnp.asarray(np.where((diff >= 0) & (diff < WINDOW), 0.0, NEG_INF).astype(np.float32))


def _wattn_kernel(q_ref, k0_ref, k1_ref, k2_ref, v0_ref, v1_ref, v2_ref, bias_ref, gl_ref, prev_ref, o_ref):
    tq = WA_TQ
    i = pl.program_id(1)
    k_refs = (k0_ref, k1_ref, k2_ref)
    v_refs = (v0_ref, v1_ref, v2_ref)
    lanes4 = lambda t: jnp.concatenate([t] * NSA_HPG, axis=1)
    biases = []
    for d in range(WA_TILES):
        in_seq = i - (WA_TILES - 1) + d >= 0
        biases.append(lanes4(jnp.where(in_seq, bias_ref[d * tq:(d + 1) * tq, :], NEG_INF)))
    for g in range(NSA_KV_GROUPS):
        heads = range(g * NSA_HPG, (g + 1) * NSA_HPG)
        q = jnp.concatenate([q_ref[0, h] for h in heads], axis=1).astype(BF16)
        q = jnp.concatenate([q, jnp.zeros_like(q)], axis=0)
        ss = [(jnp.dot(k_refs[d][0, g], q, preferred_element_type=F32) + biases[d]).astype(BF16) for d in range(WA_TILES)]
        m = functools.reduce(jnp.maximum, [jnp.max(s, axis=0, keepdims=True) for s in ss])
        acc = functools.reduce(jnp.add, [jnp.dot(v_refs[d][0, g], jnp.exp(ss[d] - m), preferred_element_type=F32)
                                         for d in range(WA_TILES)])
        gate = jnp.concatenate([jax.nn.sigmoid(gl_ref[0, h, 2:3, :]) for h in heads], axis=1)
        out = acc[0:NSA_DH] / acc[NSA_DH:NSA_DH + 1] * gate
        for n, h in enumerate(heads):
            o_ref[0, h] = prev_ref[0, h] + out[:, n * tq:(n + 1) * tq]


def nsa_win_attn(q_t, kwin, vwin_t, gl_t, prev):
    tq = WA_TQ
    G = NSA_KV_GROUPS
    qspec = pl.BlockSpec((1, NSA_HEADS, NSA_DH, tq), lambda b, i: (b, 0, 0, i))
    tile = lambda d: (lambda i: jnp.maximum(i - (WA_TILES - 1) + d, 0))
    kspec = lambda d: pl.BlockSpec((1, G, tq, LANES), lambda b, i: (b, 0, tile(d)(i), 0))
    vspec = lambda d: pl.BlockSpec((1, G, VT_ROWS, tq), lambda b, i: (b, 0, 0, tile(d)(i)))
    return pl.pallas_call(
        _wattn_kernel,
        grid=(BATCH, SEQ // tq),
        in_specs=[qspec] + [kspec(d) for d in range(WA_TILES)] + [vspec(d) for d in range(WA_TILES)] + [
            pl.BlockSpec((WA_TILES * tq, tq), lambda b, i: (0, 0)),
            pl.BlockSpec((1, NSA_HEADS, 3, tq), lambda b, i: (b, 0, 0, i)),
            qspec,
        ],
        out_specs=qspec,
        out_shape=jax.ShapeDtypeStruct((BATCH, NSA_HEADS, NSA_DH, SEQ), F32),
        input_output_aliases={2 * WA_TILES + 3: 0},
        compiler_params=_cparams(("arbitrary", "arbitrary")),
        name="nsa_win_attn",
    )(q_t, *([kwin] * WA_TILES), *([vwin_t] * WA_TILES), _window_bias(), gl_t, prev)


def nsa_mixer(nq, nkv, misc, positions, q_gain, k_gain, cmp_pe, cmp_w1, cmp_w2):
    cs = rope_tables(positions)
    q_t, ksel, kwin, vsel_t, vwin_t = nsa_prep(nq, nkv, cs, q_gain, k_gain)
    group_major = lambda t: t.reshape(BATCH, SEQ, NSA_KV_GROUPS, NSA_DH).transpose(0, 2, 1, 3)
    col = lambda n: nkv[..., n * NSA_KV_W:(n + 1) * NSA_KV_W]
    segs = lambda t: group_major(t).reshape(BATCH, NSA_KV_GROUPS, N_CMP_PAD, SEG_W)
    last = jnp.minimum(jnp.arange(N_CMP_PAD) * CMP_STRIDE + CMP_LEN - 1, SEQ - 1)
    kcmp, vcmp = nsa_compress(segs(col(0)), segs(col(1)), cmp_pe, cmp_w1, cmp_w2, k_gain[0], cs[:, last])
    kcmp = kcmp.reshape(BATCH, N_CMP_PAD, NSA_KV_GROUPS, NSA_DH).transpose(0, 2, 1, 3)
    vcmp_t = vcmp.reshape(BATCH, N_CMP_PAD, NSA_KV_GROUPS, NSA_DH).transpose(0, 2, 3, 1)
    gl_t = misc[..., GLA_GATE_RANK:GLA_GATE_RANK + NSA_HEADS * 3].reshape(BATCH, SEQ, NSA_HEADS, 3).transpose(0, 2, 3, 1)
    o_t, sel_t = nsa_cmp_attn(q_t, kcmp, vcmp_t, gl_t)
    o_t = nsa_sel_attn(q_t, ksel, vsel_t, sel_t, gl_t, o_t)
    return nsa_win_attn(q_t, kwin, vwin_t, gl_t, o_t)


ROUTE_ROWS = 8
HX_TERMS = 3
HX_W = D_MODEL + LANES


def _top2_sum(a, b, c, d):
    hi1, lo1 = jnp.maximum(a, b), jnp.minimum(a, b)
    hi2, lo2 = jnp.maximum(c, d), jnp.minimum(c, d)
    return jnp.maximum(hi1, hi2) + jnp.maximum(jnp.minimum(hi1, hi2), jnp.maximum(lo1, lo2))


def _moe_prenorm_route(xn, g_ref, sh_ref, sc_ref, wr_ref, rb_ref, hx_ref, route_ref):
    h = _rms_mod(xn, g_ref[...], sh_ref[0], sc_ref[0])
    logits = lax.dot_general(wr_ref[...], h, NT_DIMS, precision=HI, preferred_element_type=F32)
    scores = jax.nn.sigmoid(logits)
    sel = scores + rb_ref[...]
    epg = EXPERTS_PER_GROUP
    srow = lambda e: sel[e:e + 1, :]
    grp = [_top2_sum(*[srow(epg * g + r) for r in range(epg)]) for g in range(N_EXPERT_GROUPS)]
    best, gi = grp[0], jnp.zeros_like(grp[0], dtype=jnp.int32)
    for g in range(1, N_EXPERT_GROUPS):
        better = grp[g] > best
        gi = jnp.where(better, g, gi)
        best = jnp.where(better, grp[g], best)

    def in_group(mat, r):
        out = mat[r:r + 1, :]
        for g in range(1, N_EXPERT_GROUPS):
            out = jnp.where(gi == g, mat[epg * g + r:epg * g + r + 1, :], out)
        return out

    v = [in_group(sel, r) for r in range(epg)]
    sc = [in_group(scores, r) for r in range(epg)]
    b1, i1, w1 = v[0], jnp.zeros_like(gi), sc[0]
    for r in range(1, epg):
        better = v[r] > b1
        i1 = jnp.where(better, r, i1)
        w1 = jnp.where(better, sc[r], w1)
        b1 = jnp.where(better, v[r], b1)
    b2 = jnp.full_like(b1, -3e38)
    i2, w2 = jnp.zeros_like(gi), jnp.zeros_like(w1)
    for r in range(epg):
        better = (i1 != r) & (v[r] > b2)
        i2 = jnp.where(better, r, i2)
        w2 = jnp.where(better, sc[r], w2)
        b2 = jnp.where(better, v[r], b2)
    tot = w1 + w2
    w1, w2 = w1 / tot, w2 / tot
    zero = jnp.zeros_like(w1)
    route_ref[0] = jnp.concatenate([gi.astype(F32)] + [zero] * (ROUTE_ROWS - 1), axis=0)
    w = jnp.concatenate([jnp.where(i1 == r, w1, jnp.where(i2 == r, w2, 0.0)) for r in range(epg)], axis=0)
    w_hi = w.astype(BF16).astype(F32)
    w_mid = (w - w_hi).astype(BF16).astype(F32)
    w_lo = (w - w_hi - w_mid).astype(BF16).astype(F32)
    pad = jnp.zeros((LANES - HX_TERMS * epg, w.shape[1]), F32)
    hx_ref[0, :, 0:D_MODEL] = h.astype(BF16)
    hx_ref[0, :, D_MODEL:HX_W] = jnp.concatenate([w_hi, w_mid, w_lo, pad], axis=0).T.astype(BF16)


def _route_specs(tm, row, vec, const):
    in_specs = [pl.BlockSpec((1, D_MODEL), const), pl.BlockSpec((1, 1, D_MODEL), vec), pl.BlockSpec((1, 1, D_MODEL), vec),
                pl.BlockSpec((N_EXPERTS, D_MODEL), const), pl.BlockSpec((N_EXPERTS, 1), const)]
    assert tm == MOE_CHUNK
    out_specs = [pl.BlockSpec((1, tm, HX_W), lambda b, i: (b * (SEQ // tm) + i, 0, 0)),
                 pl.BlockSpec((1, ROUTE_ROWS, tm), lambda b, i: (b, 0, i))]
    out_shape = [jax.ShapeDtypeStruct((MOE_CHUNKS, MOE_CHUNK, HX_W), BF16),
                 jax.ShapeDtypeStruct((BATCH, ROUTE_ROWS, SEQ), F32)]
    return in_specs, out_specs, out_shape


def _route_args(g, shift, scale, w_router, router_bias):
    return (g.reshape(1, D_MODEL), shift.reshape(BATCH, 1, D_MODEL), scale.reshape(BATCH, 1, D_MODEL),
            w_router.T, router_bias.reshape(N_EXPERTS, 1))


OUTPROJ_TM = 512


def _outproj0_kernel(oa_ref, ob_ref, w_ref, x_ref, gate_ref, g_ref, sh_ref, sc_ref, wr_ref, rb_ref,
                     xo_ref, h_ref, route_ref):
    y = jnp.dot(oa_ref[0].astype(BF16), w_ref[0:GLA_V_W, :], preferred_element_type=F32)
    ob_t = ob_ref[0].reshape(NSA_Q_W, OUTPROJ_TM).astype(BF16)
    y += lax.dot_general(ob_t, w_ref[GLA_V_W:GLA_V_W + NSA_Q_W, :], TN_DIMS, preferred_element_type=F32)
    xn = x_ref[0] + gate_ref[0] * y
    xo_ref[0] = xn
    _moe_prenorm_route(xn, g_ref, sh_ref, sc_ref, wr_ref, rb_ref, h_ref, route_ref)


def outproj0(o_a, o_b, w_out, x, gate, route_args):
    tm = OUTPROJ_TM
    row = lambda b, i: (b, i, 0)
    vec = lambda b, i: (b, 0, 0)
    const = lambda b, i: (0, 0)
    r_in, r_out, r_shape = _route_specs(tm, row, vec, const)
    return pl.pallas_call(
        _outproj0_kernel,
        grid=(BATCH, SEQ // tm),
        in_specs=[pl.BlockSpec((1, tm, GLA_V_W), row), pl.BlockSpec((1, NSA_HEADS, NSA_DH, tm), lambda b, i: (b, 0, 0, i)),
                  pl.BlockSpec((GLA_V_W + NSA_Q_W, D_MODEL), const), pl.BlockSpec((1, tm, D_MODEL), row),
                  pl.BlockSpec((1, 1, D_MODEL), vec)] + r_in,
        out_specs=[pl.BlockSpec((1, tm, D_MODEL), row)] + r_out,
        out_shape=[jax.ShapeDtypeStruct((BATCH, SEQ, D_MODEL), F32)] + r_shape,
        compiler_params=_cparams(("arbitrary", "arbitrary")),
        name="outproj0",
    )(o_a, o_b, w_out.astype(BF16), x, gate.reshape(BATCH, 1, D_MODEL), *route_args)


GMLP_TM = 512


def _gmlp_kernel(x_ref, g1_ref, sh1_ref, sc1_ref, win_ref, ng_ref, ws_ref, bs_ref, wout_ref, gate_ref,
                 g_ref, sh_ref, sc_ref, wr_ref, rb_ref, xo_ref, h_ref, route_ref, gated_ref, v_ref):
    x = x_ref[0]
    h = _rms_mod(x, g1_ref[...], sh1_ref[0], sc1_ref[0]).astype(BF16)
    group_cols = lambda g: slice(g * SGU_GROUP_DIM, (g + 1) * SGU_GROUP_DIM)
    ssq = jnp.zeros((GMLP_TM, LANES), F32)
    for g in range(SGU_GROUPS):
        lo = SGU_WIDTH + g * SGU_GROUP_DIM
        v = jax.nn.gelu(jnp.dot(h, win_ref[:, lo:lo + SGU_GROUP_DIM], preferred_element_type=F32))
        v_ref[:, group_cols(g)] = v
        ssq += functools.reduce(jnp.add, [v[:, n * LANES:(n + 1) * LANES] ** 2 for n in range(SGU_GROUP_DIM // LANES)])
    rs = lax.rsqrt(jnp.sum(ssq, axis=-1, keepdims=True) * (1.0 / SGU_WIDTH) + NORM_EPS)
    ri = lax.broadcasted_iota(jnp.int32, (SGU_CHUNK, SGU_CHUNK), 0)
    ci = lax.broadcasted_iota(jnp.int32, (SGU_CHUNK, SGU_CHUNK), 1)
    for g in range(SGU_GROUPS):
        cols = group_cols(g)
        u = jax.nn.gelu(jnp.dot(h, win_ref[:, cols], preferred_element_type=F32))
        vn = (v_ref[:, cols] * rs * ng_ref[:, cols]).astype(BF16)
        w = jnp.where(ri >= ci, ws_ref[g], 0.0).astype(BF16)
        for c in range(GMLP_TM // SGU_CHUNK):
            rows = slice(c * SGU_CHUNK, (c + 1) * SGU_CHUNK)
            mix = jnp.dot(w, vn[rows], preferred_element_type=F32) + bs_ref[:, g:g + 1]
            gated_ref[rows, cols] = (u[rows] * mix).astype(BF16)
    y = jnp.dot(gated_ref[...], wout_ref[...], preferred_element_type=F32)
    xn = x + gate_ref[0] * y
    xo_ref[0] = xn
    _moe_prenorm_route(xn, g_ref, sh_ref, sc_ref, wr_ref, rb_ref, h_ref, route_ref)


def gmlp_layer(x, g1, shift1, scale1, w_in, norm_g, w_s, b_s, w_out, gate, route_args):
    tm = GMLP_TM
    row = lambda b, i: (b, i, 0)
    vec = lambda b, i: (b, 0, 0)
    const = lambda b, i: (0, 0)
    r_in, r_out, r_shape = _route_specs(tm, row, vec, const)
    vspec = pl.BlockSpec((1, 1, D_MODEL), vec)
    return pl.pallas_call(
        _gmlp_kernel,
        grid=(BATCH, SEQ // tm),
        in_specs=[pl.BlockSpec((1, tm, D_MODEL), row), pl.BlockSpec((1, D_MODEL), const), vspec, vspec,
                  pl.BlockSpec((D_MODEL, 2 * SGU_WIDTH), const), pl.BlockSpec((1, SGU_WIDTH), const),
                  pl.BlockSpec((SGU_GROUPS, SGU_CHUNK, SGU_CHUNK), lambda b, i: (0, 0, 0)),
                  pl.BlockSpec((SGU_CHUNK, SGU_GROUPS), const), pl.BlockSpec((SGU_WIDTH, D_MODEL), const), vspec] + r_in,
        out_specs=[pl.BlockSpec((1, tm, D_MODEL), row)] + r_out,
        out_shape=[jax.ShapeDtypeStruct((BATCH, SEQ, D_MODEL), F32)] + r_shape,
        scratch_shapes=[pltpu.VMEM((tm, SGU_WIDTH), BF16), pltpu.VMEM((tm, SGU_WIDTH), F32)],
        compiler_params=_cparams(("arbitrary", "arbitrary"), vmem_mib=56),
        name="gmlp_layer",
    )(x, g1.reshape(1, D_MODEL), shift1.reshape(BATCH, 1, D_MODEL), scale1.reshape(BATCH, 1, D_MODEL),
      w_in.astype(BF16), norm_g.reshape(1, SGU_WIDTH), w_s, b_s.T, w_out.astype(BF16),
      gate.reshape(BATCH, 1, D_MODEL), *route_args)


MOE_TM = 256
MOE_CHUNK = 512
MOE_SORTED = N_TOK + N_EXPERT_GROUPS * MOE_TM
MOE_TILES = MOE_SORTED // MOE_TM
MOE_CHUNKS = N_TOK // MOE_CHUNK
MOE_PAIRS = MOE_TILES + N_EXPERT_GROUPS * MOE_CHUNKS
FLAG_ACTIVE, FLAG_FIRST, FLAG_LAST, FLAG_ZERO = 1, 2, 4, 8
EXP_WIN = 4
CMB_WIN = 8
MOE_TSTEPS = MOE_TILES + MOE_PAIRS // EXP_WIN
MOE_CSTEPS = MOE_CHUNKS + MOE_PAIRS // CMB_WIN


def _plan_kernel(gi_ref, rank_ref, before_ref):
    gi = gi_ref[...]
    r = lax.broadcasted_iota(jnp.int32, (POS_SIDE, POS_SIDE), 0)
    c = lax.broadcasted_iota(jnp.int32, (POS_SIDE, POS_SIDE), 1)
    upper = jnp.where(r <= c, 1.0, 0.0)
    lower_strict = jnp.where(c < r, 1.0, 0.0)
    rank = jnp.zeros((POS_SIDE, POS_SIDE), F32)
    for g in range(N_EXPERT_GROUPS):
        member = jnp.where(gi == g, 1.0, 0.0)
        in_row = jnp.dot(member, upper, precision=HI, preferred_element_type=F32)
        row_total = jnp.broadcast_to(in_row[:, POS_SIDE - 1:POS_SIDE], (POS_SIDE, POS_SIDE))
        before = jnp.dot(lower_strict, row_total, precision=HI, preferred_element_type=F32)
        before_ref[g] = before
        rank += member * (before + in_row - 1.0)
    rank_ref[...] = rank


def moe_plan(route):
    tm = MOE_TM
    i32 = jnp.int32
    gi_f = route[:, 0, :].reshape(POS_SIDE, POS_SIDE)
    rank, before = pl.pallas_call(
        _plan_kernel,
        out_shape=[jax.ShapeDtypeStruct((POS_SIDE, POS_SIDE), F32),
                   jax.ShapeDtypeStruct((N_EXPERT_GROUPS, POS_SIDE, POS_SIDE), F32)],
        name="moe_plan",
    )(gi_f)
    gi = gi_f.reshape(N_TOK).astype(i32)
    groups = jnp.arange(N_EXPERT_GROUPS, dtype=i32)
    member = gi[None, :] == groups[:, None]
    tot = jnp.sum(member, axis=1).astype(i32)
    padded = (tot + tm - 1) // tm * tm
    gend = jnp.cumsum(padded).astype(i32)
    gstart = gend - padded
    pos = jnp.sum(jnp.where(member, gstart[:, None], 0), axis=0).astype(i32) + rank.reshape(N_TOK).astype(i32)
    rows_per_chunk = MOE_CHUNK // POS_SIDE
    cnt_end = jnp.concatenate([before[:, rows_per_chunk::rows_per_chunk, 0].astype(i32), tot[:, None]], axis=1)
    t = jnp.arange(MOE_TILES, dtype=i32)
    n_used = gend[-1] // tm
    tile_g = jnp.minimum(jnp.sum(gend[None, :] <= (t * tm)[:, None], axis=1), N_EXPERT_GROUPS - 1).astype(i32)
    k0 = t * tm - gstart[tile_g]
    k1 = jnp.minimum(k0 + tm, tot[tile_g]) - 1
    ce = cnt_end[tile_g]
    c_lo = jnp.sum(ce <= k0[:, None], axis=1).astype(i32)
    c_hi = jnp.sum(ce <= k1[:, None], axis=1).astype(i32)
    npairs = jnp.where(t < n_used, c_hi - c_lo + 1, 0)
    pend = jnp.cumsum(npairs).astype(i32)
    pstart = pend - npairs
    total = pend[-1]
    l = jnp.arange(MOE_PAIRS, dtype=i32)
    real = l < total
    lt = jnp.minimum(l, total - 1)

    def windows(count, win, n_steps):
        per_item = (count + win - 1) // win
        end = jnp.cumsum(per_item).astype(i32)
        start = end - per_item
        s = jnp.arange(n_steps, dtype=i32)
        real_s = s < end[-1]
        sc = jnp.minimum(s, end[-1] - 1)
        item = jnp.sum(end[None, :] <= sc[:, None], axis=1).astype(i32)
        j = sc - start[item]
        flags_s = jnp.where(real_s, FLAG_ACTIVE + jnp.where(j == 0, FLAG_FIRST, 0)
                            + jnp.where(j == per_item[item] - 1, FLAG_LAST, 0), 0).astype(i32)
        return item, j, flags_s, real_s, s - end[-1]

    tile_s, j, flags, real_s, spare = windows(npairs, EXP_WIN, MOE_TSTEPS)
    c0 = c_lo[tile_s] + EXP_WIN * j
    n_valid = jnp.minimum(EXP_WIN, c_hi[tile_s] - c0 + 1).astype(i32)
    spare_tile = jnp.minimum(n_used + spare, MOE_TILES - 1)
    flags = jnp.where(real_s, flags, jnp.where(spare_tile >= n_used, FLAG_ZERO, 0)).astype(i32)
    tile_sched = jnp.where(real_s, tile_s, spare_tile).astype(i32)
    by_tile = (tile_sched, c0.astype(i32), n_valid, flags, tile_g[tile_sched])
    cc = jnp.arange(MOE_CHUNKS, dtype=i32)
    is_pair = (cc[:, None] >= c_lo[None, :]) & (cc[:, None] <= c_hi[None, :]) & (t[None, :] < n_used)
    seen = jnp.cumsum(is_pair.reshape(-1).astype(i32))
    flat = jnp.sum(seen[None, :] <= lt[:, None], axis=1).astype(i32)
    pair_tile = flat % MOE_TILES
    per_chunk = jnp.sum(is_pair, axis=1).astype(i32)
    first_pair = jnp.cumsum(per_chunk).astype(i32) - per_chunk
    chunk_s, j, flags_c, _, _ = windows(per_chunk, CMB_WIN, MOE_CSTEPS)
    base = first_pair[chunk_s] + CMB_WIN * j
    n_valid_c = jnp.minimum(CMB_WIN, per_chunk[chunk_s] - CMB_WIN * j).astype(i32)
    tiles_c = tuple(pair_tile[jnp.minimum(base + w, total - 1)] for w in range(CMB_WIN))
    by_chunk = (chunk_s, n_valid_c, flags_c) + tiles_c
    return pos.reshape(MOE_CHUNKS, 1, MOE_CHUNK), by_tile, by_chunk


def _one_hot_rows(pos_row, tile):
    rows = tile * MOE_TM + lax.broadcasted_iota(jnp.int32, (MOE_TM, MOE_CHUNK), 0)
    return jnp.where(pos_row == rows, 1.0, 0.0).astype(BF16)


def _moe_kernel(tile_ref, c0_ref, nv_ref, flag_ref, grp_ref, *refs):
    pos_refs, hx_refs = refs[0:EXP_WIN], refs[EXP_WIN:2 * EXP_WIN]
    wg_ref, wu_ref, wd_ref, y_ref, acc_ref = refs[2 * EXP_WIN:]
    l = pl.program_id(0)
    flags = flag_ref[l]

    @pl.when((flags & FLAG_FIRST) != 0)
    def _():
        acc_ref[...] = jnp.zeros_like(acc_ref)

    for w in range(EXP_WIN):
        @pl.when(((flags & FLAG_ACTIVE) != 0) & (w < nv_ref[l]))
        def _():
            onehot = _one_hot_rows(pos_refs[w][0], tile_ref[l])
            acc_ref[...] += jnp.dot(onehot, hx_refs[w][0], preferred_element_type=F32)

    @pl.when((flags & FLAG_LAST) != 0)
    def _():
        x = acc_ref[:, 0:D_MODEL].astype(BF16)
        terms = acc_ref[:, D_MODEL:HX_W]
        y = jnp.zeros((MOE_TM, D_MODEL), F32)
        for r in range(EXPERTS_PER_GROUP):
            lanes = [n * EXPERTS_PER_GROUP + r for n in range(HX_TERMS)]
            w_r = functools.reduce(jnp.add, [terms[:, c:c + 1] for c in lanes])
            gate = jnp.dot(x, wg_ref[0, 0, r], preferred_element_type=F32)
            up = jnp.dot(x, wu_ref[0, 0, r], preferred_element_type=F32)
            hid = (_silu(gate) * up * w_r).astype(BF16)
            y += jnp.dot(hid, wd_ref[0, 0, r], preferred_element_type=F32)
        y_ref[...] = y

    @pl.when((flags & FLAG_ZERO) != 0)
    def _():
        y_ref[...] = jnp.zeros_like(y_ref)


def moe_experts(hx, pos, by_tile, w_gate, w_up, w_down, layer):
    grouped = lambda w: w.reshape(DEPTH, N_EXPERT_GROUPS, EXPERTS_PER_GROUP, *w.shape[2:])
    wspec = lambda k, n: pl.BlockSpec((1, 1, EXPERTS_PER_GROUP, k, n), lambda l, t, c, n_, f, g: (layer, g[l], 0, 0, 0))
    chunk = lambda w: (lambda l, t, c, n_, f, g: (jnp.minimum(c[l] + w, MOE_CHUNKS - 1), 0, 0))
    return pl.pallas_call(
        _moe_kernel,
        grid_spec=pltpu.PrefetchScalarGridSpec(
            num_scalar_prefetch=5,
            grid=(MOE_TSTEPS,),
            in_specs=[pl.BlockSpec((1, 1, MOE_CHUNK), chunk(w)) for w in range(EXP_WIN)]
            + [pl.BlockSpec((1, MOE_CHUNK, HX_W), chunk(w)) for w in range(EXP_WIN)]
            + [wspec(D_MODEL, EXPERT_HIDDEN), wspec(D_MODEL, EXPERT_HIDDEN), wspec(EXPERT_HIDDEN, D_MODEL)],
            out_specs=pl.BlockSpec((MOE_TM, D_MODEL), lambda l, t, c, n_, f, g: (t[l], 0)),
            scratch_shapes=[pltpu.VMEM((MOE_TM, HX_W), F32)],
        ),
        out_shape=jax.ShapeDtypeStruct((MOE_SORTED, D_MODEL), F32),
        compiler_params=_cparams(("arbitrary",), vmem_mib=56),
        name="moe_experts",
    )(*by_tile, *([pos] * EXP_WIN), *([hx] * EXP_WIN), grouped(w_gate), grouped(w_up), grouped(w_down))


def _moe_combine_kernel(chunk_ref, nv_ref, flag_ref, *refs):
    tile_refs = refs[0:CMB_WIN]
    pos_ref = refs[CMB_WIN]
    y_refs = refs[CMB_WIN + 1:2 * CMB_WIN + 1]
    x_ref, gate_ref, o_ref, acc_ref = refs[2 * CMB_WIN + 1:]
    l = pl.program_id(0)
    flags = flag_ref[l]

    @pl.when((flags & FLAG_FIRST) != 0)
    def _():
        acc_ref[...] = jnp.zeros_like(acc_ref)

    for w in range(CMB_WIN):
        @pl.when(((flags & FLAG_ACTIVE) != 0) & (w < nv_ref[l]))
        def _():
            onehot = _one_hot_rows(pos_ref[0], tile_refs[w][l])
            y = y_refs[w][...]
            y_hi = y.astype(BF16)
            y_lo = (y - y_hi.astype(F32)).astype(BF16)
            acc_ref[...] += (lax.dot_general(onehot, y_hi, TN_DIMS, preferred_element_type=F32)
                             + lax.dot_general(onehot, y_lo, TN_DIMS, preferred_element_type=F32))

    @pl.when((flags & FLAG_LAST) != 0)
    def _():
        o_ref[0] = x_ref[0] + gate_ref[0] * acc_ref[...]


def moe_combine(x, y_sorted, pos, by_chunk, gate):
    per_b = SEQ // MOE_CHUNK
    tok = lambda l, c, *_: (c[l] // per_b, c[l] % per_b, 0)
    tile = lambda w: (lambda l, c, n_, f, *tiles: (tiles[w][l], 0))
    return pl.pallas_call(
        _moe_combine_kernel,
        grid_spec=pltpu.PrefetchScalarGridSpec(
            num_scalar_prefetch=3 + CMB_WIN,
            grid=(MOE_CSTEPS,),
            in_specs=[pl.BlockSpec((1, 1, MOE_CHUNK), lambda l, c, *_: (c[l], 0, 0))]
            + [pl.BlockSpec((MOE_TM, D_MODEL), tile(w)) for w in range(CMB_WIN)]
            + [pl.BlockSpec((1, MOE_CHUNK, D_MODEL), tok),
               pl.BlockSpec((1, 1, D_MODEL), lambda l, c, *_: (c[l] // per_b, 0, 0))],
            out_specs=pl.BlockSpec((1, MOE_CHUNK, D_MODEL), tok),
            scratch_shapes=[pltpu.VMEM((MOE_CHUNK, D_MODEL), F32)],
        ),
        out_shape=jax.ShapeDtypeStruct((BATCH, SEQ, D_MODEL), F32),
        compiler_params=_cparams(("arbitrary",)),
        name="moe_combine",
    )(*by_chunk, pos, *([y_sorted] * CMB_WIN), x, gate.reshape(BATCH, 1, D_MODEL))


def moe_layer(x, hx, route, gate, w_gate, w_up, w_down, layer):
    pos, by_tile, by_chunk = moe_plan(route)
    y_sorted = moe_experts(hx, pos, by_tile, w_gate, w_up, w_down, layer)
    return moe_combine(x, y_sorted, pos, by_chunk, gate)


def kernel(x, c, positions, w_ada, b_ada, norm_g, w_in_ab, w_out_ab, gla_w_gate2, gla_b_gate, gla_norm_g, nsa_q_gain, nsa_k_gain, nsa_cmp_pe, nsa_cmp_w1, nsa_cmp_w2, w_in_c, sgu_norm_g, sgu_w_s, sgu_b_s, w_out_c, w_router, router_bias, w_gate, w_up, w_down):
    mod = ada_modulation(c, w_ada, b_ada)
    qk, gv, gr, nq, nkv, misc = inproj0(x, norm_g[0, 0], mod[0, :, 0], mod[0, :, 1], _arrange_w_in(w_in_ab[0]))
    o_a = gla_mixer(qk, gv, gr, misc, gla_w_gate2[0], gla_b_gate[0], gla_norm_g[0])
    o_b = nsa_mixer(nq, nkv, misc, positions, nsa_q_gain[0], nsa_k_gain[0], nsa_cmp_pe[0], nsa_cmp_w1[0], nsa_cmp_w2[0])
    wg, wu, wd = w_gate.astype(BF16), w_up.astype(BF16), w_down.astype(BF16)
    route_args = lambda l: _route_args(norm_g[l, 1], mod[l, :, 3], mod[l, :, 4], w_router, router_bias)
    x1, h, route = outproj0(o_a, o_b, w_out_ab[0], x, mod[0, :, 2], route_args(0))
    x2 = moe_layer(x1, h, route, mod[0, :, 5], wg, wu, wd, 0)
    x3, h, route = gmlp_layer(x2, norm_g[1, 0], mod[1, :, 0], mod[1, :, 1], w_in_c[0], sgu_norm_g[0], sgu_w_s[0],
                              sgu_b_s[0], w_out_c[0], mod[1, :, 2], route_args(1))
    return moe_layer(x3, h, route, mod[1, :, 5], wg, wu, wd, 1)
```

```python
import functools

import numpy as np
import jax
import jax.numpy as jnp
from jax import lax
from jax.experimental import pallas as pl
from jax.experimental.pallas import tpu as pltpu

D_MODEL = 1024
BATCH = 2
SEQ = 8192
DEPTH = 2
N_TOK = BATCH * SEQ

GLA_HEADS = 4
GLA_DK = 64
GLA_DV = 128
GLA_GATE_RANK = 16
GLA_TAU = 16.0
GLA_CHUNK = 64
NSA_HEADS = 8
NSA_KV_GROUPS = 2
NSA_HPG = NSA_HEADS // NSA_KV_GROUPS
NSA_DH = 64
CMP_LEN = 32
CMP_STRIDE = 16
CMP_HIDDEN = 256
SEL_BLOCK = 64
SEL_TOPK = 16
WINDOW = 512
ROPE_THETA = 500000.0
ROT_DIM = NSA_DH // 4
ROT_HALF = ROT_DIM // 2
SGU_CHUNK = 128
SGU_GROUPS = 8
SGU_WIDTH = 2048
SGU_GROUP_DIM = SGU_WIDTH // SGU_GROUPS
N_EXPERTS = 16
N_EXPERT_GROUPS = 4
EXPERTS_PER_GROUP = N_EXPERTS // N_EXPERT_GROUPS
MOE_TOPK = 2
EXPERT_HIDDEN = 512

GLA_QK_W = GLA_HEADS * GLA_DK
GLA_V_W = GLA_HEADS * GLA_DV
NSA_Q_W = NSA_HEADS * NSA_DH
NSA_KV_W = NSA_KV_GROUPS * NSA_DH
N_CMP = (SEQ - CMP_LEN) // CMP_STRIDE + 1
N_CMP_PAD = SEQ // CMP_STRIDE
N_SEL = SEQ // SEL_BLOCK

NORM_EPS = 1e-6
NEG_INF = -1e30
FORCE_BONUS = 1e4

LANES = 128
MIB = 1024 * 1024

F32 = jnp.float32
BF16 = jnp.bfloat16
HI = lax.Precision.HIGHEST
NT_DIMS = (((1,), (1,)), ((), ()))
TN_DIMS = (((0,), (0,)), ((), ()))


def _cparams(sem, vmem_mib=48):
    return pltpu.CompilerParams(dimension_semantics=sem, vmem_limit_bytes=vmem_mib * MIB)


def _rms_mod(x, g, shift, scale):
    y = x * lax.rsqrt(jnp.mean(x * x, axis=-1, keepdims=True) + NORM_EPS) * g
    return y * (1 + scale) + shift


def _silu(x):
    return x * jax.nn.sigmoid(x)


def _log_sigmoid(z):
    return jnp.minimum(z, 0.0) - jnp.log1p(jnp.exp(-jnp.abs(z)))


ADA_TN = 1536
ADA_ROWS = 8


def _ada_kernel(c_ref, w_ref, b_ref, o_ref):
    cond = _silu(c_ref[...])
    o_ref[0] = jnp.dot(cond, w_ref[0], precision=HI, preferred_element_type=F32) + b_ref[0]


def ada_modulation(c, w_ada, b_ada):
    c8 = jnp.zeros((ADA_ROWS, D_MODEL), F32).at[:BATCH].set(c)
    width = 6 * D_MODEL
    out = pl.pallas_call(
        _ada_kernel,
        grid=(DEPTH, width // ADA_TN),
        in_specs=[
            pl.BlockSpec((ADA_ROWS, D_MODEL), lambda l, j: (0, 0)),
            pl.BlockSpec((1, D_MODEL, ADA_TN), lambda l, j: (l, 0, j)),
            pl.BlockSpec((1, 1, ADA_TN), lambda l, j: (l, 0, j)),
        ],
        out_specs=pl.BlockSpec((1, ADA_ROWS, ADA_TN), lambda l, j: (l, 0, j)),
        out_shape=jax.ShapeDtypeStruct((DEPTH, ADA_ROWS, width), F32),
        compiler_params=_cparams(("arbitrary", "arbitrary")),
        name="ada_modulation",
    )(c8, w_ada, b_ada.reshape(DEPTH, 1, width))
    return out[:, :BATCH].reshape(DEPTH, BATCH, 6, D_MODEL)


INPROJ_TM = 512
INPROJ_WIDTHS = (2 * GLA_QK_W, GLA_V_W, GLA_V_W, NSA_Q_W, 6 * NSA_KV_W, LANES)


def _arrange_w_in(w_in):
    o = np.cumsum((0, GLA_QK_W, GLA_QK_W, GLA_V_W, GLA_GATE_RANK, GLA_V_W, NSA_Q_W, 6 * NSA_KV_W, NSA_HEADS * 3))
    gq_gk = w_in[:, o[0]:o[2]]
    gv = w_in[:, o[2]:o[3]]
    glr = w_in[:, o[3]:o[4]]
    gr = w_in[:, o[4]:o[5]]
    nq = w_in[:, o[5]:o[6]]
    nkv = w_in[:, o[6]:o[7]]
    ng = w_in[:, o[7]:o[8]]
    pad = jnp.zeros((D_MODEL, LANES - GLA_GATE_RANK - NSA_HEADS * 3), w_in.dtype)
    return jnp.concatenate([gq_gk, gv, gr, nq, nkv, glr, ng, pad], axis=1).astype(BF16)


def _inproj0_kernel(x_ref, g_ref, sh_ref, sc_ref, w_ref, *o_refs):
    h = _rms_mod(x_ref[0], g_ref[...], sh_ref[0], sc_ref[0]).astype(BF16)
    off = 0
    for o_ref, wd in zip(o_refs, INPROJ_WIDTHS):
        o_ref[0] = jnp.dot(h, w_ref[:, off:off + wd], preferred_element_type=F32)
        off += wd


def inproj0(x, g, shift, scale, w_arranged):
    tm = INPROJ_TM
    wtot = sum(INPROJ_WIDTHS)
    row = lambda b, i: (b, i, 0)
    vec = lambda b, i: (b, 0, 0)
    return pl.pallas_call(
        _inproj0_kernel,
        grid=(BATCH, SEQ // tm),
        in_specs=[
            pl.BlockSpec((1, tm, D_MODEL), row),
            pl.BlockSpec((1, D_MODEL), lambda b, i: (0, 0)),
            pl.BlockSpec((1, 1, D_MODEL), vec),
            pl.BlockSpec((1, 1, D_MODEL), vec),
            pl.BlockSpec((D_MODEL, wtot), lambda b, i: (0, 0)),
        ],
        out_specs=[pl.BlockSpec((1, tm, wd), row) for wd in INPROJ_WIDTHS],
        out_shape=[jax.ShapeDtypeStruct((BATCH, SEQ, wd), F32) for wd in INPROJ_WIDTHS],
        compiler_params=_cparams(("arbitrary", "arbitrary")),
        name="inproj0",
    )(x, g.reshape(1, D_MODEL), shift.reshape(BATCH, 1, D_MODEL), scale.reshape(BATCH, 1, D_MODEL), w_arranged)


GLA_TG = 512


def _gla_chunk_sums():
    i = np.arange(GLA_TG)[:, None]
    j = np.arange(GLA_TG)[None, :]
    same = (i // GLA_CHUNK) == (j // GLA_CHUNK)
    m3 = np.concatenate([same & (j <= i), same & (j % GLA_CHUNK <= GLA_CHUNK // 2), same], axis=0).astype(np.float32)
    return jnp.asarray(np.concatenate([m3, m3], axis=1), BF16)


def _gla_kernel(qk_ref, v_ref, r_ref, misc_ref, w2_ref, bg_ref, og_ref, sums_ref, o_ref, st_ref):
    C, tg = GLA_CHUNK, GLA_TG

    @pl.when(pl.program_id(0) == 0)
    def _():
        st_ref[...] = jnp.zeros_like(st_ref)

    lane = lax.broadcasted_iota(jnp.int32, (1, GLA_QK_W), 1)
    heads = [(lane >= h * GLA_DK) & (lane < (h + 1) * GLA_DK) for h in range(GLA_HEADS)]
    stack = lambda per_head, rows: jnp.concatenate([t[rows] for t in per_head], axis=0)
    stacked_row = lax.broadcasted_iota(jnp.int32, (GLA_HEADS * C, C), 0)
    causal = (stacked_row & (C - 1)) >= lax.broadcasted_iota(jnp.int32, (GLA_HEADS * C, C), 1)
    og = og_ref[...]

    def prepare(b):
        z = jnp.dot(misc_ref[b], w2_ref[...], precision=HI, preferred_element_type=F32) + bg_ref[...]
        la = _log_sigmoid(z) / GLA_TAU
        la_hi = la.astype(BF16)
        la_lo = (la - la_hi.astype(F32)).astype(BF16)
        sums = jnp.dot(sums_ref[...], jnp.concatenate([la_hi, la_lo], axis=0), preferred_element_type=F32)
        bc, b_mid, b_last = sums[0:tg], sums[tg:2 * tg], sums[2 * tg:3 * tg]
        q = qk_ref[b, :, 0:GLA_QK_W] * (GLA_DK ** -0.5)
        k = qk_ref[b, :, GLA_QK_W:2 * GLA_QK_W]
        qd = q * jnp.exp(bc - b_mid)
        kl = k * jnp.exp(b_last - bc)
        qb = q * jnp.exp(bc)
        per_head = lambda t: [jnp.where(m, t, 0.0).astype(BF16) for m in heads]
        return dict(kd=(k * jnp.exp(b_mid - bc)).astype(BF16), dec=jnp.exp(b_last), qd_h=per_head(qd),
                    qb_h=per_head(qb), kl_h=per_head(kl))

    batches = range(BATCH)
    pre = [prepare(b) for b in batches]
    st = [st_ref[b] for b in batches]
    for c in range(tg // C):
        rows = slice(c * C, (c + 1) * C)
        for b in batches:
            p = pre[b]
            v = v_ref[b, rows, :].astype(BF16)
            s = lax.dot_general(stack(p["qd_h"], rows), p["kd"][rows], NT_DIMS, preferred_element_type=F32)
            s = jnp.where(causal, s, 0.0).astype(BF16)
            o_intra = jnp.dot(s, v, preferred_element_type=F32)
            o_inter = lax.dot_general(stack(p["qb_h"], rows), st[b].astype(BF16), NT_DIMS, preferred_element_type=F32)
            v_stack = jnp.concatenate([v[:, h * GLA_DV:(h + 1) * GLA_DV] for h in range(GLA_HEADS)], axis=0)
            st[b] = st[b] * p["dec"][c * C:c * C + 1] + lax.dot_general(v_stack, stack(p["kl_h"], rows), TN_DIMS,
                                                                       preferred_element_type=F32)
            for h in range(GLA_HEADS):
                hrows = slice(h * C, (h + 1) * C)
                vcols = slice(h * GLA_DV, (h + 1) * GLA_DV)
                o = o_intra[hrows, vcols] + o_inter[hrows]
                on = o * lax.rsqrt(jnp.mean(o * o, axis=-1, keepdims=True) + NORM_EPS) * og
                o_ref[b, rows, vcols] = on * _silu(r_ref[b, rows, vcols])
    for b in batches:
        st_ref[b] = st[b]


def gla_mixer(qk, v, r, misc, w_gate2, b_gate, out_g):
    tg = GLA_TG
    w2 = jnp.zeros((LANES, GLA_QK_W), F32).at[:GLA_GATE_RANK].set(w_gate2)
    row = lambda i: (0, i, 0)
    const = lambda i: (0, 0)
    return pl.pallas_call(
        _gla_kernel,
        grid=(SEQ // tg,),
        in_specs=[
            pl.BlockSpec((BATCH, tg, 2 * GLA_QK_W), row),
            pl.BlockSpec((BATCH, tg, GLA_V_W), row),
            pl.BlockSpec((BATCH, tg, GLA_V_W), row),
            pl.BlockSpec((BATCH, tg, LANES), row),
            pl.BlockSpec((LANES, GLA_QK_W), const),
            pl.BlockSpec((1, GLA_QK_W), const),
            pl.BlockSpec((1, GLA_DV), const),
            pl.BlockSpec((3 * tg, 2 * tg), const),
        ],
        out_specs=pl.BlockSpec((BATCH, tg, GLA_V_W), row),
        out_shape=jax.ShapeDtypeStruct((BATCH, SEQ, GLA_V_W), F32),
        scratch_shapes=[pltpu.VMEM((BATCH, GLA_DV, GLA_QK_W), F32)],
        compiler_params=_cparams(("arbitrary",)),
        name="gla_mixer",
    )(qk, v, r, misc, w2, b_gate.reshape(1, GLA_QK_W), out_g.reshape(1, GLA_DV), _gla_chunk_sums())


POS_SIDE = 128


def _rope_table_kernel(freq_ref, pos_ref, cos_ref, sin_ref):
    pos = pos_ref[...].astype(F32)
    for f in range(ROT_HALF):
        ang = pos * freq_ref[f]
        cos_ref[f] = jnp.cos(ang)
        sin_ref[f] = jnp.sin(ang)


def rope_tables(positions):
    inv_freq = jnp.float32(ROPE_THETA) ** (-jnp.arange(ROT_HALF, dtype=F32) / ROT_HALF)
    shp = jax.ShapeDtypeStruct((ROT_HALF, POS_SIDE, POS_SIDE), F32)
    cos, sin = pl.pallas_call(
        _rope_table_kernel,
        in_specs=[pl.BlockSpec(memory_space=pltpu.SMEM), pl.BlockSpec(memory_space=pltpu.VMEM)],
        out_specs=[pl.BlockSpec(memory_space=pltpu.VMEM)] * 2,
        out_shape=[shp, shp],
        name="rope_tables",
    )(inv_freq, positions.reshape(POS_SIDE, POS_SIDE))
    return jnp.concatenate([cos, sin], axis=0).reshape(ROT_DIM, N_TOK).T.reshape(BATCH, SEQ, ROT_DIM)


def _rope_placement():
    place = np.zeros((ROT_DIM, 3 * LANES), np.float32)
    const = np.zeros((1, 3 * LANES), np.float32)
    for lane in range(LANES):
        i = lane % NSA_DH
        if i < ROT_HALF:
            place[i, lane] = 1.0
            place[ROT_HALF + i, LANES + lane] = -1.0
        elif i < ROT_DIM:
            place[i - ROT_HALF, lane] = 1.0
            place[i, 2 * LANES + lane] = 1.0
        else:
            const[0, lane] = 1.0
    return jnp.asarray(place), jnp.asarray(const)


def _lane_tables(cs, place_ref, const_ref):
    tab = jnp.dot(cs, place_ref[...], precision=HI, preferred_element_type=F32) + const_ref[...]
    return tab[:, 0:LANES], tab[:, LANES:2 * LANES], tab[:, 2 * LANES:3 * LANES]


def _block_diag_ones2(width):
    h = np.arange(width) // NSA_DH
    bd = (h[:, None] == h[None, :]).astype(np.float32)
    return jnp.asarray(np.concatenate([bd, bd], axis=0), BF16)


def _head_norm_rope(x, gain, bd2, c, sm, sp):
    width = x.shape[-1]
    reps = width // LANES
    sq = x * x
    sq_hi = sq.astype(BF16)
    sq_lo = (sq - sq_hi.astype(F32)).astype(BF16)
    ss = jnp.dot(jnp.concatenate([sq_hi, sq_lo], axis=1), bd2, preferred_element_type=F32)
    y = x * lax.rsqrt(ss * (1.0 / NSA_DH) + NORM_EPS) * gain
    tile = lambda t: jnp.concatenate([t] * reps, axis=1) if reps > 1 else t
    return (y * tile(c) + pltpu.roll(y, width - ROT_HALF, 1) * tile(sm) + pltpu.roll(y, ROT_HALF, 1) * tile(sp))


PREP_TM = 512


def _prep_kernel(q_ref, kv_ref, cs_ref, place_ref, const_ref, gq_ref, gk_ref, bdq_ref, bdk_ref,
                 qo_ref, kso_ref, kwo_ref, vso_ref, vwo_ref):
    tm = PREP_TM
    c, sm, sp = _lane_tables(cs_ref[0], place_ref, const_ref)
    bdk = bdk_ref[...]
    q = _head_norm_rope(q_ref[0], gq_ref[...], bdq_ref[...], c, sm, sp) * (NSA_DH ** -0.5)
    qo_ref[0] = q.T.reshape(NSA_HEADS, NSA_DH, tm)
    kv_cols = lambda n: kv_ref[0, :, n * NSA_KV_W:(n + 1) * NSA_KV_W]
    ks = _head_norm_rope(kv_cols(2), gk_ref[0:1, :], bdk, c, sm, sp)
    kw = _head_norm_rope(kv_cols(4), gk_ref[1:2, :], bdk, c, sm, sp)
    lane = lax.broadcasted_iota(jnp.int32, (tm, LANES), 1)
    token = pl.program_id(1) * tm + lax.broadcasted_iota(jnp.int32, (tm, LANES), 0)
    block_col = NSA_DH + jnp.right_shift(token & (SA_TK - 1), 6)
    onehot = jnp.where(lane == block_col, 1.0, 0.0)
    for g in range(NSA_KV_GROUPS):
        to_front = lambda t: t if g == 0 else pltpu.roll(t, NSA_DH, 1)
        kso_ref[0, g] = jnp.where(lane < NSA_DH, to_front(ks), onehot).astype(BF16)
        kwo_ref[0, g] = jnp.where(lane < NSA_DH, to_front(kw), 0.0).astype(BF16)
    tail = jnp.where(lax.broadcasted_iota(jnp.int32, (VT_ROWS - NSA_DH, tm), 0) == 0, 1.0, 0.0)
    for n, out_ref in ((3, vso_ref), (5, vwo_ref)):
        v_t = kv_cols(n).T
        for g in range(NSA_KV_GROUPS):
            out_ref[0, g] = jnp.concatenate([v_t[g * NSA_DH:(g + 1) * NSA_DH], tail], axis=0).astype(BF16)


def nsa_prep(nq, nkv, cs, q_gain, k_gain):
    tm = PREP_TM
    G = NSA_KV_GROUPS
    row = lambda b, i: (b, i, 0)
    const = lambda b, i: (0, 0)
    gq = jnp.tile(q_gain, NSA_HEADS).reshape(1, NSA_Q_W)
    gk = jnp.stack([jnp.tile(k_gain[1], NSA_KV_GROUPS), jnp.tile(k_gain[2], NSA_KV_GROUPS)])
    kslab = pl.BlockSpec((1, G, tm, LANES), lambda b, i: (b, 0, i, 0))
    vslab = pl.BlockSpec((1, G, VT_ROWS, tm), lambda b, i: (b, 0, 0, i))
    return pl.pallas_call(
        _prep_kernel,
        grid=(BATCH, SEQ // tm),
        in_specs=[
            pl.BlockSpec((1, tm, NSA_Q_W), row),
            pl.BlockSpec((1, tm, 6 * NSA_KV_W), row),
            pl.BlockSpec((1, tm, ROT_DIM), row),
            pl.BlockSpec((ROT_DIM, 3 * LANES), const),
            pl.BlockSpec((1, 3 * LANES), const),
            pl.BlockSpec((1, NSA_Q_W), const),
            pl.BlockSpec((2, NSA_KV_W), const),
            pl.BlockSpec((2 * NSA_Q_W, NSA_Q_W), const),
            pl.BlockSpec((2 * NSA_KV_W, NSA_KV_W), const),
        ],
        out_specs=[pl.BlockSpec((1, NSA_HEADS, NSA_DH, tm), lambda b, i: (b, 0, 0, i)), kslab, kslab, vslab, vslab],
        out_shape=[jax.ShapeDtypeStruct((BATCH, NSA_HEADS, NSA_DH, SEQ), F32),
                   jax.ShapeDtypeStruct((BATCH, G, SEQ, LANES), BF16), jax.ShapeDtypeStruct((BATCH, G, SEQ, LANES), BF16),
                   jax.ShapeDtypeStruct((BATCH, G, VT_ROWS, SEQ), BF16), jax.ShapeDtypeStruct((BATCH, G, VT_ROWS, SEQ), BF16)],
        compiler_params=_cparams(("arbitrary", "arbitrary")),
        name="nsa_prep",
    )(nq, nkv, cs, *_rope_placement(), gq, gk, _block_diag_ones2(NSA_Q_W), _block_diag_ones2(NSA_KV_W))


SEG_W = CMP_STRIDE * NSA_DH


def _cmp_kernel(xk_ref, xv_ref, pe_ref, w1_ref, w2_ref, gain_ref, cs_ref, place_ref, const_ref, bd_ref, ko_ref, vo_ref):
    def compress(x_ref, kv):
        out = jnp.zeros((N_CMP_PAD, LANES), F32)
        for g in range(NSA_KV_GROUPS):
            x = x_ref[0, g]
            ha = jnp.dot(x + pe_ref[kv, 0], w1_ref[kv, 0:SEG_W, :], precision=HI, preferred_element_type=F32)
            hb = jnp.dot(x + pe_ref[kv, 1], w1_ref[kv, SEG_W:2 * SEG_W, :], precision=HI, preferred_element_type=F32)
            hid = ha + pltpu.roll(hb, N_CMP_PAD - 1, 0)
            out += jnp.dot(jax.nn.gelu(hid), w2_ref[kv, g], precision=HI, preferred_element_type=F32)
        return out

    c, sm, sp = _lane_tables(cs_ref[0], place_ref, const_ref)
    ko_ref[0] = _head_norm_rope(compress(xk_ref, 0), gain_ref[...], bd_ref[...], c, sm, sp)
    vo_ref[0] = compress(xv_ref, 1)


def nsa_compress(xk, xv, cmp_pe, cmp_w1, cmp_w2, k_gain0, cs_last):
    pe = cmp_pe.reshape(2, 2, 1, SEG_W)
    w2 = jnp.zeros((2, NSA_KV_GROUPS, CMP_HIDDEN, LANES), F32)
    for g in range(NSA_KV_GROUPS):
        w2 = w2.at[:, g, :, g * NSA_DH:(g + 1) * NSA_DH].set(cmp_w2)
    seg = pl.BlockSpec((1, NSA_KV_GROUPS, N_CMP_PAD, SEG_W), lambda b: (b, 0, 0, 0))
    tab = pl.BlockSpec((1, N_CMP_PAD, LANES), lambda b: (b, 0, 0))
    full = lambda shape: pl.BlockSpec(shape, lambda b: (0,) * len(shape))
    return pl.pallas_call(
        _cmp_kernel,
        grid=(BATCH,),
        in_specs=[seg, seg, full((2, 2, 1, SEG_W)), full((2, 2 * SEG_W, CMP_HIDDEN)),
                  full((2, NSA_KV_GROUPS, CMP_HIDDEN, LANES)), full((1, LANES)),
                  pl.BlockSpec((1, N_CMP_PAD, ROT_DIM), lambda b: (b, 0, 0)), full((ROT_DIM, 3 * LANES)),
                  full((1, 3 * LANES)), full((2 * LANES, LANES))],
        out_specs=[tab, tab],
        out_shape=[jax.ShapeDtypeStruct((BATCH, N_CMP_PAD, LANES), F32)] * 2,
        compiler_params=_cparams(("arbitrary",)),
        name="nsa_compress",
    )(xk, xv, pe, cmp_w1, w2, jnp.tile(k_gain0, NSA_KV_GROUPS).reshape(1, LANES), cs_last, *_rope_placement(),
      _block_diag_ones2(LANES))


CA_TQ = 512
SUBLANES = 8


CA_COLS = NSA_HPG * CA_TQ
CMP_PER_SEL = SEL_BLOCK // CMP_STRIDE
TOPK_BANDS = 4


def split3_keys(k):
    hi = k.astype(BF16)
    lo = (k - hi.astype(F32)).astype(BF16)
    return jnp.concatenate([hi, lo, hi], axis=-1)


def _top_k_rows(score, k):
    rows, cols = score.shape
    row = lax.broadcasted_iota(jnp.int32, (rows, cols), 0).astype(F32)
    taken = jnp.zeros((rows, cols), F32)
    left = score
    for _ in range(k):
        top = jnp.max(left, axis=0, keepdims=True)
        first = jnp.min(jnp.where(left == top, row, float(rows)), axis=0, keepdims=True)
        hit = row == first
        taken = jnp.where(hit, 1.0, taken)
        left = jnp.where(hit, -jnp.inf, left)
    return taken


def _cattn_kernel(q_ref, kc_ref, vct_ref, gl_ref, o_ref, sel_ref, q3_ref, ps_ref):
    tq = CA_TQ
    q0 = pl.program_id(1) * tq
    lanes4 = lambda t: jnp.concatenate([t] * NSA_HPG, axis=1)
    cend = lax.broadcasted_iota(jnp.int32, (N_CMP_PAD, tq), 0) * CMP_STRIDE + (CMP_LEN - 1)
    tc = q0 + lax.broadcasted_iota(jnp.int32, (N_CMP_PAD, tq), 1)
    cmask = lanes4(cend <= tc)
    jj = lax.broadcasted_iota(jnp.int32, (N_SEL, tq), 0)
    tt = q0 + lax.broadcasted_iota(jnp.int32, (N_SEL, tq), 1)
    cur = jnp.right_shift(tt, 6)
    forced = (jj == 0) | (jj == cur) | (jj == cur - 1)
    valid = jj * SEL_BLOCK <= tt

    for g in range(NSA_KV_GROUPS):
        heads = range(g * NSA_HPG, (g + 1) * NSA_HPG)
        for n, h in enumerate(heads):
            q = q_ref[0, h]
            hi = q.astype(BF16)
            lo = (q - hi.astype(F32)).astype(BF16)
            for t, part in enumerate((hi, hi, lo)):
                q3_ref[g, t * NSA_DH:(t + 1) * NSA_DH, n * tq:(n + 1) * tq] = part
        s = jnp.dot(kc_ref[0, g], q3_ref[g], preferred_element_type=F32)
        s = jnp.where(cmask, s, NEG_INF)
        m = jnp.max(s, axis=0, keepdims=True)
        e = jnp.where(cmask, jnp.exp(s - m), 0.0)
        l = jnp.sum(e, axis=0, keepdims=True)
        p = e / jnp.where(l > 0.0, l, 1.0)
        gate = jnp.concatenate([jax.nn.sigmoid(gl_ref[0, h, 0:1, :]) for h in heads], axis=1)
        o = jnp.dot(vct_ref[0, g], p.astype(BF16), preferred_element_type=F32) * gate
        for n, h in enumerate(heads):
            o_ref[0, h] = o[:, n * tq:(n + 1) * tq]
        psum = functools.reduce(jnp.add, [p[:, n * tq:(n + 1) * tq] for n in range(NSA_HPG)])
        for n in range(tq // LANES):
            ps_ref[g, n] = psum[:, n * LANES:(n + 1) * LANES]

        every4th = lambda r: jnp.concatenate(
            [ps_ref[g, n, pl.ds(r, N_SEL, stride=CMP_PER_SEL), :] for n in range(tq // LANES)], axis=1)
        starts_in = [every4th(r) for r in range(CMP_PER_SEL)]
        from_prev = jnp.where(jj >= 1, pltpu.roll(starts_in[CMP_PER_SEL - 1], 1, 0), 0.0)
        imp = functools.reduce(jnp.add, starts_in) + from_prev
        score = jnp.where(valid, imp + jnp.where(forced, FORCE_BONUS, 0.0), NEG_INF)
        step = pl.program_id(1)
        steps_per_band = (SEQ // tq) // TOPK_BANDS
        for band in range(TOPK_BANDS):
            n_rows = (band + 1) * (N_SEL // TOPK_BANDS)

            @pl.when((step >= band * steps_per_band) & (step < (band + 1) * steps_per_band))
            def _():
                taken = _top_k_rows(score[0:n_rows], SEL_TOPK)
                sel_ref[0, g, 0:n_rows, :] = jnp.where(valid[0:n_rows], taken, 0.0)
                if n_rows < N_SEL:
                    sel_ref[0, g, n_rows:N_SEL, :] = jnp.zeros((N_SEL - n_rows, tq), F32)


def nsa_cmp_attn(q_t, kcmp, vcmp_t, gl_t):
    tq = CA_TQ
    G = NSA_KV_GROUPS
    return pl.pallas_call(
        _cattn_kernel,
        grid=(BATCH, SEQ // tq),
        in_specs=[
            pl.BlockSpec((1, NSA_HEADS, NSA_DH, tq), lambda b, i: (b, 0, 0, i)),
            pl.BlockSpec((1, G, N_CMP_PAD, 3 * NSA_DH), lambda b, i: (b, 0, 0, 0)),
            pl.BlockSpec((1, G, NSA_DH, N_CMP_PAD), lambda b, i: (b, 0, 0, 0)),
            pl.BlockSpec((1, NSA_HEADS, 3, tq), lambda b, i: (b, 0, 0, i)),
        ],
        out_specs=[pl.BlockSpec((1, NSA_HEADS, NSA_DH, tq), lambda b, i: (b, 0, 0, i)),
                   pl.BlockSpec((1, G, N_SEL, tq), lambda b, i: (b, 0, 0, i))],
        out_shape=[jax.ShapeDtypeStruct((BATCH, NSA_HEADS, NSA_DH, SEQ), F32),
                   jax.ShapeDtypeStruct((BATCH, G, N_SEL, SEQ), F32)],
        scratch_shapes=[pltpu.VMEM((G, 3 * NSA_DH, CA_COLS), BF16), pltpu.VMEM((G, tq // LANES, N_CMP_PAD, LANES), F32)],
        compiler_params=_cparams(("arbitrary", "arbitrary")),
        name="nsa_cmp_attn",
    )(q_t, split3_keys(kcmp), vcmp_t.astype(BF16), gl_t)


SA_TQ = 256
SA_TK = 1024
SA_PARTS = 2
SA_PART = SA_TK // SA_PARTS
M_INIT = -1e20


SA_COLS = NSA_HPG * SA_TQ
SA_BLOCKS = SA_TK // SEL_BLOCK


VT_ROWS = NSA_DH + 16


def _sattn_kernel(q_ref, k_ref, vt_ref, sel_ref, gl_ref, prev_ref, o_ref, qa_ref, acc_ref, s_ref, m_ref):
    tq, tk = SA_TQ, SA_TK
    i = pl.program_id(1)
    groups = range(NSA_KV_GROUPS)
    slots = range(2)
    for g in groups:
        for h in range(NSA_HPG):
            q = q_ref[0, g * NSA_HPG + h].astype(BF16)
            for slot in slots:
                qa_ref[slot, g, 0:NSA_DH, h * tq:(h + 1) * tq] = q
        for slot in slots:
            qa_ref[slot, g, NSA_DH:LANES, :] = jnp.zeros((LANES - NSA_DH, SA_COLS), BF16)
    acc_ref[...] = jnp.zeros_like(acc_ref)
    lanes4 = lambda t: jnp.concatenate([t] * NSA_HPG, axis=1)
    part_keys = lambda kt, part: pl.ds(pl.multiple_of(kt * tk + part * SA_PART, SA_PART), SA_PART)

    def scores(kt, slot):
        for g in groups:
            selrows = sel_ref[0, g, pl.ds(pl.multiple_of(kt * SA_BLOCKS, SA_BLOCKS), SA_BLOCKS), :]
            qa_ref[slot, g, NSA_DH:NSA_DH + SA_BLOCKS, :] = lanes4(jnp.where(selrows > 0.5, 0.0, NEG_INF)).astype(BF16)
            for part in range(SA_PARTS):
                s = jnp.dot(k_ref[0, g, part_keys(kt, part), :], qa_ref[slot, g], preferred_element_type=F32)
                s_ref[slot, g, part] = s.astype(BF16)

    def absorb(kt, slot, ms):
        out = []
        for g in groups:
            ss = [s_ref[slot, g, part] for part in range(SA_PARTS)]
            m_tile = functools.reduce(jnp.maximum, [jnp.max(s, axis=0, keepdims=True) for s in ss])
            m_new = jnp.maximum(ms[g], m_tile.astype(F32))
            acc = jnp.exp(ms[g] - m_new) * acc_ref[g]
            for part in range(SA_PARTS):
                p = jnp.exp(ss[part] - m_new.astype(BF16))
                acc += jnp.dot(vt_ref[0, g, :, part_keys(kt, part)], p, preferred_element_type=F32)
            acc_ref[g] = acc
            out.append(m_new)
        return tuple(out)

    def two_tiles(j, ms):
        kt = 2 * j
        scores(kt + 1, 1)
        ms = absorb(kt, 0, ms)
        scores(kt + 2, 0)
        return absorb(kt + 1, 1, ms)

    n_full = (i * tq) // tk
    scores(0, 0)
    m0 = tuple(jnp.full((1, SA_COLS), M_INIT, F32) for _ in groups)
    ms = lax.fori_loop(0, n_full // 2, two_tiles, m0)
    for g in groups:
        m_ref[g] = ms[g]

    def last_tile(slot):
        start = i * tq - n_full * tk
        part, row0 = start // SA_PART, pl.multiple_of(start % SA_PART, tq)
        tri = lax.broadcasted_iota(jnp.int32, (tq, tq), 0) <= lax.broadcasted_iota(jnp.int32, (tq, tq), 1)
        bias = lanes4(jnp.where(tri, 0.0, NEG_INF)).astype(BF16)
        for g in groups:
            s_ref[slot, g, part, pl.ds(row0, tq), :] += bias
        for g, m in enumerate(absorb(n_full, slot, tuple(m_ref[g] for g in groups))):
            m_ref[g] = m

    @pl.when(n_full % 2 == 0)
    def _():
        last_tile(0)

    @pl.when(n_full % 2 == 1)
    def _():
        scores(n_full, 1)
        for g, m in enumerate(absorb(n_full - 1, 0, tuple(m_ref[g] for g in groups))):
            m_ref[g] = m
        last_tile(1)

    for g in groups:
        heads = range(g * NSA_HPG, (g + 1) * NSA_HPG)
        gate = jnp.concatenate([jax.nn.sigmoid(gl_ref[0, h, 1:2, :]) for h in heads], axis=1)
        out = acc_ref[g, 0:NSA_DH, :] / acc_ref[g, NSA_DH:NSA_DH + 1, :] * gate
        for n, h in enumerate(heads):
            o_ref[0, h] = prev_ref[0, h] + out[:, n * tq:(n + 1) * tq]


def nsa_sel_attn(q_t, k_slab, vsel_t, sel_t, gl_t, prev):
    tq = SA_TQ
    G = NSA_KV_GROUPS
    ospec = pl.BlockSpec((1, NSA_HEADS, NSA_DH, tq), lambda b, i: (b, 0, 0, i))
    return pl.pallas_call(
        _sattn_kernel,
        grid=(BATCH, SEQ // tq),
        in_specs=[
            ospec,
            pl.BlockSpec((1, G, SEQ, LANES), lambda b, i: (b, 0, 0, 0)),
            pl.BlockSpec((1, G, VT_ROWS, SEQ), lambda b, i: (b, 0, 0, 0)),
            pl.BlockSpec((1, G, N_SEL, tq), lambda b, i: (b, 0, 0, i)),
            pl.BlockSpec((1, NSA_HEADS, 3, tq), lambda b, i: (b, 0, 0, i)),
            ospec,
        ],
        out_specs=ospec,
        out_shape=jax.ShapeDtypeStruct((BATCH, NSA_HEADS, NSA_DH, SEQ), F32),
        scratch_shapes=[pltpu.VMEM((2, G, LANES, SA_COLS), BF16), pltpu.VMEM((G, VT_ROWS, SA_COLS), F32),
                        pltpu.VMEM((2, G, SA_PARTS, SA_PART, SA_COLS), BF16), pltpu.VMEM((G, 1, SA_COLS), F32)],
        input_output_aliases={5: 0},
        compiler_params=_cparams(("arbitrary", "arbitrary")),
        name="nsa_sel_attn",
    )(q_t, k_slab, vsel_t, sel_t, gl_t, prev)


WA_TQ = 256
WA_TILES = WINDOW // WA_TQ + 1


def _window_bias():
    kl = np.arange(WA_TILES * WA_TQ)[:, None]
    ql = np.arange(WA_TQ)[None, :]
    diff = ql - kl + WINDOW
    return jnp.asarray(np.where((diff >= 0) & (diff < WINDOW), 0.0, NEG_INF).astype(np.float32))


def _wattn_kernel(q_ref, k0_ref, k1_ref, k2_ref, v0_ref, v1_ref, v2_ref, bias_ref, gl_ref, prev_ref, o_ref):
    tq = WA_TQ
    i = pl.program_id(1)
    k_refs = (k0_ref, k1_ref, k2_ref)
    v_refs = (v0_ref, v1_ref, v2_ref)
    lanes4 = lambda t: jnp.concatenate([t] * NSA_HPG, axis=1)
    biases = []
    for d in range(WA_TILES):
        in_seq = i - (WA_TILES - 1) + d >= 0
        biases.append(lanes4(jnp.where(in_seq, bias_ref[d * tq:(d + 1) * tq, :], NEG_INF)))
    for g in range(NSA_KV_GROUPS):
        heads = range(g * NSA_HPG, (g + 1) * NSA_HPG)
        q = jnp.concatenate([q_ref[0, h] for h in heads], axis=1).astype(BF16)
        q = jnp.concatenate([q, jnp.zeros_like(q)], axis=0)
        ss = [(jnp.dot(k_refs[d][0, g], q, preferred_element_type=F32) + biases[d]).astype(BF16) for d in range(WA_TILES)]
        m = functools.reduce(jnp.maximum, [jnp.max(s, axis=0, keepdims=True) for s in ss])
        acc = functools.reduce(jnp.add, [jnp.dot(v_refs[d][0, g], jnp.exp(ss[d] - m), preferred_element_type=F32)
                                         for d in range(WA_TILES)])
        gate = jnp.concatenate([jax.nn.sigmoid(gl_ref[0, h, 2:3, :]) for h in heads], axis=1)
        out = acc[0:NSA_DH] / acc[NSA_DH:NSA_DH + 1] * gate
        for n, h in enumerate(heads):
            o_ref[0, h] = prev_ref[0, h] + out[:, n * tq:(n + 1) * tq]


def nsa_win_attn(q_t, kwin, vwin_t, gl_t, prev):
    tq = WA_TQ
    G = NSA_KV_GROUPS
    qspec = pl.BlockSpec((1, NSA_HEADS, NSA_DH, tq), lambda b, i: (b, 0, 0, i))
    tile = lambda d: (lambda i: jnp.maximum(i - (WA_TILES - 1) + d, 0))
    kspec = lambda d: pl.BlockSpec((1, G, tq, LANES), lambda b, i: (b, 0, tile(d)(i), 0))
    vspec = lambda d: pl.BlockSpec((1, G, VT_ROWS, tq), lambda b, i: (b, 0, 0, tile(d)(i)))
    return pl.pallas_call(
        _wattn_kernel,
        grid=(BATCH, SEQ // tq),
        in_specs=[qspec] + [kspec(d) for d in range(WA_TILES)] + [vspec(d) for d in range(WA_TILES)] + [
            pl.BlockSpec((WA_TILES * tq, tq), lambda b, i: (0, 0)),
            pl.BlockSpec((1, NSA_HEADS, 3, tq), lambda b, i: (b, 0, 0, i)),
            qspec,
        ],
        out_specs=qspec,
        out_shape=jax.ShapeDtypeStruct((BATCH, NSA_HEADS, NSA_DH, SEQ), F32),
        input_output_aliases={2 * WA_TILES + 3: 0},
        compiler_params=_cparams(("arbitrary", "arbitrary")),
        name="nsa_win_attn",
    )(q_t, *([kwin] * WA_TILES), *([vwin_t] * WA_TILES), _window_bias(), gl_t, prev)


def nsa_mixer(nq, nkv, misc, positions, q_gain, k_gain, cmp_pe, cmp_w1, cmp_w2):
    cs = rope_tables(positions)
    q_t, ksel, kwin, vsel_t, vwin_t = nsa_prep(nq, nkv, cs, q_gain, k_gain)
    group_major = lambda t: t.reshape(BATCH, SEQ, NSA_KV_GROUPS, NSA_DH).transpose(0, 2, 1, 3)
    col = lambda n: nkv[..., n * NSA_KV_W:(n + 1) * NSA_KV_W]
    segs = lambda t: group_major(t).reshape(BATCH, NSA_KV_GROUPS, N_CMP_PAD, SEG_W)
    last = jnp.minimum(jnp.arange(N_CMP_PAD) * CMP_STRIDE + CMP_LEN - 1, SEQ - 1)
    kcmp, vcmp = nsa_compress(segs(col(0)), segs(col(1)), cmp_pe, cmp_w1, cmp_w2, k_gain[0], cs[:, last])
    kcmp = kcmp.reshape(BATCH, N_CMP_PAD, NSA_KV_GROUPS, NSA_DH).transpose(0, 2, 1, 3)
    vcmp_t = vcmp.reshape(BATCH, N_CMP_PAD, NSA_KV_GROUPS, NSA_DH).transpose(0, 2, 3, 1)
    gl_t = misc[..., GLA_GATE_RANK:GLA_GATE_RANK + NSA_HEADS * 3].reshape(BATCH, SEQ, NSA_HEADS, 3).transpose(0, 2, 3, 1)
    o_t, sel_t = nsa_cmp_attn(q_t, kcmp, vcmp_t, gl_t)
    o_t = nsa_sel_attn(q_t, ksel, vsel_t, sel_t, gl_t, o_t)
    return nsa_win_attn(q_t, kwin, vwin_t, gl_t, o_t)


ROUTE_ROWS = 8
HX_TERMS = 3
HX_W = D_MODEL + LANES


def _top2_sum(a, b, c, d):
    hi1, lo1 = jnp.maximum(a, b), jnp.minimum(a, b)
    hi2, lo2 = jnp.maximum(c, d), jnp.minimum(c, d)
    return jnp.maximum(hi1, hi2) + jnp.maximum(jnp.minimum(hi1, hi2), jnp.maximum(lo1, lo2))


def _moe_prenorm_route(xn, g_ref, sh_ref, sc_ref, wr_ref, rb_ref, hx_ref, route_ref):
    h = _rms_mod(xn, g_ref[...], sh_ref[0], sc_ref[0])
    logits = lax.dot_general(wr_ref[...], h, NT_DIMS, precision=HI, preferred_element_type=F32)
    scores = jax.nn.sigmoid(logits)
    sel = scores + rb_ref[...]
    epg = EXPERTS_PER_GROUP
    srow = lambda e: sel[e:e + 1, :]
    grp = [_top2_sum(*[srow(epg * g + r) for r in range(epg)]) for g in range(N_EXPERT_GROUPS)]
    best, gi = grp[0], jnp.zeros_like(grp[0], dtype=jnp.int32)
    for g in range(1, N_EXPERT_GROUPS):
        better = grp[g] > best
        gi = jnp.where(better, g, gi)
        best = jnp.where(better, grp[g], best)

    def in_group(mat, r):
        out = mat[r:r + 1, :]
        for g in range(1, N_EXPERT_GROUPS):
            out = jnp.where(gi == g, mat[epg * g + r:epg * g + r + 1, :], out)
        return out

    v = [in_group(sel, r) for r in range(epg)]
    sc = [in_group(scores, r) for r in range(epg)]
    b1, i1, w1 = v[0], jnp.zeros_like(gi), sc[0]
    for r in range(1, epg):
        better = v[r] > b1
        i1 = jnp.where(better, r, i1)
        w1 = jnp.where(better, sc[r], w1)
        b1 = jnp.where(better, v[r], b1)
    b2 = jnp.full_like(b1, -3e38)
    i2, w2 = jnp.zeros_like(gi), jnp.zeros_like(w1)
    for r in range(epg):
        better = (i1 != r) & (v[r] > b2)
        i2 = jnp.where(better, r, i2)
        w2 = jnp.where(better, sc[r], w2)
        b2 = jnp.where(better, v[r], b2)
    tot = w1 + w2
    w1, w2 = w1 / tot, w2 / tot
    zero = jnp.zeros_like(w1)
    route_ref[0] = jnp.concatenate([gi.astype(F32)] + [zero] * (ROUTE_ROWS - 1), axis=0)
    w = jnp.concatenate([jnp.where(i1 == r, w1, jnp.where(i2 == r, w2, 0.0)) for r in range(epg)], axis=0)
    w_hi = w.astype(BF16).astype(F32)
    w_mid = (w - w_hi).astype(BF16).astype(F32)
    w_lo = (w - w_hi - w_mid).astype(BF16).astype(F32)
    pad = jnp.zeros((LANES - HX_TERMS * epg, w.shape[1]), F32)
    hx_ref[0, :, 0:D_MODEL] = h.astype(BF16)
    hx_ref[0, :, D_MODEL:HX_W] = jnp.concatenate([w_hi, w_mid, w_lo, pad], axis=0).T.astype(BF16)


def _route_specs(tm, row, vec, const):
    in_specs = [pl.BlockSpec((1, D_MODEL), const), pl.BlockSpec((1, 1, D_MODEL), vec), pl.BlockSpec((1, 1, D_MODEL), vec),
                pl.BlockSpec((N_EXPERTS, D_MODEL), const), pl.BlockSpec((N_EXPERTS, 1), const)]
    assert tm == MOE_CHUNK
    out_specs = [pl.BlockSpec((1, tm, HX_W), lambda b, i: (b * (SEQ // tm) + i, 0, 0)),
                 pl.BlockSpec((1, ROUTE_ROWS, tm), lambda b, i: (b, 0, i))]
    out_shape = [jax.ShapeDtypeStruct((MOE_CHUNKS, MOE_CHUNK, HX_W), BF16),
                 jax.ShapeDtypeStruct((BATCH, ROUTE_ROWS, SEQ), F32)]
    return in_specs, out_specs, out_shape


def _route_args(g, shift, scale, w_router, router_bias):
    return (g.reshape(1, D_MODEL), shift.reshape(BATCH, 1, D_MODEL), scale.reshape(BATCH, 1, D_MODEL),
            w_router.T, router_bias.reshape(N_EXPERTS, 1))


OUTPROJ_TM = 512


def _outproj0_kernel(oa_ref, ob_ref, w_ref, x_ref, gate_ref, g_ref, sh_ref, sc_ref, wr_ref, rb_ref,
                     xo_ref, h_ref, route_ref):
    y = jnp.dot(oa_ref[0].astype(BF16), w_ref[0:GLA_V_W, :], preferred_element_type=F32)
    ob_t = ob_ref[0].reshape(NSA_Q_W, OUTPROJ_TM).astype(BF16)
    y += lax.dot_general(ob_t, w_ref[GLA_V_W:GLA_V_W + NSA_Q_W, :], TN_DIMS, preferred_element_type=F32)
    xn = x_ref[0] + gate_ref[0] * y
    xo_ref[0] = xn
    _moe_prenorm_route(xn, g_ref, sh_ref, sc_ref, wr_ref, rb_ref, h_ref, route_ref)


def outproj0(o_a, o_b, w_out, x, gate, route_args):
    tm = OUTPROJ_TM
    row = lambda b, i: (b, i, 0)
    vec = lambda b, i: (b, 0, 0)
    const = lambda b, i: (0, 0)
    r_in, r_out, r_shape = _route_specs(tm, row, vec, const)
    return pl.pallas_call(
        _outproj0_kernel,
        grid=(BATCH, SEQ // tm),
        in_specs=[pl.BlockSpec((1, tm, GLA_V_W), row), pl.BlockSpec((1, NSA_HEADS, NSA_DH, tm), lambda b, i: (b, 0, 0, i)),
                  pl.BlockSpec((GLA_V_W + NSA_Q_W, D_MODEL), const), pl.BlockSpec((1, tm, D_MODEL), row),
                  pl.BlockSpec((1, 1, D_MODEL), vec)] + r_in,
        out_specs=[pl.BlockSpec((1, tm, D_MODEL), row)] + r_out,
        out_shape=[jax.ShapeDtypeStruct((BATCH, SEQ, D_MODEL), F32)] + r_shape,
        compiler_params=_cparams(("arbitrary", "arbitrary")),
        name="outproj0",
    )(o_a, o_b, w_out.astype(BF16), x, gate.reshape(BATCH, 1, D_MODEL), *route_args)


GMLP_TM = 512


def _gmlp_kernel(x_ref, g1_ref, sh1_ref, sc1_ref, win_ref, ng_ref, ws_ref, bs_ref, wout_ref, gate_ref,
                 g_ref, sh_ref, sc_ref, wr_ref, rb_ref, xo_ref, h_ref, route_ref, gated_ref, v_ref):
    x = x_ref[0]
    h = _rms_mod(x, g1_ref[...], sh1_ref[0], sc1_ref[0]).astype(BF16)
    group_cols = lambda g: slice(g * SGU_GROUP_DIM, (g + 1) * SGU_GROUP_DIM)
    ssq = jnp.zeros((GMLP_TM, LANES), F32)
    for g in range(SGU_GROUPS):
        lo = SGU_WIDTH + g * SGU_GROUP_DIM
        v = jax.nn.gelu(jnp.dot(h, win_ref[:, lo:lo + SGU_GROUP_DIM], preferred_element_type=F32))
        v_ref[:, group_cols(g)] = v
        ssq += functools.reduce(jnp.add, [v[:, n * LANES:(n + 1) * LANES] ** 2 for n in range(SGU_GROUP_DIM // LANES)])
    rs = lax.rsqrt(jnp.sum(ssq, axis=-1, keepdims=True) * (1.0 / SGU_WIDTH) + NORM_EPS)
    ri = lax.broadcasted_iota(jnp.int32, (SGU_CHUNK, SGU_CHUNK), 0)
    ci = lax.broadcasted_iota(jnp.int32, (SGU_CHUNK, SGU_CHUNK), 1)
    for g in range(SGU_GROUPS):
        cols = group_cols(g)
        u = jax.nn.gelu(jnp.dot(h, win_ref[:, cols], preferred_element_type=F32))
        vn = (v_ref[:, cols] * rs * ng_ref[:, cols]).astype(BF16)
        w = jnp.where(ri >= ci, ws_ref[g], 0.0).astype(BF16)
        for c in range(GMLP_TM // SGU_CHUNK):
            rows = slice(c * SGU_CHUNK, (c + 1) * SGU_CHUNK)
            mix = jnp.dot(w, vn[rows], preferred_element_type=F32) + bs_ref[:, g:g + 1]
            gated_ref[rows, cols] = (u[rows] * mix).astype(BF16)
    y = jnp.dot(gated_ref[...], wout_ref[...], preferred_element_type=F32)
    xn = x + gate_ref[0] * y
    xo_ref[0] = xn
    _moe_prenorm_route(xn, g_ref, sh_ref, sc_ref, wr_ref, rb_ref, h_ref, route_ref)


def gmlp_layer(x, g1, shift1, scale1, w_in, norm_g, w_s, b_s, w_out, gate, route_args):
    tm = GMLP_TM
    row = lambda b, i: (b, i, 0)
    vec = lambda b, i: (b, 0, 0)
    const = lambda b, i: (0, 0)
    r_in, r_out, r_shape = _route_specs(tm, row, vec, const)
    vspec = pl.BlockSpec((1, 1, D_MODEL), vec)
    return pl.pallas_call(
        _gmlp_kernel,
        grid=(BATCH, SEQ // tm),
        in_specs=[pl.BlockSpec((1, tm, D_MODEL), row), pl.BlockSpec((1, D_MODEL), const), vspec, vspec,
                  pl.BlockSpec((D_MODEL, 2 * SGU_WIDTH), const), pl.BlockSpec((1, SGU_WIDTH), const),
                  pl.BlockSpec((SGU_GROUPS, SGU_CHUNK, SGU_CHUNK), lambda b, i: (0, 0, 0)),
                  pl.BlockSpec((SGU_CHUNK, SGU_GROUPS), const), pl.BlockSpec((SGU_WIDTH, D_MODEL), const), vspec] + r_in,
        out_specs=[pl.BlockSpec((1, tm, D_MODEL), row)] + r_out,
        out_shape=[jax.ShapeDtypeStruct((BATCH, SEQ, D_MODEL), F32)] + r_shape,
        scratch_shapes=[pltpu.VMEM((tm, SGU_WIDTH), BF16), pltpu.VMEM((tm, SGU_WIDTH), F32)],
        compiler_params=_cparams(("arbitrary", "arbitrary"), vmem_mib=56),
        name="gmlp_layer",
    )(x, g1.reshape(1, D_MODEL), shift1.reshape(BATCH, 1, D_MODEL), scale1.reshape(BATCH, 1, D_MODEL),
      w_in.astype(BF16), norm_g.reshape(1, SGU_WIDTH), w_s, b_s.T, w_out.astype(BF16),
      gate.reshape(BATCH, 1, D_MODEL), *route_args)


MOE_TM = 256
MOE_CHUNK = 512
MOE_SORTED = N_TOK + N_EXPERT_GROUPS * MOE_TM
MOE_TILES = MOE_SORTED // MOE_TM
MOE_CHUNKS = N_TOK // MOE_CHUNK
MOE_PAIRS = MOE_TILES + N_EXPERT_GROUPS * MOE_CHUNKS
FLAG_ACTIVE, FLAG_FIRST, FLAG_LAST, FLAG_ZERO = 1, 2, 4, 8
EXP_WIN = 4
CMB_WIN = 8
MOE_TSTEPS = MOE_TILES + MOE_PAIRS // EXP_WIN
MOE_CSTEPS = MOE_CHUNKS + MOE_PAIRS // CMB_WIN


def _plan_kernel(gi_ref, rank_ref, before_ref):
    gi = gi_ref[...]
    r = lax.broadcasted_iota(jnp.int32, (POS_SIDE, POS_SIDE), 0)
    c = lax.broadcasted_iota(jnp.int32, (POS_SIDE, POS_SIDE), 1)
    upper = jnp.where(r <= c, 1.0, 0.0)
    lower_strict = jnp.where(c < r, 1.0, 0.0)
    rank = jnp.zeros((POS_SIDE, POS_SIDE), F32)
    for g in range(N_EXPERT_GROUPS):
        member = jnp.where(gi == g, 1.0, 0.0)
        in_row = jnp.dot(member, upper, precision=HI, preferred_element_type=F32)
        row_total = jnp.broadcast_to(in_row[:, POS_SIDE - 1:POS_SIDE], (POS_SIDE, POS_SIDE))
        before = jnp.dot(lower_strict, row_total, precision=HI, preferred_element_type=F32)
        before_ref[g] = before
        rank += member * (before + in_row - 1.0)
    rank_ref[...] = rank


def moe_plan(route):
    tm = MOE_TM
    i32 = jnp.int32
    gi_f = route[:, 0, :].reshape(POS_SIDE, POS_SIDE)
    rank, before = pl.pallas_call(
        _plan_kernel,
        out_shape=[jax.ShapeDtypeStruct((POS_SIDE, POS_SIDE), F32),
                   jax.ShapeDtypeStruct((N_EXPERT_GROUPS, POS_SIDE, POS_SIDE), F32)],
        name="moe_plan",
    )(gi_f)
    gi = gi_f.reshape(N_TOK).astype(i32)
    groups = jnp.arange(N_EXPERT_GROUPS, dtype=i32)
    member = gi[None, :] == groups[:, None]
    tot = jnp.sum(member, axis=1).astype(i32)
    padded = (tot + tm - 1) // tm * tm
    gend = jnp.cumsum(padded).astype(i32)
    gstart = gend - padded
    pos = jnp.sum(jnp.where(member, gstart[:, None], 0), axis=0).astype(i32) + rank.reshape(N_TOK).astype(i32)
    rows_per_chunk = MOE_CHUNK // POS_SIDE
    cnt_end = jnp.concatenate([before[:, rows_per_chunk::rows_per_chunk, 0].astype(i32), tot[:, None]], axis=1)
    t = jnp.arange(MOE_TILES, dtype=i32)
    n_used = gend[-1] // tm
    tile_g = jnp.minimum(jnp.sum(gend[None, :] <= (t * tm)[:, None], axis=1), N_EXPERT_GROUPS - 1).astype(i32)
    k0 = t * tm - gstart[tile_g]
    k1 = jnp.minimum(k0 + tm, tot[tile_g]) - 1
    ce = cnt_end[tile_g]
    c_lo = jnp.sum(ce <= k0[:, None], axis=1).astype(i32)
    c_hi = jnp.sum(ce <= k1[:, None], axis=1).astype(i32)
    npairs = jnp.where(t < n_used, c_hi - c_lo + 1, 0)
    pend = jnp.cumsum(npairs).astype(i32)
    pstart = pend - npairs
    total = pend[-1]
    l = jnp.arange(MOE_PAIRS, dtype=i32)
    real = l < total
    lt = jnp.minimum(l, total - 1)

    def windows(count, win, n_steps):
        per_item = (count + win - 1) // win
        end = jnp.cumsum(per_item).astype(i32)
        start = end - per_item
        s = jnp.arange(n_steps, dtype=i32)
        real_s = s < end[-1]
        sc = jnp.minimum(s, end[-1] - 1)
        item = jnp.sum(end[None, :] <= sc[:, None], axis=1).astype(i32)
        j = sc - start[item]
        flags_s = jnp.where(real_s, FLAG_ACTIVE + jnp.where(j == 0, FLAG_FIRST, 0)
                            + jnp.where(j == per_item[item] - 1, FLAG_LAST, 0), 0).astype(i32)
        return item, j, flags_s, real_s, s - end[-1]

    tile_s, j, flags, real_s, spare = windows(npairs, EXP_WIN, MOE_TSTEPS)
    c0 = c_lo[tile_s] + EXP_WIN * j
    n_valid = jnp.minimum(EXP_WIN, c_hi[tile_s] - c0 + 1).astype(i32)
    spare_tile = jnp.minimum(n_used + spare, MOE_TILES - 1)
    flags = jnp.where(real_s, flags, jnp.where(spare_tile >= n_used, FLAG_ZERO, 0)).astype(i32)
    tile_sched = jnp.where(real_s, tile_s, spare_tile).astype(i32)
    by_tile = (tile_sched, c0.astype(i32), n_valid, flags, tile_g[tile_sched])
    cc = jnp.arange(MOE_CHUNKS, dtype=i32)
    is_pair = (cc[:, None] >= c_lo[None, :]) & (cc[:, None] <= c_hi[None, :]) & (t[None, :] < n_used)
    seen = jnp.cumsum(is_pair.reshape(-1).astype(i32))
    flat = jnp.sum(seen[None, :] <= lt[:, None], axis=1).astype(i32)
    pair_tile = flat % MOE_TILES
    per_chunk = jnp.sum(is_pair, axis=1).astype(i32)
    first_pair = jnp.cumsum(per_chunk).astype(i32) - per_chunk
    chunk_s, j, flags_c, _, _ = windows(per_chunk, CMB_WIN, MOE_CSTEPS)
    base = first_pair[chunk_s] + CMB_WIN * j
    n_valid_c = jnp.minimum(CMB_WIN, per_chunk[chunk_s] - CMB_WIN * j).astype(i32)
    tiles_c = tuple(pair_tile[jnp.minimum(base + w, total - 1)] for w in range(CMB_WIN))
    by_chunk = (chunk_s, n_valid_c, flags_c) + tiles_c
    return pos.reshape(MOE_CHUNKS, 1, MOE_CHUNK), by_tile, by_chunk


def _one_hot_rows(pos_row, tile):
    rows = tile * MOE_TM + lax.broadcasted_iota(jnp.int32, (MOE_TM, MOE_CHUNK), 0)
    return jnp.where(pos_row == rows, 1.0, 0.0).astype(BF16)


def _moe_kernel(tile_ref, c0_ref, nv_ref, flag_ref, grp_ref, *refs):
    pos_refs, hx_refs = refs[0:EXP_WIN], refs[EXP_WIN:2 * EXP_WIN]
    wg_ref, wu_ref, wd_ref, y_ref, acc_ref = refs[2 * EXP_WIN:]
    l = pl.program_id(0)
    flags = flag_ref[l]

    @pl.when((flags & FLAG_FIRST) != 0)
    def _():
        acc_ref[...] = jnp.zeros_like(acc_ref)

    for w in range(EXP_WIN):
        @pl.when(((flags & FLAG_ACTIVE) != 0) & (w < nv_ref[l]))
        def _():
            onehot = _one_hot_rows(pos_refs[w][0], tile_ref[l])
            acc_ref[...] += jnp.dot(onehot, hx_refs[w][0], preferred_element_type=F32)

    @pl.when((flags & FLAG_LAST) != 0)
    def _():
        x = acc_ref[:, 0:D_MODEL].astype(BF16)
        terms = acc_ref[:, D_MODEL:HX_W]
        y = jnp.zeros((MOE_TM, D_MODEL), F32)
        for r in range(EXPERTS_PER_GROUP):
            lanes = [n * EXPERTS_PER_GROUP + r for n in range(HX_TERMS)]
            w_r = functools.reduce(jnp.add, [terms[:, c:c + 1] for c in lanes])
            gate = jnp.dot(x, wg_ref[0, 0, r], preferred_element_type=F32)
            up = jnp.dot(x, wu_ref[0, 0, r], preferred_element_type=F32)
            hid = (_silu(gate) * up * w_r).astype(BF16)
            y += jnp.dot(hid, wd_ref[0, 0, r], preferred_element_type=F32)
        y_ref[...] = y

    @pl.when((flags & FLAG_ZERO) != 0)
    def _():
        y_ref[...] = jnp.zeros_like(y_ref)


def moe_experts(hx, pos, by_tile, w_gate, w_up, w_down, layer):
    grouped = lambda w: w.reshape(DEPTH, N_EXPERT_GROUPS, EXPERTS_PER_GROUP, *w.shape[2:])
    wspec = lambda k, n: pl.BlockSpec((1, 1, EXPERTS_PER_GROUP, k, n), lambda l, t, c, n_, f, g: (layer, g[l], 0, 0, 0))
    chunk = lambda w: (lambda l, t, c, n_, f, g: (jnp.minimum(c[l] + w, MOE_CHUNKS - 1), 0, 0))
    return pl.pallas_call(
        _moe_kernel,
        grid_spec=pltpu.PrefetchScalarGridSpec(
            num_scalar_prefetch=5,
            grid=(MOE_TSTEPS,),
            in_specs=[pl.BlockSpec((1, 1, MOE_CHUNK), chunk(w)) for w in range(EXP_WIN)]
            + [pl.BlockSpec((1, MOE_CHUNK, HX_W), chunk(w)) for w in range(EXP_WIN)]
            + [wspec(D_MODEL, EXPERT_HIDDEN), wspec(D_MODEL, EXPERT_HIDDEN), wspec(EXPERT_HIDDEN, D_MODEL)],
            out_specs=pl.BlockSpec((MOE_TM, D_MODEL), lambda l, t, c, n_, f, g: (t[l], 0)),
            scratch_shapes=[pltpu.VMEM((MOE_TM, HX_W), F32)],
        ),
        out_shape=jax.ShapeDtypeStruct((MOE_SORTED, D_MODEL), F32),
        compiler_params=_cparams(("arbitrary",), vmem_mib=56),
        name="moe_experts",
    )(*by_tile, *([pos] * EXP_WIN), *([hx] * EXP_WIN), grouped(w_gate), grouped(w_up), grouped(w_down))


def _moe_combine_kernel(chunk_ref, nv_ref, flag_ref, *refs):
    tile_refs = refs[0:CMB_WIN]
    pos_ref = refs[CMB_WIN]
    y_refs = refs[CMB_WIN + 1:2 * CMB_WIN + 1]
    x_ref, gate_ref, o_ref, acc_ref = refs[2 * CMB_WIN + 1:]
    l = pl.program_id(0)
    flags = flag_ref[l]

    @pl.when((flags & FLAG_FIRST) != 0)
    def _():
        acc_ref[...] = jnp.zeros_like(acc_ref)

    for w in range(CMB_WIN):
        @pl.when(((flags & FLAG_ACTIVE) != 0) & (w < nv_ref[l]))
        def _():
            onehot = _one_hot_rows(pos_ref[0], tile_refs[w][l])
            y = y_refs[w][...]
            y_hi = y.astype(BF16)
            y_lo = (y - y_hi.astype(F32)).astype(BF16)
            acc_ref[...] += (lax.dot_general(onehot, y_hi, TN_DIMS, preferred_element_type=F32)
                             + lax.dot_general(onehot, y_lo, TN_DIMS, preferred_element_type=F32))

    @pl.when((flags & FLAG_LAST) != 0)
    def _():
        o_ref[0] = x_ref[0] + gate_ref[0] * acc_ref[...]


def moe_combine(x, y_sorted, pos, by_chunk, gate):
    per_b = SEQ // MOE_CHUNK
    tok = lambda l, c, *_: (c[l] // per_b, c[l] % per_b, 0)
    tile = lambda w: (lambda l, c, n_, f, *tiles: (tiles[w][l], 0))
    return pl.pallas_call(
        _moe_combine_kernel,
        grid_spec=pltpu.PrefetchScalarGridSpec(
            num_scalar_prefetch=3 + CMB_WIN,
            grid=(MOE_CSTEPS,),
            in_specs=[pl.BlockSpec((1, 1, MOE_CHUNK), lambda l, c, *_: (c[l], 0, 0))]
            + [pl.BlockSpec((MOE_TM, D_MODEL), tile(w)) for w in range(CMB_WIN)]
            + [pl.BlockSpec((1, MOE_CHUNK, D_MODEL), tok),
               pl.BlockSpec((1, 1, D_MODEL), lambda l, c, *_: (c[l] // per_b, 0, 0))],
            out_specs=pl.BlockSpec((1, MOE_CHUNK, D_MODEL), tok),
            scratch_shapes=[pltpu.VMEM((MOE_CHUNK, D_MODEL), F32)],
        ),
        out_shape=jax.ShapeDtypeStruct((BATCH, SEQ, D_MODEL), F32),
        compiler_params=_cparams(("arbitrary",)),
        name="moe_combine",
    )(*by_chunk, pos, *([y_sorted] * CMB_WIN), x, gate.reshape(BATCH, 1, D_MODEL))


def moe_layer(x, hx, route, gate, w_gate, w_up, w_down, layer):
    pos, by_tile, by_chunk = moe_plan(route)
    y_sorted = moe_experts(hx, pos, by_tile, w_gate, w_up, w_down, layer)
    return moe_combine(x, y_sorted, pos, by_chunk, gate)


def kernel(x, c, positions, w_ada, b_ada, norm_g, w_in_ab, w_out_ab, gla_w_gate2, gla_b_gate, gla_norm_g, nsa_q_gain, nsa_k_gain, nsa_cmp_pe, nsa_cmp_w1, nsa_cmp_w2, w_in_c, sgu_norm_g, sgu_w_s, sgu_b_s, w_out_c, w_router, router_bias, w_gate, w_up, w_down):
    mod = ada_modulation(c, w_ada, b_ada)
    qk, gv, gr, nq, nkv, misc = inproj0(x, norm_g[0, 0], mod[0, :, 0], mod[0, :, 1], _arrange_w_in(w_in_ab[0]))
    o_a = gla_mixer(qk, gv, gr, misc, gla_w_gate2[0], gla_b_gate[0], gla_norm_g[0])
    o_b = nsa_mixer(nq, nkv, misc, positions, nsa_q_gain[0], nsa_k_gain[0], nsa_cmp_pe[0], nsa_cmp_w1[0], nsa_cmp_w2[0])
    wg, wu, wd = w_gate.astype(BF16), w_up.astype(BF16), w_down.astype(BF16)
    route_args = lambda l: _route_args(norm_g[l, 1], mod[l, :, 3], mod[l, :, 4], w_router, router_bias)
    x1, h, route = outproj0(o_a, o_b, w_out_ab[0], x, mod[0, :, 2], route_args(0))
    x2 = moe_layer(x1, h, route, mod[0, :, 5], wg, wu, wd, 0)
    x3, h, route = gmlp_layer(x2, norm_g[1, 0], mod[1, :, 0], mod[1, :, 1], w_in_c[0], sgu_norm_g[0], sgu_w_s[0],
                              sgu_b_s[0], w_out_c[0], mod[1, :, 2], route_args(1))
    return moe_layer(x3, h, route, mod[1, :, 5], wg, wu, wd, 1)
```

```python
import functools

import numpy as np
import jax
import jax.numpy as jnp
from jax import lax
from jax.experimental import pallas as pl
from jax.experimental.pallas import tpu as pltpu

D_MODEL = 1024
BATCH = 2
SEQ = 8192
DEPTH = 2
N_TOK = BATCH * SEQ

GLA_HEADS = 4
GLA_DK = 64
GLA_DV = 128
GLA_GATE_RANK = 16
GLA_TAU = 16.0
GLA_CHUNK = 64
NSA_HEADS = 8
NSA_KV_GROUPS = 2
NSA_HPG = NSA_HEADS // NSA_KV_GROUPS
NSA_DH = 64
CMP_LEN = 32
CMP_STRIDE = 16
CMP_HIDDEN = 256
SEL_BLOCK = 64
SEL_TOPK = 16
WINDOW = 512
ROPE_THETA = 500000.0
ROT_DIM = NSA_DH // 4
ROT_HALF = ROT_DIM // 2
SGU_CHUNK = 128
SGU_GROUPS = 8
SGU_WIDTH = 2048
SGU_GROUP_DIM = SGU_WIDTH // SGU_GROUPS
N_EXPERTS = 16
N_EXPERT_GROUPS = 4
EXPERTS_PER_GROUP = N_EXPERTS // N_EXPERT_GROUPS
MOE_TOPK = 2
EXPERT_HIDDEN = 512

GLA_QK_W = GLA_HEADS * GLA_DK
GLA_V_W = GLA_HEADS * GLA_DV
NSA_Q_W = NSA_HEADS * NSA_DH
NSA_KV_W = NSA_KV_GROUPS * NSA_DH
N_CMP = (SEQ - CMP_LEN) // CMP_STRIDE + 1
N_CMP_PAD = SEQ // CMP_STRIDE
N_SEL = SEQ // SEL_BLOCK

NORM_EPS = 1e-6
NEG_INF = -1e30
FORCE_BONUS = 1e4

LANES = 128
MIB = 1024 * 1024

F32 = jnp.float32
BF16 = jnp.bfloat16
HI = lax.Precision.HIGHEST
NT_DIMS = (((1,), (1,)), ((), ()))
TN_DIMS = (((0,), (0,)), ((), ()))


def _cparams(sem, vmem_mib=48):
    return pltpu.CompilerParams(dimension_semantics=sem, vmem_limit_bytes=vmem_mib * MIB)


def _rms_mod(x, g, shift, scale):
    y = x * lax.rsqrt(jnp.mean(x * x, axis=-1, keepdims=True) + NORM_EPS) * g
    return y * (1 + scale) + shift


def _silu(x):
    return x * jax.nn.sigmoid(x)


def _log_sigmoid(z):
    return jnp.minimum(z, 0.0) - jnp.log1p(jnp.exp(-jnp.abs(z)))


ADA_TN = 1536
ADA_ROWS = 8


def _ada_kernel(c_ref, w_ref, b_ref, o_ref):
    cond = _silu(c_ref[...])
    o_ref[0] = jnp.dot(cond, w_ref[0], precision=HI, preferred_element_type=F32) + b_ref[0]


def ada_modulation(c, w_ada, b_ada):
    c8 = jnp.zeros((ADA_ROWS, D_MODEL), F32).at[:BATCH].set(c)
    width = 6 * D_MODEL
    out = pl.pallas_call(
        _ada_kernel,
        grid=(DEPTH, width // ADA_TN),
        in_specs=[
            pl.BlockSpec((ADA_ROWS, D_MODEL), lambda l, j: (0, 0)),
            pl.BlockSpec((1, D_MODEL, ADA_TN), lambda l, j: (l, 0, j)),
            pl.BlockSpec((1, 1, ADA_TN), lambda l, j: (l, 0, j)),
        ],
        out_specs=pl.BlockSpec((1, ADA_ROWS, ADA_TN), lambda l, j: (l, 0, j)),
        out_shape=jax.ShapeDtypeStruct((DEPTH, ADA_ROWS, width), F32),
        compiler_params=_cparams(("arbitrary", "arbitrary")),
        name="ada_modulation",
    )(c8, w_ada, b_ada.reshape(DEPTH, 1, width))
    return out[:, :BATCH].reshape(DEPTH, BATCH, 6, D_MODEL)


INPROJ_TM = 512
INPROJ_WIDTHS = (2 * GLA_QK_W, GLA_V_W, GLA_V_W, NSA_Q_W, 6 * NSA_KV_W, LANES)


def _arrange_w_in(w_in):
    o = np.cumsum((0, GLA_QK_W, GLA_QK_W, GLA_V_W, GLA_GATE_RANK, GLA_V_W, NSA_Q_W, 6 * NSA_KV_W, NSA_HEADS * 3))
    gq_gk = w_in[:, o[0]:o[2]]
    gv = w_in[:, o[2]:o[3]]
    glr = w_in[:, o[3]:o[4]]
    gr = w_in[:, o[4]:o[5]]
    nq = w_in[:, o[5]:o[6]]
    nkv = w_in[:, o[6]:o[7]]
    ng = w_in[:, o[7]:o[8]]
    pad = jnp.zeros((D_MODEL, LANES - GLA_GATE_RANK - NSA_HEADS * 3), w_in.dtype)
    return jnp.concatenate([gq_gk, gv, gr, nq, nkv, glr, ng, pad], axis=1).astype(BF16)


def _inproj0_kernel(x_ref, g_ref, sh_ref, sc_ref, w_ref, *o_refs):
    h = _rms_mod(x_ref[0], g_ref[...], sh_ref[0], sc_ref[0]).astype(BF16)
    off = 0
    for o_ref, wd in zip(o_refs, INPROJ_WIDTHS):
        o_ref[0] = jnp.dot(h, w_ref[:, off:off + wd], preferred_element_type=F32)
        off += wd


def inproj0(x, g, shift, scale, w_arranged):
    tm = INPROJ_TM
    wtot = sum(INPROJ_WIDTHS)
    row = lambda b, i: (b, i, 0)
    vec = lambda b, i: (b, 0, 0)
    return pl.pallas_call(
        _inproj0_kernel,
        grid=(BATCH, SEQ // tm),
        in_specs=[
            pl.BlockSpec((1, tm, D_MODEL), row),
            pl.BlockSpec((1, D_MODEL), lambda b, i: (0, 0)),
            pl.BlockSpec((1, 1, D_MODEL), vec),
            pl.BlockSpec((1, 1, D_MODEL), vec),
            pl.BlockSpec((D_MODEL, wtot), lambda b, i: (0, 0)),
        ],
        out_specs=[pl.BlockSpec((1, tm, wd), row) for wd in INPROJ_WIDTHS],
        out_shape=[jax.ShapeDtypeStruct((BATCH, SEQ, wd), F32) for wd in INPROJ_WIDTHS],
        compiler_params=_cparams(("arbitrary", "arbitrary")),
        name="inproj0",
    )(x, g.reshape(1, D_MODEL), shift.reshape(BATCH, 1, D_MODEL), scale.reshape(BATCH, 1, D_MODEL), w_arranged)


GLA_TG = 512


def _gla_chunk_sums():
    i = np.arange(GLA_TG)[:, None]
    j = np.arange(GLA_TG)[None, :]
    same = (i // GLA_CHUNK) == (j // GLA_CHUNK)
    m3 = np.concatenate([same & (j <= i), same & (j % GLA_CHUNK <= GLA_CHUNK // 2), same], axis=0).astype(np.float32)
    return jnp.asarray(np.concatenate([m3, m3], axis=1), BF16)


def _gla_kernel(qk_ref, v_ref, r_ref, misc_ref, w2_ref, bg_ref, og_ref, sums_ref, o_ref, st_ref):
    C, tg = GLA_CHUNK, GLA_TG

    @pl.when(pl.program_id(0) == 0)
    def _():
        st_ref[...] = jnp.zeros_like(st_ref)

    lane = lax.broadcasted_iota(jnp.int32, (1, GLA_QK_W), 1)
    heads = [(lane >= h * GLA_DK) & (lane < (h + 1) * GLA_DK) for h in range(GLA_HEADS)]
    stack = lambda per_head, rows: jnp.concatenate([t[rows] for t in per_head], axis=0)
    stacked_row = lax.broadcasted_iota(jnp.int32, (GLA_HEADS * C, C), 0)
    causal = (stacked_row & (C - 1)) >= lax.broadcasted_iota(jnp.int32, (GLA_HEADS * C, C), 1)
    og = og_ref[...]

    def prepare(b):
        z = jnp.dot(misc_ref[b], w2_ref[...], precision=HI, preferred_element_type=F32) + bg_ref[...]
        la = _log_sigmoid(z) / GLA_TAU
        la_hi = la.astype(BF16)
        la_lo = (la - la_hi.astype(F32)).astype(BF16)
        sums = jnp.dot(sums_ref[...], jnp.concatenate([la_hi, la_lo], axis=0), preferred_element_type=F32)
        bc, b_mid, b_last = sums[0:tg], sums[tg:2 * tg], sums[2 * tg:3 * tg]
        q = qk_ref[b, :, 0:GLA_QK_W] * (GLA_DK ** -0.5)
        k = qk_ref[b, :, GLA_QK_W:2 * GLA_QK_W]
        qd = q * jnp.exp(bc - b_mid)
        kl = k * jnp.exp(b_last - bc)
        qb = q * jnp.exp(bc)
        per_head = lambda t: [jnp.where(m, t, 0.0).astype(BF16) for m in heads]
        return dict(kd=(k * jnp.exp(b_mid - bc)).astype(BF16), dec=jnp.exp(b_last), qd_h=per_head(qd),
                    qb_h=per_head(qb), kl_h=per_head(kl))

    batches = range(BATCH)
    pre = [prepare(b) for b in batches]
    st = [st_ref[b] for b in batches]
    for c in range(tg // C):
        rows = slice(c * C, (c + 1) * C)
        for b in batches:
            p = pre[b]
            v = v_ref[b, rows, :].astype(BF16)
            s = lax.dot_general(stack(p["qd_h"], rows), p["kd"][rows], NT_DIMS, preferred_element_type=F32)
            s = jnp.where(causal, s, 0.0).astype(BF16)
            o_intra = jnp.dot(s, v, preferred_element_type=F32)
            o_inter = lax.dot_general(stack(p["qb_h"], rows), st[b].astype(BF16), NT_DIMS, preferred_element_type=F32)
            v_stack = jnp.concatenate([v[:, h * GLA_DV:(h + 1) * GLA_DV] for h in range(GLA_HEADS)], axis=0)
            st[b] = st[b] * p["dec"][c * C:c * C + 1] + lax.dot_general(v_stack, stack(p["kl_h"], rows), TN_DIMS,
                                                                       preferred_element_type=F32)
            for h in range(GLA_HEADS):
                hrows = slice(h * C, (h + 1) * C)
                vcols = slice(h * GLA_DV, (h + 1) * GLA_DV)
                o = o_intra[hrows, vcols] + o_inter[hrows]
                on = o * lax.rsqrt(jnp.mean(o * o, axis=-1, keepdims=True) + NORM_EPS) * og
                o_ref[b, rows, vcols] = on * _silu(r_ref[b, rows, vcols])
    for b in batches:
        st_ref[b] = st[b]


def gla_mixer(qk, v, r, misc, w_gate2, b_gate, out_g):
    tg = GLA_TG
    w2 = jnp.zeros((LANES, GLA_QK_W), F32).at[:GLA_GATE_RANK].set(w_gate2)
    row = lambda i: (0, i, 0)
    const = lambda i: (0, 0)
    return pl.pallas_call(
        _gla_kernel,
        grid=(SEQ // tg,),
        in_specs=[
            pl.BlockSpec((BATCH, tg, 2 * GLA_QK_W), row),
            pl.BlockSpec((BATCH, tg, GLA_V_W), row),
            pl.BlockSpec((BATCH, tg, GLA_V_W), row),
            pl.BlockSpec((BATCH, tg, LANES), row),
            pl.BlockSpec((LANES, GLA_QK_W), const),
            pl.BlockSpec((1, GLA_QK_W), const),
            pl.BlockSpec((1, GLA_DV), const),
            pl.BlockSpec((3 * tg, 2 * tg), const),
        ],
        out_specs=pl.BlockSpec((BATCH, tg, GLA_V_W), row),
        out_shape=jax.ShapeDtypeStruct((BATCH, SEQ, GLA_V_W), F32),
        scratch_shapes=[pltpu.VMEM((BATCH, GLA_DV, GLA_QK_W), F32)],
        compiler_params=_cparams(("arbitrary",)),
        name="gla_mixer",
    )(qk, v, r, misc, w2, b_gate.reshape(1, GLA_QK_W), out_g.reshape(1, GLA_DV), _gla_chunk_sums())


POS_SIDE = 128


def _rope_table_kernel(freq_ref, pos_ref, cos_ref, sin_ref):
    pos = pos_ref[...].astype(F32)
    for f in range(ROT_HALF):
        ang = pos * freq_ref[f]
        cos_ref[f] = jnp.cos(ang)
        sin_ref[f] = jnp.sin(ang)


def rope_tables(positions):
    inv_freq = jnp.float32(ROPE_THETA) ** (-jnp.arange(ROT_HALF, dtype=F32) / ROT_HALF)
    shp = jax.ShapeDtypeStruct((ROT_HALF, POS_SIDE, POS_SIDE), F32)
    cos, sin = pl.pallas_call(
        _rope_table_kernel,
        in_specs=[pl.BlockSpec(memory_space=pltpu.SMEM), pl.BlockSpec(memory_space=pltpu.VMEM)],
        out_specs=[pl.BlockSpec(memory_space=pltpu.VMEM)] * 2,
        out_shape=[shp, shp],
        name="rope_tables",
    )(inv_freq, positions.reshape(POS_SIDE, POS_SIDE))
    return jnp.concatenate([cos, sin], axis=0).reshape(ROT_DIM, N_TOK).T.reshape(BATCH, SEQ, ROT_DIM)


def _rope_placement():
    place = np.zeros((ROT_DIM, 3 * LANES), np.float32)
    const = np.zeros((1, 3 * LANES), np.float32)
    for lane in range(LANES):
        i = lane % NSA_DH
        if i < ROT_HALF:
            place[i, lane] = 1.0
            place[ROT_HALF + i, LANES + lane] = -1.0
        elif i < ROT_DIM:
            place[i - ROT_HALF, lane] = 1.0
            place[i, 2 * LANES + lane] = 1.0
        else:
            const[0, lane] = 1.0
    return jnp.asarray(place), jnp.asarray(const)


def _lane_tables(cs, place_ref, const_ref):
    tab = jnp.dot(cs, place_ref[...], precision=HI, preferred_element_type=F32) + const_ref[...]
    return tab[:, 0:LANES], tab[:, LANES:2 * LANES], tab[:, 2 * LANES:3 * LANES]


def _block_diag_ones2(width):
    h = np.arange(width) // NSA_DH
    bd = (h[:, None] == h[None, :]).astype(np.float32)
    return jnp.asarray(np.concatenate([bd, bd], axis=0), BF16)


def _head_norm_rope(x, gain, bd2, c, sm, sp):
    width = x.shape[-1]
    reps = width // LANES
    sq = x * x
    sq_hi = sq.astype(BF16)
    sq_lo = (sq - sq_hi.astype(F32)).astype(BF16)
    ss = jnp.dot(jnp.concatenate([sq_hi, sq_lo], axis=1), bd2, preferred_element_type=F32)
    y = x * lax.rsqrt(ss * (1.0 / NSA_DH) + NORM_EPS) * gain
    tile = lambda t: jnp.concatenate([t] * reps, axis=1) if reps > 1 else t
    return (y * tile(c) + pltpu.roll(y, width - ROT_HALF, 1) * tile(sm) + pltpu.roll(y, ROT_HALF, 1) * tile(sp))


PREP_TM = 512


def _prep_kernel(q_ref, kv_ref, cs_ref, place_ref, const_ref, gq_ref, gk_ref, bdq_ref, bdk_ref,
                 qo_ref, kso_ref, kwo_ref, vso_ref, vwo_ref):
    tm = PREP_TM
    c, sm, sp = _lane_tables(cs_ref[0], place_ref, const_ref)
    bdk = bdk_ref[...]
    q = _head_norm_rope(q_ref[0], gq_ref[...], bdq_ref[...], c, sm, sp) * (NSA_DH ** -0.5)
    qo_ref[0] = q.T.reshape(NSA_HEADS, NSA_DH, tm)
    kv_cols = lambda n: kv_ref[0, :, n * NSA_KV_W:(n + 1) * NSA_KV_W]
    ks = _head_norm_rope(kv_cols(2), gk_ref[0:1, :], bdk, c, sm, sp)
    kw = _head_norm_rope(kv_cols(4), gk_ref[1:2, :], bdk, c, sm, sp)
    lane = lax.broadcasted_iota(jnp.int32, (tm, LANES), 1)
    token = pl.program_id(1) * tm + lax.broadcasted_iota(jnp.int32, (tm, LANES), 0)
    block_col = NSA_DH + jnp.right_shift(token & (SA_TK - 1), 6)
    onehot = jnp.where(lane == block_col, 1.0, 0.0)
    for g in range(NSA_KV_GROUPS):
        to_front = lambda t: t if g == 0 else pltpu.roll(t, NSA_DH, 1)
        kso_ref[0, g] = jnp.where(lane < NSA_DH, to_front(ks), onehot).astype(BF16)
        kwo_ref[0, g] = jnp.where(lane < NSA_DH, to_front(kw), 0.0).astype(BF16)
    tail = jnp.where(lax.broadcasted_iota(jnp.int32, (VT_ROWS - NSA_DH, tm), 0) == 0, 1.0, 0.0)
    for n, out_ref in ((3, vso_ref), (5, vwo_ref)):
        v_t = kv_cols(n).T
        for g in range(NSA_KV_GROUPS):
            out_ref[0, g] = jnp.concatenate([v_t[g * NSA_DH:(g + 1) * NSA_DH], tail], axis=0).astype(BF16)


def nsa_prep(nq, nkv, cs, q_gain, k_gain):
    tm = PREP_TM
    G = NSA_KV_GROUPS
    row = lambda b, i: (b, i, 0)
    const = lambda b, i: (0, 0)
    gq = jnp.tile(q_gain, NSA_HEADS).reshape(1, NSA_Q_W)
    gk = jnp.stack([jnp.tile(k_gain[1], NSA_KV_GROUPS), jnp.tile(k_gain[2], NSA_KV_GROUPS)])
    kslab = pl.BlockSpec((1, G, tm, LANES), lambda b, i: (b, 0, i, 0))
    vslab = pl.BlockSpec((1, G, VT_ROWS, tm), lambda b, i: (b, 0, 0, i))
    return pl.pallas_call(
        _prep_kernel,
        grid=(BATCH, SEQ // tm),
        in_specs=[
            pl.BlockSpec((1, tm, NSA_Q_W), row),
            pl.BlockSpec((1, tm, 6 * NSA_KV_W), row),
            pl.BlockSpec((1, tm, ROT_DIM), row),
            pl.BlockSpec((ROT_DIM, 3 * LANES), const),
            pl.BlockSpec((1, 3 * LANES), const),
            pl.BlockSpec((1, NSA_Q_W), const),
            pl.BlockSpec((2, NSA_KV_W), const),
            pl.BlockSpec((2 * NSA_Q_W, NSA_Q_W), const),
            pl.BlockSpec((2 * NSA_KV_W, NSA_KV_W), const),
        ],
        out_specs=[pl.BlockSpec((1, NSA_HEADS, NSA_DH, tm), lambda b, i: (b, 0, 0, i)), kslab, kslab, vslab, vslab],
        out_shape=[jax.ShapeDtypeStruct((BATCH, NSA_HEADS, NSA_DH, SEQ), F32),
                   jax.ShapeDtypeStruct((BATCH, G, SEQ, LANES), BF16), jax.ShapeDtypeStruct((BATCH, G, SEQ, LANES), BF16),
                   jax.ShapeDtypeStruct((BATCH, G, VT_ROWS, SEQ), BF16), jax.ShapeDtypeStruct((BATCH, G, VT_ROWS, SEQ), BF16)],
        compiler_params=_cparams(("arbitrary", "arbitrary")),
        name="nsa_prep",
    )(nq, nkv, cs, *_rope_placement(), gq, gk, _block_diag_ones2(NSA_Q_W), _block_diag_ones2(NSA_KV_W))


SEG_W = CMP_STRIDE * NSA_DH


def _cmp_kernel(xk_ref, xv_ref, pe_ref, w1_ref, w2_ref, gain_ref, cs_ref, place_ref, const_ref, bd_ref, ko_ref, vo_ref):
    def compress(x_ref, kv):
        out = jnp.zeros((N_CMP_PAD, LANES), F32)
        for g in range(NSA_KV_GROUPS):
            x = x_ref[0, g]
            ha = jnp.dot(x + pe_ref[kv, 0], w1_ref[kv, 0:SEG_W, :], precision=HI, preferred_element_type=F32)
            hb = jnp.dot(x + pe_ref[kv, 1], w1_ref[kv, SEG_W:2 * SEG_W, :], precision=HI, preferred_element_type=F32)
            hid = ha + pltpu.roll(hb, N_CMP_PAD - 1, 0)
            out += jnp.dot(jax.nn.gelu(hid), w2_ref[kv, g], precision=HI, preferred_element_type=F32)
        return out

    c, sm, sp = _lane_tables(cs_ref[0], place_ref, const_ref)
    ko_ref[0] = _head_norm_rope(compress(xk_ref, 0), gain_ref[...], bd_ref[...], c, sm, sp)
    vo_ref[0] = compress(xv_ref, 1)


def nsa_compress(xk, xv, cmp_pe, cmp_w1, cmp_w2, k_gain0, cs_last):
    pe = cmp_pe.reshape(2, 2, 1, SEG_W)
    w2 = jnp.zeros((2, NSA_KV_GROUPS, CMP_HIDDEN, LANES), F32)
    for g in range(NSA_KV_GROUPS):
        w2 = w2.at[:, g, :, g * NSA_DH:(g + 1) * NSA_DH].set(cmp_w2)
    seg = pl.BlockSpec((1, NSA_KV_GROUPS, N_CMP_PAD, SEG_W), lambda b: (b, 0, 0, 0))
    tab = pl.BlockSpec((1, N_CMP_PAD, LANES), lambda b: (b, 0, 0))
    full = lambda shape: pl.BlockSpec(shape, lambda b: (0,) * len(shape))
    return pl.pallas_call(
        _cmp_kernel,
        grid=(BATCH,),
        in_specs=[seg, seg, full((2, 2, 1, SEG_W)), full((2, 2 * SEG_W, CMP_HIDDEN)),
                  full((2, NSA_KV_GROUPS, CMP_HIDDEN, LANES)), full((1, LANES)),
                  pl.BlockSpec((1, N_CMP_PAD, ROT_DIM), lambda b: (b, 0, 0)), full((ROT_DIM, 3 * LANES)),
                  full((1, 3 * LANES)), full((2 * LANES, LANES))],
        out_specs=[tab, tab],
        out_shape=[jax.ShapeDtypeStruct((BATCH, N_CMP_PAD, LANES), F32)] * 2,
        compiler_params=_cparams(("arbitrary",)),
        name="nsa_compress",
    )(xk, xv, pe, cmp_w1, w2, jnp.tile(k_gain0, NSA_KV_GROUPS).reshape(1, LANES), cs_last, *_rope_placement(),
      _block_diag_ones2(LANES))


CA_TQ = 512
SUBLANES = 8


CA_COLS = NSA_HPG * CA_TQ
CMP_PER_SEL = SEL_BLOCK // CMP_STRIDE
TOPK_BANDS = 4


def split3_keys(k):
    hi = k.astype(BF16)
    lo = (k - hi.astype(F32)).astype(BF16)
    return jnp.concatenate([hi, lo, hi], axis=-1)


def _top_k_rows(score, k):
    rows, cols = score.shape
    row = lax.broadcasted_iota(jnp.int32, (rows, cols), 0).astype(F32)
    taken = jnp.zeros((rows, cols), F32)
    left = score
    for _ in range(k):
        top = jnp.max(left, axis=0, keepdims=True)
        first = jnp.min(jnp.where(left == top, row, float(rows)), axis=0, keepdims=True)
        hit = row == first
        taken = jnp.where(hit, 1.0, taken)
        left = jnp.where(hit, -jnp.inf, left)
    return taken


def _cattn_kernel(q_ref, kc_ref, vct_ref, gl_ref, o_ref, sel_ref, q3_ref, ps_ref):
    tq = CA_TQ
    q0 = pl.program_id(1) * tq
    lanes4 = lambda t: jnp.concatenate([t] * NSA_HPG, axis=1)
    cend = lax.broadcasted_iota(jnp.int32, (N_CMP_PAD, tq), 0) * CMP_STRIDE + (CMP_LEN - 1)
    tc = q0 + lax.broadcasted_iota(jnp.int32, (N_CMP_PAD, tq), 1)
    cmask = lanes4(cend <= tc)
    jj = lax.broadcasted_iota(jnp.int32, (N_SEL, tq), 0)
    tt = q0 + lax.broadcasted_iota(jnp.int32, (N_SEL, tq), 1)
    cur = jnp.right_shift(tt, 6)
    forced = (jj == 0) | (jj == cur) | (jj == cur - 1)
    valid = jj * SEL_BLOCK <= tt

    for g in range(NSA_KV_GROUPS):
        heads = range(g * NSA_HPG, (g + 1) * NSA_HPG)
        for n, h in enumerate(heads):
            q = q_ref[0, h]
            hi = q.astype(BF16)
            lo = (q - hi.astype(F32)).astype(BF16)
            for t, part in enumerate((hi, hi, lo)):
                q3_ref[g, t * NSA_DH:(t + 1) * NSA_DH, n * tq:(n + 1) * tq] = part
        s = jnp.dot(kc_ref[0, g], q3_ref[g], preferred_element_type=F32)
        s = jnp.where(cmask, s, NEG_INF)
        m = jnp.max(s, axis=0, keepdims=True)
        e = jnp.where(cmask, jnp.exp(s - m), 0.0)
        l = jnp.sum(e, axis=0, keepdims=True)
        p = e / jnp.where(l > 0.0, l, 1.0)
        gate = jnp.concatenate([jax.nn.sigmoid(gl_ref[0, h, 0:1, :]) for h in heads], axis=1)
        o = jnp.dot(vct_ref[0, g], p.astype(BF16), preferred_element_type=F32) * gate
        for n, h in enumerate(heads):
            o_ref[0, h] = o[:, n * tq:(n + 1) * tq]
        psum = functools.reduce(jnp.add, [p[:, n * tq:(n + 1) * tq] for n in range(NSA_HPG)])
        for n in range(tq // LANES):
            ps_ref[g, n] = psum[:, n * LANES:(n + 1) * LANES]

        every4th = lambda r: jnp.concatenate(
            [ps_ref[g, n, pl.ds(r, N_SEL, stride=CMP_PER_SEL), :] for n in range(tq // LANES)], axis=1)
        starts_in = [every4th(r) for r in range(CMP_PER_SEL)]
        from_prev = jnp.where(jj >= 1, pltpu.roll(starts_in[CMP_PER_SEL - 1], 1, 0), 0.0)
        imp = functools.reduce(jnp.add, starts_in) + from_prev
        score = jnp.where(valid, imp + jnp.where(forced, FORCE_BONUS, 0.0), NEG_INF)
        step = pl.program_id(1)
        steps_per_band = (SEQ // tq) // TOPK_BANDS
        for band in range(TOPK_BANDS):
            n_rows = (band + 1) * (N_SEL // TOPK_BANDS)

            @pl.when((step >= band * steps_per_band) & (step < (band + 1) * steps_per_band))
            def _():
                taken = _top_k_rows(score[0:n_rows], SEL_TOPK)
                sel_ref[0, g, 0:n_rows, :] = jnp.where(valid[0:n_rows], taken, 0.0)
                if n_rows < N_SEL:
                    sel_ref[0, g, n_rows:N_SEL, :] = jnp.zeros((N_SEL - n_rows, tq), F32)


def nsa_cmp_attn(q_t, kcmp, vcmp_t, gl_t):
    tq = CA_TQ
    G = NSA_KV_GROUPS
    return pl.pallas_call(
        _cattn_kernel,
        grid=(BATCH, SEQ // tq),
        in_specs=[
            pl.BlockSpec((1, NSA_HEADS, NSA_DH, tq), lambda b, i: (b, 0, 0, i)),
            pl.BlockSpec((1, G, N_CMP_PAD, 3 * NSA_DH), lambda b, i: (b, 0, 0, 0)),
            pl.BlockSpec((1, G, NSA_DH, N_CMP_PAD), lambda b, i: (b, 0, 0, 0)),
            pl.BlockSpec((1, NSA_HEADS, 3, tq), lambda b, i: (b, 0, 0, i)),
        ],
        out_specs=[pl.BlockSpec((1, NSA_HEADS, NSA_DH, tq), lambda b, i: (b, 0, 0, i)),
                   pl.BlockSpec((1, G, N_SEL, tq), lambda b, i: (b, 0, 0, i))],
        out_shape=[jax.ShapeDtypeStruct((BATCH, NSA_HEADS, NSA_DH, SEQ), F32),
                   jax.ShapeDtypeStruct((BATCH, G, N_SEL, SEQ), F32)],
        scratch_shapes=[pltpu.VMEM((G, 3 * NSA_DH, CA_COLS), BF16), pltpu.VMEM((G, tq // LANES, N_CMP_PAD, LANES), F32)],
        compiler_params=_cparams(("arbitrary", "arbitrary")),
        name="nsa_cmp_attn",
    )(q_t, split3_keys(kcmp), vcmp_t.astype(BF16), gl_t)


SA_TQ = 512
SA_TK = 1024
SA_PARTS = 2
SA_PART = SA_TK // SA_PARTS
M_INIT = -1e20


SA_COLS = NSA_HPG * SA_TQ
SA_BLOCKS = SA_TK // SEL_BLOCK


VT_ROWS = NSA_DH + 16


def _sattn_kernel(q_ref, k_ref, vt_ref, sel_ref, gl_ref, prev_ref, o_ref, qa_ref, acc_ref, s_ref, m_ref):
    tq, tk = SA_TQ, SA_TK
    i = pl.program_id(1)
    groups = range(NSA_KV_GROUPS)
    slots = range(2)
    for g in groups:
        for h in range(NSA_HPG):
            q = q_ref[0, g * NSA_HPG + h].astype(BF16)
            for slot in slots:
                qa_ref[slot, g, 0:NSA_DH, h * tq:(h + 1) * tq] = q
        for slot in slots:
            qa_ref[slot, g, NSA_DH:LANES, :] = jnp.zeros((LANES - NSA_DH, SA_COLS), BF16)
    acc_ref[...] = jnp.zeros_like(acc_ref)
    lanes4 = lambda t: jnp.concatenate([t] * NSA_HPG, axis=1)
    part_keys = lambda kt, part: pl.ds(pl.multiple_of(kt * tk + part * SA_PART, SA_PART), SA_PART)

    def scores(kt, slot):
        for g in groups:
            selrows = sel_ref[0, g, pl.ds(pl.multiple_of(kt * SA_BLOCKS, SA_BLOCKS), SA_BLOCKS), :]
            qa_ref[slot, g, NSA_DH:NSA_DH + SA_BLOCKS, :] = lanes4(jnp.where(selrows > 0.5, 0.0, NEG_INF)).astype(BF16)
            for part in range(SA_PARTS):
                s = jnp.dot(k_ref[0, g, part_keys(kt, part), :], qa_ref[slot, g], preferred_element_type=F32)
                s_ref[slot, g, part] = s.astype(BF16)

    def absorb(kt, slot, ms):
        out = []
        for g in groups:
            ss = [s_ref[slot, g, part] for part in range(SA_PARTS)]
            m_tile = functools.reduce(jnp.maximum, [jnp.max(s, axis=0, keepdims=True) for s in ss])
            m_new = jnp.maximum(ms[g], m_tile.astype(F32))
            acc = jnp.exp(ms[g] - m_new) * acc_ref[g]
            for part in range(SA_PARTS):
                p = jnp.exp(ss[part] - m_new.astype(BF16))
                acc += jnp.dot(vt_ref[0, g, :, part_keys(kt, part)], p, preferred_element_type=F32)
            acc_ref[g] = acc
            out.append(m_new)
        return tuple(out)

    def two_tiles(j, ms):
        kt = 2 * j
        scores(kt + 1, 1)
        ms = absorb(kt, 0, ms)
        scores(kt + 2, 0)
        return absorb(kt + 1, 1, ms)

    n_full = (i * tq) // tk
    scores(0, 0)
    m0 = tuple(jnp.full((1, SA_COLS), M_INIT, F32) for _ in groups)
    ms = lax.fori_loop(0, n_full // 2, two_tiles, m0)
    for g in groups:
        m_ref[g] = ms[g]

    def last_tile(slot):
        start = i * tq - n_full * tk
        part, row0 = start // SA_PART, pl.multiple_of(start % SA_PART, tq)
        tri = lax.broadcasted_iota(jnp.int32, (tq, tq), 0) <= lax.broadcasted_iota(jnp.int32, (tq, tq), 1)
        bias = lanes4(jnp.where(tri, 0.0, NEG_INF)).astype(BF16)
        for g in groups:
            s_ref[slot, g, part, pl.ds(row0, tq), :] += bias
        for g, m in enumerate(absorb(n_full, slot, tuple(m_ref[g] for g in groups))):
            m_ref[g] = m

    @pl.when(n_full % 2 == 0)
    def _():
        last_tile(0)

    @pl.when(n_full % 2 == 1)
    def _():
        scores(n_full, 1)
        for g, m in enumerate(absorb(n_full - 1, 0, tuple(m_ref[g] for g in groups))):
            m_ref[g] = m
        last_tile(1)

    for g in groups:
        heads = range(g * NSA_HPG, (g + 1) * NSA_HPG)
        gate = jnp.concatenate([jax.nn.sigmoid(gl_ref[0, h, 1:2, :]) for h in heads], axis=1)
        out = acc_ref[g, 0:NSA_DH, :] / acc_ref[g, NSA_DH:NSA_DH + 1, :] * gate
        for n, h in enumerate(heads):
            o_ref[0, h] = prev_ref[0, h] + out[:, n * tq:(n + 1) * tq]


def nsa_sel_attn(q_t, k_slab, vsel_t, sel_t, gl_t, prev):
    tq = SA_TQ
    G = NSA_KV_GROUPS
    ospec = pl.BlockSpec((1, NSA_HEADS, NSA_DH, tq), lambda b, i: (b, 0, 0, i))
    return pl.pallas_call(
        _sattn_kernel,
        grid=(BATCH, SEQ // tq),
        in_specs=[
            ospec,
            pl.BlockSpec((1, G, SEQ, LANES), lambda b, i: (b, 0, 0, 0)),
            pl.BlockSpec((1, G, VT_ROWS, SEQ), lambda b, i: (b, 0, 0, 0)),
            pl.BlockSpec((1, G, N_SEL, tq), lambda b, i: (b, 0, 0, i)),
            pl.BlockSpec((1, NSA_HEADS, 3, tq), lambda b, i: (b, 0, 0, i)),
            ospec,
        ],
        out_specs=ospec,
        out_shape=jax.ShapeDtypeStruct((BATCH, NSA_HEADS, NSA_DH, SEQ), F32),
        scratch_shapes=[pltpu.VMEM((2, G, LANES, SA_COLS), BF16), pltpu.VMEM((G, VT_ROWS, SA_COLS), F32),
                        pltpu.VMEM((2, G, SA_PARTS, SA_PART, SA_COLS), BF16), pltpu.VMEM((G, 1, SA_COLS), F32)],
        input_output_aliases={5: 0},
        compiler_params=_cparams(("arbitrary", "arbitrary")),
        name="nsa_sel_attn",
    )(q_t, k_slab, vsel_t, sel_t, gl_t, prev)


WA_TQ = 256
WA_TILES = WINDOW // WA_TQ + 1


def _window_bias():
    kl = np.arange(WA_TILES * WA_TQ)[:, None]
    ql = np.arange(WA_TQ)[None, :]
    diff = ql - kl + WINDOW
    return jnp.asarray(np.where((diff >= 0) & (diff < WINDOW), 0.0, NEG_INF).astype(np.float32))


def _wattn_kernel(q_ref, k0_ref, k1_ref, k2_ref, v0_ref, v1_ref, v2_ref, bias_ref, gl_ref, prev_ref, o_ref):
    tq = WA_TQ
    i = pl.program_id(1)
    k_refs = (k0_ref, k1_ref, k2_ref)
    v_refs = (v0_ref, v1_ref, v2_ref)
    lanes4 = lambda t: jnp.concatenate([t] * NSA_HPG, axis=1)
    biases = []
    for d in range(WA_TILES):
        in_seq = i - (WA_TILES - 1) + d >= 0
        biases.append(lanes4(jnp.where(in_seq, bias_ref[d * tq:(d + 1) * tq, :], NEG_INF)))
    for g in range(NSA_KV_GROUPS):
        heads = range(g * NSA_HPG, (g + 1) * NSA_HPG)
        q = jnp.concatenate([q_ref[0, h] for h in heads], axis=1).astype(BF16)
        q = jnp.concatenate([q, jnp.zeros_like(q)], axis=0)
        ss = [(jnp.dot(k_refs[d][0, g], q, preferred_element_type=F32) + biases[d]).astype(BF16) for d in range(WA_TILES)]
        m = functools.reduce(jnp.maximum, [jnp.max(s, axis=0, keepdims=True) for s in ss])
        acc = functools.reduce(jnp.add, [jnp.dot(v_refs[d][0, g], jnp.exp(ss[d] - m), preferred_element_type=F32)
                                         for d in range(WA_TILES)])
        gate = jnp.concatenate([jax.nn.sigmoid(gl_ref[0, h, 2:3, :]) for h in heads], axis=1)
        out = acc[0:NSA_DH] / acc[NSA_DH:NSA_DH + 1] * gate
        for n, h in enumerate(heads):
            o_ref[0, h] = prev_ref[0, h] + out[:, n * tq:(n + 1) * tq]


def nsa_win_attn(q_t, kwin, vwin_t, gl_t, prev):
    tq = WA_TQ
    G = NSA_KV_GROUPS
    qspec = pl.BlockSpec((1, NSA_HEADS, NSA_DH, tq), lambda b, i: (b, 0, 0, i))
    tile = lambda d: (lambda i: jnp.maximum(i - (WA_TILES - 1) + d, 0))
    kspec = lambda d: pl.BlockSpec((1, G, tq, LANES), lambda b, i: (b, 0, tile(d)(i), 0))
    vspec = lambda d: pl.BlockSpec((1, G, VT_ROWS, tq), lambda b, i: (b, 0, 0, tile(d)(i)))
    return pl.pallas_call(
        _wattn_kernel,
        grid=(BATCH, SEQ // tq),
        in_specs=[qspec] + [kspec(d) for d in range(WA_TILES)] + [vspec(d) for d in range(WA_TILES)] + [
            pl.BlockSpec((WA_TILES * tq, tq), lambda b, i: (0, 0)),
            pl.BlockSpec((1, NSA_HEADS, 3, tq), lambda b, i: (b, 0, 0, i)),
            qspec,
        ],
        out_specs=qspec,
        out_shape=jax.ShapeDtypeStruct((BATCH, NSA_HEADS, NSA_DH, SEQ), F32),
        input_output_aliases={2 * WA_TILES + 3: 0},
        compiler_params=_cparams(("arbitrary", "arbitrary")),
        name="nsa_win_attn",
    )(q_t, *([kwin] * WA_TILES), *([vwin_t] * WA_TILES), _window_bias(), gl_t, prev)


def nsa_mixer(nq, nkv, misc, positions, q_gain, k_gain, cmp_pe, cmp_w1, cmp_w2):
    cs = rope_tables(positions)
    q_t, ksel, kwin, vsel_t, vwin_t = nsa_prep(nq, nkv, cs, q_gain, k_gain)
    group_major = lambda t: t.reshape(BATCH, SEQ, NSA_KV_GROUPS, NSA_DH).transpose(0, 2, 1, 3)
    col = lambda n: nkv[..., n * NSA_KV_W:(n + 1) * NSA_KV_W]
    segs = lambda t: group_major(t).reshape(BATCH, NSA_KV_GROUPS, N_CMP_PAD, SEG_W)
    last = jnp.minimum(jnp.arange(N_CMP_PAD) * CMP_STRIDE + CMP_LEN - 1, SEQ - 1)
    kcmp, vcmp = nsa_compress(segs(col(0)), segs(col(1)), cmp_pe, cmp_w1, cmp_w2, k_gain[0], cs[:, last])
    kcmp = kcmp.reshape(BATCH, N_CMP_PAD, NSA_KV_GROUPS, NSA_DH).transpose(0, 2, 1, 3)
    vcmp_t = vcmp.reshape(BATCH, N_CMP_PAD, NSA_KV_GROUPS, NSA_DH).transpose(0, 2, 3, 1)
    gl_t = misc[..., GLA_GATE_RANK:GLA_GATE_RANK + NSA_HEADS * 3].reshape(BATCH, SEQ, NSA_HEADS, 3).transpose(0, 2, 3, 1)
    o_t, sel_t = nsa_cmp_attn(q_t, kcmp, vcmp_t, gl_t)
    o_t = nsa_sel_attn(q_t, ksel, vsel_t, sel_t, gl_t, o_t)
    return nsa_win_attn(q_t, kwin, vwin_t, gl_t, o_t)


ROUTE_ROWS = 8
HX_TERMS = 3
HX_W = D_MODEL + LANES


def _top2_sum(a, b, c, d):
    hi1, lo1 = jnp.maximum(a, b), jnp.minimum(a, b)
    hi2, lo2 = jnp.maximum(c, d), jnp.minimum(c, d)
    return jnp.maximum(hi1, hi2) + jnp.maximum(jnp.minimum(hi1, hi2), jnp.maximum(lo1, lo2))


def _moe_prenorm_route(xn, g_ref, sh_ref, sc_ref, wr_ref, rb_ref, hx_ref, route_ref):
    h = _rms_mod(xn, g_ref[...], sh_ref[0], sc_ref[0])
    logits = lax.dot_general(wr_ref[...], h, NT_DIMS, precision=HI, preferred_element_type=F32)
    scores = jax.nn.sigmoid(logits)
    sel = scores + rb_ref[...]
    epg = EXPERTS_PER_GROUP
    srow = lambda e: sel[e:e + 1, :]
    grp = [_top2_sum(*[srow(epg * g + r) for r in range(epg)]) for g in range(N_EXPERT_GROUPS)]
    best, gi = grp[0], jnp.zeros_like(grp[0], dtype=jnp.int32)
    for g in range(1, N_EXPERT_GROUPS):
        better = grp[g] > best
        gi = jnp.where(better, g, gi)
        best = jnp.where(better, grp[g], best)

    def in_group(mat, r):
        out = mat[r:r + 1, :]
        for g in range(1, N_EXPERT_GROUPS):
            out = jnp.where(gi == g, mat[epg * g + r:epg * g + r + 1, :], out)
        return out

    v = [in_group(sel, r) for r in range(epg)]
    sc = [in_group(scores, r) for r in range(epg)]
    b1, i1, w1 = v[0], jnp.zeros_like(gi), sc[0]
    for r in range(1, epg):
        better = v[r] > b1
        i1 = jnp.where(better, r, i1)
        w1 = jnp.where(better, sc[r], w1)
        b1 = jnp.where(better, v[r], b1)
    b2 = jnp.full_like(b1, -3e38)
    i2, w2 = jnp.zeros_like(gi), jnp.zeros_like(w1)
    for r in range(epg):
        better = (i1 != r) & (v[r] > b2)
        i2 = jnp.where(better, r, i2)
        w2 = jnp.where(better, sc[r], w2)
        b2 = jnp.where(better, v[r], b2)
    tot = w1 + w2
    w1, w2 = w1 / tot, w2 / tot
    zero = jnp.zeros_like(w1)
    route_ref[0] = jnp.concatenate([gi.astype(F32)] + [zero] * (ROUTE_ROWS - 1), axis=0)
    w = jnp.concatenate([jnp.where(i1 == r, w1, jnp.where(i2 == r, w2, 0.0)) for r in range(epg)], axis=0)
    w_hi = w.astype(BF16).astype(F32)
    w_mid = (w - w_hi).astype(BF16).astype(F32)
    w_lo = (w - w_hi - w_mid).astype(BF16).astype(F32)
    pad = jnp.zeros((LANES - HX_TERMS * epg, w.shape[1]), F32)
    hx_ref[0, :, 0:D_MODEL] = h.astype(BF16)
    hx_ref[0, :, D_MODEL:HX_W] = jnp.concatenate([w_hi, w_mid, w_lo, pad], axis=0).T.astype(BF16)


def _route_specs(tm, row, vec, const):
    in_specs = [pl.BlockSpec((1, D_MODEL), const), pl.BlockSpec((1, 1, D_MODEL), vec), pl.BlockSpec((1, 1, D_MODEL), vec),
                pl.BlockSpec((N_EXPERTS, D_MODEL), const), pl.BlockSpec((N_EXPERTS, 1), const)]
    assert tm == MOE_CHUNK
    out_specs = [pl.BlockSpec((1, tm, HX_W), lambda b, i: (b * (SEQ // tm) + i, 0, 0)),
                 pl.BlockSpec((1, ROUTE_ROWS, tm), lambda b, i: (b, 0, i))]
    out_shape = [jax.ShapeDtypeStruct((MOE_CHUNKS, MOE_CHUNK, HX_W), BF16),
                 jax.ShapeDtypeStruct((BATCH, ROUTE_ROWS, SEQ), F32)]
    return in_specs, out_specs, out_shape


def _route_args(g, shift, scale, w_router, router_bias):
    return (g.reshape(1, D_MODEL), shift.reshape(BATCH, 1, D_MODEL), scale.reshape(BATCH, 1, D_MODEL),
            w_router.T, router_bias.reshape(N_EXPERTS, 1))


OUTPROJ_TM = 512


def _outproj0_kernel(oa_ref, ob_ref, w_ref, x_ref, gate_ref, g_ref, sh_ref, sc_ref, wr_ref, rb_ref,
                     xo_ref, h_ref, route_ref):
    y = jnp.dot(oa_ref[0].astype(BF16), w_ref[0:GLA_V_W, :], preferred_element_type=F32)
    ob_t = ob_ref[0].reshape(NSA_Q_W, OUTPROJ_TM).astype(BF16)
    y += lax.dot_general(ob_t, w_ref[GLA_V_W:GLA_V_W + NSA_Q_W, :], TN_DIMS, preferred_element_type=F32)
    xn = x_ref[0] + gate_ref[0] * y
    xo_ref[0] = xn
    _moe_prenorm_route(xn, g_ref, sh_ref, sc_ref, wr_ref, rb_ref, h_ref, route_ref)


def outproj0(o_a, o_b, w_out, x, gate, route_args):
    tm = OUTPROJ_TM
    row = lambda b, i: (b, i, 0)
    vec = lambda b, i: (b, 0, 0)
    const = lambda b, i: (0, 0)
    r_in, r_out, r_shape = _route_specs(tm, row, vec, const)
    return pl.pallas_call(
        _outproj0_kernel,
        grid=(BATCH, SEQ // tm),
        in_specs=[pl.BlockSpec((1, tm, GLA_V_W), row), pl.BlockSpec((1, NSA_HEADS, NSA_DH, tm), lambda b, i: (b, 0, 0, i)),
                  pl.BlockSpec((GLA_V_W + NSA_Q_W, D_MODEL), const), pl.BlockSpec((1, tm, D_MODEL), row),
                  pl.BlockSpec((1, 1, D_MODEL), vec)] + r_in,
        out_specs=[pl.BlockSpec((1, tm, D_MODEL), row)] + r_out,
        out_shape=[jax.ShapeDtypeStruct((BATCH, SEQ, D_MODEL), F32)] + r_shape,
        compiler_params=_cparams(("arbitrary", "arbitrary")),
        name="outproj0",
    )(o_a, o_b, w_out.astype(BF16), x, gate.reshape(BATCH, 1, D_MODEL), *route_args)


GMLP_TM = 512


def _gmlp_kernel(x_ref, g1_ref, sh1_ref, sc1_ref, win_ref, ng_ref, ws_ref, bs_ref, wout_ref, gate_ref,
                 g_ref, sh_ref, sc_ref, wr_ref, rb_ref, xo_ref, h_ref, route_ref, gated_ref, v_ref):
    x = x_ref[0]
    h = _rms_mod(x, g1_ref[...], sh1_ref[0], sc1_ref[0]).astype(BF16)
    group_cols = lambda g: slice(g * SGU_GROUP_DIM, (g + 1) * SGU_GROUP_DIM)
    ssq = jnp.zeros((GMLP_TM, LANES), F32)
    for g in range(SGU_GROUPS):
        lo = SGU_WIDTH + g * SGU_GROUP_DIM
        v = jax.nn.gelu(jnp.dot(h, win_ref[:, lo:lo + SGU_GROUP_DIM], preferred_element_type=F32))
        v_ref[:, group_cols(g)] = v
        ssq += functools.reduce(jnp.add, [v[:, n * LANES:(n + 1) * LANES] ** 2 for n in range(SGU_GROUP_DIM // LANES)])
    rs = lax.rsqrt(jnp.sum(ssq, axis=-1, keepdims=True) * (1.0 / SGU_WIDTH) + NORM_EPS)
    ri = lax.broadcasted_iota(jnp.int32, (SGU_CHUNK, SGU_CHUNK), 0)
    ci = lax.broadcasted_iota(jnp.int32, (SGU_CHUNK, SGU_CHUNK), 1)
    for g in range(SGU_GROUPS):
        cols = group_cols(g)
        u = jax.nn.gelu(jnp.dot(h, win_ref[:, cols], preferred_element_type=F32))
        vn = (v_ref[:, cols] * rs * ng_ref[:, cols]).astype(BF16)
        w = jnp.where(ri >= ci, ws_ref[g], 0.0).astype(BF16)
        for c in range(GMLP_TM // SGU_CHUNK):
            rows = slice(c * SGU_CHUNK, (c + 1) * SGU_CHUNK)
            mix = jnp.dot(w, vn[rows], preferred_element_type=F32) + bs_ref[:, g:g + 1]
            gated_ref[rows, cols] = (u[rows] * mix).astype(BF16)
    y = jnp.dot(gated_ref[...], wout_ref[...], preferred_element_type=F32)
    xn = x + gate_ref[0] * y
    xo_ref[0] = xn
    _moe_prenorm_route(xn, g_ref, sh_ref, sc_ref, wr_ref, rb_ref, h_ref, route_ref)


def gmlp_layer(x, g1, shift1, scale1, w_in, norm_g, w_s, b_s, w_out, gate, route_args):
    tm = GMLP_TM
    row = lambda b, i: (b, i, 0)
    vec = lambda b, i: (b, 0, 0)
    const = lambda b, i: (0, 0)
    r_in, r_out, r_shape = _route_specs(tm, row, vec, const)
    vspec = pl.BlockSpec((1, 1, D_MODEL), vec)
    return pl.pallas_call(
        _gmlp_kernel,
        grid=(BATCH, SEQ // tm),
        in_specs=[pl.BlockSpec((1, tm, D_MODEL), row), pl.BlockSpec((1, D_MODEL), const), vspec, vspec,
                  pl.BlockSpec((D_MODEL, 2 * SGU_WIDTH), const), pl.BlockSpec((1, SGU_WIDTH), const),
                  pl.BlockSpec((SGU_GROUPS, SGU_CHUNK, SGU_CHUNK), lambda b, i: (0, 0, 0)),
                  pl.BlockSpec((SGU_CHUNK, SGU_GROUPS), const), pl.BlockSpec((SGU_WIDTH, D_MODEL), const), vspec] + r_in,
        out_specs=[pl.BlockSpec((1, tm, D_MODEL), row)] + r_out,
        out_shape=[jax.ShapeDtypeStruct((BATCH, SEQ, D_MODEL), F32)] + r_shape,
        scratch_shapes=[pltpu.VMEM((tm, SGU_WIDTH), BF16), pltpu.VMEM((tm, SGU_WIDTH), F32)],
        compiler_params=_cparams(("arbitrary", "arbitrary"), vmem_mib=56),
        name="gmlp_layer",
    )(x, g1.reshape(1, D_MODEL), shift1.reshape(BATCH, 1, D_MODEL), scale1.reshape(BATCH, 1, D_MODEL),
      w_in.astype(BF16), norm_g.reshape(1, SGU_WIDTH), w_s, b_s.T, w_out.astype(BF16),
      gate.reshape(BATCH, 1, D_MODEL), *route_args)


MOE_TM = 256
MOE_CHUNK = 512
MOE_SORTED = N_TOK + N_EXPERT_GROUPS * MOE_TM
MOE_TILES = MOE_SORTED // MOE_TM
MOE_CHUNKS = N_TOK // MOE_CHUNK
MOE_PAIRS = MOE_TILES + N_EXPERT_GROUPS * MOE_CHUNKS
FLAG_ACTIVE, FLAG_FIRST, FLAG_LAST, FLAG_ZERO = 1, 2, 4, 8
EXP_WIN = 4
CMB_WIN = 8
MOE_TSTEPS = MOE_TILES + MOE_PAIRS // EXP_WIN
MOE_CSTEPS = MOE_CHUNKS + MOE_PAIRS // CMB_WIN


def _plan_kernel(gi_ref, rank_ref, before_ref):
    gi = gi_ref[...]
    r = lax.broadcasted_iota(jnp.int32, (POS_SIDE, POS_SIDE), 0)
    c = lax.broadcasted_iota(jnp.int32, (POS_SIDE, POS_SIDE), 1)
    upper = jnp.where(r <= c, 1.0, 0.0)
    lower_strict = jnp.where(c < r, 1.0, 0.0)
    rank = jnp.zeros((POS_SIDE, POS_SIDE), F32)
    for g in range(N_EXPERT_GROUPS):
        member = jnp.where(gi == g, 1.0, 0.0)
        in_row = jnp.dot(member, upper, precision=HI, preferred_element_type=F32)
        row_total = jnp.broadcast_to(in_row[:, POS_SIDE - 1:POS_SIDE], (POS_SIDE, POS_SIDE))
        before = jnp.dot(lower_strict, row_total, precision=HI, preferred_element_type=F32)
        before_ref[g] = before
        rank += member * (before + in_row - 1.0)
    rank_ref[...] = rank


def moe_plan(route):
    tm = MOE_TM
    i32 = jnp.int32
    gi_f = route[:, 0, :].reshape(POS_SIDE, POS_SIDE)
    rank, before = pl.pallas_call(
        _plan_kernel,
        out_shape=[jax.ShapeDtypeStruct((POS_SIDE, POS_SIDE), F32),
                   jax.ShapeDtypeStruct((N_EXPERT_GROUPS, POS_SIDE, POS_SIDE), F32)],
        name="moe_plan",
    )(gi_f)
    gi = gi_f.reshape(N_TOK).astype(i32)
    groups = jnp.arange(N_EXPERT_GROUPS, dtype=i32)
    member = gi[None, :] == groups[:, None]
    tot = jnp.sum(member, axis=1).astype(i32)
    padded = (tot + tm - 1) // tm * tm
    gend = jnp.cumsum(padded).astype(i32)
    gstart = gend - padded
    pos = jnp.sum(jnp.where(member, gstart[:, None], 0), axis=0).astype(i32) + rank.reshape(N_TOK).astype(i32)
    rows_per_chunk = MOE_CHUNK // POS_SIDE
    cnt_end = jnp.concatenate([before[:, rows_per_chunk::rows_per_chunk, 0].astype(i32), tot[:, None]], axis=1)
    t = jnp.arange(MOE_TILES, dtype=i32)
    n_used = gend[-1] // tm
    tile_g = jnp.minimum(jnp.sum(gend[None, :] <= (t * tm)[:, None], axis=1), N_EXPERT_GROUPS - 1).astype(i32)
    k0 = t * tm - gstart[tile_g]
    k1 = jnp.minimum(k0 + tm, tot[tile_g]) - 1
    ce = cnt_end[tile_g]
    c_lo = jnp.sum(ce <= k0[:, None], axis=1).astype(i32)
    c_hi = jnp.sum(ce <= k1[:, None], axis=1).astype(i32)
    npairs = jnp.where(t < n_used, c_hi - c_lo + 1, 0)
    pend = jnp.cumsum(npairs).astype(i32)
    pstart = pend - npairs
    total = pend[-1]
    l = jnp.arange(MOE_PAIRS, dtype=i32)
    real = l < total
    lt = jnp.minimum(l, total - 1)

    def windows(count, win, n_steps):
        per_item = (count + win - 1) // win
        end = jnp.cumsum(per_item).astype(i32)
        start = end - per_item
        s = jnp.arange(n_steps, dtype=i32)
        real_s = s < end[-1]
        sc = jnp.minimum(s, end[-1] - 1)
        item = jnp.sum(end[None, :] <= sc[:, None], axis=1).astype(i32)
        j = sc - start[item]
        flags_s = jnp.where(real_s, FLAG_ACTIVE + jnp.where(j == 0, FLAG_FIRST, 0)
                            + jnp.where(j == per_item[item] - 1, FLAG_LAST, 0), 0).astype(i32)
        return item, j, flags_s, real_s, s - end[-1], end[-1]

    tile_s, j, flags, real_s, spare, steps_t = windows(npairs, EXP_WIN, MOE_TSTEPS)
    steps_t = steps_t + (MOE_TILES - n_used)
    c0 = c_lo[tile_s] + EXP_WIN * j
    n_valid = jnp.minimum(EXP_WIN, c_hi[tile_s] - c0 + 1).astype(i32)
    spare_tile = jnp.minimum(n_used + spare, MOE_TILES - 1)
    flags = jnp.where(real_s, flags, jnp.where(spare_tile >= n_used, FLAG_ZERO, 0)).astype(i32)
    tile_sched = jnp.where(real_s, tile_s, spare_tile).astype(i32)
    by_tile = (tile_sched, c0.astype(i32), n_valid, flags, tile_g[tile_sched])
    cc = jnp.arange(MOE_CHUNKS, dtype=i32)
    is_pair = (cc[:, None] >= c_lo[None, :]) & (cc[:, None] <= c_hi[None, :]) & (t[None, :] < n_used)
    seen = jnp.cumsum(is_pair.reshape(-1).astype(i32))
    flat = jnp.sum(seen[None, :] <= lt[:, None], axis=1).astype(i32)
    pair_tile = flat % MOE_TILES
    per_chunk = jnp.sum(is_pair, axis=1).astype(i32)
    first_pair = jnp.cumsum(per_chunk).astype(i32) - per_chunk
    chunk_s, j, flags_c, _, _, steps_c = windows(per_chunk, CMB_WIN, MOE_CSTEPS)
    base = first_pair[chunk_s] + CMB_WIN * j
    n_valid_c = jnp.minimum(CMB_WIN, per_chunk[chunk_s] - CMB_WIN * j).astype(i32)
    tiles_c = tuple(pair_tile[jnp.minimum(base + w, total - 1)] for w in range(CMB_WIN))
    by_chunk = (chunk_s, n_valid_c, flags_c) + tiles_c
    return pos.reshape(MOE_CHUNKS, 1, MOE_CHUNK), (steps_t, by_tile), (steps_c, by_chunk)


def _one_hot_rows(pos_row, tile):
    rows = tile * MOE_TM + lax.broadcasted_iota(jnp.int32, (MOE_TM, MOE_CHUNK), 0)
    return jnp.where(pos_row == rows, 1.0, 0.0).astype(BF16)


def _moe_kernel(tile_ref, c0_ref, nv_ref, flag_ref, grp_ref, *refs):
    pos_refs, hx_refs = refs[0:EXP_WIN], refs[EXP_WIN:2 * EXP_WIN]
    wg_ref, wu_ref, wd_ref, y_ref, acc_ref = refs[2 * EXP_WIN:]
    l = pl.program_id(0)
    flags = flag_ref[l]

    @pl.when((flags & FLAG_FIRST) != 0)
    def _():
        acc_ref[...] = jnp.zeros_like(acc_ref)

    for w in range(EXP_WIN):
        @pl.when(((flags & FLAG_ACTIVE) != 0) & (w < nv_ref[l]))
        def _():
            onehot = _one_hot_rows(pos_refs[w][0], tile_ref[l])
            acc_ref[...] += jnp.dot(onehot, hx_refs[w][0], preferred_element_type=F32)

    @pl.when((flags & FLAG_LAST) != 0)
    def _():
        x = acc_ref[:, 0:D_MODEL].astype(BF16)
        terms = acc_ref[:, D_MODEL:HX_W]
        y = jnp.zeros((MOE_TM, D_MODEL), F32)
        for r in range(EXPERTS_PER_GROUP):
            lanes = [n * EXPERTS_PER_GROUP + r for n in range(HX_TERMS)]
            w_r = functools.reduce(jnp.add, [terms[:, c:c + 1] for c in lanes])
            gate = jnp.dot(x, wg_ref[0, 0, r], preferred_element_type=F32)
            up = jnp.dot(x, wu_ref[0, 0, r], preferred_element_type=F32)
            hid = (_silu(gate) * up * w_r).astype(BF16)
            y += jnp.dot(hid, wd_ref[0, 0, r], preferred_element_type=F32)
        y_ref[...] = y

    @pl.when((flags & FLAG_ZERO) != 0)
    def _():
        y_ref[...] = jnp.zeros_like(y_ref)


def moe_experts(hx, pos, schedule, w_gate, w_up, w_down, layer):
    n_steps, by_tile = schedule
    grouped = lambda w: w.reshape(DEPTH, N_EXPERT_GROUPS, EXPERTS_PER_GROUP, *w.shape[2:])
    wspec = lambda k, n: pl.BlockSpec((1, 1, EXPERTS_PER_GROUP, k, n), lambda l, t, c, n_, f, g: (layer, g[l], 0, 0, 0))
    chunk = lambda w: (lambda l, t, c, n_, f, g: (jnp.minimum(c[l] + w, MOE_CHUNKS - 1), 0, 0))
    return pl.pallas_call(
        _moe_kernel,
        grid_spec=pltpu.PrefetchScalarGridSpec(
            num_scalar_prefetch=5,
            grid=(n_steps,),
            in_specs=[pl.BlockSpec((1, 1, MOE_CHUNK), chunk(w)) for w in range(EXP_WIN)]
            + [pl.BlockSpec((1, MOE_CHUNK, HX_W), chunk(w)) for w in range(EXP_WIN)]
            + [wspec(D_MODEL, EXPERT_HIDDEN), wspec(D_MODEL, EXPERT_HIDDEN), wspec(EXPERT_HIDDEN, D_MODEL)],
            out_specs=pl.BlockSpec((MOE_TM, D_MODEL), lambda l, t, c, n_, f, g: (t[l], 0)),
            scratch_shapes=[pltpu.VMEM((MOE_TM, HX_W), F32)],
        ),
        out_shape=jax.ShapeDtypeStruct((MOE_SORTED, D_MODEL), F32),
        compiler_params=_cparams(("arbitrary",), vmem_mib=56),
        name="moe_experts",
    )(*by_tile, *([pos] * EXP_WIN), *([hx] * EXP_WIN), grouped(w_gate), grouped(w_up), grouped(w_down))


def _moe_combine_kernel(chunk_ref, nv_ref, flag_ref, *refs):
    tile_refs = refs[0:CMB_WIN]
    pos_ref = refs[CMB_WIN]
    y_refs = refs[CMB_WIN + 1:2 * CMB_WIN + 1]
    x_ref, gate_ref, o_ref, acc_ref = refs[2 * CMB_WIN + 1:]
    l = pl.program_id(0)
    flags = flag_ref[l]

    @pl.when((flags & FLAG_FIRST) != 0)
    def _():
        acc_ref[...] = jnp.zeros_like(acc_ref)

    for w in range(CMB_WIN):
        @pl.when(((flags & FLAG_ACTIVE) != 0) & (w < nv_ref[l]))
        def _():
            onehot = _one_hot_rows(pos_ref[0], tile_refs[w][l])
            y = y_refs[w][...]
            y_hi = y.astype(BF16)
            y_lo = (y - y_hi.astype(F32)).astype(BF16)
            acc_ref[...] += (lax.dot_general(onehot, y_hi, TN_DIMS, preferred_element_type=F32)
                             + lax.dot_general(onehot, y_lo, TN_DIMS, preferred_element_type=F32))

    @pl.when((flags & FLAG_LAST) != 0)
    def _():
        o_ref[0] = x_ref[0] + gate_ref[0] * acc_ref[...]


def moe_combine(x, y_sorted, pos, schedule, gate):
    n_steps, by_chunk = schedule
    per_b = SEQ // MOE_CHUNK
    tok = lambda l, c, *_: (c[l] // per_b, c[l] % per_b, 0)
    tile = lambda w: (lambda l, c, n_, f, *tiles: (tiles[w][l], 0))
    return pl.pallas_call(
        _moe_combine_kernel,
        grid_spec=pltpu.PrefetchScalarGridSpec(
            num_scalar_prefetch=3 + CMB_WIN,
            grid=(n_steps,),
            in_specs=[pl.BlockSpec((1, 1, MOE_CHUNK), lambda l, c, *_: (c[l], 0, 0))]
            + [pl.BlockSpec((MOE_TM, D_MODEL), tile(w)) for w in range(CMB_WIN)]
            + [pl.BlockSpec((1, MOE_CHUNK, D_MODEL), tok),
               pl.BlockSpec((1, 1, D_MODEL), lambda l, c, *_: (c[l] // per_b, 0, 0))],
            out_specs=pl.BlockSpec((1, MOE_CHUNK, D_MODEL), tok),
            scratch_shapes=[pltpu.VMEM((MOE_CHUNK, D_MODEL), F32)],
        ),
        out_shape=jax.ShapeDtypeStruct((BATCH, SEQ, D_MODEL), F32),
        compiler_params=_cparams(("arbitrary",)),
        name="moe_combine",
    )(*by_chunk, pos, *([y_sorted] * CMB_WIN), x, gate.reshape(BATCH, 1, D_MODEL))


def moe_layer(x, hx, route, gate, w_gate, w_up, w_down, layer):
    pos, by_tile, by_chunk = moe_plan(route)
    y_sorted = moe_experts(hx, pos, by_tile, w_gate, w_up, w_down, layer)
    return moe_combine(x, y_sorted, pos, by_chunk, gate)


def kernel(x, c, positions, w_ada, b_ada, norm_g, w_in_ab, w_out_ab, gla_w_gate2, gla_b_gate, gla_norm_g, nsa_q_gain, nsa_k_gain, nsa_cmp_pe, nsa_cmp_w1, nsa_cmp_w2, w_in_c, sgu_norm_g, sgu_w_s, sgu_b_s, w_out_c, w_router, router_bias, w_gate, w_up, w_down):
    mod = ada_modulation(c, w_ada, b_ada)
    qk, gv, gr, nq, nkv, misc = inproj0(x, norm_g[0, 0], mod[0, :, 0], mod[0, :, 1], _arrange_w_in(w_in_ab[0]))
    o_a = gla_mixer(qk, gv, gr, misc, gla_w_gate2[0], gla_b_gate[0], gla_norm_g[0])
    o_b = nsa_mixer(nq, nkv, misc, positions, nsa_q_gain[0], nsa_k_gain[0], nsa_cmp_pe[0], nsa_cmp_w1[0], nsa_cmp_w2[0])
    wg, wu, wd = w_gate.astype(BF16), w_up.astype(BF16), w_down.astype(BF16)
    route_args = lambda l: _route_args(norm_g[l, 1], mod[l, :, 3], mod[l, :, 4], w_router, router_bias)
    x1, h, route = outproj0(o_a, o_b, w_out_ab[0], x, mod[0, :, 2], route_args(0))
    x2 = moe_layer(x1, h, route, mod[0, :, 5], wg, wu, wd, 0)
    x3, h, route = gmlp_layer(x2, norm_g[1, 0], mod[1, :, 0], mod[1, :, 1], w_in_c[0], sgu_norm_g[0], sgu_w_s[0],
                              sgu_b_s[0], w_out_c[0], mod[1, :, 2], route_args(1))
    return moe_layer(x3, h, route, mod[1, :, 5], wg, wu, wd, 1)
```

```python
import functools

import numpy as np
import jax
import jax.numpy as jnp
from jax import lax
from jax.experimental import pallas as pl
from jax.experimental.pallas import tpu as pltpu

D_MODEL = 1024
BATCH = 2
SEQ = 8192
DEPTH = 2
N_TOK = BATCH * SEQ

GLA_HEADS = 4
GLA_DK = 64
GLA_DV = 128
GLA_GATE_RANK = 16
GLA_TAU = 16.0
GLA_CHUNK = 64
NSA_HEADS = 8
NSA_KV_GROUPS = 2
NSA_HPG = NSA_HEADS // NSA_KV_GROUPS
NSA_DH = 64
CMP_LEN = 32
CMP_STRIDE = 16
CMP_HIDDEN = 256
SEL_BLOCK = 64
SEL_TOPK = 16
WINDOW = 512
ROPE_THETA = 500000.0
ROT_DIM = NSA_DH // 4
ROT_HALF = ROT_DIM // 2
SGU_CHUNK = 128
SGU_GROUPS = 8
SGU_WIDTH = 2048
SGU_GROUP_DIM = SGU_WIDTH // SGU_GROUPS
N_EXPERTS = 16
N_EXPERT_GROUPS = 4
EXPERTS_PER_GROUP = N_EXPERTS // N_EXPERT_GROUPS
EXPERT_HIDDEN = 512

GLA_QK_W = GLA_HEADS * GLA_DK
GLA_V_W = GLA_HEADS * GLA_DV
NSA_Q_W = NSA_HEADS * NSA_DH
NSA_KV_W = NSA_KV_GROUPS * NSA_DH
N_CMP_PAD = SEQ // CMP_STRIDE
N_SEL = SEQ // SEL_BLOCK

NORM_EPS = 1e-6
NEG_INF = -1e30
FORCE_BONUS = 1e4

LANES = 128
MIB = 1024 * 1024

F32 = jnp.float32
BF16 = jnp.bfloat16
HI = lax.Precision.HIGHEST
NT_DIMS = (((1,), (1,)), ((), ()))
TN_DIMS = (((0,), (0,)), ((), ()))


def _cparams(sem, vmem_mib=48):
    return pltpu.CompilerParams(dimension_semantics=sem, vmem_limit_bytes=vmem_mib * MIB)


def _rms_mod(x, g, shift, scale):
    y = x * lax.rsqrt(jnp.mean(x * x, axis=-1, keepdims=True) + NORM_EPS) * g
    return y * (1 + scale) + shift


def _silu(x):
    return x * jax.nn.sigmoid(x)


def _log_sigmoid(z):
    return jnp.minimum(z, 0.0) - jnp.log1p(jnp.exp(-jnp.abs(z)))


ADA_TN = 1536
ADA_ROWS = 8


def _ada_kernel(c_ref, w_ref, b_ref, o_ref):
    cond = _silu(c_ref[...])
    o_ref[0] = jnp.dot(cond, w_ref[0], precision=HI, preferred_element_type=F32) + b_ref[0]


def ada_modulation(c, w_ada, b_ada):
    c8 = jnp.zeros((ADA_ROWS, D_MODEL), F32).at[:BATCH].set(c)
    width = 6 * D_MODEL
    out = pl.pallas_call(
        _ada_kernel,
        grid=(DEPTH, width // ADA_TN),
        in_specs=[
            pl.BlockSpec((ADA_ROWS, D_MODEL), lambda l, j: (0, 0)),
            pl.BlockSpec((1, D_MODEL, ADA_TN), lambda l, j: (l, 0, j)),
            pl.BlockSpec((1, 1, ADA_TN), lambda l, j: (l, 0, j)),
        ],
        out_specs=pl.BlockSpec((1, ADA_ROWS, ADA_TN), lambda l, j: (l, 0, j)),
        out_shape=jax.ShapeDtypeStruct((DEPTH, ADA_ROWS, width), F32),
        compiler_params=_cparams(("arbitrary", "arbitrary")),
        name="ada_modulation",
    )(c8, w_ada, b_ada.reshape(DEPTH, 1, width))
    return out[:, :BATCH].reshape(DEPTH, BATCH, 6, D_MODEL)


INPROJ_TM = 512
INPROJ_WIDTHS = (2 * GLA_QK_W, GLA_V_W, GLA_V_W, NSA_Q_W, 6 * NSA_KV_W, LANES)


def _arrange_w_in(w_in):
    o = np.cumsum((0, GLA_QK_W, GLA_QK_W, GLA_V_W, GLA_GATE_RANK, GLA_V_W, NSA_Q_W, 6 * NSA_KV_W, NSA_HEADS * 3))
    gq_gk = w_in[:, o[0]:o[2]]
    gv = w_in[:, o[2]:o[3]]
    glr = w_in[:, o[3]:o[4]]
    gr = w_in[:, o[4]:o[5]]
    nq = w_in[:, o[5]:o[6]]
    nkv = w_in[:, o[6]:o[7]]
    ng = w_in[:, o[7]:o[8]]
    pad = jnp.zeros((D_MODEL, LANES - GLA_GATE_RANK - NSA_HEADS * 3), w_in.dtype)
    return jnp.concatenate([gq_gk, gv, gr, nq, nkv, glr, ng, pad], axis=1).astype(BF16)


def _inproj0_kernel(x_ref, g_ref, sh_ref, sc_ref, w_ref, *o_refs):
    h = _rms_mod(x_ref[0], g_ref[...], sh_ref[0], sc_ref[0]).astype(BF16)
    off = 0
    for o_ref, wd in zip(o_refs, INPROJ_WIDTHS):
        o_ref[0] = jnp.dot(h, w_ref[:, off:off + wd], preferred_element_type=F32)
        off += wd


def inproj0(x, g, shift, scale, w_arranged):
    tm = INPROJ_TM
    wtot = sum(INPROJ_WIDTHS)
    row = lambda b, i: (b, i, 0)
    vec = lambda b, i: (b, 0, 0)
    return pl.pallas_call(
        _inproj0_kernel,
        grid=(BATCH, SEQ // tm),
        in_specs=[
            pl.BlockSpec((1, tm, D_MODEL), row),
            pl.BlockSpec((1, D_MODEL), lambda b, i: (0, 0)),
            pl.BlockSpec((1, 1, D_MODEL), vec),
            pl.BlockSpec((1, 1, D_MODEL), vec),
            pl.BlockSpec((D_MODEL, wtot), lambda b, i: (0, 0)),
        ],
        out_specs=[pl.BlockSpec((1, tm, wd), row) for wd in INPROJ_WIDTHS],
        out_shape=[jax.ShapeDtypeStruct((BATCH, SEQ, wd), F32) for wd in INPROJ_WIDTHS],
        compiler_params=_cparams(("arbitrary", "arbitrary")),
        name="inproj0",
    )(x, g.reshape(1, D_MODEL), shift.reshape(BATCH, 1, D_MODEL), scale.reshape(BATCH, 1, D_MODEL), w_arranged)


GLA_TG = 512


def _gla_chunk_sums():
    i = np.arange(GLA_TG)[:, None]
    j = np.arange(GLA_TG)[None, :]
    same = (i // GLA_CHUNK) == (j // GLA_CHUNK)
    m3 = np.concatenate([same & (j <= i), same & (j % GLA_CHUNK <= GLA_CHUNK // 2), same], axis=0).astype(np.float32)
    return jnp.asarray(np.concatenate([m3, m3], axis=1), BF16)


def _gla_kernel(qk_ref, v_ref, r_ref, misc_ref, w2_ref, bg_ref, og_ref, sums_ref, o_ref, st_ref):
    C, tg = GLA_CHUNK, GLA_TG

    @pl.when(pl.program_id(0) == 0)
    def _():
        st_ref[...] = jnp.zeros_like(st_ref)

    lane = lax.broadcasted_iota(jnp.int32, (1, GLA_QK_W), 1)
    heads = [(lane >= h * GLA_DK) & (lane < (h + 1) * GLA_DK) for h in range(GLA_HEADS)]
    stack = lambda per_head, rows: jnp.concatenate([t[rows] for t in per_head], axis=0)
    stacked_row = lax.broadcasted_iota(jnp.int32, (GLA_HEADS * C, C), 0)
    causal = (stacked_row & (C - 1)) >= lax.broadcasted_iota(jnp.int32, (GLA_HEADS * C, C), 1)
    og = og_ref[...]

    def prepare(b):
        z = jnp.dot(misc_ref[b], w2_ref[...], precision=HI, preferred_element_type=F32) + bg_ref[...]
        la = _log_sigmoid(z) / GLA_TAU
        la_hi = la.astype(BF16)
        la_lo = (la - la_hi.astype(F32)).astype(BF16)
        sums = jnp.dot(sums_ref[...], jnp.concatenate([la_hi, la_lo], axis=0), preferred_element_type=F32)
        bc, b_mid, b_last = sums[0:tg], sums[tg:2 * tg], sums[2 * tg:3 * tg]
        q = qk_ref[b, :, 0:GLA_QK_W] * (GLA_DK ** -0.5)
        k = qk_ref[b, :, GLA_QK_W:2 * GLA_QK_W]
        qd = q * jnp.exp(bc - b_mid)
        kl = k * jnp.exp(b_last - bc)
        qb = q * jnp.exp(bc)
        per_head = lambda t: [jnp.where(m, t, 0.0).astype(BF16) for m in heads]
        return dict(kd=(k * jnp.exp(b_mid - bc)).astype(BF16), dec=jnp.exp(b_last), qd_h=per_head(qd),
                    qb_h=per_head(qb), kl_h=per_head(kl))

    batches = range(BATCH)
    pre = [prepare(b) for b in batches]
    st = [st_ref[b] for b in batches]
    for c in range(tg // C):
        rows = slice(c * C, (c + 1) * C)
        for b in batches:
            p = pre[b]
            v = v_ref[b, rows, :].astype(BF16)
            s = lax.dot_general(stack(p["qd_h"], rows), p["kd"][rows], NT_DIMS, preferred_element_type=F32)
            s = jnp.where(causal, s, 0.0).astype(BF16)
            o_intra = jnp.dot(s, v, preferred_element_type=F32)
            o_inter = lax.dot_general(stack(p["qb_h"], rows), st[b].astype(BF16), NT_DIMS, preferred_element_type=F32)
            v_stack = jnp.concatenate([v[:, h * GLA_DV:(h + 1) * GLA_DV] for h in range(GLA_HEADS)], axis=0)
            st[b] = st[b] * p["dec"][c * C:c * C + 1] + lax.dot_general(v_stack, stack(p["kl_h"], rows), TN_DIMS,
                                                                       preferred_element_type=F32)
            for h in range(GLA_HEADS):
                hrows = slice(h * C, (h + 1) * C)
                vcols = slice(h * GLA_DV, (h + 1) * GLA_DV)
                o = o_intra[hrows, vcols] + o_inter[hrows]
                on = o * lax.rsqrt(jnp.mean(o * o, axis=-1, keepdims=True) + NORM_EPS) * og
                o_ref[b, rows, vcols] = on * _silu(r_ref[b, rows, vcols])
    for b in batches:
        st_ref[b] = st[b]


def gla_mixer(qk, v, r, misc, w_gate2, b_gate, out_g):
    tg = GLA_TG
    w2 = jnp.zeros((LANES, GLA_QK_W), F32).at[:GLA_GATE_RANK].set(w_gate2)
    row = lambda i: (0, i, 0)
    const = lambda i: (0, 0)
    return pl.pallas_call(
        _gla_kernel,
        grid=(SEQ // tg,),
        in_specs=[
            pl.BlockSpec((BATCH, tg, 2 * GLA_QK_W), row),
            pl.BlockSpec((BATCH, tg, GLA_V_W), row),
            pl.BlockSpec((BATCH, tg, GLA_V_W), row),
            pl.BlockSpec((BATCH, tg, LANES), row),
            pl.BlockSpec((LANES, GLA_QK_W), const),
            pl.BlockSpec((1, GLA_QK_W), const),
            pl.BlockSpec((1, GLA_DV), const),
            pl.BlockSpec((3 * tg, 2 * tg), const),
        ],
        out_specs=pl.BlockSpec((BATCH, tg, GLA_V_W), row),
        out_shape=jax.ShapeDtypeStruct((BATCH, SEQ, GLA_V_W), F32),
        scratch_shapes=[pltpu.VMEM((BATCH, GLA_DV, GLA_QK_W), F32)],
        compiler_params=_cparams(("arbitrary",)),
        name="gla_mixer",
    )(qk, v, r, misc, w2, b_gate.reshape(1, GLA_QK_W), out_g.reshape(1, GLA_DV), _gla_chunk_sums())


POS_SIDE = 128


def _rope_table_kernel(freq_ref, pos_ref, cos_ref, sin_ref):
    pos = pos_ref[...].astype(F32)
    for f in range(ROT_HALF):
        ang = pos * freq_ref[f]
        cos_ref[f] = jnp.cos(ang)
        sin_ref[f] = jnp.sin(ang)


def rope_tables(positions):
    inv_freq = jnp.float32(ROPE_THETA) ** (-jnp.arange(ROT_HALF, dtype=F32) / ROT_HALF)
    shp = jax.ShapeDtypeStruct((ROT_HALF, POS_SIDE, POS_SIDE), F32)
    cos, sin = pl.pallas_call(
        _rope_table_kernel,
        in_specs=[pl.BlockSpec(memory_space=pltpu.SMEM), pl.BlockSpec(memory_space=pltpu.VMEM)],
        out_specs=[pl.BlockSpec(memory_space=pltpu.VMEM)] * 2,
        out_shape=[shp, shp],
        name="rope_tables",
    )(inv_freq, positions.reshape(POS_SIDE, POS_SIDE))
    return jnp.concatenate([cos, sin], axis=0).reshape(ROT_DIM, N_TOK).T.reshape(BATCH, SEQ, ROT_DIM)


def _rope_placement():
    place = np.zeros((ROT_DIM, 3 * LANES), np.float32)
    const = np.zeros((1, 3 * LANES), np.float32)
    for lane in range(LANES):
        i = lane % NSA_DH
        if i < ROT_HALF:
            place[i, lane] = 1.0
            place[ROT_HALF + i, LANES + lane] = -1.0
        elif i < ROT_DIM:
            place[i - ROT_HALF, lane] = 1.0
            place[i, 2 * LANES + lane] = 1.0
        else:
            const[0, lane] = 1.0
    return jnp.asarray(place), jnp.asarray(const)


def _lane_tables(cs, place_ref, const_ref):
    tab = jnp.dot(cs, place_ref[...], precision=HI, preferred_element_type=F32) + const_ref[...]
    return tab[:, 0:LANES], tab[:, LANES:2 * LANES], tab[:, 2 * LANES:3 * LANES]


def _block_diag_ones2(width):
    h = np.arange(width) // NSA_DH
    bd = (h[:, None] == h[None, :]).astype(np.float32)
    return jnp.asarray(np.concatenate([bd, bd], axis=0), BF16)


def _head_norm_rope(x, gain, bd2, c, sm, sp):
    width = x.shape[-1]
    reps = width // LANES
    sq = x * x
    sq_hi = sq.astype(BF16)
    sq_lo = (sq - sq_hi.astype(F32)).astype(BF16)
    ss = jnp.dot(jnp.concatenate([sq_hi, sq_lo], axis=1), bd2, preferred_element_type=F32)
    y = x * lax.rsqrt(ss * (1.0 / NSA_DH) + NORM_EPS) * gain
    tile = lambda t: jnp.concatenate([t] * reps, axis=1) if reps > 1 else t
    return (y * tile(c) + pltpu.roll(y, width - ROT_HALF, 1) * tile(sm) + pltpu.roll(y, ROT_HALF, 1) * tile(sp))


PREP_TM = 512


def _prep_kernel(q_ref, kv_ref, cs_ref, place_ref, const_ref, gq_ref, gk_ref, bdq_ref, bdk_ref,
                 qo_ref, kso_ref, kwo_ref, vso_ref, vwo_ref):
    tm = PREP_TM
    c, sm, sp = _lane_tables(cs_ref[0], place_ref, const_ref)
    bdk = bdk_ref[...]
    q = _head_norm_rope(q_ref[0], gq_ref[...], bdq_ref[...], c, sm, sp) * (NSA_DH ** -0.5)
    qo_ref[0] = q.T.reshape(NSA_HEADS, NSA_DH, tm)
    kv_cols = lambda n: kv_ref[0, :, n * NSA_KV_W:(n + 1) * NSA_KV_W]
    ks = _head_norm_rope(kv_cols(2), gk_ref[0:1, :], bdk, c, sm, sp)
    kw = _head_norm_rope(kv_cols(4), gk_ref[1:2, :], bdk, c, sm, sp)
    lane = lax.broadcasted_iota(jnp.int32, (tm, LANES), 1)
    token = pl.program_id(1) * tm + lax.broadcasted_iota(jnp.int32, (tm, LANES), 0)
    block_col = NSA_DH + jnp.right_shift(token & (SA_TK - 1), 6)
    onehot = jnp.where(lane == block_col, 1.0, 0.0)
    for g in range(NSA_KV_GROUPS):
        to_front = lambda t: t if g == 0 else pltpu.roll(t, NSA_DH, 1)
        kso_ref[0, g] = jnp.where(lane < NSA_DH, to_front(ks), onehot).astype(BF16)
        kwo_ref[0, g] = jnp.where(lane < NSA_DH, to_front(kw), 0.0).astype(BF16)
    tail = jnp.where(lax.broadcasted_iota(jnp.int32, (VT_ROWS - NSA_DH, tm), 0) == 0, 1.0, 0.0)
    for n, out_ref in ((3, vso_ref), (5, vwo_ref)):
        v_t = kv_cols(n).T
        for g in range(NSA_KV_GROUPS):
            out_ref[0, g] = jnp.concatenate([v_t[g * NSA_DH:(g + 1) * NSA_DH], tail], axis=0).astype(BF16)


def nsa_prep(nq, nkv, cs, q_gain, k_gain):
    tm = PREP_TM
    G = NSA_KV_GROUPS
    row = lambda b, i: (b, i, 0)
    const = lambda b, i: (0, 0)
    gq = jnp.tile(q_gain, NSA_HEADS).reshape(1, NSA_Q_W)
    gk = jnp.stack([jnp.tile(k_gain[1], NSA_KV_GROUPS), jnp.tile(k_gain[2], NSA_KV_GROUPS)])
    kslab = pl.BlockSpec((1, G, tm, LANES), lambda b, i: (b, 0, i, 0))
    vslab = pl.BlockSpec((1, G, VT_ROWS, tm), lambda b, i: (b, 0, 0, i))
    return pl.pallas_call(
        _prep_kernel,
        grid=(BATCH, SEQ // tm),
        in_specs=[
            pl.BlockSpec((1, tm, NSA_Q_W), row),
            pl.BlockSpec((1, tm, 6 * NSA_KV_W), row),
            pl.BlockSpec((1, tm, ROT_DIM), row),
            pl.BlockSpec((ROT_DIM, 3 * LANES), const),
            pl.BlockSpec((1, 3 * LANES), const),
            pl.BlockSpec((1, NSA_Q_W), const),
            pl.BlockSpec((2, NSA_KV_W), const),
            pl.BlockSpec((2 * NSA_Q_W, NSA_Q_W), const),
            pl.BlockSpec((2 * NSA_KV_W, NSA_KV_W), const),
        ],
        out_specs=[pl.BlockSpec((1, NSA_HEADS, NSA_DH, tm), lambda b, i: (b, 0, 0, i)), kslab, kslab, vslab, vslab],
        out_shape=[jax.ShapeDtypeStruct((BATCH, NSA_HEADS, NSA_DH, SEQ), F32),
                   jax.ShapeDtypeStruct((BATCH, G, SEQ, LANES), BF16), jax.ShapeDtypeStruct((BATCH, G, SEQ, LANES), BF16),
                   jax.ShapeDtypeStruct((BATCH, G, VT_ROWS, SEQ), BF16), jax.ShapeDtypeStruct((BATCH, G, VT_ROWS, SEQ), BF16)],
        compiler_params=_cparams(("arbitrary", "arbitrary")),
        name="nsa_prep",
    )(nq, nkv, cs, *_rope_placement(), gq, gk, _block_diag_ones2(NSA_Q_W), _block_diag_ones2(NSA_KV_W))


SEG_W = CMP_STRIDE * NSA_DH


def _cmp_kernel(xk_ref, xv_ref, pe_ref, w1_ref, w2_ref, gain_ref, cs_ref, place_ref, const_ref, bd_ref, ko_ref, vo_ref):
    def compress(x_ref, kv):
        out = jnp.zeros((N_CMP_PAD, LANES), F32)
        for g in range(NSA_KV_GROUPS):
            x = x_ref[0, g]
            ha = jnp.dot(split3_keys(x + pe_ref[kv, 0]), w1_ref[kv, 0], preferred_element_type=F32)
            hb = jnp.dot(split3_keys(x + pe_ref[kv, 1]), w1_ref[kv, 1], preferred_element_type=F32)
            hid = ha + pltpu.roll(hb, N_CMP_PAD - 1, 0)
            out += jnp.dot(jax.nn.gelu(hid), w2_ref[kv, g], precision=HI, preferred_element_type=F32)
        return out

    c, sm, sp = _lane_tables(cs_ref[0], place_ref, const_ref)
    ko_ref[0] = _head_norm_rope(compress(xk_ref, 0), gain_ref[...], bd_ref[...], c, sm, sp)
    vo_ref[0] = compress(xv_ref, 1)


def nsa_compress(xk, xv, cmp_pe, cmp_w1, cmp_w2, k_gain0, cs_last):
    pe = cmp_pe.reshape(2, 2, 1, SEG_W)
    w1 = cmp_w1.reshape(2, 2, SEG_W, CMP_HIDDEN)
    w1_hi = w1.astype(BF16)
    w1 = jnp.concatenate([w1_hi, w1_hi, (w1 - w1_hi.astype(F32)).astype(BF16)], axis=2)
    w2 = jnp.zeros((2, NSA_KV_GROUPS, CMP_HIDDEN, LANES), F32)
    for g in range(NSA_KV_GROUPS):
        w2 = w2.at[:, g, :, g * NSA_DH:(g + 1) * NSA_DH].set(cmp_w2)
    seg = pl.BlockSpec((1, NSA_KV_GROUPS, N_CMP_PAD, SEG_W), lambda b: (b, 0, 0, 0))
    tab = pl.BlockSpec((1, N_CMP_PAD, LANES), lambda b: (b, 0, 0))
    full = lambda shape: pl.BlockSpec(shape, lambda b: (0,) * len(shape))
    return pl.pallas_call(
        _cmp_kernel,
        grid=(BATCH,),
        in_specs=[seg, seg, full((2, 2, 1, SEG_W)), full((2, 2, 3 * SEG_W, CMP_HIDDEN)),
                  full((2, NSA_KV_GROUPS, CMP_HIDDEN, LANES)), full((1, LANES)),
                  pl.BlockSpec((1, N_CMP_PAD, ROT_DIM), lambda b: (b, 0, 0)), full((ROT_DIM, 3 * LANES)),
                  full((1, 3 * LANES)), full((2 * LANES, LANES))],
        out_specs=[tab, tab],
        out_shape=[jax.ShapeDtypeStruct((BATCH, N_CMP_PAD, LANES), F32)] * 2,
        compiler_params=_cparams(("arbitrary",)),
        name="nsa_compress",
    )(xk, xv, pe, w1, w2, jnp.tile(k_gain0, NSA_KV_GROUPS).reshape(1, LANES), cs_last, *_rope_placement(),
      _block_diag_ones2(LANES))


CA_TQ = 512
CA_COLS = NSA_HPG * CA_TQ
CMP_PER_SEL = SEL_BLOCK // CMP_STRIDE
TOPK_BANDS = 4


def split3_keys(k):
    hi = k.astype(BF16)
    lo = (k - hi.astype(F32)).astype(BF16)
    return jnp.concatenate([hi, lo, hi], axis=-1)


def _top_k_rows(score, k):
    rows, cols = score.shape
    row = lax.broadcasted_iota(jnp.int32, (rows, cols), 0).astype(F32)
    taken = jnp.zeros((rows, cols), F32)
    left = score
    for _ in range(k):
        top = jnp.max(left, axis=0, keepdims=True)
        first = jnp.min(jnp.where(left == top, row, float(rows)), axis=0, keepdims=True)
        hit = row == first
        taken = jnp.where(hit, 1.0, taken)
        left = jnp.where(hit, -jnp.inf, left)
    return taken


def _cattn_kernel(q_ref, kc_ref, vct_ref, gl_ref, o_ref, sel_ref, q3_ref, ps_ref):
    tq = CA_TQ
    q0 = pl.program_id(1) * tq
    lanes4 = lambda t: jnp.concatenate([t] * NSA_HPG, axis=1)
    cend = lax.broadcasted_iota(jnp.int32, (N_CMP_PAD, tq), 0) * CMP_STRIDE + (CMP_LEN - 1)
    tc = q0 + lax.broadcasted_iota(jnp.int32, (N_CMP_PAD, tq), 1)
    cmask = lanes4(cend <= tc)
    jj = lax.broadcasted_iota(jnp.int32, (N_SEL, tq), 0)
    tt = q0 + lax.broadcasted_iota(jnp.int32, (N_SEL, tq), 1)
    cur = jnp.right_shift(tt, 6)
    forced = (jj == 0) | (jj == cur) | (jj == cur - 1)
    valid = jj * SEL_BLOCK <= tt

    for g in range(NSA_KV_GROUPS):
        heads = range(g * NSA_HPG, (g + 1) * NSA_HPG)
        for n, h in enumerate(heads):
            q = q_ref[0, h]
            hi = q.astype(BF16)
            lo = (q - hi.astype(F32)).astype(BF16)
            for t, part in enumerate((hi, hi, lo)):
                q3_ref[g, t * NSA_DH:(t + 1) * NSA_DH, n * tq:(n + 1) * tq] = part
        s = jnp.dot(kc_ref[0, g], q3_ref[g], preferred_element_type=F32)
        s = jnp.where(cmask, s, NEG_INF)
        m = jnp.max(s, axis=0, keepdims=True)
        e = jnp.where(cmask, jnp.exp(s - m), 0.0)
        l = jnp.sum(e, axis=0, keepdims=True)
        p = e / jnp.where(l > 0.0, l, 1.0)
        gate = jnp.concatenate([jax.nn.sigmoid(gl_ref[0, h, 0:1, :]) for h in heads], axis=1)
        o = jnp.dot(vct_ref[0, g], p.astype(BF16), preferred_element_type=F32) * gate
        for n, h in enumerate(heads):
            o_ref[0, h] = o[:, n * tq:(n + 1) * tq]
        psum = functools.reduce(jnp.add, [p[:, n * tq:(n + 1) * tq] for n in range(NSA_HPG)])
        for n in range(tq // LANES):
            ps_ref[g, n] = psum[:, n * LANES:(n + 1) * LANES]

        every4th = lambda r: jnp.concatenate(
            [ps_ref[g, n, pl.ds(r, N_SEL, stride=CMP_PER_SEL), :] for n in range(tq // LANES)], axis=1)
        starts_in = [every4th(r) for r in range(CMP_PER_SEL)]
        from_prev = jnp.where(jj >= 1, pltpu.roll(starts_in[CMP_PER_SEL - 1], 1, 0), 0.0)
        imp = functools.reduce(jnp.add, starts_in) + from_prev
        score = jnp.where(valid, imp + jnp.where(forced, FORCE_BONUS, 0.0), NEG_INF)
        step = pl.program_id(1)
        steps_per_band = (SEQ // tq) // TOPK_BANDS
        for band in range(TOPK_BANDS):
            n_rows = (band + 1) * (N_SEL // TOPK_BANDS)

            @pl.when((step >= band * steps_per_band) & (step < (band + 1) * steps_per_band))
            def _():
                taken = _top_k_rows(score[0:n_rows], SEL_TOPK)
                sel_ref[0, g, 0:n_rows, :] = jnp.where(valid[0:n_rows], taken, 0.0)
                if n_rows < N_SEL:
                    sel_ref[0, g, n_rows:N_SEL, :] = jnp.zeros((N_SEL - n_rows, tq), F32)


def nsa_cmp_attn(q_t, kcmp, vcmp_t, gl_t):
    tq = CA_TQ
    G = NSA_KV_GROUPS
    return pl.pallas_call(
        _cattn_kernel,
        grid=(BATCH, SEQ // tq),
        in_specs=[
            pl.BlockSpec((1, NSA_HEADS, NSA_DH, tq), lambda b, i: (b, 0, 0, i)),
            pl.BlockSpec((1, G, N_CMP_PAD, 3 * NSA_DH), lambda b, i: (b, 0, 0, 0)),
            pl.BlockSpec((1, G, NSA_DH, N_CMP_PAD), lambda b, i: (b, 0, 0, 0)),
            pl.BlockSpec((1, NSA_HEADS, 3, tq), lambda b, i: (b, 0, 0, i)),
        ],
        out_specs=[pl.BlockSpec((1, NSA_HEADS, NSA_DH, tq), lambda b, i: (b, 0, 0, i)),
                   pl.BlockSpec((1, G, N_SEL, tq), lambda b, i: (b, 0, 0, i))],
        out_shape=[jax.ShapeDtypeStruct((BATCH, NSA_HEADS, NSA_DH, SEQ), F32),
                   jax.ShapeDtypeStruct((BATCH, G, N_SEL, SEQ), F32)],
        scratch_shapes=[pltpu.VMEM((G, 3 * NSA_DH, CA_COLS), BF16), pltpu.VMEM((G, tq // LANES, N_CMP_PAD, LANES), F32)],
        compiler_params=_cparams(("arbitrary", "arbitrary")),
        name="nsa_cmp_attn",
    )(q_t, split3_keys(kcmp), vcmp_t.astype(BF16), gl_t)


SA_TQ = 256
SA_TK = 1024
SA_PARTS = 2
SA_PART = SA_TK // SA_PARTS
M_INIT = -1e20


SA_COLS = NSA_HPG * SA_TQ
SA_BLOCKS = SA_TK // SEL_BLOCK


VT_ROWS = NSA_DH + 16


def _sattn_kernel(q_ref, k_ref, vt_ref, sel_ref, gl_ref, kw_ref, vw_ref, prev_ref, o_ref, qa_ref, acc_ref, s_ref,
                  m_ref):
    tq, tk = SA_TQ, SA_TK
    i = pl.program_id(1)
    groups = range(NSA_KV_GROUPS)
    slots = range(2)
    for g in groups:
        for h in range(NSA_HPG):
            q = q_ref[0, g * NSA_HPG + h].astype(BF16)
            for slot in slots:
                qa_ref[slot, g, 0:NSA_DH, h * tq:(h + 1) * tq] = q
        for slot in slots:
            qa_ref[slot, g, NSA_DH:LANES, :] = jnp.zeros((LANES - NSA_DH, SA_COLS), BF16)
    acc_ref[...] = jnp.zeros_like(acc_ref)
    lanes4 = lambda t: jnp.concatenate([t] * NSA_HPG, axis=1)
    part_keys = lambda kt, part: pl.ds(pl.multiple_of(kt * tk + part * SA_PART, SA_PART), SA_PART)

    def scores(kt, slot):
        for g in groups:
            selrows = sel_ref[0, g, pl.ds(pl.multiple_of(kt * SA_BLOCKS, SA_BLOCKS), SA_BLOCKS), :]
            qa_ref[slot, g, NSA_DH:NSA_DH + SA_BLOCKS, :] = lanes4(jnp.where(selrows > 0.5, 0.0, NEG_INF)).astype(BF16)
            for part in range(SA_PARTS):
                s = jnp.dot(k_ref[0, g, part_keys(kt, part), :], qa_ref[slot, g], preferred_element_type=F32)
                s_ref[slot, g, part] = s.astype(BF16)

    def absorb(kt, slot, ms):
        out = []
        for g in groups:
            ss = [s_ref[slot, g, part] for part in range(SA_PARTS)]
            m_tile = functools.reduce(jnp.maximum, [jnp.max(s, axis=0, keepdims=True) for s in ss])
            m_new = jnp.maximum(ms[g], m_tile.astype(F32))
            acc = jnp.exp(ms[g] - m_new) * acc_ref[g]
            for part in range(SA_PARTS):
                p = jnp.exp(ss[part] - m_new.astype(BF16))
                acc += jnp.dot(vt_ref[0, g, :, part_keys(kt, part)], p, preferred_element_type=F32)
            acc_ref[g] = acc
            out.append(m_new)
        return tuple(out)

    def two_tiles(j, ms):
        kt = 2 * j
        scores(kt + 1, 1)
        ms = absorb(kt, 0, ms)
        scores(kt + 2, 0)
        return absorb(kt + 1, 1, ms)

    n_full = (i * tq) // tk
    scores(0, 0)
    m0 = tuple(jnp.full((1, SA_COLS), M_INIT, F32) for _ in groups)
    ms = lax.fori_loop(0, n_full // 2, two_tiles, m0)
    for g in groups:
        m_ref[g] = ms[g]

    def last_tile(slot):
        start = i * tq - n_full * tk
        part, row0 = start // SA_PART, pl.multiple_of(start % SA_PART, tq)
        tri = lax.broadcasted_iota(jnp.int32, (tq, tq), 0) <= lax.broadcasted_iota(jnp.int32, (tq, tq), 1)
        bias = lanes4(jnp.where(tri, 0.0, NEG_INF)).astype(BF16)
        for g in groups:
            s_ref[slot, g, part, pl.ds(row0, tq), :] += bias
        for g, m in enumerate(absorb(n_full, slot, tuple(m_ref[g] for g in groups))):
            m_ref[g] = m

    @pl.when(n_full % 2 == 0)
    def _():
        last_tile(0)

    @pl.when(n_full % 2 == 1)
    def _():
        scores(n_full, 1)
        for g, m in enumerate(absorb(n_full - 1, 0, tuple(m_ref[g] for g in groups))):
            m_ref[g] = m
        last_tile(1)

    w_rows = WINDOW + tq
    w_start = jnp.maximum(i * tq - WINDOW, 0)
    w_keys = pl.ds(pl.multiple_of(w_start, tq), w_rows)
    behind = (i * tq + lax.broadcasted_iota(jnp.int32, (w_rows, tq), 1)
              - (w_start + lax.broadcasted_iota(jnp.int32, (w_rows, tq), 0)))
    w_bias = lanes4(jnp.where((behind >= 0) & (behind < WINDOW), 0.0, NEG_INF))
    for g in groups:
        heads = range(g * NSA_HPG, (g + 1) * NSA_HPG)
        gates = lambda branch: jnp.concatenate([jax.nn.sigmoid(gl_ref[0, h, branch:branch + 1, :]) for h in heads], axis=1)
        out = acc_ref[g, 0:NSA_DH, :] / acc_ref[g, NSA_DH:NSA_DH + 1, :] * gates(1)
        s = (jnp.dot(kw_ref[0, g, w_keys, :], qa_ref[0, g], preferred_element_type=F32) + w_bias).astype(BF16)
        p = jnp.exp(s - jnp.max(s, axis=0, keepdims=True))
        acc_w = jnp.dot(vw_ref[0, g, :, w_keys], p, preferred_element_type=F32)
        out += acc_w[0:NSA_DH] / acc_w[NSA_DH:NSA_DH + 1] * gates(2)
        for n, h in enumerate(heads):
            o_ref[0, h] = prev_ref[0, h] + out[:, n * tq:(n + 1) * tq]


def nsa_sel_win_attn(q_t, k_slab, vsel_t, sel_t, gl_t, kwin, vwin_t, prev):
    tq = SA_TQ
    G = NSA_KV_GROUPS
    ospec = pl.BlockSpec((1, NSA_HEADS, NSA_DH, tq), lambda b, i: (b, 0, 0, i))
    kspec = pl.BlockSpec((1, G, SEQ, LANES), lambda b, i: (b, 0, 0, 0))
    vspec = pl.BlockSpec((1, G, VT_ROWS, SEQ), lambda b, i: (b, 0, 0, 0))
    return pl.pallas_call(
        _sattn_kernel,
        grid=(BATCH, SEQ // tq),
        in_specs=[
            ospec, kspec, vspec,
            pl.BlockSpec((1, G, N_SEL, tq), lambda b, i: (b, 0, 0, i)),
            pl.BlockSpec((1, NSA_HEADS, 3, tq), lambda b, i: (b, 0, 0, i)),
            kspec, vspec, ospec,
        ],
        out_specs=ospec,
        out_shape=jax.ShapeDtypeStruct((BATCH, NSA_HEADS, NSA_DH, SEQ), F32),
        scratch_shapes=[pltpu.VMEM((2, G, LANES, SA_COLS), BF16), pltpu.VMEM((G, VT_ROWS, SA_COLS), F32),
                        pltpu.VMEM((2, G, SA_PARTS, SA_PART, SA_COLS), BF16), pltpu.VMEM((G, 1, SA_COLS), F32)],
        input_output_aliases={7: 0},
        compiler_params=_cparams(("arbitrary", "arbitrary"), vmem_mib=56),
        name="nsa_sel_win_attn",
    )(q_t, k_slab, vsel_t, sel_t, gl_t, kwin, vwin_t, prev)


def nsa_mixer(nq, nkv, misc, positions, q_gain, k_gain, cmp_pe, cmp_w1, cmp_w2):
    cs = rope_tables(positions)
    q_t, ksel, kwin, vsel_t, vwin_t = nsa_prep(nq, nkv, cs, q_gain, k_gain)
    group_major = lambda t: t.reshape(BATCH, SEQ, NSA_KV_GROUPS, NSA_DH).transpose(0, 2, 1, 3)
    col = lambda n: nkv[..., n * NSA_KV_W:(n + 1) * NSA_KV_W]
    segs = lambda t: group_major(t).reshape(BATCH, NSA_KV_GROUPS, N_CMP_PAD, SEG_W)
    last = jnp.minimum(jnp.arange(N_CMP_PAD) * CMP_STRIDE + CMP_LEN - 1, SEQ - 1)
    kcmp, vcmp = nsa_compress(segs(col(0)), segs(col(1)), cmp_pe, cmp_w1, cmp_w2, k_gain[0], cs[:, last])
    kcmp = kcmp.reshape(BATCH, N_CMP_PAD, NSA_KV_GROUPS, NSA_DH).transpose(0, 2, 1, 3)
    vcmp_t = vcmp.reshape(BATCH, N_CMP_PAD, NSA_KV_GROUPS, NSA_DH).transpose(0, 2, 3, 1)
    gl_t = misc[..., GLA_GATE_RANK:GLA_GATE_RANK + NSA_HEADS * 3].reshape(BATCH, SEQ, NSA_HEADS, 3).transpose(0, 2, 3, 1)
    o_t, sel_t = nsa_cmp_attn(q_t, kcmp, vcmp_t, gl_t)
    return nsa_sel_win_attn(q_t, ksel, vsel_t, sel_t, gl_t, kwin, vwin_t, o_t)


ROUTE_ROWS = 8
HX_TERMS = 3
HX_W = D_MODEL + LANES


def _top2_sum(a, b, c, d):
    hi1, lo1 = jnp.maximum(a, b), jnp.minimum(a, b)
    hi2, lo2 = jnp.maximum(c, d), jnp.minimum(c, d)
    return jnp.maximum(hi1, hi2) + jnp.maximum(jnp.minimum(hi1, hi2), jnp.maximum(lo1, lo2))


def _moe_prenorm_route(xn, g_ref, sh_ref, sc_ref, wr_ref, rb_ref, hx_ref, route_ref):
    h = _rms_mod(xn, g_ref[...], sh_ref[0], sc_ref[0])
    logits = lax.dot_general(wr_ref[...], h, NT_DIMS, precision=HI, preferred_element_type=F32)
    scores = jax.nn.sigmoid(logits)
    sel = scores + rb_ref[...]
    epg = EXPERTS_PER_GROUP
    srow = lambda e: sel[e:e + 1, :]
    grp = [_top2_sum(*[srow(epg * g + r) for r in range(epg)]) for g in range(N_EXPERT_GROUPS)]
    best, gi = grp[0], jnp.zeros_like(grp[0], dtype=jnp.int32)
    for g in range(1, N_EXPERT_GROUPS):
        better = grp[g] > best
        gi = jnp.where(better, g, gi)
        best = jnp.where(better, grp[g], best)

    def in_group(mat, r):
        out = mat[r:r + 1, :]
        for g in range(1, N_EXPERT_GROUPS):
            out = jnp.where(gi == g, mat[epg * g + r:epg * g + r + 1, :], out)
        return out

    v = [in_group(sel, r) for r in range(epg)]
    sc = [in_group(scores, r) for r in range(epg)]
    b1, i1, w1 = v[0], jnp.zeros_like(gi), sc[0]
    for r in range(1, epg):
        better = v[r] > b1
        i1 = jnp.where(better, r, i1)
        w1 = jnp.where(better, sc[r], w1)
        b1 = jnp.where(better, v[r], b1)
    b2 = jnp.full_like(b1, -3e38)
    i2, w2 = jnp.zeros_like(gi), jnp.zeros_like(w1)
    for r in range(epg):
        better = (i1 != r) & (v[r] > b2)
        i2 = jnp.where(better, r, i2)
        w2 = jnp.where(better, sc[r], w2)
        b2 = jnp.where(better, v[r], b2)
    tot = w1 + w2
    w1, w2 = w1 / tot, w2 / tot
    zero = jnp.zeros_like(w1)
    route_ref[0] = jnp.concatenate([gi.astype(F32)] + [zero] * (ROUTE_ROWS - 1), axis=0)
    w = jnp.concatenate([jnp.where(i1 == r, w1, jnp.where(i2 == r, w2, 0.0)) for r in range(epg)], axis=0)
    w_hi = w.astype(BF16).astype(F32)
    w_mid = (w - w_hi).astype(BF16).astype(F32)
    w_lo = (w - w_hi - w_mid).astype(BF16).astype(F32)
    pad = jnp.zeros((LANES - HX_TERMS * epg, w.shape[1]), F32)
    hx_ref[0, :, 0:D_MODEL] = h.astype(BF16)
    hx_ref[0, :, D_MODEL:HX_W] = jnp.concatenate([w_hi, w_mid, w_lo, pad], axis=0).T.astype(BF16)


def _route_specs(tm, row, vec, const):
    in_specs = [pl.BlockSpec((1, D_MODEL), const), pl.BlockSpec((1, 1, D_MODEL), vec), pl.BlockSpec((1, 1, D_MODEL), vec),
                pl.BlockSpec((N_EXPERTS, D_MODEL), const), pl.BlockSpec((N_EXPERTS, 1), const)]
    assert tm == MOE_CHUNK
    out_specs = [pl.BlockSpec((1, tm, HX_W), lambda b, i: (b * (SEQ // tm) + i, 0, 0)),
                 pl.BlockSpec((1, ROUTE_ROWS, tm), lambda b, i: (b, 0, i))]
    out_shape = [jax.ShapeDtypeStruct((MOE_CHUNKS, MOE_CHUNK, HX_W), BF16),
                 jax.ShapeDtypeStruct((BATCH, ROUTE_ROWS, SEQ), F32)]
    return in_specs, out_specs, out_shape


def _route_args(g, shift, scale, w_router, router_bias):
    return (g.reshape(1, D_MODEL), shift.reshape(BATCH, 1, D_MODEL), scale.reshape(BATCH, 1, D_MODEL),
            w_router.T, router_bias.reshape(N_EXPERTS, 1))


OUTPROJ_TM = 512


def _outproj0_kernel(oa_ref, ob_ref, w_ref, x_ref, gate_ref, g_ref, sh_ref, sc_ref, wr_ref, rb_ref,
                     xo_ref, h_ref, route_ref):
    y = jnp.dot(oa_ref[0].astype(BF16), w_ref[0:GLA_V_W, :], preferred_element_type=F32)
    ob_t = ob_ref[0].reshape(NSA_Q_W, OUTPROJ_TM).astype(BF16)
    y += lax.dot_general(ob_t, w_ref[GLA_V_W:GLA_V_W + NSA_Q_W, :], TN_DIMS, preferred_element_type=F32)
    xn = x_ref[0] + gate_ref[0] * y
    xo_ref[0] = xn
    _moe_prenorm_route(xn, g_ref, sh_ref, sc_ref, wr_ref, rb_ref, h_ref, route_ref)


def outproj0(o_a, o_b, w_out, x, gate, route_args):
    tm = OUTPROJ_TM
    row = lambda b, i: (b, i, 0)
    vec = lambda b, i: (b, 0, 0)
    const = lambda b, i: (0, 0)
    r_in, r_out, r_shape = _route_specs(tm, row, vec, const)
    return pl.pallas_call(
        _outproj0_kernel,
        grid=(BATCH, SEQ // tm),
        in_specs=[pl.BlockSpec((1, tm, GLA_V_W), row), pl.BlockSpec((1, NSA_HEADS, NSA_DH, tm), lambda b, i: (b, 0, 0, i)),
                  pl.BlockSpec((GLA_V_W + NSA_Q_W, D_MODEL), const), pl.BlockSpec((1, tm, D_MODEL), row),
                  pl.BlockSpec((1, 1, D_MODEL), vec)] + r_in,
        out_specs=[pl.BlockSpec((1, tm, D_MODEL), row)] + r_out,
        out_shape=[jax.ShapeDtypeStruct((BATCH, SEQ, D_MODEL), F32)] + r_shape,
        compiler_params=_cparams(("arbitrary", "arbitrary")),
        name="outproj0",
    )(o_a, o_b, w_out.astype(BF16), x, gate.reshape(BATCH, 1, D_MODEL), *route_args)


GMLP_TM = 512


def _gmlp_kernel(x_ref, g1_ref, sh1_ref, sc1_ref, win_ref, ng_ref, ws_ref, bs_ref, wout_ref, gate_ref,
                 g_ref, sh_ref, sc_ref, wr_ref, rb_ref, xo_ref, h_ref, route_ref, gated_ref, v_ref):
    x = x_ref[0]
    h = _rms_mod(x, g1_ref[...], sh1_ref[0], sc1_ref[0]).astype(BF16)
    group_cols = lambda g: slice(g * SGU_GROUP_DIM, (g + 1) * SGU_GROUP_DIM)
    ssq = jnp.zeros((GMLP_TM, LANES), F32)
    for g in range(SGU_GROUPS):
        lo = SGU_WIDTH + g * SGU_GROUP_DIM
        v = jax.nn.gelu(jnp.dot(h, win_ref[:, lo:lo + SGU_GROUP_DIM], preferred_element_type=F32))
        v_ref[:, group_cols(g)] = v
        ssq += functools.reduce(jnp.add, [v[:, n * LANES:(n + 1) * LANES] ** 2 for n in range(SGU_GROUP_DIM // LANES)])
    rs = lax.rsqrt(jnp.sum(ssq, axis=-1, keepdims=True) * (1.0 / SGU_WIDTH) + NORM_EPS)
    ri = lax.broadcasted_iota(jnp.int32, (SGU_CHUNK, SGU_CHUNK), 0)
    ci = lax.broadcasted_iota(jnp.int32, (SGU_CHUNK, SGU_CHUNK), 1)
    for g in range(SGU_GROUPS):
        cols = group_cols(g)
        u = jax.nn.gelu(jnp.dot(h, win_ref[:, cols], preferred_element_type=F32))
        vn = (v_ref[:, cols] * rs * ng_ref[:, cols]).astype(BF16)
        w = jnp.where(ri >= ci, ws_ref[g], 0.0).astype(BF16)
        for c in range(GMLP_TM // SGU_CHUNK):
            rows = slice(c * SGU_CHUNK, (c + 1) * SGU_CHUNK)
            mix = jnp.dot(w, vn[rows], preferred_element_type=F32) + bs_ref[:, g:g + 1]
            gated_ref[rows, cols] = (u[rows] * mix).astype(BF16)
    y = jnp.dot(gated_ref[...], wout_ref[...], preferred_element_type=F32)
    xn = x + gate_ref[0] * y
    xo_ref[0] = xn
    _moe_prenorm_route(xn, g_ref, sh_ref, sc_ref, wr_ref, rb_ref, h_ref, route_ref)


def gmlp_layer(x, g1, shift1, scale1, w_in, norm_g, w_s, b_s, w_out, gate, route_args):
    tm = GMLP_TM
    row = lambda b, i: (b, i, 0)
    vec = lambda b, i: (b, 0, 0)
    const = lambda b, i: (0, 0)
    r_in, r_out, r_shape = _route_specs(tm, row, vec, const)
    vspec = pl.BlockSpec((1, 1, D_MODEL), vec)
    return pl.pallas_call(
        _gmlp_kernel,
        grid=(BATCH, SEQ // tm),
        in_specs=[pl.BlockSpec((1, tm, D_MODEL), row), pl.BlockSpec((1, D_MODEL), const), vspec, vspec,
                  pl.BlockSpec((D_MODEL, 2 * SGU_WIDTH), const), pl.BlockSpec((1, SGU_WIDTH), const),
                  pl.BlockSpec((SGU_GROUPS, SGU_CHUNK, SGU_CHUNK), lambda b, i: (0, 0, 0)),
                  pl.BlockSpec((SGU_CHUNK, SGU_GROUPS), const), pl.BlockSpec((SGU_WIDTH, D_MODEL), const), vspec] + r_in,
        out_specs=[pl.BlockSpec((1, tm, D_MODEL), row)] + r_out,
        out_shape=[jax.ShapeDtypeStruct((BATCH, SEQ, D_MODEL), F32)] + r_shape,
        scratch_shapes=[pltpu.VMEM((tm, SGU_WIDTH), BF16), pltpu.VMEM((tm, SGU_WIDTH), F32)],
        compiler_params=_cparams(("arbitrary", "arbitrary"), vmem_mib=56),
        name="gmlp_layer",
    )(x, g1.reshape(1, D_MODEL), shift1.reshape(BATCH, 1, D_MODEL), scale1.reshape(BATCH, 1, D_MODEL),
      w_in.astype(BF16), norm_g.reshape(1, SGU_WIDTH), w_s, b_s.T, w_out.astype(BF16),
      gate.reshape(BATCH, 1, D_MODEL), *route_args)


MOE_TM = 256
MOE_CHUNK = 512
MOE_SORTED = N_TOK + N_EXPERT_GROUPS * MOE_TM
MOE_TILES = MOE_SORTED // MOE_TM
MOE_CHUNKS = N_TOK // MOE_CHUNK
MOE_PAIRS = MOE_TILES + N_EXPERT_GROUPS * MOE_CHUNKS
FLAG_ACTIVE, FLAG_FIRST, FLAG_LAST, FLAG_ZERO = 1, 2, 4, 8
EXP_WIN = 4
CMB_WIN = 8
MOE_TSTEPS = MOE_TILES + MOE_PAIRS // EXP_WIN
MOE_CSTEPS = MOE_CHUNKS + MOE_PAIRS // CMB_WIN


def _plan_kernel(gi_ref, rank_ref, before_ref):
    gi = gi_ref[...]
    r = lax.broadcasted_iota(jnp.int32, (POS_SIDE, POS_SIDE), 0)
    c = lax.broadcasted_iota(jnp.int32, (POS_SIDE, POS_SIDE), 1)
    upper = jnp.where(r <= c, 1.0, 0.0)
    lower_strict = jnp.where(c < r, 1.0, 0.0)
    rank = jnp.zeros((POS_SIDE, POS_SIDE), F32)
    for g in range(N_EXPERT_GROUPS):
        member = jnp.where(gi == g, 1.0, 0.0)
        in_row = jnp.dot(member, upper, precision=HI, preferred_element_type=F32)
        row_total = jnp.broadcast_to(in_row[:, POS_SIDE - 1:POS_SIDE], (POS_SIDE, POS_SIDE))
        before = jnp.dot(lower_strict, row_total, precision=HI, preferred_element_type=F32)
        before_ref[g] = before
        rank += member * (before + in_row - 1.0)
    rank_ref[...] = rank


def moe_plan(route):
    tm = MOE_TM
    i32 = jnp.int32
    gi_f = route[:, 0, :].reshape(POS_SIDE, POS_SIDE)
    rank, before = pl.pallas_call(
        _plan_kernel,
        out_shape=[jax.ShapeDtypeStruct((POS_SIDE, POS_SIDE), F32),
                   jax.ShapeDtypeStruct((N_EXPERT_GROUPS, POS_SIDE, POS_SIDE), F32)],
        name="moe_plan",
    )(gi_f)
    gi = gi_f.reshape(N_TOK).astype(i32)
    groups = jnp.arange(N_EXPERT_GROUPS, dtype=i32)
    member = gi[None, :] == groups[:, None]
    tot = jnp.sum(member, axis=1).astype(i32)
    padded = (tot + tm - 1) // tm * tm
    gend = jnp.cumsum(padded).astype(i32)
    gstart = gend - padded
    pos = jnp.sum(jnp.where(member, gstart[:, None], 0), axis=0).astype(i32) + rank.reshape(N_TOK).astype(i32)
    rows_per_chunk = MOE_CHUNK // POS_SIDE
    cnt_end = jnp.concatenate([before[:, rows_per_chunk::rows_per_chunk, 0].astype(i32), tot[:, None]], axis=1)
    t = jnp.arange(MOE_TILES, dtype=i32)
    n_used = gend[-1] // tm
    tile_g = jnp.minimum(jnp.sum(gend[None, :] <= (t * tm)[:, None], axis=1), N_EXPERT_GROUPS - 1).astype(i32)
    k0 = t * tm - gstart[tile_g]
    k1 = jnp.minimum(k0 + tm, tot[tile_g]) - 1
    ce = cnt_end[tile_g]
    c_lo = jnp.sum(ce <= k0[:, None], axis=1).astype(i32)
    c_hi = jnp.sum(ce <= k1[:, None], axis=1).astype(i32)
    npairs = jnp.where(t < n_used, c_hi - c_lo + 1, 0)
    total = jnp.sum(npairs).astype(i32)
    lt = jnp.minimum(jnp.arange(MOE_PAIRS, dtype=i32), total - 1)

    def windows(count, win, n_steps):
        per_item = (count + win - 1) // win
        end = jnp.cumsum(per_item).astype(i32)
        start = end - per_item
        s = jnp.arange(n_steps, dtype=i32)
        real_s = s < end[-1]
        sc = jnp.minimum(s, end[-1] - 1)
        item = jnp.sum(end[None, :] <= sc[:, None], axis=1).astype(i32)
        j = sc - start[item]
        flags_s = jnp.where(real_s, FLAG_ACTIVE + jnp.where(j == 0, FLAG_FIRST, 0)
                            + jnp.where(j == per_item[item] - 1, FLAG_LAST, 0), 0).astype(i32)
        return item, j, flags_s, real_s, s - end[-1], end[-1]

    tile_s, j, flags, real_s, spare, steps_t = windows(npairs, EXP_WIN, MOE_TSTEPS)
    steps_t = steps_t + (MOE_TILES - n_used)
    c0 = c_lo[tile_s] + EXP_WIN * j
    n_valid = jnp.minimum(EXP_WIN, c_hi[tile_s] - c0 + 1).astype(i32)
    spare_tile = jnp.minimum(n_used + spare, MOE_TILES - 1)
    flags = jnp.where(real_s, flags, jnp.where(spare_tile >= n_used, FLAG_ZERO, 0)).astype(i32)
    tile_sched = jnp.where(real_s, tile_s, spare_tile).astype(i32)
    by_tile = (tile_sched, c0.astype(i32), n_valid, flags, tile_g[tile_sched])
    cc = jnp.arange(MOE_CHUNKS, dtype=i32)
    is_pair = (cc[:, None] >= c_lo[None, :]) & (cc[:, None] <= c_hi[None, :]) & (t[None, :] < n_used)
    seen = jnp.cumsum(is_pair.reshape(-1).astype(i32))
    flat = jnp.sum(seen[None, :] <= lt[:, None], axis=1).astype(i32)
    pair_tile = flat % MOE_TILES
    per_chunk = jnp.sum(is_pair, axis=1).astype(i32)
    first_pair = jnp.cumsum(per_chunk).astype(i32) - per_chunk
    chunk_s, j, flags_c, _, _, steps_c = windows(per_chunk, CMB_WIN, MOE_CSTEPS)
    base = first_pair[chunk_s] + CMB_WIN * j
    n_valid_c = jnp.minimum(CMB_WIN, per_chunk[chunk_s] - CMB_WIN * j).astype(i32)
    tiles_c = tuple(pair_tile[jnp.minimum(base + w, total - 1)] for w in range(CMB_WIN))
    by_chunk = (chunk_s, n_valid_c, flags_c) + tiles_c
    return pos.reshape(MOE_CHUNKS, 1, MOE_CHUNK), (steps_t, by_tile), (steps_c, by_chunk)


def _one_hot_rows(pos_row, tile):
    rows = tile * MOE_TM + lax.broadcasted_iota(jnp.int32, (MOE_TM, MOE_CHUNK), 0)
    return jnp.where(pos_row == rows, 1.0, 0.0).astype(BF16)


def _moe_kernel(tile_ref, c0_ref, nv_ref, flag_ref, grp_ref, *refs):
    pos_refs, hx_refs = refs[0:EXP_WIN], refs[EXP_WIN:2 * EXP_WIN]
    wg_ref, wu_ref, wd_ref, y_ref, acc_ref = refs[2 * EXP_WIN:]
    l = pl.program_id(0)
    flags = flag_ref[l]

    @pl.when((flags & FLAG_FIRST) != 0)
    def _():
        acc_ref[...] = jnp.zeros_like(acc_ref)

    for w in range(EXP_WIN):
        @pl.when(((flags & FLAG_ACTIVE) != 0) & (w < nv_ref[l]))
        def _():
            onehot = _one_hot_rows(pos_refs[w][0], tile_ref[l])
            acc_ref[...] += jnp.dot(onehot, hx_refs[w][0], preferred_element_type=F32)

    @pl.when((flags & FLAG_LAST) != 0)
    def _():
        x = acc_ref[:, 0:D_MODEL].astype(BF16)
        terms = acc_ref[:, D_MODEL:HX_W]
        y = jnp.zeros((MOE_TM, D_MODEL), F32)
        for r in range(EXPERTS_PER_GROUP):
            lanes = [n * EXPERTS_PER_GROUP + r for n in range(HX_TERMS)]
            w_r = functools.reduce(jnp.add, [terms[:, c:c + 1] for c in lanes])
            gate = jnp.dot(x, wg_ref[0, 0, r], preferred_element_type=F32)
            up = jnp.dot(x, wu_ref[0, 0, r], preferred_element_type=F32)
            hid = (_silu(gate) * up * w_r).astype(BF16)
            y += jnp.dot(hid, wd_ref[0, 0, r], preferred_element_type=F32)
        y_ref[...] = y

    @pl.when((flags & FLAG_ZERO) != 0)
    def _():
        y_ref[...] = jnp.zeros_like(y_ref)


def moe_experts(hx, pos, schedule, w_gate, w_up, w_down, layer):
    n_steps, by_tile = schedule
    grouped = lambda w: w.reshape(DEPTH, N_EXPERT_GROUPS, EXPERTS_PER_GROUP, *w.shape[2:])
    wspec = lambda k, n: pl.BlockSpec((1, 1, EXPERTS_PER_GROUP, k, n), lambda l, t, c, n_, f, g: (layer, g[l], 0, 0, 0))
    chunk = lambda w: (lambda l, t, c, n_, f, g: (jnp.minimum(c[l] + w, MOE_CHUNKS - 1), 0, 0))
    return pl.pallas_call(
        _moe_kernel,
        grid_spec=pltpu.PrefetchScalarGridSpec(
            num_scalar_prefetch=5,
            grid=(n_steps,),
            in_specs=[pl.BlockSpec((1, 1, MOE_CHUNK), chunk(w)) for w in range(EXP_WIN)]
            + [pl.BlockSpec((1, MOE_CHUNK, HX_W), chunk(w)) for w in range(EXP_WIN)]
            + [wspec(D_MODEL, EXPERT_HIDDEN), wspec(D_MODEL, EXPERT_HIDDEN), wspec(EXPERT_HIDDEN, D_MODEL)],
            out_specs=pl.BlockSpec((MOE_TM, D_MODEL), lambda l, t, c, n_, f, g: (t[l], 0)),
            scratch_shapes=[pltpu.VMEM((MOE_TM, HX_W), F32)],
        ),
        out_shape=jax.ShapeDtypeStruct((MOE_SORTED, D_MODEL), F32),
        compiler_params=_cparams(("arbitrary",), vmem_mib=56),
        name="moe_experts",
    )(*by_tile, *([pos] * EXP_WIN), *([hx] * EXP_WIN), grouped(w_gate), grouped(w_up), grouped(w_down))


def _moe_combine_kernel(chunk_ref, nv_ref, flag_ref, *refs):
    tile_refs = refs[0:CMB_WIN]
    pos_ref = refs[CMB_WIN]
    y_refs = refs[CMB_WIN + 1:2 * CMB_WIN + 1]
    x_ref, gate_ref, o_ref, acc_ref = refs[2 * CMB_WIN + 1:]
    l = pl.program_id(0)
    flags = flag_ref[l]

    @pl.when((flags & FLAG_FIRST) != 0)
    def _():
        acc_ref[...] = jnp.zeros_like(acc_ref)

    for w in range(CMB_WIN):
        @pl.when(((flags & FLAG_ACTIVE) != 0) & (w < nv_ref[l]))
        def _():
            onehot = _one_hot_rows(pos_ref[0], tile_refs[w][l])
            y = y_refs[w][...]
            y_hi = y.astype(BF16)
            y_lo = (y - y_hi.astype(F32)).astype(BF16)
            acc_ref[...] += (lax.dot_general(onehot, y_hi, TN_DIMS, preferred_element_type=F32)
                             + lax.dot_general(onehot, y_lo, TN_DIMS, preferred_element_type=F32))

    @pl.when((flags & FLAG_LAST) != 0)
    def _():
        o_ref[0] = x_ref[0] + gate_ref[0] * acc_ref[...]


def moe_combine(x, y_sorted, pos, schedule, gate):
    n_steps, by_chunk = schedule
    per_b = SEQ // MOE_CHUNK
    tok = lambda l, c, *_: (c[l] // per_b, c[l] % per_b, 0)
    tile = lambda w: (lambda l, c, n_, f, *tiles: (tiles[w][l], 0))
    return pl.pallas_call(
        _moe_combine_kernel,
        grid_spec=pltpu.PrefetchScalarGridSpec(
            num_scalar_prefetch=3 + CMB_WIN,
            grid=(n_steps,),
            in_specs=[pl.BlockSpec((1, 1, MOE_CHUNK), lambda l, c, *_: (c[l], 0, 0))]
            + [pl.BlockSpec((MOE_TM, D_MODEL), tile(w)) for w in range(CMB_WIN)]
            + [pl.BlockSpec((1, MOE_CHUNK, D_MODEL), tok),
               pl.BlockSpec((1, 1, D_MODEL), lambda l, c, *_: (c[l] // per_b, 0, 0))],
            out_specs=pl.BlockSpec((1, MOE_CHUNK, D_MODEL), tok),
            scratch_shapes=[pltpu.VMEM((MOE_CHUNK, D_MODEL), F32)],
        ),
        out_shape=jax.ShapeDtypeStruct((BATCH, SEQ, D_MODEL), F32),
        compiler_params=_cparams(("arbitrary",)),
        name="moe_combine",
    )(*by_chunk, pos, *([y_sorted] * CMB_WIN), x, gate.reshape(BATCH, 1, D_MODEL))


def moe_layer(x, hx, route, gate, w_gate, w_up, w_down, layer):
    pos, by_tile, by_chunk = moe_plan(route)
    y_sorted = moe_experts(hx, pos, by_tile, w_gate, w_up, w_down, layer)
    return moe_combine(x, y_sorted, pos, by_chunk, gate)


def kernel(x, c, positions, w_ada, b_ada, norm_g, w_in_ab, w_out_ab, gla_w_gate2, gla_b_gate, gla_norm_g, nsa_q_gain, nsa_k_gain, nsa_cmp_pe, nsa_cmp_w1, nsa_cmp_w2, w_in_c, sgu_norm_g, sgu_w_s, sgu_b_s, w_out_c, w_router, router_bias, w_gate, w_up, w_down):
    mod = ada_modulation(c, w_ada, b_ada)
    qk, gv, gr, nq, nkv, misc = inproj0(x, norm_g[0, 0], mod[0, :, 0], mod[0, :, 1], _arrange_w_in(w_in_ab[0]))
    o_a = gla_mixer(qk, gv, gr, misc, gla_w_gate2[0], gla_b_gate[0], gla_norm_g[0])
    o_b = nsa_mixer(nq, nkv, misc, positions, nsa_q_gain[0], nsa_k_gain[0], nsa_cmp_pe[0], nsa_cmp_w1[0], nsa_cmp_w2[0])
    wg, wu, wd = w_gate.astype(BF16), w_up.astype(BF16), w_down.astype(BF16)
    route_args = lambda l: _route_args(norm_g[l, 1], mod[l, :, 3], mod[l, :, 4], w_router, router_bias)
    x1, h, route = outproj0(o_a, o_b, w_out_ab[0], x, mod[0, :, 2], route_args(0))
    x2 = moe_layer(x1, h, route, mod[0, :, 5], wg, wu, wd, 0)
    x3, h, route = gmlp_layer(x2, norm_g[1, 0], mod[1, :, 0], mod[1, :, 1], w_in_c[0], sgu_norm_g[0], sgu_w_s[0],
                              sgu_b_s[0], w_out_c[0], mod[1, :, 2], route_args(1))
    return moe_layer(x3, h, route, mod[1, :, 5], wg, wu, wd, 1)
```

```python
import functools

import numpy as np
import jax
import jax.numpy as jnp
from jax import lax
from jax.experimental import pallas as pl
from jax.experimental.pallas import tpu as pltpu

D_MODEL = 1024
BATCH = 2
SEQ = 8192
DEPTH = 2
N_TOK = BATCH * SEQ

GLA_HEADS = 4
GLA_DK = 64
GLA_DV = 128
GLA_GATE_RANK = 16
GLA_TAU = 16.0
GLA_CHUNK = 64
NSA_HEADS = 8
NSA_KV_GROUPS = 2
NSA_HPG = NSA_HEADS // NSA_KV_GROUPS
NSA_DH = 64
CMP_LEN = 32
CMP_STRIDE = 16
CMP_HIDDEN = 256
SEL_BLOCK = 64
SEL_TOPK = 16
WINDOW = 512
ROPE_THETA = 500000.0
ROT_DIM = NSA_DH // 4
ROT_HALF = ROT_DIM // 2
SGU_CHUNK = 128
SGU_GROUPS = 8
SGU_WIDTH = 2048
SGU_GROUP_DIM = SGU_WIDTH // SGU_GROUPS
N_EXPERTS = 16
N_EXPERT_GROUPS = 4
EXPERTS_PER_GROUP = N_EXPERTS // N_EXPERT_GROUPS
EXPERT_HIDDEN = 512

GLA_QK_W = GLA_HEADS * GLA_DK
GLA_V_W = GLA_HEADS * GLA_DV
NSA_Q_W = NSA_HEADS * NSA_DH
NSA_KV_W = NSA_KV_GROUPS * NSA_DH
N_CMP_PAD = SEQ // CMP_STRIDE
N_SEL = SEQ // SEL_BLOCK

NORM_EPS = 1e-6
NEG_INF = -1e30
FORCE_BONUS = 1e4

LANES = 128
MIB = 1024 * 1024

F32 = jnp.float32
BF16 = jnp.bfloat16
HI = lax.Precision.HIGHEST
NT_DIMS = (((1,), (1,)), ((), ()))
TN_DIMS = (((0,), (0,)), ((), ()))


def _cparams(sem, vmem_mib=48):
    return pltpu.CompilerParams(dimension_semantics=sem, vmem_limit_bytes=vmem_mib * MIB)


def _rms_mod(x, g, shift, scale):
    y = x * lax.rsqrt(jnp.mean(x * x, axis=-1, keepdims=True) + NORM_EPS) * g
    return y * (1 + scale) + shift


def _silu(x):
    return x * jax.nn.sigmoid(x)


def _log_sigmoid(z):
    return jnp.minimum(z, 0.0) - jnp.log1p(jnp.exp(-jnp.abs(z)))


ADA_TN = 1536
ADA_ROWS = 8


def _ada_kernel(c_ref, w_ref, b_ref, o_ref):
    cond = _silu(c_ref[...])
    o_ref[0] = jnp.dot(cond, w_ref[0], precision=HI, preferred_element_type=F32) + b_ref[0]


def ada_modulation(c, w_ada, b_ada):
    c8 = jnp.zeros((ADA_ROWS, D_MODEL), F32).at[:BATCH].set(c)
    width = 6 * D_MODEL
    out = pl.pallas_call(
        _ada_kernel,
        grid=(DEPTH, width // ADA_TN),
        in_specs=[
            pl.BlockSpec((ADA_ROWS, D_MODEL), lambda l, j: (0, 0)),
            pl.BlockSpec((1, D_MODEL, ADA_TN), lambda l, j: (l, 0, j)),
            pl.BlockSpec((1, 1, ADA_TN), lambda l, j: (l, 0, j)),
        ],
        out_specs=pl.BlockSpec((1, ADA_ROWS, ADA_TN), lambda l, j: (l, 0, j)),
        out_shape=jax.ShapeDtypeStruct((DEPTH, ADA_ROWS, width), F32),
        compiler_params=_cparams(("arbitrary", "arbitrary")),
        name="ada_modulation",
    )(c8, w_ada, b_ada.reshape(DEPTH, 1, width))
    return out[:, :BATCH].reshape(DEPTH, BATCH, 6, D_MODEL)


INPROJ_TM = 512
INPROJ_WIDTHS = (2 * GLA_QK_W, GLA_V_W, GLA_V_W, NSA_Q_W, 6 * NSA_KV_W, LANES)


def _arrange_w_in(w_in):
    o = np.cumsum((0, GLA_QK_W, GLA_QK_W, GLA_V_W, GLA_GATE_RANK, GLA_V_W, NSA_Q_W, 6 * NSA_KV_W, NSA_HEADS * 3))
    gq_gk = w_in[:, o[0]:o[2]]
    gv = w_in[:, o[2]:o[3]]
    glr = w_in[:, o[3]:o[4]]
    gr = w_in[:, o[4]:o[5]]
    nq = w_in[:, o[5]:o[6]]
    nkv = w_in[:, o[6]:o[7]]
    ng = w_in[:, o[7]:o[8]]
    pad = jnp.zeros((D_MODEL, LANES - GLA_GATE_RANK - NSA_HEADS * 3), w_in.dtype)
    return jnp.concatenate([gq_gk, gv, gr, nq, nkv, glr, ng, pad], axis=1).astype(BF16)


def _inproj0_kernel(x_ref, g_ref, sh_ref, sc_ref, w_ref, *o_refs):
    h = _rms_mod(x_ref[0], g_ref[...], sh_ref[0], sc_ref[0]).astype(BF16)
    off = 0
    for o_ref, wd in zip(o_refs, INPROJ_WIDTHS):
        o_ref[0] = jnp.dot(h, w_ref[:, off:off + wd], preferred_element_type=F32)
        off += wd


def inproj0(x, g, shift, scale, w_arranged):
    tm = INPROJ_TM
    wtot = sum(INPROJ_WIDTHS)
    row = lambda b, i: (b, i, 0)
    vec = lambda b, i: (b, 0, 0)
    return pl.pallas_call(
        _inproj0_kernel,
        grid=(BATCH, SEQ // tm),
        in_specs=[
            pl.BlockSpec((1, tm, D_MODEL), row),
            pl.BlockSpec((1, D_MODEL), lambda b, i: (0, 0)),
            pl.BlockSpec((1, 1, D_MODEL), vec),
            pl.BlockSpec((1, 1, D_MODEL), vec),
            pl.BlockSpec((D_MODEL, wtot), lambda b, i: (0, 0)),
        ],
        out_specs=[pl.BlockSpec((1, tm, wd), row) for wd in INPROJ_WIDTHS],
        out_shape=[jax.ShapeDtypeStruct((BATCH, SEQ, wd), F32) for wd in INPROJ_WIDTHS],
        compiler_params=_cparams(("arbitrary", "arbitrary")),
        name="inproj0",
    )(x, g.reshape(1, D_MODEL), shift.reshape(BATCH, 1, D_MODEL), scale.reshape(BATCH, 1, D_MODEL), w_arranged)


GLA_TG = 512


def _gla_chunk_sums():
    i = np.arange(GLA_TG)[:, None]
    j = np.arange(GLA_TG)[None, :]
    same = (i // GLA_CHUNK) == (j // GLA_CHUNK)
    m3 = np.concatenate([same & (j <= i), same & (j % GLA_CHUNK <= GLA_CHUNK // 2), same], axis=0).astype(np.float32)
    return jnp.asarray(np.concatenate([m3, m3], axis=1), BF16)


def _gla_kernel(qk_ref, v_ref, r_ref, misc_ref, w2_ref, bg_ref, og_ref, sums_ref, o_ref, st_ref):
    C, tg = GLA_CHUNK, GLA_TG

    @pl.when(pl.program_id(0) == 0)
    def _():
        st_ref[...] = jnp.zeros_like(st_ref)

    lane = lax.broadcasted_iota(jnp.int32, (1, GLA_QK_W), 1)
    heads = [(lane >= h * GLA_DK) & (lane < (h + 1) * GLA_DK) for h in range(GLA_HEADS)]
    stack = lambda per_head, rows: jnp.concatenate([t[rows] for t in per_head], axis=0)
    stacked_row = lax.broadcasted_iota(jnp.int32, (GLA_HEADS * C, C), 0)
    causal = (stacked_row & (C - 1)) >= lax.broadcasted_iota(jnp.int32, (GLA_HEADS * C, C), 1)
    og = og_ref[...]

    def prepare(b):
        z = jnp.dot(misc_ref[b], w2_ref[...], precision=HI, preferred_element_type=F32) + bg_ref[...]
        la = _log_sigmoid(z) / GLA_TAU
        la_hi = la.astype(BF16)
        la_lo = (la - la_hi.astype(F32)).astype(BF16)
        sums = jnp.dot(sums_ref[...], jnp.concatenate([la_hi, la_lo], axis=0), preferred_element_type=F32)
        bc, b_mid, b_last = sums[0:tg], sums[tg:2 * tg], sums[2 * tg:3 * tg]
        q = qk_ref[b, :, 0:GLA_QK_W] * (GLA_DK ** -0.5)
        k = qk_ref[b, :, GLA_QK_W:2 * GLA_QK_W]
        qd = q * jnp.exp(bc - b_mid)
        kl = k * jnp.exp(b_last - bc)
        qb = q * jnp.exp(bc)
        per_head = lambda t: [jnp.where(m, t, 0.0).astype(BF16) for m in heads]
        return dict(kd=(k * jnp.exp(b_mid - bc)).astype(BF16), dec=jnp.exp(b_last), qd_h=per_head(qd),
                    qb_h=per_head(qb), kl_h=per_head(kl))

    batches = range(BATCH)
    pre = [prepare(b) for b in batches]
    st = [st_ref[b] for b in batches]
    for c in range(tg // C):
        rows = slice(c * C, (c + 1) * C)
        for b in batches:
            p = pre[b]
            v = v_ref[b, rows, :].astype(BF16)
            s = lax.dot_general(stack(p["qd_h"], rows), p["kd"][rows], NT_DIMS, preferred_element_type=F32)
            s = jnp.where(causal, s, 0.0).astype(BF16)
            o_intra = jnp.dot(s, v, preferred_element_type=F32)
            o_inter = lax.dot_general(stack(p["qb_h"], rows), st[b].astype(BF16), NT_DIMS, preferred_element_type=F32)
            v_stack = jnp.concatenate([v[:, h * GLA_DV:(h + 1) * GLA_DV] for h in range(GLA_HEADS)], axis=0)
            st[b] = st[b] * p["dec"][c * C:c * C + 1] + lax.dot_general(v_stack, stack(p["kl_h"], rows), TN_DIMS,
                                                                       preferred_element_type=F32)
            for h in range(GLA_HEADS):
                hrows = slice(h * C, (h + 1) * C)
                vcols = slice(h * GLA_DV, (h + 1) * GLA_DV)
                o = o_intra[hrows, vcols] + o_inter[hrows]
                on = o * lax.rsqrt(jnp.mean(o * o, axis=-1, keepdims=True) + NORM_EPS) * og
                o_ref[b, rows, vcols] = on * _silu(r_ref[b, rows, vcols])
    for b in batches:
        st_ref[b] = st[b]


def gla_mixer(qk, v, r, misc, w_gate2, b_gate, out_g):
    tg = GLA_TG
    w2 = jnp.zeros((LANES, GLA_QK_W), F32).at[:GLA_GATE_RANK].set(w_gate2)
    row = lambda i: (0, i, 0)
    const = lambda i: (0, 0)
    return pl.pallas_call(
        _gla_kernel,
        grid=(SEQ // tg,),
        in_specs=[
            pl.BlockSpec((BATCH, tg, 2 * GLA_QK_W), row),
            pl.BlockSpec((BATCH, tg, GLA_V_W), row),
            pl.BlockSpec((BATCH, tg, GLA_V_W), row),
            pl.BlockSpec((BATCH, tg, LANES), row),
            pl.BlockSpec((LANES, GLA_QK_W), const),
            pl.BlockSpec((1, GLA_QK_W), const),
            pl.BlockSpec((1, GLA_DV), const),
            pl.BlockSpec((3 * tg, 2 * tg), const),
        ],
        out_specs=pl.BlockSpec((BATCH, tg, GLA_V_W), row),
        out_shape=jax.ShapeDtypeStruct((BATCH, SEQ, GLA_V_W), F32),
        scratch_shapes=[pltpu.VMEM((BATCH, GLA_DV, GLA_QK_W), F32)],
        compiler_params=_cparams(("arbitrary",)),
        name="gla_mixer",
    )(qk, v, r, misc, w2, b_gate.reshape(1, GLA_QK_W), out_g.reshape(1, GLA_DV), _gla_chunk_sums())


POS_SIDE = 128


def _rope_table_kernel(freq_ref, pos_ref, cos_ref, sin_ref):
    pos = pos_ref[...].astype(F32)
    for f in range(ROT_HALF):
        ang = pos * freq_ref[f]
        cos_ref[f] = jnp.cos(ang)
        sin_ref[f] = jnp.sin(ang)


def rope_tables(positions):
    inv_freq = jnp.float32(ROPE_THETA) ** (-jnp.arange(ROT_HALF, dtype=F32) / ROT_HALF)
    shp = jax.ShapeDtypeStruct((ROT_HALF, POS_SIDE, POS_SIDE), F32)
    cos, sin = pl.pallas_call(
        _rope_table_kernel,
        in_specs=[pl.BlockSpec(memory_space=pltpu.SMEM), pl.BlockSpec(memory_space=pltpu.VMEM)],
        out_specs=[pl.BlockSpec(memory_space=pltpu.VMEM)] * 2,
        out_shape=[shp, shp],
        name="rope_tables",
    )(inv_freq, positions.reshape(POS_SIDE, POS_SIDE))
    return jnp.concatenate([cos, sin], axis=0).reshape(ROT_DIM, N_TOK).T.reshape(BATCH, SEQ, ROT_DIM)


def _rope_placement():
    place = np.zeros((ROT_DIM, 3 * LANES), np.float32)
    const = np.zeros((1, 3 * LANES), np.float32)
    for lane in range(LANES):
        i = lane % NSA_DH
        if i < ROT_HALF:
            place[i, lane] = 1.0
            place[ROT_HALF + i, LANES + lane] = -1.0
        elif i < ROT_DIM:
            place[i - ROT_HALF, lane] = 1.0
            place[i, 2 * LANES + lane] = 1.0
        else:
            const[0, lane] = 1.0
    return jnp.asarray(place), jnp.asarray(const)


def _lane_tables(cs, place_ref, const_ref):
    tab = jnp.dot(cs, place_ref[...], precision=HI, preferred_element_type=F32) + const_ref[...]
    return tab[:, 0:LANES], tab[:, LANES:2 * LANES], tab[:, 2 * LANES:3 * LANES]


def _block_diag_ones2(width):
    h = np.arange(width) // NSA_DH
    bd = (h[:, None] == h[None, :]).astype(np.float32)
    return jnp.asarray(np.concatenate([bd, bd], axis=0), BF16)


def _head_norm_rope(x, gain, bd2, c, sm, sp):
    width = x.shape[-1]
    reps = width // LANES
    sq = x * x
    sq_hi = sq.astype(BF16)
    sq_lo = (sq - sq_hi.astype(F32)).astype(BF16)
    ss = jnp.dot(jnp.concatenate([sq_hi, sq_lo], axis=1), bd2, preferred_element_type=F32)
    y = x * lax.rsqrt(ss * (1.0 / NSA_DH) + NORM_EPS) * gain
    tile = lambda t: jnp.concatenate([t] * reps, axis=1) if reps > 1 else t
    return (y * tile(c) + pltpu.roll(y, width - ROT_HALF, 1) * tile(sm) + pltpu.roll(y, ROT_HALF, 1) * tile(sp))


PREP_TM = 512


def _prep_kernel(q_ref, kv_ref, cs_ref, place_ref, const_ref, gq_ref, gk_ref, bdq_ref, bdk_ref,
                 qo_ref, kso_ref, kwo_ref, vso_ref, vwo_ref, xk_ref, xv_ref, tok_ref):
    tm = PREP_TM
    seg_lane = lax.broadcasted_iota(jnp.int32, (tm // CMP_STRIDE, LANES), 1)
    for n, out_ref in ((0, xk_ref), (1, xv_ref)):
        tok_ref[...] = kv_ref[0, :, n * NSA_KV_W:(n + 1) * NSA_KV_W]
        for pair in range(CMP_STRIDE // 2):
            a = tok_ref[pl.ds(2 * pair, tm // CMP_STRIDE, stride=CMP_STRIDE), :]
            b = tok_ref[pl.ds(2 * pair + 1, tm // CMP_STRIDE, stride=CMP_STRIDE), :]
            piece = slice(pair * LANES, (pair + 1) * LANES)
            out_ref[0, 0, :, piece] = jnp.where(seg_lane < NSA_DH, a, pltpu.roll(b, NSA_DH, 1))
            out_ref[0, 1, :, piece] = jnp.where(seg_lane < NSA_DH, pltpu.roll(a, NSA_DH, 1), b)
    c, sm, sp = _lane_tables(cs_ref[0], place_ref, const_ref)
    bdk = bdk_ref[...]
    q = _head_norm_rope(q_ref[0], gq_ref[...], bdq_ref[...], c, sm, sp) * (NSA_DH ** -0.5)
    qo_ref[0] = q.T.reshape(NSA_HEADS, NSA_DH, tm)
    kv_cols = lambda n: kv_ref[0, :, n * NSA_KV_W:(n + 1) * NSA_KV_W]
    ks = _head_norm_rope(kv_cols(2), gk_ref[0:1, :], bdk, c, sm, sp)
    kw = _head_norm_rope(kv_cols(4), gk_ref[1:2, :], bdk, c, sm, sp)
    lane = lax.broadcasted_iota(jnp.int32, (tm, LANES), 1)
    token = pl.program_id(1) * tm + lax.broadcasted_iota(jnp.int32, (tm, LANES), 0)
    block_col = NSA_DH + jnp.right_shift(token & (SA_TK - 1), 6)
    onehot = jnp.where(lane == block_col, 1.0, 0.0)
    for g in range(NSA_KV_GROUPS):
        to_front = lambda t: t if g == 0 else pltpu.roll(t, NSA_DH, 1)
        kso_ref[0, g] = jnp.where(lane < NSA_DH, to_front(ks), onehot).astype(BF16)
        kwo_ref[0, g] = jnp.where(lane < NSA_DH, to_front(kw), 0.0).astype(BF16)
    tail = jnp.where(lax.broadcasted_iota(jnp.int32, (VT_ROWS - NSA_DH, tm), 0) == 0, 1.0, 0.0)
    for n, out_ref in ((3, vso_ref), (5, vwo_ref)):
        v_t = kv_cols(n).T
        for g in range(NSA_KV_GROUPS):
            out_ref[0, g] = jnp.concatenate([v_t[g * NSA_DH:(g + 1) * NSA_DH], tail], axis=0).astype(BF16)


def nsa_prep(nq, nkv, cs, q_gain, k_gain):
    tm = PREP_TM
    G = NSA_KV_GROUPS
    row = lambda b, i: (b, i, 0)
    const = lambda b, i: (0, 0)
    gq = jnp.tile(q_gain, NSA_HEADS).reshape(1, NSA_Q_W)
    gk = jnp.stack([jnp.tile(k_gain[1], NSA_KV_GROUPS), jnp.tile(k_gain[2], NSA_KV_GROUPS)])
    kslab = pl.BlockSpec((1, G, tm, LANES), lambda b, i: (b, 0, i, 0))
    vslab = pl.BlockSpec((1, G, VT_ROWS, tm), lambda b, i: (b, 0, 0, i))
    segs = pl.BlockSpec((1, G, tm // CMP_STRIDE, CMP_STRIDE * NSA_DH), lambda b, i: (b, 0, i, 0))
    return pl.pallas_call(
        _prep_kernel,
        grid=(BATCH, SEQ // tm),
        in_specs=[
            pl.BlockSpec((1, tm, NSA_Q_W), row),
            pl.BlockSpec((1, tm, 6 * NSA_KV_W), row),
            pl.BlockSpec((1, tm, ROT_DIM), row),
            pl.BlockSpec((ROT_DIM, 3 * LANES), const),
            pl.BlockSpec((1, 3 * LANES), const),
            pl.BlockSpec((1, NSA_Q_W), const),
            pl.BlockSpec((2, NSA_KV_W), const),
            pl.BlockSpec((2 * NSA_Q_W, NSA_Q_W), const),
            pl.BlockSpec((2 * NSA_KV_W, NSA_KV_W), const),
        ],
        out_specs=[pl.BlockSpec((1, NSA_HEADS, NSA_DH, tm), lambda b, i: (b, 0, 0, i)), kslab, kslab, vslab, vslab,
                   segs, segs],
        out_shape=[jax.ShapeDtypeStruct((BATCH, NSA_HEADS, NSA_DH, SEQ), F32),
                   jax.ShapeDtypeStruct((BATCH, G, SEQ, LANES), BF16), jax.ShapeDtypeStruct((BATCH, G, SEQ, LANES), BF16),
                   jax.ShapeDtypeStruct((BATCH, G, VT_ROWS, SEQ), BF16), jax.ShapeDtypeStruct((BATCH, G, VT_ROWS, SEQ), BF16),
                   jax.ShapeDtypeStruct((BATCH, G, N_CMP_PAD, CMP_STRIDE * NSA_DH), F32),
                   jax.ShapeDtypeStruct((BATCH, G, N_CMP_PAD, CMP_STRIDE * NSA_DH), F32)],
        scratch_shapes=[pltpu.VMEM((tm, LANES), F32)],
        compiler_params=_cparams(("arbitrary", "arbitrary")),
        name="nsa_prep",
    )(nq, nkv, cs, *_rope_placement(), gq, gk, _block_diag_ones2(NSA_Q_W), _block_diag_ones2(NSA_KV_W))


SEG_W = CMP_STRIDE * NSA_DH


def _cmp_kernel(xk_ref, xv_ref, pe_ref, w1_ref, w2_ref, gain_ref, cs_ref, place_ref, const_ref, bd_ref, ko_ref, vo_ref):
    def compress(x_ref, kv):
        out = jnp.zeros((N_CMP_PAD, LANES), F32)
        for g in range(NSA_KV_GROUPS):
            x = x_ref[0, g]
            ha = jnp.dot(split3_keys(x + pe_ref[kv, 0]), w1_ref[kv, 0], preferred_element_type=F32)
            hb = jnp.dot(split3_keys(x + pe_ref[kv, 1]), w1_ref[kv, 1], preferred_element_type=F32)
            hid = ha + pltpu.roll(hb, N_CMP_PAD - 1, 0)
            out += jnp.dot(jax.nn.gelu(hid), w2_ref[kv, g], precision=HI, preferred_element_type=F32)
        return out

    c, sm, sp = _lane_tables(cs_ref[0], place_ref, const_ref)
    ko_ref[0] = _head_norm_rope(compress(xk_ref, 0), gain_ref[...], bd_ref[...], c, sm, sp)
    vo_ref[0] = compress(xv_ref, 1)


def nsa_compress(xk, xv, cmp_pe, cmp_w1, cmp_w2, k_gain0, cs_last):
    pe = cmp_pe.reshape(2, 2, 1, SEG_W)
    w1 = cmp_w1.reshape(2, 2, SEG_W, CMP_HIDDEN)
    w1_hi = w1.astype(BF16)
    w1 = jnp.concatenate([w1_hi, w1_hi, (w1 - w1_hi.astype(F32)).astype(BF16)], axis=2)
    w2 = jnp.zeros((2, NSA_KV_GROUPS, CMP_HIDDEN, LANES), F32)
    for g in range(NSA_KV_GROUPS):
        w2 = w2.at[:, g, :, g * NSA_DH:(g + 1) * NSA_DH].set(cmp_w2)
    seg = pl.BlockSpec((1, NSA_KV_GROUPS, N_CMP_PAD, SEG_W), lambda b: (b, 0, 0, 0))
    tab = pl.BlockSpec((1, N_CMP_PAD, LANES), lambda b: (b, 0, 0))
    full = lambda shape: pl.BlockSpec(shape, lambda b: (0,) * len(shape))
    return pl.pallas_call(
        _cmp_kernel,
        grid=(BATCH,),
        in_specs=[seg, seg, full((2, 2, 1, SEG_W)), full((2, 2, 3 * SEG_W, CMP_HIDDEN)),
                  full((2, NSA_KV_GROUPS, CMP_HIDDEN, LANES)), full((1, LANES)),
                  pl.BlockSpec((1, N_CMP_PAD, ROT_DIM), lambda b: (b, 0, 0)), full((ROT_DIM, 3 * LANES)),
                  full((1, 3 * LANES)), full((2 * LANES, LANES))],
        out_specs=[tab, tab],
        out_shape=[jax.ShapeDtypeStruct((BATCH, N_CMP_PAD, LANES), F32)] * 2,
        compiler_params=_cparams(("arbitrary",)),
        name="nsa_compress",
    )(xk, xv, pe, w1, w2, jnp.tile(k_gain0, NSA_KV_GROUPS).reshape(1, LANES), cs_last, *_rope_placement(),
      _block_diag_ones2(LANES))


CA_TQ = 512
CA_COLS = NSA_HPG * CA_TQ
CMP_PER_SEL = SEL_BLOCK // CMP_STRIDE
TOPK_BANDS = 4


def split3_keys(k):
    hi = k.astype(BF16)
    lo = (k - hi.astype(F32)).astype(BF16)
    return jnp.concatenate([hi, lo, hi], axis=-1)


def _top_k_rows(score, k):
    rows, cols = score.shape
    row = lax.broadcasted_iota(jnp.int32, (rows, cols), 0).astype(F32)
    taken = jnp.zeros((rows, cols), F32)
    left = score
    for _ in range(k):
        top = jnp.max(left, axis=0, keepdims=True)
        first = jnp.min(jnp.where(left == top, row, float(rows)), axis=0, keepdims=True)
        hit = row == first
        taken = jnp.where(hit, 1.0, taken)
        left = jnp.where(hit, -jnp.inf, left)
    return taken


def _cattn_kernel(q_ref, kc_ref, vct_ref, gl_ref, o_ref, sel_ref, q3_ref, ps_ref):
    tq = CA_TQ
    q0 = pl.program_id(1) * tq
    lanes4 = lambda t: jnp.concatenate([t] * NSA_HPG, axis=1)
    cend = lax.broadcasted_iota(jnp.int32, (N_CMP_PAD, tq), 0) * CMP_STRIDE + (CMP_LEN - 1)
    tc = q0 + lax.broadcasted_iota(jnp.int32, (N_CMP_PAD, tq), 1)
    cmask = lanes4(cend <= tc)
    jj = lax.broadcasted_iota(jnp.int32, (N_SEL, tq), 0)
    tt = q0 + lax.broadcasted_iota(jnp.int32, (N_SEL, tq), 1)
    cur = jnp.right_shift(tt, 6)
    forced = (jj == 0) | (jj == cur) | (jj == cur - 1)
    valid = jj * SEL_BLOCK <= tt

    for g in range(NSA_KV_GROUPS):
        heads = range(g * NSA_HPG, (g + 1) * NSA_HPG)
        for n, h in enumerate(heads):
            q = q_ref[0, h]
            hi = q.astype(BF16)
            lo = (q - hi.astype(F32)).astype(BF16)
            for t, part in enumerate((hi, hi, lo)):
                q3_ref[g, t * NSA_DH:(t + 1) * NSA_DH, n * tq:(n + 1) * tq] = part
        s = jnp.dot(kc_ref[0, g], q3_ref[g], preferred_element_type=F32)
        s = jnp.where(cmask, s, NEG_INF)
        m = jnp.max(s, axis=0, keepdims=True)
        e = jnp.where(cmask, jnp.exp(s - m), 0.0)
        l = jnp.sum(e, axis=0, keepdims=True)
        p = e / jnp.where(l > 0.0, l, 1.0)
        gate = jnp.concatenate([jax.nn.sigmoid(gl_ref[0, h, 0:1, :]) for h in heads], axis=1)
        o = jnp.dot(vct_ref[0, g], p.astype(BF16), preferred_element_type=F32) * gate
        for n, h in enumerate(heads):
            o_ref[0, h] = o[:, n * tq:(n + 1) * tq]
        psum = functools.reduce(jnp.add, [p[:, n * tq:(n + 1) * tq] for n in range(NSA_HPG)])
        for n in range(tq // LANES):
            ps_ref[g, n] = psum[:, n * LANES:(n + 1) * LANES]

        every4th = lambda r: jnp.concatenate(
            [ps_ref[g, n, pl.ds(r, N_SEL, stride=CMP_PER_SEL), :] for n in range(tq // LANES)], axis=1)
        starts_in = [every4th(r) for r in range(CMP_PER_SEL)]
        from_prev = jnp.where(jj >= 1, pltpu.roll(starts_in[CMP_PER_SEL - 1], 1, 0), 0.0)
        imp = functools.reduce(jnp.add, starts_in) + from_prev
        score = jnp.where(valid, imp + jnp.where(forced, FORCE_BONUS, 0.0), NEG_INF)
        step = pl.program_id(1)
        steps_per_band = (SEQ // tq) // TOPK_BANDS
        for band in range(TOPK_BANDS):
            n_rows = (band + 1) * (N_SEL // TOPK_BANDS)

            @pl.when((step >= band * steps_per_band) & (step < (band + 1) * steps_per_band))
            def _():
                taken = _top_k_rows(score[0:n_rows], SEL_TOPK)
                sel_ref[0, g, 0:n_rows, :] = jnp.where(valid[0:n_rows], taken, 0.0)
                if n_rows < N_SEL:
                    sel_ref[0, g, n_rows:N_SEL, :] = jnp.zeros((N_SEL - n_rows, tq), F32)


def nsa_cmp_attn(q_t, kcmp, vcmp_t, gl_t):
    tq = CA_TQ
    G = NSA_KV_GROUPS
    return pl.pallas_call(
        _cattn_kernel,
        grid=(BATCH, SEQ // tq),
        in_specs=[
            pl.BlockSpec((1, NSA_HEADS, NSA_DH, tq), lambda b, i: (b, 0, 0, i)),
            pl.BlockSpec((1, G, N_CMP_PAD, 3 * NSA_DH), lambda b, i: (b, 0, 0, 0)),
            pl.BlockSpec((1, G, NSA_DH, N_CMP_PAD), lambda b, i: (b, 0, 0, 0)),
            pl.BlockSpec((1, NSA_HEADS, 3, tq), lambda b, i: (b, 0, 0, i)),
        ],
        out_specs=[pl.BlockSpec((1, NSA_HEADS, NSA_DH, tq), lambda b, i: (b, 0, 0, i)),
                   pl.BlockSpec((1, G, N_SEL, tq), lambda b, i: (b, 0, 0, i))],
        out_shape=[jax.ShapeDtypeStruct((BATCH, NSA_HEADS, NSA_DH, SEQ), F32),
                   jax.ShapeDtypeStruct((BATCH, G, N_SEL, SEQ), F32)],
        scratch_shapes=[pltpu.VMEM((G, 3 * NSA_DH, CA_COLS), BF16), pltpu.VMEM((G, tq // LANES, N_CMP_PAD, LANES), F32)],
        compiler_params=_cparams(("arbitrary", "arbitrary")),
        name="nsa_cmp_attn",
    )(q_t, split3_keys(kcmp), vcmp_t.astype(BF16), gl_t)


SA_TQ = 256
SA_TK = 1024
SA_PARTS = 2
SA_PART = SA_TK // SA_PARTS
M_INIT = -1e20


SA_COLS = NSA_HPG * SA_TQ
SA_BLOCKS = SA_TK // SEL_BLOCK


VT_ROWS = NSA_DH + 16


def _sattn_kernel(q_ref, k_ref, vt_ref, sel_ref, gl_ref, kw_ref, vw_ref, prev_ref, o_ref, qa_ref, acc_ref, s_ref,
                  m_ref):
    tq, tk = SA_TQ, SA_TK
    i = pl.program_id(1)
    groups = range(NSA_KV_GROUPS)
    slots = range(2)
    for g in groups:
        for h in range(NSA_HPG):
            q = q_ref[0, g * NSA_HPG + h].astype(BF16)
            for slot in slots:
                qa_ref[slot, g, 0:NSA_DH, h * tq:(h + 1) * tq] = q
        for slot in slots:
            qa_ref[slot, g, NSA_DH:LANES, :] = jnp.zeros((LANES - NSA_DH, SA_COLS), BF16)
    acc_ref[...] = jnp.zeros_like(acc_ref)
    lanes4 = lambda t: jnp.concatenate([t] * NSA_HPG, axis=1)
    part_keys = lambda kt, part: pl.ds(pl.multiple_of(kt * tk + part * SA_PART, SA_PART), SA_PART)

    def scores(kt, slot):
        for g in groups:
            selrows = sel_ref[0, g, pl.ds(pl.multiple_of(kt * SA_BLOCKS, SA_BLOCKS), SA_BLOCKS), :]
            qa_ref[slot, g, NSA_DH:NSA_DH + SA_BLOCKS, :] = lanes4(jnp.where(selrows > 0.5, 0.0, NEG_INF)).astype(BF16)
            for part in range(SA_PARTS):
                s = jnp.dot(k_ref[0, g, part_keys(kt, part), :], qa_ref[slot, g], preferred_element_type=F32)
                s_ref[slot, g, part] = s.astype(BF16)

    def absorb(kt, slot, ms):
        out = []
        for g in groups:
            ss = [s_ref[slot, g, part] for part in range(SA_PARTS)]
            m_tile = functools.reduce(jnp.maximum, [jnp.max(s, axis=0, keepdims=True) for s in ss])
            m_new = jnp.maximum(ms[g], m_tile.astype(F32))
            acc = jnp.exp(ms[g] - m_new) * acc_ref[g]
            for part in range(SA_PARTS):
                p = jnp.exp(ss[part] - m_new.astype(BF16))
                acc += jnp.dot(vt_ref[0, g, :, part_keys(kt, part)], p, preferred_element_type=F32)
            acc_ref[g] = acc
            out.append(m_new)
        return tuple(out)

    def two_tiles(j, ms):
        kt = 2 * j
        scores(kt + 1, 1)
        ms = absorb(kt, 0, ms)
        scores(kt + 2, 0)
        return absorb(kt + 1, 1, ms)

    n_full = (i * tq) // tk
    scores(0, 0)
    m0 = tuple(jnp.full((1, SA_COLS), M_INIT, F32) for _ in groups)
    ms = lax.fori_loop(0, n_full // 2, two_tiles, m0)
    for g in groups:
        m_ref[g] = ms[g]

    def last_tile(slot):
        start = i * tq - n_full * tk
        part, row0 = start // SA_PART, pl.multiple_of(start % SA_PART, tq)
        tri = lax.broadcasted_iota(jnp.int32, (tq, tq), 0) <= lax.broadcasted_iota(jnp.int32, (tq, tq), 1)
        bias = lanes4(jnp.where(tri, 0.0, NEG_INF)).astype(BF16)
        for g in groups:
            s_ref[slot, g, part, pl.ds(row0, tq), :] += bias
        for g, m in enumerate(absorb(n_full, slot, tuple(m_ref[g] for g in groups))):
            m_ref[g] = m

    @pl.when(n_full % 2 == 0)
    def _():
        last_tile(0)

    @pl.when(n_full % 2 == 1)
    def _():
        scores(n_full, 1)
        for g, m in enumerate(absorb(n_full - 1, 0, tuple(m_ref[g] for g in groups))):
            m_ref[g] = m
        last_tile(1)

    w_rows = WINDOW + tq
    w_start = jnp.maximum(i * tq - WINDOW, 0)
    w_keys = pl.ds(pl.multiple_of(w_start, tq), w_rows)
    behind = (i * tq + lax.broadcasted_iota(jnp.int32, (w_rows, tq), 1)
              - (w_start + lax.broadcasted_iota(jnp.int32, (w_rows, tq), 0)))
    w_bias = lanes4(jnp.where((behind >= 0) & (behind < WINDOW), 0.0, NEG_INF))
    for g in groups:
        heads = range(g * NSA_HPG, (g + 1) * NSA_HPG)
        gates = lambda branch: jnp.concatenate([jax.nn.sigmoid(gl_ref[0, h, branch:branch + 1, :]) for h in heads], axis=1)
        out = acc_ref[g, 0:NSA_DH, :] / acc_ref[g, NSA_DH:NSA_DH + 1, :] * gates(1)
        s = (jnp.dot(kw_ref[0, g, w_keys, :], qa_ref[0, g], preferred_element_type=F32) + w_bias).astype(BF16)
        p = jnp.exp(s - jnp.max(s, axis=0, keepdims=True))
        acc_w = jnp.dot(vw_ref[0, g, :, w_keys], p, preferred_element_type=F32)
        out += acc_w[0:NSA_DH] / acc_w[NSA_DH:NSA_DH + 1] * gates(2)
        for n, h in enumerate(heads):
            o_ref[0, h] = prev_ref[0, h] + out[:, n * tq:(n + 1) * tq]


def nsa_sel_win_attn(q_t, k_slab, vsel_t, sel_t, gl_t, kwin, vwin_t, prev):
    tq = SA_TQ
    G = NSA_KV_GROUPS
    ospec = pl.BlockSpec((1, NSA_HEADS, NSA_DH, tq), lambda b, i: (b, 0, 0, i))
    kspec = pl.BlockSpec((1, G, SEQ, LANES), lambda b, i: (b, 0, 0, 0))
    vspec = pl.BlockSpec((1, G, VT_ROWS, SEQ), lambda b, i: (b, 0, 0, 0))
    return pl.pallas_call(
        _sattn_kernel,
        grid=(BATCH, SEQ // tq),
        in_specs=[
            ospec, kspec, vspec,
            pl.BlockSpec((1, G, N_SEL, tq), lambda b, i: (b, 0, 0, i)),
            pl.BlockSpec((1, NSA_HEADS, 3, tq), lambda b, i: (b, 0, 0, i)),
            kspec, vspec, ospec,
        ],
        out_specs=ospec,
        out_shape=jax.ShapeDtypeStruct((BATCH, NSA_HEADS, NSA_DH, SEQ), F32),
        scratch_shapes=[pltpu.VMEM((2, G, LANES, SA_COLS), BF16), pltpu.VMEM((G, VT_ROWS, SA_COLS), F32),
                        pltpu.VMEM((2, G, SA_PARTS, SA_PART, SA_COLS), BF16), pltpu.VMEM((G, 1, SA_COLS), F32)],
        input_output_aliases={7: 0},
        compiler_params=_cparams(("arbitrary", "arbitrary"), vmem_mib=56),
        name="nsa_sel_win_attn",
    )(q_t, k_slab, vsel_t, sel_t, gl_t, kwin, vwin_t, prev)


def nsa_mixer(nq, nkv, misc, positions, q_gain, k_gain, cmp_pe, cmp_w1, cmp_w2):
    cs = rope_tables(positions)
    q_t, ksel, kwin, vsel_t, vwin_t, xk, xv = nsa_prep(nq, nkv, cs, q_gain, k_gain)
    last = jnp.minimum(jnp.arange(N_CMP_PAD) * CMP_STRIDE + CMP_LEN - 1, SEQ - 1)
    kcmp, vcmp = nsa_compress(xk, xv, cmp_pe, cmp_w1, cmp_w2, k_gain[0], cs[:, last])
    kcmp = kcmp.reshape(BATCH, N_CMP_PAD, NSA_KV_GROUPS, NSA_DH).transpose(0, 2, 1, 3)
    vcmp_t = vcmp.reshape(BATCH, N_CMP_PAD, NSA_KV_GROUPS, NSA_DH).transpose(0, 2, 3, 1)
    gl_t = misc[..., GLA_GATE_RANK:GLA_GATE_RANK + NSA_HEADS * 3].reshape(BATCH, SEQ, NSA_HEADS, 3).transpose(0, 2, 3, 1)
    o_t, sel_t = nsa_cmp_attn(q_t, kcmp, vcmp_t, gl_t)
    return nsa_sel_win_attn(q_t, ksel, vsel_t, sel_t, gl_t, kwin, vwin_t, o_t)


ROUTE_ROWS = 8
HX_TERMS = 3
HX_W = D_MODEL + LANES


def _top2_sum(a, b, c, d):
    hi1, lo1 = jnp.maximum(a, b), jnp.minimum(a, b)
    hi2, lo2 = jnp.maximum(c, d), jnp.minimum(c, d)
    return jnp.maximum(hi1, hi2) + jnp.maximum(jnp.minimum(hi1, hi2), jnp.maximum(lo1, lo2))


def _moe_prenorm_route(xn, g_ref, sh_ref, sc_ref, wr_ref, rb_ref, hx_ref, route_ref):
    h = _rms_mod(xn, g_ref[...], sh_ref[0], sc_ref[0])
    logits = lax.dot_general(wr_ref[...], h, NT_DIMS, precision=HI, preferred_element_type=F32)
    scores = jax.nn.sigmoid(logits)
    sel = scores + rb_ref[...]
    epg = EXPERTS_PER_GROUP
    srow = lambda e: sel[e:e + 1, :]
    grp = [_top2_sum(*[srow(epg * g + r) for r in range(epg)]) for g in range(N_EXPERT_GROUPS)]
    best, gi = grp[0], jnp.zeros_like(grp[0], dtype=jnp.int32)
    for g in range(1, N_EXPERT_GROUPS):
        better = grp[g] > best
        gi = jnp.where(better, g, gi)
        best = jnp.where(better, grp[g], best)

    def in_group(mat, r):
        out = mat[r:r + 1, :]
        for g in range(1, N_EXPERT_GROUPS):
            out = jnp.where(gi == g, mat[epg * g + r:epg * g + r + 1, :], out)
        return out

    v = [in_group(sel, r) for r in range(epg)]
    sc = [in_group(scores, r) for r in range(epg)]
    b1, i1, w1 = v[0], jnp.zeros_like(gi), sc[0]
    for r in range(1, epg):
        better = v[r] > b1
        i1 = jnp.where(better, r, i1)
        w1 = jnp.where(better, sc[r], w1)
        b1 = jnp.where(better, v[r], b1)
    b2 = jnp.full_like(b1, -3e38)
    i2, w2 = jnp.zeros_like(gi), jnp.zeros_like(w1)
    for r in range(epg):
        better = (i1 != r) & (v[r] > b2)
        i2 = jnp.where(better, r, i2)
        w2 = jnp.where(better, sc[r], w2)
        b2 = jnp.where(better, v[r], b2)
    tot = w1 + w2
    w1, w2 = w1 / tot, w2 / tot
    zero = jnp.zeros_like(w1)
    route_ref[0] = jnp.concatenate([gi.astype(F32)] + [zero] * (ROUTE_ROWS - 1), axis=0)
    w = jnp.concatenate([jnp.where(i1 == r, w1, jnp.where(i2 == r, w2, 0.0)) for r in range(epg)], axis=0)
    w_hi = w.astype(BF16).astype(F32)
    w_mid = (w - w_hi).astype(BF16).astype(F32)
    w_lo = (w - w_hi - w_mid).astype(BF16).astype(F32)
    pad = jnp.zeros((LANES - HX_TERMS * epg, w.shape[1]), F32)
    hx_ref[0, :, 0:D_MODEL] = h.astype(BF16)
    hx_ref[0, :, D_MODEL:HX_W] = jnp.concatenate([w_hi, w_mid, w_lo, pad], axis=0).T.astype(BF16)


def _route_specs(tm, row, vec, const):
    in_specs = [pl.BlockSpec((1, D_MODEL), const), pl.BlockSpec((1, 1, D_MODEL), vec), pl.BlockSpec((1, 1, D_MODEL), vec),
                pl.BlockSpec((N_EXPERTS, D_MODEL), const), pl.BlockSpec((N_EXPERTS, 1), const)]
    assert tm == MOE_CHUNK
    out_specs = [pl.BlockSpec((1, tm, HX_W), lambda b, i: (b * (SEQ // tm) + i, 0, 0)),
                 pl.BlockSpec((1, ROUTE_ROWS, tm), lambda b, i: (b, 0, i))]
    out_shape = [jax.ShapeDtypeStruct((MOE_CHUNKS, MOE_CHUNK, HX_W), BF16),
                 jax.ShapeDtypeStruct((BATCH, ROUTE_ROWS, SEQ), F32)]
    return in_specs, out_specs, out_shape


def _route_args(g, shift, scale, w_router, router_bias):
    return (g.reshape(1, D_MODEL), shift.reshape(BATCH, 1, D_MODEL), scale.reshape(BATCH, 1, D_MODEL),
            w_router.T, router_bias.reshape(N_EXPERTS, 1))


OUTPROJ_TM = 512


def _outproj0_kernel(oa_ref, ob_ref, w_ref, x_ref, gate_ref, g_ref, sh_ref, sc_ref, wr_ref, rb_ref,
                     xo_ref, h_ref, route_ref):
    y = jnp.dot(oa_ref[0].astype(BF16), w_ref[0:GLA_V_W, :], preferred_element_type=F32)
    ob_t = ob_ref[0].reshape(NSA_Q_W, OUTPROJ_TM).astype(BF16)
    y += lax.dot_general(ob_t, w_ref[GLA_V_W:GLA_V_W + NSA_Q_W, :], TN_DIMS, preferred_element_type=F32)
    xn = x_ref[0] + gate_ref[0] * y
    xo_ref[0] = xn
    _moe_prenorm_route(xn, g_ref, sh_ref, sc_ref, wr_ref, rb_ref, h_ref, route_ref)


def outproj0(o_a, o_b, w_out, x, gate, route_args):
    tm = OUTPROJ_TM
    row = lambda b, i: (b, i, 0)
    vec = lambda b, i: (b, 0, 0)
    const = lambda b, i: (0, 0)
    r_in, r_out, r_shape = _route_specs(tm, row, vec, const)
    return pl.pallas_call(
        _outproj0_kernel,
        grid=(BATCH, SEQ // tm),
        in_specs=[pl.BlockSpec((1, tm, GLA_V_W), row), pl.BlockSpec((1, NSA_HEADS, NSA_DH, tm), lambda b, i: (b, 0, 0, i)),
                  pl.BlockSpec((GLA_V_W + NSA_Q_W, D_MODEL), const), pl.BlockSpec((1, tm, D_MODEL), row),
                  pl.BlockSpec((1, 1, D_MODEL), vec)] + r_in,
        out_specs=[pl.BlockSpec((1, tm, D_MODEL), row)] + r_out,
        out_shape=[jax.ShapeDtypeStruct((BATCH, SEQ, D_MODEL), F32)] + r_shape,
        compiler_params=_cparams(("arbitrary", "arbitrary")),
        name="outproj0",
    )(o_a, o_b, w_out.astype(BF16), x, gate.reshape(BATCH, 1, D_MODEL), *route_args)


GMLP_TM = 512


def _gmlp_kernel(x_ref, g1_ref, sh1_ref, sc1_ref, win_ref, ng_ref, ws_ref, bs_ref, wout_ref, gate_ref,
                 g_ref, sh_ref, sc_ref, wr_ref, rb_ref, xo_ref, h_ref, route_ref, gated_ref, v_ref):
    x = x_ref[0]
    h = _rms_mod(x, g1_ref[...], sh1_ref[0], sc1_ref[0]).astype(BF16)
    group_cols = lambda g: slice(g * SGU_GROUP_DIM, (g + 1) * SGU_GROUP_DIM)
    ssq = jnp.zeros((GMLP_TM, LANES), F32)
    for g in range(SGU_GROUPS):
        lo = SGU_WIDTH + g * SGU_GROUP_DIM
        v = jax.nn.gelu(jnp.dot(h, win_ref[:, lo:lo + SGU_GROUP_DIM], preferred_element_type=F32))
        v_ref[:, group_cols(g)] = v
        ssq += functools.reduce(jnp.add, [v[:, n * LANES:(n + 1) * LANES] ** 2 for n in range(SGU_GROUP_DIM // LANES)])
    rs = lax.rsqrt(jnp.sum(ssq, axis=-1, keepdims=True) * (1.0 / SGU_WIDTH) + NORM_EPS)
    ri = lax.broadcasted_iota(jnp.int32, (SGU_CHUNK, SGU_CHUNK), 0)
    ci = lax.broadcasted_iota(jnp.int32, (SGU_CHUNK, SGU_CHUNK), 1)
    for g in range(SGU_GROUPS):
        cols = group_cols(g)
        u = jax.nn.gelu(jnp.dot(h, win_ref[:, cols], preferred_element_type=F32))
        vn = (v_ref[:, cols] * rs * ng_ref[:, cols]).astype(BF16)
        w = jnp.where(ri >= ci, ws_ref[g], 0.0).astype(BF16)
        for c in range(GMLP_TM // SGU_CHUNK):
            rows = slice(c * SGU_CHUNK, (c + 1) * SGU_CHUNK)
            mix = jnp.dot(w, vn[rows], preferred_element_type=F32) + bs_ref[:, g:g + 1]
            gated_ref[rows, cols] = (u[rows] * mix).astype(BF16)
    y = jnp.dot(gated_ref[...], wout_ref[...], preferred_element_type=F32)
    xn = x + gate_ref[0] * y
    xo_ref[0] = xn
    _moe_prenorm_route(xn, g_ref, sh_ref, sc_ref, wr_ref, rb_ref, h_ref, route_ref)


def gmlp_layer(x, g1, shift1, scale1, w_in, norm_g, w_s, b_s, w_out, gate, route_args):
    tm = GMLP_TM
    row = lambda b, i: (b, i, 0)
    vec = lambda b, i: (b, 0, 0)
    const = lambda b, i: (0, 0)
    r_in, r_out, r_shape = _route_specs(tm, row, vec, const)
    vspec = pl.BlockSpec((1, 1, D_MODEL), vec)
    return pl.pallas_call(
        _gmlp_kernel,
        grid=(BATCH, SEQ // tm),
        in_specs=[pl.BlockSpec((1, tm, D_MODEL), row), pl.BlockSpec((1, D_MODEL), const), vspec, vspec,
                  pl.BlockSpec((D_MODEL, 2 * SGU_WIDTH), const), pl.BlockSpec((1, SGU_WIDTH), const),
                  pl.BlockSpec((SGU_GROUPS, SGU_CHUNK, SGU_CHUNK), lambda b, i: (0, 0, 0)),
                  pl.BlockSpec((SGU_CHUNK, SGU_GROUPS), const), pl.BlockSpec((SGU_WIDTH, D_MODEL), const), vspec] + r_in,
        out_specs=[pl.BlockSpec((1, tm, D_MODEL), row)] + r_out,
        out_shape=[jax.ShapeDtypeStruct((BATCH, SEQ, D_MODEL), F32)] + r_shape,
        scratch_shapes=[pltpu.VMEM((tm, SGU_WIDTH), BF16), pltpu.VMEM((tm, SGU_WIDTH), F32)],
        compiler_params=_cparams(("arbitrary", "arbitrary"), vmem_mib=56),
        name="gmlp_layer",
    )(x, g1.reshape(1, D_MODEL), shift1.reshape(BATCH, 1, D_MODEL), scale1.reshape(BATCH, 1, D_MODEL),
      w_in.astype(BF16), norm_g.reshape(1, SGU_WIDTH), w_s, b_s.T, w_out.astype(BF16),
      gate.reshape(BATCH, 1, D_MODEL), *route_args)


MOE_TM = 256
MOE_CHUNK = 512
MOE_SORTED = N_TOK + N_EXPERT_GROUPS * MOE_TM
MOE_TILES = MOE_SORTED // MOE_TM
MOE_CHUNKS = N_TOK // MOE_CHUNK
MOE_PAIRS = MOE_TILES + N_EXPERT_GROUPS * MOE_CHUNKS
FLAG_ACTIVE, FLAG_FIRST, FLAG_LAST, FLAG_ZERO = 1, 2, 4, 8
EXP_WIN = 4
CMB_WIN = 8
MOE_TSTEPS = MOE_TILES + MOE_PAIRS // EXP_WIN
MOE_CSTEPS = MOE_CHUNKS + MOE_PAIRS // CMB_WIN


def _plan_kernel(gi_ref, rank_ref, before_ref):
    gi = gi_ref[...]
    r = lax.broadcasted_iota(jnp.int32, (POS_SIDE, POS_SIDE), 0)
    c = lax.broadcasted_iota(jnp.int32, (POS_SIDE, POS_SIDE), 1)
    upper = jnp.where(r <= c, 1.0, 0.0)
    lower_strict = jnp.where(c < r, 1.0, 0.0)
    rank = jnp.zeros((POS_SIDE, POS_SIDE), F32)
    for g in range(N_EXPERT_GROUPS):
        member = jnp.where(gi == g, 1.0, 0.0)
        in_row = jnp.dot(member, upper, precision=HI, preferred_element_type=F32)
        row_total = jnp.broadcast_to(in_row[:, POS_SIDE - 1:POS_SIDE], (POS_SIDE, POS_SIDE))
        before = jnp.dot(lower_strict, row_total, precision=HI, preferred_element_type=F32)
        before_ref[g] = before
        rank += member * (before + in_row - 1.0)
    rank_ref[...] = rank


def moe_plan(route):
    tm = MOE_TM
    i32 = jnp.int32
    gi_f = route[:, 0, :].reshape(POS_SIDE, POS_SIDE)
    rank, before = pl.pallas_call(
        _plan_kernel,
        out_shape=[jax.ShapeDtypeStruct((POS_SIDE, POS_SIDE), F32),
                   jax.ShapeDtypeStruct((N_EXPERT_GROUPS, POS_SIDE, POS_SIDE), F32)],
        name="moe_plan",
    )(gi_f)
    gi = gi_f.reshape(N_TOK).astype(i32)
    groups = jnp.arange(N_EXPERT_GROUPS, dtype=i32)
    member = gi[None, :] == groups[:, None]
    tot = jnp.sum(member, axis=1).astype(i32)
    padded = (tot + tm - 1) // tm * tm
    gend = jnp.cumsum(padded).astype(i32)
    gstart = gend - padded
    pos = jnp.sum(jnp.where(member, gstart[:, None], 0), axis=0).astype(i32) + rank.reshape(N_TOK).astype(i32)
    rows_per_chunk = MOE_CHUNK // POS_SIDE
    cnt_end = jnp.concatenate([before[:, rows_per_chunk::rows_per_chunk, 0].astype(i32), tot[:, None]], axis=1)
    t = jnp.arange(MOE_TILES, dtype=i32)
    n_used = gend[-1] // tm
    tile_g = jnp.minimum(jnp.sum(gend[None, :] <= (t * tm)[:, None], axis=1), N_EXPERT_GROUPS - 1).astype(i32)
    k0 = t * tm - gstart[tile_g]
    k1 = jnp.minimum(k0 + tm, tot[tile_g]) - 1
    ce = cnt_end[tile_g]
    c_lo = jnp.sum(ce <= k0[:, None], axis=1).astype(i32)
    c_hi = jnp.sum(ce <= k1[:, None], axis=1).astype(i32)
    npairs = jnp.where(t < n_used, c_hi - c_lo + 1, 0)
    total = jnp.sum(npairs).astype(i32)
    lt = jnp.minimum(jnp.arange(MOE_PAIRS, dtype=i32), total - 1)

    def windows(count, win, n_steps):
        per_item = (count + win - 1) // win
        end = jnp.cumsum(per_item).astype(i32)
        start = end - per_item
        s = jnp.arange(n_steps, dtype=i32)
        real_s = s < end[-1]
        sc = jnp.minimum(s, end[-1] - 1)
        item = jnp.sum(end[None, :] <= sc[:, None], axis=1).astype(i32)
        j = sc - start[item]
        flags_s = jnp.where(real_s, FLAG_ACTIVE + jnp.where(j == 0, FLAG_FIRST, 0)
                            + jnp.where(j == per_item[item] - 1, FLAG_LAST, 0), 0).astype(i32)
        return item, j, flags_s, real_s, s - end[-1], end[-1]

    tile_s, j, flags, real_s, spare, steps_t = windows(npairs, EXP_WIN, MOE_TSTEPS)
    steps_t = steps_t + (MOE_TILES - n_used)
    c0 = c_lo[tile_s] + EXP_WIN * j
    n_valid = jnp.minimum(EXP_WIN, c_hi[tile_s] - c0 + 1).astype(i32)
    spare_tile = jnp.minimum(n_used + spare, MOE_TILES - 1)
    flags = jnp.where(real_s, flags, jnp.where(spare_tile >= n_used, FLAG_ZERO, 0)).astype(i32)
    tile_sched = jnp.where(real_s, tile_s, spare_tile).astype(i32)
    by_tile = (tile_sched, c0.astype(i32), n_valid, flags, tile_g[tile_sched])
    cc = jnp.arange(MOE_CHUNKS, dtype=i32)
    is_pair = (cc[:, None] >= c_lo[None, :]) & (cc[:, None] <= c_hi[None, :]) & (t[None, :] < n_used)
    seen = jnp.cumsum(is_pair.reshape(-1).astype(i32))
    flat = jnp.sum(seen[None, :] <= lt[:, None], axis=1).astype(i32)
    pair_tile = flat % MOE_TILES
    per_chunk = jnp.sum(is_pair, axis=1).astype(i32)
    first_pair = jnp.cumsum(per_chunk).astype(i32) - per_chunk
    chunk_s, j, flags_c, _, _, steps_c = windows(per_chunk, CMB_WIN, MOE_CSTEPS)
    base = first_pair[chunk_s] + CMB_WIN * j
    n_valid_c = jnp.minimum(CMB_WIN, per_chunk[chunk_s] - CMB_WIN * j).astype(i32)
    tiles_c = tuple(pair_tile[jnp.minimum(base + w, total - 1)] for w in range(CMB_WIN))
    by_chunk = (chunk_s, n_valid_c, flags_c) + tiles_c
    return pos.reshape(MOE_CHUNKS, 1, MOE_CHUNK), (steps_t, by_tile), (steps_c, by_chunk)


def _one_hot_rows(pos_row, tile):
    rows = tile * MOE_TM + lax.broadcasted_iota(jnp.int32, (MOE_TM, MOE_CHUNK), 0)
    return jnp.where(pos_row == rows, 1.0, 0.0).astype(BF16)


def _moe_kernel(tile_ref, c0_ref, nv_ref, flag_ref, grp_ref, *refs):
    pos_refs, hx_refs = refs[0:EXP_WIN], refs[EXP_WIN:2 * EXP_WIN]
    wg_ref, wu_ref, wd_ref, y_ref, acc_ref = refs[2 * EXP_WIN:]
    l = pl.program_id(0)
    flags = flag_ref[l]

    @pl.when((flags & FLAG_FIRST) != 0)
    def _():
        acc_ref[...] = jnp.zeros_like(acc_ref)

    for w in range(EXP_WIN):
        @pl.when(((flags & FLAG_ACTIVE) != 0) & (w < nv_ref[l]))
        def _():
            onehot = _one_hot_rows(pos_refs[w][0], tile_ref[l])
            acc_ref[...] += jnp.dot(onehot, hx_refs[w][0], preferred_element_type=F32)

    @pl.when((flags & FLAG_LAST) != 0)
    def _():
        x = acc_ref[:, 0:D_MODEL].astype(BF16)
        terms = acc_ref[:, D_MODEL:HX_W]
        y = jnp.zeros((MOE_TM, D_MODEL), F32)
        for r in range(EXPERTS_PER_GROUP):
            lanes = [n * EXPERTS_PER_GROUP + r for n in range(HX_TERMS)]
            w_r = functools.reduce(jnp.add, [terms[:, c:c + 1] for c in lanes])
            gate = jnp.dot(x, wg_ref[0, 0, r], preferred_element_type=F32)
            up = jnp.dot(x, wu_ref[0, 0, r], preferred_element_type=F32)
            hid = (_silu(gate) * up * w_r).astype(BF16)
            y += jnp.dot(hid, wd_ref[0, 0, r], preferred_element_type=F32)
        y_ref[...] = y

    @pl.when((flags & FLAG_ZERO) != 0)
    def _():
        y_ref[...] = jnp.zeros_like(y_ref)


def moe_experts(hx, pos, schedule, w_gate, w_up, w_down, layer):
    n_steps, by_tile = schedule
    grouped = lambda w: w.reshape(DEPTH, N_EXPERT_GROUPS, EXPERTS_PER_GROUP, *w.shape[2:])
    wspec = lambda k, n: pl.BlockSpec((1, 1, EXPERTS_PER_GROUP, k, n), lambda l, t, c, n_, f, g: (layer, g[l], 0, 0, 0))
    chunk = lambda w: (lambda l, t, c, n_, f, g: (jnp.minimum(c[l] + w, MOE_CHUNKS - 1), 0, 0))
    return pl.pallas_call(
        _moe_kernel,
        grid_spec=pltpu.PrefetchScalarGridSpec(
            num_scalar_prefetch=5,
            grid=(n_steps,),
            in_specs=[pl.BlockSpec((1, 1, MOE_CHUNK), chunk(w)) for w in range(EXP_WIN)]
            + [pl.BlockSpec((1, MOE_CHUNK, HX_W), chunk(w)) for w in range(EXP_WIN)]
            + [wspec(D_MODEL, EXPERT_HIDDEN), wspec(D_MODEL, EXPERT_HIDDEN), wspec(EXPERT_HIDDEN, D_MODEL)],
            out_specs=pl.BlockSpec((MOE_TM, D_MODEL), lambda l, t, c, n_, f, g: (t[l], 0)),
            scratch_shapes=[pltpu.VMEM((MOE_TM, HX_W), F32)],
        ),
        out_shape=jax.ShapeDtypeStruct((MOE_SORTED, D_MODEL), F32),
        compiler_params=_cparams(("arbitrary",), vmem_mib=56),
        name="moe_experts",
    )(*by_tile, *([pos] * EXP_WIN), *([hx] * EXP_WIN), grouped(w_gate), grouped(w_up), grouped(w_down))


def _moe_combine_kernel(chunk_ref, nv_ref, flag_ref, *refs):
    tile_refs = refs[0:CMB_WIN]
    pos_ref = refs[CMB_WIN]
    y_refs = refs[CMB_WIN + 1:2 * CMB_WIN + 1]
    x_ref, gate_ref, o_ref, acc_ref = refs[2 * CMB_WIN + 1:]
    l = pl.program_id(0)
    flags = flag_ref[l]

    @pl.when((flags & FLAG_FIRST) != 0)
    def _():
        acc_ref[...] = jnp.zeros_like(acc_ref)

    for w in range(CMB_WIN):
        @pl.when(((flags & FLAG_ACTIVE) != 0) & (w < nv_ref[l]))
        def _():
            onehot = _one_hot_rows(pos_ref[0], tile_refs[w][l])
            y = y_refs[w][...]
            y_hi = y.astype(BF16)
            y_lo = (y - y_hi.astype(F32)).astype(BF16)
            acc_ref[...] += (lax.dot_general(onehot, y_hi, TN_DIMS, preferred_element_type=F32)
                             + lax.dot_general(onehot, y_lo, TN_DIMS, preferred_element_type=F32))

    @pl.when((flags & FLAG_LAST) != 0)
    def _():
        o_ref[0] = x_ref[0] + gate_ref[0] * acc_ref[...]


def moe_combine(x, y_sorted, pos, schedule, gate):
    n_steps, by_chunk = schedule
    per_b = SEQ // MOE_CHUNK
    tok = lambda l, c, *_: (c[l] // per_b, c[l] % per_b, 0)
    tile = lambda w: (lambda l, c, n_, f, *tiles: (tiles[w][l], 0))
    return pl.pallas_call(
        _moe_combine_kernel,
        grid_spec=pltpu.PrefetchScalarGridSpec(
            num_scalar_prefetch=3 + CMB_WIN,
            grid=(n_steps,),
            in_specs=[pl.BlockSpec((1, 1, MOE_CHUNK), lambda l, c, *_: (c[l], 0, 0))]
            + [pl.BlockSpec((MOE_TM, D_MODEL), tile(w)) for w in range(CMB_WIN)]
            + [pl.BlockSpec((1, MOE_CHUNK, D_MODEL), tok),
               pl.BlockSpec((1, 1, D_MODEL), lambda l, c, *_: (c[l] // per_b, 0, 0))],
            out_specs=pl.BlockSpec((1, MOE_CHUNK, D_MODEL), tok),
            scratch_shapes=[pltpu.VMEM((MOE_CHUNK, D_MODEL), F32)],
        ),
        out_shape=jax.ShapeDtypeStruct((BATCH, SEQ, D_MODEL), F32),
        compiler_params=_cparams(("arbitrary",)),
        name="moe_combine",
    )(*by_chunk, pos, *([y_sorted] * CMB_WIN), x, gate.reshape(BATCH, 1, D_MODEL))


def moe_layer(x, hx, route, gate, w_gate, w_up, w_down, layer):
    pos, by_tile, by_chunk = moe_plan(route)
    y_sorted = moe_experts(hx, pos, by_tile, w_gate, w_up, w_down, layer)
    return moe_combine(x, y_sorted, pos, by_chunk, gate)


def kernel(x, c, positions, w_ada, b_ada, norm_g, w_in_ab, w_out_ab, gla_w_gate2, gla_b_gate, gla_norm_g, nsa_q_gain, nsa_k_gain, nsa_cmp_pe, nsa_cmp_w1, nsa_cmp_w2, w_in_c, sgu_norm_g, sgu_w_s, sgu_b_s, w_out_c, w_router, router_bias, w_gate, w_up, w_down):
    mod = ada_modulation(c, w_ada, b_ada)
    qk, gv, gr, nq, nkv, misc = inproj0(x, norm_g[0, 0], mod[0, :, 0], mod[0, :, 1], _arrange_w_in(w_in_ab[0]))
    o_a = gla_mixer(qk, gv, gr, misc, gla_w_gate2[0], gla_b_gate[0], gla_norm_g[0])
    o_b = nsa_mixer(nq, nkv, misc, positions, nsa_q_gain[0], nsa_k_gain[0], nsa_cmp_pe[0], nsa_cmp_w1[0], nsa_cmp_w2[0])
    wg, wu, wd = w_gate.astype(BF16), w_up.astype(BF16), w_down.astype(BF16)
    route_args = lambda l: _route_args(norm_g[l, 1], mod[l, :, 3], mod[l, :, 4], w_router, router_bias)
    x1, h, route = outproj0(o_a, o_b, w_out_ab[0], x, mod[0, :, 2], route_args(0))
    x2 = moe_layer(x1, h, route, mod[0, :, 5], wg, wu, wd, 0)
    x3, h, route = gmlp_layer(x2, norm_g[1, 0], mod[1, :, 0], mod[1, :, 1], w_in_c[0], sgu_norm_g[0], sgu_w_s[0],
                              sgu_b_s[0], w_out_c[0], mod[1, :, 2], route_args(1))
    return moe_layer(x3, h, route, mod[1, :, 5], wg, wu, wd, 1)
```

```python
import functools

import numpy as np
import jax
import jax.numpy as jnp
from jax import lax
from jax.experimental import pallas as pl
from jax.experimental.pallas import tpu as pltpu

D_MODEL = 1024
BATCH = 2
SEQ = 8192
DEPTH = 2
N_TOK = BATCH * SEQ

GLA_HEADS = 4
GLA_DK = 64
GLA_DV = 128
GLA_GATE_RANK = 16
GLA_TAU = 16.0
GLA_CHUNK = 64
NSA_HEADS = 8
NSA_KV_GROUPS = 2
NSA_HPG = NSA_HEADS // NSA_KV_GROUPS
NSA_DH = 64
CMP_LEN = 32
CMP_STRIDE = 16
CMP_HIDDEN = 256
SEL_BLOCK = 64
SEL_TOPK = 16
WINDOW = 512
ROPE_THETA = 500000.0
ROT_DIM = NSA_DH // 4
ROT_HALF = ROT_DIM // 2
SGU_CHUNK = 128
SGU_GROUPS = 8
SGU_WIDTH = 2048
SGU_GROUP_DIM = SGU_WIDTH // SGU_GROUPS
N_EXPERTS = 16
N_EXPERT_GROUPS = 4
EXPERTS_PER_GROUP = N_EXPERTS // N_EXPERT_GROUPS
EXPERT_HIDDEN = 512

GLA_QK_W = GLA_HEADS * GLA_DK
GLA_V_W = GLA_HEADS * GLA_DV
NSA_Q_W = NSA_HEADS * NSA_DH
NSA_KV_W = NSA_KV_GROUPS * NSA_DH
N_CMP_PAD = SEQ // CMP_STRIDE
N_SEL = SEQ // SEL_BLOCK

NORM_EPS = 1e-6
NEG_INF = -1e30
FORCE_BONUS = 1e4

LANES = 128
MIB = 1024 * 1024

F32 = jnp.float32
BF16 = jnp.bfloat16
HI = lax.Precision.HIGHEST
NT_DIMS = (((1,), (1,)), ((), ()))
TN_DIMS = (((0,), (0,)), ((), ()))


def _cparams(sem, vmem_mib=48):
    return pltpu.CompilerParams(dimension_semantics=sem, vmem_limit_bytes=vmem_mib * MIB)


def _rms_mod(x, g, shift, scale):
    y = x * lax.rsqrt(jnp.mean(x * x, axis=-1, keepdims=True) + NORM_EPS) * g
    return y * (1 + scale) + shift


def _silu(x):
    return x * jax.nn.sigmoid(x)


def _log_sigmoid(z):
    return jnp.minimum(z, 0.0) - jnp.log1p(jnp.exp(-jnp.abs(z)))


ADA_TN = 1536
ADA_ROWS = 8


def _ada_kernel(c_ref, w_ref, b_ref, o_ref):
    cond = _silu(c_ref[...])
    o_ref[0] = jnp.dot(cond, w_ref[0], precision=HI, preferred_element_type=F32) + b_ref[0]


def ada_modulation(c, w_ada, b_ada):
    c8 = jnp.zeros((ADA_ROWS, D_MODEL), F32).at[:BATCH].set(c)
    width = 6 * D_MODEL
    out = pl.pallas_call(
        _ada_kernel,
        grid=(DEPTH, width // ADA_TN),
        in_specs=[
            pl.BlockSpec((ADA_ROWS, D_MODEL), lambda l, j: (0, 0)),
            pl.BlockSpec((1, D_MODEL, ADA_TN), lambda l, j: (l, 0, j)),
            pl.BlockSpec((1, 1, ADA_TN), lambda l, j: (l, 0, j)),
        ],
        out_specs=pl.BlockSpec((1, ADA_ROWS, ADA_TN), lambda l, j: (l, 0, j)),
        out_shape=jax.ShapeDtypeStruct((DEPTH, ADA_ROWS, width), F32),
        compiler_params=_cparams(("arbitrary", "arbitrary")),
        name="ada_modulation",
    )(c8, w_ada, b_ada.reshape(DEPTH, 1, width))
    return out[:, :BATCH].reshape(DEPTH, BATCH, 6, D_MODEL)


INPROJ_TM = 512
INPROJ_WIDTHS = (2 * GLA_QK_W, GLA_V_W, GLA_V_W, NSA_Q_W, 6 * NSA_KV_W, LANES)


def _arrange_w_in(w_in):
    o = np.cumsum((0, GLA_QK_W, GLA_QK_W, GLA_V_W, GLA_GATE_RANK, GLA_V_W, NSA_Q_W, 6 * NSA_KV_W, NSA_HEADS * 3))
    gq_gk = w_in[:, o[0]:o[2]]
    gv = w_in[:, o[2]:o[3]]
    glr = w_in[:, o[3]:o[4]]
    gr = w_in[:, o[4]:o[5]]
    nq = w_in[:, o[5]:o[6]]
    nkv = w_in[:, o[6]:o[7]]
    ng = w_in[:, o[7]:o[8]]
    pad = jnp.zeros((D_MODEL, LANES - GLA_GATE_RANK - NSA_HEADS * 3), w_in.dtype)
    return jnp.concatenate([gq_gk, gv, gr, nq, nkv, glr, ng, pad], axis=1).astype(BF16)


def _inproj0_kernel(x_ref, g_ref, sh_ref, sc_ref, w_ref, *o_refs):
    h = _rms_mod(x_ref[0], g_ref[...], sh_ref[0], sc_ref[0]).astype(BF16)
    off = 0
    for o_ref, wd in zip(o_refs, INPROJ_WIDTHS):
        o_ref[0] = jnp.dot(h, w_ref[:, off:off + wd], preferred_element_type=F32)
        off += wd


def inproj0(x, g, shift, scale, w_arranged):
    tm = INPROJ_TM
    wtot = sum(INPROJ_WIDTHS)
    row = lambda b, i: (b, i, 0)
    vec = lambda b, i: (b, 0, 0)
    return pl.pallas_call(
        _inproj0_kernel,
        grid=(BATCH, SEQ // tm),
        in_specs=[
            pl.BlockSpec((1, tm, D_MODEL), row),
            pl.BlockSpec((1, D_MODEL), lambda b, i: (0, 0)),
            pl.BlockSpec((1, 1, D_MODEL), vec),
            pl.BlockSpec((1, 1, D_MODEL), vec),
            pl.BlockSpec((D_MODEL, wtot), lambda b, i: (0, 0)),
        ],
        out_specs=[pl.BlockSpec((1, tm, wd), row) for wd in INPROJ_WIDTHS],
        out_shape=[jax.ShapeDtypeStruct((BATCH, SEQ, wd), F32) for wd in INPROJ_WIDTHS],
        compiler_params=_cparams(("arbitrary", "arbitrary")),
        name="inproj0",
    )(x, g.reshape(1, D_MODEL), shift.reshape(BATCH, 1, D_MODEL), scale.reshape(BATCH, 1, D_MODEL), w_arranged)


GLA_TG = 512


def _gla_chunk_sums():
    i = np.arange(GLA_TG)[:, None]
    j = np.arange(GLA_TG)[None, :]
    same = (i // GLA_CHUNK) == (j // GLA_CHUNK)
    m3 = np.concatenate([same & (j <= i), same & (j % GLA_CHUNK <= GLA_CHUNK // 2), same], axis=0).astype(np.float32)
    return jnp.asarray(np.concatenate([m3, m3], axis=1), BF16)


def _gla_kernel(qk_ref, v_ref, r_ref, misc_ref, w2_ref, bg_ref, og_ref, sums_ref, o_ref, st_ref):
    C, tg = GLA_CHUNK, GLA_TG

    @pl.when(pl.program_id(0) == 0)
    def _():
        st_ref[...] = jnp.zeros_like(st_ref)

    lane = lax.broadcasted_iota(jnp.int32, (1, GLA_QK_W), 1)
    heads = [(lane >= h * GLA_DK) & (lane < (h + 1) * GLA_DK) for h in range(GLA_HEADS)]
    stack = lambda per_head, rows: jnp.concatenate([t[rows] for t in per_head], axis=0)
    stacked_row = lax.broadcasted_iota(jnp.int32, (GLA_HEADS * C, C), 0)
    causal = (stacked_row & (C - 1)) >= lax.broadcasted_iota(jnp.int32, (GLA_HEADS * C, C), 1)
    og = og_ref[...]

    def prepare(b):
        z = jnp.dot(misc_ref[b], w2_ref[...], precision=HI, preferred_element_type=F32) + bg_ref[...]
        la = _log_sigmoid(z) / GLA_TAU
        la_hi = la.astype(BF16)
        la_lo = (la - la_hi.astype(F32)).astype(BF16)
        sums = jnp.dot(sums_ref[...], jnp.concatenate([la_hi, la_lo], axis=0), preferred_element_type=F32)
        bc, b_mid, b_last = sums[0:tg], sums[tg:2 * tg], sums[2 * tg:3 * tg]
        q = qk_ref[b, :, 0:GLA_QK_W] * (GLA_DK ** -0.5)
        k = qk_ref[b, :, GLA_QK_W:2 * GLA_QK_W]
        qd = q * jnp.exp(bc - b_mid)
        kl = k * jnp.exp(b_last - bc)
        qb = q * jnp.exp(bc)
        per_head = lambda t: [jnp.where(m, t, 0.0).astype(BF16) for m in heads]
        return dict(kd=(k * jnp.exp(b_mid - bc)).astype(BF16), dec=jnp.exp(b_last), qd_h=per_head(qd),
                    qb_h=per_head(qb), kl_h=per_head(kl))

    batches = range(BATCH)
    pre = [prepare(b) for b in batches]
    st = [st_ref[b] for b in batches]
    for c in range(tg // C):
        rows = slice(c * C, (c + 1) * C)
        for b in batches:
            p = pre[b]
            v = v_ref[b, rows, :].astype(BF16)
            s = lax.dot_general(stack(p["qd_h"], rows), p["kd"][rows], NT_DIMS, preferred_element_type=F32)
            s = jnp.where(causal, s, 0.0).astype(BF16)
            o_intra = jnp.dot(s, v, preferred_element_type=F32)
            o_inter = lax.dot_general(stack(p["qb_h"], rows), st[b].astype(BF16), NT_DIMS, preferred_element_type=F32)
            v_stack = jnp.concatenate([v[:, h * GLA_DV:(h + 1) * GLA_DV] for h in range(GLA_HEADS)], axis=0)
            st[b] = st[b] * p["dec"][c * C:c * C + 1] + lax.dot_general(v_stack, stack(p["kl_h"], rows), TN_DIMS,
                                                                       preferred_element_type=F32)
            for h in range(GLA_HEADS):
                hrows = slice(h * C, (h + 1) * C)
                vcols = slice(h * GLA_DV, (h + 1) * GLA_DV)
                o = o_intra[hrows, vcols] + o_inter[hrows]
                on = o * lax.rsqrt(jnp.mean(o * o, axis=-1, keepdims=True) + NORM_EPS) * og
                o_ref[b, rows, vcols] = on * _silu(r_ref[b, rows, vcols])
    for b in batches:
        st_ref[b] = st[b]


def gla_mixer(qk, v, r, misc, w_gate2, b_gate, out_g):
    tg = GLA_TG
    w2 = jnp.zeros((LANES, GLA_QK_W), F32).at[:GLA_GATE_RANK].set(w_gate2)
    row = lambda i: (0, i, 0)
    const = lambda i: (0, 0)
    return pl.pallas_call(
        _gla_kernel,
        grid=(SEQ // tg,),
        in_specs=[
            pl.BlockSpec((BATCH, tg, 2 * GLA_QK_W), row),
            pl.BlockSpec((BATCH, tg, GLA_V_W), row),
            pl.BlockSpec((BATCH, tg, GLA_V_W), row),
            pl.BlockSpec((BATCH, tg, LANES), row),
            pl.BlockSpec((LANES, GLA_QK_W), const),
            pl.BlockSpec((1, GLA_QK_W), const),
            pl.BlockSpec((1, GLA_DV), const),
            pl.BlockSpec((3 * tg, 2 * tg), const),
        ],
        out_specs=pl.BlockSpec((BATCH, tg, GLA_V_W), row),
        out_shape=jax.ShapeDtypeStruct((BATCH, SEQ, GLA_V_W), F32),
        scratch_shapes=[pltpu.VMEM((BATCH, GLA_DV, GLA_QK_W), F32)],
        compiler_params=_cparams(("arbitrary",)),
        name="gla_mixer",
    )(qk, v, r, misc, w2, b_gate.reshape(1, GLA_QK_W), out_g.reshape(1, GLA_DV), _gla_chunk_sums())


POS_SIDE = 128


def _rope_table_kernel(freq_ref, pos_ref, cos_ref, sin_ref):
    pos = pos_ref[...].astype(F32)
    for f in range(ROT_HALF):
        ang = pos * freq_ref[f]
        cos_ref[f] = jnp.cos(ang)
        sin_ref[f] = jnp.sin(ang)


def rope_tables(positions):
    inv_freq = jnp.float32(ROPE_THETA) ** (-jnp.arange(ROT_HALF, dtype=F32) / ROT_HALF)
    shp = jax.ShapeDtypeStruct((ROT_HALF, POS_SIDE, POS_SIDE), F32)
    cos, sin = pl.pallas_call(
        _rope_table_kernel,
        in_specs=[pl.BlockSpec(memory_space=pltpu.SMEM), pl.BlockSpec(memory_space=pltpu.VMEM)],
        out_specs=[pl.BlockSpec(memory_space=pltpu.VMEM)] * 2,
        out_shape=[shp, shp],
        name="rope_tables",
    )(inv_freq, positions.reshape(POS_SIDE, POS_SIDE))
    return jnp.concatenate([cos, sin], axis=0).reshape(ROT_DIM, N_TOK)


def _rope_placement():
    place = np.zeros((ROT_DIM, 3 * LANES), np.float32)
    const = np.zeros((1, 3 * LANES), np.float32)
    for lane in range(LANES):
        i = lane % NSA_DH
        if i < ROT_HALF:
            place[i, lane] = 1.0
            place[ROT_HALF + i, LANES + lane] = -1.0
        elif i < ROT_DIM:
            place[i - ROT_HALF, lane] = 1.0
            place[i, 2 * LANES + lane] = 1.0
        else:
            const[0, lane] = 1.0
    return jnp.asarray(place.T), jnp.asarray(const)


def _lane_tables(cs, place_ref, const_ref):
    tab = jnp.dot(place_ref[...], cs, precision=HI, preferred_element_type=F32).T + const_ref[...]
    return tab[:, 0:LANES], tab[:, LANES:2 * LANES], tab[:, 2 * LANES:3 * LANES]


def _block_diag_ones2(width):
    h = np.arange(width) // NSA_DH
    bd = (h[:, None] == h[None, :]).astype(np.float32)
    return jnp.asarray(np.concatenate([bd, bd], axis=0), BF16)


def _head_norm_rope(x, gain, bd2, c, sm, sp):
    width = x.shape[-1]
    reps = width // LANES
    sq = x * x
    sq_hi = sq.astype(BF16)
    sq_lo = (sq - sq_hi.astype(F32)).astype(BF16)
    ss = jnp.dot(jnp.concatenate([sq_hi, sq_lo], axis=1), bd2, preferred_element_type=F32)
    y = x * lax.rsqrt(ss * (1.0 / NSA_DH) + NORM_EPS) * gain
    tile = lambda t: jnp.concatenate([t] * reps, axis=1) if reps > 1 else t
    return (y * tile(c) + pltpu.roll(y, width - ROT_HALF, 1) * tile(sm) + pltpu.roll(y, ROT_HALF, 1) * tile(sp))


PREP_TM = 512


def _prep_kernel(q_ref, kv_ref, cs_ref, place_ref, const_ref, gq_ref, gk_ref, bdq_ref, bdk_ref,
                 qo_ref, kso_ref, kwo_ref, vso_ref, vwo_ref, xk_ref, xv_ref, tok_ref):
    tm = PREP_TM
    seg_lane = lax.broadcasted_iota(jnp.int32, (tm // CMP_STRIDE, LANES), 1)
    for n, out_ref in ((0, xk_ref), (1, xv_ref)):
        tok_ref[...] = kv_ref[0, :, n * NSA_KV_W:(n + 1) * NSA_KV_W]
        for pair in range(CMP_STRIDE // 2):
            a = tok_ref[pl.ds(2 * pair, tm // CMP_STRIDE, stride=CMP_STRIDE), :]
            b = tok_ref[pl.ds(2 * pair + 1, tm // CMP_STRIDE, stride=CMP_STRIDE), :]
            piece = slice(pair * LANES, (pair + 1) * LANES)
            out_ref[0, 0, :, piece] = jnp.where(seg_lane < NSA_DH, a, pltpu.roll(b, NSA_DH, 1))
            out_ref[0, 1, :, piece] = jnp.where(seg_lane < NSA_DH, pltpu.roll(a, NSA_DH, 1), b)
    c, sm, sp = _lane_tables(cs_ref[...], place_ref, const_ref)
    bdk = bdk_ref[...]
    q = _head_norm_rope(q_ref[0], gq_ref[...], bdq_ref[...], c, sm, sp) * (NSA_DH ** -0.5)
    qo_ref[0] = q.T.reshape(NSA_HEADS, NSA_DH, tm)
    kv_cols = lambda n: kv_ref[0, :, n * NSA_KV_W:(n + 1) * NSA_KV_W]
    ks = _head_norm_rope(kv_cols(2), gk_ref[0:1, :], bdk, c, sm, sp)
    kw = _head_norm_rope(kv_cols(4), gk_ref[1:2, :], bdk, c, sm, sp)
    lane = lax.broadcasted_iota(jnp.int32, (tm, LANES), 1)
    token = pl.program_id(1) * tm + lax.broadcasted_iota(jnp.int32, (tm, LANES), 0)
    block_col = NSA_DH + jnp.right_shift(token & (SA_TK - 1), 6)
    onehot = jnp.where(lane == block_col, 1.0, 0.0)
    for g in range(NSA_KV_GROUPS):
        to_front = lambda t: t if g == 0 else pltpu.roll(t, NSA_DH, 1)
        kso_ref[0, g] = jnp.where(lane < NSA_DH, to_front(ks), onehot).astype(BF16)
        kwo_ref[0, g] = jnp.where(lane < NSA_DH, to_front(kw), 0.0).astype(BF16)
    tail = jnp.where(lax.broadcasted_iota(jnp.int32, (VT_ROWS - NSA_DH, tm), 0) == 0, 1.0, 0.0)
    for n, out_ref in ((3, vso_ref), (5, vwo_ref)):
        v_t = kv_cols(n).T
        for g in range(NSA_KV_GROUPS):
            out_ref[0, g] = jnp.concatenate([v_t[g * NSA_DH:(g + 1) * NSA_DH], tail], axis=0).astype(BF16)


def nsa_prep(nq, nkv, cs, q_gain, k_gain):
    tm = PREP_TM
    G = NSA_KV_GROUPS
    row = lambda b, i: (b, i, 0)
    const = lambda b, i: (0, 0)
    gq = jnp.tile(q_gain, NSA_HEADS).reshape(1, NSA_Q_W)
    gk = jnp.stack([jnp.tile(k_gain[1], NSA_KV_GROUPS), jnp.tile(k_gain[2], NSA_KV_GROUPS)])
    kslab = pl.BlockSpec((1, G, tm, LANES), lambda b, i: (b, 0, i, 0))
    vslab = pl.BlockSpec((1, G, VT_ROWS, tm), lambda b, i: (b, 0, 0, i))
    segs = pl.BlockSpec((1, G, tm // CMP_STRIDE, CMP_STRIDE * NSA_DH), lambda b, i: (b, 0, i, 0))
    return pl.pallas_call(
        _prep_kernel,
        grid=(BATCH, SEQ // tm),
        in_specs=[
            pl.BlockSpec((1, tm, NSA_Q_W), row),
            pl.BlockSpec((1, tm, 6 * NSA_KV_W), row),
            pl.BlockSpec((ROT_DIM, tm), lambda b, i: (0, b * (SEQ // tm) + i)),
            pl.BlockSpec((3 * LANES, ROT_DIM), const),
            pl.BlockSpec((1, 3 * LANES), const),
            pl.BlockSpec((1, NSA_Q_W), const),
            pl.BlockSpec((2, NSA_KV_W), const),
            pl.BlockSpec((2 * NSA_Q_W, NSA_Q_W), const),
            pl.BlockSpec((2 * NSA_KV_W, NSA_KV_W), const),
        ],
        out_specs=[pl.BlockSpec((1, NSA_HEADS, NSA_DH, tm), lambda b, i: (b, 0, 0, i)), kslab, kslab, vslab, vslab,
                   segs, segs],
        out_shape=[jax.ShapeDtypeStruct((BATCH, NSA_HEADS, NSA_DH, SEQ), F32),
                   jax.ShapeDtypeStruct((BATCH, G, SEQ, LANES), BF16), jax.ShapeDtypeStruct((BATCH, G, SEQ, LANES), BF16),
                   jax.ShapeDtypeStruct((BATCH, G, VT_ROWS, SEQ), BF16), jax.ShapeDtypeStruct((BATCH, G, VT_ROWS, SEQ), BF16),
                   jax.ShapeDtypeStruct((BATCH, G, N_CMP_PAD, CMP_STRIDE * NSA_DH), F32),
                   jax.ShapeDtypeStruct((BATCH, G, N_CMP_PAD, CMP_STRIDE * NSA_DH), F32)],
        scratch_shapes=[pltpu.VMEM((tm, LANES), F32)],
        compiler_params=_cparams(("arbitrary", "arbitrary")),
        name="nsa_prep",
    )(nq, nkv, cs, *_rope_placement(), gq, gk, _block_diag_ones2(NSA_Q_W), _block_diag_ones2(NSA_KV_W))


SEG_W = CMP_STRIDE * NSA_DH


def _cmp_kernel(xk_ref, xv_ref, pe_ref, w1_ref, w2_ref, gain_ref, cs_ref, place_ref, const_ref, bd_ref, ko_ref, vo_ref):
    def compress(x_ref, kv):
        out = jnp.zeros((N_CMP_PAD, LANES), F32)
        for g in range(NSA_KV_GROUPS):
            x = x_ref[0, g]
            ha = jnp.dot(split3_keys(x + pe_ref[kv, 0]), w1_ref[kv, 0], preferred_element_type=F32)
            hb = jnp.dot(split3_keys(x + pe_ref[kv, 1]), w1_ref[kv, 1], preferred_element_type=F32)
            hid = ha + pltpu.roll(hb, N_CMP_PAD - 1, 0)
            out += jnp.dot(jax.nn.gelu(hid), w2_ref[kv, g], precision=HI, preferred_element_type=F32)
        return out

    c, sm, sp = _lane_tables(cs_ref[...], place_ref, const_ref)
    ko_ref[0] = _head_norm_rope(compress(xk_ref, 0), gain_ref[...], bd_ref[...], c, sm, sp)
    vo_ref[0] = compress(xv_ref, 1)


def nsa_compress(xk, xv, cmp_pe, cmp_w1, cmp_w2, k_gain0, cs_last):
    pe = cmp_pe.reshape(2, 2, 1, SEG_W)
    w1 = cmp_w1.reshape(2, 2, SEG_W, CMP_HIDDEN)
    w1_hi = w1.astype(BF16)
    w1 = jnp.concatenate([w1_hi, w1_hi, (w1 - w1_hi.astype(F32)).astype(BF16)], axis=2)
    w2 = jnp.zeros((2, NSA_KV_GROUPS, CMP_HIDDEN, LANES), F32)
    for g in range(NSA_KV_GROUPS):
        w2 = w2.at[:, g, :, g * NSA_DH:(g + 1) * NSA_DH].set(cmp_w2)
    seg = pl.BlockSpec((1, NSA_KV_GROUPS, N_CMP_PAD, SEG_W), lambda b: (b, 0, 0, 0))
    tab = pl.BlockSpec((1, N_CMP_PAD, LANES), lambda b: (b, 0, 0))
    full = lambda shape: pl.BlockSpec(shape, lambda b: (0,) * len(shape))
    return pl.pallas_call(
        _cmp_kernel,
        grid=(BATCH,),
        in_specs=[seg, seg, full((2, 2, 1, SEG_W)), full((2, 2, 3 * SEG_W, CMP_HIDDEN)),
                  full((2, NSA_KV_GROUPS, CMP_HIDDEN, LANES)), full((1, LANES)),
                  pl.BlockSpec((ROT_DIM, N_CMP_PAD), lambda b: (0, b)), full((3 * LANES, ROT_DIM)),
                  full((1, 3 * LANES)), full((2 * LANES, LANES))],
        out_specs=[tab, tab],
        out_shape=[jax.ShapeDtypeStruct((BATCH, N_CMP_PAD, LANES), F32)] * 2,
        compiler_params=_cparams(("arbitrary",)),
        name="nsa_compress",
    )(xk, xv, pe, w1, w2, jnp.tile(k_gain0, NSA_KV_GROUPS).reshape(1, LANES), cs_last, *_rope_placement(),
      _block_diag_ones2(LANES))


CA_TQ = 512
CA_COLS = NSA_HPG * CA_TQ
CMP_PER_SEL = SEL_BLOCK // CMP_STRIDE
TOPK_BANDS = 4


def split3_keys(k):
    hi = k.astype(BF16)
    lo = (k - hi.astype(F32)).astype(BF16)
    return jnp.concatenate([hi, lo, hi], axis=-1)


def _top_k_rows(score, k):
    rows, cols = score.shape
    row = lax.broadcasted_iota(jnp.int32, (rows, cols), 0).astype(F32)
    taken = jnp.zeros((rows, cols), F32)
    left = score
    for _ in range(k):
        top = jnp.max(left, axis=0, keepdims=True)
        first = jnp.min(jnp.where(left == top, row, float(rows)), axis=0, keepdims=True)
        hit = row == first
        taken = jnp.where(hit, 1.0, taken)
        left = jnp.where(hit, -jnp.inf, left)
    return taken


def _cattn_kernel(q_ref, kc_ref, vct_ref, gl_ref, o_ref, sel_ref, q3_ref, ps_ref):
    tq = CA_TQ
    q0 = pl.program_id(1) * tq
    lanes4 = lambda t: jnp.concatenate([t] * NSA_HPG, axis=1)
    cend = lax.broadcasted_iota(jnp.int32, (N_CMP_PAD, tq), 0) * CMP_STRIDE + (CMP_LEN - 1)
    tc = q0 + lax.broadcasted_iota(jnp.int32, (N_CMP_PAD, tq), 1)
    cmask = lanes4(cend <= tc)
    jj = lax.broadcasted_iota(jnp.int32, (N_SEL, tq), 0)
    tt = q0 + lax.broadcasted_iota(jnp.int32, (N_SEL, tq), 1)
    cur = jnp.right_shift(tt, 6)
    forced = (jj == 0) | (jj == cur) | (jj == cur - 1)
    valid = jj * SEL_BLOCK <= tt

    for g in range(NSA_KV_GROUPS):
        heads = range(g * NSA_HPG, (g + 1) * NSA_HPG)
        for n, h in enumerate(heads):
            q = q_ref[0, h]
            hi = q.astype(BF16)
            lo = (q - hi.astype(F32)).astype(BF16)
            for t, part in enumerate((hi, hi, lo)):
                q3_ref[g, t * NSA_DH:(t + 1) * NSA_DH, n * tq:(n + 1) * tq] = part
        s = jnp.dot(kc_ref[0, g], q3_ref[g], preferred_element_type=F32)
        s = jnp.where(cmask, s, NEG_INF)
        m = jnp.max(s, axis=0, keepdims=True)
        e = jnp.where(cmask, jnp.exp(s - m), 0.0)
        l = jnp.sum(e, axis=0, keepdims=True)
        p = e / jnp.where(l > 0.0, l, 1.0)
        gate = jnp.concatenate([jax.nn.sigmoid(gl_ref[0, h, 0:1, :]) for h in heads], axis=1)
        o = jnp.dot(vct_ref[0, g], p.astype(BF16), preferred_element_type=F32) * gate
        for n, h in enumerate(heads):
            o_ref[0, h] = o[:, n * tq:(n + 1) * tq]
        psum = functools.reduce(jnp.add, [p[:, n * tq:(n + 1) * tq] for n in range(NSA_HPG)])
        for n in range(tq // LANES):
            ps_ref[g, n] = psum[:, n * LANES:(n + 1) * LANES]

        every4th = lambda r: jnp.concatenate(
            [ps_ref[g, n, pl.ds(r, N_SEL, stride=CMP_PER_SEL), :] for n in range(tq // LANES)], axis=1)
        starts_in = [every4th(r) for r in range(CMP_PER_SEL)]
        from_prev = jnp.where(jj >= 1, pltpu.roll(starts_in[CMP_PER_SEL - 1], 1, 0), 0.0)
        imp = functools.reduce(jnp.add, starts_in) + from_prev
        score = jnp.where(valid, imp + jnp.where(forced, FORCE_BONUS, 0.0), NEG_INF)
        step = pl.program_id(1)
        steps_per_band = (SEQ // tq) // TOPK_BANDS
        for band in range(TOPK_BANDS):
            n_rows = (band + 1) * (N_SEL // TOPK_BANDS)

            @pl.when((step >= band * steps_per_band) & (step < (band + 1) * steps_per_band))
            def _():
                taken = _top_k_rows(score[0:n_rows], SEL_TOPK)
                sel_ref[0, g, 0:n_rows, :] = jnp.where(valid[0:n_rows], taken, 0.0)
                if n_rows < N_SEL:
                    sel_ref[0, g, n_rows:N_SEL, :] = jnp.zeros((N_SEL - n_rows, tq), F32)


def nsa_cmp_attn(q_t, kcmp, vcmp_t, gl_t):
    tq = CA_TQ
    G = NSA_KV_GROUPS
    return pl.pallas_call(
        _cattn_kernel,
        grid=(BATCH, SEQ // tq),
        in_specs=[
            pl.BlockSpec((1, NSA_HEADS, NSA_DH, tq), lambda b, i: (b, 0, 0, i)),
            pl.BlockSpec((1, G, N_CMP_PAD, 3 * NSA_DH), lambda b, i: (b, 0, 0, 0)),
            pl.BlockSpec((1, G, NSA_DH, N_CMP_PAD), lambda b, i: (b, 0, 0, 0)),
            pl.BlockSpec((1, NSA_HEADS, 3, tq), lambda b, i: (b, 0, 0, i)),
        ],
        out_specs=[pl.BlockSpec((1, NSA_HEADS, NSA_DH, tq), lambda b, i: (b, 0, 0, i)),
                   pl.BlockSpec((1, G, N_SEL, tq), lambda b, i: (b, 0, 0, i))],
        out_shape=[jax.ShapeDtypeStruct((BATCH, NSA_HEADS, NSA_DH, SEQ), F32),
                   jax.ShapeDtypeStruct((BATCH, G, N_SEL, SEQ), F32)],
        scratch_shapes=[pltpu.VMEM((G, 3 * NSA_DH, CA_COLS), BF16), pltpu.VMEM((G, tq // LANES, N_CMP_PAD, LANES), F32)],
        compiler_params=_cparams(("arbitrary", "arbitrary")),
        name="nsa_cmp_attn",
    )(q_t, split3_keys(kcmp), vcmp_t.astype(BF16), gl_t)


SA_TQ = 256
SA_TK = 1024
SA_PARTS = 2
SA_PART = SA_TK // SA_PARTS
M_INIT = -1e20


SA_COLS = NSA_HPG * SA_TQ
SA_BLOCKS = SA_TK // SEL_BLOCK


VT_ROWS = NSA_DH + 16


def _sattn_kernel(q_ref, k_ref, vt_ref, sel_ref, gl_ref, kw_ref, vw_ref, prev_ref, o_ref, qa_ref, acc_ref, s_ref,
                  m_ref):
    tq, tk = SA_TQ, SA_TK
    i = pl.program_id(1)
    groups = range(NSA_KV_GROUPS)
    slots = range(2)
    for g in groups:
        for h in range(NSA_HPG):
            q = q_ref[0, g * NSA_HPG + h].astype(BF16)
            for slot in slots:
                qa_ref[slot, g, 0:NSA_DH, h * tq:(h + 1) * tq] = q
        for slot in slots:
            qa_ref[slot, g, NSA_DH:LANES, :] = jnp.zeros((LANES - NSA_DH, SA_COLS), BF16)
    acc_ref[...] = jnp.zeros_like(acc_ref)
    lanes4 = lambda t: jnp.concatenate([t] * NSA_HPG, axis=1)
    part_keys = lambda kt, part: pl.ds(pl.multiple_of(kt * tk + part * SA_PART, SA_PART), SA_PART)

    def scores(kt, slot):
        for g in groups:
            selrows = sel_ref[0, g, pl.ds(pl.multiple_of(kt * SA_BLOCKS, SA_BLOCKS), SA_BLOCKS), :]
            qa_ref[slot, g, NSA_DH:NSA_DH + SA_BLOCKS, :] = lanes4(jnp.where(selrows > 0.5, 0.0, NEG_INF)).astype(BF16)
            for part in range(SA_PARTS):
                s = jnp.dot(k_ref[0, g, part_keys(kt, part), :], qa_ref[slot, g], preferred_element_type=F32)
                s_ref[slot, g, part] = s.astype(BF16)

    def absorb(kt, slot, ms):
        out = []
        for g in groups:
            ss = [s_ref[slot, g, part] for part in range(SA_PARTS)]
            m_tile = functools.reduce(jnp.maximum, [jnp.max(s, axis=0, keepdims=True) for s in ss])
            m_new = jnp.maximum(ms[g], m_tile.astype(F32))
            acc = jnp.exp(ms[g] - m_new) * acc_ref[g]
            for part in range(SA_PARTS):
                p = jnp.exp(ss[part] - m_new.astype(BF16))
                acc += jnp.dot(vt_ref[0, g, :, part_keys(kt, part)], p, preferred_element_type=F32)
            acc_ref[g] = acc
            out.append(m_new)
        return tuple(out)

    def two_tiles(j, ms):
        kt = 2 * j
        scores(kt + 1, 1)
        ms = absorb(kt, 0, ms)
        scores(kt + 2, 0)
        return absorb(kt + 1, 1, ms)

    n_full = (i * tq) // tk
    scores(0, 0)
    m0 = tuple(jnp.full((1, SA_COLS), M_INIT, F32) for _ in groups)
    ms = lax.fori_loop(0, n_full // 2, two_tiles, m0)
    for g in groups:
        m_ref[g] = ms[g]

    def last_tile(slot):
        start = i * tq - n_full * tk
        part, row0 = start // SA_PART, pl.multiple_of(start % SA_PART, tq)
        tri = lax.broadcasted_iota(jnp.int32, (tq, tq), 0) <= lax.broadcasted_iota(jnp.int32, (tq, tq), 1)
        bias = lanes4(jnp.where(tri, 0.0, NEG_INF)).astype(BF16)
        for g in groups:
            s_ref[slot, g, part, pl.ds(row0, tq), :] += bias
        for g, m in enumerate(absorb(n_full, slot, tuple(m_ref[g] for g in groups))):
            m_ref[g] = m

    @pl.when(n_full % 2 == 0)
    def _():
        last_tile(0)

    @pl.when(n_full % 2 == 1)
    def _():
        scores(n_full, 1)
        for g, m in enumerate(absorb(n_full - 1, 0, tuple(m_ref[g] for g in groups))):
            m_ref[g] = m
        last_tile(1)

    w_rows = WINDOW + tq
    w_start = jnp.maximum(i * tq - WINDOW, 0)
    w_keys = pl.ds(pl.multiple_of(w_start, tq), w_rows)
    behind = (i * tq + lax.broadcasted_iota(jnp.int32, (w_rows, tq), 1)
              - (w_start + lax.broadcasted_iota(jnp.int32, (w_rows, tq), 0)))
    w_bias = lanes4(jnp.where((behind >= 0) & (behind < WINDOW), 0.0, NEG_INF))
    for g in groups:
        heads = range(g * NSA_HPG, (g + 1) * NSA_HPG)
        gates = lambda branch: jnp.concatenate([jax.nn.sigmoid(gl_ref[0, h, branch:branch + 1, :]) for h in heads], axis=1)
        out = acc_ref[g, 0:NSA_DH, :] / acc_ref[g, NSA_DH:NSA_DH + 1, :] * gates(1)
        s = (jnp.dot(kw_ref[0, g, w_keys, :], qa_ref[0, g], preferred_element_type=F32) + w_bias).astype(BF16)
        p = jnp.exp(s - jnp.max(s, axis=0, keepdims=True))
        acc_w = jnp.dot(vw_ref[0, g, :, w_keys], p, preferred_element_type=F32)
        out += acc_w[0:NSA_DH] / acc_w[NSA_DH:NSA_DH + 1] * gates(2)
        for n, h in enumerate(heads):
            o_ref[0, h] = prev_ref[0, h] + out[:, n * tq:(n + 1) * tq]


def nsa_sel_win_attn(q_t, k_slab, vsel_t, sel_t, gl_t, kwin, vwin_t, prev):
    tq = SA_TQ
    G = NSA_KV_GROUPS
    ospec = pl.BlockSpec((1, NSA_HEADS, NSA_DH, tq), lambda b, i: (b, 0, 0, i))
    kspec = pl.BlockSpec((1, G, SEQ, LANES), lambda b, i: (b, 0, 0, 0))
    vspec = pl.BlockSpec((1, G, VT_ROWS, SEQ), lambda b, i: (b, 0, 0, 0))
    return pl.pallas_call(
        _sattn_kernel,
        grid=(BATCH, SEQ // tq),
        in_specs=[
            ospec, kspec, vspec,
            pl.BlockSpec((1, G, N_SEL, tq), lambda b, i: (b, 0, 0, i)),
            pl.BlockSpec((1, NSA_HEADS, 3, tq), lambda b, i: (b, 0, 0, i)),
            kspec, vspec, ospec,
        ],
        out_specs=ospec,
        out_shape=jax.ShapeDtypeStruct((BATCH, NSA_HEADS, NSA_DH, SEQ), F32),
        scratch_shapes=[pltpu.VMEM((2, G, LANES, SA_COLS), BF16), pltpu.VMEM((G, VT_ROWS, SA_COLS), F32),
                        pltpu.VMEM((2, G, SA_PARTS, SA_PART, SA_COLS), BF16), pltpu.VMEM((G, 1, SA_COLS), F32)],
        input_output_aliases={7: 0},
        compiler_params=_cparams(("arbitrary", "arbitrary"), vmem_mib=56),
        name="nsa_sel_win_attn",
    )(q_t, k_slab, vsel_t, sel_t, gl_t, kwin, vwin_t, prev)


def nsa_mixer(nq, nkv, misc, positions, q_gain, k_gain, cmp_pe, cmp_w1, cmp_w2):
    cs = rope_tables(positions)
    q_t, ksel, kwin, vsel_t, vwin_t, xk, xv = nsa_prep(nq, nkv, cs, q_gain, k_gain)
    last = jnp.minimum(jnp.arange(N_CMP_PAD) * CMP_STRIDE + CMP_LEN - 1, SEQ - 1)
    last = (jnp.arange(BATCH)[:, None] * SEQ + last[None, :]).reshape(-1)
    kcmp, vcmp = nsa_compress(xk, xv, cmp_pe, cmp_w1, cmp_w2, k_gain[0], cs[:, last])
    kcmp = kcmp.reshape(BATCH, N_CMP_PAD, NSA_KV_GROUPS, NSA_DH).transpose(0, 2, 1, 3)
    vcmp_t = vcmp.reshape(BATCH, N_CMP_PAD, NSA_KV_GROUPS, NSA_DH).transpose(0, 2, 3, 1)
    gl_t = misc[..., GLA_GATE_RANK:GLA_GATE_RANK + NSA_HEADS * 3].reshape(BATCH, SEQ, NSA_HEADS, 3).transpose(0, 2, 3, 1)
    o_t, sel_t = nsa_cmp_attn(q_t, kcmp, vcmp_t, gl_t)
    return nsa_sel_win_attn(q_t, ksel, vsel_t, sel_t, gl_t, kwin, vwin_t, o_t)


ROUTE_ROWS = 8
HX_TERMS = 3
HX_W = D_MODEL + LANES


def _top2_sum(a, b, c, d):
    hi1, lo1 = jnp.maximum(a, b), jnp.minimum(a, b)
    hi2, lo2 = jnp.maximum(c, d), jnp.minimum(c, d)
    return jnp.maximum(hi1, hi2) + jnp.maximum(jnp.minimum(hi1, hi2), jnp.maximum(lo1, lo2))


def _moe_prenorm_route(xn, g_ref, sh_ref, sc_ref, wr_ref, rb_ref, hx_ref, route_ref):
    h = _rms_mod(xn, g_ref[...], sh_ref[0], sc_ref[0])
    logits = lax.dot_general(wr_ref[...], h, NT_DIMS, precision=HI, preferred_element_type=F32)
    scores = jax.nn.sigmoid(logits)
    sel = scores + rb_ref[...]
    epg = EXPERTS_PER_GROUP
    srow = lambda e: sel[e:e + 1, :]
    grp = [_top2_sum(*[srow(epg * g + r) for r in range(epg)]) for g in range(N_EXPERT_GROUPS)]
    best, gi = grp[0], jnp.zeros_like(grp[0], dtype=jnp.int32)
    for g in range(1, N_EXPERT_GROUPS):
        better = grp[g] > best
        gi = jnp.where(better, g, gi)
        best = jnp.where(better, grp[g], best)

    def in_group(mat, r):
        out = mat[r:r + 1, :]
        for g in range(1, N_EXPERT_GROUPS):
            out = jnp.where(gi == g, mat[epg * g + r:epg * g + r + 1, :], out)
        return out

    v = [in_group(sel, r) for r in range(epg)]
    sc = [in_group(scores, r) for r in range(epg)]
    b1, i1, w1 = v[0], jnp.zeros_like(gi), sc[0]
    for r in range(1, epg):
        better = v[r] > b1
        i1 = jnp.where(better, r, i1)
        w1 = jnp.where(better, sc[r], w1)
        b1 = jnp.where(better, v[r], b1)
    b2 = jnp.full_like(b1, -3e38)
    i2, w2 = jnp.zeros_like(gi), jnp.zeros_like(w1)
    for r in range(epg):
        better = (i1 != r) & (v[r] > b2)
        i2 = jnp.where(better, r, i2)
        w2 = jnp.where(better, sc[r], w2)
        b2 = jnp.where(better, v[r], b2)
    tot = w1 + w2
    w1, w2 = w1 / tot, w2 / tot
    zero = jnp.zeros_like(w1)
    route_ref[0] = jnp.concatenate([gi.astype(F32)] + [zero] * (ROUTE_ROWS - 1), axis=0)
    w = jnp.concatenate([jnp.where(i1 == r, w1, jnp.where(i2 == r, w2, 0.0)) for r in range(epg)], axis=0)
    w_hi = w.astype(BF16).astype(F32)
    w_mid = (w - w_hi).astype(BF16).astype(F32)
    w_lo = (w - w_hi - w_mid).astype(BF16).astype(F32)
    pad = jnp.zeros((LANES - HX_TERMS * epg, w.shape[1]), F32)
    hx_ref[0, :, 0:D_MODEL] = h.astype(BF16)
    hx_ref[0, :, D_MODEL:HX_W] = jnp.concatenate([w_hi, w_mid, w_lo, pad], axis=0).T.astype(BF16)


def _route_specs(tm, row, vec, const):
    in_specs = [pl.BlockSpec((1, D_MODEL), const), pl.BlockSpec((1, 1, D_MODEL), vec), pl.BlockSpec((1, 1, D_MODEL), vec),
                pl.BlockSpec((N_EXPERTS, D_MODEL), const), pl.BlockSpec((N_EXPERTS, 1), const)]
    assert tm == MOE_CHUNK
    out_specs = [pl.BlockSpec((1, tm, HX_W), lambda b, i: (b * (SEQ // tm) + i, 0, 0)),
                 pl.BlockSpec((1, ROUTE_ROWS, tm), lambda b, i: (b, 0, i))]
    out_shape = [jax.ShapeDtypeStruct((MOE_CHUNKS, MOE_CHUNK, HX_W), BF16),
                 jax.ShapeDtypeStruct((BATCH, ROUTE_ROWS, SEQ), F32)]
    return in_specs, out_specs, out_shape


def _route_args(g, shift, scale, w_router, router_bias):
    return (g.reshape(1, D_MODEL), shift.reshape(BATCH, 1, D_MODEL), scale.reshape(BATCH, 1, D_MODEL),
            w_router.T, router_bias.reshape(N_EXPERTS, 1))


OUTPROJ_TM = 512


def _outproj0_kernel(oa_ref, ob_ref, w_ref, x_ref, gate_ref, g_ref, sh_ref, sc_ref, wr_ref, rb_ref,
                     xo_ref, h_ref, route_ref):
    y = jnp.dot(oa_ref[0].astype(BF16), w_ref[0:GLA_V_W, :], preferred_element_type=F32)
    ob_t = ob_ref[0].reshape(NSA_Q_W, OUTPROJ_TM).astype(BF16)
    y += lax.dot_general(ob_t, w_ref[GLA_V_W:GLA_V_W + NSA_Q_W, :], TN_DIMS, preferred_element_type=F32)
    xn = x_ref[0] + gate_ref[0] * y
    xo_ref[0] = xn
    _moe_prenorm_route(xn, g_ref, sh_ref, sc_ref, wr_ref, rb_ref, h_ref, route_ref)


def outproj0(o_a, o_b, w_out, x, gate, route_args):
    tm = OUTPROJ_TM
    row = lambda b, i: (b, i, 0)
    vec = lambda b, i: (b, 0, 0)
    const = lambda b, i: (0, 0)
    r_in, r_out, r_shape = _route_specs(tm, row, vec, const)
    return pl.pallas_call(
        _outproj0_kernel,
        grid=(BATCH, SEQ // tm),
        in_specs=[pl.BlockSpec((1, tm, GLA_V_W), row), pl.BlockSpec((1, NSA_HEADS, NSA_DH, tm), lambda b, i: (b, 0, 0, i)),
                  pl.BlockSpec((GLA_V_W + NSA_Q_W, D_MODEL), const), pl.BlockSpec((1, tm, D_MODEL), row),
                  pl.BlockSpec((1, 1, D_MODEL), vec)] + r_in,
        out_specs=[pl.BlockSpec((1, tm, D_MODEL), row)] + r_out,
        out_shape=[jax.ShapeDtypeStruct((BATCH, SEQ, D_MODEL), F32)] + r_shape,
        compiler_params=_cparams(("arbitrary", "arbitrary")),
        name="outproj0",
    )(o_a, o_b, w_out.astype(BF16), x, gate.reshape(BATCH, 1, D_MODEL), *route_args)


GMLP_TM = 512


def _gmlp_kernel(x_ref, g1_ref, sh1_ref, sc1_ref, win_ref, ng_ref, ws_ref, bs_ref, wout_ref, gate_ref,
                 g_ref, sh_ref, sc_ref, wr_ref, rb_ref, xo_ref, h_ref, route_ref, gated_ref, v_ref):
    x = x_ref[0]
    h = _rms_mod(x, g1_ref[...], sh1_ref[0], sc1_ref[0]).astype(BF16)
    group_cols = lambda g: slice(g * SGU_GROUP_DIM, (g + 1) * SGU_GROUP_DIM)
    ssq = jnp.zeros((GMLP_TM, LANES), F32)
    for g in range(SGU_GROUPS):
        lo = SGU_WIDTH + g * SGU_GROUP_DIM
        v = jax.nn.gelu(jnp.dot(h, win_ref[:, lo:lo + SGU_GROUP_DIM], preferred_element_type=F32))
        v_ref[:, group_cols(g)] = v
        ssq += functools.reduce(jnp.add, [v[:, n * LANES:(n + 1) * LANES] ** 2 for n in range(SGU_GROUP_DIM // LANES)])
    rs = lax.rsqrt(jnp.sum(ssq, axis=-1, keepdims=True) * (1.0 / SGU_WIDTH) + NORM_EPS)
    ri = lax.broadcasted_iota(jnp.int32, (SGU_CHUNK, SGU_CHUNK), 0)
    ci = lax.broadcasted_iota(jnp.int32, (SGU_CHUNK, SGU_CHUNK), 1)
    for g in range(SGU_GROUPS):
        cols = group_cols(g)
        u = jax.nn.gelu(jnp.dot(h, win_ref[:, cols], preferred_element_type=F32))
        vn = (v_ref[:, cols] * rs * ng_ref[:, cols]).astype(BF16)
        w = jnp.where(ri >= ci, ws_ref[g], 0.0).astype(BF16)
        for c in range(GMLP_TM // SGU_CHUNK):
            rows = slice(c * SGU_CHUNK, (c + 1) * SGU_CHUNK)
            mix = jnp.dot(w, vn[rows], preferred_element_type=F32) + bs_ref[:, g:g + 1]
            gated_ref[rows, cols] = (u[rows] * mix).astype(BF16)
    y = jnp.dot(gated_ref[...], wout_ref[...], preferred_element_type=F32)
    xn = x + gate_ref[0] * y
    xo_ref[0] = xn
    _moe_prenorm_route(xn, g_ref, sh_ref, sc_ref, wr_ref, rb_ref, h_ref, route_ref)


def gmlp_layer(x, g1, shift1, scale1, w_in, norm_g, w_s, b_s, w_out, gate, route_args):
    tm = GMLP_TM
    row = lambda b, i: (b, i, 0)
    vec = lambda b, i: (b, 0, 0)
    const = lambda b, i: (0, 0)
    r_in, r_out, r_shape = _route_specs(tm, row, vec, const)
    vspec = pl.BlockSpec((1, 1, D_MODEL), vec)
    return pl.pallas_call(
        _gmlp_kernel,
        grid=(BATCH, SEQ // tm),
        in_specs=[pl.BlockSpec((1, tm, D_MODEL), row), pl.BlockSpec((1, D_MODEL), const), vspec, vspec,
                  pl.BlockSpec((D_MODEL, 2 * SGU_WIDTH), const), pl.BlockSpec((1, SGU_WIDTH), const),
                  pl.BlockSpec((SGU_GROUPS, SGU_CHUNK, SGU_CHUNK), lambda b, i: (0, 0, 0)),
                  pl.BlockSpec((SGU_CHUNK, SGU_GROUPS), const), pl.BlockSpec((SGU_WIDTH, D_MODEL), const), vspec] + r_in,
        out_specs=[pl.BlockSpec((1, tm, D_MODEL), row)] + r_out,
        out_shape=[jax.ShapeDtypeStruct((BATCH, SEQ, D_MODEL), F32)] + r_shape,
        scratch_shapes=[pltpu.VMEM((tm, SGU_WIDTH), BF16), pltpu.VMEM((tm, SGU_WIDTH), F32)],
        compiler_params=_cparams(("arbitrary", "arbitrary"), vmem_mib=56),
        name="gmlp_layer",
    )(x, g1.reshape(1, D_MODEL), shift1.reshape(BATCH, 1, D_MODEL), scale1.reshape(BATCH, 1, D_MODEL),
      w_in.astype(BF16), norm_g.reshape(1, SGU_WIDTH), w_s, b_s.T, w_out.astype(BF16),
      gate.reshape(BATCH, 1, D_MODEL), *route_args)


MOE_TM = 256
MOE_CHUNK = 512
MOE_SORTED = N_TOK + N_EXPERT_GROUPS * MOE_TM
MOE_TILES = MOE_SORTED // MOE_TM
MOE_CHUNKS = N_TOK // MOE_CHUNK
MOE_PAIRS = MOE_TILES + N_EXPERT_GROUPS * MOE_CHUNKS
FLAG_ACTIVE, FLAG_FIRST, FLAG_LAST, FLAG_ZERO = 1, 2, 4, 8
EXP_WIN = 4
CMB_WIN = 8
MOE_TSTEPS = MOE_TILES + MOE_PAIRS // EXP_WIN
MOE_CSTEPS = MOE_CHUNKS + MOE_PAIRS // CMB_WIN


def _plan_kernel(gi_ref, rank_ref, before_ref):
    gi = gi_ref[...]
    r = lax.broadcasted_iota(jnp.int32, (POS_SIDE, POS_SIDE), 0)
    c = lax.broadcasted_iota(jnp.int32, (POS_SIDE, POS_SIDE), 1)
    upper = jnp.where(r <= c, 1.0, 0.0)
    lower_strict = jnp.where(c < r, 1.0, 0.0)
    rank = jnp.zeros((POS_SIDE, POS_SIDE), F32)
    for g in range(N_EXPERT_GROUPS):
        member = jnp.where(gi == g, 1.0, 0.0)
        in_row = jnp.dot(member, upper, precision=HI, preferred_element_type=F32)
        row_total = jnp.broadcast_to(in_row[:, POS_SIDE - 1:POS_SIDE], (POS_SIDE, POS_SIDE))
        before = jnp.dot(lower_strict, row_total, precision=HI, preferred_element_type=F32)
        before_ref[g] = before
        rank += member * (before + in_row - 1.0)
    rank_ref[...] = rank


def moe_plan(route):
    tm = MOE_TM
    i32 = jnp.int32
    gi_f = route[:, 0, :].reshape(POS_SIDE, POS_SIDE)
    rank, before = pl.pallas_call(
        _plan_kernel,
        out_shape=[jax.ShapeDtypeStruct((POS_SIDE, POS_SIDE), F32),
                   jax.ShapeDtypeStruct((N_EXPERT_GROUPS, POS_SIDE, POS_SIDE), F32)],
        name="moe_plan",
    )(gi_f)
    gi = gi_f.reshape(N_TOK).astype(i32)
    groups = jnp.arange(N_EXPERT_GROUPS, dtype=i32)
    member = gi[None, :] == groups[:, None]
    tot = jnp.sum(member, axis=1).astype(i32)
    padded = (tot + tm - 1) // tm * tm
    gend = jnp.cumsum(padded).astype(i32)
    gstart = gend - padded
    pos = jnp.sum(jnp.where(member, gstart[:, None], 0), axis=0).astype(i32) + rank.reshape(N_TOK).astype(i32)
    rows_per_chunk = MOE_CHUNK // POS_SIDE
    cnt_end = jnp.concatenate([before[:, rows_per_chunk::rows_per_chunk, 0].astype(i32), tot[:, None]], axis=1)
    t = jnp.arange(MOE_TILES, dtype=i32)
    n_used = gend[-1] // tm
    tile_g = jnp.minimum(jnp.sum(gend[None, :] <= (t * tm)[:, None], axis=1), N_EXPERT_GROUPS - 1).astype(i32)
    k0 = t * tm - gstart[tile_g]
    k1 = jnp.minimum(k0 + tm, tot[tile_g]) - 1
    ce = cnt_end[tile_g]
    c_lo = jnp.sum(ce <= k0[:, None], axis=1).astype(i32)
    c_hi = jnp.sum(ce <= k1[:, None], axis=1).astype(i32)
    npairs = jnp.where(t < n_used, c_hi - c_lo + 1, 0)
    total = jnp.sum(npairs).astype(i32)
    lt = jnp.minimum(jnp.arange(MOE_PAIRS, dtype=i32), total - 1)

    def windows(count, win, n_steps):
        per_item = (count + win - 1) // win
        end = jnp.cumsum(per_item).astype(i32)
        start = end - per_item
        s = jnp.arange(n_steps, dtype=i32)
        real_s = s < end[-1]
        sc = jnp.minimum(s, end[-1] - 1)
        item = jnp.sum(end[None, :] <= sc[:, None], axis=1).astype(i32)
        j = sc - start[item]
        flags_s = jnp.where(real_s, FLAG_ACTIVE + jnp.where(j == 0, FLAG_FIRST, 0)
                            + jnp.where(j == per_item[item] - 1, FLAG_LAST, 0), 0).astype(i32)
        return item, j, flags_s, real_s, s - end[-1], end[-1]

    tile_s, j, flags, real_s, spare, steps_t = windows(npairs, EXP_WIN, MOE_TSTEPS)
    steps_t = steps_t + (MOE_TILES - n_used)
    c0 = c_lo[tile_s] + EXP_WIN * j
    n_valid = jnp.minimum(EXP_WIN, c_hi[tile_s] - c0 + 1).astype(i32)
    spare_tile = jnp.minimum(n_used + spare, MOE_TILES - 1)
    flags = jnp.where(real_s, flags, jnp.where(spare_tile >= n_used, FLAG_ZERO, 0)).astype(i32)
    tile_sched = jnp.where(real_s, tile_s, spare_tile).astype(i32)
    by_tile = (tile_sched, c0.astype(i32), n_valid, flags, tile_g[tile_sched])
    cc = jnp.arange(MOE_CHUNKS, dtype=i32)
    is_pair = (cc[:, None] >= c_lo[None, :]) & (cc[:, None] <= c_hi[None, :]) & (t[None, :] < n_used)
    seen = jnp.cumsum(is_pair.reshape(-1).astype(i32))
    flat = jnp.sum(seen[None, :] <= lt[:, None], axis=1).astype(i32)
    pair_tile = flat % MOE_TILES
    per_chunk = jnp.sum(is_pair, axis=1).astype(i32)
    first_pair = jnp.cumsum(per_chunk).astype(i32) - per_chunk
    chunk_s, j, flags_c, _, _, steps_c = windows(per_chunk, CMB_WIN, MOE_CSTEPS)
    base = first_pair[chunk_s] + CMB_WIN * j
    n_valid_c = jnp.minimum(CMB_WIN, per_chunk[chunk_s] - CMB_WIN * j).astype(i32)
    tiles_c = tuple(pair_tile[jnp.minimum(base + w, total - 1)] for w in range(CMB_WIN))
    by_chunk = (chunk_s, n_valid_c, flags_c) + tiles_c
    return pos.reshape(MOE_CHUNKS, 1, MOE_CHUNK), (steps_t, by_tile), (steps_c, by_chunk)


def _one_hot_rows(pos_row, tile):
    rows = tile * MOE_TM + lax.broadcasted_iota(jnp.int32, (MOE_TM, MOE_CHUNK), 0)
    return jnp.where(pos_row == rows, 1.0, 0.0).astype(BF16)


def _moe_kernel(tile_ref, c0_ref, nv_ref, flag_ref, grp_ref, *refs):
    pos_refs, hx_refs = refs[0:EXP_WIN], refs[EXP_WIN:2 * EXP_WIN]
    wg_ref, wu_ref, wd_ref, y_ref, acc_ref = refs[2 * EXP_WIN:]
    l = pl.program_id(0)
    flags = flag_ref[l]

    @pl.when((flags & FLAG_FIRST) != 0)
    def _():
        acc_ref[...] = jnp.zeros_like(acc_ref)

    for w in range(EXP_WIN):
        @pl.when(((flags & FLAG_ACTIVE) != 0) & (w < nv_ref[l]))
        def _():
            onehot = _one_hot_rows(pos_refs[w][0], tile_ref[l])
            acc_ref[...] += jnp.dot(onehot, hx_refs[w][0], preferred_element_type=F32)

    @pl.when((flags & FLAG_LAST) != 0)
    def _():
        x = acc_ref[:, 0:D_MODEL].astype(BF16)
        terms = acc_ref[:, D_MODEL:HX_W]
        y = jnp.zeros((MOE_TM, D_MODEL), F32)
        for r in range(EXPERTS_PER_GROUP):
            lanes = [n * EXPERTS_PER_GROUP + r for n in range(HX_TERMS)]
            w_r = functools.reduce(jnp.add, [terms[:, c:c + 1] for c in lanes])
            gate = jnp.dot(x, wg_ref[0, 0, r], preferred_element_type=F32)
            up = jnp.dot(x, wu_ref[0, 0, r], preferred_element_type=F32)
            hid = (_silu(gate) * up * w_r).astype(BF16)
            y += jnp.dot(hid, wd_ref[0, 0, r], preferred_element_type=F32)
        y_ref[...] = y

    @pl.when((flags & FLAG_ZERO) != 0)
    def _():
        y_ref[...] = jnp.zeros_like(y_ref)


def moe_experts(hx, pos, schedule, w_gate, w_up, w_down, layer):
    n_steps, by_tile = schedule
    grouped = lambda w: w.reshape(DEPTH, N_EXPERT_GROUPS, EXPERTS_PER_GROUP, *w.shape[2:])
    wspec = lambda k, n: pl.BlockSpec((1, 1, EXPERTS_PER_GROUP, k, n), lambda l, t, c, n_, f, g: (layer, g[l], 0, 0, 0))
    chunk = lambda w: (lambda l, t, c, n_, f, g: (jnp.minimum(c[l] + w, MOE_CHUNKS - 1), 0, 0))
    return pl.pallas_call(
        _moe_kernel,
        grid_spec=pltpu.PrefetchScalarGridSpec(
            num_scalar_prefetch=5,
            grid=(n_steps,),
            in_specs=[pl.BlockSpec((1, 1, MOE_CHUNK), chunk(w)) for w in range(EXP_WIN)]
            + [pl.BlockSpec((1, MOE_CHUNK, HX_W), chunk(w)) for w in range(EXP_WIN)]
            + [wspec(D_MODEL, EXPERT_HIDDEN), wspec(D_MODEL, EXPERT_HIDDEN), wspec(EXPERT_HIDDEN, D_MODEL)],
            out_specs=pl.BlockSpec((MOE_TM, D_MODEL), lambda l, t, c, n_, f, g: (t[l], 0)),
            scratch_shapes=[pltpu.VMEM((MOE_TM, HX_W), F32)],
        ),
        out_shape=jax.ShapeDtypeStruct((MOE_SORTED, D_MODEL), F32),
        compiler_params=_cparams(("arbitrary",), vmem_mib=56),
        name="moe_experts",
    )(*by_tile, *([pos] * EXP_WIN), *([hx] * EXP_WIN), grouped(w_gate), grouped(w_up), grouped(w_down))


def _moe_combine_kernel(chunk_ref, nv_ref, flag_ref, *refs):
    tile_refs = refs[0:CMB_WIN]
    pos_ref = refs[CMB_WIN]
    y_refs = refs[CMB_WIN + 1:2 * CMB_WIN + 1]
    x_ref, gate_ref, o_ref, acc_ref = refs[2 * CMB_WIN + 1:]
    l = pl.program_id(0)
    flags = flag_ref[l]

    @pl.when((flags & FLAG_FIRST) != 0)
    def _():
        acc_ref[...] = jnp.zeros_like(acc_ref)

    for w in range(CMB_WIN):
        @pl.when(((flags & FLAG_ACTIVE) != 0) & (w < nv_ref[l]))
        def _():
            onehot = _one_hot_rows(pos_ref[0], tile_refs[w][l])
            y = y_refs[w][...]
            y_hi = y.astype(BF16)
            y_lo = (y - y_hi.astype(F32)).astype(BF16)
            acc_ref[...] += (lax.dot_general(onehot, y_hi, TN_DIMS, preferred_element_type=F32)
                             + lax.dot_general(onehot, y_lo, TN_DIMS, preferred_element_type=F32))

    @pl.when((flags & FLAG_LAST) != 0)
    def _():
        o_ref[0] = x_ref[0] + gate_ref[0] * acc_ref[...]


def moe_combine(x, y_sorted, pos, schedule, gate):
    n_steps, by_chunk = schedule
    per_b = SEQ // MOE_CHUNK
    tok = lambda l, c, *_: (c[l] // per_b, c[l] % per_b, 0)
    tile = lambda w: (lambda l, c, n_, f, *tiles: (tiles[w][l], 0))
    return pl.pallas_call(
        _moe_combine_kernel,
        grid_spec=pltpu.PrefetchScalarGridSpec(
            num_scalar_prefetch=3 + CMB_WIN,
            grid=(n_steps,),
            in_specs=[pl.BlockSpec((1, 1, MOE_CHUNK), lambda l, c, *_: (c[l], 0, 0))]
            + [pl.BlockSpec((MOE_TM, D_MODEL), tile(w)) for w in range(CMB_WIN)]
            + [pl.BlockSpec((1, MOE_CHUNK, D_MODEL), tok),
               pl.BlockSpec((1, 1, D_MODEL), lambda l, c, *_: (c[l] // per_b, 0, 0))],
            out_specs=pl.BlockSpec((1, MOE_CHUNK, D_MODEL), tok),
            scratch_shapes=[pltpu.VMEM((MOE_CHUNK, D_MODEL), F32)],
        ),
        out_shape=jax.ShapeDtypeStruct((BATCH, SEQ, D_MODEL), F32),
        compiler_params=_cparams(("arbitrary",)),
        name="moe_combine",
    )(*by_chunk, pos, *([y_sorted] * CMB_WIN), x, gate.reshape(BATCH, 1, D_MODEL))


def moe_layer(x, hx, route, gate, w_gate, w_up, w_down, layer):
    pos, by_tile, by_chunk = moe_plan(route)
    y_sorted = moe_experts(hx, pos, by_tile, w_gate, w_up, w_down, layer)
    return moe_combine(x, y_sorted, pos, by_chunk, gate)


def kernel(x, c, positions, w_ada, b_ada, norm_g, w_in_ab, w_out_ab, gla_w_gate2, gla_b_gate, gla_norm_g, nsa_q_gain, nsa_k_gain, nsa_cmp_pe, nsa_cmp_w1, nsa_cmp_w2, w_in_c, sgu_norm_g, sgu_w_s, sgu_b_s, w_out_c, w_router, router_bias, w_gate, w_up, w_down):
    mod = ada_modulation(c, w_ada, b_ada)
    qk, gv, gr, nq, nkv, misc = inproj0(x, norm_g[0, 0], mod[0, :, 0], mod[0, :, 1], _arrange_w_in(w_in_ab[0]))
    o_a = gla_mixer(qk, gv, gr, misc, gla_w_gate2[0], gla_b_gate[0], gla_norm_g[0])
    o_b = nsa_mixer(nq, nkv, misc, positions, nsa_q_gain[0], nsa_k_gain[0], nsa_cmp_pe[0], nsa_cmp_w1[0], nsa_cmp_w2[0])
    wg, wu, wd = w_gate.astype(BF16), w_up.astype(BF16), w_down.astype(BF16)
    route_args = lambda l: _route_args(norm_g[l, 1], mod[l, :, 3], mod[l, :, 4], w_router, router_bias)
    x1, h, route = outproj0(o_a, o_b, w_out_ab[0], x, mod[0, :, 2], route_args(0))
    x2 = moe_layer(x1, h, route, mod[0, :, 5], wg, wu, wd, 0)
    x3, h, route = gmlp_layer(x2, norm_g[1, 0], mod[1, :, 0], mod[1, :, 1], w_in_c[0], sgu_norm_g[0], sgu_w_s[0],
                              sgu_b_s[0], w_out_c[0], mod[1, :, 2], route_args(1))
    return moe_layer(x3, h, route, mod[1, :, 5], wg, wu, wd, 1)
```

```python
import functools

import numpy as np
import jax
import jax.numpy as jnp
from jax import lax
from jax.experimental import pallas as pl
from jax.experimental.pallas import tpu as pltpu

D_MODEL = 1024
BATCH = 2
SEQ = 8192
DEPTH = 2
N_TOK = BATCH * SEQ

GLA_HEADS = 4
GLA_DK = 64
GLA_DV = 128
GLA_GATE_RANK = 16
GLA_TAU = 16.0
GLA_CHUNK = 64
NSA_HEADS = 8
NSA_KV_GROUPS = 2
NSA_HPG = NSA_HEADS // NSA_KV_GROUPS
NSA_DH = 64
CMP_LEN = 32
CMP_STRIDE = 16
CMP_HIDDEN = 256
SEL_BLOCK = 64
SEL_TOPK = 16
WINDOW = 512
ROPE_THETA = 500000.0
ROT_DIM = NSA_DH // 4
ROT_HALF = ROT_DIM // 2
SGU_CHUNK = 128
SGU_GROUPS = 8
SGU_WIDTH = 2048
SGU_GROUP_DIM = SGU_WIDTH // SGU_GROUPS
N_EXPERTS = 16
N_EXPERT_GROUPS = 4
EXPERTS_PER_GROUP = N_EXPERTS // N_EXPERT_GROUPS
EXPERT_HIDDEN = 512

GLA_QK_W = GLA_HEADS * GLA_DK
GLA_V_W = GLA_HEADS * GLA_DV
NSA_Q_W = NSA_HEADS * NSA_DH
NSA_KV_W = NSA_KV_GROUPS * NSA_DH
N_CMP_PAD = SEQ // CMP_STRIDE
N_SEL = SEQ // SEL_BLOCK

NORM_EPS = 1e-6
NEG_INF = -1e30
FORCE_BONUS = 1e4

LANES = 128
MIB = 1024 * 1024

F32 = jnp.float32
BF16 = jnp.bfloat16
HI = lax.Precision.HIGHEST
NT_DIMS = (((1,), (1,)), ((), ()))
TN_DIMS = (((0,), (0,)), ((), ()))


def _cparams(sem, vmem_mib=48):
    return pltpu.CompilerParams(dimension_semantics=sem, vmem_limit_bytes=vmem_mib * MIB)


def _rms_mod(x, g, shift, scale):
    y = x * lax.rsqrt(jnp.mean(x * x, axis=-1, keepdims=True) + NORM_EPS) * g
    return y * (1 + scale) + shift


def _silu(x):
    return x * jax.nn.sigmoid(x)


def _log_sigmoid(z):
    return jnp.minimum(z, 0.0) - jnp.log1p(jnp.exp(-jnp.abs(z)))


ADA_TN = 1536
ADA_ROWS = 8


def _ada_kernel(c_ref, w_ref, b_ref, o_ref):
    cond = _silu(c_ref[...])
    o_ref[0] = jnp.dot(cond, w_ref[0], precision=HI, preferred_element_type=F32) + b_ref[0]


def ada_modulation(c, w_ada, b_ada):
    c8 = jnp.zeros((ADA_ROWS, D_MODEL), F32).at[:BATCH].set(c)
    width = 6 * D_MODEL
    out = pl.pallas_call(
        _ada_kernel,
        grid=(DEPTH, width // ADA_TN),
        in_specs=[
            pl.BlockSpec((ADA_ROWS, D_MODEL), lambda l, j: (0, 0)),
            pl.BlockSpec((1, D_MODEL, ADA_TN), lambda l, j: (l, 0, j)),
            pl.BlockSpec((1, 1, ADA_TN), lambda l, j: (l, 0, j)),
        ],
        out_specs=pl.BlockSpec((1, ADA_ROWS, ADA_TN), lambda l, j: (l, 0, j)),
        out_shape=jax.ShapeDtypeStruct((DEPTH, ADA_ROWS, width), F32),
        compiler_params=_cparams(("arbitrary", "arbitrary")),
        name="ada_modulation",
    )(c8, w_ada, b_ada.reshape(DEPTH, 1, width))
    return out[:, :BATCH].reshape(DEPTH, BATCH, 6, D_MODEL)


INPROJ_TM = 512
INPROJ_WIDTHS = (2 * GLA_QK_W, GLA_V_W, GLA_V_W, NSA_Q_W, 6 * NSA_KV_W, LANES)
GATE_ROWS = NSA_HEADS * 3


def _arrange_w_in(w_in):
    o = np.cumsum((0, GLA_QK_W, GLA_QK_W, GLA_V_W, GLA_GATE_RANK, GLA_V_W, NSA_Q_W, 6 * NSA_KV_W, NSA_HEADS * 3))
    gq_gk = w_in[:, o[0]:o[2]]
    gv = w_in[:, o[2]:o[3]]
    glr = w_in[:, o[3]:o[4]]
    gr = w_in[:, o[4]:o[5]]
    nq = w_in[:, o[5]:o[6]]
    nkv = w_in[:, o[6]:o[7]]
    ng = w_in[:, o[7]:o[8]]
    pad = jnp.zeros((D_MODEL, LANES - GLA_GATE_RANK - NSA_HEADS * 3), w_in.dtype)
    return jnp.concatenate([gq_gk, gv, gr, nq, nkv, glr, ng, pad], axis=1).astype(BF16)


def _inproj0_kernel(x_ref, g_ref, sh_ref, sc_ref, w_ref, *o_refs):
    *col_refs, gl_ref = o_refs
    h = _rms_mod(x_ref[0], g_ref[...], sh_ref[0], sc_ref[0]).astype(BF16)
    off = 0
    for o_ref, wd in zip(col_refs, INPROJ_WIDTHS):
        cols = jnp.dot(h, w_ref[:, off:off + wd], preferred_element_type=F32)
        o_ref[0] = cols
        off += wd
    gl_ref[0] = cols.T[GLA_GATE_RANK:GLA_GATE_RANK + GATE_ROWS]


def inproj0(x, g, shift, scale, w_arranged):
    tm = INPROJ_TM
    wtot = sum(INPROJ_WIDTHS)
    row = lambda b, i: (b, i, 0)
    vec = lambda b, i: (b, 0, 0)
    return pl.pallas_call(
        _inproj0_kernel,
        grid=(BATCH, SEQ // tm),
        in_specs=[
            pl.BlockSpec((1, tm, D_MODEL), row),
            pl.BlockSpec((1, D_MODEL), lambda b, i: (0, 0)),
            pl.BlockSpec((1, 1, D_MODEL), vec),
            pl.BlockSpec((1, 1, D_MODEL), vec),
            pl.BlockSpec((D_MODEL, wtot), lambda b, i: (0, 0)),
        ],
        out_specs=[pl.BlockSpec((1, tm, wd), row) for wd in INPROJ_WIDTHS]
        + [pl.BlockSpec((1, GATE_ROWS, tm), lambda b, i: (b, 0, i))],
        out_shape=[jax.ShapeDtypeStruct((BATCH, SEQ, wd), F32) for wd in INPROJ_WIDTHS]
        + [jax.ShapeDtypeStruct((BATCH, GATE_ROWS, SEQ), F32)],
        compiler_params=_cparams(("arbitrary", "arbitrary")),
        name="inproj0",
    )(x, g.reshape(1, D_MODEL), shift.reshape(BATCH, 1, D_MODEL), scale.reshape(BATCH, 1, D_MODEL), w_arranged)


GLA_TG = 512


def _gla_chunk_sums():
    i = np.arange(GLA_TG)[:, None]
    j = np.arange(GLA_TG)[None, :]
    same = (i // GLA_CHUNK) == (j // GLA_CHUNK)
    m3 = np.concatenate([same & (j <= i), same & (j % GLA_CHUNK <= GLA_CHUNK // 2), same], axis=0).astype(np.float32)
    return jnp.asarray(np.concatenate([m3, m3], axis=1), BF16)


def _gla_kernel(qk_ref, v_ref, r_ref, misc_ref, w2_ref, bg_ref, og_ref, sums_ref, o_ref, st_ref):
    C, tg = GLA_CHUNK, GLA_TG

    @pl.when(pl.program_id(0) == 0)
    def _():
        st_ref[...] = jnp.zeros_like(st_ref)

    lane = lax.broadcasted_iota(jnp.int32, (1, GLA_QK_W), 1)
    heads = [(lane >= h * GLA_DK) & (lane < (h + 1) * GLA_DK) for h in range(GLA_HEADS)]
    stack = lambda per_head, rows: jnp.concatenate([t[rows] for t in per_head], axis=0)
    stacked_row = lax.broadcasted_iota(jnp.int32, (GLA_HEADS * C, C), 0)
    causal = (stacked_row & (C - 1)) >= lax.broadcasted_iota(jnp.int32, (GLA_HEADS * C, C), 1)
    og = og_ref[...]

    def prepare(b):
        z = jnp.dot(misc_ref[b], w2_ref[...], precision=HI, preferred_element_type=F32) + bg_ref[...]
        la = _log_sigmoid(z) / GLA_TAU
        la_hi = la.astype(BF16)
        la_lo = (la - la_hi.astype(F32)).astype(BF16)
        sums = jnp.dot(sums_ref[...], jnp.concatenate([la_hi, la_lo], axis=0), preferred_element_type=F32)
        bc, b_mid, b_last = sums[0:tg], sums[tg:2 * tg], sums[2 * tg:3 * tg]
        q = qk_ref[b, :, 0:GLA_QK_W] * (GLA_DK ** -0.5)
        k = qk_ref[b, :, GLA_QK_W:2 * GLA_QK_W]
        qd = q * jnp.exp(bc - b_mid)
        kl = k * jnp.exp(b_last - bc)
        qb = q * jnp.exp(bc)
        per_head = lambda t: [jnp.where(m, t, 0.0).astype(BF16) for m in heads]
        return dict(kd=(k * jnp.exp(b_mid - bc)).astype(BF16), dec=jnp.exp(b_last), qd_h=per_head(qd),
                    qb_h=per_head(qb), kl_h=per_head(kl))

    batches = range(BATCH)
    pre = [prepare(b) for b in batches]
    st = [st_ref[b] for b in batches]
    for c in range(tg // C):
        rows = slice(c * C, (c + 1) * C)
        for b in batches:
            p = pre[b]
            v = v_ref[b, rows, :].astype(BF16)
            s = lax.dot_general(stack(p["qd_h"], rows), p["kd"][rows], NT_DIMS, preferred_element_type=F32)
            s = jnp.where(causal, s, 0.0).astype(BF16)
            o_intra = jnp.dot(s, v, preferred_element_type=F32)
            o_inter = lax.dot_general(stack(p["qb_h"], rows), st[b].astype(BF16), NT_DIMS, preferred_element_type=F32)
            v_stack = jnp.concatenate([v[:, h * GLA_DV:(h + 1) * GLA_DV] for h in range(GLA_HEADS)], axis=0)
            st[b] = st[b] * p["dec"][c * C:c * C + 1] + lax.dot_general(v_stack, stack(p["kl_h"], rows), TN_DIMS,
                                                                       preferred_element_type=F32)
            for h in range(GLA_HEADS):
                hrows = slice(h * C, (h + 1) * C)
                vcols = slice(h * GLA_DV, (h + 1) * GLA_DV)
                o = o_intra[hrows, vcols] + o_inter[hrows]
                on = o * lax.rsqrt(jnp.mean(o * o, axis=-1, keepdims=True) + NORM_EPS) * og
                o_ref[b, rows, vcols] = on * _silu(r_ref[b, rows, vcols])
    for b in batches:
        st_ref[b] = st[b]


def gla_mixer(qk, v, r, misc, w_gate2, b_gate, out_g):
    tg = GLA_TG
    w2 = jnp.zeros((LANES, GLA_QK_W), F32).at[:GLA_GATE_RANK].set(w_gate2)
    row = lambda i: (0, i, 0)
    const = lambda i: (0, 0)
    return pl.pallas_call(
        _gla_kernel,
        grid=(SEQ // tg,),
        in_specs=[
            pl.BlockSpec((BATCH, tg, 2 * GLA_QK_W), row),
            pl.BlockSpec((BATCH, tg, GLA_V_W), row),
            pl.BlockSpec((BATCH, tg, GLA_V_W), row),
            pl.BlockSpec((BATCH, tg, LANES), row),
            pl.BlockSpec((LANES, GLA_QK_W), const),
            pl.BlockSpec((1, GLA_QK_W), const),
            pl.BlockSpec((1, GLA_DV), const),
            pl.BlockSpec((3 * tg, 2 * tg), const),
        ],
        out_specs=pl.BlockSpec((BATCH, tg, GLA_V_W), row),
        out_shape=jax.ShapeDtypeStruct((BATCH, SEQ, GLA_V_W), F32),
        scratch_shapes=[pltpu.VMEM((BATCH, GLA_DV, GLA_QK_W), F32)],
        compiler_params=_cparams(("arbitrary",)),
        name="gla_mixer",
    )(qk, v, r, misc, w2, b_gate.reshape(1, GLA_QK_W), out_g.reshape(1, GLA_DV), _gla_chunk_sums())


POS_SIDE = 128


def _rope_table_kernel(freq_ref, pos_ref, cos_ref, sin_ref):
    pos = pos_ref[...].astype(F32)
    for f in range(ROT_HALF):
        ang = pos * freq_ref[f]
        cos_ref[f] = jnp.cos(ang)
        sin_ref[f] = jnp.sin(ang)


def rope_tables(positions):
    inv_freq = jnp.float32(ROPE_THETA) ** (-jnp.arange(ROT_HALF, dtype=F32) / ROT_HALF)
    shp = jax.ShapeDtypeStruct((ROT_HALF, POS_SIDE, POS_SIDE), F32)
    cos, sin = pl.pallas_call(
        _rope_table_kernel,
        in_specs=[pl.BlockSpec(memory_space=pltpu.SMEM), pl.BlockSpec(memory_space=pltpu.VMEM)],
        out_specs=[pl.BlockSpec(memory_space=pltpu.VMEM)] * 2,
        out_shape=[shp, shp],
        name="rope_tables",
    )(inv_freq, positions.reshape(POS_SIDE, POS_SIDE))
    return jnp.concatenate([cos, sin], axis=0).reshape(ROT_DIM, N_TOK)


def _rope_placement():
    place = np.zeros((ROT_DIM, 3 * LANES), np.float32)
    const = np.zeros((1, 3 * LANES), np.float32)
    for lane in range(LANES):
        i = lane % NSA_DH
        if i < ROT_HALF:
            place[i, lane] = 1.0
            place[ROT_HALF + i, LANES + lane] = -1.0
        elif i < ROT_DIM:
            place[i - ROT_HALF, lane] = 1.0
            place[i, 2 * LANES + lane] = 1.0
        else:
            const[0, lane] = 1.0
    return jnp.asarray(place.T), jnp.asarray(const)


def _lane_tables(cs, place_ref, const_ref):
    tab = jnp.dot(place_ref[...], cs, precision=HI, preferred_element_type=F32).T + const_ref[...]
    return tab[:, 0:LANES], tab[:, LANES:2 * LANES], tab[:, 2 * LANES:3 * LANES]


def _block_diag_ones2(width):
    h = np.arange(width) // NSA_DH
    bd = (h[:, None] == h[None, :]).astype(np.float32)
    return jnp.asarray(np.concatenate([bd, bd], axis=0), BF16)


def _head_norm_rope(x, gain, bd2, c, sm, sp):
    width = x.shape[-1]
    reps = width // LANES
    sq = x * x
    sq_hi = sq.astype(BF16)
    sq_lo = (sq - sq_hi.astype(F32)).astype(BF16)
    ss = jnp.dot(jnp.concatenate([sq_hi, sq_lo], axis=1), bd2, preferred_element_type=F32)
    y = x * lax.rsqrt(ss * (1.0 / NSA_DH) + NORM_EPS) * gain
    tile = lambda t: jnp.concatenate([t] * reps, axis=1) if reps > 1 else t
    return (y * tile(c) + pltpu.roll(y, width - ROT_HALF, 1) * tile(sm) + pltpu.roll(y, ROT_HALF, 1) * tile(sp))


PREP_TM = 512


def _prep_kernel(q_ref, kv_ref, cs_ref, place_ref, const_ref, gq_ref, gk_ref, bdq_ref, bdk_ref,
                 qo_ref, kso_ref, kwo_ref, vso_ref, vwo_ref, xk_ref, xv_ref, tok_ref):
    tm = PREP_TM
    seg_lane = lax.broadcasted_iota(jnp.int32, (tm // CMP_STRIDE, LANES), 1)
    for n, out_ref in ((0, xk_ref), (1, xv_ref)):
        tok_ref[...] = kv_ref[0, :, n * NSA_KV_W:(n + 1) * NSA_KV_W]
        for pair in range(CMP_STRIDE // 2):
            a = tok_ref[pl.ds(2 * pair, tm // CMP_STRIDE, stride=CMP_STRIDE), :]
            b = tok_ref[pl.ds(2 * pair + 1, tm // CMP_STRIDE, stride=CMP_STRIDE), :]
            piece = slice(pair * LANES, (pair + 1) * LANES)
            out_ref[0, 0, :, piece] = jnp.where(seg_lane < NSA_DH, a, pltpu.roll(b, NSA_DH, 1))
            out_ref[0, 1, :, piece] = jnp.where(seg_lane < NSA_DH, pltpu.roll(a, NSA_DH, 1), b)
    c, sm, sp = _lane_tables(cs_ref[...], place_ref, const_ref)
    bdk = bdk_ref[...]
    q = _head_norm_rope(q_ref[0], gq_ref[...], bdq_ref[...], c, sm, sp) * (NSA_DH ** -0.5)
    qo_ref[0] = q.T.reshape(NSA_HEADS, NSA_DH, tm)
    kv_cols = lambda n: kv_ref[0, :, n * NSA_KV_W:(n + 1) * NSA_KV_W]
    ks = _head_norm_rope(kv_cols(2), gk_ref[0:1, :], bdk, c, sm, sp)
    kw = _head_norm_rope(kv_cols(4), gk_ref[1:2, :], bdk, c, sm, sp)
    lane = lax.broadcasted_iota(jnp.int32, (tm, LANES), 1)
    token = pl.program_id(1) * tm + lax.broadcasted_iota(jnp.int32, (tm, LANES), 0)
    block_col = NSA_DH + jnp.right_shift(token & (SA_TK - 1), 6)
    onehot = jnp.where(lane == block_col, 1.0, 0.0)
    for g in range(NSA_KV_GROUPS):
        to_front = lambda t: t if g == 0 else pltpu.roll(t, NSA_DH, 1)
        kso_ref[0, g] = jnp.where(lane < NSA_DH, to_front(ks), onehot).astype(BF16)
        kwo_ref[0, g] = jnp.where(lane < NSA_DH, to_front(kw), 0.0).astype(BF16)
    tail = jnp.where(lax.broadcasted_iota(jnp.int32, (VT_ROWS - NSA_DH, tm), 0) == 0, 1.0, 0.0)
    for n, out_ref in ((3, vso_ref), (5, vwo_ref)):
        v_t = kv_cols(n).T
        for g in range(NSA_KV_GROUPS):
            out_ref[0, g] = jnp.concatenate([v_t[g * NSA_DH:(g + 1) * NSA_DH], tail], axis=0).astype(BF16)


def nsa_prep(nq, nkv, cs, q_gain, k_gain):
    tm = PREP_TM
    G = NSA_KV_GROUPS
    row = lambda b, i: (b, i, 0)
    const = lambda b, i: (0, 0)
    gq = jnp.tile(q_gain, NSA_HEADS).reshape(1, NSA_Q_W)
    gk = jnp.stack([jnp.tile(k_gain[1], NSA_KV_GROUPS), jnp.tile(k_gain[2], NSA_KV_GROUPS)])
    kslab = pl.BlockSpec((1, G, tm, LANES), lambda b, i: (b, 0, i, 0))
    vslab = pl.BlockSpec((1, G, VT_ROWS, tm), lambda b, i: (b, 0, 0, i))
    segs = pl.BlockSpec((1, G, tm // CMP_STRIDE, CMP_STRIDE * NSA_DH), lambda b, i: (b, 0, i, 0))
    return pl.pallas_call(
        _prep_kernel,
        grid=(BATCH, SEQ // tm),
        in_specs=[
            pl.BlockSpec((1, tm, NSA_Q_W), row),
            pl.BlockSpec((1, tm, 6 * NSA_KV_W), row),
            pl.BlockSpec((ROT_DIM, tm), lambda b, i: (0, b * (SEQ // tm) + i)),
            pl.BlockSpec((3 * LANES, ROT_DIM), const),
            pl.BlockSpec((1, 3 * LANES), const),
            pl.BlockSpec((1, NSA_Q_W), const),
            pl.BlockSpec((2, NSA_KV_W), const),
            pl.BlockSpec((2 * NSA_Q_W, NSA_Q_W), const),
            pl.BlockSpec((2 * NSA_KV_W, NSA_KV_W), const),
        ],
        out_specs=[pl.BlockSpec((1, NSA_HEADS, NSA_DH, tm), lambda b, i: (b, 0, 0, i)), kslab, kslab, vslab, vslab,
                   segs, segs],
        out_shape=[jax.ShapeDtypeStruct((BATCH, NSA_HEADS, NSA_DH, SEQ), F32),
                   jax.ShapeDtypeStruct((BATCH, G, SEQ, LANES), BF16), jax.ShapeDtypeStruct((BATCH, G, SEQ, LANES), BF16),
                   jax.ShapeDtypeStruct((BATCH, G, VT_ROWS, SEQ), BF16), jax.ShapeDtypeStruct((BATCH, G, VT_ROWS, SEQ), BF16),
                   jax.ShapeDtypeStruct((BATCH, G, N_CMP_PAD, CMP_STRIDE * NSA_DH), F32),
                   jax.ShapeDtypeStruct((BATCH, G, N_CMP_PAD, CMP_STRIDE * NSA_DH), F32)],
        scratch_shapes=[pltpu.VMEM((tm, LANES), F32)],
        compiler_params=_cparams(("arbitrary", "arbitrary")),
        name="nsa_prep",
    )(nq, nkv, cs, *_rope_placement(), gq, gk, _block_diag_ones2(NSA_Q_W), _block_diag_ones2(NSA_KV_W))


SEG_W = CMP_STRIDE * NSA_DH


def _cmp_kernel(xk_ref, xv_ref, pe_ref, w1_ref, w2_ref, gain_ref, cs_ref, place_ref, const_ref, bd_ref, ko_ref, vo_ref):
    def compress(x_ref, kv):
        out = jnp.zeros((N_CMP_PAD, LANES), F32)
        for g in range(NSA_KV_GROUPS):
            x = x_ref[0, g]
            ha = jnp.dot(split3_keys(x + pe_ref[kv, 0]), w1_ref[kv, 0], preferred_element_type=F32)
            hb = jnp.dot(split3_keys(x + pe_ref[kv, 1]), w1_ref[kv, 1], preferred_element_type=F32)
            hid = ha + pltpu.roll(hb, N_CMP_PAD - 1, 0)
            out += jnp.dot(jax.nn.gelu(hid), w2_ref[kv, g], precision=HI, preferred_element_type=F32)
        return out

    c, sm, sp = _lane_tables(cs_ref[...], place_ref, const_ref)
    ko_ref[0] = _head_norm_rope(compress(xk_ref, 0), gain_ref[...], bd_ref[...], c, sm, sp)
    vo_ref[0] = compress(xv_ref, 1)


def nsa_compress(xk, xv, cmp_pe, cmp_w1, cmp_w2, k_gain0, cs_last):
    pe = cmp_pe.reshape(2, 2, 1, SEG_W)
    w1 = cmp_w1.reshape(2, 2, SEG_W, CMP_HIDDEN)
    w1_hi = w1.astype(BF16)
    w1 = jnp.concatenate([w1_hi, w1_hi, (w1 - w1_hi.astype(F32)).astype(BF16)], axis=2)
    w2 = jnp.zeros((2, NSA_KV_GROUPS, CMP_HIDDEN, LANES), F32)
    for g in range(NSA_KV_GROUPS):
        w2 = w2.at[:, g, :, g * NSA_DH:(g + 1) * NSA_DH].set(cmp_w2)
    seg = pl.BlockSpec((1, NSA_KV_GROUPS, N_CMP_PAD, SEG_W), lambda b: (b, 0, 0, 0))
    tab = pl.BlockSpec((1, N_CMP_PAD, LANES), lambda b: (b, 0, 0))
    full = lambda shape: pl.BlockSpec(shape, lambda b: (0,) * len(shape))
    return pl.pallas_call(
        _cmp_kernel,
        grid=(BATCH,),
        in_specs=[seg, seg, full((2, 2, 1, SEG_W)), full((2, 2, 3 * SEG_W, CMP_HIDDEN)),
                  full((2, NSA_KV_GROUPS, CMP_HIDDEN, LANES)), full((1, LANES)),
                  pl.BlockSpec((ROT_DIM, N_CMP_PAD), lambda b: (0, b)), full((3 * LANES, ROT_DIM)),
                  full((1, 3 * LANES)), full((2 * LANES, LANES))],
        out_specs=[tab, tab],
        out_shape=[jax.ShapeDtypeStruct((BATCH, N_CMP_PAD, LANES), F32)] * 2,
        compiler_params=_cparams(("arbitrary",)),
        name="nsa_compress",
    )(xk, xv, pe, w1, w2, jnp.tile(k_gain0, NSA_KV_GROUPS).reshape(1, LANES), cs_last, *_rope_placement(),
      _block_diag_ones2(LANES))


CA_TQ = 512
CA_COLS = NSA_HPG * CA_TQ
CMP_PER_SEL = SEL_BLOCK // CMP_STRIDE
TOPK_BANDS = 4


def split3_keys(k):
    hi = k.astype(BF16)
    lo = (k - hi.astype(F32)).astype(BF16)
    return jnp.concatenate([hi, lo, hi], axis=-1)


def _top_k_rows(score, k):
    rows, cols = score.shape
    row = lax.broadcasted_iota(jnp.int32, (rows, cols), 0).astype(F32)
    taken = jnp.zeros((rows, cols), F32)
    left = score
    for _ in range(k):
        top = jnp.max(left, axis=0, keepdims=True)
        first = jnp.min(jnp.where(left == top, row, float(rows)), axis=0, keepdims=True)
        hit = row == first
        taken = jnp.where(hit, 1.0, taken)
        left = jnp.where(hit, -jnp.inf, left)
    return taken


def _cattn_kernel(q_ref, kc_ref, vct_ref, gl_ref, o_ref, sel_ref, q3_ref, ps_ref):
    tq = CA_TQ
    q0 = pl.program_id(1) * tq
    lanes4 = lambda t: jnp.concatenate([t] * NSA_HPG, axis=1)
    cend = lax.broadcasted_iota(jnp.int32, (N_CMP_PAD, tq), 0) * CMP_STRIDE + (CMP_LEN - 1)
    tc = q0 + lax.broadcasted_iota(jnp.int32, (N_CMP_PAD, tq), 1)
    cmask = lanes4(cend <= tc)
    jj = lax.broadcasted_iota(jnp.int32, (N_SEL, tq), 0)
    tt = q0 + lax.broadcasted_iota(jnp.int32, (N_SEL, tq), 1)
    cur = jnp.right_shift(tt, 6)
    forced = (jj == 0) | (jj == cur) | (jj == cur - 1)
    valid = jj * SEL_BLOCK <= tt

    for g in range(NSA_KV_GROUPS):
        heads = range(g * NSA_HPG, (g + 1) * NSA_HPG)
        for n, h in enumerate(heads):
            q = q_ref[0, h]
            hi = q.astype(BF16)
            lo = (q - hi.astype(F32)).astype(BF16)
            for t, part in enumerate((hi, hi, lo)):
                q3_ref[g, t * NSA_DH:(t + 1) * NSA_DH, n * tq:(n + 1) * tq] = part
        s = jnp.dot(kc_ref[0, g], q3_ref[g], preferred_element_type=F32)
        s = jnp.where(cmask, s, NEG_INF)
        m = jnp.max(s, axis=0, keepdims=True)
        e = jnp.where(cmask, jnp.exp(s - m), 0.0)
        l = jnp.sum(e, axis=0, keepdims=True)
        p = e / jnp.where(l > 0.0, l, 1.0)
        gate = jnp.concatenate([jax.nn.sigmoid(gl_ref[0, 3 * h:3 * h + 1, :]) for h in heads], axis=1)
        o = jnp.dot(vct_ref[0, g], p.astype(BF16), preferred_element_type=F32) * gate
        for n, h in enumerate(heads):
            o_ref[0, h] = o[:, n * tq:(n + 1) * tq]
        psum = functools.reduce(jnp.add, [p[:, n * tq:(n + 1) * tq] for n in range(NSA_HPG)])
        for n in range(tq // LANES):
            ps_ref[g, n] = psum[:, n * LANES:(n + 1) * LANES]

        every4th = lambda r: jnp.concatenate(
            [ps_ref[g, n, pl.ds(r, N_SEL, stride=CMP_PER_SEL), :] for n in range(tq // LANES)], axis=1)
        starts_in = [every4th(r) for r in range(CMP_PER_SEL)]
        from_prev = jnp.where(jj >= 1, pltpu.roll(starts_in[CMP_PER_SEL - 1], 1, 0), 0.0)
        imp = functools.reduce(jnp.add, starts_in) + from_prev
        score = jnp.where(valid, imp + jnp.where(forced, FORCE_BONUS, 0.0), NEG_INF)
        step = pl.program_id(1)
        steps_per_band = (SEQ // tq) // TOPK_BANDS
        for band in range(TOPK_BANDS):
            n_rows = (band + 1) * (N_SEL // TOPK_BANDS)

            @pl.when((step >= band * steps_per_band) & (step < (band + 1) * steps_per_band))
            def _():
                taken = _top_k_rows(score[0:n_rows], SEL_TOPK)
                sel_ref[0, g, 0:n_rows, :] = jnp.where(valid[0:n_rows], taken, 0.0)
                if n_rows < N_SEL:
                    sel_ref[0, g, n_rows:N_SEL, :] = jnp.zeros((N_SEL - n_rows, tq), F32)


def nsa_cmp_attn(q_t, kcmp, vcmp_t, gl_t):
    tq = CA_TQ
    G = NSA_KV_GROUPS
    return pl.pallas_call(
        _cattn_kernel,
        grid=(BATCH, SEQ // tq),
        in_specs=[
            pl.BlockSpec((1, NSA_HEADS, NSA_DH, tq), lambda b, i: (b, 0, 0, i)),
            pl.BlockSpec((1, G, N_CMP_PAD, 3 * NSA_DH), lambda b, i: (b, 0, 0, 0)),
            pl.BlockSpec((1, G, NSA_DH, N_CMP_PAD), lambda b, i: (b, 0, 0, 0)),
            pl.BlockSpec((1, GATE_ROWS, tq), lambda b, i: (b, 0, i)),
        ],
        out_specs=[pl.BlockSpec((1, NSA_HEADS, NSA_DH, tq), lambda b, i: (b, 0, 0, i)),
                   pl.BlockSpec((1, G, N_SEL, tq), lambda b, i: (b, 0, 0, i))],
        out_shape=[jax.ShapeDtypeStruct((BATCH, NSA_HEADS, NSA_DH, SEQ), F32),
                   jax.ShapeDtypeStruct((BATCH, G, N_SEL, SEQ), F32)],
        scratch_shapes=[pltpu.VMEM((G, 3 * NSA_DH, CA_COLS), BF16), pltpu.VMEM((G, tq // LANES, N_CMP_PAD, LANES), F32)],
        compiler_params=_cparams(("arbitrary", "arbitrary")),
        name="nsa_cmp_attn",
    )(q_t, split3_keys(kcmp), vcmp_t.astype(BF16), gl_t)


SA_TQ = 256
SA_TK = 1024
SA_PARTS = 2
SA_PART = SA_TK // SA_PARTS
M_INIT = -1e20


SA_COLS = NSA_HPG * SA_TQ
SA_BLOCKS = SA_TK // SEL_BLOCK


VT_ROWS = NSA_DH + 16


def _sattn_kernel(q_ref, k_ref, vt_ref, sel_ref, gl_ref, kw_ref, vw_ref, prev_ref, o_ref, qa_ref, acc_ref, s_ref,
                  m_ref):
    tq, tk = SA_TQ, SA_TK
    i = pl.program_id(1)
    groups = range(NSA_KV_GROUPS)
    slots = range(2)
    for g in groups:
        for h in range(NSA_HPG):
            q = q_ref[0, g * NSA_HPG + h].astype(BF16)
            for slot in slots:
                qa_ref[slot, g, 0:NSA_DH, h * tq:(h + 1) * tq] = q
        for slot in slots:
            qa_ref[slot, g, NSA_DH:LANES, :] = jnp.zeros((LANES - NSA_DH, SA_COLS), BF16)
    acc_ref[...] = jnp.zeros_like(acc_ref)
    lanes4 = lambda t: jnp.concatenate([t] * NSA_HPG, axis=1)
    part_keys = lambda kt, part: pl.ds(pl.multiple_of(kt * tk + part * SA_PART, SA_PART), SA_PART)

    def scores(kt, slot):
        for g in groups:
            selrows = sel_ref[0, g, pl.ds(pl.multiple_of(kt * SA_BLOCKS, SA_BLOCKS), SA_BLOCKS), :]
            qa_ref[slot, g, NSA_DH:NSA_DH + SA_BLOCKS, :] = lanes4(jnp.where(selrows > 0.5, 0.0, NEG_INF)).astype(BF16)
            for part in range(SA_PARTS):
                s = jnp.dot(k_ref[0, g, part_keys(kt, part), :], qa_ref[slot, g], preferred_element_type=F32)
                s_ref[slot, g, part] = s.astype(BF16)

    def absorb(kt, slot, ms):
        out = []
        for g in groups:
            ss = [s_ref[slot, g, part] for part in range(SA_PARTS)]
            m_tile = functools.reduce(jnp.maximum, [jnp.max(s, axis=0, keepdims=True) for s in ss])
            m_new = jnp.maximum(ms[g], m_tile.astype(F32))
            acc = jnp.exp(ms[g] - m_new) * acc_ref[g]
            for part in range(SA_PARTS):
                p = jnp.exp(ss[part] - m_new.astype(BF16))
                acc += jnp.dot(vt_ref[0, g, :, part_keys(kt, part)], p, preferred_element_type=F32)
            acc_ref[g] = acc
            out.append(m_new)
        return tuple(out)

    def two_tiles(j, ms):
        kt = 2 * j
        scores(kt + 1, 1)
        ms = absorb(kt, 0, ms)
        scores(kt + 2, 0)
        return absorb(kt + 1, 1, ms)

    n_full = (i * tq) // tk
    scores(0, 0)
    m0 = tuple(jnp.full((1, SA_COLS), M_INIT, F32) for _ in groups)
    ms = lax.fori_loop(0, n_full // 2, two_tiles, m0)
    for g in groups:
        m_ref[g] = ms[g]

    def last_tile(slot):
        start = i * tq - n_full * tk
        part, row0 = start // SA_PART, pl.multiple_of(start % SA_PART, tq)
        tri = lax.broadcasted_iota(jnp.int32, (tq, tq), 0) <= lax.broadcasted_iota(jnp.int32, (tq, tq), 1)
        bias = lanes4(jnp.where(tri, 0.0, NEG_INF)).astype(BF16)
        for g in groups:
            s_ref[slot, g, part, pl.ds(row0, tq), :] += bias
        for g, m in enumerate(absorb(n_full, slot, tuple(m_ref[g] for g in groups))):
            m_ref[g] = m

    @pl.when(n_full % 2 == 0)
    def _():
        last_tile(0)

    @pl.when(n_full % 2 == 1)
    def _():
        scores(n_full, 1)
        for g, m in enumerate(absorb(n_full - 1, 0, tuple(m_ref[g] for g in groups))):
            m_ref[g] = m
        last_tile(1)

    w_rows = WINDOW + tq
    w_start = jnp.maximum(i * tq - WINDOW, 0)
    w_keys = pl.ds(pl.multiple_of(w_start, tq), w_rows)
    behind = (i * tq + lax.broadcasted_iota(jnp.int32, (w_rows, tq), 1)
              - (w_start + lax.broadcasted_iota(jnp.int32, (w_rows, tq), 0)))
    w_bias = lanes4(jnp.where((behind >= 0) & (behind < WINDOW), 0.0, NEG_INF))
    for g in groups:
        heads = range(g * NSA_HPG, (g + 1) * NSA_HPG)
        gates = lambda branch: jnp.concatenate(
            [jax.nn.sigmoid(gl_ref[0, 3 * h + branch:3 * h + branch + 1, :]) for h in heads], axis=1)
        out = acc_ref[g, 0:NSA_DH, :] / acc_ref[g, NSA_DH:NSA_DH + 1, :] * gates(1)
        s = (jnp.dot(kw_ref[0, g, w_keys, :], qa_ref[0, g], preferred_element_type=F32) + w_bias).astype(BF16)
        p = jnp.exp(s - jnp.max(s, axis=0, keepdims=True))
        acc_w = jnp.dot(vw_ref[0, g, :, w_keys], p, preferred_element_type=F32)
        out += acc_w[0:NSA_DH] / acc_w[NSA_DH:NSA_DH + 1] * gates(2)
        for n, h in enumerate(heads):
            o_ref[0, h] = prev_ref[0, h] + out[:, n * tq:(n + 1) * tq]


def nsa_sel_win_attn(q_t, k_slab, vsel_t, sel_t, gl_t, kwin, vwin_t, prev):
    tq = SA_TQ
    G = NSA_KV_GROUPS
    ospec = pl.BlockSpec((1, NSA_HEADS, NSA_DH, tq), lambda b, i: (b, 0, 0, i))
    kspec = pl.BlockSpec((1, G, SEQ, LANES), lambda b, i: (b, 0, 0, 0))
    vspec = pl.BlockSpec((1, G, VT_ROWS, SEQ), lambda b, i: (b, 0, 0, 0))
    return pl.pallas_call(
        _sattn_kernel,
        grid=(BATCH, SEQ // tq),
        in_specs=[
            ospec, kspec, vspec,
            pl.BlockSpec((1, G, N_SEL, tq), lambda b, i: (b, 0, 0, i)),
            pl.BlockSpec((1, GATE_ROWS, tq), lambda b, i: (b, 0, i)),
            kspec, vspec, ospec,
        ],
        out_specs=ospec,
        out_shape=jax.ShapeDtypeStruct((BATCH, NSA_HEADS, NSA_DH, SEQ), F32),
        scratch_shapes=[pltpu.VMEM((2, G, LANES, SA_COLS), BF16), pltpu.VMEM((G, VT_ROWS, SA_COLS), F32),
                        pltpu.VMEM((2, G, SA_PARTS, SA_PART, SA_COLS), BF16), pltpu.VMEM((G, 1, SA_COLS), F32)],
        input_output_aliases={7: 0},
        compiler_params=_cparams(("arbitrary", "arbitrary"), vmem_mib=56),
        name="nsa_sel_win_attn",
    )(q_t, k_slab, vsel_t, sel_t, gl_t, kwin, vwin_t, prev)


def nsa_mixer(nq, nkv, gl_t, positions, q_gain, k_gain, cmp_pe, cmp_w1, cmp_w2):
    cs = rope_tables(positions)
    q_t, ksel, kwin, vsel_t, vwin_t, xk, xv = nsa_prep(nq, nkv, cs, q_gain, k_gain)
    last = jnp.minimum(jnp.arange(N_CMP_PAD) * CMP_STRIDE + CMP_LEN - 1, SEQ - 1)
    last = (jnp.arange(BATCH)[:, None] * SEQ + last[None, :]).reshape(-1)
    kcmp, vcmp = nsa_compress(xk, xv, cmp_pe, cmp_w1, cmp_w2, k_gain[0], cs[:, last])
    kcmp = kcmp.reshape(BATCH, N_CMP_PAD, NSA_KV_GROUPS, NSA_DH).transpose(0, 2, 1, 3)
    vcmp_t = vcmp.reshape(BATCH, N_CMP_PAD, NSA_KV_GROUPS, NSA_DH).transpose(0, 2, 3, 1)
    o_t, sel_t = nsa_cmp_attn(q_t, kcmp, vcmp_t, gl_t)
    return nsa_sel_win_attn(q_t, ksel, vsel_t, sel_t, gl_t, kwin, vwin_t, o_t)


ROUTE_ROWS = 8
HX_TERMS = 3
HX_W = D_MODEL + LANES


def _top2_sum(a, b, c, d):
    hi1, lo1 = jnp.maximum(a, b), jnp.minimum(a, b)
    hi2, lo2 = jnp.maximum(c, d), jnp.minimum(c, d)
    return jnp.maximum(hi1, hi2) + jnp.maximum(jnp.minimum(hi1, hi2), jnp.maximum(lo1, lo2))


def _moe_prenorm_route(xn, g_ref, sh_ref, sc_ref, wr_ref, rb_ref, hx_ref, route_ref):
    h = _rms_mod(xn, g_ref[...], sh_ref[0], sc_ref[0])
    logits = lax.dot_general(wr_ref[...], h, NT_DIMS, precision=HI, preferred_element_type=F32)
    scores = jax.nn.sigmoid(logits)
    sel = scores + rb_ref[...]
    epg = EXPERTS_PER_GROUP
    srow = lambda e: sel[e:e + 1, :]
    grp = [_top2_sum(*[srow(epg * g + r) for r in range(epg)]) for g in range(N_EXPERT_GROUPS)]
    best, gi = grp[0], jnp.zeros_like(grp[0], dtype=jnp.int32)
    for g in range(1, N_EXPERT_GROUPS):
        better = grp[g] > best
        gi = jnp.where(better, g, gi)
        best = jnp.where(better, grp[g], best)

    def in_group(mat, r):
        out = mat[r:r + 1, :]
        for g in range(1, N_EXPERT_GROUPS):
            out = jnp.where(gi == g, mat[epg * g + r:epg * g + r + 1, :], out)
        return out

    v = [in_group(sel, r) for r in range(epg)]
    sc = [in_group(scores, r) for r in range(epg)]
    b1, i1, w1 = v[0], jnp.zeros_like(gi), sc[0]
    for r in range(1, epg):
        better = v[r] > b1
        i1 = jnp.where(better, r, i1)
        w1 = jnp.where(better, sc[r], w1)
        b1 = jnp.where(better, v[r], b1)
    b2 = jnp.full_like(b1, -3e38)
    i2, w2 = jnp.zeros_like(gi), jnp.zeros_like(w1)
    for r in range(epg):
        better = (i1 != r) & (v[r] > b2)
        i2 = jnp.where(better, r, i2)
        w2 = jnp.where(better, sc[r], w2)
        b2 = jnp.where(better, v[r], b2)
    tot = w1 + w2
    w1, w2 = w1 / tot, w2 / tot
    zero = jnp.zeros_like(w1)
    route_ref[0] = jnp.concatenate([gi.astype(F32)] + [zero] * (ROUTE_ROWS - 1), axis=0)
    w = jnp.concatenate([jnp.where(i1 == r, w1, jnp.where(i2 == r, w2, 0.0)) for r in range(epg)], axis=0)
    w_hi = w.astype(BF16).astype(F32)
    w_mid = (w - w_hi).astype(BF16).astype(F32)
    w_lo = (w - w_hi - w_mid).astype(BF16).astype(F32)
    pad = jnp.zeros((LANES - HX_TERMS * epg, w.shape[1]), F32)
    hx_ref[0, :, 0:D_MODEL] = h.astype(BF16)
    hx_ref[0, :, D_MODEL:HX_W] = jnp.concatenate([w_hi, w_mid, w_lo, pad], axis=0).T.astype(BF16)


def _route_specs(tm, row, vec, const):
    in_specs = [pl.BlockSpec((1, D_MODEL), const), pl.BlockSpec((1, 1, D_MODEL), vec), pl.BlockSpec((1, 1, D_MODEL), vec),
                pl.BlockSpec((N_EXPERTS, D_MODEL), const), pl.BlockSpec((N_EXPERTS, 1), const)]
    assert tm == MOE_CHUNK
    out_specs = [pl.BlockSpec((1, tm, HX_W), lambda b, i: (b * (SEQ // tm) + i, 0, 0)),
                 pl.BlockSpec((1, ROUTE_ROWS, tm), lambda b, i: (b, 0, i))]
    out_shape = [jax.ShapeDtypeStruct((MOE_CHUNKS, MOE_CHUNK, HX_W), BF16),
                 jax.ShapeDtypeStruct((BATCH, ROUTE_ROWS, SEQ), F32)]
    return in_specs, out_specs, out_shape


def _route_args(g, shift, scale, w_router, router_bias):
    return (g.reshape(1, D_MODEL), shift.reshape(BATCH, 1, D_MODEL), scale.reshape(BATCH, 1, D_MODEL),
            w_router.T, router_bias.reshape(N_EXPERTS, 1))


OUTPROJ_TM = 512


def _outproj0_kernel(oa_ref, ob_ref, w_ref, x_ref, gate_ref, g_ref, sh_ref, sc_ref, wr_ref, rb_ref,
                     xo_ref, h_ref, route_ref):
    y = jnp.dot(oa_ref[0].astype(BF16), w_ref[0:GLA_V_W, :], preferred_element_type=F32)
    ob_t = ob_ref[0].reshape(NSA_Q_W, OUTPROJ_TM).astype(BF16)
    y += lax.dot_general(ob_t, w_ref[GLA_V_W:GLA_V_W + NSA_Q_W, :], TN_DIMS, preferred_element_type=F32)
    xn = x_ref[0] + gate_ref[0] * y
    xo_ref[0] = xn
    _moe_prenorm_route(xn, g_ref, sh_ref, sc_ref, wr_ref, rb_ref, h_ref, route_ref)


def outproj0(o_a, o_b, w_out, x, gate, route_args):
    tm = OUTPROJ_TM
    row = lambda b, i: (b, i, 0)
    vec = lambda b, i: (b, 0, 0)
    const = lambda b, i: (0, 0)
    r_in, r_out, r_shape = _route_specs(tm, row, vec, const)
    return pl.pallas_call(
        _outproj0_kernel,
        grid=(BATCH, SEQ // tm),
        in_specs=[pl.BlockSpec((1, tm, GLA_V_W), row), pl.BlockSpec((1, NSA_HEADS, NSA_DH, tm), lambda b, i: (b, 0, 0, i)),
                  pl.BlockSpec((GLA_V_W + NSA_Q_W, D_MODEL), const), pl.BlockSpec((1, tm, D_MODEL), row),
                  pl.BlockSpec((1, 1, D_MODEL), vec)] + r_in,
        out_specs=[pl.BlockSpec((1, tm, D_MODEL), row)] + r_out,
        out_shape=[jax.ShapeDtypeStruct((BATCH, SEQ, D_MODEL), F32)] + r_shape,
        compiler_params=_cparams(("arbitrary", "arbitrary")),
        name="outproj0",
    )(o_a, o_b, w_out.astype(BF16), x, gate.reshape(BATCH, 1, D_MODEL), *route_args)


GMLP_TM = 512


def _gmlp_kernel(x_ref, g1_ref, sh1_ref, sc1_ref, win_ref, ng_ref, ws_ref, bs_ref, wout_ref, gate_ref,
                 g_ref, sh_ref, sc_ref, wr_ref, rb_ref, xo_ref, h_ref, route_ref, gated_ref, v_ref):
    x = x_ref[0]
    h = _rms_mod(x, g1_ref[...], sh1_ref[0], sc1_ref[0]).astype(BF16)
    group_cols = lambda g: slice(g * SGU_GROUP_DIM, (g + 1) * SGU_GROUP_DIM)
    ssq = jnp.zeros((GMLP_TM, LANES), F32)
    for g in range(SGU_GROUPS):
        lo = SGU_WIDTH + g * SGU_GROUP_DIM
        v = jax.nn.gelu(jnp.dot(h, win_ref[:, lo:lo + SGU_GROUP_DIM], preferred_element_type=F32))
        v_ref[:, group_cols(g)] = v
        ssq += functools.reduce(jnp.add, [v[:, n * LANES:(n + 1) * LANES] ** 2 for n in range(SGU_GROUP_DIM // LANES)])
    rs = lax.rsqrt(jnp.sum(ssq, axis=-1, keepdims=True) * (1.0 / SGU_WIDTH) + NORM_EPS)
    ri = lax.broadcasted_iota(jnp.int32, (SGU_CHUNK, SGU_CHUNK), 0)
    ci = lax.broadcasted_iota(jnp.int32, (SGU_CHUNK, SGU_CHUNK), 1)
    for g in range(SGU_GROUPS):
        cols = group_cols(g)
        u = jax.nn.gelu(jnp.dot(h, win_ref[:, cols], preferred_element_type=F32))
        vn = (v_ref[:, cols] * rs * ng_ref[:, cols]).astype(BF16)
        w = jnp.where(ri >= ci, ws_ref[g], 0.0).astype(BF16)
        for c in range(GMLP_TM // SGU_CHUNK):
            rows = slice(c * SGU_CHUNK, (c + 1) * SGU_CHUNK)
            mix = jnp.dot(w, vn[rows], preferred_element_type=F32) + bs_ref[:, g:g + 1]
            gated_ref[rows, cols] = (u[rows] * mix).astype(BF16)
    y = jnp.dot(gated_ref[...], wout_ref[...], preferred_element_type=F32)
    xn = x + gate_ref[0] * y
    xo_ref[0] = xn
    _moe_prenorm_route(xn, g_ref, sh_ref, sc_ref, wr_ref, rb_ref, h_ref, route_ref)


def gmlp_layer(x, g1, shift1, scale1, w_in, norm_g, w_s, b_s, w_out, gate, route_args):
    tm = GMLP_TM
    row = lambda b, i: (b, i, 0)
    vec = lambda b, i: (b, 0, 0)
    const = lambda b, i: (0, 0)
    r_in, r_out, r_shape = _route_specs(tm, row, vec, const)
    vspec = pl.BlockSpec((1, 1, D_MODEL), vec)
    return pl.pallas_call(
        _gmlp_kernel,
        grid=(BATCH, SEQ // tm),
        in_specs=[pl.BlockSpec((1, tm, D_MODEL), row), pl.BlockSpec((1, D_MODEL), const), vspec, vspec,
                  pl.BlockSpec((D_MODEL, 2 * SGU_WIDTH), const), pl.BlockSpec((1, SGU_WIDTH), const),
                  pl.BlockSpec((SGU_GROUPS, SGU_CHUNK, SGU_CHUNK), lambda b, i: (0, 0, 0)),
                  pl.BlockSpec((SGU_CHUNK, SGU_GROUPS), const), pl.BlockSpec((SGU_WIDTH, D_MODEL), const), vspec] + r_in,
        out_specs=[pl.BlockSpec((1, tm, D_MODEL), row)] + r_out,
        out_shape=[jax.ShapeDtypeStruct((BATCH, SEQ, D_MODEL), F32)] + r_shape,
        scratch_shapes=[pltpu.VMEM((tm, SGU_WIDTH), BF16), pltpu.VMEM((tm, SGU_WIDTH), F32)],
        compiler_params=_cparams(("arbitrary", "arbitrary"), vmem_mib=56),
        name="gmlp_layer",
    )(x, g1.reshape(1, D_MODEL), shift1.reshape(BATCH, 1, D_MODEL), scale1.reshape(BATCH, 1, D_MODEL),
      w_in.astype(BF16), norm_g.reshape(1, SGU_WIDTH), w_s, b_s.T, w_out.astype(BF16),
      gate.reshape(BATCH, 1, D_MODEL), *route_args)


MOE_TM = 256
MOE_CHUNK = 512
MOE_SORTED = N_TOK + N_EXPERT_GROUPS * MOE_TM
MOE_TILES = MOE_SORTED // MOE_TM
MOE_CHUNKS = N_TOK // MOE_CHUNK
MOE_PAIRS = MOE_TILES + N_EXPERT_GROUPS * MOE_CHUNKS
FLAG_ACTIVE, FLAG_FIRST, FLAG_LAST, FLAG_ZERO = 1, 2, 4, 8
EXP_WIN = 4
CMB_WIN = 8
MOE_TSTEPS = MOE_TILES + MOE_PAIRS // EXP_WIN
MOE_CSTEPS = MOE_CHUNKS + MOE_PAIRS // CMB_WIN


def _plan_kernel(gi_ref, rank_ref, before_ref):
    gi = gi_ref[...]
    r = lax.broadcasted_iota(jnp.int32, (POS_SIDE, POS_SIDE), 0)
    c = lax.broadcasted_iota(jnp.int32, (POS_SIDE, POS_SIDE), 1)
    upper = jnp.where(r <= c, 1.0, 0.0)
    lower_strict = jnp.where(c < r, 1.0, 0.0)
    rank = jnp.zeros((POS_SIDE, POS_SIDE), F32)
    for g in range(N_EXPERT_GROUPS):
        member = jnp.where(gi == g, 1.0, 0.0)
        in_row = jnp.dot(member, upper, precision=HI, preferred_element_type=F32)
        row_total = jnp.broadcast_to(in_row[:, POS_SIDE - 1:POS_SIDE], (POS_SIDE, POS_SIDE))
        before = jnp.dot(lower_strict, row_total, precision=HI, preferred_element_type=F32)
        before_ref[g] = before
        rank += member * (before + in_row - 1.0)
    rank_ref[...] = rank


def moe_plan(route):
    tm = MOE_TM
    i32 = jnp.int32
    gi_f = route[:, 0, :].reshape(POS_SIDE, POS_SIDE)
    rank, before = pl.pallas_call(
        _plan_kernel,
        out_shape=[jax.ShapeDtypeStruct((POS_SIDE, POS_SIDE), F32),
                   jax.ShapeDtypeStruct((N_EXPERT_GROUPS, POS_SIDE, POS_SIDE), F32)],
        name="moe_plan",
    )(gi_f)
    gi = gi_f.reshape(N_TOK).astype(i32)
    groups = jnp.arange(N_EXPERT_GROUPS, dtype=i32)
    member = gi[None, :] == groups[:, None]
    tot = jnp.sum(member, axis=1).astype(i32)
    padded = (tot + tm - 1) // tm * tm
    gend = jnp.cumsum(padded).astype(i32)
    gstart = gend - padded
    pos = jnp.sum(jnp.where(member, gstart[:, None], 0), axis=0).astype(i32) + rank.reshape(N_TOK).astype(i32)
    rows_per_chunk = MOE_CHUNK // POS_SIDE
    cnt_end = jnp.concatenate([before[:, rows_per_chunk::rows_per_chunk, 0].astype(i32), tot[:, None]], axis=1)
    t = jnp.arange(MOE_TILES, dtype=i32)
    n_used = gend[-1] // tm
    tile_g = jnp.minimum(jnp.sum(gend[None, :] <= (t * tm)[:, None], axis=1), N_EXPERT_GROUPS - 1).astype(i32)
    k0 = t * tm - gstart[tile_g]
    k1 = jnp.minimum(k0 + tm, tot[tile_g]) - 1
    ce = cnt_end[tile_g]
    c_lo = jnp.sum(ce <= k0[:, None], axis=1).astype(i32)
    c_hi = jnp.sum(ce <= k1[:, None], axis=1).astype(i32)
    npairs = jnp.where(t < n_used, c_hi - c_lo + 1, 0)
    total = jnp.sum(npairs).astype(i32)
    lt = jnp.minimum(jnp.arange(MOE_PAIRS, dtype=i32), total - 1)

    def windows(count, win, n_steps):
        per_item = (count + win - 1) // win
        end = jnp.cumsum(per_item).astype(i32)
        start = end - per_item
        s = jnp.arange(n_steps, dtype=i32)
        real_s = s < end[-1]
        sc = jnp.minimum(s, end[-1] - 1)
        item = jnp.sum(end[None, :] <= sc[:, None], axis=1).astype(i32)
        j = sc - start[item]
        flags_s = jnp.where(real_s, FLAG_ACTIVE + jnp.where(j == 0, FLAG_FIRST, 0)
                            + jnp.where(j == per_item[item] - 1, FLAG_LAST, 0), 0).astype(i32)
        return item, j, flags_s, real_s, s - end[-1], end[-1]

    tile_s, j, flags, real_s, spare, steps_t = windows(npairs, EXP_WIN, MOE_TSTEPS)
    steps_t = steps_t + (MOE_TILES - n_used)
    c0 = c_lo[tile_s] + EXP_WIN * j
    n_valid = jnp.minimum(EXP_WIN, c_hi[tile_s] - c0 + 1).astype(i32)
    spare_tile = jnp.minimum(n_used + spare, MOE_TILES - 1)
    flags = jnp.where(real_s, flags, jnp.where(spare_tile >= n_used, FLAG_ZERO, 0)).astype(i32)
    tile_sched = jnp.where(real_s, tile_s, spare_tile).astype(i32)
    by_tile = (tile_sched, c0.astype(i32), n_valid, flags, tile_g[tile_sched])
    cc = jnp.arange(MOE_CHUNKS, dtype=i32)
    is_pair = (cc[:, None] >= c_lo[None, :]) & (cc[:, None] <= c_hi[None, :]) & (t[None, :] < n_used)
    seen = jnp.cumsum(is_pair.reshape(-1).astype(i32))
    flat = jnp.sum(seen[None, :] <= lt[:, None], axis=1).astype(i32)
    pair_tile = flat % MOE_TILES
    per_chunk = jnp.sum(is_pair, axis=1).astype(i32)
    first_pair = jnp.cumsum(per_chunk).astype(i32) - per_chunk
    chunk_s, j, flags_c, _, _, steps_c = windows(per_chunk, CMB_WIN, MOE_CSTEPS)
    base = first_pair[chunk_s] + CMB_WIN * j
    n_valid_c = jnp.minimum(CMB_WIN, per_chunk[chunk_s] - CMB_WIN * j).astype(i32)
    tiles_c = tuple(pair_tile[jnp.minimum(base + w, total - 1)] for w in range(CMB_WIN))
    by_chunk = (chunk_s, n_valid_c, flags_c) + tiles_c
    return pos.reshape(MOE_CHUNKS, 1, MOE_CHUNK), (steps_t, by_tile), (steps_c, by_chunk)


def _one_hot_rows(pos_row, tile):
    rows = tile * MOE_TM + lax.broadcasted_iota(jnp.int32, (MOE_TM, MOE_CHUNK), 0)
    return jnp.where(pos_row == rows, 1.0, 0.0).astype(BF16)


def _moe_kernel(tile_ref, c0_ref, nv_ref, flag_ref, grp_ref, *refs):
    pos_refs, hx_refs = refs[0:EXP_WIN], refs[EXP_WIN:2 * EXP_WIN]
    wg_ref, wu_ref, wd_ref, y_ref, acc_ref = refs[2 * EXP_WIN:]
    l = pl.program_id(0)
    flags = flag_ref[l]

    @pl.when((flags & FLAG_FIRST) != 0)
    def _():
        acc_ref[...] = jnp.zeros_like(acc_ref)

    for w in range(EXP_WIN):
        @pl.when(((flags & FLAG_ACTIVE) != 0) & (w < nv_ref[l]))
        def _():
            onehot = _one_hot_rows(pos_refs[w][0], tile_ref[l])
            acc_ref[...] += jnp.dot(onehot, hx_refs[w][0], preferred_element_type=F32)

    @pl.when((flags & FLAG_LAST) != 0)
    def _():
        x = acc_ref[:, 0:D_MODEL].astype(BF16)
        terms = acc_ref[:, D_MODEL:HX_W]
        y = jnp.zeros((MOE_TM, D_MODEL), F32)
        for r in range(EXPERTS_PER_GROUP):
            lanes = [n * EXPERTS_PER_GROUP + r for n in range(HX_TERMS)]
            w_r = functools.reduce(jnp.add, [terms[:, c:c + 1] for c in lanes])
            gate = jnp.dot(x, wg_ref[0, 0, r], preferred_element_type=F32)
            up = jnp.dot(x, wu_ref[0, 0, r], preferred_element_type=F32)
            hid = (_silu(gate) * up * w_r).astype(BF16)
            y += jnp.dot(hid, wd_ref[0, 0, r], preferred_element_type=F32)
        y_ref[...] = y

    @pl.when((flags & FLAG_ZERO) != 0)
    def _():
        y_ref[...] = jnp.zeros_like(y_ref)


def moe_experts(hx, pos, schedule, w_gate, w_up, w_down, layer):
    n_steps, by_tile = schedule
    grouped = lambda w: w.reshape(DEPTH, N_EXPERT_GROUPS, EXPERTS_PER_GROUP, *w.shape[2:])
    wspec = lambda k, n: pl.BlockSpec((1, 1, EXPERTS_PER_GROUP, k, n), lambda l, t, c, n_, f, g: (layer, g[l], 0, 0, 0))
    chunk = lambda w: (lambda l, t, c, n_, f, g: (jnp.minimum(c[l] + w, MOE_CHUNKS - 1), 0, 0))
    return pl.pallas_call(
        _moe_kernel,
        grid_spec=pltpu.PrefetchScalarGridSpec(
            num_scalar_prefetch=5,
            grid=(n_steps,),
            in_specs=[pl.BlockSpec((1, 1, MOE_CHUNK), chunk(w)) for w in range(EXP_WIN)]
            + [pl.BlockSpec((1, MOE_CHUNK, HX_W), chunk(w)) for w in range(EXP_WIN)]
            + [wspec(D_MODEL, EXPERT_HIDDEN), wspec(D_MODEL, EXPERT_HIDDEN), wspec(EXPERT_HIDDEN, D_MODEL)],
            out_specs=pl.BlockSpec((MOE_TM, D_MODEL), lambda l, t, c, n_, f, g: (t[l], 0)),
            scratch_shapes=[pltpu.VMEM((MOE_TM, HX_W), F32)],
        ),
        out_shape=jax.ShapeDtypeStruct((MOE_SORTED, D_MODEL), F32),
        compiler_params=_cparams(("arbitrary",), vmem_mib=56),
        name="moe_experts",
    )(*by_tile, *([pos] * EXP_WIN), *([hx] * EXP_WIN), grouped(w_gate), grouped(w_up), grouped(w_down))


def _moe_combine_kernel(chunk_ref, nv_ref, flag_ref, *refs):
    tile_refs = refs[0:CMB_WIN]
    pos_ref = refs[CMB_WIN]
    y_refs = refs[CMB_WIN + 1:2 * CMB_WIN + 1]
    x_ref, gate_ref, o_ref, acc_ref = refs[2 * CMB_WIN + 1:]
    l = pl.program_id(0)
    flags = flag_ref[l]

    @pl.when((flags & FLAG_FIRST) != 0)
    def _():
        acc_ref[...] = jnp.zeros_like(acc_ref)

    for w in range(CMB_WIN):
        @pl.when(((flags & FLAG_ACTIVE) != 0) & (w < nv_ref[l]))
        def _():
            onehot = _one_hot_rows(pos_ref[0], tile_refs[w][l])
            y = y_refs[w][...]
            y_hi = y.astype(BF16)
            y_lo = (y - y_hi.astype(F32)).astype(BF16)
            acc_ref[...] += (lax.dot_general(onehot, y_hi, TN_DIMS, preferred_element_type=F32)
                             + lax.dot_general(onehot, y_lo, TN_DIMS, preferred_element_type=F32))

    @pl.when((flags & FLAG_LAST) != 0)
    def _():
        o_ref[0] = x_ref[0] + gate_ref[0] * acc_ref[...]


def moe_combine(x, y_sorted, pos, schedule, gate):
    n_steps, by_chunk = schedule
    per_b = SEQ // MOE_CHUNK
    tok = lambda l, c, *_: (c[l] // per_b, c[l] % per_b, 0)
    tile = lambda w: (lambda l, c, n_, f, *tiles: (tiles[w][l], 0))
    return pl.pallas_call(
        _moe_combine_kernel,
        grid_spec=pltpu.PrefetchScalarGridSpec(
            num_scalar_prefetch=3 + CMB_WIN,
            grid=(n_steps,),
            in_specs=[pl.BlockSpec((1, 1, MOE_CHUNK), lambda l, c, *_: (c[l], 0, 0))]
            + [pl.BlockSpec((MOE_TM, D_MODEL), tile(w)) for w in range(CMB_WIN)]
            + [pl.BlockSpec((1, MOE_CHUNK, D_MODEL), tok),
               pl.BlockSpec((1, 1, D_MODEL), lambda l, c, *_: (c[l] // per_b, 0, 0))],
            out_specs=pl.BlockSpec((1, MOE_CHUNK, D_MODEL), tok),
            scratch_shapes=[pltpu.VMEM((MOE_CHUNK, D_MODEL), F32)],
        ),
        out_shape=jax.ShapeDtypeStruct((BATCH, SEQ, D_MODEL), F32),
        compiler_params=_cparams(("arbitrary",)),
        name="moe_combine",
    )(*by_chunk, pos, *([y_sorted] * CMB_WIN), x, gate.reshape(BATCH, 1, D_MODEL))


def moe_layer(x, hx, route, gate, w_gate, w_up, w_down, layer):
    pos, by_tile, by_chunk = moe_plan(route)
    y_sorted = moe_experts(hx, pos, by_tile, w_gate, w_up, w_down, layer)
    return moe_combine(x, y_sorted, pos, by_chunk, gate)


def kernel(x, c, positions, w_ada, b_ada, norm_g, w_in_ab, w_out_ab, gla_w_gate2, gla_b_gate, gla_norm_g, nsa_q_gain, nsa_k_gain, nsa_cmp_pe, nsa_cmp_w1, nsa_cmp_w2, w_in_c, sgu_norm_g, sgu_w_s, sgu_b_s, w_out_c, w_router, router_bias, w_gate, w_up, w_down):
    mod = ada_modulation(c, w_ada, b_ada)
    qk, gv, gr, nq, nkv, misc, gl_t = inproj0(x, norm_g[0, 0], mod[0, :, 0], mod[0, :, 1], _arrange_w_in(w_in_ab[0]))
    o_a = gla_mixer(qk, gv, gr, misc, gla_w_gate2[0], gla_b_gate[0], gla_norm_g[0])
    o_b = nsa_mixer(nq, nkv, gl_t, positions, nsa_q_gain[0], nsa_k_gain[0], nsa_cmp_pe[0], nsa_cmp_w1[0], nsa_cmp_w2[0])
    wg, wu, wd = w_gate.astype(BF16), w_up.astype(BF16), w_down.astype(BF16)
    route_args = lambda l: _route_args(norm_g[l, 1], mod[l, :, 3], mod[l, :, 4], w_router, router_bias)
    x1, h, route = outproj0(o_a, o_b, w_out_ab[0], x, mod[0, :, 2], route_args(0))
    x2 = moe_layer(x1, h, route, mod[0, :, 5], wg, wu, wd, 0)
    x3, h, route = gmlp_layer(x2, norm_g[1, 0], mod[1, :, 0], mod[1, :, 1], w_in_c[0], sgu_norm_g[0], sgu_w_s[0],
                              sgu_b_s[0], w_out_c[0], mod[1, :, 2], route_args(1))
    return moe_layer(x3, h, route, mod[1, :, 5], wg, wu, wd, 1)
```
